```python
import math
import jax
import jax.numpy as jnp
from jax import lax
import numpy as np

D_MODEL = 1024
BATCH = 16
SEQ = 256
DEPTH = 2
DEC_BATCH = 4
DEC_SEQ = 4096
PAST_LEN = 512

GRID_W = 64
N_MIXERS = 2
N_CONV = (DEPTH + 1) // 2
N_MLA = DEPTH // 2

N_HEADS = 8
QK_NOPE = 128
QK_ROPE = 64
V_DIM = 128
Q_LORA = 384
KV_LORA = 256
ROPE_THETA = 10000.0
Q_BLOCK = 128
ATTN_SCALE = (QK_NOPE + QK_ROPE) ** -0.5

CONV_W = 3

N_EXPERTS = 64
TOP_K = 8
N_GROUPS = 8
TOPK_GROUPS = 4
D_EXPERT = 256
D_SHARED = 256
ROUTED_SCALE = 2.5
EXPERT_BLOCK = 128

ALPHA = (2 * DEPTH) ** 0.25
BETA = (8 * DEPTH) ** -0.25
LN_EPS = 1e-5
RMS_EPS = 1e-6

kernel_name = "hybrid_conv_mla_moe_diffusion_step"


def layer_norm(x, g, b):
    xf = x.astype(jnp.float32)
    mu = jnp.mean(xf, axis=-1, keepdims=True)
    var = jnp.mean(jnp.square(xf - mu), axis=-1, keepdims=True)
    y = (xf - mu) * lax.rsqrt(var + LN_EPS) * g.astype(jnp.float32) + b.astype(jnp.float32)
    return y.astype(x.dtype)


def rms_norm(x, g):
    xf = x.astype(jnp.float32)
    y = xf * lax.rsqrt(jnp.mean(jnp.square(xf), axis=-1, keepdims=True) + RMS_EPS) * g.astype(jnp.float32)
    return y.astype(x.dtype)


def adaln(cvec, w, b):
    m = jax.nn.silu(cvec) @ w + b
    return jnp.split(m, 6, axis=-1)


def short_conv(h, w_in, conv_k, w_out):
    b_gate, c_gate, xv = jnp.split(h @ w_in, 3, axis=-1)
    u = c_gate * xv
    u = lax.conv_general_dilated(u, conv_k[:, None, :].astype(u.dtype), window_strides=(1,),
                                 padding=((CONV_W // 2, CONV_W // 2),),
                                 dimension_numbers=("NWC", "WIO", "NWC"),
                                 feature_group_count=u.shape[-1])
    return (b_gate * u) @ w_out


def grid_angles(n):
    rows = n // GRID_W
    row = jnp.repeat(jnp.arange(rows), GRID_W).astype(jnp.float32)
    col = jnp.tile(jnp.arange(GRID_W), rows).astype(jnp.float32)
    nf = QK_ROPE // 4
    inv = ROPE_THETA ** (-jnp.arange(nf, dtype=jnp.float32) / nf)
    return row[:, None] * inv, col[:, None] * inv


def rope_axis(x, ang):
    nf = ang.shape[-1]
    cos = jnp.cos(ang).astype(x.dtype)
    sin = jnp.sin(ang).astype(x.dtype)
    x1, x2 = x[..., :nf], x[..., nf:]
    return jnp.concatenate([x1 * cos - x2 * sin, x2 * cos + x1 * sin], axis=-1)


def axial_rope(x, ang_r, ang_c):
    a = x.shape[-1] // 2
    return jnp.concatenate([rope_axis(x[..., :a], ang_r), rope_axis(x[..., a:], ang_c)], axis=-1)


def mla_project(h, w_dqkv, q_norm, w_uq, kv_norm):
    bsz, t, _ = h.shape
    d = h @ w_dqkv
    cq = rms_norm(d[..., :Q_LORA], q_norm)
    ckv = rms_norm(d[..., Q_LORA:Q_LORA + KV_LORA], kv_norm)
    krope = d[..., Q_LORA + KV_LORA:]
    q = (cq @ w_uq).reshape(bsz, t, N_HEADS, QK_NOPE + QK_ROPE)
    return q[..., :QK_NOPE], q[..., QK_NOPE:], ckv, krope


def mla_expand(ckv, w_ukv):
    bsz, t, _ = ckv.shape
    kv = (ckv @ w_ukv).reshape(bsz, t, N_HEADS, QK_NOPE + V_DIM)
    return kv[..., :QK_NOPE], kv[..., QK_NOPE:]


def mla_attend(qn, qr, kn, kr, v):
    bsz, t, h, _ = qn.shape
    nb = t // Q_BLOCK

    def to_blocks(a):
        return a.reshape(bsz, nb, Q_BLOCK, *a.shape[2:]).swapaxes(0, 1)

    def block(args):
        qn_b, qr_b = args
        s = jnp.einsum("bqhd,bkhd->bhqk", qn_b, kn) + jnp.einsum("bqhr,bkr->bhqk", qr_b, kr)
        p = jax.nn.softmax(s.astype(jnp.float32) * ATTN_SCALE, axis=-1).astype(v.dtype)
        return jnp.einsum("bhqk,bkhd->bqhd", p, v)

    o = lax.map(block, (to_blocks(qn), to_blocks(qr)))
    return o.swapaxes(0, 1).reshape(bsz, t, h * V_DIM)


def mla_context(h, w_dqkv, q_norm, w_uq, kv_norm, w_ukv, w_o):
    qn, qr, ckv, kr = mla_project(h, w_dqkv, q_norm, w_uq, kv_norm)
    kn, v = mla_expand(ckv, w_ukv)
    y = mla_attend(qn, qr, kn, kr, v) @ w_o
    return y, ckv, kr


def mla_latent(h, ctx_ckv, ctx_krope, w_dqkv, q_norm, w_uq, kv_norm, w_ukv, w_o):
    qn, qr, ckv, kr = mla_project(h, w_dqkv, q_norm, w_uq, kv_norm)
    ang_r, ang_c = grid_angles(h.shape[1])
    qr = axial_rope(qr, ang_r[:, None, :], ang_c[:, None, :])
    kr = axial_rope(kr, ang_r, ang_c)
    kn, v = mla_expand(ckv, w_ukv)
    kn_c, v_c = mla_expand(ctx_ckv.astype(h.dtype), w_ukv)
    kn_all = jnp.concatenate([kn, kn_c], axis=1)
    kr_all = jnp.concatenate([kr, ctx_krope.astype(h.dtype)], axis=1)
    v_all = jnp.concatenate([v, v_c], axis=1)
    return mla_attend(qn, qr, kn_all, kr_all, v_all) @ w_o


def swiglu(x, wg, wu, wd):
    return (jax.nn.silu(x @ wg) * (x @ wu)) @ wd


def routed_experts(xt, idx, w, wg, wu, wd):
    t, d = xt.shape
    tk = t * TOP_K
    flat_e = idx.reshape(-1)
    flat_tok = jnp.arange(tk, dtype=jnp.int32) // TOP_K
    flat_w = w.reshape(-1)
    order = jnp.argsort(flat_e, stable=True)
    se = flat_e[order]
    counts = jax.ops.segment_sum(jnp.ones((tk,), jnp.int32), flat_e, num_segments=N_EXPERTS)
    start = jnp.cumsum(counts) - counts
    pcounts = (counts + EXPERT_BLOCK - 1) // EXPERT_BLOCK * EXPERT_BLOCK
    pend = jnp.cumsum(pcounts)
    pstart = pend - pcounts
    dest = pstart[se] + jnp.arange(tk, dtype=jnp.int32) - start[se]
    n_blocks = -(-tk // EXPERT_BLOCK) + N_EXPERTS
    n_slots = n_blocks * EXPERT_BLOCK
    slot_tok = jnp.full((n_slots,), t, jnp.int32).at[dest].set(flat_tok[order])
    slot_w = jnp.zeros((n_slots,), xt.dtype).at[dest].set(flat_w[order])
    block_e = jnp.minimum(jnp.searchsorted(pend, jnp.arange(n_blocks, dtype=jnp.int32) * EXPERT_BLOCK,
                                           side="right"), N_EXPERTS - 1)
    xpad = jnp.concatenate([xt, jnp.zeros((1, d), xt.dtype)], axis=0)

    def run_block(args):
        toks, e, wb = args
        xb = xpad[toks]
        return swiglu(xb, wg[e], wu[e], wd[e]) * wb[:, None]

    out = lax.map(run_block, (slot_tok.reshape(n_blocks, EXPERT_BLOCK), block_e,
                              slot_w.reshape(n_blocks, EXPERT_BLOCK)))
    return jnp.zeros((t + 1, d), xt.dtype).at[slot_tok].add(out.reshape(n_slots, d))[:t]


def moe(x, router_w, router_bias, wg, wu, wd, sg, su, sd):
    shape = x.shape
    xt = x.reshape(-1, shape[-1])
    t = xt.shape[0]
    scores = jax.nn.sigmoid(xt.astype(jnp.float32) @ router_w.astype(jnp.float32))
    biased = scores + router_bias.astype(jnp.float32)
    grp = biased.reshape(t, N_GROUPS, N_EXPERTS // N_GROUPS)
    gscore = lax.top_k(grp, 2)[0].sum(axis=-1)
    _, gidx = lax.top_k(gscore, TOPK_GROUPS)
    gmask = jax.nn.one_hot(gidx, N_GROUPS, dtype=jnp.float32).sum(axis=1) > 0
    emask = jnp.repeat(gmask, N_EXPERTS // N_GROUPS, axis=1)
    _, idx = lax.top_k(jnp.where(emask, biased, -jnp.inf), TOP_K)
    w = jnp.take_along_axis(scores, idx, axis=-1)
    w = w / jnp.sum(w, axis=-1, keepdims=True) * ROUTED_SCALE
    routed = routed_experts(xt, idx.astype(jnp.int32), w.astype(xt.dtype), wg, wu, wd)
    return (swiglu(xt, sg, su, sd) + routed).reshape(shape)


def setup_inputs(seed: int = 0) -> dict:
    key = jax.random.key(seed)
    ks = iter(jax.random.split(key, 40))
    f32 = jnp.float32

    def nrm(shape, scale):
        return jax.random.normal(next(ks), shape, f32) * scale

    D = D_MODEL
    return {
        "x_prompt": nrm((BATCH, SEQ, D), 1.0),
        "x_sample": nrm((DEC_BATCH, DEC_SEQ, D), 1.0),
        "cache_ckv": nrm((DEC_BATCH, N_MLA, PAST_LEN, KV_LORA), 1.0),
        "cache_krope": nrm((DEC_BATCH, N_MLA, PAST_LEN, QK_ROPE), 1.0),
        "c": nrm((DEC_BATCH, D), 1.0),
        "c_ctx": nrm((D,), 1.0),
        "ada_w": nrm((DEPTH, D, 6 * D), D ** -0.5),
        "ada_b": nrm((DEPTH, 6 * D), 0.02),
        "ln_g": 1.0 + nrm((DEPTH, 2, D), 0.02),
        "ln_b": nrm((DEPTH, 2, D), 0.02),
        "conv_w_in": nrm((N_CONV, D, 3 * D), D ** -0.5),
        "conv_k": nrm((N_CONV, CONV_W, D), CONV_W ** -0.5),
        "conv_w_out": nrm((N_CONV, D, D), BETA * D ** -0.5),
        "mla_w_dqkv": nrm((N_MLA, D, Q_LORA + KV_LORA + QK_ROPE), D ** -0.5),
        "mla_q_norm": 1.0 + nrm((N_MLA, Q_LORA), 0.02),
        "mla_w_uq": nrm((N_MLA, Q_LORA, N_HEADS * (QK_NOPE + QK_ROPE)), Q_LORA ** -0.5),
        "mla_kv_norm": 1.0 + nrm((N_MLA, KV_LORA), 0.02),
        "mla_w_ukv": nrm((N_MLA, KV_LORA, N_HEADS * (QK_NOPE + V_DIM)), KV_LORA ** -0.5),
        "mla_w_o": nrm((N_MLA, N_HEADS * V_DIM, D), BETA * (N_HEADS * V_DIM) ** -0.5),
        "router_w": nrm((DEPTH, D, N_EXPERTS), D ** -0.5),
        "router_bias": nrm((DEPTH, N_EXPERTS), 0.01),
        "exp_w_gate": nrm((DEPTH, N_EXPERTS, D, D_EXPERT), D ** -0.5),
        "exp_w_up": nrm((DEPTH, N_EXPERTS, D, D_EXPERT), D ** -0.5),
        "exp_w_down": nrm((DEPTH, N_EXPERTS, D_EXPERT, D), BETA * D_EXPERT ** -0.5),
        "sh_w_gate": nrm((DEPTH, D, D_SHARED), D ** -0.5),
        "sh_w_up": nrm((DEPTH, D, D_SHARED), D ** -0.5),
        "sh_w_down": nrm((DEPTH, D_SHARED, D), BETA * D_SHARED ** -0.5),
    }


def reference(x_prompt, x_sample, cache_ckv, cache_krope, c, c_ctx, ada_w, ada_b, ln_g, ln_b,
              conv_w_in, conv_k, conv_w_out, mla_w_dqkv, mla_q_norm, mla_w_uq, mla_kv_norm,
              mla_w_ukv, mla_w_o, router_w, router_bias, exp_w_gate, exp_w_up, exp_w_down,
              sh_w_gate, sh_w_up, sh_w_down):
    xc = x_prompt
    xl = x_sample
    new_ckv = []
    new_krope = []
    for l in range(DEPTH):
        sm_c, cm_c, gm_c, sf_c, cf_c, gf_c = adaln(c_ctx, ada_w[l], ada_b[l])
        sm_l, cm_l, gm_l, sf_l, cf_l, gf_l = [t[:, None, :] for t in adaln(c, ada_w[l], ada_b[l])]
        hc = xc * (1 + cm_c) + sm_c
        hl = xl * (1 + cm_l) + sm_l
        j = l // N_MIXERS
        if l % N_MIXERS == 0:
            yc = short_conv(hc, conv_w_in[j], conv_k[j], conv_w_out[j])
            yl = short_conv(hl, conv_w_in[j], conv_k[j], conv_w_out[j])
        else:
            yc, ckv, kr = mla_context(hc, mla_w_dqkv[j], mla_q_norm[j], mla_w_uq[j], mla_kv_norm[j],
                                      mla_w_ukv[j], mla_w_o[j])
            new_ckv.append(ckv)
            new_krope.append(kr)
            yl = mla_latent(hl, cache_ckv[:, j], cache_krope[:, j], mla_w_dqkv[j], mla_q_norm[j],
                            mla_w_uq[j], mla_kv_norm[j], mla_w_ukv[j], mla_w_o[j])
        xc = layer_norm(ALPHA * xc + gm_c * yc, ln_g[l, 0], ln_b[l, 0])
        xl = layer_norm(ALPHA * xl + gm_l * yl, ln_g[l, 0], ln_b[l, 0])
        hc = xc * (1 + cf_c) + sf_c
        hl = xl * (1 + cf_l) + sf_l
        yc = moe(hc, router_w[l], router_bias[l], exp_w_gate[l], exp_w_up[l], exp_w_down[l],
                 sh_w_gate[l], sh_w_up[l], sh_w_down[l])
        yl = moe(hl, router_w[l], router_bias[l], exp_w_gate[l], exp_w_up[l], exp_w_down[l],
                 sh_w_gate[l], sh_w_up[l], sh_w_down[l])
        xc = layer_norm(ALPHA * xc + gf_c * yc, ln_g[l, 1], ln_b[l, 1])
        xl = layer_norm(ALPHA * xl + gf_l * yl, ln_g[l, 1], ln_b[l, 1])
    state_ckv = jnp.stack(new_ckv, axis=1)
    state_krope = jnp.stack(new_krope, axis=1)
    return (xc, xl, state_ckv, state_krope)
```

```python
import functools
import math

import jax
import jax.numpy as jnp
from jax import lax
from jax.experimental import pallas as pl
from jax.experimental.pallas import tpu as pltpu

D_MODEL = 1024
BATCH = 16
SEQ = 256
DEPTH = 2
DEC_BATCH = 4
DEC_SEQ = 4096
PAST_LEN = 512
GRID_W = 64

N_HEADS = 8
QK_NOPE = 128
QK_ROPE = 64
V_DIM = 128
Q_LORA = 384
KV_LORA = 256
ROPE_THETA = 10000.0
ATTN_SCALE = (QK_NOPE + QK_ROPE) ** -0.5
HEAD_PAD = 256

N_EXPERTS = 64
TOP_K = 8
N_GROUPS = 8
TOPK_GROUPS = 4
GROUP_SIZE = N_EXPERTS // N_GROUPS
D_EXPERT = 256
D_SHARED = 256
ROUTED_SCALE = 2.5

ALPHA = (2 * DEPTH) ** 0.25
LN_EPS = 1e-5
RMS_EPS = 1e-6

GROUP_ROWS = 4096
N_GROUPS_ROWS = 1 + DEC_BATCH
T_CTX = BATCH * SEQ
T_LAT = DEC_BATCH * DEC_SEQ
T_ALL = T_CTX + T_LAT
LANES = 128
SUBLANES = 8

F32 = jnp.float32
BF16 = jnp.bfloat16
NEG_INF = float("-inf")


def _dot(a, b):
    return jnp.dot(a, b, preferred_element_type=F32)


def _dot_nt(a, b, precision=None):
    return lax.dot_general(a, b, (((1,), (1,)), ((), ())), precision=precision,
                           preferred_element_type=F32)


def _layer_norm(v, g, b):
    mu = jnp.mean(v, axis=-1, keepdims=True)
    d = v - mu
    var = jnp.mean(d * d, axis=-1, keepdims=True)
    return d * lax.rsqrt(var + LN_EPS) * g + b


def _rms_norm(v, g):
    return v * lax.rsqrt(jnp.mean(v * v, axis=-1, keepdims=True) + RMS_EPS) * g


def _silu(v):
    return v / (1.0 + jnp.exp(-v))


def _sigmoid(v):
    return 1.0 / (1.0 + jnp.exp(-v))


def _adaln_kernel(c_ref, w_ref, b_ref, o_ref):
    c = c_ref[...]
    s = _silu(c)
    o_ref[0, 0] = jnp.dot(s, w_ref[0], precision=lax.Precision.HIGHEST,
                          preferred_element_type=F32) + b_ref[0]


def _adaln(cvecs, ada_w, ada_b):
    out = pl.pallas_call(
        _adaln_kernel,
        grid=(DEPTH, 6),
        in_specs=[
            pl.BlockSpec((SUBLANES, D_MODEL), lambda l, j: (0, 0)),
            pl.BlockSpec((1, D_MODEL, D_MODEL), lambda l, j: (l, 0, j)),
            pl.BlockSpec((1, 1, D_MODEL), lambda l, j: (l, 0, j)),
        ],
        out_specs=pl.BlockSpec((1, 1, SUBLANES, D_MODEL), lambda l, j: (l, j, 0, 0)),
        out_shape=jax.ShapeDtypeStruct((DEPTH, 6, SUBLANES, D_MODEL), F32),
        compiler_params=pltpu.CompilerParams(dimension_semantics=("parallel", "parallel")),
        name="adaln",
    )(cvecs, ada_w, ada_b.reshape(DEPTH, 1, 6 * D_MODEL))
    return jnp.transpose(out[:, :, :N_GROUPS_ROWS, :], (0, 2, 1, 3))


CONV_TM = 512


def _conv_kernel(x_ref, xp_ref, xn_ref, mods_ref, win_ref, ck_ref, wout_ref, g_ref, b_ref, o_ref):
    i = pl.program_id(0)
    tm = x_ref.shape[0]
    m = mods_ref[0]
    sm, cm, gm = m[0:1], m[1:2], m[2:3]
    x = x_ref[...]
    h = (x * (1.0 + cm) + sm).astype(BF16)
    z = _dot(h, win_ref[...])
    bg = z[:, :D_MODEL]
    u = z[:, D_MODEL:2 * D_MODEL] * z[:, 2 * D_MODEL:]

    def halo_u(xh_ref):
        hh = (xh_ref[...] * (1.0 + cm) + sm).astype(BF16)
        zh = _dot(hh, win_ref[:, D_MODEL:])
        return zh[:, :D_MODEL] * zh[:, D_MODEL:]

    u_prev = halo_u(xp_ref)[SUBLANES - 1:SUBLANES]
    u_next = halo_u(xn_ref)[0:1]

    row = lax.broadcasted_iota(jnp.int32, (tm, 1), 0)
    grow = i * tm + row
    seq_len = jnp.where(grow < T_CTX, SEQ, DEC_SEQ)
    pos = jnp.bitwise_and(grow, seq_len - 1)
    is_first = pos == 0
    is_last = pos == seq_len - 1

    left = pltpu.roll(u, 1, 0)
    left = jnp.where(row == 0, u_prev, left)
    left = jnp.where(is_first, 0.0, left)
    right = pltpu.roll(u, tm - 1, 0)
    right = jnp.where(row == tm - 1, u_next, right)
    right = jnp.where(is_last, 0.0, right)

    ck = ck_ref[...]
    conv = left * ck[0:1] + u * ck[1:2] + right * ck[2:3]
    v = (bg * conv).astype(BF16)
    y = _dot(v, wout_ref[...])
    o_ref[...] = _layer_norm(ALPHA * x + gm * y, g_ref[...], b_ref[...])


def _conv_mixer(x, mods, w_in, conv_k, w_out, ln_g, ln_b):
    t = x.shape[0]
    tm = CONV_TM
    nblk8 = t // SUBLANES
    per8 = tm // SUBLANES
    return pl.pallas_call(
        _conv_kernel,
        grid=(t // tm,),
        in_specs=[
            pl.BlockSpec((tm, D_MODEL), lambda i: (i, 0)),
            pl.BlockSpec((SUBLANES, D_MODEL), lambda i: (jnp.maximum(i * per8 - 1, 0), 0)),
            pl.BlockSpec((SUBLANES, D_MODEL), lambda i: (jnp.minimum((i + 1) * per8, nblk8 - 1), 0)),
            pl.BlockSpec((1, 6, D_MODEL), lambda i: (i * tm // GROUP_ROWS, 0, 0)),
            pl.BlockSpec((D_MODEL, 3 * D_MODEL), lambda i: (0, 0)),
            pl.BlockSpec((3, D_MODEL), lambda i: (0, 0)),
            pl.BlockSpec((D_MODEL, D_MODEL), lambda i: (0, 0)),
            pl.BlockSpec((1, D_MODEL), lambda i: (0, 0)),
            pl.BlockSpec((1, D_MODEL), lambda i: (0, 0)),
        ],
        out_specs=pl.BlockSpec((tm, D_MODEL), lambda i: (i, 0)),
        out_shape=jax.ShapeDtypeStruct((t, D_MODEL), F32),
        compiler_params=pltpu.CompilerParams(dimension_semantics=("parallel",)),
        name="conv_mixer",
    )(x, x, x, mods, w_in, conv_k, w_out, ln_g, ln_b)


ROUTER_TM = 512


def _first_argmax_mask(cur, ridx, n):
    mx = jnp.max(cur, axis=0, keepdims=True)
    first = jnp.min(jnp.where(cur == mx, ridx, n), axis=0, keepdims=True)
    return ridx == first, mx


def _router_kernel(x_ref, mods_ref, rwt_ref, bias_ref, gate_ref):
    tm = x_ref.shape[0]
    m = mods_ref[0]
    sf, cf = m[3:4], m[4:5]
    hc = x_ref[...] * (1.0 + cf) + sf
    logits = _dot_nt(rwt_ref[...], hc, precision=lax.Precision.HIGHEST)
    scores = _sigmoid(logits)
    biased = scores + bias_ref[...]

    ridx8 = lax.broadcasted_iota(jnp.int32, (GROUP_SIZE, tm), 0)
    gscore = jnp.full((N_GROUPS, tm), NEG_INF, F32)
    for g in range(N_GROUPS):
        blk = biased[g * GROUP_SIZE:(g + 1) * GROUP_SIZE]
        sel, m1 = _first_argmax_mask(blk, ridx8, GROUP_SIZE)
        m2 = jnp.max(jnp.where(sel, NEG_INF, blk), axis=0, keepdims=True)
        gscore = jnp.where(ridx8 == g, m1 + m2, gscore)

    gmask = jnp.zeros((N_GROUPS, tm), jnp.bool_)
    cur = gscore
    for _ in range(TOPK_GROUPS):
        sel, _unused = _first_argmax_mask(cur, ridx8, N_GROUPS)
        gmask = jnp.logical_or(gmask, sel)
        cur = jnp.where(sel, NEG_INF, cur)

    gmask_f = gmask.astype(F32)
    blocks = []
    for g in range(N_GROUPS):
        keep = jnp.broadcast_to(gmask_f[g:g + 1], (GROUP_SIZE, tm)) > 0.5
        blocks.append(jnp.where(keep, biased[g * GROUP_SIZE:(g + 1) * GROUP_SIZE], NEG_INF))
    cur = jnp.concatenate(blocks, axis=0)

    ridx = lax.broadcasted_iota(jnp.int32, (N_EXPERTS, tm), 0)
    chosen = jnp.zeros((N_EXPERTS, tm), jnp.bool_)
    for _ in range(TOP_K):
        sel, _unused = _first_argmax_mask(cur, ridx, N_EXPERTS)
        chosen = jnp.logical_or(chosen, sel)
        cur = jnp.where(sel, NEG_INF, cur)

    w = jnp.where(chosen, scores, 0.0)
    w = w / jnp.sum(w, axis=0, keepdims=True) * ROUTED_SCALE
    wpad = jnp.concatenate([w, jnp.zeros((LANES - N_EXPERTS, tm), F32)], axis=0)
    gate_ref[...] = wpad.T


def _router(x, mods, router_wt, router_bias):
    t = x.shape[0]
    tm = ROUTER_TM
    return pl.pallas_call(
        _router_kernel,
        grid=(t // tm,),
        in_specs=[
            pl.BlockSpec((tm, D_MODEL), lambda i: (i, 0)),
            pl.BlockSpec((1, 6, D_MODEL), lambda i: (i * tm // GROUP_ROWS, 0, 0)),
            pl.BlockSpec((N_EXPERTS, D_MODEL), lambda i: (0, 0)),
            pl.BlockSpec((N_EXPERTS, 1), lambda i: (0, 0)),
        ],
        out_specs=pl.BlockSpec((tm, LANES), lambda i: (i, 0)),
        out_shape=jax.ShapeDtypeStruct((t, LANES), F32),
        compiler_params=pltpu.CompilerParams(dimension_semantics=("parallel",)),
        name="moe_router",
    )(x, mods, router_wt, router_bias)


MOE_TM = 1024


def _moe_kernel(x_ref, mods_ref, gate_ref, sg_ref, su_ref, sd_ref, wg_ref, wu_ref, wd_ref,
                g_ref, b_ref, o_ref, hc_ref, acc_ref):
    e = pl.program_id(1)
    tm = x_ref.shape[0]
    m = mods_ref[0]

    @pl.when(e == 0)
    def _():
        sf, cf = m[3:4], m[4:5]
        hc = (x_ref[...] * (1.0 + cf) + sf).astype(BF16)
        hc_ref[...] = hc
        a = _silu(_dot(hc, sg_ref[...])) * _dot(hc, su_ref[...])
        acc_ref[...] = _dot(a.astype(BF16), sd_ref[...])

    hc = hc_ref[...]
    g = _dot(hc, wg_ref[0, 0].astype(BF16))
    u = _dot(hc, wu_ref[0, 0].astype(BF16))
    lane = lax.broadcasted_iota(jnp.int32, (tm, LANES), 1)
    gate = jnp.sum(jnp.where(lane == e, gate_ref[...], 0.0), axis=1, keepdims=True)
    a = (_silu(g) * u * gate).astype(BF16)
    acc_ref[...] += _dot(a, wd_ref[0, 0].astype(BF16))

    @pl.when(e == N_EXPERTS - 1)
    def _():
        gf = m[5:6]
        o_ref[...] = _layer_norm(ALPHA * x_ref[...] + gf * acc_ref[...], g_ref[...], b_ref[...])


def _moe(x, mods, gates, sg, su, sd, wg, wu, wd, ln_g, ln_b, layer):
    t = x.shape[0]
    tm = MOE_TM
    return pl.pallas_call(
        _moe_kernel,
        grid=(t // tm, N_EXPERTS),
        in_specs=[
            pl.BlockSpec((tm, D_MODEL), lambda i, e: (i, 0)),
            pl.BlockSpec((1, 6, D_MODEL), lambda i, e: (i * tm // GROUP_ROWS, 0, 0)),
            pl.BlockSpec((tm, LANES), lambda i, e: (i, 0)),
            pl.BlockSpec((D_MODEL, D_SHARED), lambda i, e: (0, 0)),
            pl.BlockSpec((D_MODEL, D_SHARED), lambda i, e: (0, 0)),
            pl.BlockSpec((D_SHARED, D_MODEL), lambda i, e: (0, 0)),
            pl.BlockSpec((1, 1, D_MODEL, D_EXPERT), lambda i, e: (layer, e, 0, 0)),
            pl.BlockSpec((1, 1, D_MODEL, D_EXPERT), lambda i, e: (layer, e, 0, 0)),
            pl.BlockSpec((1, 1, D_EXPERT, D_MODEL), lambda i, e: (layer, e, 0, 0)),
            pl.BlockSpec((1, D_MODEL), lambda i, e: (0, 0)),
            pl.BlockSpec((1, D_MODEL), lambda i, e: (0, 0)),
        ],
        out_specs=pl.BlockSpec((tm, D_MODEL), lambda i, e: (i, 0)),
        out_shape=jax.ShapeDtypeStruct((t, D_MODEL), F32),
        scratch_shapes=[pltpu.VMEM((tm, D_MODEL), BF16), pltpu.VMEM((tm, D_MODEL), F32)],
        compiler_params=pltpu.CompilerParams(dimension_semantics=("parallel", "arbitrary")),
        name="moe_experts",
    )(x, mods, gates, sg, su, sd, wg, wu, wd, ln_g, ln_b)


MLA_TM = 512


def _mla_proj_kernel(x_ref, mods_ref, wdq_ref, wdkv_ref, wkr_ref, qn_ref, kvn_ref, wuq_ref,
                     wukn_ref, wuv_ref, ta_ref, tb_ref, q_ref, k_ref, v_ref, ckv_ref, kr_ref):
    m = mods_ref[0]
    sm, cm = m[0:1], m[1:2]
    h = (x_ref[...] * (1.0 + cm) + sm).astype(BF16)
    cq = _rms_norm(_dot(h, wdq_ref[...]), qn_ref[...])
    ckv = _rms_norm(_dot(h, wdkv_ref[...]), kvn_ref[...])
    kr2 = _dot(h, wkr_ref[...])
    ckv_ref[...] = ckv
    kr_ref[...] = kr2

    ta = ta_ref[0]
    tb = tb_ref[0]
    ka = ta[:, QK_NOPE:]
    kb = tb[:, QK_NOPE:]
    krr = kr2 * ka + pltpu.roll(kr2, QK_ROPE, 1) * kb

    qpre = _dot(cq.astype(BF16), wuq_ref[...])
    ckv_b = ckv.astype(BF16)
    kn = _dot(ckv_b, wukn_ref[...])
    v_ref[...] = _dot(ckv_b, wuv_ref[...]).astype(BF16)
    for hd in range(N_HEADS):
        qh = qpre[:, hd * HEAD_PAD:(hd + 1) * HEAD_PAD]
        qrot = qh * ta + pltpu.roll(qh, HEAD_PAD - QK_ROPE, 1) * tb
        q_ref[:, hd * HEAD_PAD:(hd + 1) * HEAD_PAD] = qrot.astype(BF16)
        k_ref[:, hd * HEAD_PAD:hd * HEAD_PAD + QK_NOPE] = kn[:, hd * QK_NOPE:(hd + 1) * QK_NOPE].astype(BF16)
        k_ref[:, hd * HEAD_PAD + QK_NOPE:(hd + 1) * HEAD_PAD] = krr.astype(BF16)


def _mla_proj(x, mods, p, rope_a, rope_b):
    t = x.shape[0]
    tm = MLA_TM
    full = lambda shape: pl.BlockSpec(shape, lambda i: (0,) * len(shape))
    rope_spec = pl.BlockSpec(
        (1, tm, HEAD_PAD),
        lambda i: (jnp.minimum(i * tm // GROUP_ROWS, 1), (i * tm % GROUP_ROWS) // tm, 0))
    return pl.pallas_call(
        _mla_proj_kernel,
        grid=(t // tm,),
        in_specs=[
            pl.BlockSpec((tm, D_MODEL), lambda i: (i, 0)),
            pl.BlockSpec((1, 6, D_MODEL), lambda i: (i * tm // GROUP_ROWS, 0, 0)),
            full((D_MODEL, Q_LORA)), full((D_MODEL, KV_LORA)), full((D_MODEL, 2 * QK_ROPE)),
            full((1, Q_LORA)), full((1, KV_LORA)),
            full((Q_LORA, N_HEADS * HEAD_PAD)),
            full((KV_LORA, N_HEADS * QK_NOPE)), full((KV_LORA, N_HEADS * V_DIM)),
            rope_spec, rope_spec,
        ],
        out_specs=[
            pl.BlockSpec((tm, N_HEADS * HEAD_PAD), lambda i: (i, 0)),
            pl.BlockSpec((tm, N_HEADS * HEAD_PAD), lambda i: (i, 0)),
            pl.BlockSpec((tm, N_HEADS * V_DIM), lambda i: (i, 0)),
            pl.BlockSpec((tm, KV_LORA), lambda i: (i, 0)),
            pl.BlockSpec((tm, 2 * QK_ROPE), lambda i: (i, 0)),
        ],
        out_shape=[
            jax.ShapeDtypeStruct((t, N_HEADS * HEAD_PAD), BF16),
            jax.ShapeDtypeStruct((t, N_HEADS * HEAD_PAD), BF16),
            jax.ShapeDtypeStruct((t, N_HEADS * V_DIM), BF16),
            jax.ShapeDtypeStruct((t, KV_LORA), F32),
            jax.ShapeDtypeStruct((t, 2 * QK_ROPE), F32),
        ],
        compiler_params=pltpu.CompilerParams(dimension_semantics=("parallel",)),
        name="mla_proj",
    )(x, mods, p["w_dq"], p["w_dkv"], p["w_kr"], p["q_norm"], p["kv_norm"], p["w_uq"],
      p["w_ukn"], p["w_uv"], rope_a, rope_b)


def _cache_kv_kernel(ckv_ref, kr_ref, wukn_ref, wuv_ref, k_ref, v_ref):
    ckv_b = ckv_ref[...].astype(BF16)
    kn = _dot(ckv_b, wukn_ref[...])
    v_ref[...] = _dot(ckv_b, wuv_ref[...]).astype(BF16)
    kr = kr_ref[...].astype(BF16)
    for hd in range(N_HEADS):
        k_ref[:, hd * HEAD_PAD:hd * HEAD_PAD + QK_NOPE] = kn[:, hd * QK_NOPE:(hd + 1) * QK_NOPE].astype(BF16)
        k_ref[:, hd * HEAD_PAD + QK_NOPE:(hd + 1) * HEAD_PAD] = kr


def _cache_kv(ckv, kr_pad, p):
    t = ckv.shape[0]
    tm = PAST_LEN
    full = lambda shape: pl.BlockSpec(shape, lambda i: (0,) * len(shape))
    return pl.pallas_call(
        _cache_kv_kernel,
        grid=(t // tm,),
        in_specs=[
            pl.BlockSpec((tm, KV_LORA), lambda i: (i, 0)),
            pl.BlockSpec((tm, 2 * QK_ROPE), lambda i: (i, 0)),
            full((KV_LORA, N_HEADS * QK_NOPE)), full((KV_LORA, N_HEADS * V_DIM)),
        ],
        out_specs=[
            pl.BlockSpec((tm, N_HEADS * HEAD_PAD), lambda i: (i, 0)),
            pl.BlockSpec((tm, N_HEADS * V_DIM), lambda i: (i, 0)),
        ],
        out_shape=[
            jax.ShapeDtypeStruct((t, N_HEADS * HEAD_PAD), BF16),
            jax.ShapeDtypeStruct((t, N_HEADS * V_DIM), BF16),
        ],
        compiler_params=pltpu.CompilerParams(dimension_semantics=("parallel",)),
        name="mla_cache_kv",
    )(ckv, kr_pad, p["w_ukn"], p["w_uv"])


def _ctx_attn_kernel(q_ref, k_ref, v_ref, o_ref):
    for hd in range(N_HEADS):
        q = q_ref[:, hd * HEAD_PAD:(hd + 1) * HEAD_PAD]
        k = k_ref[:, hd * HEAD_PAD:(hd + 1) * HEAD_PAD]
        s = _dot_nt(q, k) * ATTN_SCALE
        s = s - jnp.max(s, axis=-1, keepdims=True)
        p = jnp.exp(s)
        p = p / jnp.sum(p, axis=-1, keepdims=True)
        o = _dot(p.astype(BF16), v_ref[:, hd * V_DIM:(hd + 1) * V_DIM])
        o_ref[:, hd * V_DIM:(hd + 1) * V_DIM] = o.astype(BF16)


def _ctx_attention(q, k, v):
    return pl.pallas_call(
        _ctx_attn_kernel,
        grid=(BATCH,),
        in_specs=[
            pl.BlockSpec((SEQ, N_HEADS * HEAD_PAD), lambda b: (b, 0)),
            pl.BlockSpec((SEQ, N_HEADS * HEAD_PAD), lambda b: (b, 0)),
            pl.BlockSpec((SEQ, N_HEADS * V_DIM), lambda b: (b, 0)),
        ],
        out_specs=pl.BlockSpec((SEQ, N_HEADS * V_DIM), lambda b: (b, 0)),
        out_shape=jax.ShapeDtypeStruct((T_CTX, N_HEADS * V_DIM), BF16),
        compiler_params=pltpu.CompilerParams(dimension_semantics=("parallel",)),
        name="ctx_attention",
    )(q, k, v)


LAT_TQ = 512
LAT_TK = 512
LAT_KSTEPS = DEC_SEQ // LAT_TK + PAST_LEN // LAT_TK


def _lat_attn_kernel(q_ref, kl_ref, vl_ref, kc_ref, vc_ref, o_ref, m_ref, l_ref, acc_ref):
    kj = pl.program_id(3)
    n_lat = DEC_SEQ // LAT_TK

    @pl.when(kj == 0)
    def _():
        m_ref[...] = jnp.full(m_ref.shape, NEG_INF, F32)
        l_ref[...] = jnp.zeros(l_ref.shape, F32)
        acc_ref[...] = jnp.zeros(acc_ref.shape, F32)

    def step(k_ref, v_ref):
        s = _dot_nt(q_ref[...], k_ref[...]) * ATTN_SCALE
        m_prev = m_ref[...]
        m_new = jnp.maximum(m_prev, jnp.max(s, axis=-1, keepdims=True))
        alpha = jnp.exp(m_prev - m_new)
        p = jnp.exp(s - m_new)
        l_ref[...] = alpha * l_ref[...] + jnp.sum(p, axis=-1, keepdims=True)
        acc_ref[...] = alpha * acc_ref[...] + _dot(p.astype(BF16), v_ref[...])
        m_ref[...] = m_new

    @pl.when(kj < n_lat)
    def _():
        step(kl_ref, vl_ref)

    @pl.when(kj >= n_lat)
    def _():
        step(kc_ref, vc_ref)

    @pl.when(kj == LAT_KSTEPS - 1)
    def _():
        o_ref[...] = (acc_ref[...] / l_ref[...]).astype(BF16)


def _lat_attention(q, k, v, kc, vc):
    nq = DEC_SEQ // LAT_TQ
    nk = DEC_SEQ // LAT_TK
    return pl.pallas_call(
        _lat_attn_kernel,
        grid=(DEC_BATCH, N_HEADS, nq, LAT_KSTEPS),
        in_specs=[
            pl.BlockSpec((LAT_TQ, HEAD_PAD), lambda b, h, i, j: (b * nq + i, h)),
            pl.BlockSpec((LAT_TK, HEAD_PAD), lambda b, h, i, j: (b * nk + jnp.minimum(j, nk - 1), h)),
            pl.BlockSpec((LAT_TK, V_DIM), lambda b, h, i, j: (b * nk + jnp.minimum(j, nk - 1), h)),
            pl.BlockSpec((PAST_LEN, HEAD_PAD), lambda b, h, i, j: (b, h)),
            pl.BlockSpec((PAST_LEN, V_DIM), lambda b, h, i, j: (b, h)),
        ],
        out_specs=pl.BlockSpec((LAT_TQ, V_DIM), lambda b, h, i, j: (b * nq + i, h)),
        out_shape=jax.ShapeDtypeStruct((T_LAT, N_HEADS * V_DIM), BF16),
        scratch_shapes=[pltpu.VMEM((LAT_TQ, 1), F32), pltpu.VMEM((LAT_TQ, 1), F32),
                        pltpu.VMEM((LAT_TQ, V_DIM), F32)],
        compiler_params=pltpu.CompilerParams(
            dimension_semantics=("parallel", "parallel", "parallel", "arbitrary")),
        name="lat_attention",
    )(q, k, v, kc, vc)


OPROJ_TM = 512


def _oproj_kernel(o_ref, x_ref, mods_ref, wo_ref, g_ref, b_ref, out_ref):
    gm = mods_ref[0][2:3]
    y = _dot(o_ref[...], wo_ref[...])
    out_ref[...] = _layer_norm(ALPHA * x_ref[...] + gm * y, g_ref[...], b_ref[...])


def _oproj(o, x, mods, w_o, ln_g, ln_b):
    t = x.shape[0]
    tm = OPROJ_TM
    return pl.pallas_call(
        _oproj_kernel,
        grid=(t // tm,),
        in_specs=[
            pl.BlockSpec((tm, N_HEADS * V_DIM), lambda i: (i, 0)),
            pl.BlockSpec((tm, D_MODEL), lambda i: (i, 0)),
            pl.BlockSpec((1, 6, D_MODEL), lambda i: (i * tm // GROUP_ROWS, 0, 0)),
            pl.BlockSpec((N_HEADS * V_DIM, D_MODEL), lambda i: (0, 0)),
            pl.BlockSpec((1, D_MODEL), lambda i: (0, 0)),
            pl.BlockSpec((1, D_MODEL), lambda i: (0, 0)),
        ],
        out_specs=pl.BlockSpec((tm, D_MODEL), lambda i: (i, 0)),
        out_shape=jax.ShapeDtypeStruct((t, D_MODEL), F32),
        compiler_params=pltpu.CompilerParams(dimension_semantics=("parallel",)),
        name="attn_oproj",
    )(o, x, mods, w_o, ln_g, ln_b)


def _swap16(w):
    q = QK_ROPE // 4
    return jnp.concatenate([w[..., q:2 * q], w[..., :q], w[..., 3 * q:], w[..., 2 * q:3 * q]], axis=-1)


def _mla_params(w_dqkv, q_norm, w_uq, kv_norm, w_ukv):
    w_kr = w_dqkv[:, Q_LORA + KV_LORA:]
    wq = w_uq.reshape(Q_LORA, N_HEADS, QK_NOPE + QK_ROPE)
    wq_r = wq[..., QK_NOPE:]
    wq = jnp.concatenate([wq[..., :QK_NOPE], wq_r, _swap16(wq_r)], axis=-1)
    wkv = w_ukv.reshape(KV_LORA, N_HEADS, QK_NOPE + V_DIM)
    return {
        "w_dq": w_dqkv[:, :Q_LORA].astype(BF16),
        "w_dkv": w_dqkv[:, Q_LORA:Q_LORA + KV_LORA].astype(BF16),
        "w_kr": jnp.concatenate([w_kr, _swap16(w_kr)], axis=-1).astype(BF16),
        "q_norm": q_norm.reshape(1, Q_LORA),
        "kv_norm": kv_norm.reshape(1, KV_LORA),
        "w_uq": wq.reshape(Q_LORA, N_HEADS * HEAD_PAD).astype(BF16),
        "w_ukn": wkv[..., :QK_NOPE].reshape(KV_LORA, N_HEADS * QK_NOPE).astype(BF16),
        "w_uv": wkv[..., QK_NOPE:].reshape(KV_LORA, N_HEADS * V_DIM).astype(BF16),
    }


def _rope_tables():
    nf = QK_ROPE // 4
    t = jnp.arange(DEC_SEQ)
    row = (t // GRID_W).astype(F32)
    col = (t % GRID_W).astype(F32)
    inv = ROPE_THETA ** (-jnp.arange(nf, dtype=F32) / nf)
    ar, ac = row[:, None] * inv, col[:, None] * inv
    cos = jnp.concatenate([jnp.cos(ar), jnp.cos(ar), jnp.cos(ac), jnp.cos(ac)], axis=-1)
    sin = jnp.concatenate([-jnp.sin(ar), jnp.sin(ar), -jnp.sin(ac), jnp.sin(ac)], axis=-1)
    ones = jnp.ones((DEC_SEQ, QK_NOPE), F32)
    zeros_n = jnp.zeros((DEC_SEQ, QK_NOPE), F32)
    zeros_p = jnp.zeros((DEC_SEQ, HEAD_PAD - QK_NOPE - QK_ROPE), F32)
    a_rot = jnp.concatenate([ones, cos, zeros_p], axis=-1)
    b_rot = jnp.concatenate([zeros_n, sin, zeros_p], axis=-1)
    a_id = jnp.concatenate([ones, jnp.ones((DEC_SEQ, QK_ROPE), F32), zeros_p], axis=-1)
    b_id = jnp.zeros((DEC_SEQ, HEAD_PAD), F32)
    return jnp.stack([a_id, a_rot]), jnp.stack([b_id, b_rot])


def kernel(x_prompt, x_sample, cache_ckv, cache_krope, c, c_ctx, ada_w, ada_b, ln_g, ln_b, conv_w_in, conv_k, conv_w_out, mla_w_dqkv, mla_q_norm, mla_w_uq, mla_kv_norm, mla_w_ukv, mla_w_o, router_w, router_bias, exp_w_gate, exp_w_up, exp_w_down, sh_w_gate, sh_w_up, sh_w_down):
    x = jnp.concatenate([x_prompt.reshape(T_CTX, D_MODEL), x_sample.reshape(T_LAT, D_MODEL)], axis=0)
    cvecs = jnp.concatenate([c_ctx[None, :], c, jnp.zeros((SUBLANES - N_GROUPS_ROWS, D_MODEL), F32)], axis=0)
    mods = _adaln(cvecs, ada_w, ada_b)

    def ln(l, k):
        return ln_g[l, k].reshape(1, D_MODEL), ln_b[l, k].reshape(1, D_MODEL)

    def moe_layer(xin, l):
        gates = _router(xin, mods[l], router_w[l].T, router_bias[l].reshape(N_EXPERTS, 1))
        return _moe(xin, mods[l], gates, sh_w_gate[l].astype(BF16), sh_w_up[l].astype(BF16),
                    sh_w_down[l].astype(BF16), exp_w_gate, exp_w_up, exp_w_down, *ln(l, 1), layer=l)

    x = _conv_mixer(x, mods[0], conv_w_in[0].astype(BF16), conv_k[0], conv_w_out[0].astype(BF16), *ln(0, 0))
    x = moe_layer(x, 0)

    p = _mla_params(mla_w_dqkv[0], mla_q_norm[0], mla_w_uq[0], mla_kv_norm[0], mla_w_ukv[0])
    rope_a, rope_b = _rope_tables()
    q, k, v, ckv, kr = _mla_proj(x, mods[1], p, rope_a, rope_b)
    kr_cache = jnp.concatenate([cache_krope[:, 0].reshape(DEC_BATCH * PAST_LEN, QK_ROPE),
                                jnp.zeros((DEC_BATCH * PAST_LEN, QK_ROPE), F32)], axis=-1)
    kc, vc = _cache_kv(cache_ckv[:, 0].reshape(DEC_BATCH * PAST_LEN, KV_LORA), kr_cache, p)
    o_ctx = _ctx_attention(q[:T_CTX], k[:T_CTX], v[:T_CTX])
    o_lat = _lat_attention(q[T_CTX:], k[T_CTX:], v[T_CTX:], kc, vc)
    o = jnp.concatenate([o_ctx, o_lat], axis=0)
    x = _oproj(o, x, mods[1], mla_w_o[0].astype(BF16), *ln(1, 0))
    x = moe_layer(x, 1)

    y_prompt = x[:T_CTX].reshape(BATCH, SEQ, D_MODEL)
    y_sample = x[T_CTX:].reshape(DEC_BATCH, DEC_SEQ, D_MODEL)
    state_ckv = ckv[:T_CTX].reshape(BATCH, 1, SEQ, KV_LORA)
    state_krope = kr[:T_CTX, :QK_ROPE].reshape(BATCH, 1, SEQ, QK_ROPE)
    return (y_prompt, y_sample, state_ckv, state_krope)
```

```python
import functools
import math

import jax
import jax.numpy as jnp
import numpy as np
from jax import lax
from jax.experimental import pallas as pl
from jax.experimental.pallas import tpu as pltpu

D_MODEL = 1024
BATCH = 16
SEQ = 256
DEPTH = 2
DEC_BATCH = 4
DEC_SEQ = 4096
PAST_LEN = 512
GRID_W = 64

N_HEADS = 8
QK_NOPE = 128
QK_ROPE = 64
V_DIM = 128
Q_LORA = 384
KV_LORA = 256
ROPE_THETA = 10000.0
ATTN_SCALE = (QK_NOPE + QK_ROPE) ** -0.5
HEAD_PAD = 256
Q_PRESCALE = ATTN_SCALE * math.log2(math.e)

N_EXPERTS = 64
TOP_K = 8
N_GROUPS = 8
TOPK_GROUPS = 4
GROUP_SIZE = N_EXPERTS // N_GROUPS
D_EXPERT = 256
D_SHARED = 256
ROUTED_SCALE = 2.5

ALPHA = (2 * DEPTH) ** 0.25
LN_EPS = 1e-5
RMS_EPS = 1e-6

GROUP_ROWS = 4096
N_GROUPS_ROWS = 1 + DEC_BATCH
T_CTX = BATCH * SEQ
T_LAT = DEC_BATCH * DEC_SEQ
T_ALL = T_CTX + T_LAT
LANES = 128
SUBLANES = 8

F32 = jnp.float32
BF16 = jnp.bfloat16
NEG_INF = float("-inf")


def _dot(a, b):
    return jnp.dot(a, b, preferred_element_type=F32)


def _dot_nt(a, b, precision=None):
    return lax.dot_general(a, b, (((1,), (1,)), ((), ())), precision=precision,
                           preferred_element_type=F32)


def _layer_norm(v, g, b):
    mu = jnp.mean(v, axis=-1, keepdims=True)
    d = v - mu
    var = jnp.mean(d * d, axis=-1, keepdims=True)
    return d * lax.rsqrt(var + LN_EPS) * g + b


def _rms_norm(v, g):
    return v * lax.rsqrt(jnp.mean(v * v, axis=-1, keepdims=True) + RMS_EPS) * g


def _silu(v):
    return v / (1.0 + jnp.exp(-v))


def _sigmoid(v):
    return 1.0 / (1.0 + jnp.exp(-v))


def _adaln_kernel(c_ref, w_ref, b_ref, o_ref):
    c = c_ref[...]
    s = _silu(c)
    o_ref[0, 0] = jnp.dot(s, w_ref[0], precision=lax.Precision.HIGHEST,
                          preferred_element_type=F32) + b_ref[0]


def _adaln(cvecs, ada_w, ada_b):
    out = pl.pallas_call(
        _adaln_kernel,
        grid=(DEPTH, 6),
        in_specs=[
            pl.BlockSpec((SUBLANES, D_MODEL), lambda l, j: (0, 0)),
            pl.BlockSpec((1, D_MODEL, D_MODEL), lambda l, j: (l, 0, j)),
            pl.BlockSpec((1, 1, D_MODEL), lambda l, j: (l, 0, j)),
        ],
        out_specs=pl.BlockSpec((1, 1, SUBLANES, D_MODEL), lambda l, j: (l, j, 0, 0)),
        out_shape=jax.ShapeDtypeStruct((DEPTH, 6, SUBLANES, D_MODEL), F32),
        compiler_params=pltpu.CompilerParams(dimension_semantics=("parallel", "parallel")),
        name="adaln",
    )(cvecs, ada_w, ada_b.reshape(DEPTH, 1, 6 * D_MODEL))
    return jnp.transpose(out[:, :, :N_GROUPS_ROWS, :], (0, 2, 1, 3))


CONV_TM = 512


def _conv_kernel(x_ref, xp_ref, xn_ref, mods_ref, win_ref, ck_ref, wout_ref, g_ref, b_ref, o_ref):
    i = pl.program_id(0)
    tm = x_ref.shape[0]
    m = mods_ref[0]
    sm, cm, gm = m[0:1], m[1:2], m[2:3]
    x = x_ref[...]
    h = (x * (1.0 + cm) + sm).astype(BF16)
    z = _dot(h, win_ref[...])
    bg = z[:, :D_MODEL]
    u = z[:, D_MODEL:2 * D_MODEL] * z[:, 2 * D_MODEL:]

    def halo_u(xh_ref):
        hh = (xh_ref[...] * (1.0 + cm) + sm).astype(BF16)
        zh = _dot(hh, win_ref[:, D_MODEL:])
        return zh[:, :D_MODEL] * zh[:, D_MODEL:]

    u_prev = halo_u(xp_ref)[SUBLANES - 1:SUBLANES]
    u_next = halo_u(xn_ref)[0:1]

    row = lax.broadcasted_iota(jnp.int32, (tm, 1), 0)
    grow = i * tm + row
    seq_len = jnp.where(grow < T_CTX, SEQ, DEC_SEQ)
    pos = jnp.bitwise_and(grow, seq_len - 1)
    is_first = pos == 0
    is_last = pos == seq_len - 1

    left = pltpu.roll(u, 1, 0)
    left = jnp.where(row == 0, u_prev, left)
    left = jnp.where(is_first, 0.0, left)
    right = pltpu.roll(u, tm - 1, 0)
    right = jnp.where(row == tm - 1, u_next, right)
    right = jnp.where(is_last, 0.0, right)

    ck = ck_ref[...]
    conv = left * ck[0:1] + u * ck[1:2] + right * ck[2:3]
    v = (bg * conv).astype(BF16)
    y = _dot(v, wout_ref[...])
    o_ref[...] = _layer_norm(ALPHA * x + gm * y, g_ref[...], b_ref[...])


def _conv_mixer(x, mods, w_in, conv_k, w_out, ln_g, ln_b):
    t = x.shape[0]
    tm = CONV_TM
    nblk8 = t // SUBLANES
    per8 = tm // SUBLANES
    return pl.pallas_call(
        _conv_kernel,
        grid=(t // tm,),
        in_specs=[
            pl.BlockSpec((tm, D_MODEL), lambda i: (i, 0)),
            pl.BlockSpec((SUBLANES, D_MODEL), lambda i: (jnp.maximum(i * per8 - 1, 0), 0)),
            pl.BlockSpec((SUBLANES, D_MODEL), lambda i: (jnp.minimum((i + 1) * per8, nblk8 - 1), 0)),
            pl.BlockSpec((1, 6, D_MODEL), lambda i: (i * tm // GROUP_ROWS, 0, 0)),
            pl.BlockSpec((D_MODEL, 3 * D_MODEL), lambda i: (0, 0)),
            pl.BlockSpec((3, D_MODEL), lambda i: (0, 0)),
            pl.BlockSpec((D_MODEL, D_MODEL), lambda i: (0, 0)),
            pl.BlockSpec((1, D_MODEL), lambda i: (0, 0)),
            pl.BlockSpec((1, D_MODEL), lambda i: (0, 0)),
        ],
        out_specs=pl.BlockSpec((tm, D_MODEL), lambda i: (i, 0)),
        out_shape=jax.ShapeDtypeStruct((t, D_MODEL), F32),
        compiler_params=pltpu.CompilerParams(dimension_semantics=("parallel",)),
        name="conv_mixer",
    )(x, x, x, mods, w_in, conv_k, w_out, ln_g, ln_b)


ROUTER_TM = 512


def _first_argmax_mask(cur, ridx, n):
    mx = jnp.max(cur, axis=0, keepdims=True)
    first = jnp.min(jnp.where(cur == mx, ridx, n), axis=0, keepdims=True)
    return ridx == first, mx


def _router_kernel(x_ref, mods_ref, rwt_ref, bias_ref, gate_ref):
    tm = x_ref.shape[0]
    m = mods_ref[0]
    sf, cf = m[3:4], m[4:5]
    hc = x_ref[...] * (1.0 + cf) + sf
    logits = _dot_nt(rwt_ref[...], hc, precision=lax.Precision.HIGHEST)
    scores = _sigmoid(logits)
    biased = scores + bias_ref[...]

    ridx8 = lax.broadcasted_iota(jnp.int32, (GROUP_SIZE, tm), 0)
    gscore = jnp.full((N_GROUPS, tm), NEG_INF, F32)
    for g in range(N_GROUPS):
        blk = biased[g * GROUP_SIZE:(g + 1) * GROUP_SIZE]
        sel, m1 = _first_argmax_mask(blk, ridx8, GROUP_SIZE)
        m2 = jnp.max(jnp.where(sel, NEG_INF, blk), axis=0, keepdims=True)
        gscore = jnp.where(ridx8 == g, m1 + m2, gscore)

    gmask = jnp.zeros((N_GROUPS, tm), jnp.bool_)
    cur = gscore
    for _ in range(TOPK_GROUPS):
        sel, _unused = _first_argmax_mask(cur, ridx8, N_GROUPS)
        gmask = jnp.logical_or(gmask, sel)
        cur = jnp.where(sel, NEG_INF, cur)

    gmask_f = gmask.astype(F32)
    blocks = []
    for g in range(N_GROUPS):
        keep = jnp.broadcast_to(gmask_f[g:g + 1], (GROUP_SIZE, tm)) > 0.5
        blocks.append(jnp.where(keep, biased[g * GROUP_SIZE:(g + 1) * GROUP_SIZE], NEG_INF))
    cur = jnp.concatenate(blocks, axis=0)

    ridx = lax.broadcasted_iota(jnp.int32, (N_EXPERTS, tm), 0)
    chosen = jnp.zeros((N_EXPERTS, tm), jnp.bool_)
    for _ in range(TOP_K):
        sel, _unused = _first_argmax_mask(cur, ridx, N_EXPERTS)
        chosen = jnp.logical_or(chosen, sel)
        cur = jnp.where(sel, NEG_INF, cur)

    w = jnp.where(chosen, scores, 0.0)
    w = w / jnp.sum(w, axis=0, keepdims=True) * ROUTED_SCALE
    wpad = jnp.concatenate([w, jnp.zeros((LANES - N_EXPERTS, tm), F32)], axis=0)
    gate_ref[...] = wpad.T


def _router(x, mods, router_wt, router_bias):
    t = x.shape[0]
    tm = ROUTER_TM
    return pl.pallas_call(
        _router_kernel,
        grid=(t // tm,),
        in_specs=[
            pl.BlockSpec((tm, D_MODEL), lambda i: (i, 0)),
            pl.BlockSpec((1, 6, D_MODEL), lambda i: (i * tm // GROUP_ROWS, 0, 0)),
            pl.BlockSpec((N_EXPERTS, D_MODEL), lambda i: (0, 0)),
            pl.BlockSpec((N_EXPERTS, 1), lambda i: (0, 0)),
        ],
        out_specs=pl.BlockSpec((tm, LANES), lambda i: (i, 0)),
        out_shape=jax.ShapeDtypeStruct((t, LANES), F32),
        compiler_params=pltpu.CompilerParams(dimension_semantics=("parallel",)),
        name="moe_router",
    )(x, mods, router_wt, router_bias)


MOE_TM = 1024


def _moe_kernel(x_ref, mods_ref, gate_ref, sg_ref, su_ref, sd_ref, wg_ref, wu_ref, wd_ref,
                g_ref, b_ref, o_ref, hc_ref, acc_ref):
    e = pl.program_id(1)
    tm = x_ref.shape[0]
    m = mods_ref[0]

    @pl.when(e == 0)
    def _():
        sf, cf = m[3:4], m[4:5]
        hc = (x_ref[...] * (1.0 + cf) + sf).astype(BF16)
        hc_ref[...] = hc
        a = _silu(_dot(hc, sg_ref[...])) * _dot(hc, su_ref[...])
        acc_ref[...] = _dot(a.astype(BF16), sd_ref[...])

    hc = hc_ref[...]
    g = _dot(hc, wg_ref[0, 0].astype(BF16))
    u = _dot(hc, wu_ref[0, 0].astype(BF16))
    lane = lax.broadcasted_iota(jnp.int32, (tm, LANES), 1)
    gate = jnp.sum(jnp.where(lane == e, gate_ref[...], 0.0), axis=1, keepdims=True)
    a = (_silu(g) * u * gate).astype(BF16)
    acc_ref[...] += _dot(a, wd_ref[0, 0].astype(BF16))

    @pl.when(e == N_EXPERTS - 1)
    def _():
        gf = m[5:6]
        o_ref[...] = _layer_norm(ALPHA * x_ref[...] + gf * acc_ref[...], g_ref[...], b_ref[...])


def _moe(x, mods, gates, sg, su, sd, wg, wu, wd, ln_g, ln_b, layer):
    t = x.shape[0]
    tm = MOE_TM
    return pl.pallas_call(
        _moe_kernel,
        grid=(t // tm, N_EXPERTS),
        in_specs=[
            pl.BlockSpec((tm, D_MODEL), lambda i, e: (i, 0)),
            pl.BlockSpec((1, 6, D_MODEL), lambda i, e: (i * tm // GROUP_ROWS, 0, 0)),
            pl.BlockSpec((tm, LANES), lambda i, e: (i, 0)),
            pl.BlockSpec((D_MODEL, D_SHARED), lambda i, e: (0, 0)),
            pl.BlockSpec((D_MODEL, D_SHARED), lambda i, e: (0, 0)),
            pl.BlockSpec((D_SHARED, D_MODEL), lambda i, e: (0, 0)),
            pl.BlockSpec((1, 1, D_MODEL, D_EXPERT), lambda i, e: (layer, e, 0, 0)),
            pl.BlockSpec((1, 1, D_MODEL, D_EXPERT), lambda i, e: (layer, e, 0, 0)),
            pl.BlockSpec((1, 1, D_EXPERT, D_MODEL), lambda i, e: (layer, e, 0, 0)),
            pl.BlockSpec((1, D_MODEL), lambda i, e: (0, 0)),
            pl.BlockSpec((1, D_MODEL), lambda i, e: (0, 0)),
        ],
        out_specs=pl.BlockSpec((tm, D_MODEL), lambda i, e: (i, 0)),
        out_shape=jax.ShapeDtypeStruct((t, D_MODEL), F32),
        scratch_shapes=[pltpu.VMEM((tm, D_MODEL), BF16), pltpu.VMEM((tm, D_MODEL), F32)],
        compiler_params=pltpu.CompilerParams(dimension_semantics=("parallel", "arbitrary")),
        name="moe_experts",
    )(x, mods, gates, sg, su, sd, wg, wu, wd, ln_g, ln_b)


MLA_TM = 512


def _mla_proj_kernel(x_ref, mods_ref, wdq_ref, wdkv_ref, wkr_ref, qn_ref, kvn_ref, wuq_ref,
                     wukn_ref, wuv_ref, ta_ref, tb_ref, q_ref, k_ref, v_ref, ckv_ref, kr_ref):
    m = mods_ref[0]
    sm, cm = m[0:1], m[1:2]
    h = (x_ref[...] * (1.0 + cm) + sm).astype(BF16)
    cq = _rms_norm(_dot(h, wdq_ref[...]), qn_ref[...])
    ckv = _rms_norm(_dot(h, wdkv_ref[...]), kvn_ref[...])
    kr2 = _dot(h, wkr_ref[...])
    ckv_ref[...] = ckv
    kr_ref[...] = kr2

    ka = ta_ref[0]
    kb = tb_ref[0]
    tm = ka.shape[0]
    ta = jnp.concatenate([jnp.full((tm, QK_NOPE), Q_PRESCALE, F32), ka * Q_PRESCALE], axis=1)
    tb = jnp.concatenate([jnp.zeros((tm, QK_NOPE), F32), kb * Q_PRESCALE], axis=1)
    krr = kr2 * ka + pltpu.roll(kr2, QK_ROPE, 1) * kb

    qpre = _dot(cq.astype(BF16), wuq_ref[...])
    ckv_b = ckv.astype(BF16)
    kn = _dot(ckv_b, wukn_ref[...])
    v_ref[...] = _dot(ckv_b, wuv_ref[...]).astype(BF16)
    for hd in range(N_HEADS):
        qh = qpre[:, hd * HEAD_PAD:(hd + 1) * HEAD_PAD]
        qrot = qh * ta + pltpu.roll(qh, HEAD_PAD - QK_ROPE, 1) * tb
        q_ref[:, hd * HEAD_PAD:(hd + 1) * HEAD_PAD] = qrot.astype(BF16)
        k_ref[:, hd * HEAD_PAD:hd * HEAD_PAD + QK_NOPE] = kn[:, hd * QK_NOPE:(hd + 1) * QK_NOPE].astype(BF16)
        k_ref[:, hd * HEAD_PAD + QK_NOPE:(hd + 1) * HEAD_PAD] = krr.astype(BF16)


def _mla_proj(x, mods, p, rope_a, rope_b):
    t = x.shape[0]
    tm = MLA_TM
    full = lambda shape: pl.BlockSpec(shape, lambda i: (0,) * len(shape))
    rope_spec = pl.BlockSpec(
        (1, tm, 2 * QK_ROPE),
        lambda i: (jnp.minimum(i * tm // GROUP_ROWS, 1), (i * tm % GROUP_ROWS) // tm, 0))
    return pl.pallas_call(
        _mla_proj_kernel,
        grid=(t // tm,),
        in_specs=[
            pl.BlockSpec((tm, D_MODEL), lambda i: (i, 0)),
            pl.BlockSpec((1, 6, D_MODEL), lambda i: (i * tm // GROUP_ROWS, 0, 0)),
            full((D_MODEL, Q_LORA)), full((D_MODEL, KV_LORA)), full((D_MODEL, 2 * QK_ROPE)),
            full((1, Q_LORA)), full((1, KV_LORA)),
            full((Q_LORA, N_HEADS * HEAD_PAD)),
            full((KV_LORA, N_HEADS * QK_NOPE)), full((KV_LORA, N_HEADS * V_DIM)),
            rope_spec, rope_spec,
        ],
        out_specs=[
            pl.BlockSpec((tm, N_HEADS * HEAD_PAD), lambda i: (i, 0)),
            pl.BlockSpec((tm, N_HEADS * HEAD_PAD), lambda i: (i, 0)),
            pl.BlockSpec((tm, N_HEADS * V_DIM), lambda i: (i, 0)),
            pl.BlockSpec((tm, KV_LORA), lambda i: (i, 0)),
            pl.BlockSpec((tm, 2 * QK_ROPE), lambda i: (i, 0)),
        ],
        out_shape=[
            jax.ShapeDtypeStruct((t, N_HEADS * HEAD_PAD), BF16),
            jax.ShapeDtypeStruct((t, N_HEADS * HEAD_PAD), BF16),
            jax.ShapeDtypeStruct((t, N_HEADS * V_DIM), BF16),
            jax.ShapeDtypeStruct((t, KV_LORA), F32),
            jax.ShapeDtypeStruct((t, 2 * QK_ROPE), F32),
        ],
        compiler_params=pltpu.CompilerParams(dimension_semantics=("parallel",)),
        name="mla_proj",
    )(x, mods, p["w_dq"], p["w_dkv"], p["w_kr"], p["q_norm"], p["kv_norm"], p["w_uq"],
      p["w_ukn"], p["w_uv"], rope_a, rope_b)


def _cache_kv_kernel(ckv_ref, kr_ref, wukn_ref, wuv_ref, k_ref, v_ref):
    ckv_b = ckv_ref[...].astype(BF16)
    kn = _dot(ckv_b, wukn_ref[...])
    v_ref[...] = _dot(ckv_b, wuv_ref[...]).astype(BF16)
    kr = kr_ref[...].astype(BF16)
    for hd in range(N_HEADS):
        k_ref[:, hd * HEAD_PAD:hd * HEAD_PAD + QK_NOPE] = kn[:, hd * QK_NOPE:(hd + 1) * QK_NOPE].astype(BF16)
        k_ref[:, hd * HEAD_PAD + QK_NOPE:(hd + 1) * HEAD_PAD] = kr


def _cache_kv(ckv, kr_pad, p):
    t = ckv.shape[0]
    tm = PAST_LEN
    full = lambda shape: pl.BlockSpec(shape, lambda i: (0,) * len(shape))
    return pl.pallas_call(
        _cache_kv_kernel,
        grid=(t // tm,),
        in_specs=[
            pl.BlockSpec((tm, KV_LORA), lambda i: (i, 0)),
            pl.BlockSpec((tm, 2 * QK_ROPE), lambda i: (i, 0)),
            full((KV_LORA, N_HEADS * QK_NOPE)), full((KV_LORA, N_HEADS * V_DIM)),
        ],
        out_specs=[
            pl.BlockSpec((tm, N_HEADS * HEAD_PAD), lambda i: (i, 0)),
            pl.BlockSpec((tm, N_HEADS * V_DIM), lambda i: (i, 0)),
        ],
        out_shape=[
            jax.ShapeDtypeStruct((t, N_HEADS * HEAD_PAD), BF16),
            jax.ShapeDtypeStruct((t, N_HEADS * V_DIM), BF16),
        ],
        compiler_params=pltpu.CompilerParams(dimension_semantics=("parallel",)),
        name="mla_cache_kv",
    )(ckv, kr_pad, p["w_ukn"], p["w_uv"])


def _ctx_attn_kernel(q_ref, k_ref, v_ref, o_ref):
    for hd in range(N_HEADS):
        q = q_ref[:, hd * HEAD_PAD:(hd + 1) * HEAD_PAD]
        k = k_ref[:, hd * HEAD_PAD:(hd + 1) * HEAD_PAD]
        s = _dot_nt(q, k)
        s = s - jnp.max(s, axis=-1, keepdims=True)
        p = jnp.exp2(s)
        p = p / jnp.sum(p, axis=-1, keepdims=True)
        o = _dot(p.astype(BF16), v_ref[:, hd * V_DIM:(hd + 1) * V_DIM])
        o_ref[:, hd * V_DIM:(hd + 1) * V_DIM] = o.astype(BF16)


def _ctx_attention(q, k, v):
    return pl.pallas_call(
        _ctx_attn_kernel,
        grid=(BATCH,),
        in_specs=[
            pl.BlockSpec((SEQ, N_HEADS * HEAD_PAD), lambda b: (b, 0)),
            pl.BlockSpec((SEQ, N_HEADS * HEAD_PAD), lambda b: (b, 0)),
            pl.BlockSpec((SEQ, N_HEADS * V_DIM), lambda b: (b, 0)),
        ],
        out_specs=pl.BlockSpec((SEQ, N_HEADS * V_DIM), lambda b: (b, 0)),
        out_shape=jax.ShapeDtypeStruct((T_ALL, N_HEADS * V_DIM), BF16),
        compiler_params=pltpu.CompilerParams(dimension_semantics=("parallel",)),
        name="ctx_attention",
    )(q, k, v)


LAT_TQ = 512
LAT_TK = 512


def _lat_attn_kernel(q_ref, k_ref, v_ref, kc_ref, vc_ref, oin_ref, o_ref, s_ref, p_ref, m_ref):
    del oin_ref
    n_lat = DEC_SEQ // LAT_TK
    th = q_ref.shape[0] // 2
    halves = [pl.ds(0, th), pl.ds(th, th)]
    qs = [q_ref[r, :] for r in halves]

    mparts = [jnp.full((th, LANES), NEG_INF, F32) for _ in halves]
    for c in range(n_lat + 1):
        k = kc_ref[...] if c == n_lat else k_ref[c * LAT_TK:(c + 1) * LAT_TK, :]
        for hf, r in enumerate(halves):
            s = _dot_nt(qs[hf], k)
            s_ref[r, c * LAT_TK:(c + 1) * LAT_TK] = s
            for j in range(LAT_TK // LANES):
                mparts[hf] = jnp.maximum(mparts[hf], s[:, j * LANES:(j + 1) * LANES])
    for hf, r in enumerate(halves):
        m_ref[r, :] = jnp.broadcast_to(jnp.max(mparts[hf], axis=-1, keepdims=True), (th, LANES))

    lparts = [jnp.zeros((th, LANES), F32) for _ in halves]
    accs = [jnp.zeros((th, V_DIM), F32) for _ in halves]
    for c in range(n_lat + 1):
        v = vc_ref[...] if c == n_lat else v_ref[c * LAT_TK:(c + 1) * LAT_TK, :]
        for hf, r in enumerate(halves):
            for j in range(c * LAT_TK // LANES, (c + 1) * LAT_TK // LANES):
                p = jnp.exp2(s_ref[r, j * LANES:(j + 1) * LANES] - m_ref[r, :])
                lparts[hf] = lparts[hf] + p
                p_ref[r, j * LANES:(j + 1) * LANES] = p.astype(BF16)
            accs[hf] = accs[hf] + _dot(p_ref[r, c * LAT_TK:(c + 1) * LAT_TK], v)
    for hf, r in enumerate(halves):
        o_ref[r, :] = (accs[hf] / jnp.sum(lparts[hf], axis=-1, keepdims=True)).astype(BF16)


def _lat_attention(q, k, v, kc, vc, o_all):
    nq = DEC_SEQ // LAT_TQ
    return pl.pallas_call(
        _lat_attn_kernel,
        grid=(DEC_BATCH, N_HEADS, nq),
        in_specs=[
            pl.BlockSpec((LAT_TQ, HEAD_PAD), lambda b, h, i: ((b + 1) * nq + i, h)),
            pl.BlockSpec((DEC_SEQ, HEAD_PAD), lambda b, h, i: (b + 1, h)),
            pl.BlockSpec((DEC_SEQ, V_DIM), lambda b, h, i: (b + 1, h)),
            pl.BlockSpec((PAST_LEN, HEAD_PAD), lambda b, h, i: (b, h)),
            pl.BlockSpec((PAST_LEN, V_DIM), lambda b, h, i: (b, h)),
            pl.BlockSpec(memory_space=pl.ANY),
        ],
        out_specs=pl.BlockSpec((LAT_TQ, V_DIM), lambda b, h, i: ((b + 1) * nq + i, h)),
        out_shape=jax.ShapeDtypeStruct((T_ALL, N_HEADS * V_DIM), BF16),
        scratch_shapes=[pltpu.VMEM((LAT_TQ, DEC_SEQ + PAST_LEN), F32),
                        pltpu.VMEM((LAT_TQ, DEC_SEQ + PAST_LEN), BF16),
                        pltpu.VMEM((LAT_TQ, LANES), F32)],
        input_output_aliases={5: 0},
        compiler_params=pltpu.CompilerParams(
            dimension_semantics=("parallel", "parallel", "parallel")),
        name="lat_attention",
    )(q, k, v, kc, vc, o_all)


OPROJ_TM = 512


def _oproj_kernel(o_ref, x_ref, mods_ref, wo_ref, g_ref, b_ref, out_ref):
    gm = mods_ref[0][2:3]
    y = _dot(o_ref[...], wo_ref[...])
    out_ref[...] = _layer_norm(ALPHA * x_ref[...] + gm * y, g_ref[...], b_ref[...])


def _oproj(o, x, mods, w_o, ln_g, ln_b):
    t = x.shape[0]
    tm = OPROJ_TM
    return pl.pallas_call(
        _oproj_kernel,
        grid=(t // tm,),
        in_specs=[
            pl.BlockSpec((tm, N_HEADS * V_DIM), lambda i: (i, 0)),
            pl.BlockSpec((tm, D_MODEL), lambda i: (i, 0)),
            pl.BlockSpec((1, 6, D_MODEL), lambda i: (i * tm // GROUP_ROWS, 0, 0)),
            pl.BlockSpec((N_HEADS * V_DIM, D_MODEL), lambda i: (0, 0)),
            pl.BlockSpec((1, D_MODEL), lambda i: (0, 0)),
            pl.BlockSpec((1, D_MODEL), lambda i: (0, 0)),
        ],
        out_specs=pl.BlockSpec((tm, D_MODEL), lambda i: (i, 0)),
        out_shape=jax.ShapeDtypeStruct((t, D_MODEL), F32),
        compiler_params=pltpu.CompilerParams(dimension_semantics=("parallel",)),
        name="attn_oproj",
    )(o, x, mods, w_o, ln_g, ln_b)


def _swap16(w):
    q = QK_ROPE // 4
    return jnp.concatenate([w[..., q:2 * q], w[..., :q], w[..., 3 * q:], w[..., 2 * q:3 * q]], axis=-1)


def _mla_params(w_dqkv, q_norm, w_uq, kv_norm, w_ukv):
    w_kr = w_dqkv[:, Q_LORA + KV_LORA:]
    wq = w_uq.reshape(Q_LORA, N_HEADS, QK_NOPE + QK_ROPE)
    wq_r = wq[..., QK_NOPE:]
    wq = jnp.concatenate([wq[..., :QK_NOPE], wq_r, _swap16(wq_r)], axis=-1)
    wkv = w_ukv.reshape(KV_LORA, N_HEADS, QK_NOPE + V_DIM)
    return {
        "w_dq": w_dqkv[:, :Q_LORA].astype(BF16),
        "w_dkv": w_dqkv[:, Q_LORA:Q_LORA + KV_LORA].astype(BF16),
        "w_kr": jnp.concatenate([w_kr, _swap16(w_kr)], axis=-1).astype(BF16),
        "q_norm": q_norm.reshape(1, Q_LORA),
        "kv_norm": kv_norm.reshape(1, KV_LORA),
        "w_uq": wq.reshape(Q_LORA, N_HEADS * HEAD_PAD).astype(BF16),
        "w_ukn": wkv[..., :QK_NOPE].reshape(KV_LORA, N_HEADS * QK_NOPE).astype(BF16),
        "w_uv": wkv[..., QK_NOPE:].reshape(KV_LORA, N_HEADS * V_DIM).astype(BF16),
    }


def _rope_tables():
    nf = QK_ROPE // 4
    t = np.arange(DEC_SEQ)
    row = (t // GRID_W).astype(np.float32)
    col = (t % GRID_W).astype(np.float32)
    inv = (ROPE_THETA ** (-np.arange(nf, dtype=np.float32) / nf)).astype(np.float32)
    ar, ac = row[:, None] * inv, col[:, None] * inv
    pad = np.zeros((DEC_SEQ, QK_ROPE), np.float32)
    cos = np.concatenate([np.cos(ar), np.cos(ar), np.cos(ac), np.cos(ac), pad], axis=-1)
    sin = np.concatenate([-np.sin(ar), np.sin(ar), -np.sin(ac), np.sin(ac), pad], axis=-1)
    cos_id = np.concatenate([np.ones((DEC_SEQ, QK_ROPE), np.float32), pad], axis=-1)
    sin_id = np.zeros((DEC_SEQ, 2 * QK_ROPE), np.float32)
    return (jnp.asarray(np.stack([cos_id, cos]).astype(np.float32)),
            jnp.asarray(np.stack([sin_id, sin]).astype(np.float32)))


def kernel(x_prompt, x_sample, cache_ckv, cache_krope, c, c_ctx, ada_w, ada_b, ln_g, ln_b, conv_w_in, conv_k, conv_w_out, mla_w_dqkv, mla_q_norm, mla_w_uq, mla_kv_norm, mla_w_ukv, mla_w_o, router_w, router_bias, exp_w_gate, exp_w_up, exp_w_down, sh_w_gate, sh_w_up, sh_w_down):
    x = jnp.concatenate([x_prompt.reshape(T_CTX, D_MODEL), x_sample.reshape(T_LAT, D_MODEL)], axis=0)
    cvecs = jnp.concatenate([c_ctx[None, :], c, jnp.zeros((SUBLANES - N_GROUPS_ROWS, D_MODEL), F32)], axis=0)
    mods = _adaln(cvecs, ada_w, ada_b)

    def ln(l, k):
        return ln_g[l, k].reshape(1, D_MODEL), ln_b[l, k].reshape(1, D_MODEL)

    def moe_layer(xin, l):
        gates = _router(xin, mods[l], router_w[l].T, router_bias[l].reshape(N_EXPERTS, 1))
        return _moe(xin, mods[l], gates, sh_w_gate[l].astype(BF16), sh_w_up[l].astype(BF16),
                    sh_w_down[l].astype(BF16), exp_w_gate, exp_w_up, exp_w_down, *ln(l, 1), layer=l)

    x = _conv_mixer(x, mods[0], conv_w_in[0].astype(BF16), conv_k[0], conv_w_out[0].astype(BF16), *ln(0, 0))
    x = moe_layer(x, 0)

    p = _mla_params(mla_w_dqkv[0], mla_q_norm[0], mla_w_uq[0], mla_kv_norm[0], mla_w_ukv[0])
    rope_a, rope_b = _rope_tables()
    q, k, v, ckv, kr = _mla_proj(x, mods[1], p, rope_a, rope_b)
    kr_cache = jnp.concatenate([cache_krope[:, 0].reshape(DEC_BATCH * PAST_LEN, QK_ROPE),
                                jnp.zeros((DEC_BATCH * PAST_LEN, QK_ROPE), F32)], axis=-1)
    kc, vc = _cache_kv(cache_ckv[:, 0].reshape(DEC_BATCH * PAST_LEN, KV_LORA), kr_cache, p)
    o = _ctx_attention(q, k, v)
    o = _lat_attention(q, k, v, kc, vc, o)
    x = _oproj(o, x, mods[1], mla_w_o[0].astype(BF16), *ln(1, 0))
    x = moe_layer(x, 1)

    y_prompt = x[:T_CTX].reshape(BATCH, SEQ, D_MODEL)
    y_sample = x[T_CTX:].reshape(DEC_BATCH, DEC_SEQ, D_MODEL)
    state_ckv = ckv[:T_CTX].reshape(BATCH, 1, SEQ, KV_LORA)
    state_krope = kr[:T_CTX, :QK_ROPE].reshape(BATCH, 1, SEQ, QK_ROPE)
    return (y_prompt, y_sample, state_ckv, state_krope)
```

```python
import functools
import math

import jax
import jax.numpy as jnp
import numpy as np
from jax import lax
from jax.experimental import pallas as pl
from jax.experimental.pallas import tpu as pltpu

D_MODEL = 1024
BATCH = 16
SEQ = 256
DEPTH = 2
DEC_BATCH = 4
DEC_SEQ = 4096
PAST_LEN = 512
GRID_W = 64

N_HEADS = 8
QK_NOPE = 128
QK_ROPE = 64
V_DIM = 128
Q_LORA = 384
KV_LORA = 256
ROPE_THETA = 10000.0
ATTN_SCALE = (QK_NOPE + QK_ROPE) ** -0.5
HEAD_PAD = 256
Q_PRESCALE = ATTN_SCALE * math.log2(math.e)

N_EXPERTS = 64
TOP_K = 8
N_GROUPS = 8
TOPK_GROUPS = 4
GROUP_SIZE = N_EXPERTS // N_GROUPS
D_EXPERT = 256
D_SHARED = 256
ROUTED_SCALE = 2.5

ALPHA = (2 * DEPTH) ** 0.25
LN_EPS = 1e-5
RMS_EPS = 1e-6

GROUP_ROWS = 4096
N_GROUPS_ROWS = 1 + DEC_BATCH
T_CTX = BATCH * SEQ
T_LAT = DEC_BATCH * DEC_SEQ
T_ALL = T_CTX + T_LAT
LANES = 128
SUBLANES = 8

F32 = jnp.float32
BF16 = jnp.bfloat16
NEG_INF = float("-inf")


def _dot(a, b):
    return jnp.dot(a, b, preferred_element_type=F32)


def _dot_nt(a, b, precision=None):
    return lax.dot_general(a, b, (((1,), (1,)), ((), ())), precision=precision,
                           preferred_element_type=F32)


def _layer_norm(v, g, b):
    mu = jnp.mean(v, axis=-1, keepdims=True)
    d = v - mu
    var = jnp.mean(d * d, axis=-1, keepdims=True)
    return d * lax.rsqrt(var + LN_EPS) * g + b


def _rms_norm(v, g):
    return v * lax.rsqrt(jnp.mean(v * v, axis=-1, keepdims=True) + RMS_EPS) * g


def _silu(v):
    return v / (1.0 + jnp.exp(-v))


def _sigmoid(v):
    return 1.0 / (1.0 + jnp.exp(-v))


def _adaln_kernel(c_ref, w_ref, b_ref, o_ref):
    c = c_ref[...]
    s = _silu(c)
    o_ref[0, 0] = jnp.dot(s, w_ref[0], precision=lax.Precision.HIGHEST,
                          preferred_element_type=F32) + b_ref[0]


def _adaln(cvecs, ada_w, ada_b):
    out = pl.pallas_call(
        _adaln_kernel,
        grid=(DEPTH, 6),
        in_specs=[
            pl.BlockSpec((SUBLANES, D_MODEL), lambda l, j: (0, 0)),
            pl.BlockSpec((1, D_MODEL, D_MODEL), lambda l, j: (l, 0, j)),
            pl.BlockSpec((1, 1, D_MODEL), lambda l, j: (l, 0, j)),
        ],
        out_specs=pl.BlockSpec((1, 1, SUBLANES, D_MODEL), lambda l, j: (l, j, 0, 0)),
        out_shape=jax.ShapeDtypeStruct((DEPTH, 6, SUBLANES, D_MODEL), F32),
        compiler_params=pltpu.CompilerParams(dimension_semantics=("parallel", "parallel")),
        name="adaln",
    )(cvecs, ada_w, ada_b.reshape(DEPTH, 1, 6 * D_MODEL))
    return jnp.transpose(out[:, :, :N_GROUPS_ROWS, :], (0, 2, 1, 3))


CONV_TM = 512


def _conv_kernel(x_ref, xp_ref, xn_ref, mods_ref, win_ref, ck_ref, wout_ref, g_ref, b_ref, o_ref):
    i = pl.program_id(0)
    tm = x_ref.shape[0]
    m = mods_ref[0]
    sm, cm, gm = m[0:1], m[1:2], m[2:3]
    x = x_ref[...]
    h = (x * (1.0 + cm) + sm).astype(BF16)
    z = _dot(h, win_ref[...])
    bg = z[:, :D_MODEL]
    u = z[:, D_MODEL:2 * D_MODEL] * z[:, 2 * D_MODEL:]

    def halo_u(xh_ref):
        hh = (xh_ref[...] * (1.0 + cm) + sm).astype(BF16)
        zh = _dot(hh, win_ref[:, D_MODEL:])
        return zh[:, :D_MODEL] * zh[:, D_MODEL:]

    u_prev = halo_u(xp_ref)[SUBLANES - 1:SUBLANES]
    u_next = halo_u(xn_ref)[0:1]

    row = lax.broadcasted_iota(jnp.int32, (tm, 1), 0)
    grow = i * tm + row
    seq_len = jnp.where(grow < T_CTX, SEQ, DEC_SEQ)
    pos = jnp.bitwise_and(grow, seq_len - 1)
    is_first = pos == 0
    is_last = pos == seq_len - 1

    left = pltpu.roll(u, 1, 0)
    left = jnp.where(row == 0, u_prev, left)
    left = jnp.where(is_first, 0.0, left)
    right = pltpu.roll(u, tm - 1, 0)
    right = jnp.where(row == tm - 1, u_next, right)
    right = jnp.where(is_last, 0.0, right)

    ck = ck_ref[...]
    conv = left * ck[0:1] + u * ck[1:2] + right * ck[2:3]
    v = (bg * conv).astype(BF16)
    y = _dot(v, wout_ref[...])
    o_ref[...] = _layer_norm(ALPHA * x + gm * y, g_ref[...], b_ref[...])


def _conv_mixer(x, mods, w_in, conv_k, w_out, ln_g, ln_b):
    t = x.shape[0]
    tm = CONV_TM
    nblk8 = t // SUBLANES
    per8 = tm // SUBLANES
    return pl.pallas_call(
        _conv_kernel,
        grid=(t // tm,),
        in_specs=[
            pl.BlockSpec((tm, D_MODEL), lambda i: (i, 0)),
            pl.BlockSpec((SUBLANES, D_MODEL), lambda i: (jnp.maximum(i * per8 - 1, 0), 0)),
            pl.BlockSpec((SUBLANES, D_MODEL), lambda i: (jnp.minimum((i + 1) * per8, nblk8 - 1), 0)),
            pl.BlockSpec((1, 6, D_MODEL), lambda i: (i * tm // GROUP_ROWS, 0, 0)),
            pl.BlockSpec((D_MODEL, 3 * D_MODEL), lambda i: (0, 0)),
            pl.BlockSpec((3, D_MODEL), lambda i: (0, 0)),
            pl.BlockSpec((D_MODEL, D_MODEL), lambda i: (0, 0)),
            pl.BlockSpec((1, D_MODEL), lambda i: (0, 0)),
            pl.BlockSpec((1, D_MODEL), lambda i: (0, 0)),
        ],
        out_specs=pl.BlockSpec((tm, D_MODEL), lambda i: (i, 0)),
        out_shape=jax.ShapeDtypeStruct((t, D_MODEL), F32),
        compiler_params=pltpu.CompilerParams(dimension_semantics=("parallel",)),
        name="conv_mixer",
    )(x, x, x, mods, w_in, conv_k, w_out, ln_g, ln_b)


ROUTER_TM = 512


def _first_argmax_mask(cur, ridx, n):
    mx = jnp.max(cur, axis=0, keepdims=True)
    first = jnp.min(jnp.where(cur == mx, ridx, n), axis=0, keepdims=True)
    return ridx == first, mx


def _router_kernel(x_ref, mods_ref, rwt_ref, bias_ref, idx_ref, pos_ref, wcol_ref, count_ref, carry_ref):
    tm = x_ref.shape[0]
    m = mods_ref[0]
    sf, cf = m[3:4], m[4:5]
    hc = x_ref[...] * (1.0 + cf) + sf
    logits = _dot_nt(rwt_ref[...], hc, precision=lax.Precision.HIGHEST)
    scores = _sigmoid(logits)
    biased = scores + bias_ref[...]

    ridx8 = lax.broadcasted_iota(jnp.int32, (GROUP_SIZE, tm), 0)
    gscore = jnp.full((N_GROUPS, tm), NEG_INF, F32)
    for g in range(N_GROUPS):
        blk = biased[g * GROUP_SIZE:(g + 1) * GROUP_SIZE]
        sel, m1 = _first_argmax_mask(blk, ridx8, GROUP_SIZE)
        m2 = jnp.max(jnp.where(sel, NEG_INF, blk), axis=0, keepdims=True)
        gscore = jnp.where(ridx8 == g, m1 + m2, gscore)

    gmask = jnp.zeros((N_GROUPS, tm), jnp.bool_)
    cur = gscore
    for _ in range(TOPK_GROUPS):
        sel, _unused = _first_argmax_mask(cur, ridx8, N_GROUPS)
        gmask = jnp.logical_or(gmask, sel)
        cur = jnp.where(sel, NEG_INF, cur)

    gmask_f = gmask.astype(F32)
    blocks = []
    for g in range(N_GROUPS):
        keep = jnp.broadcast_to(gmask_f[g:g + 1], (GROUP_SIZE, tm)) > 0.5
        blocks.append(jnp.where(keep, biased[g * GROUP_SIZE:(g + 1) * GROUP_SIZE], NEG_INF))
    cur = jnp.concatenate(blocks, axis=0)

    @pl.when(pl.program_id(0) == 0)
    def _():
        carry_ref[...] = jnp.zeros(carry_ref.shape, F32)

    ridx = lax.broadcasted_iota(jnp.int32, (N_EXPERTS, tm), 0)
    kidx = lax.broadcasted_iota(jnp.int32, (TOP_K, tm), 0)
    sels = []
    chosen = jnp.zeros((N_EXPERTS, tm), jnp.bool_)
    idx_rows = jnp.zeros((TOP_K, tm), jnp.int32)
    for k in range(TOP_K):
        mx = jnp.max(cur, axis=0, keepdims=True)
        first = jnp.min(jnp.where(cur == mx, ridx, N_EXPERTS), axis=0, keepdims=True)
        sel = ridx == first
        sels.append(sel)
        chosen = jnp.logical_or(chosen, sel)
        idx_rows = jnp.where(kidx == k, first, idx_rows)
        cur = jnp.where(sel, NEG_INF, cur)

    onehot = chosen.astype(F32)
    t_row = lax.broadcasted_iota(jnp.int32, (tm, tm), 0)
    t_col = lax.broadcasted_iota(jnp.int32, (tm, tm), 1)
    before = (t_row < t_col).astype(BF16)
    rank = carry_ref[...] + _dot(onehot.astype(BF16), before)
    carry_ref[...] = carry_ref[...] + jnp.sum(onehot, axis=1, keepdims=True)
    count_ref[...] = jnp.broadcast_to(carry_ref[...], count_ref.shape)

    w = jnp.where(chosen, scores, 0.0)
    w = w / jnp.sum(w, axis=0, keepdims=True) * ROUTED_SCALE
    pos_rows = jnp.zeros((TOP_K, tm), F32)
    w_rows = jnp.zeros((TOP_K, tm), F32)
    for k in range(TOP_K):
        pos_rows = jnp.where(kidx == k, jnp.sum(jnp.where(sels[k], rank, 0.0), axis=0, keepdims=True), pos_rows)
        w_rows = jnp.where(kidx == k, jnp.sum(jnp.where(sels[k], w, 0.0), axis=0, keepdims=True), w_rows)
    idx_ref[...] = idx_rows
    pos_ref[...] = pos_rows.astype(jnp.int32)
    wpad = jnp.concatenate([w_rows, jnp.zeros((LANES - TOP_K, tm), F32)], axis=0)
    wcol_ref[...] = wpad.T


def _router(x, mods, router_wt, router_bias):
    t = x.shape[0]
    tm = ROUTER_TM
    return pl.pallas_call(
        _router_kernel,
        grid=(t // tm,),
        in_specs=[
            pl.BlockSpec((tm, D_MODEL), lambda i: (i, 0)),
            pl.BlockSpec((1, 6, D_MODEL), lambda i: (i * tm // GROUP_ROWS, 0, 0)),
            pl.BlockSpec((N_EXPERTS, D_MODEL), lambda i: (0, 0)),
            pl.BlockSpec((N_EXPERTS, 1), lambda i: (0, 0)),
        ],
        out_specs=[
            pl.BlockSpec((TOP_K, tm), lambda i: (0, i)),
            pl.BlockSpec((TOP_K, tm), lambda i: (0, i)),
            pl.BlockSpec((tm, LANES), lambda i: (i, 0)),
            pl.BlockSpec((N_EXPERTS, LANES), lambda i: (0, 0)),
        ],
        out_shape=[
            jax.ShapeDtypeStruct((TOP_K, t), jnp.int32),
            jax.ShapeDtypeStruct((TOP_K, t), jnp.int32),
            jax.ShapeDtypeStruct((t, LANES), F32),
            jax.ShapeDtypeStruct((N_EXPERTS, LANES), F32),
        ],
        scratch_shapes=[pltpu.VMEM((N_EXPERTS, 1), F32)],
        compiler_params=pltpu.CompilerParams(dimension_semantics=("arbitrary",)),
        name="moe_router",
    )(x, mods, router_wt, router_bias)


FFN_TM = 256
N_SLOTS = T_ALL * TOP_K
N_TILES = N_SLOTS // FFN_TM + N_EXPERTS
N_SORTED = N_TILES * FFN_TM


def _expert_layout(counts):
    padded = (counts + FFN_TM - 1) // FFN_TM * FFN_TM
    end = jnp.cumsum(padded)
    start = end - padded
    tile_row = jnp.arange(N_TILES, dtype=jnp.int32) * FFN_TM
    tile_expert = jnp.minimum(jnp.searchsorted(end, tile_row, side="right"), N_EXPERTS - 1)
    n_used = (end[-1] // FFN_TM).astype(jnp.int32).reshape(1)
    return start.astype(jnp.int32), end.astype(jnp.int32), tile_expert.astype(jnp.int32), n_used


DEST_TM = 2048


def _dest_kernel(start_ref, idx_ref, pos_ref, dest_ref):
    idx = idx_ref[...]
    base = jnp.zeros(idx.shape, jnp.int32)
    for e in range(N_EXPERTS):
        base = jnp.where(idx == e, start_ref[e], base)
    dest_ref[...] = base + pos_ref[...]


def _dest_rows(start, idx, pos):
    t = idx.shape[1]
    return pl.pallas_call(
        _dest_kernel,
        grid_spec=pltpu.PrefetchScalarGridSpec(
            num_scalar_prefetch=1,
            grid=(t // DEST_TM,),
            in_specs=[pl.BlockSpec((TOP_K, DEST_TM), lambda i, s: (0, i)),
                      pl.BlockSpec((TOP_K, DEST_TM), lambda i, s: (0, i))],
            out_specs=pl.BlockSpec((TOP_K, DEST_TM), lambda i, s: (0, i)),
        ),
        out_shape=jax.ShapeDtypeStruct((TOP_K, t), jnp.int32),
        compiler_params=pltpu.CompilerParams(dimension_semantics=("parallel",)),
        name="moe_dest",
    )(start, idx, pos)


DISPATCH_TM = 512


def _dispatch_kernel(end_ref, cnt_ref, nu_ref, x_ref, mods_ref, dest_ref, xs_hbm, hc_ref, zero_ref, sem, zsem):
    tm = x_ref.shape[0]

    def zero_copy(row):
        return pltpu.make_async_copy(zero_ref, xs_hbm.at[pl.ds(pl.multiple_of(row, FFN_TM), FFN_TM)], zsem)

    @pl.when(pl.program_id(0) == 0)
    def _():
        zero_ref[...] = jnp.zeros(zero_ref.shape, F32)

        @pl.loop(0, N_EXPERTS)
        def _(e):
            @pl.when(cnt_ref[e] > 0)
            def _():
                zero_copy(end_ref[e] - FFN_TM).start()

        @pl.loop(nu_ref[0], N_TILES)
        def _(j):
            zero_copy(j * FFN_TM).start()

        @pl.loop(0, N_EXPERTS)
        def _(e):
            @pl.when(cnt_ref[e] > 0)
            def _():
                zero_copy(0).wait()

        @pl.loop(nu_ref[0], N_TILES)
        def _(j):
            zero_copy(0).wait()

    m = mods_ref[0]
    sf, cf = m[3:4], m[4:5]
    hc_ref[...] = x_ref[...] * (1.0 + cf) + sf

    @pl.loop(0, tm)
    def _(r):
        for k in range(TOP_K):
            pltpu.make_async_copy(hc_ref.at[pl.ds(r, 1)], xs_hbm.at[pl.ds(dest_ref[k, r], 1)], sem).start()

    for k in range(TOP_K):
        pltpu.make_async_copy(hc_ref, xs_hbm.at[pl.ds(0, tm)], sem).wait()


def _dispatch(x, mods, dest, end, counts, n_used):
    t = x.shape[0]
    tm = DISPATCH_TM
    return pl.pallas_call(
        _dispatch_kernel,
        grid_spec=pltpu.PrefetchScalarGridSpec(
            num_scalar_prefetch=3,
            grid=(t // tm,),
            in_specs=[
                pl.BlockSpec((tm, D_MODEL), lambda i, e, c, u: (i, 0)),
                pl.BlockSpec((1, 6, D_MODEL), lambda i, e, c, u: (i * tm // GROUP_ROWS, 0, 0)),
                pl.BlockSpec((TOP_K, tm), lambda i, e, c, u: (0, i), memory_space=pltpu.SMEM),
            ],
            out_specs=pl.BlockSpec(memory_space=pl.ANY),
            scratch_shapes=[pltpu.VMEM((tm, D_MODEL), F32), pltpu.VMEM((FFN_TM, D_MODEL), F32),
                            pltpu.SemaphoreType.DMA, pltpu.SemaphoreType.DMA],
        ),
        out_shape=jax.ShapeDtypeStruct((N_SORTED, D_MODEL), F32),
        compiler_params=pltpu.CompilerParams(dimension_semantics=("arbitrary",)),
        name="moe_dispatch",
    )(end, counts, n_used, x, mods, dest)


def _ffn_kernel(te_ref, nu_ref, xs_ref, wg_ref, wu_ref, wd_ref, ys_ref):
    @pl.when(pl.program_id(0) < nu_ref[0])
    def _():
        xb = xs_ref[...].astype(BF16)
        g = _dot(xb, wg_ref[0, 0].astype(BF16))
        u = _dot(xb, wu_ref[0, 0].astype(BF16))
        a = (_silu(g) * u).astype(BF16)
        ys_ref[...] = _dot(a, wd_ref[0, 0].astype(BF16))

    @pl.when(pl.program_id(0) >= nu_ref[0])
    def _():
        ys_ref[...] = jnp.zeros(ys_ref.shape, F32)


def _expert_ffn(xs, tile_expert, n_used, wg, wu, wd, layer):
    def row_map(i, te, nu):
        return (jnp.minimum(i, nu[0] - 1), 0)

    def out_map(i, te, nu):
        return (i, 0)

    def w_map(i, te, nu):
        return (layer, te[jnp.minimum(i, nu[0] - 1)], 0, 0)

    return pl.pallas_call(
        _ffn_kernel,
        grid_spec=pltpu.PrefetchScalarGridSpec(
            num_scalar_prefetch=2,
            grid=(N_TILES,),
            in_specs=[
                pl.BlockSpec((FFN_TM, D_MODEL), row_map),
                pl.BlockSpec((1, 1, D_MODEL, D_EXPERT), w_map),
                pl.BlockSpec((1, 1, D_MODEL, D_EXPERT), w_map),
                pl.BlockSpec((1, 1, D_EXPERT, D_MODEL), w_map),
            ],
            out_specs=pl.BlockSpec((FFN_TM, D_MODEL), out_map),
        ),
        out_shape=jax.ShapeDtypeStruct((N_SORTED, D_MODEL), F32),
        compiler_params=pltpu.CompilerParams(dimension_semantics=("arbitrary",)),
        name="moe_expert_ffn",
    )(tile_expert, n_used, xs, wg, wu, wd)


COMBINE_TM = 256


def _combine_kernel(x_ref, mods_ref, wcol_ref, dcur_ref, dnext_ref, sg_ref, su_ref, sd_ref, g_ref, b_ref,
                    ys_hbm, o_ref, buf_ref, sem):
    i = pl.program_id(0)
    n = pl.num_programs(0)
    tm = x_ref.shape[0]
    slot = lax.rem(i, 2)

    def issue(d_ref, s):
        @pl.loop(0, tm)
        def _(r):
            for k in range(TOP_K):
                pltpu.make_async_copy(ys_hbm.at[pl.ds(d_ref[k, r], 1)], buf_ref.at[s, k, pl.ds(r, 1)],
                                      sem.at[s]).start()

    @pl.when(i == 0)
    def _():
        issue(dcur_ref, 0)

    @pl.when(i + 1 < n)
    def _():
        issue(dnext_ref, 1 - slot)

    m = mods_ref[0]
    sf, cf, gf = m[3:4], m[4:5], m[5:6]
    x = x_ref[...]
    hc = (x * (1.0 + cf) + sf).astype(BF16)
    a = _silu(_dot(hc, sg_ref[...])) * _dot(hc, su_ref[...])
    y = _dot(a.astype(BF16), sd_ref[...])

    for k in range(TOP_K):
        pltpu.make_async_copy(ys_hbm.at[pl.ds(0, tm)], buf_ref.at[slot, k], sem.at[slot]).wait()
    wcol = wcol_ref[...]
    for k in range(TOP_K):
        y = y + wcol[:, k:k + 1] * buf_ref[slot, k]
    o_ref[...] = _layer_norm(ALPHA * x + gf * y, g_ref[...], b_ref[...])


def _combine(x, mods, wcol, dest, ys, sg, su, sd, ln_g, ln_b):
    t = x.shape[0]
    tm = COMBINE_TM
    n = t // tm
    full = lambda shape: pl.BlockSpec(shape, lambda i: (0,) * len(shape))
    return pl.pallas_call(
        _combine_kernel,
        grid=(n,),
        in_specs=[
            pl.BlockSpec((tm, D_MODEL), lambda i: (i, 0)),
            pl.BlockSpec((1, 6, D_MODEL), lambda i: (i * tm // GROUP_ROWS, 0, 0)),
            pl.BlockSpec((tm, LANES), lambda i: (i, 0)),
            pl.BlockSpec((TOP_K, tm), lambda i: (0, i), memory_space=pltpu.SMEM),
            pl.BlockSpec((TOP_K, tm), lambda i: (0, jnp.minimum(i + 1, n - 1)), memory_space=pltpu.SMEM),
            full((D_MODEL, D_SHARED)), full((D_MODEL, D_SHARED)), full((D_SHARED, D_MODEL)),
            full((1, D_MODEL)), full((1, D_MODEL)),
            pl.BlockSpec(memory_space=pl.ANY),
        ],
        out_specs=pl.BlockSpec((tm, D_MODEL), lambda i: (i, 0)),
        out_shape=jax.ShapeDtypeStruct((t, D_MODEL), F32),
        scratch_shapes=[pltpu.VMEM((2, TOP_K, tm, D_MODEL), F32), pltpu.SemaphoreType.DMA((2,))],
        compiler_params=pltpu.CompilerParams(dimension_semantics=("arbitrary",)),
        name="moe_combine",
    )(x, mods, wcol, dest, dest, sg, su, sd, ln_g, ln_b, ys)


MLA_TM = 512


def _mla_proj_kernel(x_ref, mods_ref, wdq_ref, wdkv_ref, wkr_ref, qn_ref, kvn_ref, wuq_ref,
                     wukn_ref, wuv_ref, ta_ref, tb_ref, q_ref, k_ref, v_ref, ckv_ref, kr_ref):
    m = mods_ref[0]
    sm, cm = m[0:1], m[1:2]
    h = (x_ref[...] * (1.0 + cm) + sm).astype(BF16)
    cq = _rms_norm(_dot(h, wdq_ref[...]), qn_ref[...])
    ckv = _rms_norm(_dot(h, wdkv_ref[...]), kvn_ref[...])
    kr2 = _dot(h, wkr_ref[...])
    ckv_ref[...] = ckv
    kr_ref[...] = kr2

    ka = ta_ref[0]
    kb = tb_ref[0]
    tm = ka.shape[0]
    ta = jnp.concatenate([jnp.full((tm, QK_NOPE), Q_PRESCALE, F32), ka * Q_PRESCALE], axis=1)
    tb = jnp.concatenate([jnp.zeros((tm, QK_NOPE), F32), kb * Q_PRESCALE], axis=1)
    krr = kr2 * ka + pltpu.roll(kr2, QK_ROPE, 1) * kb

    qpre = _dot(cq.astype(BF16), wuq_ref[...])
    ckv_b = ckv.astype(BF16)
    kn = _dot(ckv_b, wukn_ref[...])
    v_ref[...] = _dot(ckv_b, wuv_ref[...]).astype(BF16)
    for hd in range(N_HEADS):
        qh = qpre[:, hd * HEAD_PAD:(hd + 1) * HEAD_PAD]
        qrot = qh * ta + pltpu.roll(qh, HEAD_PAD - QK_ROPE, 1) * tb
        q_ref[:, hd * HEAD_PAD:(hd + 1) * HEAD_PAD] = qrot.astype(BF16)
        k_ref[:, hd * HEAD_PAD:hd * HEAD_PAD + QK_NOPE] = kn[:, hd * QK_NOPE:(hd + 1) * QK_NOPE].astype(BF16)
        k_ref[:, hd * HEAD_PAD + QK_NOPE:(hd + 1) * HEAD_PAD] = krr.astype(BF16)


def _mla_proj(x, mods, p, rope_a, rope_b):
    t = x.shape[0]
    tm = MLA_TM
    full = lambda shape: pl.BlockSpec(shape, lambda i: (0,) * len(shape))
    rope_spec = pl.BlockSpec(
        (1, tm, 2 * QK_ROPE),
        lambda i: (jnp.minimum(i * tm // GROUP_ROWS, 1), (i * tm % GROUP_ROWS) // tm, 0))
    return pl.pallas_call(
        _mla_proj_kernel,
        grid=(t // tm,),
        in_specs=[
            pl.BlockSpec((tm, D_MODEL), lambda i: (i, 0)),
            pl.BlockSpec((1, 6, D_MODEL), lambda i: (i * tm // GROUP_ROWS, 0, 0)),
            full((D_MODEL, Q_LORA)), full((D_MODEL, KV_LORA)), full((D_MODEL, 2 * QK_ROPE)),
            full((1, Q_LORA)), full((1, KV_LORA)),
            full((Q_LORA, N_HEADS * HEAD_PAD)),
            full((KV_LORA, N_HEADS * QK_NOPE)), full((KV_LORA, N_HEADS * V_DIM)),
            rope_spec, rope_spec,
        ],
        out_specs=[
            pl.BlockSpec((tm, N_HEADS * HEAD_PAD), lambda i: (i, 0)),
            pl.BlockSpec((tm, N_HEADS * HEAD_PAD), lambda i: (i, 0)),
            pl.BlockSpec((tm, N_HEADS * V_DIM), lambda i: (i, 0)),
            pl.BlockSpec((tm, KV_LORA), lambda i: (i, 0)),
            pl.BlockSpec((tm, 2 * QK_ROPE), lambda i: (i, 0)),
        ],
        out_shape=[
            jax.ShapeDtypeStruct((t, N_HEADS * HEAD_PAD), BF16),
            jax.ShapeDtypeStruct((t, N_HEADS * HEAD_PAD), BF16),
            jax.ShapeDtypeStruct((t, N_HEADS * V_DIM), BF16),
            jax.ShapeDtypeStruct((t, KV_LORA), F32),
            jax.ShapeDtypeStruct((t, 2 * QK_ROPE), F32),
        ],
        compiler_params=pltpu.CompilerParams(dimension_semantics=("parallel",)),
        name="mla_proj",
    )(x, mods, p["w_dq"], p["w_dkv"], p["w_kr"], p["q_norm"], p["kv_norm"], p["w_uq"],
      p["w_ukn"], p["w_uv"], rope_a, rope_b)


def _cache_kv_kernel(ckv_ref, kr_ref, wukn_ref, wuv_ref, k_ref, v_ref):
    ckv_b = ckv_ref[...].astype(BF16)
    kn = _dot(ckv_b, wukn_ref[...])
    v_ref[...] = _dot(ckv_b, wuv_ref[...]).astype(BF16)
    kr = kr_ref[...].astype(BF16)
    for hd in range(N_HEADS):
        k_ref[:, hd * HEAD_PAD:hd * HEAD_PAD + QK_NOPE] = kn[:, hd * QK_NOPE:(hd + 1) * QK_NOPE].astype(BF16)
        k_ref[:, hd * HEAD_PAD + QK_NOPE:(hd + 1) * HEAD_PAD] = kr


def _cache_kv(ckv, kr_pad, p):
    t = ckv.shape[0]
    tm = PAST_LEN
    full = lambda shape: pl.BlockSpec(shape, lambda i: (0,) * len(shape))
    return pl.pallas_call(
        _cache_kv_kernel,
        grid=(t // tm,),
        in_specs=[
            pl.BlockSpec((tm, KV_LORA), lambda i: (i, 0)),
            pl.BlockSpec((tm, 2 * QK_ROPE), lambda i: (i, 0)),
            full((KV_LORA, N_HEADS * QK_NOPE)), full((KV_LORA, N_HEADS * V_DIM)),
        ],
        out_specs=[
            pl.BlockSpec((tm, N_HEADS * HEAD_PAD), lambda i: (i, 0)),
            pl.BlockSpec((tm, N_HEADS * V_DIM), lambda i: (i, 0)),
        ],
        out_shape=[
            jax.ShapeDtypeStruct((t, N_HEADS * HEAD_PAD), BF16),
            jax.ShapeDtypeStruct((t, N_HEADS * V_DIM), BF16),
        ],
        compiler_params=pltpu.CompilerParams(dimension_semantics=("parallel",)),
        name="mla_cache_kv",
    )(ckv, kr_pad, p["w_ukn"], p["w_uv"])


def _ctx_attn_kernel(q_ref, k_ref, v_ref, o_ref):
    for hd in range(N_HEADS):
        q = q_ref[:, hd * HEAD_PAD:(hd + 1) * HEAD_PAD]
        k = k_ref[:, hd * HEAD_PAD:(hd + 1) * HEAD_PAD]
        s = _dot_nt(q, k)
        s = s - jnp.max(s, axis=-1, keepdims=True)
        p = jnp.exp2(s)
        p = p / jnp.sum(p, axis=-1, keepdims=True)
        o = _dot(p.astype(BF16), v_ref[:, hd * V_DIM:(hd + 1) * V_DIM])
        o_ref[:, hd * V_DIM:(hd + 1) * V_DIM] = o.astype(BF16)


def _ctx_attention(q, k, v):
    return pl.pallas_call(
        _ctx_attn_kernel,
        grid=(BATCH,),
        in_specs=[
            pl.BlockSpec((SEQ, N_HEADS * HEAD_PAD), lambda b: (b, 0)),
            pl.BlockSpec((SEQ, N_HEADS * HEAD_PAD), lambda b: (b, 0)),
            pl.BlockSpec((SEQ, N_HEADS * V_DIM), lambda b: (b, 0)),
        ],
        out_specs=pl.BlockSpec((SEQ, N_HEADS * V_DIM), lambda b: (b, 0)),
        out_shape=jax.ShapeDtypeStruct((T_CTX, N_HEADS * V_DIM), BF16),
        compiler_params=pltpu.CompilerParams(dimension_semantics=("parallel",)),
        name="ctx_attention",
    )(q, k, v)


LAT_TQ = 512
LAT_TK = 512


def _lat_attn_kernel(q_ref, k_ref, v_ref, kc_ref, vc_ref, o_ref, s_ref, p_ref, m_ref):
    n_lat = DEC_SEQ // LAT_TK
    th = q_ref.shape[0] // 2
    halves = [pl.ds(0, th), pl.ds(th, th)]
    qs = [q_ref[r, :] for r in halves]

    mparts = [jnp.full((th, LANES), NEG_INF, F32) for _ in halves]
    for c in range(n_lat + 1):
        k = kc_ref[...] if c == n_lat else k_ref[c * LAT_TK:(c + 1) * LAT_TK, :]
        for hf, r in enumerate(halves):
            s = _dot_nt(qs[hf], k)
            s_ref[r, c * LAT_TK:(c + 1) * LAT_TK] = s
            for j in range(LAT_TK // LANES):
                mparts[hf] = jnp.maximum(mparts[hf], s[:, j * LANES:(j + 1) * LANES])
    for hf, r in enumerate(halves):
        m_ref[r, :] = jnp.broadcast_to(jnp.max(mparts[hf], axis=-1, keepdims=True), (th, LANES))

    lparts = [jnp.zeros((th, LANES), F32) for _ in halves]
    accs = [jnp.zeros((th, V_DIM), F32) for _ in halves]
    for c in range(n_lat + 1):
        v = vc_ref[...] if c == n_lat else v_ref[c * LAT_TK:(c + 1) * LAT_TK, :]
        for hf, r in enumerate(halves):
            for j in range(c * LAT_TK // LANES, (c + 1) * LAT_TK // LANES):
                p = jnp.exp2(s_ref[r, j * LANES:(j + 1) * LANES] - m_ref[r, :])
                lparts[hf] = lparts[hf] + p
                p_ref[r, j * LANES:(j + 1) * LANES] = p.astype(BF16)
            accs[hf] = accs[hf] + _dot(p_ref[r, c * LAT_TK:(c + 1) * LAT_TK], v)
    for hf, r in enumerate(halves):
        o_ref[r, :] = (accs[hf] / jnp.sum(lparts[hf], axis=-1, keepdims=True)).astype(BF16)


def _lat_attention(q, k, v, kc, vc):
    nq = DEC_SEQ // LAT_TQ
    return pl.pallas_call(
        _lat_attn_kernel,
        grid=(DEC_BATCH, N_HEADS, nq),
        in_specs=[
            pl.BlockSpec((LAT_TQ, HEAD_PAD), lambda b, h, i: ((b + 1) * nq + i, h)),
            pl.BlockSpec((DEC_SEQ, HEAD_PAD), lambda b, h, i: (b + 1, h)),
            pl.BlockSpec((DEC_SEQ, V_DIM), lambda b, h, i: (b + 1, h)),
            pl.BlockSpec((PAST_LEN, HEAD_PAD), lambda b, h, i: (b, h)),
            pl.BlockSpec((PAST_LEN, V_DIM), lambda b, h, i: (b, h)),
        ],
        out_specs=pl.BlockSpec((LAT_TQ, V_DIM), lambda b, h, i: (b * nq + i, h)),
        out_shape=jax.ShapeDtypeStruct((T_LAT, N_HEADS * V_DIM), BF16),
        scratch_shapes=[pltpu.VMEM((LAT_TQ, DEC_SEQ + PAST_LEN), F32),
                        pltpu.VMEM((LAT_TQ, DEC_SEQ + PAST_LEN), BF16),
                        pltpu.VMEM((LAT_TQ, LANES), F32)],
        compiler_params=pltpu.CompilerParams(
            dimension_semantics=("parallel", "parallel", "parallel")),
        name="lat_attention",
    )(q, k, v, kc, vc)


OPROJ_TM = 512


def _oproj_kernel(oc_ref, ol_ref, x_ref, mods_ref, wo_ref, g_ref, b_ref, out_ref):
    is_ctx = pl.program_id(0) < T_CTX // OPROJ_TM
    o = jnp.where(is_ctx, oc_ref[...], ol_ref[...])
    gm = mods_ref[0][2:3]
    y = _dot(o, wo_ref[...])
    out_ref[...] = _layer_norm(ALPHA * x_ref[...] + gm * y, g_ref[...], b_ref[...])


def _oproj(o_ctx, o_lat, x, mods, w_o, ln_g, ln_b):
    t = x.shape[0]
    tm = OPROJ_TM
    n_ctx = T_CTX // tm
    return pl.pallas_call(
        _oproj_kernel,
        grid=(t // tm,),
        in_specs=[
            pl.BlockSpec((tm, N_HEADS * V_DIM), lambda i: (jnp.minimum(i, n_ctx - 1), 0)),
            pl.BlockSpec((tm, N_HEADS * V_DIM), lambda i: (jnp.maximum(i - n_ctx, 0), 0)),
            pl.BlockSpec((tm, D_MODEL), lambda i: (i, 0)),
            pl.BlockSpec((1, 6, D_MODEL), lambda i: (i * tm // GROUP_ROWS, 0, 0)),
            pl.BlockSpec((N_HEADS * V_DIM, D_MODEL), lambda i: (0, 0)),
            pl.BlockSpec((1, D_MODEL), lambda i: (0, 0)),
            pl.BlockSpec((1, D_MODEL), lambda i: (0, 0)),
        ],
        out_specs=pl.BlockSpec((tm, D_MODEL), lambda i: (i, 0)),
        out_shape=jax.ShapeDtypeStruct((t, D_MODEL), F32),
        compiler_params=pltpu.CompilerParams(dimension_semantics=("parallel",)),
        name="attn_oproj",
    )(o_ctx, o_lat, x, mods, w_o, ln_g, ln_b)


def _swap16(w):
    q = QK_ROPE // 4
    return jnp.concatenate([w[..., q:2 * q], w[..., :q], w[..., 3 * q:], w[..., 2 * q:3 * q]], axis=-1)


def _mla_params(w_dqkv, q_norm, w_uq, kv_norm, w_ukv):
    w_kr = w_dqkv[:, Q_LORA + KV_LORA:]
    wq = w_uq.reshape(Q_LORA, N_HEADS, QK_NOPE + QK_ROPE)
    wq_r = wq[..., QK_NOPE:]
    wq = jnp.concatenate([wq[..., :QK_NOPE], wq_r, _swap16(wq_r)], axis=-1)
    wkv = w_ukv.reshape(KV_LORA, N_HEADS, QK_NOPE + V_DIM)
    return {
        "w_dq": w_dqkv[:, :Q_LORA].astype(BF16),
        "w_dkv": w_dqkv[:, Q_LORA:Q_LORA + KV_LORA].astype(BF16),
        "w_kr": jnp.concatenate([w_kr, _swap16(w_kr)], axis=-1).astype(BF16),
        "q_norm": q_norm.reshape(1, Q_LORA),
        "kv_norm": kv_norm.reshape(1, KV_LORA),
        "w_uq": wq.reshape(Q_LORA, N_HEADS * HEAD_PAD).astype(BF16),
        "w_ukn": wkv[..., :QK_NOPE].reshape(KV_LORA, N_HEADS * QK_NOPE).astype(BF16),
        "w_uv": wkv[..., QK_NOPE:].reshape(KV_LORA, N_HEADS * V_DIM).astype(BF16),
    }


def _rope_tables():
    nf = QK_ROPE // 4
    t = np.arange(DEC_SEQ)
    row = (t // GRID_W).astype(np.float32)
    col = (t % GRID_W).astype(np.float32)
    inv = (ROPE_THETA ** (-np.arange(nf, dtype=np.float32) / nf)).astype(np.float32)
    ar, ac = row[:, None] * inv, col[:, None] * inv
    pad = np.zeros((DEC_SEQ, QK_ROPE), np.float32)
    cos = np.concatenate([np.cos(ar), np.cos(ar), np.cos(ac), np.cos(ac), pad], axis=-1)
    sin = np.concatenate([-np.sin(ar), np.sin(ar), -np.sin(ac), np.sin(ac), pad], axis=-1)
    cos_id = np.concatenate([np.ones((DEC_SEQ, QK_ROPE), np.float32), pad], axis=-1)
    sin_id = np.zeros((DEC_SEQ, 2 * QK_ROPE), np.float32)
    return (jnp.asarray(np.stack([cos_id, cos]).astype(np.float32)),
            jnp.asarray(np.stack([sin_id, sin]).astype(np.float32)))


def kernel(x_prompt, x_sample, cache_ckv, cache_krope, c, c_ctx, ada_w, ada_b, ln_g, ln_b, conv_w_in, conv_k, conv_w_out, mla_w_dqkv, mla_q_norm, mla_w_uq, mla_kv_norm, mla_w_ukv, mla_w_o, router_w, router_bias, exp_w_gate, exp_w_up, exp_w_down, sh_w_gate, sh_w_up, sh_w_down):
    x = jnp.concatenate([x_prompt.reshape(T_CTX, D_MODEL), x_sample.reshape(T_LAT, D_MODEL)], axis=0)
    cvecs = jnp.concatenate([c_ctx[None, :], c, jnp.zeros((SUBLANES - N_GROUPS_ROWS, D_MODEL), F32)], axis=0)
    mods = _adaln(cvecs, ada_w, ada_b)

    def ln(l, k):
        return ln_g[l, k].reshape(1, D_MODEL), ln_b[l, k].reshape(1, D_MODEL)

    def moe_layer(xin, l):
        idx, pos, wcol, counts = _router(xin, mods[l], router_w[l].T, router_bias[l].reshape(N_EXPERTS, 1))
        cnt = counts[:, 0].astype(jnp.int32)
        start, end, tile_expert, n_used = _expert_layout(cnt)
        dest = _dest_rows(start, idx, pos)
        xs = _dispatch(xin, mods[l], dest, end, cnt, n_used)
        ys = _expert_ffn(xs, tile_expert, n_used, exp_w_gate, exp_w_up, exp_w_down, l)
        return _combine(xin, mods[l], wcol, dest, ys, sh_w_gate[l].astype(BF16), sh_w_up[l].astype(BF16),
                        sh_w_down[l].astype(BF16), *ln(l, 1))

    x = _conv_mixer(x, mods[0], conv_w_in[0].astype(BF16), conv_k[0], conv_w_out[0].astype(BF16), *ln(0, 0))
    x = moe_layer(x, 0)

    p = _mla_params(mla_w_dqkv[0], mla_q_norm[0], mla_w_uq[0], mla_kv_norm[0], mla_w_ukv[0])
    rope_a, rope_b = _rope_tables()
    q, k, v, ckv, kr = _mla_proj(x, mods[1], p, rope_a, rope_b)
    kr_cache = jnp.concatenate([cache_krope[:, 0].reshape(DEC_BATCH * PAST_LEN, QK_ROPE),
                                jnp.zeros((DEC_BATCH * PAST_LEN, QK_ROPE), F32)], axis=-1)
    kc, vc = _cache_kv(cache_ckv[:, 0].reshape(DEC_BATCH * PAST_LEN, KV_LORA), kr_cache, p)
    o_ctx = _ctx_attention(q, k, v)
    o_lat = _lat_attention(q, k, v, kc, vc)
    x = _oproj(o_ctx, o_lat, x, mods[1], mla_w_o[0].astype(BF16), *ln(1, 0))
    x = moe_layer(x, 1)

    y_prompt = x[:T_CTX].reshape(BATCH, SEQ, D_MODEL)
    y_sample = x[T_CTX:].reshape(DEC_BATCH, DEC_SEQ, D_MODEL)
    state_ckv = ckv[:T_CTX].reshape(BATCH, 1, SEQ, KV_LORA)
    state_krope = kr[:T_CTX, :QK_ROPE].reshape(BATCH, 1, SEQ, QK_ROPE)
    return (y_prompt, y_sample, state_ckv, state_krope)
```

```python
import functools
import math

import jax
import jax.numpy as jnp
import numpy as np
from jax import lax
from jax.experimental import pallas as pl
from jax.experimental.pallas import tpu as pltpu
from jax.experimental.pallas import tpu_sc as plsc

D_MODEL = 1024
BATCH = 16
SEQ = 256
DEPTH = 2
DEC_BATCH = 4
DEC_SEQ = 4096
PAST_LEN = 512
GRID_W = 64

N_HEADS = 8
QK_NOPE = 128
QK_ROPE = 64
V_DIM = 128
Q_LORA = 384
KV_LORA = 256
ROPE_THETA = 10000.0
ATTN_SCALE = (QK_NOPE + QK_ROPE) ** -0.5
HEAD_PAD = 256
Q_PRESCALE = ATTN_SCALE * math.log2(math.e)

N_EXPERTS = 64
TOP_K = 8
N_GROUPS = 8
TOPK_GROUPS = 4
GROUP_SIZE = N_EXPERTS // N_GROUPS
D_EXPERT = 256
D_SHARED = 256
ROUTED_SCALE = 2.5

ALPHA = (2 * DEPTH) ** 0.25
LN_EPS = 1e-5
RMS_EPS = 1e-6

GROUP_ROWS = 4096
N_GROUPS_ROWS = 1 + DEC_BATCH
T_CTX = BATCH * SEQ
T_LAT = DEC_BATCH * DEC_SEQ
T_ALL = T_CTX + T_LAT
LANES = 128
SUBLANES = 8

F32 = jnp.float32
BF16 = jnp.bfloat16
NEG_INF = float("-inf")


def _dot(a, b):
    return jnp.dot(a, b, preferred_element_type=F32)


def _dot_nt(a, b, precision=None):
    return lax.dot_general(a, b, (((1,), (1,)), ((), ())), precision=precision,
                           preferred_element_type=F32)


def _layer_norm(v, g, b):
    mu = jnp.mean(v, axis=-1, keepdims=True)
    d = v - mu
    var = jnp.mean(d * d, axis=-1, keepdims=True)
    return d * lax.rsqrt(var + LN_EPS) * g + b


def _rms_norm(v, g):
    return v * lax.rsqrt(jnp.mean(v * v, axis=-1, keepdims=True) + RMS_EPS) * g


def _silu(v):
    return v / (1.0 + jnp.exp(-v))


def _sigmoid(v):
    return 1.0 / (1.0 + jnp.exp(-v))


def _adaln_kernel(c_ref, w_ref, b_ref, o_ref):
    c = c_ref[...]
    s = _silu(c)
    o_ref[0, 0] = jnp.dot(s, w_ref[0], precision=lax.Precision.HIGHEST,
                          preferred_element_type=F32) + b_ref[0]


def _adaln(cvecs, ada_w, ada_b):
    out = pl.pallas_call(
        _adaln_kernel,
        grid=(DEPTH, 6),
        in_specs=[
            pl.BlockSpec((SUBLANES, D_MODEL), lambda l, j: (0, 0)),
            pl.BlockSpec((1, D_MODEL, D_MODEL), lambda l, j: (l, 0, j)),
            pl.BlockSpec((1, 1, D_MODEL), lambda l, j: (l, 0, j)),
        ],
        out_specs=pl.BlockSpec((1, 1, SUBLANES, D_MODEL), lambda l, j: (l, j, 0, 0)),
        out_shape=jax.ShapeDtypeStruct((DEPTH, 6, SUBLANES, D_MODEL), F32),
        compiler_params=pltpu.CompilerParams(dimension_semantics=("parallel", "parallel")),
        name="adaln",
    )(cvecs, ada_w, ada_b.reshape(DEPTH, 1, 6 * D_MODEL))
    return jnp.transpose(out[:, :, :N_GROUPS_ROWS, :], (0, 2, 1, 3))


CONV_TM = 512


def _conv_kernel(x_ref, xp_ref, xn_ref, mods_ref, win_ref, ck_ref, wout_ref, g_ref, b_ref, o_ref):
    i = pl.program_id(0)
    tm = x_ref.shape[0]
    m = mods_ref[0]
    sm, cm, gm = m[0:1], m[1:2], m[2:3]
    x = x_ref[...]
    h = (x * (1.0 + cm) + sm).astype(BF16)
    z = _dot(h, win_ref[...])
    bg = z[:, :D_MODEL]
    u = z[:, D_MODEL:2 * D_MODEL] * z[:, 2 * D_MODEL:]

    def halo_u(xh_ref):
        hh = (xh_ref[...] * (1.0 + cm) + sm).astype(BF16)
        zh = _dot(hh, win_ref[:, D_MODEL:])
        return zh[:, :D_MODEL] * zh[:, D_MODEL:]

    u_prev = halo_u(xp_ref)[SUBLANES - 1:SUBLANES]
    u_next = halo_u(xn_ref)[0:1]

    row = lax.broadcasted_iota(jnp.int32, (tm, 1), 0)
    grow = i * tm + row
    seq_len = jnp.where(grow < T_CTX, SEQ, DEC_SEQ)
    pos = jnp.bitwise_and(grow, seq_len - 1)
    is_first = pos == 0
    is_last = pos == seq_len - 1

    left = pltpu.roll(u, 1, 0)
    left = jnp.where(row == 0, u_prev, left)
    left = jnp.where(is_first, 0.0, left)
    right = pltpu.roll(u, tm - 1, 0)
    right = jnp.where(row == tm - 1, u_next, right)
    right = jnp.where(is_last, 0.0, right)

    ck = ck_ref[...]
    conv = left * ck[0:1] + u * ck[1:2] + right * ck[2:3]
    v = (bg * conv).astype(BF16)
    y = _dot(v, wout_ref[...])
    o_ref[...] = _layer_norm(ALPHA * x + gm * y, g_ref[...], b_ref[...])


def _conv_mixer(x, mods, w_in, conv_k, w_out, ln_g, ln_b):
    t = x.shape[0]
    tm = CONV_TM
    nblk8 = t // SUBLANES
    per8 = tm // SUBLANES
    return pl.pallas_call(
        _conv_kernel,
        grid=(t // tm,),
        in_specs=[
            pl.BlockSpec((tm, D_MODEL), lambda i: (i, 0)),
            pl.BlockSpec((SUBLANES, D_MODEL), lambda i: (jnp.maximum(i * per8 - 1, 0), 0)),
            pl.BlockSpec((SUBLANES, D_MODEL), lambda i: (jnp.minimum((i + 1) * per8, nblk8 - 1), 0)),
            pl.BlockSpec((1, 6, D_MODEL), lambda i: (i * tm // GROUP_ROWS, 0, 0)),
            pl.BlockSpec((D_MODEL, 3 * D_MODEL), lambda i: (0, 0)),
            pl.BlockSpec((3, D_MODEL), lambda i: (0, 0)),
            pl.BlockSpec((D_MODEL, D_MODEL), lambda i: (0, 0)),
            pl.BlockSpec((1, D_MODEL), lambda i: (0, 0)),
            pl.BlockSpec((1, D_MODEL), lambda i: (0, 0)),
        ],
        out_specs=pl.BlockSpec((tm, D_MODEL), lambda i: (i, 0)),
        out_shape=jax.ShapeDtypeStruct((t, D_MODEL), F32),
        compiler_params=pltpu.CompilerParams(dimension_semantics=("parallel",)),
        name="conv_mixer",
    )(x, x, x, mods, w_in, conv_k, w_out, ln_g, ln_b)


ROUTER_TM = 512


def _first_argmax_mask(cur, ridx, n):
    mx = jnp.max(cur, axis=0, keepdims=True)
    first = jnp.min(jnp.where(cur == mx, ridx, n), axis=0, keepdims=True)
    return ridx == first, mx


def _router_kernel(x_ref, mods_ref, rwt_ref, bias_ref, idx_ref, pos_ref, wcol_ref, count_ref, hpa_ref, hpb_ref,
                   carry_ref):
    tm = x_ref.shape[0]
    m = mods_ref[0]
    sf, cf = m[3:4], m[4:5]
    hc = x_ref[...] * (1.0 + cf) + sf
    words = _pack_rows(hc)
    hpa_ref[...] = words[:, :D_QUARTER]
    hpb_ref[...] = words[:, D_QUARTER:]
    logits = _dot_nt(rwt_ref[...], hc, precision=lax.Precision.HIGHEST)
    scores = _sigmoid(logits)
    biased = scores + bias_ref[...]

    ridx8 = lax.broadcasted_iota(jnp.int32, (GROUP_SIZE, tm), 0)
    gscore = jnp.full((N_GROUPS, tm), NEG_INF, F32)
    for g in range(N_GROUPS):
        blk = biased[g * GROUP_SIZE:(g + 1) * GROUP_SIZE]
        sel, m1 = _first_argmax_mask(blk, ridx8, GROUP_SIZE)
        m2 = jnp.max(jnp.where(sel, NEG_INF, blk), axis=0, keepdims=True)
        gscore = jnp.where(ridx8 == g, m1 + m2, gscore)

    gmask = jnp.zeros((N_GROUPS, tm), jnp.bool_)
    cur = gscore
    for _ in range(TOPK_GROUPS):
        sel, _unused = _first_argmax_mask(cur, ridx8, N_GROUPS)
        gmask = jnp.logical_or(gmask, sel)
        cur = jnp.where(sel, NEG_INF, cur)

    gmask_f = gmask.astype(F32)
    blocks = []
    for g in range(N_GROUPS):
        keep = jnp.broadcast_to(gmask_f[g:g + 1], (GROUP_SIZE, tm)) > 0.5
        blocks.append(jnp.where(keep, biased[g * GROUP_SIZE:(g + 1) * GROUP_SIZE], NEG_INF))
    cur = jnp.concatenate(blocks, axis=0)

    @pl.when(pl.program_id(0) == 0)
    def _():
        carry_ref[...] = jnp.zeros(carry_ref.shape, F32)

    ridx = lax.broadcasted_iota(jnp.int32, (N_EXPERTS, tm), 0)
    kidx = lax.broadcasted_iota(jnp.int32, (TOP_K, tm), 0)
    sels = []
    chosen = jnp.zeros((N_EXPERTS, tm), jnp.bool_)
    idx_rows = jnp.zeros((TOP_K, tm), jnp.int32)
    for k in range(TOP_K):
        mx = jnp.max(cur, axis=0, keepdims=True)
        first = jnp.min(jnp.where(cur == mx, ridx, N_EXPERTS), axis=0, keepdims=True)
        sel = ridx == first
        sels.append(sel)
        chosen = jnp.logical_or(chosen, sel)
        idx_rows = jnp.where(kidx == k, first, idx_rows)
        cur = jnp.where(sel, NEG_INF, cur)

    onehot = chosen.astype(F32)
    t_row = lax.broadcasted_iota(jnp.int32, (tm, tm), 0)
    t_col = lax.broadcasted_iota(jnp.int32, (tm, tm), 1)
    before = (t_row < t_col).astype(BF16)
    rank = carry_ref[...] + _dot(onehot.astype(BF16), before)
    carry_ref[...] = carry_ref[...] + jnp.sum(onehot, axis=1, keepdims=True)
    count_ref[...] = jnp.broadcast_to(carry_ref[...], count_ref.shape)

    w = jnp.where(chosen, scores, 0.0)
    w = w / jnp.sum(w, axis=0, keepdims=True) * ROUTED_SCALE
    pos_rows = jnp.zeros((TOP_K, tm), F32)
    w_rows = jnp.zeros((TOP_K, tm), F32)
    for k in range(TOP_K):
        pos_rows = jnp.where(kidx == k, jnp.sum(jnp.where(sels[k], rank, 0.0), axis=0, keepdims=True), pos_rows)
        w_rows = jnp.where(kidx == k, jnp.sum(jnp.where(sels[k], w, 0.0), axis=0, keepdims=True), w_rows)
    idx_ref[...] = idx_rows
    pos_ref[...] = pos_rows.astype(jnp.int32)
    wpad = jnp.concatenate([w_rows, jnp.zeros((LANES - TOP_K, tm), F32)], axis=0)
    wcol_ref[...] = wpad.T


def _router(x, mods, router_wt, router_bias):
    t = x.shape[0]
    tm = ROUTER_TM
    return pl.pallas_call(
        _router_kernel,
        grid=(t // tm,),
        in_specs=[
            pl.BlockSpec((tm, D_MODEL), lambda i: (i, 0)),
            pl.BlockSpec((1, 6, D_MODEL), lambda i: (i * tm // GROUP_ROWS, 0, 0)),
            pl.BlockSpec((N_EXPERTS, D_MODEL), lambda i: (0, 0)),
            pl.BlockSpec((N_EXPERTS, 1), lambda i: (0, 0)),
        ],
        out_specs=[
            pl.BlockSpec((TOP_K, tm), lambda i: (0, i)),
            pl.BlockSpec((TOP_K, tm), lambda i: (0, i)),
            pl.BlockSpec((tm, LANES), lambda i: (i, 0)),
            pl.BlockSpec((N_EXPERTS, LANES), lambda i: (0, 0)),
            pl.BlockSpec((tm, D_QUARTER), lambda i: (i, 0)),
            pl.BlockSpec((tm, D_QUARTER), lambda i: (i, 0)),
        ],
        out_shape=[
            jax.ShapeDtypeStruct((TOP_K, t), jnp.int32),
            jax.ShapeDtypeStruct((TOP_K, t), jnp.int32),
            jax.ShapeDtypeStruct((t, LANES), F32),
            jax.ShapeDtypeStruct((N_EXPERTS, LANES), F32),
            jax.ShapeDtypeStruct((t, D_QUARTER), jnp.uint32),
            jax.ShapeDtypeStruct((t, D_QUARTER), jnp.uint32),
        ],
        scratch_shapes=[pltpu.VMEM((N_EXPERTS, 1), F32)],
        compiler_params=pltpu.CompilerParams(dimension_semantics=("arbitrary",)),
        name="moe_router",
    )(x, mods, router_wt, router_bias)


FFN_TM = 256
N_SLOTS = T_ALL * TOP_K
N_TILES = N_SLOTS // FFN_TM + N_EXPERTS
N_SORTED = N_TILES * FFN_TM


def _expert_layout(counts):
    padded = (counts + FFN_TM - 1) // FFN_TM * FFN_TM
    end = jnp.cumsum(padded)
    start = end - padded
    tile_row = jnp.arange(N_TILES, dtype=jnp.int32) * FFN_TM
    tile_expert = jnp.minimum(jnp.sum(end[None, :] <= tile_row[:, None], axis=1), N_EXPERTS - 1)
    tile_valid = jnp.clip(start[tile_expert] + counts[tile_expert] - tile_row, 0, FFN_TM)
    n_used = (end[-1] // FFN_TM).astype(jnp.int32).reshape(1)
    return start.astype(jnp.int32), tile_expert.astype(jnp.int32), tile_valid.astype(jnp.int32), n_used


D_HALF = D_MODEL // 2
D_QUARTER = D_MODEL // 4


def _pack_rows(v):
    hi = lax.bitcast_convert_type(v[:, :D_HALF].astype(BF16).astype(F32), jnp.uint32)
    lo = lax.bitcast_convert_type(v[:, D_HALF:].astype(BF16).astype(F32), jnp.uint32)
    return jnp.bitwise_or(hi, jnp.right_shift(lo, jnp.uint32(16)))


def _unpack_rows(w):
    hi = lax.bitcast_convert_type(jnp.bitwise_and(w, jnp.uint32(0xFFFF0000)), F32)
    lo = lax.bitcast_convert_type(jnp.left_shift(w, jnp.uint32(16)), F32)
    return hi, lo


DEST_TM = 2048


def _dest_kernel(start_ref, idx_ref, pos_ref, dest_ref):
    idx = idx_ref[...]
    base = jnp.zeros(idx.shape, jnp.int32)
    for e in range(N_EXPERTS):
        base = jnp.where(idx == e, start_ref[e], base)
    dest_ref[...] = base + pos_ref[...]


def _dest_rows(start, idx, pos):
    t = idx.shape[1]
    return pl.pallas_call(
        _dest_kernel,
        grid_spec=pltpu.PrefetchScalarGridSpec(
            num_scalar_prefetch=1,
            grid=(t // DEST_TM,),
            in_specs=[pl.BlockSpec((TOP_K, DEST_TM), lambda i, s: (0, i)),
                      pl.BlockSpec((TOP_K, DEST_TM), lambda i, s: (0, i))],
            out_specs=pl.BlockSpec((TOP_K, DEST_TM), lambda i, s: (0, i)),
        ),
        out_shape=jax.ShapeDtypeStruct((TOP_K, t), jnp.int32),
        compiler_params=pltpu.CompilerParams(dimension_semantics=("parallel",)),
        name="moe_dest",
    )(start, idx, pos)


SC_WINDOW = 128


def _sc_mesh():
    return plsc.VectorSubcoreMesh(core_axis_name="c", subcore_axis_name="s")


def _sc_scatter_rows(x, dest):
    t = x.shape[0]

    @functools.partial(
        pl.kernel,
        out_type=jax.ShapeDtypeStruct((N_SORTED, D_QUARTER), x.dtype),
        mesh=_sc_mesh(),
        scratch_types=[],
    )
    def scatter(x_hbm, i_hbm, o_hbm):
        def body(x_vmem, i_vmem):
            pltpu.sync_copy(x_vmem, o_hbm.at[i_vmem.at[0]])

        pltpu.emit_pipeline(
            body,
            grid=(t // SC_WINDOW, TOP_K),
            in_specs=[pl.BlockSpec((SC_WINDOW, D_QUARTER), lambda i, k: (i, 0)),
                      pl.BlockSpec((1, SC_WINDOW), lambda i, k: (k, i))],
            out_specs=[],
            core_axis_name=("c", "s"),
            dimension_semantics=(pltpu.PARALLEL, pltpu.ARBITRARY),
        )(x_hbm, i_hbm)

    return scatter(x, dest)


def _sc_gather_rows(table, idx):
    m = idx.shape[1]

    @functools.partial(
        pl.kernel,
        out_type=jax.ShapeDtypeStruct((m, D_QUARTER), table.dtype),
        mesh=_sc_mesh(),
        scratch_types=[],
    )
    def gather(t_hbm, i_hbm, o_hbm):
        def body(i_vmem, o_vmem):
            pltpu.sync_copy(t_hbm.at[i_vmem.at[0]], o_vmem)

        pltpu.emit_pipeline(
            body,
            grid=(m // SC_WINDOW,),
            in_specs=[pl.BlockSpec((1, SC_WINDOW), lambda i: (0, i))],
            out_specs=[pl.BlockSpec((SC_WINDOW, D_QUARTER), lambda i: (i, 0))],
            core_axis_name=("c", "s"),
            dimension_semantics=(pltpu.PARALLEL,),
        )(i_hbm, o_hbm)

    return gather(table, idx)


def _ffn_kernel(te_ref, tv_ref, nu_ref, xa_ref, xb_ref, wg_ref, wu_ref, wd_ref, ya_ref, yb_ref):
    i = pl.program_id(0)

    @pl.when(i < nu_ref[0])
    def _():
        row = lax.broadcasted_iota(jnp.int32, xa_ref.shape, 0)
        live = row < tv_ref[i]
        hi_a, lo_a = _unpack_rows(jnp.where(live, xa_ref[...], jnp.uint32(0)))
        hi_b, lo_b = _unpack_rows(jnp.where(live, xb_ref[...], jnp.uint32(0)))
        xb = jnp.concatenate([hi_a.astype(BF16), hi_b.astype(BF16), lo_a.astype(BF16), lo_b.astype(BF16)], axis=1)
        g = _dot(xb, wg_ref[0, 0].astype(BF16))
        u = _dot(xb, wu_ref[0, 0].astype(BF16))
        a = (_silu(g) * u).astype(BF16)
        words = _pack_rows(_dot(a, wd_ref[0, 0].astype(BF16)))
        ya_ref[...] = words[:, :D_QUARTER]
        yb_ref[...] = words[:, D_QUARTER:]

    @pl.when(i >= nu_ref[0])
    def _():
        ya_ref[...] = jnp.zeros(ya_ref.shape, jnp.uint32)
        yb_ref[...] = jnp.zeros(yb_ref.shape, jnp.uint32)


def _expert_ffn(xs_a, xs_b, tile_expert, tile_valid, n_used, wg, wu, wd, layer):
    def row_map(i, te, tv, nu):
        return (jnp.minimum(i, nu[0] - 1), 0)

    def out_map(i, te, tv, nu):
        return (i, 0)

    def w_map(i, te, tv, nu):
        return (layer, te[jnp.minimum(i, nu[0] - 1)], 0, 0)

    return pl.pallas_call(
        _ffn_kernel,
        grid_spec=pltpu.PrefetchScalarGridSpec(
            num_scalar_prefetch=3,
            grid=(N_TILES,),
            in_specs=[
                pl.BlockSpec((FFN_TM, D_QUARTER), row_map),
                pl.BlockSpec((FFN_TM, D_QUARTER), row_map),
                pl.BlockSpec((1, 1, D_MODEL, D_EXPERT), w_map),
                pl.BlockSpec((1, 1, D_MODEL, D_EXPERT), w_map),
                pl.BlockSpec((1, 1, D_EXPERT, D_MODEL), w_map),
            ],
            out_specs=[pl.BlockSpec((FFN_TM, D_QUARTER), out_map), pl.BlockSpec((FFN_TM, D_QUARTER), out_map)],
        ),
        out_shape=[jax.ShapeDtypeStruct((N_SORTED, D_QUARTER), jnp.uint32),
                   jax.ShapeDtypeStruct((N_SORTED, D_QUARTER), jnp.uint32)],
        compiler_params=pltpu.CompilerParams(dimension_semantics=("arbitrary",)),
        name="moe_expert_ffn",
    )(tile_expert, tile_valid, n_used, xs_a, xs_b, wg, wu, wd)


COMBINE_TM = 256


def _combine_kernel(x_ref, mods_ref, wcol_ref, ga_ref, gb_ref, sg_ref, su_ref, sd_ref, g_ref, b_ref, o_ref):
    m = mods_ref[0]
    sf, cf, gf = m[3:4], m[4:5], m[5:6]
    x = x_ref[...]
    hc = (x * (1.0 + cf) + sf).astype(BF16)
    a = _silu(_dot(hc, sg_ref[...])) * _dot(hc, su_ref[...])
    y = _dot(a.astype(BF16), sd_ref[...])
    wcol = wcol_ref[...]
    parts = [y[:, q * D_QUARTER:(q + 1) * D_QUARTER] for q in range(4)]
    for k in range(TOP_K):
        hi_a, lo_a = _unpack_rows(ga_ref[k])
        hi_b, lo_b = _unpack_rows(gb_ref[k])
        wk = wcol[:, k:k + 1]
        parts = [parts[0] + wk * hi_a, parts[1] + wk * hi_b, parts[2] + wk * lo_a, parts[3] + wk * lo_b]
    y = jnp.concatenate(parts, axis=1)
    o_ref[...] = _layer_norm(ALPHA * x + gf * y, g_ref[...], b_ref[...])


def _combine(x, mods, wcol, g_a, g_b, sg, su, sd, ln_g, ln_b):
    t = x.shape[0]
    tm = COMBINE_TM
    full = lambda shape: pl.BlockSpec(shape, lambda i: (0,) * len(shape))
    return pl.pallas_call(
        _combine_kernel,
        grid=(t // tm,),
        in_specs=[
            pl.BlockSpec((tm, D_MODEL), lambda i: (i, 0)),
            pl.BlockSpec((1, 6, D_MODEL), lambda i: (i * tm // GROUP_ROWS, 0, 0)),
            pl.BlockSpec((tm, LANES), lambda i: (i, 0)),
            pl.BlockSpec((TOP_K, tm, D_QUARTER), lambda i: (0, i, 0)),
            pl.BlockSpec((TOP_K, tm, D_QUARTER), lambda i: (0, i, 0)),
            full((D_MODEL, D_SHARED)), full((D_MODEL, D_SHARED)), full((D_SHARED, D_MODEL)),
            full((1, D_MODEL)), full((1, D_MODEL)),
        ],
        out_specs=pl.BlockSpec((tm, D_MODEL), lambda i: (i, 0)),
        out_shape=jax.ShapeDtypeStruct((t, D_MODEL), F32),
        compiler_params=pltpu.CompilerParams(dimension_semantics=("parallel",)),
        name="moe_combine",
    )(x, mods, wcol, g_a, g_b, sg, su, sd, ln_g, ln_b)


MLA_TM = 512


def _mla_proj_kernel(x_ref, mods_ref, wdq_ref, wdkv_ref, wkr_ref, qn_ref, kvn_ref, wuq_ref,
                     wukn_ref, wuv_ref, ta_ref, tb_ref, q_ref, k_ref, v_ref, ckv_ref, kr_ref):
    m = mods_ref[0]
    sm, cm = m[0:1], m[1:2]
    h = (x_ref[...] * (1.0 + cm) + sm).astype(BF16)
    cq = _rms_norm(_dot(h, wdq_ref[...]), qn_ref[...])
    ckv = _rms_norm(_dot(h, wdkv_ref[...]), kvn_ref[...])
    kr2 = _dot(h, wkr_ref[...])
    ckv_ref[...] = ckv
    kr_ref[...] = kr2

    ka = ta_ref[0]
    kb = tb_ref[0]
    tm = ka.shape[0]
    ta = jnp.concatenate([jnp.full((tm, QK_NOPE), Q_PRESCALE, F32), ka * Q_PRESCALE], axis=1)
    tb = jnp.concatenate([jnp.zeros((tm, QK_NOPE), F32), kb * Q_PRESCALE], axis=1)
    krr = kr2 * ka + pltpu.roll(kr2, QK_ROPE, 1) * kb

    qpre = _dot(cq.astype(BF16), wuq_ref[...])
    ckv_b = ckv.astype(BF16)
    kn = _dot(ckv_b, wukn_ref[...])
    v_ref[...] = _dot(ckv_b, wuv_ref[...]).astype(BF16)
    for hd in range(N_HEADS):
        qh = qpre[:, hd * HEAD_PAD:(hd + 1) * HEAD_PAD]
        qrot = qh * ta + pltpu.roll(qh, HEAD_PAD - QK_ROPE, 1) * tb
        q_ref[:, hd * HEAD_PAD:(hd + 1) * HEAD_PAD] = qrot.astype(BF16)
        k_ref[:, hd * HEAD_PAD:hd * HEAD_PAD + QK_NOPE] = kn[:, hd * QK_NOPE:(hd + 1) * QK_NOPE].astype(BF16)
        k_ref[:, hd * HEAD_PAD + QK_NOPE:(hd + 1) * HEAD_PAD] = krr.astype(BF16)


def _mla_proj(x, mods, p, rope_a, rope_b):
    t = x.shape[0]
    tm = MLA_TM
    full = lambda shape: pl.BlockSpec(shape, lambda i: (0,) * len(shape))
    rope_spec = pl.BlockSpec(
        (1, tm, 2 * QK_ROPE),
        lambda i: (jnp.minimum(i * tm // GROUP_ROWS, 1), (i * tm % GROUP_ROWS) // tm, 0))
    return pl.pallas_call(
        _mla_proj_kernel,
        grid=(t // tm,),
        in_specs=[
            pl.BlockSpec((tm, D_MODEL), lambda i: (i, 0)),
            pl.BlockSpec((1, 6, D_MODEL), lambda i: (i * tm // GROUP_ROWS, 0, 0)),
            full((D_MODEL, Q_LORA)), full((D_MODEL, KV_LORA)), full((D_MODEL, 2 * QK_ROPE)),
            full((1, Q_LORA)), full((1, KV_LORA)),
            full((Q_LORA, N_HEADS * HEAD_PAD)),
            full((KV_LORA, N_HEADS * QK_NOPE)), full((KV_LORA, N_HEADS * V_DIM)),
            rope_spec, rope_spec,
        ],
        out_specs=[
            pl.BlockSpec((tm, N_HEADS * HEAD_PAD), lambda i: (i, 0)),
            pl.BlockSpec((tm, N_HEADS * HEAD_PAD), lambda i: (i, 0)),
            pl.BlockSpec((tm, N_HEADS * V_DIM), lambda i: (i, 0)),
            pl.BlockSpec((tm, KV_LORA), lambda i: (i, 0)),
            pl.BlockSpec((tm, 2 * QK_ROPE), lambda i: (i, 0)),
        ],
        out_shape=[
            jax.ShapeDtypeStruct((t, N_HEADS * HEAD_PAD), BF16),
            jax.ShapeDtypeStruct((t, N_HEADS * HEAD_PAD), BF16),
            jax.ShapeDtypeStruct((t, N_HEADS * V_DIM), BF16),
            jax.ShapeDtypeStruct((t, KV_LORA), F32),
            jax.ShapeDtypeStruct((t, 2 * QK_ROPE), F32),
        ],
        compiler_params=pltpu.CompilerParams(dimension_semantics=("parallel",)),
        name="mla_proj",
    )(x, mods, p["w_dq"], p["w_dkv"], p["w_kr"], p["q_norm"], p["kv_norm"], p["w_uq"],
      p["w_ukn"], p["w_uv"], rope_a, rope_b)


def _cache_kv_kernel(ckv_ref, kr_ref, wukn_ref, wuv_ref, k_ref, v_ref):
    ckv_b = ckv_ref[...].astype(BF16)
    kn = _dot(ckv_b, wukn_ref[...])
    v_ref[...] = _dot(ckv_b, wuv_ref[...]).astype(BF16)
    kr = kr_ref[...].astype(BF16)
    for hd in range(N_HEADS):
        k_ref[:, hd * HEAD_PAD:hd * HEAD_PAD + QK_NOPE] = kn[:, hd * QK_NOPE:(hd + 1) * QK_NOPE].astype(BF16)
        k_ref[:, hd * HEAD_PAD + QK_NOPE:(hd + 1) * HEAD_PAD] = kr


def _cache_kv(ckv, kr_pad, p):
    t = ckv.shape[0]
    tm = PAST_LEN
    full = lambda shape: pl.BlockSpec(shape, lambda i: (0,) * len(shape))
    return pl.pallas_call(
        _cache_kv_kernel,
        grid=(t // tm,),
        in_specs=[
            pl.BlockSpec((tm, KV_LORA), lambda i: (i, 0)),
            pl.BlockSpec((tm, 2 * QK_ROPE), lambda i: (i, 0)),
            full((KV_LORA, N_HEADS * QK_NOPE)), full((KV_LORA, N_HEADS * V_DIM)),
        ],
        out_specs=[
            pl.BlockSpec((tm, N_HEADS * HEAD_PAD), lambda i: (i, 0)),
            pl.BlockSpec((tm, N_HEADS * V_DIM), lambda i: (i, 0)),
        ],
        out_shape=[
            jax.ShapeDtypeStruct((t, N_HEADS * HEAD_PAD), BF16),
            jax.ShapeDtypeStruct((t, N_HEADS * V_DIM), BF16),
        ],
        compiler_params=pltpu.CompilerParams(dimension_semantics=("parallel",)),
        name="mla_cache_kv",
    )(ckv, kr_pad, p["w_ukn"], p["w_uv"])


def _ctx_attn_kernel(q_ref, k_ref, v_ref, o_ref):
    for hd in range(N_HEADS):
        q = q_ref[:, hd * HEAD_PAD:(hd + 1) * HEAD_PAD]
        k = k_ref[:, hd * HEAD_PAD:(hd + 1) * HEAD_PAD]
        s = _dot_nt(q, k)
        s = s - jnp.max(s, axis=-1, keepdims=True)
        p = jnp.exp2(s)
        p = p / jnp.sum(p, axis=-1, keepdims=True)
        o = _dot(p.astype(BF16), v_ref[:, hd * V_DIM:(hd + 1) * V_DIM])
        o_ref[:, hd * V_DIM:(hd + 1) * V_DIM] = o.astype(BF16)


def _ctx_attention(q, k, v):
    return pl.pallas_call(
        _ctx_attn_kernel,
        grid=(BATCH,),
        in_specs=[
            pl.BlockSpec((SEQ, N_HEADS * HEAD_PAD), lambda b: (b, 0)),
            pl.BlockSpec((SEQ, N_HEADS * HEAD_PAD), lambda b: (b, 0)),
            pl.BlockSpec((SEQ, N_HEADS * V_DIM), lambda b: (b, 0)),
        ],
        out_specs=pl.BlockSpec((SEQ, N_HEADS * V_DIM), lambda b: (b, 0)),
        out_shape=jax.ShapeDtypeStruct((T_CTX, N_HEADS * V_DIM), BF16),
        compiler_params=pltpu.CompilerParams(dimension_semantics=("parallel",)),
        name="ctx_attention",
    )(q, k, v)


LAT_TQ = 512
LAT_TK = 512


def _lat_attn_kernel(q_ref, k_ref, v_ref, kc_ref, vc_ref, o_ref, s_ref, p_ref, m_ref):
    n_lat = DEC_SEQ // LAT_TK
    th = q_ref.shape[0] // 2
    halves = [pl.ds(0, th), pl.ds(th, th)]
    qs = [q_ref[r, :] for r in halves]

    mparts = [jnp.full((th, LANES), NEG_INF, F32) for _ in halves]
    for c in range(n_lat + 1):
        k = kc_ref[...] if c == n_lat else k_ref[c * LAT_TK:(c + 1) * LAT_TK, :]
        for hf, r in enumerate(halves):
            s = _dot_nt(qs[hf], k)
            s_ref[r, c * LAT_TK:(c + 1) * LAT_TK] = s
            for j in range(LAT_TK // LANES):
                mparts[hf] = jnp.maximum(mparts[hf], s[:, j * LANES:(j + 1) * LANES])
    for hf, r in enumerate(halves):
        m_ref[r, :] = jnp.broadcast_to(jnp.max(mparts[hf], axis=-1, keepdims=True), (th, LANES))

    lparts = [jnp.zeros((th, LANES), F32) for _ in halves]
    accs = [jnp.zeros((th, V_DIM), F32) for _ in halves]
    for c in range(n_lat + 1):
        v = vc_ref[...] if c == n_lat else v_ref[c * LAT_TK:(c + 1) * LAT_TK, :]
        for hf, r in enumerate(halves):
            for j in range(c * LAT_TK // LANES, (c + 1) * LAT_TK // LANES):
                p = jnp.exp2(s_ref[r, j * LANES:(j + 1) * LANES] - m_ref[r, :])
                lparts[hf] = lparts[hf] + p
                p_ref[r, j * LANES:(j + 1) * LANES] = p.astype(BF16)
            accs[hf] = accs[hf] + _dot(p_ref[r, c * LAT_TK:(c + 1) * LAT_TK], v)
    for hf, r in enumerate(halves):
        o_ref[r, :] = (accs[hf] / jnp.sum(lparts[hf], axis=-1, keepdims=True)).astype(BF16)


def _lat_attention(q, k, v, kc, vc):
    nq = DEC_SEQ // LAT_TQ
    return pl.pallas_call(
        _lat_attn_kernel,
        grid=(DEC_BATCH, N_HEADS, nq),
        in_specs=[
            pl.BlockSpec((LAT_TQ, HEAD_PAD), lambda b, h, i: ((b + 1) * nq + i, h)),
            pl.BlockSpec((DEC_SEQ, HEAD_PAD), lambda b, h, i: (b + 1, h)),
            pl.BlockSpec((DEC_SEQ, V_DIM), lambda b, h, i: (b + 1, h)),
            pl.BlockSpec((PAST_LEN, HEAD_PAD), lambda b, h, i: (b, h)),
            pl.BlockSpec((PAST_LEN, V_DIM), lambda b, h, i: (b, h)),
        ],
        out_specs=pl.BlockSpec((LAT_TQ, V_DIM), lambda b, h, i: (b * nq + i, h)),
        out_shape=jax.ShapeDtypeStruct((T_LAT, N_HEADS * V_DIM), BF16),
        scratch_shapes=[pltpu.VMEM((LAT_TQ, DEC_SEQ + PAST_LEN), F32),
                        pltpu.VMEM((LAT_TQ, DEC_SEQ + PAST_LEN), BF16),
                        pltpu.VMEM((LAT_TQ, LANES), F32)],
        compiler_params=pltpu.CompilerParams(
            dimension_semantics=("parallel", "parallel", "parallel")),
        name="lat_attention",
    )(q, k, v, kc, vc)


OPROJ_TM = 512


def _oproj_kernel(oc_ref, ol_ref, x_ref, mods_ref, wo_ref, g_ref, b_ref, out_ref):
    is_ctx = pl.program_id(0) < T_CTX // OPROJ_TM
    o = jnp.where(is_ctx, oc_ref[...], ol_ref[...])
    gm = mods_ref[0][2:3]
    y = _dot(o, wo_ref[...])
    out_ref[...] = _layer_norm(ALPHA * x_ref[...] + gm * y, g_ref[...], b_ref[...])


def _oproj(o_ctx, o_lat, x, mods, w_o, ln_g, ln_b):
    t = x.shape[0]
    tm = OPROJ_TM
    n_ctx = T_CTX // tm
    return pl.pallas_call(
        _oproj_kernel,
        grid=(t // tm,),
        in_specs=[
            pl.BlockSpec((tm, N_HEADS * V_DIM), lambda i: (jnp.minimum(i, n_ctx - 1), 0)),
            pl.BlockSpec((tm, N_HEADS * V_DIM), lambda i: (jnp.maximum(i - n_ctx, 0), 0)),
            pl.BlockSpec((tm, D_MODEL), lambda i: (i, 0)),
            pl.BlockSpec((1, 6, D_MODEL), lambda i: (i * tm // GROUP_ROWS, 0, 0)),
            pl.BlockSpec((N_HEADS * V_DIM, D_MODEL), lambda i: (0, 0)),
            pl.BlockSpec((1, D_MODEL), lambda i: (0, 0)),
            pl.BlockSpec((1, D_MODEL), lambda i: (0, 0)),
        ],
        out_specs=pl.BlockSpec((tm, D_MODEL), lambda i: (i, 0)),
        out_shape=jax.ShapeDtypeStruct((t, D_MODEL), F32),
        compiler_params=pltpu.CompilerParams(dimension_semantics=("parallel",)),
        name="attn_oproj",
    )(o_ctx, o_lat, x, mods, w_o, ln_g, ln_b)


def _swap16(w):
    q = QK_ROPE // 4
    return jnp.concatenate([w[..., q:2 * q], w[..., :q], w[..., 3 * q:], w[..., 2 * q:3 * q]], axis=-1)


def _mla_params(w_dqkv, q_norm, w_uq, kv_norm, w_ukv):
    w_kr = w_dqkv[:, Q_LORA + KV_LORA:]
    wq = w_uq.reshape(Q_LORA, N_HEADS, QK_NOPE + QK_ROPE)
    wq_r = wq[..., QK_NOPE:]
    wq = jnp.concatenate([wq[..., :QK_NOPE], wq_r, _swap16(wq_r)], axis=-1)
    wkv = w_ukv.reshape(KV_LORA, N_HEADS, QK_NOPE + V_DIM)
    return {
        "w_dq": w_dqkv[:, :Q_LORA].astype(BF16),
        "w_dkv": w_dqkv[:, Q_LORA:Q_LORA + KV_LORA].astype(BF16),
        "w_kr": jnp.concatenate([w_kr, _swap16(w_kr)], axis=-1).astype(BF16),
        "q_norm": q_norm.reshape(1, Q_LORA),
        "kv_norm": kv_norm.reshape(1, KV_LORA),
        "w_uq": wq.reshape(Q_LORA, N_HEADS * HEAD_PAD).astype(BF16),
        "w_ukn": wkv[..., :QK_NOPE].reshape(KV_LORA, N_HEADS * QK_NOPE).astype(BF16),
        "w_uv": wkv[..., QK_NOPE:].reshape(KV_LORA, N_HEADS * V_DIM).astype(BF16),
    }


def _rope_tables():
    nf = QK_ROPE // 4
    t = np.arange(DEC_SEQ)
    row = (t // GRID_W).astype(np.float32)
    col = (t % GRID_W).astype(np.float32)
    inv = (ROPE_THETA ** (-np.arange(nf, dtype=np.float32) / nf)).astype(np.float32)
    ar, ac = row[:, None] * inv, col[:, None] * inv
    pad = np.zeros((DEC_SEQ, QK_ROPE), np.float32)
    cos = np.concatenate([np.cos(ar), np.cos(ar), np.cos(ac), np.cos(ac), pad], axis=-1)
    sin = np.concatenate([-np.sin(ar), np.sin(ar), -np.sin(ac), np.sin(ac), pad], axis=-1)
    cos_id = np.concatenate([np.ones((DEC_SEQ, QK_ROPE), np.float32), pad], axis=-1)
    sin_id = np.zeros((DEC_SEQ, 2 * QK_ROPE), np.float32)
    return (jnp.asarray(np.stack([cos_id, cos]).astype(np.float32)),
            jnp.asarray(np.stack([sin_id, sin]).astype(np.float32)))


def kernel(x_prompt, x_sample, cache_ckv, cache_krope, c, c_ctx, ada_w, ada_b, ln_g, ln_b, conv_w_in, conv_k, conv_w_out, mla_w_dqkv, mla_q_norm, mla_w_uq, mla_kv_norm, mla_w_ukv, mla_w_o, router_w, router_bias, exp_w_gate, exp_w_up, exp_w_down, sh_w_gate, sh_w_up, sh_w_down):
    x = jnp.concatenate([x_prompt.reshape(T_CTX, D_MODEL), x_sample.reshape(T_LAT, D_MODEL)], axis=0)
    cvecs = jnp.concatenate([c_ctx[None, :], c, jnp.zeros((SUBLANES - N_GROUPS_ROWS, D_MODEL), F32)], axis=0)
    mods = _adaln(cvecs, ada_w, ada_b)

    def ln(l, k):
        return ln_g[l, k].reshape(1, D_MODEL), ln_b[l, k].reshape(1, D_MODEL)

    def moe_layer(xin, l):
        idx, pos, wcol, counts, hp_a, hp_b = _router(xin, mods[l], router_w[l].T,
                                                     router_bias[l].reshape(N_EXPERTS, 1))
        cnt = counts[:, 0].astype(jnp.int32)
        start, tile_expert, tile_valid, n_used = _expert_layout(cnt)
        dest = _dest_rows(start, idx, pos)
        xs_a = _sc_scatter_rows(hp_a, dest)
        xs_b = _sc_scatter_rows(hp_b, dest)
        ys_a, ys_b = _expert_ffn(xs_a, xs_b, tile_expert, tile_valid, n_used, exp_w_gate, exp_w_up, exp_w_down, l)
        dest_row = dest.reshape(1, N_SLOTS)
        g_a = _sc_gather_rows(ys_a, dest_row).reshape(TOP_K, T_ALL, D_QUARTER)
        g_b = _sc_gather_rows(ys_b, dest_row).reshape(TOP_K, T_ALL, D_QUARTER)
        return _combine(xin, mods[l], wcol, g_a, g_b, sh_w_gate[l].astype(BF16), sh_w_up[l].astype(BF16),
                        sh_w_down[l].astype(BF16), *ln(l, 1))

    x = _conv_mixer(x, mods[0], conv_w_in[0].astype(BF16), conv_k[0], conv_w_out[0].astype(BF16), *ln(0, 0))
    x = moe_layer(x, 0)

    p = _mla_params(mla_w_dqkv[0], mla_q_norm[0], mla_w_uq[0], mla_kv_norm[0], mla_w_ukv[0])
    rope_a, rope_b = _rope_tables()
    q, k, v, ckv, kr = _mla_proj(x, mods[1], p, rope_a, rope_b)
    kr_cache = jnp.concatenate([cache_krope[:, 0].reshape(DEC_BATCH * PAST_LEN, QK_ROPE),
                                jnp.zeros((DEC_BATCH * PAST_LEN, QK_ROPE), F32)], axis=-1)
    kc, vc = _cache_kv(cache_ckv[:, 0].reshape(DEC_BATCH * PAST_LEN, KV_LORA), kr_cache, p)
    o_ctx = _ctx_attention(q, k, v)
    o_lat = _lat_attention(q, k, v, kc, vc)
    x = _oproj(o_ctx, o_lat, x, mods[1], mla_w_o[0].astype(BF16), *ln(1, 0))
    x = moe_layer(x, 1)

    y_prompt = x[:T_CTX].reshape(BATCH, SEQ, D_MODEL)
    y_sample = x[T_CTX:].reshape(DEC_BATCH, DEC_SEQ, D_MODEL)
    state_ckv = ckv[:T_CTX].reshape(BATCH, 1, SEQ, KV_LORA)
    state_krope = kr[:T_CTX, :QK_ROPE].reshape(BATCH, 1, SEQ, QK_ROPE)
    return (y_prompt, y_sample, state_ckv, state_krope)
```

```python
import functools
import math

import jax
import jax.numpy as jnp
import numpy as np
from jax import lax
from jax.experimental import pallas as pl
from jax.experimental.pallas import tpu as pltpu
from jax.experimental.pallas import tpu_sc as plsc

D_MODEL = 1024
BATCH = 16
SEQ = 256
DEPTH = 2
DEC_BATCH = 4
DEC_SEQ = 4096
PAST_LEN = 512
GRID_W = 64

N_HEADS = 8
QK_NOPE = 128
QK_ROPE = 64
V_DIM = 128
Q_LORA = 384
KV_LORA = 256
ROPE_THETA = 10000.0
ATTN_SCALE = (QK_NOPE + QK_ROPE) ** -0.5
HEAD_PAD = 256
Q_PRESCALE = ATTN_SCALE * math.log2(math.e)

N_EXPERTS = 64
TOP_K = 8
N_GROUPS = 8
TOPK_GROUPS = 4
GROUP_SIZE = N_EXPERTS // N_GROUPS
D_EXPERT = 256
D_SHARED = 256
ROUTED_SCALE = 2.5

ALPHA = (2 * DEPTH) ** 0.25
LN_EPS = 1e-5
RMS_EPS = 1e-6

GROUP_ROWS = 4096
N_GROUPS_ROWS = 1 + DEC_BATCH
T_CTX = BATCH * SEQ
T_LAT = DEC_BATCH * DEC_SEQ
T_ALL = T_CTX + T_LAT
LANES = 128
SUBLANES = 8

F32 = jnp.float32
BF16 = jnp.bfloat16
NEG_INF = float("-inf")


def _dot(a, b):
    return jnp.dot(a, b, preferred_element_type=F32)


def _dot_nt(a, b, precision=None):
    return lax.dot_general(a, b, (((1,), (1,)), ((), ())), precision=precision,
                           preferred_element_type=F32)


def _layer_norm(v, g, b):
    mu = jnp.mean(v, axis=-1, keepdims=True)
    d = v - mu
    var = jnp.mean(d * d, axis=-1, keepdims=True)
    return d * lax.rsqrt(var + LN_EPS) * g + b


def _rms_norm(v, g):
    return v * lax.rsqrt(jnp.mean(v * v, axis=-1, keepdims=True) + RMS_EPS) * g


def _silu(v):
    return v / (1.0 + jnp.exp(-v))


def _sigmoid(v):
    return 1.0 / (1.0 + jnp.exp(-v))


def _adaln_kernel(c_ref, w_ref, b_ref, o_ref):
    c = c_ref[...]
    s = _silu(c)
    o_ref[0, 0] = jnp.dot(s, w_ref[0], precision=lax.Precision.HIGHEST,
                          preferred_element_type=F32) + b_ref[0]


def _adaln(cvecs, ada_w, ada_b):
    out = pl.pallas_call(
        _adaln_kernel,
        grid=(DEPTH, 6),
        in_specs=[
            pl.BlockSpec((SUBLANES, D_MODEL), lambda l, j: (0, 0)),
            pl.BlockSpec((1, D_MODEL, D_MODEL), lambda l, j: (l, 0, j)),
            pl.BlockSpec((1, 1, D_MODEL), lambda l, j: (l, 0, j)),
        ],
        out_specs=pl.BlockSpec((1, 1, SUBLANES, D_MODEL), lambda l, j: (l, j, 0, 0)),
        out_shape=jax.ShapeDtypeStruct((DEPTH, 6, SUBLANES, D_MODEL), F32),
        compiler_params=pltpu.CompilerParams(dimension_semantics=("parallel", "parallel")),
        name="adaln",
    )(cvecs, ada_w, ada_b.reshape(DEPTH, 1, 6 * D_MODEL))
    return jnp.transpose(out[:, :, :N_GROUPS_ROWS, :], (0, 2, 1, 3))


CONV_TM = 512


def _conv_kernel(xc_ref, xcp_ref, xcn_ref, xl_ref, xlp_ref, xln_ref, mods_ref, win_ref, ck_ref, wout_ref,
                 g_ref, b_ref, o_ref):
    i = pl.program_id(0)
    tm = xc_ref.shape[0]
    is_ctx = i < T_CTX // tm
    m = mods_ref[0]
    sm, cm, gm = m[0:1], m[1:2], m[2:3]
    x = jnp.where(is_ctx, xc_ref[...], xl_ref[...])
    xp = jnp.where(is_ctx, xcp_ref[...], xlp_ref[...])
    xn = jnp.where(is_ctx, xcn_ref[...], xln_ref[...])
    h = (x * (1.0 + cm) + sm).astype(BF16)
    z = _dot(h, win_ref[...])
    bg = z[:, :D_MODEL]
    u = z[:, D_MODEL:2 * D_MODEL] * z[:, 2 * D_MODEL:]

    def halo_u(xh):
        hh = (xh * (1.0 + cm) + sm).astype(BF16)
        zh = _dot(hh, win_ref[:, D_MODEL:])
        return zh[:, :D_MODEL] * zh[:, D_MODEL:]

    u_prev = halo_u(xp)[SUBLANES - 1:SUBLANES]
    u_next = halo_u(xn)[0:1]

    row = lax.broadcasted_iota(jnp.int32, (tm, 1), 0)
    grow = i * tm + row
    seq_len = jnp.where(grow < T_CTX, SEQ, DEC_SEQ)
    pos = jnp.bitwise_and(grow, seq_len - 1)
    is_first = pos == 0
    is_last = pos == seq_len - 1

    left = pltpu.roll(u, 1, 0)
    left = jnp.where(row == 0, u_prev, left)
    left = jnp.where(is_first, 0.0, left)
    right = pltpu.roll(u, tm - 1, 0)
    right = jnp.where(row == tm - 1, u_next, right)
    right = jnp.where(is_last, 0.0, right)

    ck = ck_ref[...]
    conv = left * ck[0:1] + u * ck[1:2] + right * ck[2:3]
    v = (bg * conv).astype(BF16)
    y = _dot(v, wout_ref[...])
    o_ref[...] = _layer_norm(ALPHA * x + gm * y, g_ref[...], b_ref[...])


def _conv_mixer(x_ctx, x_lat, mods, w_in, conv_k, w_out, ln_g, ln_b):
    tm = CONV_TM
    per8 = tm // SUBLANES
    n_ctx = T_CTX // tm
    n_lat = T_LAT // tm

    def stream_specs(first, n):
        def blk(i):
            return jnp.clip(i - first, 0, n - 1)
        return [
            pl.BlockSpec((tm, D_MODEL), lambda i: (blk(i), 0)),
            pl.BlockSpec((SUBLANES, D_MODEL), lambda i: (jnp.maximum(blk(i) * per8 - 1, 0), 0)),
            pl.BlockSpec((SUBLANES, D_MODEL), lambda i: (jnp.minimum((blk(i) + 1) * per8, n * per8 - 1), 0)),
        ]

    return pl.pallas_call(
        _conv_kernel,
        grid=(n_ctx + n_lat,),
        in_specs=stream_specs(0, n_ctx) + stream_specs(n_ctx, n_lat) + [
            pl.BlockSpec((1, 6, D_MODEL), lambda i: (i * tm // GROUP_ROWS, 0, 0)),
            pl.BlockSpec((D_MODEL, 3 * D_MODEL), lambda i: (0, 0)),
            pl.BlockSpec((3, D_MODEL), lambda i: (0, 0)),
            pl.BlockSpec((D_MODEL, D_MODEL), lambda i: (0, 0)),
            pl.BlockSpec((1, D_MODEL), lambda i: (0, 0)),
            pl.BlockSpec((1, D_MODEL), lambda i: (0, 0)),
        ],
        out_specs=pl.BlockSpec((tm, D_MODEL), lambda i: (i, 0)),
        out_shape=jax.ShapeDtypeStruct((T_ALL, D_MODEL), F32),
        compiler_params=pltpu.CompilerParams(dimension_semantics=("parallel",)),
        name="conv_mixer",
    )(x_ctx, x_ctx, x_ctx, x_lat, x_lat, x_lat, mods, w_in, conv_k, w_out, ln_g, ln_b)


ROUTER_TM = 512


def _first_argmax_mask(cur, ridx, n):
    mx = jnp.max(cur, axis=0, keepdims=True)
    first = jnp.min(jnp.where(cur == mx, ridx, n), axis=0, keepdims=True)
    return ridx == first, mx


def _router_kernel(x_ref, mods_ref, rwt_ref, bias_ref, idx_ref, pos_ref, wcol_ref, count_ref, hpa_ref, hpb_ref,
                   carry_ref):
    tm = x_ref.shape[0]
    m = mods_ref[0]
    sf, cf = m[3:4], m[4:5]
    hc = x_ref[...] * (1.0 + cf) + sf
    words = _pack_rows(hc)
    hpa_ref[...] = words[:, :D_QUARTER]
    hpb_ref[...] = words[:, D_QUARTER:]
    logits = _dot_nt(rwt_ref[...], hc, precision=lax.Precision.HIGHEST)
    scores = _sigmoid(logits)
    biased = scores + bias_ref[...]

    ridx8 = lax.broadcasted_iota(jnp.int32, (GROUP_SIZE, tm), 0)
    gscore = jnp.full((N_GROUPS, tm), NEG_INF, F32)
    for g in range(N_GROUPS):
        blk = biased[g * GROUP_SIZE:(g + 1) * GROUP_SIZE]
        sel, m1 = _first_argmax_mask(blk, ridx8, GROUP_SIZE)
        m2 = jnp.max(jnp.where(sel, NEG_INF, blk), axis=0, keepdims=True)
        gscore = jnp.where(ridx8 == g, m1 + m2, gscore)

    gmask = jnp.zeros((N_GROUPS, tm), jnp.bool_)
    cur = gscore
    for _ in range(TOPK_GROUPS):
        sel, _unused = _first_argmax_mask(cur, ridx8, N_GROUPS)
        gmask = jnp.logical_or(gmask, sel)
        cur = jnp.where(sel, NEG_INF, cur)

    gmask_f = gmask.astype(F32)
    blocks = []
    for g in range(N_GROUPS):
        keep = jnp.broadcast_to(gmask_f[g:g + 1], (GROUP_SIZE, tm)) > 0.5
        blocks.append(jnp.where(keep, biased[g * GROUP_SIZE:(g + 1) * GROUP_SIZE], NEG_INF))
    cur = jnp.concatenate(blocks, axis=0)

    @pl.when(pl.program_id(0) == 0)
    def _():
        carry_ref[...] = jnp.zeros(carry_ref.shape, F32)

    ridx = lax.broadcasted_iota(jnp.int32, (N_EXPERTS, tm), 0)
    kidx = lax.broadcasted_iota(jnp.int32, (TOP_K, tm), 0)
    sels = []
    chosen = jnp.zeros((N_EXPERTS, tm), jnp.bool_)
    idx_rows = jnp.zeros((TOP_K, tm), jnp.int32)
    for k in range(TOP_K):
        mx = jnp.max(cur, axis=0, keepdims=True)
        first = jnp.min(jnp.where(cur == mx, ridx, N_EXPERTS), axis=0, keepdims=True)
        sel = ridx == first
        sels.append(sel)
        chosen = jnp.logical_or(chosen, sel)
        idx_rows = jnp.where(kidx == k, first, idx_rows)
        cur = jnp.where(sel, NEG_INF, cur)

    onehot = chosen.astype(F32)
    t_row = lax.broadcasted_iota(jnp.int32, (tm, tm), 0)
    t_col = lax.broadcasted_iota(jnp.int32, (tm, tm), 1)
    before = (t_row < t_col).astype(BF16)
    rank = carry_ref[...] + _dot(onehot.astype(BF16), before)
    carry_ref[...] = carry_ref[...] + jnp.sum(onehot, axis=1, keepdims=True)
    count_ref[...] = jnp.broadcast_to(carry_ref[...], count_ref.shape)

    w = jnp.where(chosen, scores, 0.0)
    w = w / jnp.sum(w, axis=0, keepdims=True) * ROUTED_SCALE
    pos_rows = jnp.zeros((TOP_K, tm), F32)
    w_rows = jnp.zeros((TOP_K, tm), F32)
    for k in range(TOP_K):
        pos_rows = jnp.where(kidx == k, jnp.sum(jnp.where(sels[k], rank, 0.0), axis=0, keepdims=True), pos_rows)
        w_rows = jnp.where(kidx == k, jnp.sum(jnp.where(sels[k], w, 0.0), axis=0, keepdims=True), w_rows)
    idx_ref[...] = idx_rows
    pos_ref[...] = pos_rows.astype(jnp.int32)
    wpad = jnp.concatenate([w_rows, jnp.zeros((LANES - TOP_K, tm), F32)], axis=0)
    wcol_ref[...] = wpad.T


def _router(x, mods, router_wt, router_bias):
    t = x.shape[0]
    tm = ROUTER_TM
    return pl.pallas_call(
        _router_kernel,
        grid=(t // tm,),
        in_specs=[
            pl.BlockSpec((tm, D_MODEL), lambda i: (i, 0)),
            pl.BlockSpec((1, 6, D_MODEL), lambda i: (i * tm // GROUP_ROWS, 0, 0)),
            pl.BlockSpec((N_EXPERTS, D_MODEL), lambda i: (0, 0)),
            pl.BlockSpec((N_EXPERTS, 1), lambda i: (0, 0)),
        ],
        out_specs=[
            pl.BlockSpec((TOP_K, tm), lambda i: (0, i)),
            pl.BlockSpec((TOP_K, tm), lambda i: (0, i)),
            pl.BlockSpec((tm, LANES), lambda i: (i, 0)),
            pl.BlockSpec((N_EXPERTS, LANES), lambda i: (0, 0)),
            pl.BlockSpec((tm, D_QUARTER), lambda i: (i, 0)),
            pl.BlockSpec((tm, D_QUARTER), lambda i: (i, 0)),
        ],
        out_shape=[
            jax.ShapeDtypeStruct((TOP_K, t), jnp.int32),
            jax.ShapeDtypeStruct((TOP_K, t), jnp.int32),
            jax.ShapeDtypeStruct((t, LANES), F32),
            jax.ShapeDtypeStruct((N_EXPERTS, LANES), F32),
            jax.ShapeDtypeStruct((t, D_QUARTER), jnp.uint32),
            jax.ShapeDtypeStruct((t, D_QUARTER), jnp.uint32),
        ],
        scratch_shapes=[pltpu.VMEM((N_EXPERTS, 1), F32)],
        compiler_params=pltpu.CompilerParams(dimension_semantics=("arbitrary",)),
        name="moe_router",
    )(x, mods, router_wt, router_bias)


FFN_TM = 512
N_SLOTS = T_ALL * TOP_K
N_TILES = N_SLOTS // FFN_TM + N_EXPERTS
N_SORTED = N_TILES * FFN_TM


def _expert_layout(counts):
    padded = (counts + FFN_TM - 1) // FFN_TM * FFN_TM
    end = jnp.cumsum(padded)
    start = end - padded
    tile_row = jnp.arange(N_TILES, dtype=jnp.int32) * FFN_TM
    tile_expert = jnp.minimum(jnp.sum(end[None, :] <= tile_row[:, None], axis=1), N_EXPERTS - 1)
    tile_valid = jnp.clip(start[tile_expert] + counts[tile_expert] - tile_row, 0, FFN_TM)
    n_used = (end[-1] // FFN_TM).astype(jnp.int32).reshape(1)
    return start.astype(jnp.int32), tile_expert.astype(jnp.int32), tile_valid.astype(jnp.int32), n_used


D_HALF = D_MODEL // 2
D_QUARTER = D_MODEL // 4


def _pack_rows(v):
    hi = lax.bitcast_convert_type(v[:, :D_HALF].astype(BF16).astype(F32), jnp.uint32)
    lo = lax.bitcast_convert_type(v[:, D_HALF:].astype(BF16).astype(F32), jnp.uint32)
    return jnp.bitwise_or(hi, jnp.right_shift(lo, jnp.uint32(16)))


def _unpack_rows(w):
    hi = lax.bitcast_convert_type(jnp.bitwise_and(w, jnp.uint32(0xFFFF0000)), F32)
    lo = lax.bitcast_convert_type(jnp.left_shift(w, jnp.uint32(16)), F32)
    return hi, lo


DEST_TM = 2048


def _dest_kernel(start_ref, idx_ref, pos_ref, dest_ref):
    idx = idx_ref[...]
    base = jnp.zeros(idx.shape, jnp.int32)
    for e in range(N_EXPERTS):
        base = jnp.where(idx == e, start_ref[e], base)
    dest_ref[...] = base + pos_ref[...]


def _dest_rows(start, idx, pos):
    t = idx.shape[1]
    return pl.pallas_call(
        _dest_kernel,
        grid_spec=pltpu.PrefetchScalarGridSpec(
            num_scalar_prefetch=1,
            grid=(t // DEST_TM,),
            in_specs=[pl.BlockSpec((TOP_K, DEST_TM), lambda i, s: (0, i)),
                      pl.BlockSpec((TOP_K, DEST_TM), lambda i, s: (0, i))],
            out_specs=pl.BlockSpec((TOP_K, DEST_TM), lambda i, s: (0, i)),
        ),
        out_shape=jax.ShapeDtypeStruct((TOP_K, t), jnp.int32),
        compiler_params=pltpu.CompilerParams(dimension_semantics=("parallel",)),
        name="moe_dest",
    )(start, idx, pos)


SC_WINDOW = 128


def _sc_mesh():
    return plsc.VectorSubcoreMesh(core_axis_name="c", subcore_axis_name="s")


def _sc_scatter_rows(x, dest):
    t = x.shape[0]

    @functools.partial(
        pl.kernel,
        out_type=jax.ShapeDtypeStruct((N_SORTED, D_QUARTER), x.dtype),
        mesh=_sc_mesh(),
        scratch_types=[],
    )
    def scatter(x_hbm, i_hbm, o_hbm):
        def body(x_vmem, i_vmem):
            pltpu.sync_copy(x_vmem, o_hbm.at[i_vmem.at[0]])

        pltpu.emit_pipeline(
            body,
            grid=(t // SC_WINDOW, TOP_K),
            in_specs=[pl.BlockSpec((SC_WINDOW, D_QUARTER), lambda i, k: (i, 0)),
                      pl.BlockSpec((1, SC_WINDOW), lambda i, k: (k, i))],
            out_specs=[],
            core_axis_name=("c", "s"),
            dimension_semantics=(pltpu.PARALLEL, pltpu.ARBITRARY),
        )(x_hbm, i_hbm)

    return scatter(x, dest)


def _sc_gather_rows(table, idx):
    m = idx.shape[1]

    @functools.partial(
        pl.kernel,
        out_type=jax.ShapeDtypeStruct((m, D_QUARTER), table.dtype),
        mesh=_sc_mesh(),
        scratch_types=[],
    )
    def gather(t_hbm, i_hbm, o_hbm):
        def body(i_vmem, o_vmem):
            pltpu.sync_copy(t_hbm.at[i_vmem.at[0]], o_vmem)

        pltpu.emit_pipeline(
            body,
            grid=(m // SC_WINDOW,),
            in_specs=[pl.BlockSpec((1, SC_WINDOW), lambda i: (0, i))],
            out_specs=[pl.BlockSpec((SC_WINDOW, D_QUARTER), lambda i: (i, 0))],
            core_axis_name=("c", "s"),
            dimension_semantics=(pltpu.PARALLEL,),
        )(i_hbm, o_hbm)

    return gather(table, idx)


def _ffn_kernel(te_ref, tv_ref, nu_ref, xa_ref, xb_ref, wg_ref, wu_ref, wd_ref, ya_ref, yb_ref,
                wgb_ref, wub_ref, wdb_ref):
    i = pl.program_id(0)

    @pl.when(jnp.logical_and(i < nu_ref[0],
                             jnp.logical_or(i == 0, te_ref[i] != te_ref[jnp.maximum(i - 1, 0)])))
    def _():
        wgb_ref[...] = wg_ref[0, 0].astype(BF16)
        wub_ref[...] = wu_ref[0, 0].astype(BF16)
        wdb_ref[...] = wd_ref[0, 0].astype(BF16)

    @pl.when(i < nu_ref[0])
    def _():
        wg = wgb_ref[...]
        wu = wub_ref[...]
        wd = wdb_ref[...]
        th = FFN_TM // 2
        for r0 in (0, th):
            rows = pl.ds(r0, th)
            live = (lax.broadcasted_iota(jnp.int32, (th, D_QUARTER), 0) + r0) < tv_ref[i]
            hi_a, lo_a = _unpack_rows(jnp.where(live, xa_ref[rows, :], jnp.uint32(0)))
            hi_b, lo_b = _unpack_rows(jnp.where(live, xb_ref[rows, :], jnp.uint32(0)))
            xb = jnp.concatenate([hi_a.astype(BF16), hi_b.astype(BF16), lo_a.astype(BF16), lo_b.astype(BF16)],
                                 axis=1)
            a = (_silu(_dot(xb, wg)) * _dot(xb, wu)).astype(BF16)
            words = _pack_rows(_dot(a, wd))
            ya_ref[rows, :] = words[:, :D_QUARTER]
            yb_ref[rows, :] = words[:, D_QUARTER:]

    @pl.when(i >= nu_ref[0])
    def _():
        ya_ref[...] = jnp.zeros(ya_ref.shape, jnp.uint32)
        yb_ref[...] = jnp.zeros(yb_ref.shape, jnp.uint32)


def _expert_ffn(xs_a, xs_b, tile_expert, tile_valid, n_used, wg, wu, wd, layer):
    def row_map(i, te, tv, nu):
        return (jnp.minimum(i, nu[0] - 1), 0)

    def out_map(i, te, tv, nu):
        return (i, 0)

    def w_map(i, te, tv, nu):
        return (layer, te[jnp.minimum(i, nu[0] - 1)], 0, 0)

    return pl.pallas_call(
        _ffn_kernel,
        grid_spec=pltpu.PrefetchScalarGridSpec(
            num_scalar_prefetch=3,
            grid=(N_TILES,),
            in_specs=[
                pl.BlockSpec((FFN_TM, D_QUARTER), row_map),
                pl.BlockSpec((FFN_TM, D_QUARTER), row_map),
                pl.BlockSpec((1, 1, D_MODEL, D_EXPERT), w_map),
                pl.BlockSpec((1, 1, D_MODEL, D_EXPERT), w_map),
                pl.BlockSpec((1, 1, D_EXPERT, D_MODEL), w_map),
            ],
            out_specs=[pl.BlockSpec((FFN_TM, D_QUARTER), out_map), pl.BlockSpec((FFN_TM, D_QUARTER), out_map)],
            scratch_shapes=[pltpu.VMEM((D_MODEL, D_EXPERT), BF16), pltpu.VMEM((D_MODEL, D_EXPERT), BF16),
                            pltpu.VMEM((D_EXPERT, D_MODEL), BF16)],
        ),
        out_shape=[jax.ShapeDtypeStruct((N_SORTED, D_QUARTER), jnp.uint32),
                   jax.ShapeDtypeStruct((N_SORTED, D_QUARTER), jnp.uint32)],
        compiler_params=pltpu.CompilerParams(dimension_semantics=("arbitrary",)),
        name="moe_expert_ffn",
    )(tile_expert, tile_valid, n_used, xs_a, xs_b, wg, wu, wd)


COMBINE_TM = 256


def _combine_kernel(x_ref, mods_ref, wcol_ref, ga_ref, gb_ref, sg_ref, su_ref, sd_ref, g_ref, b_ref, *o_refs):
    m = mods_ref[0]
    sf, cf, gf = m[3:4], m[4:5], m[5:6]
    x = x_ref[...]
    hc = (x * (1.0 + cf) + sf).astype(BF16)
    a = _silu(_dot(hc, sg_ref[...])) * _dot(hc, su_ref[...])
    y = _dot(a.astype(BF16), sd_ref[...])
    wcol = wcol_ref[...]
    parts = [y[:, q * D_QUARTER:(q + 1) * D_QUARTER] for q in range(4)]
    for k in range(TOP_K):
        hi_a, lo_a = _unpack_rows(ga_ref[k])
        hi_b, lo_b = _unpack_rows(gb_ref[k])
        wk = wcol[:, k:k + 1]
        parts = [parts[0] + wk * hi_a, parts[1] + wk * hi_b, parts[2] + wk * lo_a, parts[3] + wk * lo_b]
    y = jnp.concatenate(parts, axis=1)
    out = _layer_norm(ALPHA * x + gf * y, g_ref[...], b_ref[...])
    if len(o_refs) == 1:
        o_refs[0][...] = out
    else:
        is_ctx = pl.program_id(0) < T_CTX // x_ref.shape[0]

        @pl.when(is_ctx)
        def _():
            o_refs[0][...] = out

        @pl.when(jnp.logical_not(is_ctx))
        def _():
            o_refs[1][...] = out


def _combine(x, mods, wcol, g_a, g_b, sg, su, sd, ln_g, ln_b, split_streams):
    t = x.shape[0]
    tm = COMBINE_TM
    n_ctx = T_CTX // tm
    full = lambda shape: pl.BlockSpec(shape, lambda i: (0,) * len(shape))
    if split_streams:
        out_specs = [pl.BlockSpec((tm, D_MODEL), lambda i: (jnp.minimum(i, n_ctx - 1), 0)),
                     pl.BlockSpec((tm, D_MODEL), lambda i: (jnp.maximum(i - n_ctx, 0), 0))]
        out_shape = [jax.ShapeDtypeStruct((T_CTX, D_MODEL), F32), jax.ShapeDtypeStruct((T_LAT, D_MODEL), F32)]
    else:
        out_specs = pl.BlockSpec((tm, D_MODEL), lambda i: (i, 0))
        out_shape = jax.ShapeDtypeStruct((t, D_MODEL), F32)
    return pl.pallas_call(
        _combine_kernel,
        grid=(t // tm,),
        in_specs=[
            pl.BlockSpec((tm, D_MODEL), lambda i: (i, 0)),
            pl.BlockSpec((1, 6, D_MODEL), lambda i: (i * tm // GROUP_ROWS, 0, 0)),
            pl.BlockSpec((tm, LANES), lambda i: (i, 0)),
            pl.BlockSpec((TOP_K, tm, D_QUARTER), lambda i: (0, i, 0)),
            pl.BlockSpec((TOP_K, tm, D_QUARTER), lambda i: (0, i, 0)),
            full((D_MODEL, D_SHARED)), full((D_MODEL, D_SHARED)), full((D_SHARED, D_MODEL)),
            full((1, D_MODEL)), full((1, D_MODEL)),
        ],
        out_specs=out_specs,
        out_shape=out_shape,
        compiler_params=pltpu.CompilerParams(dimension_semantics=("arbitrary",)),
        name="moe_combine",
    )(x, mods, wcol, g_a, g_b, sg, su, sd, ln_g, ln_b)


MLA_TM = 512


def _mla_proj_kernel(x_ref, mods_ref, wdq_ref, wdkv_ref, wkr_ref, qn_ref, kvn_ref, wuq_ref,
                     wukn_ref, wuv_ref, ta_ref, tb_ref, q_ref, k_ref, v_ref, ckv_ref, kr_ref):
    m = mods_ref[0]
    sm, cm = m[0:1], m[1:2]
    h = (x_ref[...] * (1.0 + cm) + sm).astype(BF16)
    cq = _rms_norm(_dot(h, wdq_ref[...]), qn_ref[...])
    ckv = _rms_norm(_dot(h, wdkv_ref[...]), kvn_ref[...])
    kr2 = _dot(h, wkr_ref[...])
    ckv_ref[...] = ckv
    kr_ref[...] = kr2

    ka = ta_ref[0]
    kb = tb_ref[0]
    tm = ka.shape[0]
    ta = jnp.concatenate([jnp.full((tm, QK_NOPE), Q_PRESCALE, F32), ka * Q_PRESCALE], axis=1)
    tb = jnp.concatenate([jnp.zeros((tm, QK_NOPE), F32), kb * Q_PRESCALE], axis=1)
    krr = kr2 * ka + pltpu.roll(kr2, QK_ROPE, 1) * kb

    qpre = _dot(cq.astype(BF16), wuq_ref[...])
    ckv_b = ckv.astype(BF16)
    kn = _dot(ckv_b, wukn_ref[...])
    v_ref[...] = _dot(ckv_b, wuv_ref[...]).astype(BF16)
    for hd in range(N_HEADS):
        qh = qpre[:, hd * HEAD_PAD:(hd + 1) * HEAD_PAD]
        qrot = qh * ta + pltpu.roll(qh, HEAD_PAD - QK_ROPE, 1) * tb
        q_ref[:, hd * HEAD_PAD:(hd + 1) * HEAD_PAD] = qrot.astype(BF16)
        k_ref[:, hd * HEAD_PAD:hd * HEAD_PAD + QK_NOPE] = kn[:, hd * QK_NOPE:(hd + 1) * QK_NOPE].astype(BF16)
        k_ref[:, hd * HEAD_PAD + QK_NOPE:(hd + 1) * HEAD_PAD] = krr.astype(BF16)


def _mla_proj(x, mods, p, rope_a, rope_b):
    t = x.shape[0]
    tm = MLA_TM
    full = lambda shape: pl.BlockSpec(shape, lambda i: (0,) * len(shape))
    rope_spec = pl.BlockSpec(
        (1, tm, 2 * QK_ROPE),
        lambda i: (jnp.minimum(i * tm // GROUP_ROWS, 1), (i * tm % GROUP_ROWS) // tm, 0))
    return pl.pallas_call(
        _mla_proj_kernel,
        grid=(t // tm,),
        in_specs=[
            pl.BlockSpec((tm, D_MODEL), lambda i: (i, 0)),
            pl.BlockSpec((1, 6, D_MODEL), lambda i: (i * tm // GROUP_ROWS, 0, 0)),
            full((D_MODEL, Q_LORA)), full((D_MODEL, KV_LORA)), full((D_MODEL, 2 * QK_ROPE)),
            full((1, Q_LORA)), full((1, KV_LORA)),
            full((Q_LORA, N_HEADS * HEAD_PAD)),
            full((KV_LORA, N_HEADS * QK_NOPE)), full((KV_LORA, N_HEADS * V_DIM)),
            rope_spec, rope_spec,
        ],
        out_specs=[
            pl.BlockSpec((tm, N_HEADS * HEAD_PAD), lambda i: (i, 0)),
            pl.BlockSpec((tm, N_HEADS * HEAD_PAD), lambda i: (i, 0)),
            pl.BlockSpec((tm, N_HEADS * V_DIM), lambda i: (i, 0)),
            pl.BlockSpec((tm, KV_LORA), lambda i: (i, 0)),
            pl.BlockSpec((tm, 2 * QK_ROPE), lambda i: (i, 0)),
        ],
        out_shape=[
            jax.ShapeDtypeStruct((t, N_HEADS * HEAD_PAD), BF16),
            jax.ShapeDtypeStruct((t, N_HEADS * HEAD_PAD), BF16),
            jax.ShapeDtypeStruct((t, N_HEADS * V_DIM), BF16),
            jax.ShapeDtypeStruct((t, KV_LORA), F32),
            jax.ShapeDtypeStruct((t, 2 * QK_ROPE), F32),
        ],
        compiler_params=pltpu.CompilerParams(dimension_semantics=("parallel",)),
        name="mla_proj",
    )(x, mods, p["w_dq"], p["w_dkv"], p["w_kr"], p["q_norm"], p["kv_norm"], p["w_uq"],
      p["w_ukn"], p["w_uv"], rope_a, rope_b)


def _cache_kv_kernel(ckv_ref, kr_ref, wukn_ref, wuv_ref, k_ref, v_ref):
    ckv_b = ckv_ref[...].astype(BF16)
    kn = _dot(ckv_b, wukn_ref[...])
    v_ref[...] = _dot(ckv_b, wuv_ref[...]).astype(BF16)
    kr = kr_ref[...].astype(BF16)
    for hd in range(N_HEADS):
        k_ref[:, hd * HEAD_PAD:hd * HEAD_PAD + QK_NOPE] = kn[:, hd * QK_NOPE:(hd + 1) * QK_NOPE].astype(BF16)
        k_ref[:, hd * HEAD_PAD + QK_NOPE:(hd + 1) * HEAD_PAD] = kr


def _cache_kv(ckv, kr_pad, p):
    t = ckv.shape[0]
    tm = PAST_LEN
    full = lambda shape: pl.BlockSpec(shape, lambda i: (0,) * len(shape))
    return pl.pallas_call(
        _cache_kv_kernel,
        grid=(t // tm,),
        in_specs=[
            pl.BlockSpec((tm, KV_LORA), lambda i: (i, 0)),
            pl.BlockSpec((tm, 2 * QK_ROPE), lambda i: (i, 0)),
            full((KV_LORA, N_HEADS * QK_NOPE)), full((KV_LORA, N_HEADS * V_DIM)),
        ],
        out_specs=[
            pl.BlockSpec((tm, N_HEADS * HEAD_PAD), lambda i: (i, 0)),
            pl.BlockSpec((tm, N_HEADS * V_DIM), lambda i: (i, 0)),
        ],
        out_shape=[
            jax.ShapeDtypeStruct((t, N_HEADS * HEAD_PAD), BF16),
            jax.ShapeDtypeStruct((t, N_HEADS * V_DIM), BF16),
        ],
        compiler_params=pltpu.CompilerParams(dimension_semantics=("parallel",)),
        name="mla_cache_kv",
    )(ckv, kr_pad, p["w_ukn"], p["w_uv"])


def _ctx_attn_kernel(q_ref, k_ref, v_ref, o_ref):
    for hd in range(N_HEADS):
        q = q_ref[:, hd * HEAD_PAD:(hd + 1) * HEAD_PAD]
        k = k_ref[:, hd * HEAD_PAD:(hd + 1) * HEAD_PAD]
        s = _dot_nt(q, k)
        s = s - jnp.max(s, axis=-1, keepdims=True)
        p = jnp.exp2(s)
        p = p / jnp.sum(p, axis=-1, keepdims=True)
        o = _dot(p.astype(BF16), v_ref[:, hd * V_DIM:(hd + 1) * V_DIM])
        o_ref[:, hd * V_DIM:(hd + 1) * V_DIM] = o.astype(BF16)


def _ctx_attention(q, k, v):
    return pl.pallas_call(
        _ctx_attn_kernel,
        grid=(BATCH,),
        in_specs=[
            pl.BlockSpec((SEQ, N_HEADS * HEAD_PAD), lambda b: (b, 0)),
            pl.BlockSpec((SEQ, N_HEADS * HEAD_PAD), lambda b: (b, 0)),
            pl.BlockSpec((SEQ, N_HEADS * V_DIM), lambda b: (b, 0)),
        ],
        out_specs=pl.BlockSpec((SEQ, N_HEADS * V_DIM), lambda b: (b, 0)),
        out_shape=jax.ShapeDtypeStruct((T_CTX, N_HEADS * V_DIM), BF16),
        compiler_params=pltpu.CompilerParams(dimension_semantics=("parallel",)),
        name="ctx_attention",
    )(q, k, v)


LAT_TQ = 1024
LAT_TK = 512
LAT_PIECES = 4


def _lat_attn_kernel(q_ref, k_ref, v_ref, kc_ref, vc_ref, o_ref, s_ref, p_ref, m_ref):
    n_chunks = DEC_SEQ // LAT_TK + 1
    tp = q_ref.shape[0] // LAT_PIECES
    pieces = [pl.ds(j * tp, tp) for j in range(LAT_PIECES)]
    groups = [pieces[:2], pieces[2:]]
    state = {}

    def keys(c):
        return kc_ref[...] if c == n_chunks - 1 else k_ref[c * LAT_TK:(c + 1) * LAT_TK, :]

    def values(c):
        return vc_ref[...] if c == n_chunks - 1 else v_ref[c * LAT_TK:(c + 1) * LAT_TK, :]

    def qk(r, c):
        s = _dot_nt(q_ref[r, :], keys(c))
        s_ref[r, c * LAT_TK:(c + 1) * LAT_TK] = s
        mp = state.get(("m", r.start), jnp.full((tp, LANES), NEG_INF, F32))
        for j in range(LAT_TK // LANES):
            mp = jnp.maximum(mp, s[:, j * LANES:(j + 1) * LANES])
        state[("m", r.start)] = mp

    def row_max(r):
        m_ref[r, :] = jnp.broadcast_to(jnp.max(state[("m", r.start)], axis=-1, keepdims=True), (tp, LANES))

    def exp_chunk(r, c, after=None):
        lp = state.get(("l", r.start), jnp.zeros((tp, LANES), F32))
        m = m_ref[r, :]
        if after is not None:
            bits = lax.bitcast_convert_type(after, jnp.uint32)
            zero = lax.shift_right_logical(lax.shift_right_logical(bits, jnp.uint32(16)), jnp.uint32(16))
            m = m + lax.bitcast_convert_type(zero, F32)
        for j in range(c * LAT_TK // LANES, (c + 1) * LAT_TK // LANES):
            p = jnp.exp2(s_ref[r, j * LANES:(j + 1) * LANES] - m)
            lp = lp + p
            p_ref[r, j * LANES:(j + 1) * LANES] = p.astype(BF16)
        state[("l", r.start)] = lp

    def pv(r, c):
        acc = state.get(("a", r.start), jnp.zeros((tp, V_DIM), F32))
        state[("a", r.start)] = acc + _dot(p_ref[r, c * LAT_TK:(c + 1) * LAT_TK], values(c))

    def finish(r):
        o_ref[r, :] = (state[("a", r.start)] / jnp.sum(state[("l", r.start)], axis=-1, keepdims=True)).astype(BF16)

    for c in range(n_chunks):
        for r in groups[0]:
            qk(r, c)
    for r in groups[0]:
        row_max(r)
    for c in range(n_chunks):
        for r in groups[1]:
            qk(r, c)
        for r in groups[0]:
            exp_chunk(r, c)
    for r in groups[1]:
        row_max(r)
    for c in range(n_chunks):
        for r in groups[0]:
            pv(r, c)
        for r0, r in zip(groups[0], groups[1]):
            exp_chunk(r, c, after=state[("a", r0.start)])
    for r in groups[0]:
        finish(r)
    for c in range(n_chunks):
        for r in groups[1]:
            pv(r, c)
    for r in groups[1]:
        finish(r)


def _lat_attention(q, k, v, kc, vc):
    nq = DEC_SEQ // LAT_TQ
    return pl.pallas_call(
        _lat_attn_kernel,
        grid=(DEC_BATCH, N_HEADS, nq),
        in_specs=[
            pl.BlockSpec((LAT_TQ, HEAD_PAD), lambda b, h, i: ((b + 1) * nq + i, h)),
            pl.BlockSpec((DEC_SEQ, HEAD_PAD), lambda b, h, i: (b + 1, h)),
            pl.BlockSpec((DEC_SEQ, V_DIM), lambda b, h, i: (b + 1, h)),
            pl.BlockSpec((PAST_LEN, HEAD_PAD), lambda b, h, i: (b, h)),
            pl.BlockSpec((PAST_LEN, V_DIM), lambda b, h, i: (b, h)),
        ],
        out_specs=pl.BlockSpec((LAT_TQ, V_DIM), lambda b, h, i: (b * nq + i, h)),
        out_shape=jax.ShapeDtypeStruct((T_LAT, N_HEADS * V_DIM), BF16),
        scratch_shapes=[pltpu.VMEM((LAT_TQ, DEC_SEQ + PAST_LEN), F32),
                        pltpu.VMEM((LAT_TQ, DEC_SEQ + PAST_LEN), BF16),
                        pltpu.VMEM((LAT_TQ, LANES), F32)],
        compiler_params=pltpu.CompilerParams(
            dimension_semantics=("parallel", "parallel", "parallel")),
        name="lat_attention",
    )(q, k, v, kc, vc)


OPROJ_TM = 512


def _oproj_kernel(oc_ref, ol_ref, x_ref, mods_ref, wo_ref, g_ref, b_ref, out_ref):
    is_ctx = pl.program_id(0) < T_CTX // OPROJ_TM
    o = jnp.where(is_ctx, oc_ref[...], ol_ref[...])
    gm = mods_ref[0][2:3]
    y = _dot(o, wo_ref[...])
    out_ref[...] = _layer_norm(ALPHA * x_ref[...] + gm * y, g_ref[...], b_ref[...])


def _oproj(o_ctx, o_lat, x, mods, w_o, ln_g, ln_b):
    t = x.shape[0]
    tm = OPROJ_TM
    n_ctx = T_CTX // tm
    return pl.pallas_call(
        _oproj_kernel,
        grid=(t // tm,),
        in_specs=[
            pl.BlockSpec((tm, N_HEADS * V_DIM), lambda i: (jnp.minimum(i, n_ctx - 1), 0)),
            pl.BlockSpec((tm, N_HEADS * V_DIM), lambda i: (jnp.maximum(i - n_ctx, 0), 0)),
            pl.BlockSpec((tm, D_MODEL), lambda i: (i, 0)),
            pl.BlockSpec((1, 6, D_MODEL), lambda i: (i * tm // GROUP_ROWS, 0, 0)),
            pl.BlockSpec((N_HEADS * V_DIM, D_MODEL), lambda i: (0, 0)),
            pl.BlockSpec((1, D_MODEL), lambda i: (0, 0)),
            pl.BlockSpec((1, D_MODEL), lambda i: (0, 0)),
        ],
        out_specs=pl.BlockSpec((tm, D_MODEL), lambda i: (i, 0)),
        out_shape=jax.ShapeDtypeStruct((t, D_MODEL), F32),
        compiler_params=pltpu.CompilerParams(dimension_semantics=("parallel",)),
        name="attn_oproj",
    )(o_ctx, o_lat, x, mods, w_o, ln_g, ln_b)


def _swap16(w):
    q = QK_ROPE // 4
    return jnp.concatenate([w[..., q:2 * q], w[..., :q], w[..., 3 * q:], w[..., 2 * q:3 * q]], axis=-1)


def _mla_params(w_dqkv, q_norm, w_uq, kv_norm, w_ukv):
    w_kr = w_dqkv[:, Q_LORA + KV_LORA:]
    wq = w_uq.reshape(Q_LORA, N_HEADS, QK_NOPE + QK_ROPE)
    wq_r = wq[..., QK_NOPE:]
    wq = jnp.concatenate([wq[..., :QK_NOPE], wq_r, _swap16(wq_r)], axis=-1)
    wkv = w_ukv.reshape(KV_LORA, N_HEADS, QK_NOPE + V_DIM)
    return {
        "w_dq": w_dqkv[:, :Q_LORA].astype(BF16),
        "w_dkv": w_dqkv[:, Q_LORA:Q_LORA + KV_LORA].astype(BF16),
        "w_kr": jnp.concatenate([w_kr, _swap16(w_kr)], axis=-1).astype(BF16),
        "q_norm": q_norm.reshape(1, Q_LORA),
        "kv_norm": kv_norm.reshape(1, KV_LORA),
        "w_uq": wq.reshape(Q_LORA, N_HEADS * HEAD_PAD).astype(BF16),
        "w_ukn": wkv[..., :QK_NOPE].reshape(KV_LORA, N_HEADS * QK_NOPE).astype(BF16),
        "w_uv": wkv[..., QK_NOPE:].reshape(KV_LORA, N_HEADS * V_DIM).astype(BF16),
    }


def _rope_tables():
    nf = QK_ROPE // 4
    t = np.arange(DEC_SEQ)
    row = (t // GRID_W).astype(np.float32)
    col = (t % GRID_W).astype(np.float32)
    inv = (ROPE_THETA ** (-np.arange(nf, dtype=np.float32) / nf)).astype(np.float32)
    ar, ac = row[:, None] * inv, col[:, None] * inv
    pad = np.zeros((DEC_SEQ, QK_ROPE), np.float32)
    cos = np.concatenate([np.cos(ar), np.cos(ar), np.cos(ac), np.cos(ac), pad], axis=-1)
    sin = np.concatenate([-np.sin(ar), np.sin(ar), -np.sin(ac), np.sin(ac), pad], axis=-1)
    cos_id = np.concatenate([np.ones((DEC_SEQ, QK_ROPE), np.float32), pad], axis=-1)
    sin_id = np.zeros((DEC_SEQ, 2 * QK_ROPE), np.float32)
    return (jnp.asarray(np.stack([cos_id, cos]).astype(np.float32)),
            jnp.asarray(np.stack([sin_id, sin]).astype(np.float32)))


def kernel(x_prompt, x_sample, cache_ckv, cache_krope, c, c_ctx, ada_w, ada_b, ln_g, ln_b, conv_w_in, conv_k, conv_w_out, mla_w_dqkv, mla_q_norm, mla_w_uq, mla_kv_norm, mla_w_ukv, mla_w_o, router_w, router_bias, exp_w_gate, exp_w_up, exp_w_down, sh_w_gate, sh_w_up, sh_w_down):
    cvecs =jnp.concatenate([c_ctx[None, :], c, jnp.zeros((SUBLANES - N_GROUPS_ROWS, D_MODEL), F32)], axis=0)
    mods = _adaln(cvecs, ada_w, ada_b)

    def ln(l, k):
        return ln_g[l, k].reshape(1, D_MODEL), ln_b[l, k].reshape(1, D_MODEL)

    def moe_layer(xin, l, split_streams):
        idx, pos, wcol, counts, hp_a, hp_b = _router(xin, mods[l], router_w[l].T,
                                                     router_bias[l].reshape(N_EXPERTS, 1))
        cnt = counts[:, 0].astype(jnp.int32)
        start, tile_expert, tile_valid, n_used = _expert_layout(cnt)
        dest = _dest_rows(start, idx, pos)
        xs_a = _sc_scatter_rows(hp_a, dest)
        xs_b = _sc_scatter_rows(hp_b, dest)
        ys_a, ys_b = _expert_ffn(xs_a, xs_b, tile_expert, tile_valid, n_used, exp_w_gate, exp_w_up, exp_w_down, l)
        dest_row = dest.reshape(1, N_SLOTS)
        g_a = _sc_gather_rows(ys_a, dest_row).reshape(TOP_K, T_ALL, D_QUARTER)
        g_b = _sc_gather_rows(ys_b, dest_row).reshape(TOP_K, T_ALL, D_QUARTER)
        return _combine(xin, mods[l], wcol, g_a, g_b, sh_w_gate[l].astype(BF16), sh_w_up[l].astype(BF16),
                        sh_w_down[l].astype(BF16), *ln(l, 1), split_streams=split_streams)

    x = _conv_mixer(x_prompt.reshape(T_CTX, D_MODEL), x_sample.reshape(T_LAT, D_MODEL), mods[0],
                    conv_w_in[0].astype(BF16), conv_k[0], conv_w_out[0].astype(BF16), *ln(0, 0))
    x = moe_layer(x, 0, split_streams=False)

    p = _mla_params(mla_w_dqkv[0], mla_q_norm[0], mla_w_uq[0], mla_kv_norm[0], mla_w_ukv[0])
    rope_a, rope_b = _rope_tables()
    q, k, v, ckv, kr = _mla_proj(x, mods[1], p, rope_a, rope_b)
    kr_cache = jnp.concatenate([cache_krope[:, 0].reshape(DEC_BATCH * PAST_LEN, QK_ROPE),
                                jnp.zeros((DEC_BATCH * PAST_LEN, QK_ROPE), F32)], axis=-1)
    kc, vc = _cache_kv(cache_ckv[:, 0].reshape(DEC_BATCH * PAST_LEN, KV_LORA), kr_cache, p)
    o_ctx = _ctx_attention(q, k, v)
    o_lat = _lat_attention(q, k, v, kc, vc)
    x = _oproj(o_ctx, o_lat, x, mods[1], mla_w_o[0].astype(BF16), *ln(1, 0))
    y_ctx, y_lat = moe_layer(x, 1, split_streams=True)

    y_prompt = y_ctx.reshape(BATCH, SEQ, D_MODEL)
    y_sample = y_lat.reshape(DEC_BATCH, DEC_SEQ, D_MODEL)
    state_ckv = ckv[:T_CTX].reshape(BATCH, 1, SEQ, KV_LORA)
    state_krope = kr[:T_CTX, :QK_ROPE].reshape(BATCH, 1, SEQ, QK_ROPE)
    return (y_prompt, y_sample, state_ckv, state_krope)
```

```python
import functools
import math

import jax
import jax.numpy as jnp
import numpy as np
from jax import lax
from jax.experimental import pallas as pl
from jax.experimental.pallas import tpu as pltpu
from jax.experimental.pallas import tpu_sc as plsc

D_MODEL = 1024
BATCH = 16
SEQ = 256
DEPTH = 2
DEC_BATCH = 4
DEC_SEQ = 4096
PAST_LEN = 512
GRID_W = 64

N_HEADS = 8
QK_NOPE = 128
QK_ROPE = 64
V_DIM = 128
Q_LORA = 384
KV_LORA = 256
ROPE_THETA = 10000.0
ATTN_SCALE = (QK_NOPE + QK_ROPE) ** -0.5
HEAD_PAD = 256
Q_PRESCALE = ATTN_SCALE * math.log2(math.e)

N_EXPERTS = 64
TOP_K = 8
N_GROUPS = 8
TOPK_GROUPS = 4
GROUP_SIZE = N_EXPERTS // N_GROUPS
D_EXPERT = 256
D_SHARED = 256
ROUTED_SCALE = 2.5

ALPHA = (2 * DEPTH) ** 0.25
LN_EPS = 1e-5
RMS_EPS = 1e-6

GROUP_ROWS = 4096
N_GROUPS_ROWS = 1 + DEC_BATCH
T_CTX = BATCH * SEQ
T_LAT = DEC_BATCH * DEC_SEQ
T_ALL = T_CTX + T_LAT
LANES = 128
SUBLANES = 8

F32 = jnp.float32
BF16 = jnp.bfloat16
NEG_INF = float("-inf")


def _dot(a, b):
    return jnp.dot(a, b, preferred_element_type=F32)


def _dot_nt(a, b, precision=None):
    return lax.dot_general(a, b, (((1,), (1,)), ((), ())), precision=precision,
                           preferred_element_type=F32)


def _layer_norm(v, g, b):
    mu = jnp.mean(v, axis=-1, keepdims=True)
    d = v - mu
    var = jnp.mean(d * d, axis=-1, keepdims=True)
    return d * lax.rsqrt(var + LN_EPS) * g + b


def _rms_norm(v, g):
    return v * lax.rsqrt(jnp.mean(v * v, axis=-1, keepdims=True) + RMS_EPS) * g


def _silu(v):
    return v / (1.0 + jnp.exp(-v))


def _sigmoid(v):
    return 1.0 / (1.0 + jnp.exp(-v))


def _adaln_kernel(c_ref, w_ref, b_ref, o_ref):
    c = c_ref[...]
    s = _silu(c)
    o_ref[0, 0] = jnp.dot(s, w_ref[0], precision=lax.Precision.HIGHEST,
                          preferred_element_type=F32) + b_ref[0]


def _adaln(cvecs, ada_w, ada_b):
    out = pl.pallas_call(
        _adaln_kernel,
        grid=(DEPTH, 6),
        in_specs=[
            pl.BlockSpec((SUBLANES, D_MODEL), lambda l, j: (0, 0)),
            pl.BlockSpec((1, D_MODEL, D_MODEL), lambda l, j: (l, 0, j)),
            pl.BlockSpec((1, 1, D_MODEL), lambda l, j: (l, 0, j)),
        ],
        out_specs=pl.BlockSpec((1, 1, SUBLANES, D_MODEL), lambda l, j: (l, j, 0, 0)),
        out_shape=jax.ShapeDtypeStruct((DEPTH, 6, SUBLANES, D_MODEL), F32),
        compiler_params=pltpu.CompilerParams(dimension_semantics=("parallel", "parallel")),
        name="adaln",
    )(cvecs, ada_w, ada_b.reshape(DEPTH, 1, 6 * D_MODEL))
    return jnp.transpose(out[:, :, :N_GROUPS_ROWS, :], (0, 2, 1, 3))


CONV_TM = 512


def _conv_kernel(xc_ref, xcp_ref, xcn_ref, xl_ref, xlp_ref, xln_ref, mods_ref, win_ref, ck_ref, wout_ref,
                 g_ref, b_ref, o_ref):
    i = pl.program_id(0)
    tm = xc_ref.shape[0]
    is_ctx = i < T_CTX // tm
    m = mods_ref[0]
    sm, cm, gm = m[0:1], m[1:2], m[2:3]
    x = jnp.where(is_ctx, xc_ref[...], xl_ref[...])
    xp = jnp.where(is_ctx, xcp_ref[...], xlp_ref[...])
    xn = jnp.where(is_ctx, xcn_ref[...], xln_ref[...])
    h = (x * (1.0 + cm) + sm).astype(BF16)
    z = _dot(h, win_ref[...])
    bg = z[:, :D_MODEL]
    u = z[:, D_MODEL:2 * D_MODEL] * z[:, 2 * D_MODEL:]

    def halo_u(xh):
        hh = (xh * (1.0 + cm) + sm).astype(BF16)
        zh = _dot(hh, win_ref[:, D_MODEL:])
        return zh[:, :D_MODEL] * zh[:, D_MODEL:]

    u_prev = halo_u(xp)[SUBLANES - 1:SUBLANES]
    u_next = halo_u(xn)[0:1]

    row = lax.broadcasted_iota(jnp.int32, (tm, 1), 0)
    grow = i * tm + row
    seq_len = jnp.where(grow < T_CTX, SEQ, DEC_SEQ)
    pos = jnp.bitwise_and(grow, seq_len - 1)
    is_first = pos == 0
    is_last = pos == seq_len - 1

    left = pltpu.roll(u, 1, 0)
    left = jnp.where(row == 0, u_prev, left)
    left = jnp.where(is_first, 0.0, left)
    right = pltpu.roll(u, tm - 1, 0)
    right = jnp.where(row == tm - 1, u_next, right)
    right = jnp.where(is_last, 0.0, right)

    ck = ck_ref[...]
    conv = left * ck[0:1] + u * ck[1:2] + right * ck[2:3]
    v = (bg * conv).astype(BF16)
    y = _dot(v, wout_ref[...])
    o_ref[...] = _layer_norm(ALPHA * x + gm * y, g_ref[...], b_ref[...])


def _conv_mixer(x_ctx, x_lat, mods, w_in, conv_k, w_out, ln_g, ln_b):
    tm = CONV_TM
    per8 = tm // SUBLANES
    n_ctx = T_CTX // tm
    n_lat = T_LAT // tm

    def stream_specs(first, n):
        def blk(i):
            return jnp.clip(i - first, 0, n - 1)
        return [
            pl.BlockSpec((tm, D_MODEL), lambda i: (blk(i), 0)),
            pl.BlockSpec((SUBLANES, D_MODEL), lambda i: (jnp.maximum(blk(i) * per8 - 1, 0), 0)),
            pl.BlockSpec((SUBLANES, D_MODEL), lambda i: (jnp.minimum((blk(i) + 1) * per8, n * per8 - 1), 0)),
        ]

    return pl.pallas_call(
        _conv_kernel,
        grid=(n_ctx + n_lat,),
        in_specs=stream_specs(0, n_ctx) + stream_specs(n_ctx, n_lat) + [
            pl.BlockSpec((1, 6, D_MODEL), lambda i: (i * tm // GROUP_ROWS, 0, 0)),
            pl.BlockSpec((D_MODEL, 3 * D_MODEL), lambda i: (0, 0)),
            pl.BlockSpec((3, D_MODEL), lambda i: (0, 0)),
            pl.BlockSpec((D_MODEL, D_MODEL), lambda i: (0, 0)),
            pl.BlockSpec((1, D_MODEL), lambda i: (0, 0)),
            pl.BlockSpec((1, D_MODEL), lambda i: (0, 0)),
        ],
        out_specs=pl.BlockSpec((tm, D_MODEL), lambda i: (i, 0)),
        out_shape=jax.ShapeDtypeStruct((T_ALL, D_MODEL), F32),
        compiler_params=pltpu.CompilerParams(dimension_semantics=("parallel",)),
        name="conv_mixer",
    )(x_ctx, x_ctx, x_ctx, x_lat, x_lat, x_lat, mods, w_in, conv_k, w_out, ln_g, ln_b)


ROUTER_TM = 512
MOE_CHUNKS = 2


def _first_argmax_mask(cur, ridx, n):
    mx = jnp.max(cur, axis=0, keepdims=True)
    first = jnp.min(jnp.where(cur == mx, ridx, n), axis=0, keepdims=True)
    return ridx == first, mx


def _router_kernel(x_ref, mods_ref, rwt_ref, bias_ref, idx_ref, pos_ref, wcol_ref, count_ref, hpa_ref, hpb_ref,
                   carry_ref):
    tm = x_ref.shape[0]
    m = mods_ref[0]
    sf, cf = m[3:4], m[4:5]
    hc = x_ref[...] * (1.0 + cf) + sf
    words = _pack_rows(hc)
    hpa_ref[...] = words[:, :D_QUARTER]
    hpb_ref[...] = words[:, D_QUARTER:]
    logits = _dot_nt(rwt_ref[...], hc, precision=lax.Precision.HIGHEST)
    scores = _sigmoid(logits)
    biased = scores + bias_ref[...]

    ridx8 = lax.broadcasted_iota(jnp.int32, (GROUP_SIZE, tm), 0)
    gscore = jnp.full((N_GROUPS, tm), NEG_INF, F32)
    for g in range(N_GROUPS):
        blk = biased[g * GROUP_SIZE:(g + 1) * GROUP_SIZE]
        sel, m1 = _first_argmax_mask(blk, ridx8, GROUP_SIZE)
        m2 = jnp.max(jnp.where(sel, NEG_INF, blk), axis=0, keepdims=True)
        gscore = jnp.where(ridx8 == g, m1 + m2, gscore)

    gmask = jnp.zeros((N_GROUPS, tm), jnp.bool_)
    cur = gscore
    for _ in range(TOPK_GROUPS):
        sel, _unused = _first_argmax_mask(cur, ridx8, N_GROUPS)
        gmask = jnp.logical_or(gmask, sel)
        cur = jnp.where(sel, NEG_INF, cur)

    gmask_f = gmask.astype(F32)
    blocks = []
    for g in range(N_GROUPS):
        keep = jnp.broadcast_to(gmask_f[g:g + 1], (GROUP_SIZE, tm)) > 0.5
        blocks.append(jnp.where(keep, biased[g * GROUP_SIZE:(g + 1) * GROUP_SIZE], NEG_INF))
    cur = jnp.concatenate(blocks, axis=0)

    @pl.when(lax.rem(pl.program_id(0), pl.num_programs(0) // MOE_CHUNKS) == 0)
    def _():
        carry_ref[...] = jnp.zeros(carry_ref.shape, F32)

    ridx = lax.broadcasted_iota(jnp.int32, (N_EXPERTS, tm), 0)
    kidx = lax.broadcasted_iota(jnp.int32, (TOP_K, tm), 0)
    sels = []
    chosen = jnp.zeros((N_EXPERTS, tm), jnp.bool_)
    idx_rows = jnp.zeros((TOP_K, tm), jnp.int32)
    for k in range(TOP_K):
        mx = jnp.max(cur, axis=0, keepdims=True)
        first = jnp.min(jnp.where(cur == mx, ridx, N_EXPERTS), axis=0, keepdims=True)
        sel = ridx == first
        sels.append(sel)
        chosen = jnp.logical_or(chosen, sel)
        idx_rows = jnp.where(kidx == k, first, idx_rows)
        cur = jnp.where(sel, NEG_INF, cur)

    onehot = chosen.astype(F32)
    t_row = lax.broadcasted_iota(jnp.int32, (tm, tm), 0)
    t_col = lax.broadcasted_iota(jnp.int32, (tm, tm), 1)
    before = (t_row < t_col).astype(BF16)
    rank = carry_ref[...] + _dot(onehot.astype(BF16), before)
    carry_ref[...] = carry_ref[...] + jnp.sum(onehot, axis=1, keepdims=True)
    count_ref[0] = jnp.broadcast_to(carry_ref[...], count_ref.shape[1:])

    w = jnp.where(chosen, scores, 0.0)
    w = w / jnp.sum(w, axis=0, keepdims=True) * ROUTED_SCALE
    pos_rows = jnp.zeros((TOP_K, tm), F32)
    w_rows = jnp.zeros((TOP_K, tm), F32)
    for k in range(TOP_K):
        pos_rows = jnp.where(kidx == k, jnp.sum(jnp.where(sels[k], rank, 0.0), axis=0, keepdims=True), pos_rows)
        w_rows = jnp.where(kidx == k, jnp.sum(jnp.where(sels[k], w, 0.0), axis=0, keepdims=True), w_rows)
    idx_ref[...] = idx_rows
    pos_ref[...] = pos_rows.astype(jnp.int32)
    wpad = jnp.concatenate([w_rows, jnp.zeros((LANES - TOP_K, tm), F32)], axis=0)
    wcol_ref[...] = wpad.T


def _router(x, mods, router_wt, router_bias):
    t = x.shape[0]
    tm = ROUTER_TM
    per_chunk = t // tm // MOE_CHUNKS
    return pl.pallas_call(
        _router_kernel,
        grid=(t // tm,),
        in_specs=[
            pl.BlockSpec((tm, D_MODEL), lambda i: (i, 0)),
            pl.BlockSpec((1, 6, D_MODEL), lambda i: (i * tm // GROUP_ROWS, 0, 0)),
            pl.BlockSpec((N_EXPERTS, D_MODEL), lambda i: (0, 0)),
            pl.BlockSpec((N_EXPERTS, 1), lambda i: (0, 0)),
        ],
        out_specs=[
            pl.BlockSpec((TOP_K, tm), lambda i: (0, i)),
            pl.BlockSpec((TOP_K, tm), lambda i: (0, i)),
            pl.BlockSpec((tm, LANES), lambda i: (i, 0)),
            pl.BlockSpec((1, N_EXPERTS, LANES), lambda i: (i // per_chunk, 0, 0)),
            pl.BlockSpec((tm, D_QUARTER), lambda i: (i, 0)),
            pl.BlockSpec((tm, D_QUARTER), lambda i: (i, 0)),
        ],
        out_shape=[
            jax.ShapeDtypeStruct((TOP_K, t), jnp.int32),
            jax.ShapeDtypeStruct((TOP_K, t), jnp.int32),
            jax.ShapeDtypeStruct((t, LANES), F32),
            jax.ShapeDtypeStruct((MOE_CHUNKS, N_EXPERTS, LANES), F32),
            jax.ShapeDtypeStruct((t, D_QUARTER), jnp.uint32),
            jax.ShapeDtypeStruct((t, D_QUARTER), jnp.uint32),
        ],
        scratch_shapes=[pltpu.VMEM((N_EXPERTS, 1), F32)],
        compiler_params=pltpu.CompilerParams(dimension_semantics=("arbitrary",)),
        name="moe_router",
    )(x, mods, router_wt, router_bias)


FFN_TM = 512
T_CHUNK = T_ALL // MOE_CHUNKS
N_SLOTS = T_CHUNK * TOP_K
N_TILES = N_SLOTS // FFN_TM + N_EXPERTS
N_SORTED = N_TILES * FFN_TM


def _expert_layout(counts):
    padded = (counts + FFN_TM - 1) // FFN_TM * FFN_TM
    end = jnp.cumsum(padded)
    start = end - padded
    tile_row = jnp.arange(N_TILES, dtype=jnp.int32) * FFN_TM
    tile_expert = jnp.minimum(jnp.sum(end[None, :] <= tile_row[:, None], axis=1), N_EXPERTS - 1)
    of_tile = tile_expert[:, None] == jnp.arange(N_EXPERTS, dtype=jnp.int32)[None, :]
    live_end = jnp.sum(jnp.where(of_tile, (start + counts)[None, :], 0), axis=1)
    tile_valid = jnp.clip(live_end - tile_row, 0, FFN_TM)
    n_used = (end[-1] // FFN_TM).astype(jnp.int32).reshape(1)
    return start.astype(jnp.int32), tile_expert.astype(jnp.int32), tile_valid.astype(jnp.int32), n_used


D_HALF = D_MODEL // 2
D_QUARTER = D_MODEL // 4


def _pack_rows(v):
    hi = lax.bitcast_convert_type(v[:, :D_HALF].astype(BF16).astype(F32), jnp.uint32)
    lo = lax.bitcast_convert_type(v[:, D_HALF:].astype(BF16).astype(F32), jnp.uint32)
    return jnp.bitwise_or(hi, jnp.right_shift(lo, jnp.uint32(16)))


def _unpack_rows(w):
    hi = lax.bitcast_convert_type(jnp.bitwise_and(w, jnp.uint32(0xFFFF0000)), F32)
    lo = lax.bitcast_convert_type(jnp.left_shift(w, jnp.uint32(16)), F32)
    return hi, lo


DEST_TM = 2048


def _dest_kernel(start_ref, idx_ref, pos_ref, dest_ref):
    idx = idx_ref[...]
    base = jnp.zeros(idx.shape, jnp.int32)
    for e in range(N_EXPERTS):
        base = jnp.where(idx == e, start_ref[e], base)
    dest_ref[...] = base + pos_ref[...]


def _dest_rows(start, idx, pos):
    t = idx.shape[1]
    return pl.pallas_call(
        _dest_kernel,
        grid_spec=pltpu.PrefetchScalarGridSpec(
            num_scalar_prefetch=1,
            grid=(t // DEST_TM,),
            in_specs=[pl.BlockSpec((TOP_K, DEST_TM), lambda i, s: (0, i)),
                      pl.BlockSpec((TOP_K, DEST_TM), lambda i, s: (0, i))],
            out_specs=pl.BlockSpec((TOP_K, DEST_TM), lambda i, s: (0, i)),
        ),
        out_shape=jax.ShapeDtypeStruct((TOP_K, t), jnp.int32),
        compiler_params=pltpu.CompilerParams(dimension_semantics=("parallel",)),
        name="moe_dest",
    )(start, idx, pos)


SC_WINDOW = 128


def _sc_mesh():
    return plsc.VectorSubcoreMesh(core_axis_name="c", subcore_axis_name="s")


def _sc_scatter_rows(x, dest, row0):
    t = dest.shape[1]
    blk0 = row0 // SC_WINDOW

    @functools.partial(
        pl.kernel,
        out_type=jax.ShapeDtypeStruct((N_SORTED, D_QUARTER), x.dtype),
        mesh=_sc_mesh(),
        scratch_types=[],
    )
    def scatter(x_hbm, i_hbm, o_hbm):
        def body(x_vmem, i_vmem):
            pltpu.sync_copy(x_vmem, o_hbm.at[i_vmem.at[0]])

        pltpu.emit_pipeline(
            body,
            grid=(t // SC_WINDOW, TOP_K),
            in_specs=[pl.BlockSpec((SC_WINDOW, D_QUARTER), lambda i, k: (i + blk0, 0)),
                      pl.BlockSpec((1, SC_WINDOW), lambda i, k: (k, i))],
            out_specs=[],
            core_axis_name=("c", "s"),
            dimension_semantics=(pltpu.PARALLEL, pltpu.ARBITRARY),
        )(x_hbm, i_hbm)

    return scatter(x, dest)


def _sc_gather_rows(table, idx):
    m = idx.shape[1]

    @functools.partial(
        pl.kernel,
        out_type=jax.ShapeDtypeStruct((m, D_QUARTER), table.dtype),
        mesh=_sc_mesh(),
        scratch_types=[],
    )
    def gather(t_hbm, i_hbm, o_hbm):
        def body(i_vmem, o_vmem):
            pltpu.sync_copy(t_hbm.at[i_vmem.at[0]], o_vmem)

        pltpu.emit_pipeline(
            body,
            grid=(m // SC_WINDOW,),
            in_specs=[pl.BlockSpec((1, SC_WINDOW), lambda i: (0, i))],
            out_specs=[pl.BlockSpec((SC_WINDOW, D_QUARTER), lambda i: (i, 0))],
            core_axis_name=("c", "s"),
            dimension_semantics=(pltpu.PARALLEL,),
        )(i_hbm, o_hbm)

    return gather(table, idx)


def _ffn_kernel(te_ref, tv_ref, nu_ref, xa_ref, xb_ref, wg_ref, wu_ref, wd_ref, ya_ref, yb_ref,
                wgb_ref, wub_ref, wdb_ref):
    i = pl.program_id(0)

    @pl.when(jnp.logical_and(i < nu_ref[0],
                             jnp.logical_or(i == 0, te_ref[i] != te_ref[jnp.maximum(i - 1, 0)])))
    def _():
        wgb_ref[...] = wg_ref[0, 0].astype(BF16)
        wub_ref[...] = wu_ref[0, 0].astype(BF16)
        wdb_ref[...] = wd_ref[0, 0].astype(BF16)

    @pl.when(i < nu_ref[0])
    def _():
        wg = wgb_ref[...]
        wu = wub_ref[...]
        wd = wdb_ref[...]
        th = FFN_TM // 2
        for r0 in (0, th):
            rows = pl.ds(r0, th)
            live = (lax.broadcasted_iota(jnp.int32, (th, D_QUARTER), 0) + r0) < tv_ref[i]
            hi_a, lo_a = _unpack_rows(jnp.where(live, xa_ref[rows, :], jnp.uint32(0)))
            hi_b, lo_b = _unpack_rows(jnp.where(live, xb_ref[rows, :], jnp.uint32(0)))
            xb = jnp.concatenate([hi_a.astype(BF16), hi_b.astype(BF16), lo_a.astype(BF16), lo_b.astype(BF16)],
                                 axis=1)
            a = (_silu(_dot(xb, wg)) * _dot(xb, wu)).astype(BF16)
            words = _pack_rows(_dot(a, wd))
            ya_ref[rows, :] = words[:, :D_QUARTER]
            yb_ref[rows, :] = words[:, D_QUARTER:]

    @pl.when(i >= nu_ref[0])
    def _():
        ya_ref[...] = jnp.zeros(ya_ref.shape, jnp.uint32)
        yb_ref[...] = jnp.zeros(yb_ref.shape, jnp.uint32)


def _expert_ffn(xs_a, xs_b, tile_expert, tile_valid, n_used, wg, wu, wd, layer):
    def row_map(i, te, tv, nu):
        return (jnp.minimum(i, nu[0] - 1), 0)

    def out_map(i, te, tv, nu):
        return (i, 0)

    def w_map(i, te, tv, nu):
        return (layer, te[jnp.minimum(i, nu[0] - 1)], 0, 0)

    return pl.pallas_call(
        _ffn_kernel,
        grid_spec=pltpu.PrefetchScalarGridSpec(
            num_scalar_prefetch=3,
            grid=(N_TILES,),
            in_specs=[
                pl.BlockSpec((FFN_TM, D_QUARTER), row_map),
                pl.BlockSpec((FFN_TM, D_QUARTER), row_map),
                pl.BlockSpec((1, 1, D_MODEL, D_EXPERT), w_map),
                pl.BlockSpec((1, 1, D_MODEL, D_EXPERT), w_map),
                pl.BlockSpec((1, 1, D_EXPERT, D_MODEL), w_map),
            ],
            out_specs=[pl.BlockSpec((FFN_TM, D_QUARTER), out_map), pl.BlockSpec((FFN_TM, D_QUARTER), out_map)],
            scratch_shapes=[pltpu.VMEM((D_MODEL, D_EXPERT), BF16), pltpu.VMEM((D_MODEL, D_EXPERT), BF16),
                            pltpu.VMEM((D_EXPERT, D_MODEL), BF16)],
        ),
        out_shape=[jax.ShapeDtypeStruct((N_SORTED, D_QUARTER), jnp.uint32),
                   jax.ShapeDtypeStruct((N_SORTED, D_QUARTER), jnp.uint32)],
        compiler_params=pltpu.CompilerParams(dimension_semantics=("arbitrary",)),
        name="moe_expert_ffn",
    )(tile_expert, tile_valid, n_used, xs_a, xs_b, wg, wu, wd)


COMBINE_TM = 256


def _combine_kernel(n_prev, tile0, x_ref, mods_ref, wcol_ref, ga_ref, gb_ref, sg_ref, su_ref, sd_ref, g_ref, b_ref,
                    *rest):
    o_refs = rest[n_prev:]
    m = mods_ref[0]
    sf, cf, gf = m[3:4], m[4:5], m[5:6]
    x = x_ref[...]
    hc = (x * (1.0 + cf) + sf).astype(BF16)
    a = _silu(_dot(hc, sg_ref[...])) * _dot(hc, su_ref[...])
    y = _dot(a.astype(BF16), sd_ref[...])
    wcol = wcol_ref[...]
    parts = [y[:, q * D_QUARTER:(q + 1) * D_QUARTER] for q in range(4)]
    for k in range(TOP_K):
        hi_a, lo_a = _unpack_rows(ga_ref[k])
        hi_b, lo_b = _unpack_rows(gb_ref[k])
        wk = wcol[:, k:k + 1]
        parts = [parts[0] + wk * hi_a, parts[1] + wk * hi_b, parts[2] + wk * lo_a, parts[3] + wk * lo_b]
    y = jnp.concatenate(parts, axis=1)
    out = _layer_norm(ALPHA * x + gf * y, g_ref[...], b_ref[...])
    if len(o_refs) == 1:
        o_refs[0][...] = out
    else:
        is_ctx = pl.program_id(0) + tile0 < T_CTX // x_ref.shape[0]

        @pl.when(is_ctx)
        def _():
            o_refs[0][...] = out

        @pl.when(jnp.logical_not(is_ctx))
        def _():
            o_refs[1][...] = out


def _combine(x, mods, wcol, g_a, g_b, sg, su, sd, ln_g, ln_b, chunk, split_streams, prev):
    tm = COMBINE_TM
    n = T_CHUNK // tm
    tile0 = chunk * n
    n_ctx = T_CTX // tm
    full = lambda shape: pl.BlockSpec(shape, lambda i: (0,) * len(shape))
    layouts = {
        "all": (lambda i: (tile0 + i, 0), T_ALL),
        "ctx": (lambda i: (jnp.minimum(tile0 + i, n_ctx - 1), 0), T_CTX),
        "lat": (lambda i: (jnp.maximum(tile0 + i - n_ctx, 0), 0), T_LAT),
    }
    if split_streams:
        kinds = (["ctx"] if tile0 < n_ctx else []) + (["lat"] if tile0 + n > n_ctx else [])
    else:
        kinds = ["all"]
    carried = [kd for kd in kinds if kd in prev]
    outs = pl.pallas_call(
        functools.partial(_combine_kernel, len(carried), tile0),
        grid=(n,),
        in_specs=[
            pl.BlockSpec((tm, D_MODEL), lambda i: (tile0 + i, 0)),
            pl.BlockSpec((1, 6, D_MODEL), lambda i: ((tile0 + i) * tm // GROUP_ROWS, 0, 0)),
            pl.BlockSpec((tm, LANES), lambda i: (tile0 + i, 0)),
            pl.BlockSpec((TOP_K, tm, D_QUARTER), lambda i: (0, i, 0)),
            pl.BlockSpec((TOP_K, tm, D_QUARTER), lambda i: (0, i, 0)),
            full((D_MODEL, D_SHARED)), full((D_MODEL, D_SHARED)), full((D_SHARED, D_MODEL)),
            full((1, D_MODEL)), full((1, D_MODEL)),
        ] + [pl.BlockSpec(memory_space=pl.ANY) for _ in carried],
        out_specs=[pl.BlockSpec((tm, D_MODEL), layouts[kd][0]) for kd in kinds],
        out_shape=[jax.ShapeDtypeStruct((layouts[kd][1], D_MODEL), F32) for kd in kinds],
        input_output_aliases={10 + j: kinds.index(kd) for j, kd in enumerate(carried)},
        compiler_params=pltpu.CompilerParams(dimension_semantics=("arbitrary",)),
        name="moe_combine",
    )(x, mods, wcol, g_a, g_b, sg, su, sd, ln_g, ln_b, *[prev[kd] for kd in carried])
    return {**prev, **dict(zip(kinds, outs))}


MLA_TM = 512


def _mla_proj_kernel(x_ref, mods_ref, wdq_ref, wdkv_ref, wkr_ref, qn_ref, kvn_ref, wuq_ref,
                     wukn_ref, wuv_ref, ta_ref, tb_ref, q_ref, k_ref, v_ref, ckv_ref, kr_ref):
    m = mods_ref[0]
    sm, cm = m[0:1], m[1:2]
    h = (x_ref[...] * (1.0 + cm) + sm).astype(BF16)
    cq = _rms_norm(_dot(h, wdq_ref[...]), qn_ref[...])
    ckv = _rms_norm(_dot(h, wdkv_ref[...]), kvn_ref[...])
    kr2 = _dot(h, wkr_ref[...])
    ckv_ref[...] = ckv
    kr_ref[...] = kr2

    ka = ta_ref[0]
    kb = tb_ref[0]
    tm = ka.shape[0]
    ta = jnp.concatenate([jnp.full((tm, QK_NOPE), Q_PRESCALE, F32), ka * Q_PRESCALE], axis=1)
    tb = jnp.concatenate([jnp.zeros((tm, QK_NOPE), F32), kb * Q_PRESCALE], axis=1)
    krr = kr2 * ka + pltpu.roll(kr2, QK_ROPE, 1) * kb

    qpre = _dot(cq.astype(BF16), wuq_ref[...])
    ckv_b = ckv.astype(BF16)
    kn = _dot(ckv_b, wukn_ref[...])
    v_ref[...] = _dot(ckv_b, wuv_ref[...]).astype(BF16)
    for hd in range(N_HEADS):
        qh = qpre[:, hd * HEAD_PAD:(hd + 1) * HEAD_PAD]
        qrot = qh * ta + pltpu.roll(qh, HEAD_PAD - QK_ROPE, 1) * tb
        q_ref[:, hd * HEAD_PAD:(hd + 1) * HEAD_PAD] = qrot.astype(BF16)
        k_ref[:, hd * HEAD_PAD:hd * HEAD_PAD + QK_NOPE] = kn[:, hd * QK_NOPE:(hd + 1) * QK_NOPE].astype(BF16)
        k_ref[:, hd * HEAD_PAD + QK_NOPE:(hd + 1) * HEAD_PAD] = krr.astype(BF16)


def _mla_proj(x, mods, p, rope_a, rope_b):
    t = x.shape[0]
    tm = MLA_TM
    full = lambda shape: pl.BlockSpec(shape, lambda i: (0,) * len(shape))
    rope_spec = pl.BlockSpec(
        (1, tm, 2 * QK_ROPE),
        lambda i: (jnp.minimum(i * tm // GROUP_ROWS, 1), (i * tm % GROUP_ROWS) // tm, 0))
    return pl.pallas_call(
        _mla_proj_kernel,
        grid=(t // tm,),
        in_specs=[
            pl.BlockSpec((tm, D_MODEL), lambda i: (i, 0)),
            pl.BlockSpec((1, 6, D_MODEL), lambda i: (i * tm // GROUP_ROWS, 0, 0)),
            full((D_MODEL, Q_LORA)), full((D_MODEL, KV_LORA)), full((D_MODEL, 2 * QK_ROPE)),
            full((1, Q_LORA)), full((1, KV_LORA)),
            full((Q_LORA, N_HEADS * HEAD_PAD)),
            full((KV_LORA, N_HEADS * QK_NOPE)), full((KV_LORA, N_HEADS * V_DIM)),
            rope_spec, rope_spec,
        ],
        out_specs=[
            pl.BlockSpec((tm, N_HEADS * HEAD_PAD), lambda i: (i, 0)),
            pl.BlockSpec((tm, N_HEADS * HEAD_PAD), lambda i: (i, 0)),
            pl.BlockSpec((tm, N_HEADS * V_DIM), lambda i: (i, 0)),
            pl.BlockSpec((tm, KV_LORA), lambda i: (i, 0)),
            pl.BlockSpec((tm, 2 * QK_ROPE), lambda i: (i, 0)),
        ],
        out_shape=[
            jax.ShapeDtypeStruct((t, N_HEADS * HEAD_PAD), BF16),
            jax.ShapeDtypeStruct((t, N_HEADS * HEAD_PAD), BF16),
            jax.ShapeDtypeStruct((t, N_HEADS * V_DIM), BF16),
            jax.ShapeDtypeStruct((t, KV_LORA), F32),
            jax.ShapeDtypeStruct((t, 2 * QK_ROPE), F32),
        ],
        compiler_params=pltpu.CompilerParams(dimension_semantics=("parallel",)),
        name="mla_proj",
    )(x, mods, p["w_dq"], p["w_dkv"], p["w_kr"], p["q_norm"], p["kv_norm"], p["w_uq"],
      p["w_ukn"], p["w_uv"], rope_a, rope_b)


def _cache_kv_kernel(ckv_ref, kr_ref, wukn_ref, wuv_ref, k_ref, v_ref):
    ckv_b = ckv_ref[...].astype(BF16)
    kn = _dot(ckv_b, wukn_ref[...])
    v_ref[...] = _dot(ckv_b, wuv_ref[...]).astype(BF16)
    kr = kr_ref[...].astype(BF16)
    for hd in range(N_HEADS):
        k_ref[:, hd * HEAD_PAD:hd * HEAD_PAD + QK_NOPE] = kn[:, hd * QK_NOPE:(hd + 1) * QK_NOPE].astype(BF16)
        k_ref[:, hd * HEAD_PAD + QK_NOPE:(hd + 1) * HEAD_PAD] = kr


def _cache_kv(ckv, kr_pad, p):
    t = ckv.shape[0]
    tm = PAST_LEN
    full = lambda shape: pl.BlockSpec(shape, lambda i: (0,) * len(shape))
    return pl.pallas_call(
        _cache_kv_kernel,
        grid=(t // tm,),
        in_specs=[
            pl.BlockSpec((tm, KV_LORA), lambda i: (i, 0)),
            pl.BlockSpec((tm, 2 * QK_ROPE), lambda i: (i, 0)),
            full((KV_LORA, N_HEADS * QK_NOPE)), full((KV_LORA, N_HEADS * V_DIM)),
        ],
        out_specs=[
            pl.BlockSpec((tm, N_HEADS * HEAD_PAD), lambda i: (i, 0)),
            pl.BlockSpec((tm, N_HEADS * V_DIM), lambda i: (i, 0)),
        ],
        out_shape=[
            jax.ShapeDtypeStruct((t, N_HEADS * HEAD_PAD), BF16),
            jax.ShapeDtypeStruct((t, N_HEADS * V_DIM), BF16),
        ],
        compiler_params=pltpu.CompilerParams(dimension_semantics=("parallel",)),
        name="mla_cache_kv",
    )(ckv, kr_pad, p["w_ukn"], p["w_uv"])


def _ctx_attn_kernel(q_ref, k_ref, v_ref, o_ref):
    for hd in range(N_HEADS):
        q = q_ref[:, hd * HEAD_PAD:(hd + 1) * HEAD_PAD]
        k = k_ref[:, hd * HEAD_PAD:(hd + 1) * HEAD_PAD]
        s = _dot_nt(q, k)
        s = s - jnp.max(s, axis=-1, keepdims=True)
        p = jnp.exp2(s)
        p = p / jnp.sum(p, axis=-1, keepdims=True)
        o = _dot(p.astype(BF16), v_ref[:, hd * V_DIM:(hd + 1) * V_DIM])
        o_ref[:, hd * V_DIM:(hd + 1) * V_DIM] = o.astype(BF16)


def _ctx_attention(q, k, v):
    return pl.pallas_call(
        _ctx_attn_kernel,
        grid=(BATCH,),
        in_specs=[
            pl.BlockSpec((SEQ, N_HEADS * HEAD_PAD), lambda b: (b, 0)),
            pl.BlockSpec((SEQ, N_HEADS * HEAD_PAD), lambda b: (b, 0)),
            pl.BlockSpec((SEQ, N_HEADS * V_DIM), lambda b: (b, 0)),
        ],
        out_specs=pl.BlockSpec((SEQ, N_HEADS * V_DIM), lambda b: (b, 0)),
        out_shape=jax.ShapeDtypeStruct((T_CTX, N_HEADS * V_DIM), BF16),
        compiler_params=pltpu.CompilerParams(dimension_semantics=("parallel",)),
        name="ctx_attention",
    )(q, k, v)


LAT_TQ = 1024
LAT_TK = 512
LAT_PIECES = 4


def _lat_attn_kernel(q_ref, k_ref, v_ref, kc_ref, vc_ref, o_ref, s_ref, p_ref, m_ref):
    n_chunks = DEC_SEQ // LAT_TK + 1
    tp = q_ref.shape[0] // LAT_PIECES
    pieces = [pl.ds(j * tp, tp) for j in range(LAT_PIECES)]
    groups = [pieces[:2], pieces[2:]]
    state = {}

    def keys(c):
        return kc_ref[...] if c == n_chunks - 1 else k_ref[c * LAT_TK:(c + 1) * LAT_TK, :]

    def values(c):
        return vc_ref[...] if c == n_chunks - 1 else v_ref[c * LAT_TK:(c + 1) * LAT_TK, :]

    def qk(r, c):
        s = _dot_nt(q_ref[r, :], keys(c))
        s_ref[r, c * LAT_TK:(c + 1) * LAT_TK] = s
        mp = state.get(("m", r.start), jnp.full((tp, LANES), NEG_INF, F32))
        for j in range(LAT_TK // LANES):
            mp = jnp.maximum(mp, s[:, j * LANES:(j + 1) * LANES])
        state[("m", r.start)] = mp

    def row_max(r):
        m_ref[r, :] = jnp.broadcast_to(jnp.max(state[("m", r.start)], axis=-1, keepdims=True), (tp, LANES))

    def exp_chunk(r, c, after=None):
        lp = state.get(("l", r.start), jnp.zeros((tp, LANES), F32))
        m = m_ref[r, :]
        if after is not None:
            bits = lax.bitcast_convert_type(after, jnp.uint32)
            zero = lax.shift_right_logical(lax.shift_right_logical(bits, jnp.uint32(16)), jnp.uint32(16))
            m = m + lax.bitcast_convert_type(zero, F32)
        for j in range(c * LAT_TK // LANES, (c + 1) * LAT_TK // LANES):
            p = jnp.exp2(s_ref[r, j * LANES:(j + 1) * LANES] - m)
            lp = lp + p
            p_ref[r, j * LANES:(j + 1) * LANES] = p.astype(BF16)
        state[("l", r.start)] = lp

    def pv(r, c):
        acc = state.get(("a", r.start), jnp.zeros((tp, V_DIM), F32))
        state[("a", r.start)] = acc + _dot(p_ref[r, c * LAT_TK:(c + 1) * LAT_TK], values(c))

    def finish(r):
        o_ref[r, :] = (state[("a", r.start)] / jnp.sum(state[("l", r.start)], axis=-1, keepdims=True)).astype(BF16)

    for c in range(n_chunks):
        for r in groups[0]:
            qk(r, c)
    for r in groups[0]:
        row_max(r)
    for c in range(n_chunks):
        for r in groups[1]:
            qk(r, c)
        for r in groups[0]:
            exp_chunk(r, c)
    for r in groups[1]:
        row_max(r)
    for c in range(n_chunks):
        for r in groups[0]:
            pv(r, c)
        for r0, r in zip(groups[0], groups[1]):
            exp_chunk(r, c, after=state[("a", r0.start)])
    for r in groups[0]:
        finish(r)
    for c in range(n_chunks):
        for r in groups[1]:
            pv(r, c)
    for r in groups[1]:
        finish(r)


def _lat_attention(q, k, v, kc, vc):
    nq = DEC_SEQ // LAT_TQ
    return pl.pallas_call(
        _lat_attn_kernel,
        grid=(DEC_BATCH, N_HEADS, nq),
        in_specs=[
            pl.BlockSpec((LAT_TQ, HEAD_PAD), lambda b, h, i: ((b + 1) * nq + i, h)),
            pl.BlockSpec((DEC_SEQ, HEAD_PAD), lambda b, h, i: (b + 1, h)),
            pl.BlockSpec((DEC_SEQ, V_DIM), lambda b, h, i: (b + 1, h)),
            pl.BlockSpec((PAST_LEN, HEAD_PAD), lambda b, h, i: (b, h)),
            pl.BlockSpec((PAST_LEN, V_DIM), lambda b, h, i: (b, h)),
        ],
        out_specs=pl.BlockSpec((LAT_TQ, V_DIM), lambda b, h, i: (b * nq + i, h)),
        out_shape=jax.ShapeDtypeStruct((T_LAT, N_HEADS * V_DIM), BF16),
        scratch_shapes=[pltpu.VMEM((LAT_TQ, DEC_SEQ + PAST_LEN), F32),
                        pltpu.VMEM((LAT_TQ, DEC_SEQ + PAST_LEN), BF16),
                        pltpu.VMEM((LAT_TQ, LANES), F32)],
        compiler_params=pltpu.CompilerParams(
            dimension_semantics=("parallel", "parallel", "parallel")),
        name="lat_attention",
    )(q, k, v, kc, vc)


OPROJ_TM = 512


def _oproj_kernel(oc_ref, ol_ref, x_ref, mods_ref, wo_ref, g_ref, b_ref, out_ref):
    is_ctx = pl.program_id(0) < T_CTX // OPROJ_TM
    o = jnp.where(is_ctx, oc_ref[...], ol_ref[...])
    gm = mods_ref[0][2:3]
    y = _dot(o, wo_ref[...])
    out_ref[...] = _layer_norm(ALPHA * x_ref[...] + gm * y, g_ref[...], b_ref[...])


def _oproj(o_ctx, o_lat, x, mods, w_o, ln_g, ln_b):
    t = x.shape[0]
    tm = OPROJ_TM
    n_ctx = T_CTX // tm
    return pl.pallas_call(
        _oproj_kernel,
        grid=(t // tm,),
        in_specs=[
            pl.BlockSpec((tm, N_HEADS * V_DIM), lambda i: (jnp.minimum(i, n_ctx - 1), 0)),
            pl.BlockSpec((tm, N_HEADS * V_DIM), lambda i: (jnp.maximum(i - n_ctx, 0), 0)),
            pl.BlockSpec((tm, D_MODEL), lambda i: (i, 0)),
            pl.BlockSpec((1, 6, D_MODEL), lambda i: (i * tm // GROUP_ROWS, 0, 0)),
            pl.BlockSpec((N_HEADS * V_DIM, D_MODEL), lambda i: (0, 0)),
            pl.BlockSpec((1, D_MODEL), lambda i: (0, 0)),
            pl.BlockSpec((1, D_MODEL), lambda i: (0, 0)),
        ],
        out_specs=pl.BlockSpec((tm, D_MODEL), lambda i: (i, 0)),
        out_shape=jax.ShapeDtypeStruct((t, D_MODEL), F32),
        compiler_params=pltpu.CompilerParams(dimension_semantics=("parallel",)),
        name="attn_oproj",
    )(o_ctx, o_lat, x, mods, w_o, ln_g, ln_b)


def _swap16(w):
    q = QK_ROPE // 4
    return jnp.concatenate([w[..., q:2 * q], w[..., :q], w[..., 3 * q:], w[..., 2 * q:3 * q]], axis=-1)


def _mla_params(w_dqkv, q_norm, w_uq, kv_norm, w_ukv):
    w_kr = w_dqkv[:, Q_LORA + KV_LORA:]
    wq = w_uq.reshape(Q_LORA, N_HEADS, QK_NOPE + QK_ROPE)
    wq_r = wq[..., QK_NOPE:]
    wq = jnp.concatenate([wq[..., :QK_NOPE], wq_r, _swap16(wq_r)], axis=-1)
    wkv = w_ukv.reshape(KV_LORA, N_HEADS, QK_NOPE + V_DIM)
    return {
        "w_dq": w_dqkv[:, :Q_LORA].astype(BF16),
        "w_dkv": w_dqkv[:, Q_LORA:Q_LORA + KV_LORA].astype(BF16),
        "w_kr": jnp.concatenate([w_kr, _swap16(w_kr)], axis=-1).astype(BF16),
        "q_norm": q_norm.reshape(1, Q_LORA),
        "kv_norm": kv_norm.reshape(1, KV_LORA),
        "w_uq": wq.reshape(Q_LORA, N_HEADS * HEAD_PAD).astype(BF16),
        "w_ukn": wkv[..., :QK_NOPE].reshape(KV_LORA, N_HEADS * QK_NOPE).astype(BF16),
        "w_uv": wkv[..., QK_NOPE:].reshape(KV_LORA, N_HEADS * V_DIM).astype(BF16),
    }


def _rope_tables():
    nf = QK_ROPE // 4
    t = np.arange(DEC_SEQ)
    row = (t // GRID_W).astype(np.float32)
    col = (t % GRID_W).astype(np.float32)
    inv = (ROPE_THETA ** (-np.arange(nf, dtype=np.float32) / nf)).astype(np.float32)
    ar, ac = row[:, None] * inv, col[:, None] * inv
    pad = np.zeros((DEC_SEQ, QK_ROPE), np.float32)
    cos = np.concatenate([np.cos(ar), np.cos(ar), np.cos(ac), np.cos(ac), pad], axis=-1)
    sin = np.concatenate([-np.sin(ar), np.sin(ar), -np.sin(ac), np.sin(ac), pad], axis=-1)
    cos_id = np.concatenate([np.ones((DEC_SEQ, QK_ROPE), np.float32), pad], axis=-1)
    sin_id = np.zeros((DEC_SEQ, 2 * QK_ROPE), np.float32)
    return (jnp.asarray(np.stack([cos_id, cos]).astype(np.float32)),
            jnp.asarray(np.stack([sin_id, sin]).astype(np.float32)))


def kernel(x_prompt, x_sample, cache_ckv, cache_krope, c, c_ctx, ada_w, ada_b, ln_g, ln_b, conv_w_in, conv_k, conv_w_out, mla_w_dqkv, mla_q_norm, mla_w_uq, mla_kv_norm, mla_w_ukv, mla_w_o, router_w, router_bias, exp_w_gate, exp_w_up, exp_w_down, sh_w_gate, sh_w_up, sh_w_down):
    cvecs =jnp.concatenate([c_ctx[None, :], c, jnp.zeros((SUBLANES - N_GROUPS_ROWS, D_MODEL), F32)], axis=0)
    mods = _adaln(cvecs, ada_w, ada_b)

    def ln(l, k):
        return ln_g[l, k].reshape(1, D_MODEL), ln_b[l, k].reshape(1, D_MODEL)

    def moe_layer(xin, l, split_streams):
        idx, pos, wcol, counts, hp_a, hp_b = _router(xin, mods[l], router_w[l].T,
                                                     router_bias[l].reshape(N_EXPERTS, 1))
        shared = (sh_w_gate[l].astype(BF16), sh_w_up[l].astype(BF16), sh_w_down[l].astype(BF16))
        outs = {}
        for ch in range(MOE_CHUNKS):
            row0 = ch * T_CHUNK
            cnt = counts[ch, :, 0].astype(jnp.int32)
            start, tile_expert, tile_valid, n_used = _expert_layout(cnt)
            dest = _dest_rows(start, idx[:, row0:row0 + T_CHUNK], pos[:, row0:row0 + T_CHUNK])
            xs_a = _sc_scatter_rows(hp_a, dest, row0)
            xs_b = _sc_scatter_rows(hp_b, dest, row0)
            ys_a, ys_b = _expert_ffn(xs_a, xs_b, tile_expert, tile_valid, n_used,
                                     exp_w_gate, exp_w_up, exp_w_down, l)
            dest_row = dest.reshape(1, N_SLOTS)
            g_a = _sc_gather_rows(ys_a, dest_row).reshape(TOP_K, T_CHUNK, D_QUARTER)
            g_b = _sc_gather_rows(ys_b, dest_row).reshape(TOP_K, T_CHUNK, D_QUARTER)
            outs = _combine(xin, mods[l], wcol, g_a, g_b, *shared, *ln(l, 1), chunk=ch,
                            split_streams=split_streams, prev=outs)
        return (outs["ctx"], outs["lat"]) if split_streams else outs["all"]

    x = _conv_mixer(x_prompt.reshape(T_CTX, D_MODEL), x_sample.reshape(T_LAT, D_MODEL), mods[0],
                    conv_w_in[0].astype(BF16), conv_k[0], conv_w_out[0].astype(BF16), *ln(0, 0))
    x = moe_layer(x, 0, split_streams=False)

    p = _mla_params(mla_w_dqkv[0], mla_q_norm[0], mla_w_uq[0], mla_kv_norm[0], mla_w_ukv[0])
    rope_a, rope_b = _rope_tables()
    q, k, v, ckv, kr = _mla_proj(x, mods[1], p, rope_a, rope_b)
    kr_cache = jnp.concatenate([cache_krope[:, 0].reshape(DEC_BATCH * PAST_LEN, QK_ROPE),
                                jnp.zeros((DEC_BATCH * PAST_LEN, QK_ROPE), F32)], axis=-1)
    kc, vc = _cache_kv(cache_ckv[:, 0].reshape(DEC_BATCH * PAST_LEN, KV_LORA), kr_cache, p)
    o_ctx = _ctx_attention(q, k, v)
    o_lat = _lat_attention(q, k, v, kc, vc)
    x = _oproj(o_ctx, o_lat, x, mods[1], mla_w_o[0].astype(BF16), *ln(1, 0))
    y_ctx, y_lat = moe_layer(x, 1, split_streams=True)

    y_prompt = y_ctx.reshape(BATCH, SEQ, D_MODEL)
    y_sample = y_lat.reshape(DEC_BATCH, DEC_SEQ, D_MODEL)
    state_ckv = ckv[:T_CTX].reshape(BATCH, 1, SEQ, KV_LORA)
    state_krope = kr[:T_CTX, :QK_ROPE].reshape(BATCH, 1, SEQ, QK_ROPE)
    return (y_prompt, y_sample, state_ckv, state_krope)
```

```python
import functools
import math

import jax
import jax.numpy as jnp
import numpy as np
from jax import lax
from jax.experimental import pallas as pl
from jax.experimental.pallas import tpu as pltpu
from jax.experimental.pallas import tpu_sc as plsc

D_MODEL = 1024
BATCH = 16
SEQ = 256
DEPTH = 2
DEC_BATCH = 4
DEC_SEQ = 4096
PAST_LEN = 512
GRID_W = 64

N_HEADS = 8
QK_NOPE = 128
QK_ROPE = 64
V_DIM = 128
Q_LORA = 384
KV_LORA = 256
ROPE_THETA = 10000.0
ATTN_SCALE = (QK_NOPE + QK_ROPE) ** -0.5
HEAD_PAD = 256
Q_PRESCALE = ATTN_SCALE * math.log2(math.e)

N_EXPERTS = 64
TOP_K = 8
N_GROUPS = 8
TOPK_GROUPS = 4
GROUP_SIZE = N_EXPERTS // N_GROUPS
D_EXPERT = 256
D_SHARED = 256
ROUTED_SCALE = 2.5

ALPHA = (2 * DEPTH) ** 0.25
LN_EPS = 1e-5
RMS_EPS = 1e-6

GROUP_ROWS = 4096
N_GROUPS_ROWS = 1 + DEC_BATCH
T_CTX = BATCH * SEQ
T_LAT = DEC_BATCH * DEC_SEQ
T_ALL = T_CTX + T_LAT
LANES = 128
SUBLANES = 8

F32 = jnp.float32
BF16 = jnp.bfloat16
NEG_INF = float("-inf")


def _dot(a, b):
    return jnp.dot(a, b, preferred_element_type=F32)


def _dot_nt(a, b, precision=None):
    return lax.dot_general(a, b, (((1,), (1,)), ((), ())), precision=precision,
                           preferred_element_type=F32)


def _layer_norm(v, g, b):
    mu = jnp.mean(v, axis=-1, keepdims=True)
    d = v - mu
    var = jnp.mean(d * d, axis=-1, keepdims=True)
    return d * lax.rsqrt(var + LN_EPS) * g + b


def _rms_norm(v, g):
    return v * lax.rsqrt(jnp.mean(v * v, axis=-1, keepdims=True) + RMS_EPS) * g


def _silu(v):
    return v / (1.0 + jnp.exp(-v))


def _sigmoid(v):
    return 1.0 / (1.0 + jnp.exp(-v))


def _adaln_kernel(c_ref, w_ref, b_ref, o_ref):
    c = c_ref[...]
    s = _silu(c)
    o_ref[0, 0] = jnp.dot(s, w_ref[0], precision=lax.Precision.HIGHEST,
                          preferred_element_type=F32) + b_ref[0]


def _adaln(cvecs, ada_w, ada_b):
    out = pl.pallas_call(
        _adaln_kernel,
        grid=(DEPTH, 6),
        in_specs=[
            pl.BlockSpec((SUBLANES, D_MODEL), lambda l, j: (0, 0)),
            pl.BlockSpec((1, D_MODEL, D_MODEL), lambda l, j: (l, 0, j)),
            pl.BlockSpec((1, 1, D_MODEL), lambda l, j: (l, 0, j)),
        ],
        out_specs=pl.BlockSpec((1, 1, SUBLANES, D_MODEL), lambda l, j: (l, j, 0, 0)),
        out_shape=jax.ShapeDtypeStruct((DEPTH, 6, SUBLANES, D_MODEL), F32),
        compiler_params=pltpu.CompilerParams(dimension_semantics=("parallel", "parallel")),
        name="adaln",
    )(cvecs, ada_w, ada_b.reshape(DEPTH, 1, 6 * D_MODEL))
    return jnp.transpose(out[:, :, :N_GROUPS_ROWS, :], (0, 2, 1, 3))


CONV_TM = 512


def _conv_kernel(xc_ref, xcp_ref, xcn_ref, xl_ref, xlp_ref, xln_ref, mods_ref, win_ref, ck_ref, wout_ref,
                 g_ref, b_ref, o_ref):
    i = pl.program_id(0)
    tm = xc_ref.shape[0]
    is_ctx = i < T_CTX // tm
    m = mods_ref[0]
    sm, cm, gm = m[0:1], m[1:2], m[2:3]
    x = jnp.where(is_ctx, xc_ref[...], xl_ref[...])
    xp = jnp.where(is_ctx, xcp_ref[...], xlp_ref[...])
    xn = jnp.where(is_ctx, xcn_ref[...], xln_ref[...])
    h = (x * (1.0 + cm) + sm).astype(BF16)
    z = _dot(h, win_ref[...])
    bg = z[:, :D_MODEL]
    u = z[:, D_MODEL:2 * D_MODEL] * z[:, 2 * D_MODEL:]

    def halo_u(xh):
        hh = (xh * (1.0 + cm) + sm).astype(BF16)
        zh = _dot(hh, win_ref[:, D_MODEL:])
        return zh[:, :D_MODEL] * zh[:, D_MODEL:]

    u_prev = halo_u(xp)[SUBLANES - 1:SUBLANES]
    u_next = halo_u(xn)[0:1]

    row = lax.broadcasted_iota(jnp.int32, (tm, 1), 0)
    grow = i * tm + row
    seq_len = jnp.where(grow < T_CTX, SEQ, DEC_SEQ)
    pos = jnp.bitwise_and(grow, seq_len - 1)
    is_first = pos == 0
    is_last = pos == seq_len - 1

    left = pltpu.roll(u, 1, 0)
    left = jnp.where(row == 0, u_prev, left)
    left = jnp.where(is_first, 0.0, left)
    right = pltpu.roll(u, tm - 1, 0)
    right = jnp.where(row == tm - 1, u_next, right)
    right = jnp.where(is_last, 0.0, right)

    ck = ck_ref[...]
    conv = left * ck[0:1] + u * ck[1:2] + right * ck[2:3]
    v = (bg * conv).astype(BF16)
    y = _dot(v, wout_ref[...])
    o_ref[...] = _layer_norm(ALPHA * x + gm * y, g_ref[...], b_ref[...])


def _conv_mixer(x_ctx, x_lat, mods, w_in, conv_k, w_out, ln_g, ln_b):
    tm = CONV_TM
    per8 = tm // SUBLANES
    n_ctx = T_CTX // tm
    n_lat = T_LAT // tm

    def stream_specs(first, n):
        def blk(i):
            return jnp.clip(i - first, 0, n - 1)
        return [
            pl.BlockSpec((tm, D_MODEL), lambda i: (blk(i), 0)),
            pl.BlockSpec((SUBLANES, D_MODEL), lambda i: (jnp.maximum(blk(i) * per8 - 1, 0), 0)),
            pl.BlockSpec((SUBLANES, D_MODEL), lambda i: (jnp.minimum((blk(i) + 1) * per8, n * per8 - 1), 0)),
        ]

    return pl.pallas_call(
        _conv_kernel,
        grid=(n_ctx + n_lat,),
        in_specs=stream_specs(0, n_ctx) + stream_specs(n_ctx, n_lat) + [
            pl.BlockSpec((1, 6, D_MODEL), lambda i: (i * tm // GROUP_ROWS, 0, 0)),
            pl.BlockSpec((D_MODEL, 3 * D_MODEL), lambda i: (0, 0)),
            pl.BlockSpec((3, D_MODEL), lambda i: (0, 0)),
            pl.BlockSpec((D_MODEL, D_MODEL), lambda i: (0, 0)),
            pl.BlockSpec((1, D_MODEL), lambda i: (0, 0)),
            pl.BlockSpec((1, D_MODEL), lambda i: (0, 0)),
        ],
        out_specs=pl.BlockSpec((tm, D_MODEL), lambda i: (i, 0)),
        out_shape=jax.ShapeDtypeStruct((T_ALL, D_MODEL), F32),
        compiler_params=pltpu.CompilerParams(dimension_semantics=("parallel",)),
        name="conv_mixer",
    )(x_ctx, x_ctx, x_ctx, x_lat, x_lat, x_lat, mods, w_in, conv_k, w_out, ln_g, ln_b)


ROUTER_TM = 512
MOE_CHUNKS = ((0, T_CTX), (T_CTX, T_LAT))


def _first_argmax_mask(cur, ridx, n):
    mx = jnp.max(cur, axis=0, keepdims=True)
    first = jnp.min(jnp.where(cur == mx, ridx, n), axis=0, keepdims=True)
    return ridx == first, mx


def _router_kernel(x_ref, mods_ref, rwt_ref, bias_ref, idx_ref, pos_ref, wcol_ref, count_ref, hpa_ref, hpb_ref,
                   carry_ref):
    tm = x_ref.shape[0]
    m = mods_ref[0]
    sf, cf = m[3:4], m[4:5]
    hc = x_ref[...] * (1.0 + cf) + sf
    words = _pack_rows(hc)
    hpa_ref[...] = words[:, :D_QUARTER]
    hpb_ref[...] = words[:, D_QUARTER:]
    logits = _dot_nt(rwt_ref[...], hc, precision=lax.Precision.HIGHEST)
    scores = _sigmoid(logits)
    biased = scores + bias_ref[...]

    ridx8 = lax.broadcasted_iota(jnp.int32, (GROUP_SIZE, tm), 0)
    gscore = jnp.full((N_GROUPS, tm), NEG_INF, F32)
    for g in range(N_GROUPS):
        blk = biased[g * GROUP_SIZE:(g + 1) * GROUP_SIZE]
        sel, m1 = _first_argmax_mask(blk, ridx8, GROUP_SIZE)
        m2 = jnp.max(jnp.where(sel, NEG_INF, blk), axis=0, keepdims=True)
        gscore = jnp.where(ridx8 == g, m1 + m2, gscore)

    gmask = jnp.zeros((N_GROUPS, tm), jnp.bool_)
    cur = gscore
    for _ in range(TOPK_GROUPS):
        sel, _unused = _first_argmax_mask(cur, ridx8, N_GROUPS)
        gmask = jnp.logical_or(gmask, sel)
        cur = jnp.where(sel, NEG_INF, cur)

    gmask_f = gmask.astype(F32)
    blocks = []
    for g in range(N_GROUPS):
        keep = jnp.broadcast_to(gmask_f[g:g + 1], (GROUP_SIZE, tm)) > 0.5
        blocks.append(jnp.where(keep, biased[g * GROUP_SIZE:(g + 1) * GROUP_SIZE], NEG_INF))
    cur = jnp.concatenate(blocks, axis=0)

    first_tiles = [row0 // tm for row0, _rows in MOE_CHUNKS]
    starts_chunk = functools.reduce(jnp.logical_or, [pl.program_id(0) == ft for ft in first_tiles])

    @pl.when(starts_chunk)
    def _():
        carry_ref[...] = jnp.zeros(carry_ref.shape, F32)

    ridx = lax.broadcasted_iota(jnp.int32, (N_EXPERTS, tm), 0)
    kidx = lax.broadcasted_iota(jnp.int32, (TOP_K, tm), 0)
    sels = []
    chosen = jnp.zeros((N_EXPERTS, tm), jnp.bool_)
    idx_rows = jnp.zeros((TOP_K, tm), jnp.int32)
    for k in range(TOP_K):
        mx = jnp.max(cur, axis=0, keepdims=True)
        first = jnp.min(jnp.where(cur == mx, ridx, N_EXPERTS), axis=0, keepdims=True)
        sel = ridx == first
        sels.append(sel)
        chosen = jnp.logical_or(chosen, sel)
        idx_rows = jnp.where(kidx == k, first, idx_rows)
        cur = jnp.where(sel, NEG_INF, cur)

    onehot = chosen.astype(F32)
    t_row = lax.broadcasted_iota(jnp.int32, (tm, tm), 0)
    t_col = lax.broadcasted_iota(jnp.int32, (tm, tm), 1)
    before = (t_row < t_col).astype(BF16)
    rank = carry_ref[...] + _dot(onehot.astype(BF16), before)
    carry_ref[...] = carry_ref[...] + jnp.sum(onehot, axis=1, keepdims=True)
    count_ref[0] = jnp.broadcast_to(carry_ref[...], count_ref.shape[1:])

    w = jnp.where(chosen, scores, 0.0)
    w = w / jnp.sum(w, axis=0, keepdims=True) * ROUTED_SCALE
    pos_rows = jnp.zeros((TOP_K, tm), F32)
    w_rows = jnp.zeros((TOP_K, tm), F32)
    for k in range(TOP_K):
        pos_rows = jnp.where(kidx == k, jnp.sum(jnp.where(sels[k], rank, 0.0), axis=0, keepdims=True), pos_rows)
        w_rows = jnp.where(kidx == k, jnp.sum(jnp.where(sels[k], w, 0.0), axis=0, keepdims=True), w_rows)
    idx_ref[...] = idx_rows
    pos_ref[...] = pos_rows.astype(jnp.int32)
    wpad = jnp.concatenate([w_rows, jnp.zeros((LANES - TOP_K, tm), F32)], axis=0)
    wcol_ref[...] = wpad.T


def _router(x, mods, router_wt, router_bias):
    t = x.shape[0]
    tm = ROUTER_TM

    def chunk_of(i):
        return sum((i >= row0 // tm).astype(jnp.int32) for row0, _rows in MOE_CHUNKS[1:])

    return pl.pallas_call(
        _router_kernel,
        grid=(t // tm,),
        in_specs=[
            pl.BlockSpec((tm, D_MODEL), lambda i: (i, 0)),
            pl.BlockSpec((1, 6, D_MODEL), lambda i: (i * tm // GROUP_ROWS, 0, 0)),
            pl.BlockSpec((N_EXPERTS, D_MODEL), lambda i: (0, 0)),
            pl.BlockSpec((N_EXPERTS, 1), lambda i: (0, 0)),
        ],
        out_specs=[
            pl.BlockSpec((TOP_K, tm), lambda i: (0, i)),
            pl.BlockSpec((TOP_K, tm), lambda i: (0, i)),
            pl.BlockSpec((tm, LANES), lambda i: (i, 0)),
            pl.BlockSpec((1, N_EXPERTS, LANES), lambda i: (chunk_of(i), 0, 0)),
            pl.BlockSpec((tm, D_QUARTER), lambda i: (i, 0)),
            pl.BlockSpec((tm, D_QUARTER), lambda i: (i, 0)),
        ],
        out_shape=[
            jax.ShapeDtypeStruct((TOP_K, t), jnp.int32),
            jax.ShapeDtypeStruct((TOP_K, t), jnp.int32),
            jax.ShapeDtypeStruct((t, LANES), F32),
            jax.ShapeDtypeStruct((len(MOE_CHUNKS), N_EXPERTS, LANES), F32),
            jax.ShapeDtypeStruct((t, D_QUARTER), jnp.uint32),
            jax.ShapeDtypeStruct((t, D_QUARTER), jnp.uint32),
        ],
        scratch_shapes=[pltpu.VMEM((N_EXPERTS, 1), F32)],
        compiler_params=pltpu.CompilerParams(dimension_semantics=("arbitrary",)),
        name="moe_router",
    )(x, mods, router_wt, router_bias)


FFN_TM = 512


def _n_tiles(rows):
    return rows * TOP_K // FFN_TM + N_EXPERTS


def _expert_layout(counts, n_tiles):
    padded = (counts + FFN_TM - 1) // FFN_TM * FFN_TM
    end = jnp.cumsum(padded)
    start = end - padded
    tile_row = jnp.arange(n_tiles, dtype=jnp.int32) * FFN_TM
    tile_expert = jnp.minimum(jnp.sum(end[None, :] <= tile_row[:, None], axis=1), N_EXPERTS - 1)
    of_tile = tile_expert[:, None] == jnp.arange(N_EXPERTS, dtype=jnp.int32)[None, :]
    live_end = jnp.sum(jnp.where(of_tile, (start + counts)[None, :], 0), axis=1)
    tile_valid = jnp.clip(live_end - tile_row, 0, FFN_TM)
    n_used = (end[-1] // FFN_TM).astype(jnp.int32).reshape(1)
    return start.astype(jnp.int32), tile_expert.astype(jnp.int32), tile_valid.astype(jnp.int32), n_used


D_HALF = D_MODEL // 2
D_QUARTER = D_MODEL // 4


def _pack_rows(v):
    hi = lax.bitcast_convert_type(v[:, :D_HALF].astype(BF16).astype(F32), jnp.uint32)
    lo = lax.bitcast_convert_type(v[:, D_HALF:].astype(BF16).astype(F32), jnp.uint32)
    return jnp.bitwise_or(hi, jnp.right_shift(lo, jnp.uint32(16)))


def _unpack_rows(w):
    hi = lax.bitcast_convert_type(jnp.bitwise_and(w, jnp.uint32(0xFFFF0000)), F32)
    lo = lax.bitcast_convert_type(jnp.left_shift(w, jnp.uint32(16)), F32)
    return hi, lo


DEST_TM = 2048


def _dest_kernel(start_ref, idx_ref, pos_ref, dest_ref):
    idx = idx_ref[...]
    base = jnp.zeros(idx.shape, jnp.int32)
    for e in range(N_EXPERTS):
        base = jnp.where(idx == e, start_ref[e], base)
    dest_ref[...] = base + pos_ref[...]


def _dest_rows(start, idx, pos):
    t = idx.shape[1]
    return pl.pallas_call(
        _dest_kernel,
        grid_spec=pltpu.PrefetchScalarGridSpec(
            num_scalar_prefetch=1,
            grid=(t // DEST_TM,),
            in_specs=[pl.BlockSpec((TOP_K, DEST_TM), lambda i, s: (0, i)),
                      pl.BlockSpec((TOP_K, DEST_TM), lambda i, s: (0, i))],
            out_specs=pl.BlockSpec((TOP_K, DEST_TM), lambda i, s: (0, i)),
        ),
        out_shape=jax.ShapeDtypeStruct((TOP_K, t), jnp.int32),
        compiler_params=pltpu.CompilerParams(dimension_semantics=("parallel",)),
        name="moe_dest",
    )(start, idx, pos)


SC_WINDOW = 128


def _sc_mesh():
    return plsc.VectorSubcoreMesh(core_axis_name="c", subcore_axis_name="s")


def _sc_scatter_rows(x, dest, row0, n_sorted):
    t = dest.shape[1]
    blk0 = row0 // SC_WINDOW

    @functools.partial(
        pl.kernel,
        out_type=jax.ShapeDtypeStruct((n_sorted, D_QUARTER), x.dtype),
        mesh=_sc_mesh(),
        scratch_types=[],
    )
    def scatter(x_hbm, i_hbm, o_hbm):
        def body(x_vmem, i_vmem):
            pltpu.sync_copy(x_vmem, o_hbm.at[i_vmem.at[0]])

        pltpu.emit_pipeline(
            body,
            grid=(t // SC_WINDOW, TOP_K),
            in_specs=[pl.BlockSpec((SC_WINDOW, D_QUARTER), lambda i, k: (i + blk0, 0)),
                      pl.BlockSpec((1, SC_WINDOW), lambda i, k: (k, i))],
            out_specs=[],
            core_axis_name=("c", "s"),
            dimension_semantics=(pltpu.PARALLEL, pltpu.ARBITRARY),
        )(x_hbm, i_hbm)

    return scatter(x, dest)


def _sc_gather_rows(table, idx):
    m = idx.shape[1]

    @functools.partial(
        pl.kernel,
        out_type=jax.ShapeDtypeStruct((m, D_QUARTER), table.dtype),
        mesh=_sc_mesh(),
        scratch_types=[],
    )
    def gather(t_hbm, i_hbm, o_hbm):
        def body(i_vmem, o_vmem):
            pltpu.sync_copy(t_hbm.at[i_vmem.at[0]], o_vmem)

        pltpu.emit_pipeline(
            body,
            grid=(m // SC_WINDOW,),
            in_specs=[pl.BlockSpec((1, SC_WINDOW), lambda i: (0, i))],
            out_specs=[pl.BlockSpec((SC_WINDOW, D_QUARTER), lambda i: (i, 0))],
            core_axis_name=("c", "s"),
            dimension_semantics=(pltpu.PARALLEL,),
        )(i_hbm, o_hbm)

    return gather(table, idx)


def _ffn_kernel(te_ref, tv_ref, nu_ref, xa_ref, xb_ref, wg_ref, wu_ref, wd_ref, ya_ref, yb_ref,
                wgb_ref, wub_ref, wdb_ref):
    i = pl.program_id(0)

    @pl.when(jnp.logical_and(i < nu_ref[0],
                             jnp.logical_or(i == 0, te_ref[i] != te_ref[jnp.maximum(i - 1, 0)])))
    def _():
        wgb_ref[...] = wg_ref[0, 0].astype(BF16)
        wub_ref[...] = wu_ref[0, 0].astype(BF16)
        wdb_ref[...] = wd_ref[0, 0].astype(BF16)

    @pl.when(i < nu_ref[0])
    def _():
        wg = wgb_ref[...]
        wu = wub_ref[...]
        wd = wdb_ref[...]
        th = FFN_TM // 2
        for r0 in (0, th):
            rows = pl.ds(r0, th)
            live = (lax.broadcasted_iota(jnp.int32, (th, D_QUARTER), 0) + r0) < tv_ref[i]
            hi_a, lo_a = _unpack_rows(jnp.where(live, xa_ref[rows, :], jnp.uint32(0)))
            hi_b, lo_b = _unpack_rows(jnp.where(live, xb_ref[rows, :], jnp.uint32(0)))
            xb = jnp.concatenate([hi_a.astype(BF16), hi_b.astype(BF16), lo_a.astype(BF16), lo_b.astype(BF16)],
                                 axis=1)
            a = (_silu(_dot(xb, wg)) * _dot(xb, wu)).astype(BF16)
            words = _pack_rows(_dot(a, wd))
            ya_ref[rows, :] = words[:, :D_QUARTER]
            yb_ref[rows, :] = words[:, D_QUARTER:]

    @pl.when(i >= nu_ref[0])
    def _():
        ya_ref[...] = jnp.zeros(ya_ref.shape, jnp.uint32)
        yb_ref[...] = jnp.zeros(yb_ref.shape, jnp.uint32)


def _expert_ffn(xs_a, xs_b, tile_expert, tile_valid, n_used, wg, wu, wd, layer):
    n_tiles = xs_a.shape[0] // FFN_TM

    def row_map(i, te, tv, nu):
        return (jnp.minimum(i, nu[0] - 1), 0)

    def out_map(i, te, tv, nu):
        return (i, 0)

    def w_map(i, te, tv, nu):
        return (layer, te[jnp.minimum(i, nu[0] - 1)], 0, 0)

    return pl.pallas_call(
        _ffn_kernel,
        grid_spec=pltpu.PrefetchScalarGridSpec(
            num_scalar_prefetch=3,
            grid=(n_tiles,),
            in_specs=[
                pl.BlockSpec((FFN_TM, D_QUARTER), row_map),
                pl.BlockSpec((FFN_TM, D_QUARTER), row_map),
                pl.BlockSpec((1, 1, D_MODEL, D_EXPERT), w_map),
                pl.BlockSpec((1, 1, D_MODEL, D_EXPERT), w_map),
                pl.BlockSpec((1, 1, D_EXPERT, D_MODEL), w_map),
            ],
            out_specs=[pl.BlockSpec((FFN_TM, D_QUARTER), out_map), pl.BlockSpec((FFN_TM, D_QUARTER), out_map)],
            scratch_shapes=[pltpu.VMEM((D_MODEL, D_EXPERT), BF16), pltpu.VMEM((D_MODEL, D_EXPERT), BF16),
                            pltpu.VMEM((D_EXPERT, D_MODEL), BF16)],
        ),
        out_shape=[jax.ShapeDtypeStruct((n_tiles * FFN_TM, D_QUARTER), jnp.uint32),
                   jax.ShapeDtypeStruct((n_tiles * FFN_TM, D_QUARTER), jnp.uint32)],
        compiler_params=pltpu.CompilerParams(dimension_semantics=("arbitrary",)),
        name="moe_expert_ffn",
    )(tile_expert, tile_valid, n_used, xs_a, xs_b, wg, wu, wd)


COMBINE_TM = 256


def _combine_kernel(n_prev, tile0, x_ref, mods_ref, wcol_ref, ga_ref, gb_ref, sg_ref, su_ref, sd_ref, g_ref, b_ref,
                    *rest):
    o_refs = rest[n_prev:]
    m = mods_ref[0]
    sf, cf, gf = m[3:4], m[4:5], m[5:6]
    x = x_ref[...]
    hc = (x * (1.0 + cf) + sf).astype(BF16)
    a = _silu(_dot(hc, sg_ref[...])) * _dot(hc, su_ref[...])
    y = _dot(a.astype(BF16), sd_ref[...])
    wcol = wcol_ref[...]
    parts = [y[:, q * D_QUARTER:(q + 1) * D_QUARTER] for q in range(4)]
    for k in range(TOP_K):
        hi_a, lo_a = _unpack_rows(ga_ref[k])
        hi_b, lo_b = _unpack_rows(gb_ref[k])
        wk = wcol[:, k:k + 1]
        parts = [parts[0] + wk * hi_a, parts[1] + wk * hi_b, parts[2] + wk * lo_a, parts[3] + wk * lo_b]
    y = jnp.concatenate(parts, axis=1)
    out = _layer_norm(ALPHA * x + gf * y, g_ref[...], b_ref[...])
    if len(o_refs) == 1:
        o_refs[0][...] = out
    else:
        is_ctx = pl.program_id(0) + tile0 < T_CTX // x_ref.shape[0]

        @pl.when(is_ctx)
        def _():
            o_refs[0][...] = out

        @pl.when(jnp.logical_not(is_ctx))
        def _():
            o_refs[1][...] = out


def _combine(x, mods, wcol, g_a, g_b, sg, su, sd, ln_g, ln_b, row0, split_streams, prev):
    tm = COMBINE_TM
    n = g_a.shape[1] // tm
    tile0 = row0 // tm
    n_ctx = T_CTX // tm
    full = lambda shape: pl.BlockSpec(shape, lambda i: (0,) * len(shape))
    layouts = {
        "all": (lambda i: (tile0 + i, 0), T_ALL),
        "ctx": (lambda i: (jnp.minimum(tile0 + i, n_ctx - 1), 0), T_CTX),
        "lat": (lambda i: (jnp.maximum(tile0 + i - n_ctx, 0), 0), T_LAT),
    }
    if split_streams:
        kinds = (["ctx"] if tile0 < n_ctx else []) + (["lat"] if tile0 + n > n_ctx else [])
    else:
        kinds = ["all"]
    carried = [kd for kd in kinds if kd in prev]
    outs = pl.pallas_call(
        functools.partial(_combine_kernel, len(carried), tile0),
        grid=(n,),
        in_specs=[
            pl.BlockSpec((tm, D_MODEL), lambda i: (tile0 + i, 0)),
            pl.BlockSpec((1, 6, D_MODEL), lambda i: ((tile0 + i) * tm // GROUP_ROWS, 0, 0)),
            pl.BlockSpec((tm, LANES), lambda i: (tile0 + i, 0)),
            pl.BlockSpec((TOP_K, tm, D_QUARTER), lambda i: (0, i, 0)),
            pl.BlockSpec((TOP_K, tm, D_QUARTER), lambda i: (0, i, 0)),
            full((D_MODEL, D_SHARED)), full((D_MODEL, D_SHARED)), full((D_SHARED, D_MODEL)),
            full((1, D_MODEL)), full((1, D_MODEL)),
        ] + [pl.BlockSpec(memory_space=pl.ANY) for _ in carried],
        out_specs=[pl.BlockSpec((tm, D_MODEL), layouts[kd][0]) for kd in kinds],
        out_shape=[jax.ShapeDtypeStruct((layouts[kd][1], D_MODEL), F32) for kd in kinds],
        input_output_aliases={10 + j: kinds.index(kd) for j, kd in enumerate(carried)},
        compiler_params=pltpu.CompilerParams(dimension_semantics=("arbitrary",)),
        name="moe_combine",
    )(x, mods, wcol, g_a, g_b, sg, su, sd, ln_g, ln_b, *[prev[kd] for kd in carried])
    return {**prev, **dict(zip(kinds, outs))}


MLA_TM = 512


def _mla_proj_kernel(x_ref, mods_ref, wdq_ref, wdkv_ref, wkr_ref, qn_ref, kvn_ref, wuq_ref,
                     wukn_ref, wuv_ref, ta_ref, tb_ref, q_ref, k_ref, v_ref, ckv_ref, kr_ref):
    m = mods_ref[0]
    sm, cm = m[0:1], m[1:2]
    h = (x_ref[...] * (1.0 + cm) + sm).astype(BF16)
    cq = _rms_norm(_dot(h, wdq_ref[...]), qn_ref[...])
    ckv = _rms_norm(_dot(h, wdkv_ref[...]), kvn_ref[...])
    kr2 = _dot(h, wkr_ref[...])
    ckv_ref[...] = ckv
    kr_ref[...] = kr2

    ka = ta_ref[0]
    kb = tb_ref[0]
    tm = ka.shape[0]
    ta = jnp.concatenate([jnp.full((tm, QK_NOPE), Q_PRESCALE, F32), ka * Q_PRESCALE], axis=1)
    tb = jnp.concatenate([jnp.zeros((tm, QK_NOPE), F32), kb * Q_PRESCALE], axis=1)
    krr = kr2 * ka + pltpu.roll(kr2, QK_ROPE, 1) * kb

    qpre = _dot(cq.astype(BF16), wuq_ref[...])
    ckv_b = ckv.astype(BF16)
    kn = _dot(ckv_b, wukn_ref[...])
    v_ref[...] = _dot(ckv_b, wuv_ref[...]).astype(BF16)
    for hd in range(N_HEADS):
        qh = qpre[:, hd * HEAD_PAD:(hd + 1) * HEAD_PAD]
        qrot = qh * ta + pltpu.roll(qh, HEAD_PAD - QK_ROPE, 1) * tb
        q_ref[:, hd * HEAD_PAD:(hd + 1) * HEAD_PAD] = qrot.astype(BF16)
        k_ref[:, hd * HEAD_PAD:hd * HEAD_PAD + QK_NOPE] = kn[:, hd * QK_NOPE:(hd + 1) * QK_NOPE].astype(BF16)
        k_ref[:, hd * HEAD_PAD + QK_NOPE:(hd + 1) * HEAD_PAD] = krr.astype(BF16)


def _mla_proj(x, mods, p, rope_a, rope_b):
    t = x.shape[0]
    tm = MLA_TM
    full = lambda shape: pl.BlockSpec(shape, lambda i: (0,) * len(shape))
    rope_spec = pl.BlockSpec(
        (1, tm, 2 * QK_ROPE),
        lambda i: (jnp.minimum(i * tm // GROUP_ROWS, 1), (i * tm % GROUP_ROWS) // tm, 0))
    return pl.pallas_call(
        _mla_proj_kernel,
        grid=(t // tm,),
        in_specs=[
            pl.BlockSpec((tm, D_MODEL), lambda i: (i, 0)),
            pl.BlockSpec((1, 6, D_MODEL), lambda i: (i * tm // GROUP_ROWS, 0, 0)),
            full((D_MODEL, Q_LORA)), full((D_MODEL, KV_LORA)), full((D_MODEL, 2 * QK_ROPE)),
            full((1, Q_LORA)), full((1, KV_LORA)),
            full((Q_LORA, N_HEADS * HEAD_PAD)),
            full((KV_LORA, N_HEADS * QK_NOPE)), full((KV_LORA, N_HEADS * V_DIM)),
            rope_spec, rope_spec,
        ],
        out_specs=[
            pl.BlockSpec((tm, N_HEADS * HEAD_PAD), lambda i: (i, 0)),
            pl.BlockSpec((tm, N_HEADS * HEAD_PAD), lambda i: (i, 0)),
            pl.BlockSpec((tm, N_HEADS * V_DIM), lambda i: (i, 0)),
            pl.BlockSpec((tm, KV_LORA), lambda i: (i, 0)),
            pl.BlockSpec((tm, 2 * QK_ROPE), lambda i: (i, 0)),
        ],
        out_shape=[
            jax.ShapeDtypeStruct((t, N_HEADS * HEAD_PAD), BF16),
            jax.ShapeDtypeStruct((t, N_HEADS * HEAD_PAD), BF16),
            jax.ShapeDtypeStruct((t, N_HEADS * V_DIM), BF16),
            jax.ShapeDtypeStruct((t, KV_LORA), F32),
            jax.ShapeDtypeStruct((t, 2 * QK_ROPE), F32),
        ],
        compiler_params=pltpu.CompilerParams(dimension_semantics=("parallel",)),
        name="mla_proj",
    )(x, mods, p["w_dq"], p["w_dkv"], p["w_kr"], p["q_norm"], p["kv_norm"], p["w_uq"],
      p["w_ukn"], p["w_uv"], rope_a, rope_b)


def _cache_kv_kernel(ckv_ref, kr_ref, wukn_ref, wuv_ref, k_ref, v_ref):
    ckv_b = ckv_ref[...].astype(BF16)
    kn = _dot(ckv_b, wukn_ref[...])
    v_ref[...] = _dot(ckv_b, wuv_ref[...]).astype(BF16)
    kr = kr_ref[...].astype(BF16)
    for hd in range(N_HEADS):
        k_ref[:, hd * HEAD_PAD:hd * HEAD_PAD + QK_NOPE] = kn[:, hd * QK_NOPE:(hd + 1) * QK_NOPE].astype(BF16)
        k_ref[:, hd * HEAD_PAD + QK_NOPE:(hd + 1) * HEAD_PAD] = kr


def _cache_kv(ckv, kr_pad, p):
    t = ckv.shape[0]
    tm = PAST_LEN
    full = lambda shape: pl.BlockSpec(shape, lambda i: (0,) * len(shape))
    return pl.pallas_call(
        _cache_kv_kernel,
        grid=(t // tm,),
        in_specs=[
            pl.BlockSpec((tm, KV_LORA), lambda i: (i, 0)),
            pl.BlockSpec((tm, 2 * QK_ROPE), lambda i: (i, 0)),
            full((KV_LORA, N_HEADS * QK_NOPE)), full((KV_LORA, N_HEADS * V_DIM)),
        ],
        out_specs=[
            pl.BlockSpec((tm, N_HEADS * HEAD_PAD), lambda i: (i, 0)),
            pl.BlockSpec((tm, N_HEADS * V_DIM), lambda i: (i, 0)),
        ],
        out_shape=[
            jax.ShapeDtypeStruct((t, N_HEADS * HEAD_PAD), BF16),
            jax.ShapeDtypeStruct((t, N_HEADS * V_DIM), BF16),
        ],
        compiler_params=pltpu.CompilerParams(dimension_semantics=("parallel",)),
        name="mla_cache_kv",
    )(ckv, kr_pad, p["w_ukn"], p["w_uv"])


def _ctx_attn_kernel(q_ref, k_ref, v_ref, o_ref):
    for hd in range(N_HEADS):
        q = q_ref[:, hd * HEAD_PAD:(hd + 1) * HEAD_PAD]
        k = k_ref[:, hd * HEAD_PAD:(hd + 1) * HEAD_PAD]
        s = _dot_nt(q, k)
        s = s - jnp.max(s, axis=-1, keepdims=True)
        p = jnp.exp2(s)
        p = p / jnp.sum(p, axis=-1, keepdims=True)
        o = _dot(p.astype(BF16), v_ref[:, hd * V_DIM:(hd + 1) * V_DIM])
        o_ref[:, hd * V_DIM:(hd + 1) * V_DIM] = o.astype(BF16)


def _ctx_attention(q, k, v):
    return pl.pallas_call(
        _ctx_attn_kernel,
        grid=(BATCH,),
        in_specs=[
            pl.BlockSpec((SEQ, N_HEADS * HEAD_PAD), lambda b: (b, 0)),
            pl.BlockSpec((SEQ, N_HEADS * HEAD_PAD), lambda b: (b, 0)),
            pl.BlockSpec((SEQ, N_HEADS * V_DIM), lambda b: (b, 0)),
        ],
        out_specs=pl.BlockSpec((SEQ, N_HEADS * V_DIM), lambda b: (b, 0)),
        out_shape=jax.ShapeDtypeStruct((T_CTX, N_HEADS * V_DIM), BF16),
        compiler_params=pltpu.CompilerParams(dimension_semantics=("parallel",)),
        name="ctx_attention",
    )(q, k, v)


LAT_TQ = 1024
LAT_TK = 512
LAT_PIECES = 4


def _lat_attn_kernel(q_ref, k_ref, v_ref, kc_ref, vc_ref, o_ref, s_ref, p_ref, m_ref):
    n_chunks = DEC_SEQ // LAT_TK + 1
    tp = q_ref.shape[0] // LAT_PIECES
    pieces = [pl.ds(j * tp, tp) for j in range(LAT_PIECES)]
    groups = [pieces[:2], pieces[2:]]
    state = {}

    def keys(c):
        return kc_ref[...] if c == n_chunks - 1 else k_ref[c * LAT_TK:(c + 1) * LAT_TK, :]

    def values(c):
        return vc_ref[...] if c == n_chunks - 1 else v_ref[c * LAT_TK:(c + 1) * LAT_TK, :]

    def qk(r, c):
        s = _dot_nt(q_ref[r, :], keys(c))
        s_ref[r, c * LAT_TK:(c + 1) * LAT_TK] = s
        mp = state.get(("m", r.start), jnp.full((tp, LANES), NEG_INF, F32))
        for j in range(LAT_TK // LANES):
            mp = jnp.maximum(mp, s[:, j * LANES:(j + 1) * LANES])
        state[("m", r.start)] = mp

    def row_max(r):
        m_ref[r, :] = jnp.broadcast_to(jnp.max(state[("m", r.start)], axis=-1, keepdims=True), (tp, LANES))

    def exp_chunk(r, c, after=None):
        lp = state.get(("l", r.start), jnp.zeros((tp, LANES), F32))
        m = m_ref[r, :]
        if after is not None:
            bits = lax.bitcast_convert_type(after, jnp.uint32)
            zero = lax.shift_right_logical(lax.shift_right_logical(bits, jnp.uint32(16)), jnp.uint32(16))
            m = m + lax.bitcast_convert_type(zero, F32)
        for j in range(c * LAT_TK // LANES, (c + 1) * LAT_TK // LANES):
            p = jnp.exp2(s_ref[r, j * LANES:(j + 1) * LANES] - m)
            lp = lp + p
            p_ref[r, j * LANES:(j + 1) * LANES] = p.astype(BF16)
        state[("l", r.start)] = lp

    def pv(r, c):
        acc = state.get(("a", r.start), jnp.zeros((tp, V_DIM), F32))
        state[("a", r.start)] = acc + _dot(p_ref[r, c * LAT_TK:(c + 1) * LAT_TK], values(c))

    def finish(r):
        o_ref[r, :] = (state[("a", r.start)] / jnp.sum(state[("l", r.start)], axis=-1, keepdims=True)).astype(BF16)

    for c in range(n_chunks):
        for r in groups[0]:
            qk(r, c)
    for r in groups[0]:
        row_max(r)
    for c in range(n_chunks):
        for r in groups[1]:
            qk(r, c)
        for r in groups[0]:
            exp_chunk(r, c)
    for r in groups[1]:
        row_max(r)
    for c in range(n_chunks):
        for r in groups[0]:
            pv(r, c)
        for r0, r in zip(groups[0], groups[1]):
            exp_chunk(r, c, after=state[("a", r0.start)])
    for r in groups[0]:
        finish(r)
    for c in range(n_chunks):
        for r in groups[1]:
            pv(r, c)
    for r in groups[1]:
        finish(r)


def _lat_attention(q, k, v, kc, vc):
    nq = DEC_SEQ // LAT_TQ
    return pl.pallas_call(
        _lat_attn_kernel,
        grid=(DEC_BATCH, N_HEADS, nq),
        in_specs=[
            pl.BlockSpec((LAT_TQ, HEAD_PAD), lambda b, h, i: ((b + 1) * nq + i, h)),
            pl.BlockSpec((DEC_SEQ, HEAD_PAD), lambda b, h, i: (b + 1, h)),
            pl.BlockSpec((DEC_SEQ, V_DIM), lambda b, h, i: (b + 1, h)),
            pl.BlockSpec((PAST_LEN, HEAD_PAD), lambda b, h, i: (b, h)),
            pl.BlockSpec((PAST_LEN, V_DIM), lambda b, h, i: (b, h)),
        ],
        out_specs=pl.BlockSpec((LAT_TQ, V_DIM), lambda b, h, i: (b * nq + i, h)),
        out_shape=jax.ShapeDtypeStruct((T_LAT, N_HEADS * V_DIM), BF16),
        scratch_shapes=[pltpu.VMEM((LAT_TQ, DEC_SEQ + PAST_LEN), F32),
                        pltpu.VMEM((LAT_TQ, DEC_SEQ + PAST_LEN), BF16),
                        pltpu.VMEM((LAT_TQ, LANES), F32)],
        compiler_params=pltpu.CompilerParams(
            dimension_semantics=("parallel", "parallel", "parallel")),
        name="lat_attention",
    )(q, k, v, kc, vc)


OPROJ_TM = 512


def _oproj_kernel(oc_ref, ol_ref, x_ref, mods_ref, wo_ref, g_ref, b_ref, out_ref):
    is_ctx = pl.program_id(0) < T_CTX // OPROJ_TM
    o = jnp.where(is_ctx, oc_ref[...], ol_ref[...])
    gm = mods_ref[0][2:3]
    y = _dot(o, wo_ref[...])
    out_ref[...] = _layer_norm(ALPHA * x_ref[...] + gm * y, g_ref[...], b_ref[...])


def _oproj(o_ctx, o_lat, x, mods, w_o, ln_g, ln_b):
    t = x.shape[0]
    tm = OPROJ_TM
    n_ctx = T_CTX // tm
    return pl.pallas_call(
        _oproj_kernel,
        grid=(t // tm,),
        in_specs=[
            pl.BlockSpec((tm, N_HEADS * V_DIM), lambda i: (jnp.minimum(i, n_ctx - 1), 0)),
            pl.BlockSpec((tm, N_HEADS * V_DIM), lambda i: (jnp.maximum(i - n_ctx, 0), 0)),
            pl.BlockSpec((tm, D_MODEL), lambda i: (i, 0)),
            pl.BlockSpec((1, 6, D_MODEL), lambda i: (i * tm // GROUP_ROWS, 0, 0)),
            pl.BlockSpec((N_HEADS * V_DIM, D_MODEL), lambda i: (0, 0)),
            pl.BlockSpec((1, D_MODEL), lambda i: (0, 0)),
            pl.BlockSpec((1, D_MODEL), lambda i: (0, 0)),
        ],
        out_specs=pl.BlockSpec((tm, D_MODEL), lambda i: (i, 0)),
        out_shape=jax.ShapeDtypeStruct((t, D_MODEL), F32),
        compiler_params=pltpu.CompilerParams(dimension_semantics=("parallel",)),
        name="attn_oproj",
    )(o_ctx, o_lat, x, mods, w_o, ln_g, ln_b)


def _swap16(w):
    q = QK_ROPE // 4
    return jnp.concatenate([w[..., q:2 * q], w[..., :q], w[..., 3 * q:], w[..., 2 * q:3 * q]], axis=-1)


def _mla_params(w_dqkv, q_norm, w_uq, kv_norm, w_ukv):
    w_kr = w_dqkv[:, Q_LORA + KV_LORA:]
    wq = w_uq.reshape(Q_LORA, N_HEADS, QK_NOPE + QK_ROPE)
    wq_r = wq[..., QK_NOPE:]
    wq = jnp.concatenate([wq[..., :QK_NOPE], wq_r, _swap16(wq_r)], axis=-1)
    wkv = w_ukv.reshape(KV_LORA, N_HEADS, QK_NOPE + V_DIM)
    return {
        "w_dq": w_dqkv[:, :Q_LORA].astype(BF16),
        "w_dkv": w_dqkv[:, Q_LORA:Q_LORA + KV_LORA].astype(BF16),
        "w_kr": jnp.concatenate([w_kr, _swap16(w_kr)], axis=-1).astype(BF16),
        "q_norm": q_norm.reshape(1, Q_LORA),
        "kv_norm": kv_norm.reshape(1, KV_LORA),
        "w_uq": wq.reshape(Q_LORA, N_HEADS * HEAD_PAD).astype(BF16),
        "w_ukn": wkv[..., :QK_NOPE].reshape(KV_LORA, N_HEADS * QK_NOPE).astype(BF16),
        "w_uv": wkv[..., QK_NOPE:].reshape(KV_LORA, N_HEADS * V_DIM).astype(BF16),
    }


def _rope_tables():
    nf = QK_ROPE // 4
    t = np.arange(DEC_SEQ)
    row = (t // GRID_W).astype(np.float32)
    col = (t % GRID_W).astype(np.float32)
    inv = (ROPE_THETA ** (-np.arange(nf, dtype=np.float32) / nf)).astype(np.float32)
    ar, ac = row[:, None] * inv, col[:, None] * inv
    pad = np.zeros((DEC_SEQ, QK_ROPE), np.float32)
    cos = np.concatenate([np.cos(ar), np.cos(ar), np.cos(ac), np.cos(ac), pad], axis=-1)
    sin = np.concatenate([-np.sin(ar), np.sin(ar), -np.sin(ac), np.sin(ac), pad], axis=-1)
    cos_id = np.concatenate([np.ones((DEC_SEQ, QK_ROPE), np.float32), pad], axis=-1)
    sin_id = np.zeros((DEC_SEQ, 2 * QK_ROPE), np.float32)
    return (jnp.asarray(np.stack([cos_id, cos]).astype(np.float32)),
            jnp.asarray(np.stack([sin_id, sin]).astype(np.float32)))


def kernel(x_prompt, x_sample, cache_ckv, cache_krope, c, c_ctx, ada_w, ada_b, ln_g, ln_b, conv_w_in, conv_k, conv_w_out, mla_w_dqkv, mla_q_norm, mla_w_uq, mla_kv_norm, mla_w_ukv, mla_w_o, router_w, router_bias, exp_w_gate, exp_w_up, exp_w_down, sh_w_gate, sh_w_up, sh_w_down):
    cvecs =jnp.concatenate([c_ctx[None, :], c, jnp.zeros((SUBLANES - N_GROUPS_ROWS, D_MODEL), F32)], axis=0)
    mods = _adaln(cvecs, ada_w, ada_b)

    def ln(l, k):
        return ln_g[l, k].reshape(1, D_MODEL), ln_b[l, k].reshape(1, D_MODEL)

    def moe_layer(xin, l, split_streams):
        idx, pos, wcol, counts, hp_a, hp_b = _router(xin, mods[l], router_w[l].T,
                                                     router_bias[l].reshape(N_EXPERTS, 1))
        shared = (sh_w_gate[l].astype(BF16), sh_w_up[l].astype(BF16), sh_w_down[l].astype(BF16))
        outs = {}
        for ch in sorted(range(len(MOE_CHUNKS)), key=lambda j: -MOE_CHUNKS[j][1]):
            row0, rows = MOE_CHUNKS[ch]
            n_tiles = _n_tiles(rows)
            cnt = counts[ch, :, 0].astype(jnp.int32)
            start, tile_expert, tile_valid, n_used = _expert_layout(cnt, n_tiles)
            dest = _dest_rows(start, idx[:, row0:row0 + rows], pos[:, row0:row0 + rows])
            xs_a = _sc_scatter_rows(hp_a, dest, row0, n_tiles * FFN_TM)
            xs_b = _sc_scatter_rows(hp_b, dest, row0, n_tiles * FFN_TM)
            ys_a, ys_b = _expert_ffn(xs_a, xs_b, tile_expert, tile_valid, n_used,
                                     exp_w_gate, exp_w_up, exp_w_down, l)
            dest_row = dest.reshape(1, rows * TOP_K)
            g_a = _sc_gather_rows(ys_a, dest_row).reshape(TOP_K, rows, D_QUARTER)
            g_b = _sc_gather_rows(ys_b, dest_row).reshape(TOP_K, rows, D_QUARTER)
            outs = _combine(xin, mods[l], wcol, g_a, g_b, *shared, *ln(l, 1), row0=row0,
                            split_streams=split_streams, prev=outs)
        return (outs["ctx"], outs["lat"]) if split_streams else outs["all"]

    x = _conv_mixer(x_prompt.reshape(T_CTX, D_MODEL), x_sample.reshape(T_LAT, D_MODEL), mods[0],
                    conv_w_in[0].astype(BF16), conv_k[0], conv_w_out[0].astype(BF16), *ln(0, 0))
    x = moe_layer(x, 0, split_streams=False)

    p = _mla_params(mla_w_dqkv[0], mla_q_norm[0], mla_w_uq[0], mla_kv_norm[0], mla_w_ukv[0])
    rope_a, rope_b = _rope_tables()
    q, k, v, ckv, kr = _mla_proj(x, mods[1], p, rope_a, rope_b)
    kr_cache = jnp.concatenate([cache_krope[:, 0].reshape(DEC_BATCH * PAST_LEN, QK_ROPE),
                                jnp.zeros((DEC_BATCH * PAST_LEN, QK_ROPE), F32)], axis=-1)
    kc, vc = _cache_kv(cache_ckv[:, 0].reshape(DEC_BATCH * PAST_LEN, KV_LORA), kr_cache, p)
    o_ctx = _ctx_attention(q, k, v)
    o_lat = _lat_attention(q, k, v, kc, vc)
    x = _oproj(o_ctx, o_lat, x, mods[1], mla_w_o[0].astype(BF16), *ln(1, 0))
    y_ctx, y_lat = moe_layer(x, 1, split_streams=True)

    y_prompt = y_ctx.reshape(BATCH, SEQ, D_MODEL)
    y_sample = y_lat.reshape(DEC_BATCH, DEC_SEQ, D_MODEL)
    state_ckv = ckv[:T_CTX].reshape(BATCH, 1, SEQ, KV_LORA)
    state_krope = kr[:T_CTX, :QK_ROPE].reshape(BATCH, 1, SEQ, QK_ROPE)
    return (y_prompt, y_sample, state_ckv, state_krope)
```

```python
import functools
import math

import jax
import jax.numpy as jnp
import numpy as np
from jax import lax
from jax.experimental import pallas as pl
from jax.experimental.pallas import tpu as pltpu
from jax.experimental.pallas import tpu_sc as plsc

D_MODEL = 1024
BATCH = 16
SEQ = 256
DEPTH = 2
DEC_BATCH = 4
DEC_SEQ = 4096
PAST_LEN = 512
GRID_W = 64

N_HEADS = 8
QK_NOPE = 128
QK_ROPE = 64
V_DIM = 128
Q_LORA = 384
KV_LORA = 256
ROPE_THETA = 10000.0
ATTN_SCALE = (QK_NOPE + QK_ROPE) ** -0.5
HEAD_PAD = 256
Q_PRESCALE = ATTN_SCALE * math.log2(math.e)

N_EXPERTS = 64
TOP_K = 8
N_GROUPS = 8
TOPK_GROUPS = 4
GROUP_SIZE = N_EXPERTS // N_GROUPS
D_EXPERT = 256
D_SHARED = 256
ROUTED_SCALE = 2.5

ALPHA = (2 * DEPTH) ** 0.25
LN_EPS = 1e-5
RMS_EPS = 1e-6

GROUP_ROWS = 4096
N_GROUPS_ROWS = 1 + DEC_BATCH
T_CTX = BATCH * SEQ
T_LAT = DEC_BATCH * DEC_SEQ
T_ALL = T_CTX + T_LAT
LANES = 128
SUBLANES = 8

F32 = jnp.float32
BF16 = jnp.bfloat16
NEG_INF = float("-inf")


def _dot(a, b):
    return jnp.dot(a, b, preferred_element_type=F32)


def _dot_nt(a, b, precision=None):
    return lax.dot_general(a, b, (((1,), (1,)), ((), ())), precision=precision,
                           preferred_element_type=F32)


def _layer_norm(v, g, b):
    mu = jnp.mean(v, axis=-1, keepdims=True)
    d = v - mu
    var = jnp.mean(d * d, axis=-1, keepdims=True)
    return d * lax.rsqrt(var + LN_EPS) * g + b


def _rms_norm(v, g):
    return v * lax.rsqrt(jnp.mean(v * v, axis=-1, keepdims=True) + RMS_EPS) * g


def _silu(v):
    return v / (1.0 + jnp.exp(-v))


def _sigmoid(v):
    return 1.0 / (1.0 + jnp.exp(-v))


def _adaln_kernel(c_ref, w_ref, b_ref, o_ref):
    c = c_ref[...]
    s = _silu(c)
    o_ref[0, 0] = jnp.dot(s, w_ref[0], precision=lax.Precision.HIGHEST,
                          preferred_element_type=F32) + b_ref[0]


def _adaln(cvecs, ada_w, ada_b):
    out = pl.pallas_call(
        _adaln_kernel,
        grid=(DEPTH, 6),
        in_specs=[
            pl.BlockSpec((SUBLANES, D_MODEL), lambda l, j: (0, 0)),
            pl.BlockSpec((1, D_MODEL, D_MODEL), lambda l, j: (l, 0, j)),
            pl.BlockSpec((1, 1, D_MODEL), lambda l, j: (l, 0, j)),
        ],
        out_specs=pl.BlockSpec((1, 1, SUBLANES, D_MODEL), lambda l, j: (l, j, 0, 0)),
        out_shape=jax.ShapeDtypeStruct((DEPTH, 6, SUBLANES, D_MODEL), F32),
        compiler_params=pltpu.CompilerParams(dimension_semantics=("parallel", "parallel")),
        name="adaln",
    )(cvecs, ada_w, ada_b.reshape(DEPTH, 1, 6 * D_MODEL))
    return jnp.transpose(out[:, :, :N_GROUPS_ROWS, :], (0, 2, 1, 3))


CONV_TM = 1024


def _conv_kernel(xc_ref, xcp_ref, xcn_ref, xl_ref, xlp_ref, xln_ref, mods_ref, win_ref, ck_ref, wout_ref,
                 g_ref, b_ref, o_ref):
    i = pl.program_id(0)
    tm = xc_ref.shape[0]
    is_ctx = i < T_CTX // tm
    m = mods_ref[0]
    sm, cm, gm = m[0:1], m[1:2], m[2:3]
    x = jnp.where(is_ctx, xc_ref[...], xl_ref[...])
    xp = jnp.where(is_ctx, xcp_ref[...], xlp_ref[...])
    xn = jnp.where(is_ctx, xcn_ref[...], xln_ref[...])
    def halo_u(xh):
        hh = (xh * (1.0 + cm) + sm).astype(BF16)
        zh = _dot(hh, win_ref[:, D_MODEL:])
        return zh[:, :D_MODEL] * zh[:, D_MODEL:]

    th = tm // 2
    halves = [x[:th], x[th:]]
    zs = [_dot((xh * (1.0 + cm) + sm).astype(BF16), win_ref[...]) for xh in halves]
    us = [z[:, D_MODEL:2 * D_MODEL] * z[:, 2 * D_MODEL:] for z in zs]
    befores = [halo_u(xp)[SUBLANES - 1:SUBLANES], us[0][th - 1:th]]
    afters = [us[1][0:1], halo_u(xn)[0:1]]

    ck = ck_ref[...]
    row = lax.broadcasted_iota(jnp.int32, (th, 1), 0)
    for hf in range(2):
        grow = i * tm + hf * th + row
        seq_len = jnp.where(grow < T_CTX, SEQ, DEC_SEQ)
        pos = jnp.bitwise_and(grow, seq_len - 1)
        u = us[hf]
        left = jnp.where(row == 0, befores[hf], pltpu.roll(u, 1, 0))
        left = jnp.where(pos == 0, 0.0, left)
        right = jnp.where(row == th - 1, afters[hf], pltpu.roll(u, th - 1, 0))
        right = jnp.where(pos == seq_len - 1, 0.0, right)
        conv = left * ck[0:1] + u * ck[1:2] + right * ck[2:3]
        v = (zs[hf][:, :D_MODEL] * conv).astype(BF16)
        y = _dot(v, wout_ref[...])
        o_ref[hf * th:(hf + 1) * th, :] = _layer_norm(ALPHA * halves[hf] + gm * y, g_ref[...], b_ref[...])


def _conv_mixer(x_ctx, x_lat, mods, w_in, conv_k, w_out, ln_g, ln_b):
    tm = CONV_TM
    per8 = tm // SUBLANES
    n_ctx = T_CTX // tm
    n_lat = T_LAT // tm

    def stream_specs(first, n):
        def blk(i):
            return jnp.clip(i - first, 0, n - 1)
        return [
            pl.BlockSpec((tm, D_MODEL), lambda i: (blk(i), 0)),
            pl.BlockSpec((SUBLANES, D_MODEL), lambda i: (jnp.maximum(blk(i) * per8 - 1, 0), 0)),
            pl.BlockSpec((SUBLANES, D_MODEL), lambda i: (jnp.minimum((blk(i) + 1) * per8, n * per8 - 1), 0)),
        ]

    return pl.pallas_call(
        _conv_kernel,
        grid=(n_ctx + n_lat,),
        in_specs=stream_specs(0, n_ctx) + stream_specs(n_ctx, n_lat) + [
            pl.BlockSpec((1, 6, D_MODEL), lambda i: (i * tm // GROUP_ROWS, 0, 0)),
            pl.BlockSpec((D_MODEL, 3 * D_MODEL), lambda i: (0, 0)),
            pl.BlockSpec((3, D_MODEL), lambda i: (0, 0)),
            pl.BlockSpec((D_MODEL, D_MODEL), lambda i: (0, 0)),
            pl.BlockSpec((1, D_MODEL), lambda i: (0, 0)),
            pl.BlockSpec((1, D_MODEL), lambda i: (0, 0)),
        ],
        out_specs=pl.BlockSpec((tm, D_MODEL), lambda i: (i, 0)),
        out_shape=jax.ShapeDtypeStruct((T_ALL, D_MODEL), F32),
        compiler_params=pltpu.CompilerParams(dimension_semantics=("parallel",)),
        name="conv_mixer",
    )(x_ctx, x_ctx, x_ctx, x_lat, x_lat, x_lat, mods, w_in, conv_k, w_out, ln_g, ln_b)


ROUTER_TM = 512
MOE_CHUNKS = ((0, T_ALL),)
COMBINE_SPLITS = 4


def _first_argmax_mask(cur, ridx, n):
    mx = jnp.max(cur, axis=0, keepdims=True)
    first = jnp.min(jnp.where(cur == mx, ridx, n), axis=0, keepdims=True)
    return ridx == first, mx


def _router_kernel(x_ref, mods_ref, rwt_ref, bias_ref, idx_ref, pos_ref, wcol_ref, count_ref, hpa_ref, hpb_ref,
                   carry_ref):
    tm = x_ref.shape[0]
    m = mods_ref[0]
    sf, cf = m[3:4], m[4:5]
    hc = x_ref[...] * (1.0 + cf) + sf
    words = _pack_rows(hc)
    hpa_ref[...] = words[:, :D_QUARTER]
    hpb_ref[...] = words[:, D_QUARTER:]
    logits = _dot_nt(rwt_ref[...], hc, precision=lax.Precision.HIGHEST)
    scores = _sigmoid(logits)
    biased = scores + bias_ref[...]

    ridx8 = lax.broadcasted_iota(jnp.int32, (GROUP_SIZE, tm), 0)
    gscore = jnp.full((N_GROUPS, tm), NEG_INF, F32)
    for g in range(N_GROUPS):
        blk = biased[g * GROUP_SIZE:(g + 1) * GROUP_SIZE]
        sel, m1 = _first_argmax_mask(blk, ridx8, GROUP_SIZE)
        m2 = jnp.max(jnp.where(sel, NEG_INF, blk), axis=0, keepdims=True)
        gscore = jnp.where(ridx8 == g, m1 + m2, gscore)

    gmask = jnp.zeros((N_GROUPS, tm), jnp.bool_)
    cur = gscore
    for _ in range(TOPK_GROUPS):
        sel, _unused = _first_argmax_mask(cur, ridx8, N_GROUPS)
        gmask = jnp.logical_or(gmask, sel)
        cur = jnp.where(sel, NEG_INF, cur)

    gmask_f = gmask.astype(F32)
    blocks = []
    for g in range(N_GROUPS):
        keep = jnp.broadcast_to(gmask_f[g:g + 1], (GROUP_SIZE, tm)) > 0.5
        blocks.append(jnp.where(keep, biased[g * GROUP_SIZE:(g + 1) * GROUP_SIZE], NEG_INF))
    cur = jnp.concatenate(blocks, axis=0)

    first_tiles = [row0 // tm for row0, _rows in MOE_CHUNKS]
    starts_chunk = functools.reduce(jnp.logical_or, [pl.program_id(0) == ft for ft in first_tiles])

    @pl.when(starts_chunk)
    def _():
        carry_ref[...] = jnp.zeros(carry_ref.shape, F32)

    ridx = lax.broadcasted_iota(jnp.int32, (N_EXPERTS, tm), 0)
    kidx = lax.broadcasted_iota(jnp.int32, (TOP_K, tm), 0)
    sels = []
    chosen = jnp.zeros((N_EXPERTS, tm), jnp.bool_)
    idx_rows = jnp.zeros((TOP_K, tm), jnp.int32)
    for k in range(TOP_K):
        mx = jnp.max(cur, axis=0, keepdims=True)
        first = jnp.min(jnp.where(cur == mx, ridx, N_EXPERTS), axis=0, keepdims=True)
        sel = ridx == first
        sels.append(sel)
        chosen = jnp.logical_or(chosen, sel)
        idx_rows = jnp.where(kidx == k, first, idx_rows)
        cur = jnp.where(sel, NEG_INF, cur)

    onehot = chosen.astype(F32)
    t_row = lax.broadcasted_iota(jnp.int32, (tm, tm), 0)
    t_col = lax.broadcasted_iota(jnp.int32, (tm, tm), 1)
    before = (t_row < t_col).astype(BF16)
    rank = carry_ref[...] + _dot(onehot.astype(BF16), before)
    carry_ref[...] = carry_ref[...] + jnp.sum(onehot, axis=1, keepdims=True)
    count_ref[0] = jnp.broadcast_to(carry_ref[...], count_ref.shape[1:])

    w = jnp.where(chosen, scores, 0.0)
    w = w / jnp.sum(w, axis=0, keepdims=True) * ROUTED_SCALE
    pos_rows = jnp.zeros((TOP_K, tm), F32)
    w_rows = jnp.zeros((TOP_K, tm), F32)
    for k in range(TOP_K):
        pos_rows = jnp.where(kidx == k, jnp.sum(jnp.where(sels[k], rank, 0.0), axis=0, keepdims=True), pos_rows)
        w_rows = jnp.where(kidx == k, jnp.sum(jnp.where(sels[k], w, 0.0), axis=0, keepdims=True), w_rows)
    idx_ref[...] = idx_rows
    pos_ref[...] = pos_rows.astype(jnp.int32)
    wpad = jnp.concatenate([w_rows, jnp.zeros((LANES - TOP_K, tm), F32)], axis=0)
    wcol_ref[...] = wpad.T


def _router(x, mods, router_wt, router_bias):
    t = x.shape[0]
    tm = ROUTER_TM

    def chunk_of(i):
        return sum((i >= row0 // tm).astype(jnp.int32) for row0, _rows in MOE_CHUNKS[1:])

    return pl.pallas_call(
        _router_kernel,
        grid=(t // tm,),
        in_specs=[
            pl.BlockSpec((tm, D_MODEL), lambda i: (i, 0)),
            pl.BlockSpec((1, 6, D_MODEL), lambda i: (i * tm // GROUP_ROWS, 0, 0)),
            pl.BlockSpec((N_EXPERTS, D_MODEL), lambda i: (0, 0)),
            pl.BlockSpec((N_EXPERTS, 1), lambda i: (0, 0)),
        ],
        out_specs=[
            pl.BlockSpec((TOP_K, tm), lambda i: (0, i)),
            pl.BlockSpec((TOP_K, tm), lambda i: (0, i)),
            pl.BlockSpec((tm, LANES), lambda i: (i, 0)),
            pl.BlockSpec((1, N_EXPERTS, LANES), lambda i: (chunk_of(i), 0, 0)),
            pl.BlockSpec((tm, D_QUARTER), lambda i: (i, 0)),
            pl.BlockSpec((tm, D_QUARTER), lambda i: (i, 0)),
        ],
        out_shape=[
            jax.ShapeDtypeStruct((TOP_K, t), jnp.int32),
            jax.ShapeDtypeStruct((TOP_K, t), jnp.int32),
            jax.ShapeDtypeStruct((t, LANES), F32),
            jax.ShapeDtypeStruct((len(MOE_CHUNKS), N_EXPERTS, LANES), F32),
            jax.ShapeDtypeStruct((t, D_QUARTER), jnp.uint32),
            jax.ShapeDtypeStruct((t, D_QUARTER), jnp.uint32),
        ],
        scratch_shapes=[pltpu.VMEM((N_EXPERTS, 1), F32)],
        compiler_params=pltpu.CompilerParams(dimension_semantics=("arbitrary",)),
        name="moe_router",
    )(x, mods, router_wt, router_bias)


FFN_TM = 512


def _n_tiles(rows):
    return rows * TOP_K // FFN_TM + N_EXPERTS


def _expert_layout(counts, n_tiles):
    padded = (counts + FFN_TM - 1) // FFN_TM * FFN_TM
    end = jnp.cumsum(padded)
    start = end - padded
    tile_row = jnp.arange(n_tiles, dtype=jnp.int32) * FFN_TM
    tile_expert = jnp.minimum(jnp.sum(end[None, :] <= tile_row[:, None], axis=1), N_EXPERTS - 1)
    of_tile = tile_expert[:, None] == jnp.arange(N_EXPERTS, dtype=jnp.int32)[None, :]
    live_end = jnp.sum(jnp.where(of_tile, (start + counts)[None, :], 0), axis=1)
    tile_valid = jnp.clip(live_end - tile_row, 0, FFN_TM)
    n_used = (end[-1] // FFN_TM).astype(jnp.int32).reshape(1)
    return start.astype(jnp.int32), tile_expert.astype(jnp.int32), tile_valid.astype(jnp.int32), n_used


D_HALF = D_MODEL // 2
D_QUARTER = D_MODEL // 4


def _pack_rows(v):
    hi = lax.bitcast_convert_type(v[:, :D_HALF].astype(BF16).astype(F32), jnp.uint32)
    lo = lax.bitcast_convert_type(v[:, D_HALF:].astype(BF16).astype(F32), jnp.uint32)
    return jnp.bitwise_or(hi, jnp.right_shift(lo, jnp.uint32(16)))


def _unpack_rows(w):
    hi = lax.bitcast_convert_type(jnp.bitwise_and(w, jnp.uint32(0xFFFF0000)), F32)
    lo = lax.bitcast_convert_type(jnp.left_shift(w, jnp.uint32(16)), F32)
    return hi, lo


DEST_TM = 2048


def _dest_kernel(start_ref, idx_ref, pos_ref, dest_ref):
    idx = idx_ref[...]
    base = jnp.zeros(idx.shape, jnp.int32)
    for e in range(N_EXPERTS):
        base = jnp.where(idx == e, start_ref[e], base)
    dest_ref[...] = base + pos_ref[...]


def _dest_rows(start, idx, pos):
    t = idx.shape[1]
    return pl.pallas_call(
        _dest_kernel,
        grid_spec=pltpu.PrefetchScalarGridSpec(
            num_scalar_prefetch=1,
            grid=(t // DEST_TM,),
            in_specs=[pl.BlockSpec((TOP_K, DEST_TM), lambda i, s: (0, i)),
                      pl.BlockSpec((TOP_K, DEST_TM), lambda i, s: (0, i))],
            out_specs=pl.BlockSpec((TOP_K, DEST_TM), lambda i, s: (0, i)),
        ),
        out_shape=jax.ShapeDtypeStruct((TOP_K, t), jnp.int32),
        compiler_params=pltpu.CompilerParams(dimension_semantics=("parallel",)),
        name="moe_dest",
    )(start, idx, pos)


SC_WINDOW = 128


def _sc_mesh():
    return plsc.VectorSubcoreMesh(core_axis_name="c", subcore_axis_name="s")


def _sc_scatter_rows(x, dest, row0, n_sorted):
    t = dest.shape[1]
    blk0 = row0 // SC_WINDOW

    @functools.partial(
        pl.kernel,
        out_type=jax.ShapeDtypeStruct((n_sorted, D_QUARTER), x.dtype),
        mesh=_sc_mesh(),
        scratch_types=[],
    )
    def scatter(x_hbm, i_hbm, o_hbm):
        def body(x_vmem, i_vmem):
            pltpu.sync_copy(x_vmem, o_hbm.at[i_vmem.at[0]])

        pltpu.emit_pipeline(
            body,
            grid=(t // SC_WINDOW, TOP_K),
            in_specs=[pl.BlockSpec((SC_WINDOW, D_QUARTER), lambda i, k: (i + blk0, 0)),
                      pl.BlockSpec((1, SC_WINDOW), lambda i, k: (k, i))],
            out_specs=[],
            core_axis_name=("c", "s"),
            dimension_semantics=(pltpu.PARALLEL, pltpu.ARBITRARY),
        )(x_hbm, i_hbm)

    return scatter(x, dest)


def _sc_gather_rows(table, idx):
    m = idx.shape[1]

    @functools.partial(
        pl.kernel,
        out_type=jax.ShapeDtypeStruct((m, D_QUARTER), table.dtype),
        mesh=_sc_mesh(),
        scratch_types=[],
    )
    def gather(t_hbm, i_hbm, o_hbm):
        def body(i_vmem, o_vmem):
            pltpu.sync_copy(t_hbm.at[i_vmem.at[0]], o_vmem)

        pltpu.emit_pipeline(
            body,
            grid=(m // SC_WINDOW,),
            in_specs=[pl.BlockSpec((1, SC_WINDOW), lambda i: (0, i))],
            out_specs=[pl.BlockSpec((SC_WINDOW, D_QUARTER), lambda i: (i, 0))],
            core_axis_name=("c", "s"),
            dimension_semantics=(pltpu.PARALLEL,),
        )(i_hbm, o_hbm)

    return gather(table, idx)


def _ffn_kernel(te_ref, tv_ref, nu_ref, xa_ref, xb_ref, wg_ref, wu_ref, wd_ref, ya_ref, yb_ref,
                wgb_ref, wub_ref, wdb_ref):
    i = pl.program_id(0)

    @pl.when(jnp.logical_and(i < nu_ref[0],
                             jnp.logical_or(i == 0, te_ref[i] != te_ref[jnp.maximum(i - 1, 0)])))
    def _():
        wgb_ref[...] = wg_ref[0, 0].astype(BF16)
        wub_ref[...] = wu_ref[0, 0].astype(BF16)
        wdb_ref[...] = wd_ref[0, 0].astype(BF16)

    @pl.when(i < nu_ref[0])
    def _():
        wg = wgb_ref[...]
        wu = wub_ref[...]
        wd = wdb_ref[...]
        th = FFN_TM // 2
        for r0 in (0, th):
            rows = pl.ds(r0, th)
            live = (lax.broadcasted_iota(jnp.int32, (th, D_QUARTER), 0) + r0) < tv_ref[i]
            hi_a, lo_a = _unpack_rows(jnp.where(live, xa_ref[rows, :], jnp.uint32(0)))
            hi_b, lo_b = _unpack_rows(jnp.where(live, xb_ref[rows, :], jnp.uint32(0)))
            xb = jnp.concatenate([hi_a.astype(BF16), hi_b.astype(BF16), lo_a.astype(BF16), lo_b.astype(BF16)],
                                 axis=1)
            a = (_silu(_dot(xb, wg)) * _dot(xb, wu)).astype(BF16)
            words = _pack_rows(_dot(a, wd))
            ya_ref[rows, :] = words[:, :D_QUARTER]
            yb_ref[rows, :] = words[:, D_QUARTER:]

    @pl.when(i >= nu_ref[0])
    def _():
        ya_ref[...] = jnp.zeros(ya_ref.shape, jnp.uint32)
        yb_ref[...] = jnp.zeros(yb_ref.shape, jnp.uint32)


def _expert_ffn(xs_a, xs_b, tile_expert, tile_valid, n_used, wg, wu, wd, layer):
    n_tiles = xs_a.shape[0] // FFN_TM

    def row_map(i, te, tv, nu):
        return (jnp.minimum(i, nu[0] - 1), 0)

    def out_map(i, te, tv, nu):
        return (i, 0)

    def w_map(i, te, tv, nu):
        return (layer, te[jnp.minimum(i, nu[0] - 1)], 0, 0)

    return pl.pallas_call(
        _ffn_kernel,
        grid_spec=pltpu.PrefetchScalarGridSpec(
            num_scalar_prefetch=3,
            grid=(n_tiles,),
            in_specs=[
                pl.BlockSpec((FFN_TM, D_QUARTER), row_map),
                pl.BlockSpec((FFN_TM, D_QUARTER), row_map),
                pl.BlockSpec((1, 1, D_MODEL, D_EXPERT), w_map),
                pl.BlockSpec((1, 1, D_MODEL, D_EXPERT), w_map),
                pl.BlockSpec((1, 1, D_EXPERT, D_MODEL), w_map),
            ],
            out_specs=[pl.BlockSpec((FFN_TM, D_QUARTER), out_map), pl.BlockSpec((FFN_TM, D_QUARTER), out_map)],
            scratch_shapes=[pltpu.VMEM((D_MODEL, D_EXPERT), BF16), pltpu.VMEM((D_MODEL, D_EXPERT), BF16),
                            pltpu.VMEM((D_EXPERT, D_MODEL), BF16)],
        ),
        out_shape=[jax.ShapeDtypeStruct((n_tiles * FFN_TM, D_QUARTER), jnp.uint32),
                   jax.ShapeDtypeStruct((n_tiles * FFN_TM, D_QUARTER), jnp.uint32)],
        compiler_params=pltpu.CompilerParams(dimension_semantics=("arbitrary",)),
        name="moe_expert_ffn",
    )(tile_expert, tile_valid, n_used, xs_a, xs_b, wg, wu, wd)


COMBINE_TM = 256


def _combine_kernel(n_prev, tile0, x_ref, mods_ref, wcol_ref, ga_ref, gb_ref, sg_ref, su_ref, sd_ref, g_ref, b_ref,
                    *rest):
    o_refs = rest[n_prev:]
    m = mods_ref[0]
    sf, cf, gf = m[3:4], m[4:5], m[5:6]
    x = x_ref[...]
    hc = (x * (1.0 + cf) + sf).astype(BF16)
    a = _silu(_dot(hc, sg_ref[...])) * _dot(hc, su_ref[...])
    y = _dot(a.astype(BF16), sd_ref[...])
    wcol = wcol_ref[...]
    parts = [y[:, q * D_QUARTER:(q + 1) * D_QUARTER] for q in range(4)]
    for k in range(TOP_K):
        hi_a, lo_a = _unpack_rows(ga_ref[k])
        hi_b, lo_b = _unpack_rows(gb_ref[k])
        wk = wcol[:, k:k + 1]
        parts = [parts[0] + wk * hi_a, parts[1] + wk * hi_b, parts[2] + wk * lo_a, parts[3] + wk * lo_b]
    y = jnp.concatenate(parts, axis=1)
    out = _layer_norm(ALPHA * x + gf * y, g_ref[...], b_ref[...])
    if len(o_refs) == 1:
        o_refs[0][...] = out
    else:
        is_ctx = pl.program_id(0) + tile0 < T_CTX // x_ref.shape[0]

        @pl.when(is_ctx)
        def _():
            o_refs[0][...] = out

        @pl.when(jnp.logical_not(is_ctx))
        def _():
            o_refs[1][...] = out


def _combine(x, mods, wcol, g_a, g_b, sg, su, sd, ln_g, ln_b, row0, split_streams, prev):
    tm = COMBINE_TM
    n = g_a.shape[1] // tm
    tile0 = row0 // tm
    n_ctx = T_CTX // tm
    full = lambda shape: pl.BlockSpec(shape, lambda i: (0,) * len(shape))
    layouts = {
        "all": (lambda i: (tile0 + i, 0), T_ALL),
        "ctx": (lambda i: (jnp.minimum(tile0 + i, n_ctx - 1), 0), T_CTX),
        "lat": (lambda i: (jnp.maximum(tile0 + i - n_ctx, 0), 0), T_LAT),
    }
    if split_streams:
        kinds = (["ctx"] if tile0 < n_ctx else []) + (["lat"] if tile0 + n > n_ctx else [])
    else:
        kinds = ["all"]
    carried = [kd for kd in kinds if kd in prev]
    outs = pl.pallas_call(
        functools.partial(_combine_kernel, len(carried), tile0),
        grid=(n,),
        in_specs=[
            pl.BlockSpec((tm, D_MODEL), lambda i: (tile0 + i, 0)),
            pl.BlockSpec((1, 6, D_MODEL), lambda i: ((tile0 + i) * tm // GROUP_ROWS, 0, 0)),
            pl.BlockSpec((tm, LANES), lambda i: (tile0 + i, 0)),
            pl.BlockSpec((TOP_K, tm, D_QUARTER), lambda i: (0, i, 0)),
            pl.BlockSpec((TOP_K, tm, D_QUARTER), lambda i: (0, i, 0)),
            full((D_MODEL, D_SHARED)), full((D_MODEL, D_SHARED)), full((D_SHARED, D_MODEL)),
            full((1, D_MODEL)), full((1, D_MODEL)),
        ] + [pl.BlockSpec(memory_space=pl.ANY) for _ in carried],
        out_specs=[pl.BlockSpec((tm, D_MODEL), layouts[kd][0]) for kd in kinds],
        out_shape=[jax.ShapeDtypeStruct((layouts[kd][1], D_MODEL), F32) for kd in kinds],
        input_output_aliases={10 + j: kinds.index(kd) for j, kd in enumerate(carried)},
        compiler_params=pltpu.CompilerParams(dimension_semantics=("arbitrary",)),
        name="moe_combine",
    )(x, mods, wcol, g_a, g_b, sg, su, sd, ln_g, ln_b, *[prev[kd] for kd in carried])
    return {**prev, **dict(zip(kinds, outs))}


MLA_TM = 512


def _mla_proj_kernel(x_ref, mods_ref, wdq_ref, wdkv_ref, wkr_ref, qn_ref, kvn_ref, wuq_ref,
                     wukn_ref, wuv_ref, ta_ref, tb_ref, q_ref, k_ref, v_ref, ckv_ref, kr_ref):
    m = mods_ref[0]
    sm, cm = m[0:1], m[1:2]
    h = (x_ref[...] * (1.0 + cm) + sm).astype(BF16)
    cq = _rms_norm(_dot(h, wdq_ref[...]), qn_ref[...])
    ckv = _rms_norm(_dot(h, wdkv_ref[...]), kvn_ref[...])
    kr2 = _dot(h, wkr_ref[...])
    ckv_ref[...] = ckv
    kr_ref[...] = kr2

    ka = ta_ref[0]
    kb = tb_ref[0]
    tm = ka.shape[0]
    ta = jnp.concatenate([jnp.full((tm, QK_NOPE), Q_PRESCALE, F32), ka * Q_PRESCALE], axis=1)
    tb = jnp.concatenate([jnp.zeros((tm, QK_NOPE), F32), kb * Q_PRESCALE], axis=1)
    krr = kr2 * ka + pltpu.roll(kr2, QK_ROPE, 1) * kb

    qpre = _dot(cq.astype(BF16), wuq_ref[...])
    ckv_b = ckv.astype(BF16)
    kn = _dot(ckv_b, wukn_ref[...])
    v_ref[...] = _dot(ckv_b, wuv_ref[...]).astype(BF16)
    for hd in range(N_HEADS):
        qh = qpre[:, hd * HEAD_PAD:(hd + 1) * HEAD_PAD]
        qrot = qh * ta + pltpu.roll(qh, HEAD_PAD - QK_ROPE, 1) * tb
        q_ref[:, hd * HEAD_PAD:(hd + 1) * HEAD_PAD] = qrot.astype(BF16)
        k_ref[:, hd * HEAD_PAD:hd * HEAD_PAD + QK_NOPE] = kn[:, hd * QK_NOPE:(hd + 1) * QK_NOPE].astype(BF16)
        k_ref[:, hd * HEAD_PAD + QK_NOPE:(hd + 1) * HEAD_PAD] = krr.astype(BF16)


def _mla_proj(x, mods, p, rope_a, rope_b):
    t = x.shape[0]
    tm = MLA_TM
    full = lambda shape: pl.BlockSpec(shape, lambda i: (0,) * len(shape))
    rope_spec = pl.BlockSpec(
        (1, tm, 2 * QK_ROPE),
        lambda i: (jnp.minimum(i * tm // GROUP_ROWS, 1), (i * tm % GROUP_ROWS) // tm, 0))
    return pl.pallas_call(
        _mla_proj_kernel,
        grid=(t // tm,),
        in_specs=[
            pl.BlockSpec((tm, D_MODEL), lambda i: (i, 0)),
            pl.BlockSpec((1, 6, D_MODEL), lambda i: (i * tm // GROUP_ROWS, 0, 0)),
            full((D_MODEL, Q_LORA)), full((D_MODEL, KV_LORA)), full((D_MODEL, 2 * QK_ROPE)),
            full((1, Q_LORA)), full((1, KV_LORA)),
            full((Q_LORA, N_HEADS * HEAD_PAD)),
            full((KV_LORA, N_HEADS * QK_NOPE)), full((KV_LORA, N_HEADS * V_DIM)),
            rope_spec, rope_spec,
        ],
        out_specs=[
            pl.BlockSpec((tm, N_HEADS * HEAD_PAD), lambda i: (i, 0)),
            pl.BlockSpec((tm, N_HEADS * HEAD_PAD), lambda i: (i, 0)),
            pl.BlockSpec((tm, N_HEADS * V_DIM), lambda i: (i, 0)),
            pl.BlockSpec((tm, KV_LORA), lambda i: (i, 0)),
            pl.BlockSpec((tm, 2 * QK_ROPE), lambda i: (i, 0)),
        ],
        out_shape=[
            jax.ShapeDtypeStruct((t, N_HEADS * HEAD_PAD), BF16),
            jax.ShapeDtypeStruct((t, N_HEADS * HEAD_PAD), BF16),
            jax.ShapeDtypeStruct((t, N_HEADS * V_DIM), BF16),
            jax.ShapeDtypeStruct((t, KV_LORA), F32),
            jax.ShapeDtypeStruct((t, 2 * QK_ROPE), F32),
        ],
        compiler_params=pltpu.CompilerParams(dimension_semantics=("parallel",)),
        name="mla_proj",
    )(x, mods, p["w_dq"], p["w_dkv"], p["w_kr"], p["q_norm"], p["kv_norm"], p["w_uq"],
      p["w_ukn"], p["w_uv"], rope_a, rope_b)


def _cache_kv_kernel(ckv_ref, kr_ref, wukn_ref, wuv_ref, k_ref, v_ref):
    ckv_b = ckv_ref[...].astype(BF16)
    kn = _dot(ckv_b, wukn_ref[...])
    v_ref[...] = _dot(ckv_b, wuv_ref[...]).astype(BF16)
    kr = kr_ref[...].astype(BF16)
    for hd in range(N_HEADS):
        k_ref[:, hd * HEAD_PAD:hd * HEAD_PAD + QK_NOPE] = kn[:, hd * QK_NOPE:(hd + 1) * QK_NOPE].astype(BF16)
        k_ref[:, hd * HEAD_PAD + QK_NOPE:(hd + 1) * HEAD_PAD] = kr


def _cache_kv(ckv, kr_pad, p):
    t = ckv.shape[0]
    tm = PAST_LEN
    full = lambda shape: pl.BlockSpec(shape, lambda i: (0,) * len(shape))
    return pl.pallas_call(
        _cache_kv_kernel,
        grid=(t // tm,),
        in_specs=[
            pl.BlockSpec((tm, KV_LORA), lambda i: (i, 0)),
            pl.BlockSpec((tm, 2 * QK_ROPE), lambda i: (i, 0)),
            full((KV_LORA, N_HEADS * QK_NOPE)), full((KV_LORA, N_HEADS * V_DIM)),
        ],
        out_specs=[
            pl.BlockSpec((tm, N_HEADS * HEAD_PAD), lambda i: (i, 0)),
            pl.BlockSpec((tm, N_HEADS * V_DIM), lambda i: (i, 0)),
        ],
        out_shape=[
            jax.ShapeDtypeStruct((t, N_HEADS * HEAD_PAD), BF16),
            jax.ShapeDtypeStruct((t, N_HEADS * V_DIM), BF16),
        ],
        compiler_params=pltpu.CompilerParams(dimension_semantics=("parallel",)),
        name="mla_cache_kv",
    )(ckv, kr_pad, p["w_ukn"], p["w_uv"])


def _ctx_attn_kernel(q_ref, k_ref, v_ref, o_ref):
    for hd in range(N_HEADS):
        q = q_ref[:, hd * HEAD_PAD:(hd + 1) * HEAD_PAD]
        k = k_ref[:, hd * HEAD_PAD:(hd + 1) * HEAD_PAD]
        s = _dot_nt(q, k)
        s = s - jnp.max(s, axis=-1, keepdims=True)
        p = jnp.exp2(s)
        p = p / jnp.sum(p, axis=-1, keepdims=True)
        o = _dot(p.astype(BF16), v_ref[:, hd * V_DIM:(hd + 1) * V_DIM])
        o_ref[:, hd * V_DIM:(hd + 1) * V_DIM] = o.astype(BF16)


def _ctx_attention(q, k, v):
    return pl.pallas_call(
        _ctx_attn_kernel,
        grid=(BATCH,),
        in_specs=[
            pl.BlockSpec((SEQ, N_HEADS * HEAD_PAD), lambda b: (b, 0)),
            pl.BlockSpec((SEQ, N_HEADS * HEAD_PAD), lambda b: (b, 0)),
            pl.BlockSpec((SEQ, N_HEADS * V_DIM), lambda b: (b, 0)),
        ],
        out_specs=pl.BlockSpec((SEQ, N_HEADS * V_DIM), lambda b: (b, 0)),
        out_shape=jax.ShapeDtypeStruct((T_CTX, N_HEADS * V_DIM), BF16),
        compiler_params=pltpu.CompilerParams(dimension_semantics=("parallel",)),
        name="ctx_attention",
    )(q, k, v)


LAT_TQ = 1024
LAT_TK = 512
LAT_PIECES = 4


def _lat_attn_kernel(q_ref, k_ref, v_ref, kc_ref, vc_ref, o_ref, s_ref, p_ref, m_ref):
    n_chunks = DEC_SEQ // LAT_TK + 1
    tp = q_ref.shape[0] // LAT_PIECES
    pieces = [pl.ds(j * tp, tp) for j in range(LAT_PIECES)]
    groups = [pieces[:2], pieces[2:]]
    state = {}

    def keys(c):
        return kc_ref[...] if c == n_chunks - 1 else k_ref[c * LAT_TK:(c + 1) * LAT_TK, :]

    def values(c):
        return vc_ref[...] if c == n_chunks - 1 else v_ref[c * LAT_TK:(c + 1) * LAT_TK, :]

    def qk(r, c):
        s = _dot_nt(q_ref[r, :], keys(c))
        s_ref[r, c * LAT_TK:(c + 1) * LAT_TK] = s
        mp = state.get(("m", r.start), jnp.full((tp, LANES), NEG_INF, F32))
        for j in range(LAT_TK // LANES):
            mp = jnp.maximum(mp, s[:, j * LANES:(j + 1) * LANES])
        state[("m", r.start)] = mp

    def row_max(r):
        m_ref[r, :] = jnp.broadcast_to(jnp.max(state[("m", r.start)], axis=-1, keepdims=True), (tp, LANES))

    def exp_chunk(r, c, after=None):
        lp = state.get(("l", r.start), jnp.zeros((tp, LANES), F32))
        m = m_ref[r, :]
        if after is not None:
            bits = lax.bitcast_convert_type(after, jnp.uint32)
            zero = lax.shift_right_logical(lax.shift_right_logical(bits, jnp.uint32(16)), jnp.uint32(16))
            m = m + lax.bitcast_convert_type(zero, F32)
        for j in range(c * LAT_TK // LANES, (c + 1) * LAT_TK // LANES):
            p = jnp.exp2(s_ref[r, j * LANES:(j + 1) * LANES] - m)
            lp = lp + p
            p_ref[r, j * LANES:(j + 1) * LANES] = p.astype(BF16)
        state[("l", r.start)] = lp

    def pv(r, c):
        acc = state.get(("a", r.start), jnp.zeros((tp, V_DIM), F32))
        state[("a", r.start)] = acc + _dot(p_ref[r, c * LAT_TK:(c + 1) * LAT_TK], values(c))

    def finish(r):
        o_ref[r, :] = (state[("a", r.start)] / jnp.sum(state[("l", r.start)], axis=-1, keepdims=True)).astype(BF16)

    for c in range(n_chunks):
        for r in groups[0]:
            qk(r, c)
    for r in groups[0]:
        row_max(r)
    for c in range(n_chunks):
        for r in groups[1]:
            qk(r, c)
        for r in groups[0]:
            exp_chunk(r, c)
    for r in groups[1]:
        row_max(r)
    for c in range(n_chunks):
        for r in groups[0]:
            pv(r, c)
        for r0, r in zip(groups[0], groups[1]):
            exp_chunk(r, c, after=state[("a", r0.start)])
    for r in groups[0]:
        finish(r)
    for c in range(n_chunks):
        for r in groups[1]:
            pv(r, c)
    for r in groups[1]:
        finish(r)


def _lat_attention(q, k, v, kc, vc):
    nq = DEC_SEQ // LAT_TQ
    return pl.pallas_call(
        _lat_attn_kernel,
        grid=(DEC_BATCH, N_HEADS, nq),
        in_specs=[
            pl.BlockSpec((LAT_TQ, HEAD_PAD), lambda b, h, i: ((b + 1) * nq + i, h)),
            pl.BlockSpec((DEC_SEQ, HEAD_PAD), lambda b, h, i: (b + 1, h)),
            pl.BlockSpec((DEC_SEQ, V_DIM), lambda b, h, i: (b + 1, h)),
            pl.BlockSpec((PAST_LEN, HEAD_PAD), lambda b, h, i: (b, h)),
            pl.BlockSpec((PAST_LEN, V_DIM), lambda b, h, i: (b, h)),
        ],
        out_specs=pl.BlockSpec((LAT_TQ, V_DIM), lambda b, h, i: (b * nq + i, h)),
        out_shape=jax.ShapeDtypeStruct((T_LAT, N_HEADS * V_DIM), BF16),
        scratch_shapes=[pltpu.VMEM((LAT_TQ, DEC_SEQ + PAST_LEN), F32),
                        pltpu.VMEM((LAT_TQ, DEC_SEQ + PAST_LEN), BF16),
                        pltpu.VMEM((LAT_TQ, LANES), F32)],
        compiler_params=pltpu.CompilerParams(
            dimension_semantics=("parallel", "parallel", "parallel")),
        name="lat_attention",
    )(q, k, v, kc, vc)


OPROJ_TM = 512


def _oproj_kernel(oc_ref, ol_ref, x_ref, mods_ref, wo_ref, g_ref, b_ref, out_ref):
    is_ctx = pl.program_id(0) < T_CTX // OPROJ_TM
    o = jnp.where(is_ctx, oc_ref[...], ol_ref[...])
    gm = mods_ref[0][2:3]
    y = _dot(o, wo_ref[...])
    out_ref[...] = _layer_norm(ALPHA * x_ref[...] + gm * y, g_ref[...], b_ref[...])


def _oproj(o_ctx, o_lat, x, mods, w_o, ln_g, ln_b):
    t = x.shape[0]
    tm = OPROJ_TM
    n_ctx = T_CTX // tm
    return pl.pallas_call(
        _oproj_kernel,
        grid=(t // tm,),
        in_specs=[
            pl.BlockSpec((tm, N_HEADS * V_DIM), lambda i: (jnp.minimum(i, n_ctx - 1), 0)),
            pl.BlockSpec((tm, N_HEADS * V_DIM), lambda i: (jnp.maximum(i - n_ctx, 0), 0)),
            pl.BlockSpec((tm, D_MODEL), lambda i: (i, 0)),
            pl.BlockSpec((1, 6, D_MODEL), lambda i: (i * tm // GROUP_ROWS, 0, 0)),
            pl.BlockSpec((N_HEADS * V_DIM, D_MODEL), lambda i: (0, 0)),
            pl.BlockSpec((1, D_MODEL), lambda i: (0, 0)),
            pl.BlockSpec((1, D_MODEL), lambda i: (0, 0)),
        ],
        out_specs=pl.BlockSpec((tm, D_MODEL), lambda i: (i, 0)),
        out_shape=jax.ShapeDtypeStruct((t, D_MODEL), F32),
        compiler_params=pltpu.CompilerParams(dimension_semantics=("parallel",)),
        name="attn_oproj",
    )(o_ctx, o_lat, x, mods, w_o, ln_g, ln_b)


def _swap16(w):
    q = QK_ROPE // 4
    return jnp.concatenate([w[..., q:2 * q], w[..., :q], w[..., 3 * q:], w[..., 2 * q:3 * q]], axis=-1)


def _mla_params(w_dqkv, q_norm, w_uq, kv_norm, w_ukv):
    w_kr = w_dqkv[:, Q_LORA + KV_LORA:]
    wq = w_uq.reshape(Q_LORA, N_HEADS, QK_NOPE + QK_ROPE)
    wq_r = wq[..., QK_NOPE:]
    wq = jnp.concatenate([wq[..., :QK_NOPE], wq_r, _swap16(wq_r)], axis=-1)
    wkv = w_ukv.reshape(KV_LORA, N_HEADS, QK_NOPE + V_DIM)
    return {
        "w_dq": w_dqkv[:, :Q_LORA].astype(BF16),
        "w_dkv": w_dqkv[:, Q_LORA:Q_LORA + KV_LORA].astype(BF16),
        "w_kr": jnp.concatenate([w_kr, _swap16(w_kr)], axis=-1).astype(BF16),
        "q_norm": q_norm.reshape(1, Q_LORA),
        "kv_norm": kv_norm.reshape(1, KV_LORA),
        "w_uq": wq.reshape(Q_LORA, N_HEADS * HEAD_PAD).astype(BF16),
        "w_ukn": wkv[..., :QK_NOPE].reshape(KV_LORA, N_HEADS * QK_NOPE).astype(BF16),
        "w_uv": wkv[..., QK_NOPE:].reshape(KV_LORA, N_HEADS * V_DIM).astype(BF16),
    }


def _rope_tables():
    nf = QK_ROPE // 4
    t = np.arange(DEC_SEQ)
    row = (t // GRID_W).astype(np.float32)
    col = (t % GRID_W).astype(np.float32)
    inv = (ROPE_THETA ** (-np.arange(nf, dtype=np.float32) / nf)).astype(np.float32)
    ar, ac = row[:, None] * inv, col[:, None] * inv
    pad = np.zeros((DEC_SEQ, QK_ROPE), np.float32)
    cos = np.concatenate([np.cos(ar), np.cos(ar), np.cos(ac), np.cos(ac), pad], axis=-1)
    sin = np.concatenate([-np.sin(ar), np.sin(ar), -np.sin(ac), np.sin(ac), pad], axis=-1)
    cos_id = np.concatenate([np.ones((DEC_SEQ, QK_ROPE), np.float32), pad], axis=-1)
    sin_id = np.zeros((DEC_SEQ, 2 * QK_ROPE), np.float32)
    return (jnp.asarray(np.stack([cos_id, cos]).astype(np.float32)),
            jnp.asarray(np.stack([sin_id, sin]).astype(np.float32)))


def kernel(x_prompt, x_sample, cache_ckv, cache_krope, c, c_ctx, ada_w, ada_b, ln_g, ln_b, conv_w_in, conv_k, conv_w_out, mla_w_dqkv, mla_q_norm, mla_w_uq, mla_kv_norm, mla_w_ukv, mla_w_o, router_w, router_bias, exp_w_gate, exp_w_up, exp_w_down, sh_w_gate, sh_w_up, sh_w_down):
    cvecs =jnp.concatenate([c_ctx[None, :], c, jnp.zeros((SUBLANES - N_GROUPS_ROWS, D_MODEL), F32)], axis=0)
    mods = _adaln(cvecs, ada_w, ada_b)

    def ln(l, k):
        return ln_g[l, k].reshape(1, D_MODEL), ln_b[l, k].reshape(1, D_MODEL)

    def moe_layer(xin, l, split_streams):
        idx, pos, wcol, counts, hp_a, hp_b = _router(xin, mods[l], router_w[l].T,
                                                     router_bias[l].reshape(N_EXPERTS, 1))
        shared = (sh_w_gate[l].astype(BF16), sh_w_up[l].astype(BF16), sh_w_down[l].astype(BF16))
        outs = {}
        for ch in sorted(range(len(MOE_CHUNKS)), key=lambda j: -MOE_CHUNKS[j][1]):
            row0, rows = MOE_CHUNKS[ch]
            n_tiles = _n_tiles(rows)
            cnt = counts[ch, :, 0].astype(jnp.int32)
            start, tile_expert, tile_valid, n_used = _expert_layout(cnt, n_tiles)
            dest = _dest_rows(start, idx[:, row0:row0 + rows], pos[:, row0:row0 + rows])
            xs_a = _sc_scatter_rows(hp_a, dest, row0, n_tiles * FFN_TM)
            xs_b = _sc_scatter_rows(hp_b, dest, row0, n_tiles * FFN_TM)
            ys_a, ys_b = _expert_ffn(xs_a, xs_b, tile_expert, tile_valid, n_used,
                                     exp_w_gate, exp_w_up, exp_w_down, l)
            piece = rows // COMBINE_SPLITS
            for s in range(COMBINE_SPLITS):
                dest_row = dest[:, s * piece:(s + 1) * piece].reshape(1, piece * TOP_K)
                g_a = _sc_gather_rows(ys_a, dest_row).reshape(TOP_K, piece, D_QUARTER)
                g_b = _sc_gather_rows(ys_b, dest_row).reshape(TOP_K, piece, D_QUARTER)
                outs = _combine(xin, mods[l], wcol, g_a, g_b, *shared, *ln(l, 1), row0=row0 + s * piece,
                                split_streams=split_streams, prev=outs)
        return (outs["ctx"], outs["lat"]) if split_streams else outs["all"]

    x = _conv_mixer(x_prompt.reshape(T_CTX, D_MODEL), x_sample.reshape(T_LAT, D_MODEL), mods[0],
                    conv_w_in[0].astype(BF16), conv_k[0], conv_w_out[0].astype(BF16), *ln(0, 0))
    x = moe_layer(x, 0, split_streams=False)

    p = _mla_params(mla_w_dqkv[0], mla_q_norm[0], mla_w_uq[0], mla_kv_norm[0], mla_w_ukv[0])
    rope_a, rope_b = _rope_tables()
    q, k, v, ckv, kr = _mla_proj(x, mods[1], p, rope_a, rope_b)
    kr_cache = jnp.concatenate([cache_krope[:, 0].reshape(DEC_BATCH * PAST_LEN, QK_ROPE),
                                jnp.zeros((DEC_BATCH * PAST_LEN, QK_ROPE), F32)], axis=-1)
    kc, vc = _cache_kv(cache_ckv[:, 0].reshape(DEC_BATCH * PAST_LEN, KV_LORA), kr_cache, p)
    o_ctx = _ctx_attention(q, k, v)
    o_lat = _lat_attention(q, k, v, kc, vc)
    x = _oproj(o_ctx, o_lat, x, mods[1], mla_w_o[0].astype(BF16), *ln(1, 0))
    y_ctx, y_lat = moe_layer(x, 1, split_streams=True)

    y_prompt = y_ctx.reshape(BATCH, SEQ, D_MODEL)
    y_sample = y_lat.reshape(DEC_BATCH, DEC_SEQ, D_MODEL)
    state_ckv = ckv[:T_CTX].reshape(BATCH, 1, SEQ, KV_LORA)
    state_krope = kr[:T_CTX, :QK_ROPE].reshape(BATCH, 1, SEQ, QK_ROPE)
    return (y_prompt, y_sample, state_ckv, state_krope)
```

```python
import functools
import math

import jax
import jax.numpy as jnp
import numpy as np
from jax import lax
from jax.experimental import pallas as pl
from jax.experimental.pallas import tpu as pltpu
from jax.experimental.pallas import tpu_sc as plsc

D_MODEL = 1024
BATCH = 16
SEQ = 256
DEPTH = 2
DEC_BATCH = 4
DEC_SEQ = 4096
PAST_LEN = 512
GRID_W = 64

N_HEADS = 8
QK_NOPE = 128
QK_ROPE = 64
V_DIM = 128
Q_LORA = 384
KV_LORA = 256
ROPE_THETA = 10000.0
ATTN_SCALE = (QK_NOPE + QK_ROPE) ** -0.5
HEAD_PAD = 256
Q_PRESCALE = ATTN_SCALE * math.log2(math.e)

N_EXPERTS = 64
TOP_K = 8
N_GROUPS = 8
TOPK_GROUPS = 4
GROUP_SIZE = N_EXPERTS // N_GROUPS
D_EXPERT = 256
D_SHARED = 256
ROUTED_SCALE = 2.5

ALPHA = (2 * DEPTH) ** 0.25
LN_EPS = 1e-5
RMS_EPS = 1e-6

GROUP_ROWS = 4096
N_GROUPS_ROWS = 1 + DEC_BATCH
T_CTX = BATCH * SEQ
T_LAT = DEC_BATCH * DEC_SEQ
T_ALL = T_CTX + T_LAT
LANES = 128
SUBLANES = 8

F32 = jnp.float32
BF16 = jnp.bfloat16
NEG_INF = float("-inf")


def _dot(a, b):
    return jnp.dot(a, b, preferred_element_type=F32)


def _dot_nt(a, b, precision=None):
    return lax.dot_general(a, b, (((1,), (1,)), ((), ())), precision=precision,
                           preferred_element_type=F32)


def _layer_norm(v, g, b):
    mu = jnp.mean(v, axis=-1, keepdims=True)
    d = v - mu
    var = jnp.mean(d * d, axis=-1, keepdims=True)
    return d * lax.rsqrt(var + LN_EPS) * g + b


def _rms_norm(v, g):
    return v * lax.rsqrt(jnp.mean(v * v, axis=-1, keepdims=True) + RMS_EPS) * g


def _silu(v):
    return v / (1.0 + jnp.exp(-v))


def _sigmoid(v):
    return 1.0 / (1.0 + jnp.exp(-v))


def _adaln_kernel(c_ref, w_ref, b_ref, o_ref):
    c = c_ref[...]
    s = _silu(c)
    o_ref[0, 0] = jnp.dot(s, w_ref[0], precision=lax.Precision.HIGHEST,
                          preferred_element_type=F32) + b_ref[0]


def _adaln(cvecs, ada_w, ada_b):
    out = pl.pallas_call(
        _adaln_kernel,
        grid=(DEPTH, 6),
        in_specs=[
            pl.BlockSpec((SUBLANES, D_MODEL), lambda l, j: (0, 0)),
            pl.BlockSpec((1, D_MODEL, D_MODEL), lambda l, j: (l, 0, j)),
            pl.BlockSpec((1, 1, D_MODEL), lambda l, j: (l, 0, j)),
        ],
        out_specs=pl.BlockSpec((1, 1, SUBLANES, D_MODEL), lambda l, j: (l, j, 0, 0)),
        out_shape=jax.ShapeDtypeStruct((DEPTH, 6, SUBLANES, D_MODEL), F32),
        compiler_params=pltpu.CompilerParams(dimension_semantics=("parallel", "parallel")),
        name="adaln",
    )(cvecs, ada_w, ada_b.reshape(DEPTH, 1, 6 * D_MODEL))
    return jnp.transpose(out[:, :, :N_GROUPS_ROWS, :], (0, 2, 1, 3))


CONV_TM = 1024


def _conv_kernel(xc_ref, xcp_ref, xcn_ref, xl_ref, xlp_ref, xln_ref, mods_ref, win_ref, ck_ref, wout_ref,
                 g_ref, b_ref, o_ref):
    i = pl.program_id(0)
    tm = xc_ref.shape[0]
    is_ctx = i < T_CTX // tm
    m = mods_ref[0]
    sm, cm, gm = m[0:1], m[1:2], m[2:3]
    x = jnp.where(is_ctx, xc_ref[...], xl_ref[...])
    xp = jnp.where(is_ctx, xcp_ref[...], xlp_ref[...])
    xn = jnp.where(is_ctx, xcn_ref[...], xln_ref[...])
    def halo_u(xh):
        hh = (xh * (1.0 + cm) + sm).astype(BF16)
        zh = _dot(hh, win_ref[:, D_MODEL:])
        return zh[:, :D_MODEL] * zh[:, D_MODEL:]

    th = tm // 2
    halves = [x[:th], x[th:]]
    zs = [_dot((xh * (1.0 + cm) + sm).astype(BF16), win_ref[...]) for xh in halves]
    us = [z[:, D_MODEL:2 * D_MODEL] * z[:, 2 * D_MODEL:] for z in zs]
    befores = [halo_u(xp)[SUBLANES - 1:SUBLANES], us[0][th - 1:th]]
    afters = [us[1][0:1], halo_u(xn)[0:1]]

    ck = ck_ref[...]
    row = lax.broadcasted_iota(jnp.int32, (th, 1), 0)
    for hf in range(2):
        grow = i * tm + hf * th + row
        seq_len = jnp.where(grow < T_CTX, SEQ, DEC_SEQ)
        pos = jnp.bitwise_and(grow, seq_len - 1)
        u = us[hf]
        left = jnp.where(row == 0, befores[hf], pltpu.roll(u, 1, 0))
        left = jnp.where(pos == 0, 0.0, left)
        right = jnp.where(row == th - 1, afters[hf], pltpu.roll(u, th - 1, 0))
        right = jnp.where(pos == seq_len - 1, 0.0, right)
        conv = left * ck[0:1] + u * ck[1:2] + right * ck[2:3]
        v = (zs[hf][:, :D_MODEL] * conv).astype(BF16)
        y = _dot(v, wout_ref[...])
        o_ref[hf * th:(hf + 1) * th, :] = _layer_norm(ALPHA * halves[hf] + gm * y, g_ref[...], b_ref[...])


def _conv_mixer(x_ctx, x_lat, mods, w_in, conv_k, w_out, ln_g, ln_b):
    tm = CONV_TM
    per8 = tm // SUBLANES
    n_ctx = T_CTX // tm
    n_lat = T_LAT // tm

    def stream_specs(first, n):
        def blk(i):
            return jnp.clip(i - first, 0, n - 1)
        return [
            pl.BlockSpec((tm, D_MODEL), lambda i: (blk(i), 0)),
            pl.BlockSpec((SUBLANES, D_MODEL), lambda i: (jnp.maximum(blk(i) * per8 - 1, 0), 0)),
            pl.BlockSpec((SUBLANES, D_MODEL), lambda i: (jnp.minimum((blk(i) + 1) * per8, n * per8 - 1), 0)),
        ]

    return pl.pallas_call(
        _conv_kernel,
        grid=(n_ctx + n_lat,),
        in_specs=stream_specs(0, n_ctx) + stream_specs(n_ctx, n_lat) + [
            pl.BlockSpec((1, 6, D_MODEL), lambda i: (i * tm // GROUP_ROWS, 0, 0)),
            pl.BlockSpec((D_MODEL, 3 * D_MODEL), lambda i: (0, 0)),
            pl.BlockSpec((3, D_MODEL), lambda i: (0, 0)),
            pl.BlockSpec((D_MODEL, D_MODEL), lambda i: (0, 0)),
            pl.BlockSpec((1, D_MODEL), lambda i: (0, 0)),
            pl.BlockSpec((1, D_MODEL), lambda i: (0, 0)),
        ],
        out_specs=pl.BlockSpec((tm, D_MODEL), lambda i: (i, 0)),
        out_shape=jax.ShapeDtypeStruct((T_ALL, D_MODEL), F32),
        compiler_params=pltpu.CompilerParams(dimension_semantics=("parallel",)),
        name="conv_mixer",
    )(x_ctx, x_ctx, x_ctx, x_lat, x_lat, x_lat, mods, w_in, conv_k, w_out, ln_g, ln_b)


ROUTER_TM = 512
MOE_CHUNKS = ((0, T_ALL),)
COMBINE_SPLITS = 4


def _first_argmax_mask(cur, ridx, n):
    mx = jnp.max(cur, axis=0, keepdims=True)
    first = jnp.min(jnp.where(cur == mx, ridx, n), axis=0, keepdims=True)
    return ridx == first, mx


def _router_kernel(x_ref, mods_ref, rwt_ref, bias_ref, idx_ref, pos_ref, wcol_ref, count_ref, hpa_ref, hpb_ref,
                   carry_ref):
    tm = x_ref.shape[0]
    m = mods_ref[0]
    sf, cf = m[3:4], m[4:5]
    hc = x_ref[...] * (1.0 + cf) + sf
    words = _pack_rows(hc)
    hpa_ref[...] = words[:, :D_QUARTER]
    hpb_ref[...] = words[:, D_QUARTER:]
    logits = _dot_nt(rwt_ref[...], hc, precision=lax.Precision.HIGHEST)
    scores = _sigmoid(logits)
    biased = scores + bias_ref[...]

    ridx8 = lax.broadcasted_iota(jnp.int32, (GROUP_SIZE, tm), 0)
    gscore = jnp.full((N_GROUPS, tm), NEG_INF, F32)
    for g in range(N_GROUPS):
        blk = biased[g * GROUP_SIZE:(g + 1) * GROUP_SIZE]
        sel, m1 = _first_argmax_mask(blk, ridx8, GROUP_SIZE)
        m2 = jnp.max(jnp.where(sel, NEG_INF, blk), axis=0, keepdims=True)
        gscore = jnp.where(ridx8 == g, m1 + m2, gscore)

    gmask = jnp.zeros((N_GROUPS, tm), jnp.bool_)
    cur = gscore
    for _ in range(TOPK_GROUPS):
        sel, _unused = _first_argmax_mask(cur, ridx8, N_GROUPS)
        gmask = jnp.logical_or(gmask, sel)
        cur = jnp.where(sel, NEG_INF, cur)

    gmask_f = gmask.astype(F32)
    blocks = []
    for g in range(N_GROUPS):
        keep = jnp.broadcast_to(gmask_f[g:g + 1], (GROUP_SIZE, tm)) > 0.5
        blocks.append(jnp.where(keep, biased[g * GROUP_SIZE:(g + 1) * GROUP_SIZE], NEG_INF))
    cur = jnp.concatenate(blocks, axis=0)

    first_tiles = [row0 // tm for row0, _rows in MOE_CHUNKS]
    starts_chunk = functools.reduce(jnp.logical_or, [pl.program_id(0) == ft for ft in first_tiles])

    @pl.when(starts_chunk)
    def _():
        carry_ref[...] = jnp.zeros(carry_ref.shape, F32)

    ridx = lax.broadcasted_iota(jnp.int32, (N_EXPERTS, tm), 0)
    kidx = lax.broadcasted_iota(jnp.int32, (TOP_K, tm), 0)
    sels = []
    chosen = jnp.zeros((N_EXPERTS, tm), jnp.bool_)
    idx_rows = jnp.zeros((TOP_K, tm), jnp.int32)
    for k in range(TOP_K):
        mx = jnp.max(cur, axis=0, keepdims=True)
        first = jnp.min(jnp.where(cur == mx, ridx, N_EXPERTS), axis=0, keepdims=True)
        sel = ridx == first
        sels.append(sel)
        chosen = jnp.logical_or(chosen, sel)
        idx_rows = jnp.where(kidx == k, first, idx_rows)
        cur = jnp.where(sel, NEG_INF, cur)

    onehot = chosen.astype(F32)
    t_row = lax.broadcasted_iota(jnp.int32, (tm, tm), 0)
    t_col = lax.broadcasted_iota(jnp.int32, (tm, tm), 1)
    before = (t_row < t_col).astype(BF16)
    rank = carry_ref[...] + _dot(onehot.astype(BF16), before)
    carry_ref[...] = carry_ref[...] + jnp.sum(onehot, axis=1, keepdims=True)
    count_ref[0] = jnp.broadcast_to(carry_ref[...], count_ref.shape[1:])

    w = jnp.where(chosen, scores, 0.0)
    w = w / jnp.sum(w, axis=0, keepdims=True) * ROUTED_SCALE
    pos_rows = jnp.zeros((TOP_K, tm), F32)
    w_rows = jnp.zeros((TOP_K, tm), F32)
    for k in range(TOP_K):
        pos_rows = jnp.where(kidx == k, jnp.sum(jnp.where(sels[k], rank, 0.0), axis=0, keepdims=True), pos_rows)
        w_rows = jnp.where(kidx == k, jnp.sum(jnp.where(sels[k], w, 0.0), axis=0, keepdims=True), w_rows)
    idx_ref[...] = idx_rows
    pos_ref[...] = pos_rows.astype(jnp.int32)
    wpad = jnp.concatenate([w_rows, jnp.zeros((LANES - TOP_K, tm), F32)], axis=0)
    wcol_ref[...] = wpad.T


def _router(x, mods, router_wt, router_bias):
    t = x.shape[0]
    tm = ROUTER_TM

    def chunk_of(i):
        return sum((i >= row0 // tm).astype(jnp.int32) for row0, _rows in MOE_CHUNKS[1:])

    return pl.pallas_call(
        _router_kernel,
        grid=(t // tm,),
        in_specs=[
            pl.BlockSpec((tm, D_MODEL), lambda i: (i, 0)),
            pl.BlockSpec((1, 6, D_MODEL), lambda i: (i * tm // GROUP_ROWS, 0, 0)),
            pl.BlockSpec((N_EXPERTS, D_MODEL), lambda i: (0, 0)),
            pl.BlockSpec((N_EXPERTS, 1), lambda i: (0, 0)),
        ],
        out_specs=[
            pl.BlockSpec((TOP_K, tm), lambda i: (0, i)),
            pl.BlockSpec((TOP_K, tm), lambda i: (0, i)),
            pl.BlockSpec((tm, LANES), lambda i: (i, 0)),
            pl.BlockSpec((1, N_EXPERTS, LANES), lambda i: (chunk_of(i), 0, 0)),
            pl.BlockSpec((tm, D_QUARTER), lambda i: (i, 0)),
            pl.BlockSpec((tm, D_QUARTER), lambda i: (i, 0)),
        ],
        out_shape=[
            jax.ShapeDtypeStruct((TOP_K, t), jnp.int32),
            jax.ShapeDtypeStruct((TOP_K, t), jnp.int32),
            jax.ShapeDtypeStruct((t, LANES), F32),
            jax.ShapeDtypeStruct((len(MOE_CHUNKS), N_EXPERTS, LANES), F32),
            jax.ShapeDtypeStruct((t, D_QUARTER), jnp.uint32),
            jax.ShapeDtypeStruct((t, D_QUARTER), jnp.uint32),
        ],
        scratch_shapes=[pltpu.VMEM((N_EXPERTS, 1), F32)],
        compiler_params=pltpu.CompilerParams(dimension_semantics=("arbitrary",)),
        name="moe_router",
    )(x, mods, router_wt, router_bias)


FFN_TM = 512


def _n_tiles(rows):
    return rows * TOP_K // FFN_TM + N_EXPERTS


def _expert_layout(counts, n_tiles):
    padded = (counts + FFN_TM - 1) // FFN_TM * FFN_TM
    end = jnp.cumsum(padded)
    start = end - padded
    tile_row = jnp.arange(n_tiles, dtype=jnp.int32) * FFN_TM
    tile_expert = jnp.minimum(jnp.sum(end[None, :] <= tile_row[:, None], axis=1), N_EXPERTS - 1)
    of_tile = tile_expert[:, None] == jnp.arange(N_EXPERTS, dtype=jnp.int32)[None, :]
    live_end = jnp.sum(jnp.where(of_tile, (start + counts)[None, :], 0), axis=1)
    tile_valid = jnp.clip(live_end - tile_row, 0, FFN_TM)
    n_used = (end[-1] // FFN_TM).astype(jnp.int32).reshape(1)
    return start.astype(jnp.int32), tile_expert.astype(jnp.int32), tile_valid.astype(jnp.int32), n_used


D_HALF = D_MODEL // 2
D_QUARTER = D_MODEL // 4


def _pack_rows(v):
    hi = lax.bitcast_convert_type(v[:, :D_HALF].astype(BF16).astype(F32), jnp.uint32)
    lo = lax.bitcast_convert_type(v[:, D_HALF:].astype(BF16).astype(F32), jnp.uint32)
    return jnp.bitwise_or(hi, jnp.right_shift(lo, jnp.uint32(16)))


def _unpack_rows(w):
    hi = lax.bitcast_convert_type(jnp.bitwise_and(w, jnp.uint32(0xFFFF0000)), F32)
    lo = lax.bitcast_convert_type(jnp.left_shift(w, jnp.uint32(16)), F32)
    return hi, lo


DEST_TM = 2048


def _dest_kernel(start_ref, idx_ref, pos_ref, dest_ref):
    idx = idx_ref[...]
    base = jnp.zeros(idx.shape, jnp.int32)
    for e in range(N_EXPERTS):
        base = jnp.where(idx == e, start_ref[e], base)
    dest_ref[...] = base + pos_ref[...]


def _dest_rows(start, idx, pos):
    t = idx.shape[1]
    return pl.pallas_call(
        _dest_kernel,
        grid_spec=pltpu.PrefetchScalarGridSpec(
            num_scalar_prefetch=1,
            grid=(t // DEST_TM,),
            in_specs=[pl.BlockSpec((TOP_K, DEST_TM), lambda i, s: (0, i)),
                      pl.BlockSpec((TOP_K, DEST_TM), lambda i, s: (0, i))],
            out_specs=pl.BlockSpec((TOP_K, DEST_TM), lambda i, s: (0, i)),
        ),
        out_shape=jax.ShapeDtypeStruct((TOP_K, t), jnp.int32),
        compiler_params=pltpu.CompilerParams(dimension_semantics=("parallel",)),
        name="moe_dest",
    )(start, idx, pos)


SC_WINDOW = 128


def _sc_mesh():
    return plsc.VectorSubcoreMesh(core_axis_name="c", subcore_axis_name="s")


def _sc_scatter_rows(x, dest, row0, n_sorted):
    t = dest.shape[1]
    blk0 = row0 // SC_WINDOW

    @functools.partial(
        pl.kernel,
        out_type=jax.ShapeDtypeStruct((n_sorted, D_QUARTER), x.dtype),
        mesh=_sc_mesh(),
        scratch_types=[],
    )
    def scatter(x_hbm, i_hbm, o_hbm):
        def body(x_vmem, i_vmem):
            pltpu.sync_copy(x_vmem, o_hbm.at[i_vmem.at[0]])

        pltpu.emit_pipeline(
            body,
            grid=(t // SC_WINDOW, TOP_K),
            in_specs=[pl.BlockSpec((SC_WINDOW, D_QUARTER), lambda i, k: (i + blk0, 0)),
                      pl.BlockSpec((1, SC_WINDOW), lambda i, k: (k, i))],
            out_specs=[],
            core_axis_name=("c", "s"),
            dimension_semantics=(pltpu.PARALLEL, pltpu.ARBITRARY),
        )(x_hbm, i_hbm)

    return scatter(x, dest)


def _sc_gather_rows(table, idx):
    m = idx.shape[1]

    @functools.partial(
        pl.kernel,
        out_type=jax.ShapeDtypeStruct((m, D_QUARTER), table.dtype),
        mesh=_sc_mesh(),
        scratch_types=[],
    )
    def gather(t_hbm, i_hbm, o_hbm):
        def body(i_vmem, o_vmem):
            pltpu.sync_copy(t_hbm.at[i_vmem.at[0]], o_vmem)

        pltpu.emit_pipeline(
            body,
            grid=(m // SC_WINDOW,),
            in_specs=[pl.BlockSpec((1, SC_WINDOW), lambda i: (0, i))],
            out_specs=[pl.BlockSpec((SC_WINDOW, D_QUARTER), lambda i: (i, 0))],
            core_axis_name=("c", "s"),
            dimension_semantics=(pltpu.PARALLEL,),
        )(i_hbm, o_hbm)

    return gather(table, idx)


FFN_STEP_TILES = 2


def _ffn_kernel(te_ref, tv_ref, nu_ref, xa_ref, xb_ref, *rest):
    n_slots = FFN_STEP_TILES
    w_f32 = rest[:3 * n_slots]
    ya_ref, yb_ref = rest[3 * n_slots:3 * n_slots + 2]
    w_bf16 = rest[3 * n_slots + 2:]
    step = pl.program_id(0)

    for s in range(n_slots):
        tile = step * n_slots + s
        active = tile < nu_ref[0]
        wg_ref, wu_ref, wd_ref = w_f32[3 * s:3 * s + 3]
        wgb_ref, wub_ref, wdb_ref = w_bf16[3 * s:3 * s + 3]
        tile_rows = s * FFN_TM

        changed = jnp.logical_or(step == 0, te_ref[tile] != te_ref[jnp.maximum(tile - n_slots, 0)])

        @pl.when(jnp.logical_and(active, changed))
        def _():
            wgb_ref[...] = wg_ref[0, 0].astype(BF16)
            wub_ref[...] = wu_ref[0, 0].astype(BF16)
            wdb_ref[...] = wd_ref[0, 0].astype(BF16)

        @pl.when(active)
        def _():
            wg = wgb_ref[...]
            wu = wub_ref[...]
            wd = wdb_ref[...]
            th = FFN_TM // 2
            for r0 in (0, th):
                rows = pl.ds(tile_rows + r0, th)
                live = (lax.broadcasted_iota(jnp.int32, (th, D_QUARTER), 0) + r0) < tv_ref[tile]
                hi_a, lo_a = _unpack_rows(jnp.where(live, xa_ref[rows, :], jnp.uint32(0)))
                hi_b, lo_b = _unpack_rows(jnp.where(live, xb_ref[rows, :], jnp.uint32(0)))
                xb = jnp.concatenate([hi_a.astype(BF16), hi_b.astype(BF16), lo_a.astype(BF16), lo_b.astype(BF16)],
                                     axis=1)
                a = (_silu(_dot(xb, wg)) * _dot(xb, wu)).astype(BF16)
                words = _pack_rows(_dot(a, wd))
                ya_ref[rows, :] = words[:, :D_QUARTER]
                yb_ref[rows, :] = words[:, D_QUARTER:]

        @pl.when(jnp.logical_not(active))
        def _():
            rows = pl.ds(tile_rows, FFN_TM)
            ya_ref[rows, :] = jnp.zeros((FFN_TM, D_QUARTER), jnp.uint32)
            yb_ref[rows, :] = jnp.zeros((FFN_TM, D_QUARTER), jnp.uint32)


def _expert_ffn(xs_a, xs_b, tile_expert, tile_valid, n_used, wg, wu, wd, layer):
    n_slots = FFN_STEP_TILES
    n_tiles = xs_a.shape[0] // FFN_TM
    step_rows = n_slots * FFN_TM

    def row_map(i, te, tv, nu):
        return (jnp.minimum(i, (nu[0] - 1) // n_slots), 0)

    def out_map(i, te, tv, nu):
        return (i, 0)

    def w_map(s):
        return lambda i, te, tv, nu: (layer, te[jnp.minimum(i * n_slots + s, nu[0] - 1)], 0, 0)

    w_specs = []
    for s in range(n_slots):
        w_specs += [pl.BlockSpec((1, 1, D_MODEL, D_EXPERT), w_map(s)),
                    pl.BlockSpec((1, 1, D_MODEL, D_EXPERT), w_map(s)),
                    pl.BlockSpec((1, 1, D_EXPERT, D_MODEL), w_map(s))]
    return pl.pallas_call(
        _ffn_kernel,
        grid_spec=pltpu.PrefetchScalarGridSpec(
            num_scalar_prefetch=3,
            grid=(n_tiles // n_slots,),
            in_specs=[pl.BlockSpec((step_rows, D_QUARTER), row_map),
                      pl.BlockSpec((step_rows, D_QUARTER), row_map)] + w_specs,
            out_specs=[pl.BlockSpec((step_rows, D_QUARTER), out_map), pl.BlockSpec((step_rows, D_QUARTER), out_map)],
            scratch_shapes=[pltpu.VMEM((D_MODEL, D_EXPERT), BF16), pltpu.VMEM((D_MODEL, D_EXPERT), BF16),
                            pltpu.VMEM((D_EXPERT, D_MODEL), BF16)] * n_slots,
        ),
        out_shape=[jax.ShapeDtypeStruct((n_tiles * FFN_TM, D_QUARTER), jnp.uint32),
                   jax.ShapeDtypeStruct((n_tiles * FFN_TM, D_QUARTER), jnp.uint32)],
        compiler_params=pltpu.CompilerParams(dimension_semantics=("arbitrary",)),
        name="moe_expert_ffn",
    )(tile_expert, tile_valid, n_used, xs_a, xs_b, *([wg, wu, wd] * n_slots))


COMBINE_TM = 256


def _combine_kernel(n_prev, tile0, x_ref, mods_ref, wcol_ref, ga_ref, gb_ref, sg_ref, su_ref, sd_ref, g_ref, b_ref,
                    *rest):
    o_refs = rest[n_prev:]
    m = mods_ref[0]
    sf, cf, gf = m[3:4], m[4:5], m[5:6]
    x = x_ref[...]
    hc = (x * (1.0 + cf) + sf).astype(BF16)
    a = _silu(_dot(hc, sg_ref[...])) * _dot(hc, su_ref[...])
    y = _dot(a.astype(BF16), sd_ref[...])
    wcol = wcol_ref[...]
    parts = [y[:, q * D_QUARTER:(q + 1) * D_QUARTER] for q in range(4)]
    for k in range(TOP_K):
        hi_a, lo_a = _unpack_rows(ga_ref[k])
        hi_b, lo_b = _unpack_rows(gb_ref[k])
        wk = wcol[:, k:k + 1]
        parts = [parts[0] + wk * hi_a, parts[1] + wk * hi_b, parts[2] + wk * lo_a, parts[3] + wk * lo_b]
    y = jnp.concatenate(parts, axis=1)
    out = _layer_norm(ALPHA * x + gf * y, g_ref[...], b_ref[...])
    if len(o_refs) == 1:
        o_refs[0][...] = out
    else:
        is_ctx = pl.program_id(0) + tile0 < T_CTX // x_ref.shape[0]

        @pl.when(is_ctx)
        def _():
            o_refs[0][...] = out

        @pl.when(jnp.logical_not(is_ctx))
        def _():
            o_refs[1][...] = out


def _combine(x, mods, wcol, g_a, g_b, sg, su, sd, ln_g, ln_b, row0, split_streams, prev):
    tm = COMBINE_TM
    n = g_a.shape[1] // tm
    tile0 = row0 // tm
    n_ctx = T_CTX // tm
    full = lambda shape: pl.BlockSpec(shape, lambda i: (0,) * len(shape))
    layouts = {
        "all": (lambda i: (tile0 + i, 0), T_ALL),
        "ctx": (lambda i: (jnp.minimum(tile0 + i, n_ctx - 1), 0), T_CTX),
        "lat": (lambda i: (jnp.maximum(tile0 + i - n_ctx, 0), 0), T_LAT),
    }
    if split_streams:
        kinds = (["ctx"] if tile0 < n_ctx else []) + (["lat"] if tile0 + n > n_ctx else [])
    else:
        kinds = ["all"]
    carried = [kd for kd in kinds if kd in prev]
    outs = pl.pallas_call(
        functools.partial(_combine_kernel, len(carried), tile0),
        grid=(n,),
        in_specs=[
            pl.BlockSpec((tm, D_MODEL), lambda i: (tile0 + i, 0)),
            pl.BlockSpec((1, 6, D_MODEL), lambda i: ((tile0 + i) * tm // GROUP_ROWS, 0, 0)),
            pl.BlockSpec((tm, LANES), lambda i: (tile0 + i, 0)),
            pl.BlockSpec((TOP_K, tm, D_QUARTER), lambda i: (0, i, 0)),
            pl.BlockSpec((TOP_K, tm, D_QUARTER), lambda i: (0, i, 0)),
            full((D_MODEL, D_SHARED)), full((D_MODEL, D_SHARED)), full((D_SHARED, D_MODEL)),
            full((1, D_MODEL)), full((1, D_MODEL)),
        ] + [pl.BlockSpec(memory_space=pl.ANY) for _ in carried],
        out_specs=[pl.BlockSpec((tm, D_MODEL), layouts[kd][0]) for kd in kinds],
        out_shape=[jax.ShapeDtypeStruct((layouts[kd][1], D_MODEL), F32) for kd in kinds],
        input_output_aliases={10 + j: kinds.index(kd) for j, kd in enumerate(carried)},
        compiler_params=pltpu.CompilerParams(dimension_semantics=("arbitrary",)),
        name="moe_combine",
    )(x, mods, wcol, g_a, g_b, sg, su, sd, ln_g, ln_b, *[prev[kd] for kd in carried])
    return {**prev, **dict(zip(kinds, outs))}


MLA_TM = 512


def _mla_proj_kernel(x_ref, mods_ref, wdq_ref, wdkv_ref, wkr_ref, qn_ref, kvn_ref, wuq_ref,
                     wukn_ref, wuv_ref, ta_ref, tb_ref, q_ref, k_ref, v_ref, ckv_ref, kr_ref):
    m = mods_ref[0]
    sm, cm = m[0:1], m[1:2]
    h = (x_ref[...] * (1.0 + cm) + sm).astype(BF16)
    cq = _rms_norm(_dot(h, wdq_ref[...]), qn_ref[...])
    ckv = _rms_norm(_dot(h, wdkv_ref[...]), kvn_ref[...])
    kr2 = _dot(h, wkr_ref[...])
    ckv_ref[...] = ckv
    kr_ref[...] = kr2

    ka = ta_ref[0]
    kb = tb_ref[0]
    tm = ka.shape[0]
    ta = jnp.concatenate([jnp.full((tm, QK_NOPE), Q_PRESCALE, F32), ka * Q_PRESCALE], axis=1)
    tb = jnp.concatenate([jnp.zeros((tm, QK_NOPE), F32), kb * Q_PRESCALE], axis=1)
    krr = kr2 * ka + pltpu.roll(kr2, QK_ROPE, 1) * kb

    qpre = _dot(cq.astype(BF16), wuq_ref[...])
    ckv_b = ckv.astype(BF16)
    kn = _dot(ckv_b, wukn_ref[...])
    v_ref[...] = _dot(ckv_b, wuv_ref[...]).astype(BF16)
    for hd in range(N_HEADS):
        qh = qpre[:, hd * HEAD_PAD:(hd + 1) * HEAD_PAD]
        qrot = qh * ta + pltpu.roll(qh, HEAD_PAD - QK_ROPE, 1) * tb
        q_ref[:, hd * HEAD_PAD:(hd + 1) * HEAD_PAD] = qrot.astype(BF16)
        k_ref[:, hd * HEAD_PAD:hd * HEAD_PAD + QK_NOPE] = kn[:, hd * QK_NOPE:(hd + 1) * QK_NOPE].astype(BF16)
        k_ref[:, hd * HEAD_PAD + QK_NOPE:(hd + 1) * HEAD_PAD] = krr.astype(BF16)


def _mla_proj(x, mods, p, rope_a, rope_b):
    t = x.shape[0]
    tm = MLA_TM
    full = lambda shape: pl.BlockSpec(shape, lambda i: (0,) * len(shape))
    rope_spec = pl.BlockSpec(
        (1, tm, 2 * QK_ROPE),
        lambda i: (jnp.minimum(i * tm // GROUP_ROWS, 1), (i * tm % GROUP_ROWS) // tm, 0))
    return pl.pallas_call(
        _mla_proj_kernel,
        grid=(t // tm,),
        in_specs=[
            pl.BlockSpec((tm, D_MODEL), lambda i: (i, 0)),
            pl.BlockSpec((1, 6, D_MODEL), lambda i: (i * tm // GROUP_ROWS, 0, 0)),
            full((D_MODEL, Q_LORA)), full((D_MODEL, KV_LORA)), full((D_MODEL, 2 * QK_ROPE)),
            full((1, Q_LORA)), full((1, KV_LORA)),
            full((Q_LORA, N_HEADS * HEAD_PAD)),
            full((KV_LORA, N_HEADS * QK_NOPE)), full((KV_LORA, N_HEADS * V_DIM)),
            rope_spec, rope_spec,
        ],
        out_specs=[
            pl.BlockSpec((tm, N_HEADS * HEAD_PAD), lambda i: (i, 0)),
            pl.BlockSpec((tm, N_HEADS * HEAD_PAD), lambda i: (i, 0)),
            pl.BlockSpec((tm, N_HEADS * V_DIM), lambda i: (i, 0)),
            pl.BlockSpec((tm, KV_LORA), lambda i: (i, 0)),
            pl.BlockSpec((tm, 2 * QK_ROPE), lambda i: (i, 0)),
        ],
        out_shape=[
            jax.ShapeDtypeStruct((t, N_HEADS * HEAD_PAD), BF16),
            jax.ShapeDtypeStruct((t, N_HEADS * HEAD_PAD), BF16),
            jax.ShapeDtypeStruct((t, N_HEADS * V_DIM), BF16),
            jax.ShapeDtypeStruct((t, KV_LORA), F32),
            jax.ShapeDtypeStruct((t, 2 * QK_ROPE), F32),
        ],
        compiler_params=pltpu.CompilerParams(dimension_semantics=("parallel",)),
        name="mla_proj",
    )(x, mods, p["w_dq"], p["w_dkv"], p["w_kr"], p["q_norm"], p["kv_norm"], p["w_uq"],
      p["w_ukn"], p["w_uv"], rope_a, rope_b)


def _cache_kv_kernel(ckv_ref, kr_ref, wukn_ref, wuv_ref, k_ref, v_ref):
    ckv_b = ckv_ref[...].astype(BF16)
    kn = _dot(ckv_b, wukn_ref[...])
    v_ref[...] = _dot(ckv_b, wuv_ref[...]).astype(BF16)
    kr = kr_ref[...].astype(BF16)
    for hd in range(N_HEADS):
        k_ref[:, hd * HEAD_PAD:hd * HEAD_PAD + QK_NOPE] = kn[:, hd * QK_NOPE:(hd + 1) * QK_NOPE].astype(BF16)
        k_ref[:, hd * HEAD_PAD + QK_NOPE:(hd + 1) * HEAD_PAD] = kr


def _cache_kv(ckv, kr_pad, p):
    t = ckv.shape[0]
    tm = PAST_LEN
    full = lambda shape: pl.BlockSpec(shape, lambda i: (0,) * len(shape))
    return pl.pallas_call(
        _cache_kv_kernel,
        grid=(t // tm,),
        in_specs=[
            pl.BlockSpec((tm, KV_LORA), lambda i: (i, 0)),
            pl.BlockSpec((tm, 2 * QK_ROPE), lambda i: (i, 0)),
            full((KV_LORA, N_HEADS * QK_NOPE)), full((KV_LORA, N_HEADS * V_DIM)),
        ],
        out_specs=[
            pl.BlockSpec((tm, N_HEADS * HEAD_PAD), lambda i: (i, 0)),
            pl.BlockSpec((tm, N_HEADS * V_DIM), lambda i: (i, 0)),
        ],
        out_shape=[
            jax.ShapeDtypeStruct((t, N_HEADS * HEAD_PAD), BF16),
            jax.ShapeDtypeStruct((t, N_HEADS * V_DIM), BF16),
        ],
        compiler_params=pltpu.CompilerParams(dimension_semantics=("parallel",)),
        name="mla_cache_kv",
    )(ckv, kr_pad, p["w_ukn"], p["w_uv"])


def _ctx_attn_kernel(q_ref, k_ref, v_ref, o_ref):
    for hd in range(N_HEADS):
        q = q_ref[:, hd * HEAD_PAD:(hd + 1) * HEAD_PAD]
        k = k_ref[:, hd * HEAD_PAD:(hd + 1) * HEAD_PAD]
        s = _dot_nt(q, k)
        s = s - jnp.max(s, axis=-1, keepdims=True)
        p = jnp.exp2(s)
        p = p / jnp.sum(p, axis=-1, keepdims=True)
        o = _dot(p.astype(BF16), v_ref[:, hd * V_DIM:(hd + 1) * V_DIM])
        o_ref[:, hd * V_DIM:(hd + 1) * V_DIM] = o.astype(BF16)


def _ctx_attention(q, k, v):
    return pl.pallas_call(
        _ctx_attn_kernel,
        grid=(BATCH,),
        in_specs=[
            pl.BlockSpec((SEQ, N_HEADS * HEAD_PAD), lambda b: (b, 0)),
            pl.BlockSpec((SEQ, N_HEADS * HEAD_PAD), lambda b: (b, 0)),
            pl.BlockSpec((SEQ, N_HEADS * V_DIM), lambda b: (b, 0)),
        ],
        out_specs=pl.BlockSpec((SEQ, N_HEADS * V_DIM), lambda b: (b, 0)),
        out_shape=jax.ShapeDtypeStruct((T_CTX, N_HEADS * V_DIM), BF16),
        compiler_params=pltpu.CompilerParams(dimension_semantics=("parallel",)),
        name="ctx_attention",
    )(q, k, v)


LAT_TQ = 1024
LAT_TK = 512
LAT_PIECES = 4


def _lat_attn_kernel(q_ref, k_ref, v_ref, kc_ref, vc_ref, o_ref, s_ref, p_ref, m_ref):
    n_chunks = DEC_SEQ // LAT_TK + 1
    tp = q_ref.shape[0] // LAT_PIECES
    pieces = [pl.ds(j * tp, tp) for j in range(LAT_PIECES)]
    groups = [pieces[:2], pieces[2:]]
    state = {}

    def keys(c):
        return kc_ref[...] if c == n_chunks - 1 else k_ref[c * LAT_TK:(c + 1) * LAT_TK, :]

    def values(c):
        return vc_ref[...] if c == n_chunks - 1 else v_ref[c * LAT_TK:(c + 1) * LAT_TK, :]

    def qk(r, c):
        s = _dot_nt(q_ref[r, :], keys(c))
        s_ref[r, c * LAT_TK:(c + 1) * LAT_TK] = s
        mp = state.get(("m", r.start), jnp.full((tp, LANES), NEG_INF, F32))
        for j in range(LAT_TK // LANES):
            mp = jnp.maximum(mp, s[:, j * LANES:(j + 1) * LANES])
        state[("m", r.start)] = mp

    def row_max(r):
        m_ref[r, :] = jnp.broadcast_to(jnp.max(state[("m", r.start)], axis=-1, keepdims=True), (tp, LANES))

    def exp_chunk(r, c, after=None):
        lp = state.get(("l", r.start), jnp.zeros((tp, LANES), F32))
        m = m_ref[r, :]
        if after is not None:
            bits = lax.bitcast_convert_type(after, jnp.uint32)
            zero = lax.shift_right_logical(lax.shift_right_logical(bits, jnp.uint32(16)), jnp.uint32(16))
            m = m + lax.bitcast_convert_type(zero, F32)
        for j in range(c * LAT_TK // LANES, (c + 1) * LAT_TK // LANES):
            p = jnp.exp2(s_ref[r, j * LANES:(j + 1) * LANES] - m)
            lp = lp + p
            p_ref[r, j * LANES:(j + 1) * LANES] = p.astype(BF16)
        state[("l", r.start)] = lp

    def pv(r, c):
        acc = state.get(("a", r.start), jnp.zeros((tp, V_DIM), F32))
        state[("a", r.start)] = acc + _dot(p_ref[r, c * LAT_TK:(c + 1) * LAT_TK], values(c))

    def finish(r):
        o_ref[r, :] = (state[("a", r.start)] / jnp.sum(state[("l", r.start)], axis=-1, keepdims=True)).astype(BF16)

    for c in range(n_chunks):
        for r in groups[0]:
            qk(r, c)
    for r in groups[0]:
        row_max(r)
    for c in range(n_chunks):
        for r in groups[1]:
            qk(r, c)
        for r in groups[0]:
            exp_chunk(r, c)
    for r in groups[1]:
        row_max(r)
    for c in range(n_chunks):
        for r in groups[0]:
            pv(r, c)
        for r0, r in zip(groups[0], groups[1]):
            exp_chunk(r, c, after=state[("a", r0.start)])
    for r in groups[0]:
        finish(r)
    for c in range(n_chunks):
        for r in groups[1]:
            pv(r, c)
    for r in groups[1]:
        finish(r)


def _lat_attention(q, k, v, kc, vc):
    nq = DEC_SEQ // LAT_TQ
    return pl.pallas_call(
        _lat_attn_kernel,
        grid=(DEC_BATCH, N_HEADS, nq),
        in_specs=[
            pl.BlockSpec((LAT_TQ, HEAD_PAD), lambda b, h, i: ((b + 1) * nq + i, h)),
            pl.BlockSpec((DEC_SEQ, HEAD_PAD), lambda b, h, i: (b + 1, h)),
            pl.BlockSpec((DEC_SEQ, V_DIM), lambda b, h, i: (b + 1, h)),
            pl.BlockSpec((PAST_LEN, HEAD_PAD), lambda b, h, i: (b, h)),
            pl.BlockSpec((PAST_LEN, V_DIM), lambda b, h, i: (b, h)),
        ],
        out_specs=pl.BlockSpec((LAT_TQ, V_DIM), lambda b, h, i: (b * nq + i, h)),
        out_shape=jax.ShapeDtypeStruct((T_LAT, N_HEADS * V_DIM), BF16),
        scratch_shapes=[pltpu.VMEM((LAT_TQ, DEC_SEQ + PAST_LEN), F32),
                        pltpu.VMEM((LAT_TQ, DEC_SEQ + PAST_LEN), BF16),
                        pltpu.VMEM((LAT_TQ, LANES), F32)],
        compiler_params=pltpu.CompilerParams(
            dimension_semantics=("parallel", "parallel", "parallel")),
        name="lat_attention",
    )(q, k, v, kc, vc)


OPROJ_TM = 512


def _oproj_kernel(oc_ref, ol_ref, x_ref, mods_ref, wo_ref, g_ref, b_ref, out_ref):
    is_ctx = pl.program_id(0) < T_CTX // OPROJ_TM
    o = jnp.where(is_ctx, oc_ref[...], ol_ref[...])
    gm = mods_ref[0][2:3]
    y = _dot(o, wo_ref[...])
    out_ref[...] = _layer_norm(ALPHA * x_ref[...] + gm * y, g_ref[...], b_ref[...])


def _oproj(o_ctx, o_lat, x, mods, w_o, ln_g, ln_b):
    t = x.shape[0]
    tm = OPROJ_TM
    n_ctx = T_CTX // tm
    return pl.pallas_call(
        _oproj_kernel,
        grid=(t // tm,),
        in_specs=[
            pl.BlockSpec((tm, N_HEADS * V_DIM), lambda i: (jnp.minimum(i, n_ctx - 1), 0)),
            pl.BlockSpec((tm, N_HEADS * V_DIM), lambda i: (jnp.maximum(i - n_ctx, 0), 0)),
            pl.BlockSpec((tm, D_MODEL), lambda i: (i, 0)),
            pl.BlockSpec((1, 6, D_MODEL), lambda i: (i * tm // GROUP_ROWS, 0, 0)),
            pl.BlockSpec((N_HEADS * V_DIM, D_MODEL), lambda i: (0, 0)),
            pl.BlockSpec((1, D_MODEL), lambda i: (0, 0)),
            pl.BlockSpec((1, D_MODEL), lambda i: (0, 0)),
        ],
        out_specs=pl.BlockSpec((tm, D_MODEL), lambda i: (i, 0)),
        out_shape=jax.ShapeDtypeStruct((t, D_MODEL), F32),
        compiler_params=pltpu.CompilerParams(dimension_semantics=("parallel",)),
        name="attn_oproj",
    )(o_ctx, o_lat, x, mods, w_o, ln_g, ln_b)


def _swap16(w):
    q = QK_ROPE // 4
    return jnp.concatenate([w[..., q:2 * q], w[..., :q], w[..., 3 * q:], w[..., 2 * q:3 * q]], axis=-1)


def _mla_params(w_dqkv, q_norm, w_uq, kv_norm, w_ukv):
    w_kr = w_dqkv[:, Q_LORA + KV_LORA:]
    wq = w_uq.reshape(Q_LORA, N_HEADS, QK_NOPE + QK_ROPE)
    wq_r = wq[..., QK_NOPE:]
    wq = jnp.concatenate([wq[..., :QK_NOPE], wq_r, _swap16(wq_r)], axis=-1)
    wkv = w_ukv.reshape(KV_LORA, N_HEADS, QK_NOPE + V_DIM)
    return {
        "w_dq": w_dqkv[:, :Q_LORA].astype(BF16),
        "w_dkv": w_dqkv[:, Q_LORA:Q_LORA + KV_LORA].astype(BF16),
        "w_kr": jnp.concatenate([w_kr, _swap16(w_kr)], axis=-1).astype(BF16),
        "q_norm": q_norm.reshape(1, Q_LORA),
        "kv_norm": kv_norm.reshape(1, KV_LORA),
        "w_uq": wq.reshape(Q_LORA, N_HEADS * HEAD_PAD).astype(BF16),
        "w_ukn": wkv[..., :QK_NOPE].reshape(KV_LORA, N_HEADS * QK_NOPE).astype(BF16),
        "w_uv": wkv[..., QK_NOPE:].reshape(KV_LORA, N_HEADS * V_DIM).astype(BF16),
    }


def _rope_tables():
    nf = QK_ROPE // 4
    t = np.arange(DEC_SEQ)
    row = (t // GRID_W).astype(np.float32)
    col = (t % GRID_W).astype(np.float32)
    inv = (ROPE_THETA ** (-np.arange(nf, dtype=np.float32) / nf)).astype(np.float32)
    ar, ac = row[:, None] * inv, col[:, None] * inv
    pad = np.zeros((DEC_SEQ, QK_ROPE), np.float32)
    cos = np.concatenate([np.cos(ar), np.cos(ar), np.cos(ac), np.cos(ac), pad], axis=-1)
    sin = np.concatenate([-np.sin(ar), np.sin(ar), -np.sin(ac), np.sin(ac), pad], axis=-1)
    cos_id = np.concatenate([np.ones((DEC_SEQ, QK_ROPE), np.float32), pad], axis=-1)
    sin_id = np.zeros((DEC_SEQ, 2 * QK_ROPE), np.float32)
    return (jnp.asarray(np.stack([cos_id, cos]).astype(np.float32)),
            jnp.asarray(np.stack([sin_id, sin]).astype(np.float32)))


def kernel(x_prompt, x_sample, cache_ckv, cache_krope, c, c_ctx, ada_w, ada_b, ln_g, ln_b, conv_w_in, conv_k, conv_w_out, mla_w_dqkv, mla_q_norm, mla_w_uq, mla_kv_norm, mla_w_ukv, mla_w_o, router_w, router_bias, exp_w_gate, exp_w_up, exp_w_down, sh_w_gate, sh_w_up, sh_w_down):
    cvecs =jnp.concatenate([c_ctx[None, :], c, jnp.zeros((SUBLANES - N_GROUPS_ROWS, D_MODEL), F32)], axis=0)
    mods = _adaln(cvecs, ada_w, ada_b)

    def ln(l, k):
        return ln_g[l, k].reshape(1, D_MODEL), ln_b[l, k].reshape(1, D_MODEL)

    def moe_layer(xin, l, split_streams):
        idx, pos, wcol, counts, hp_a, hp_b = _router(xin, mods[l], router_w[l].T,
                                                     router_bias[l].reshape(N_EXPERTS, 1))
        shared = (sh_w_gate[l].astype(BF16), sh_w_up[l].astype(BF16), sh_w_down[l].astype(BF16))
        outs = {}
        for ch in sorted(range(len(MOE_CHUNKS)), key=lambda j: -MOE_CHUNKS[j][1]):
            row0, rows = MOE_CHUNKS[ch]
            n_tiles = _n_tiles(rows)
            cnt = counts[ch, :, 0].astype(jnp.int32)
            start, tile_expert, tile_valid, n_used = _expert_layout(cnt, n_tiles)
            dest = _dest_rows(start, idx[:, row0:row0 + rows], pos[:, row0:row0 + rows])
            xs_a = _sc_scatter_rows(hp_a, dest, row0, n_tiles * FFN_TM)
            xs_b = _sc_scatter_rows(hp_b, dest, row0, n_tiles * FFN_TM)
            ys_a, ys_b = _expert_ffn(xs_a, xs_b, tile_expert, tile_valid, n_used,
                                     exp_w_gate, exp_w_up, exp_w_down, l)
            piece = rows // COMBINE_SPLITS
            for s in range(COMBINE_SPLITS):
                dest_row = dest[:, s * piece:(s + 1) * piece].reshape(1, piece * TOP_K)
                g_a = _sc_gather_rows(ys_a, dest_row).reshape(TOP_K, piece, D_QUARTER)
                g_b = _sc_gather_rows(ys_b, dest_row).reshape(TOP_K, piece, D_QUARTER)
                outs = _combine(xin, mods[l], wcol, g_a, g_b, *shared, *ln(l, 1), row0=row0 + s * piece,
                                split_streams=split_streams, prev=outs)
        return (outs["ctx"], outs["lat"]) if split_streams else outs["all"]

    x = _conv_mixer(x_prompt.reshape(T_CTX, D_MODEL), x_sample.reshape(T_LAT, D_MODEL), mods[0],
                    conv_w_in[0].astype(BF16), conv_k[0], conv_w_out[0].astype(BF16), *ln(0, 0))
    x = moe_layer(x, 0, split_streams=False)

    p = _mla_params(mla_w_dqkv[0], mla_q_norm[0], mla_w_uq[0], mla_kv_norm[0], mla_w_ukv[0])
    rope_a, rope_b = _rope_tables()
    q, k, v, ckv, kr = _mla_proj(x, mods[1], p, rope_a, rope_b)
    kr_cache = jnp.concatenate([cache_krope[:, 0].reshape(DEC_BATCH * PAST_LEN, QK_ROPE),
                                jnp.zeros((DEC_BATCH * PAST_LEN, QK_ROPE), F32)], axis=-1)
    kc, vc = _cache_kv(cache_ckv[:, 0].reshape(DEC_BATCH * PAST_LEN, KV_LORA), kr_cache, p)
    o_ctx = _ctx_attention(q, k, v)
    o_lat = _lat_attention(q, k, v, kc, vc)
    x = _oproj(o_ctx, o_lat, x, mods[1], mla_w_o[0].astype(BF16), *ln(1, 0))
    y_ctx, y_lat = moe_layer(x, 1, split_streams=True)

    y_prompt = y_ctx.reshape(BATCH, SEQ, D_MODEL)
    y_sample = y_lat.reshape(DEC_BATCH, DEC_SEQ, D_MODEL)
    state_ckv = ckv[:T_CTX].reshape(BATCH, 1, SEQ, KV_LORA)
    state_krope = kr[:T_CTX, :QK_ROPE].reshape(BATCH, 1, SEQ, QK_ROPE)
    return (y_prompt, y_sample, state_ckv, state_krope)
```

```python
import functools
import math

import jax
import jax.numpy as jnp
import numpy as np
from jax import lax
from jax.experimental import pallas as pl
from jax.experimental.pallas import tpu as pltpu
from jax.experimental.pallas import tpu_sc as plsc

D_MODEL = 1024
BATCH = 16
SEQ = 256
DEPTH = 2
DEC_BATCH = 4
DEC_SEQ = 4096
PAST_LEN = 512
GRID_W = 64

N_HEADS = 8
QK_NOPE = 128
QK_ROPE = 64
V_DIM = 128
Q_LORA = 384
KV_LORA = 256
ROPE_THETA = 10000.0
ATTN_SCALE = (QK_NOPE + QK_ROPE) ** -0.5
HEAD_PAD = 256
Q_PRESCALE = ATTN_SCALE * math.log2(math.e)

N_EXPERTS = 64
TOP_K = 8
N_GROUPS = 8
TOPK_GROUPS = 4
GROUP_SIZE = N_EXPERTS // N_GROUPS
D_EXPERT = 256
D_SHARED = 256
ROUTED_SCALE = 2.5

ALPHA = (2 * DEPTH) ** 0.25
LN_EPS = 1e-5
RMS_EPS = 1e-6

GROUP_ROWS = 4096
N_GROUPS_ROWS = 1 + DEC_BATCH
T_CTX = BATCH * SEQ
T_LAT = DEC_BATCH * DEC_SEQ
T_ALL = T_CTX + T_LAT
LANES = 128
SUBLANES = 8

F32 = jnp.float32
BF16 = jnp.bfloat16
NEG_INF = float("-inf")


def _dot(a, b):
    return jnp.dot(a, b, preferred_element_type=F32)


def _dot_nt(a, b, precision=None):
    return lax.dot_general(a, b, (((1,), (1,)), ((), ())), precision=precision,
                           preferred_element_type=F32)


def _layer_norm(v, g, b):
    mu = jnp.mean(v, axis=-1, keepdims=True)
    d = v - mu
    var = jnp.mean(d * d, axis=-1, keepdims=True)
    return d * lax.rsqrt(var + LN_EPS) * g + b


def _rms_norm(v, g):
    return v * lax.rsqrt(jnp.mean(v * v, axis=-1, keepdims=True) + RMS_EPS) * g


def _silu(v):
    return v / (1.0 + jnp.exp(-v))


def _sigmoid(v):
    return 1.0 / (1.0 + jnp.exp(-v))


def _adaln_kernel(c_ref, w_ref, b_ref, o_ref):
    c = c_ref[...]
    s = _silu(c)
    o_ref[0, 0] = jnp.dot(s, w_ref[0], precision=lax.Precision.HIGHEST,
                          preferred_element_type=F32) + b_ref[0]


def _adaln(cvecs, ada_w, ada_b):
    out = pl.pallas_call(
        _adaln_kernel,
        grid=(DEPTH, 6),
        in_specs=[
            pl.BlockSpec((SUBLANES, D_MODEL), lambda l, j: (0, 0)),
            pl.BlockSpec((1, D_MODEL, D_MODEL), lambda l, j: (l, 0, j)),
            pl.BlockSpec((1, 1, D_MODEL), lambda l, j: (l, 0, j)),
        ],
        out_specs=pl.BlockSpec((1, 1, SUBLANES, D_MODEL), lambda l, j: (l, j, 0, 0)),
        out_shape=jax.ShapeDtypeStruct((DEPTH, 6, SUBLANES, D_MODEL), F32),
        compiler_params=pltpu.CompilerParams(dimension_semantics=("parallel", "parallel")),
        name="adaln",
    )(cvecs, ada_w, ada_b.reshape(DEPTH, 1, 6 * D_MODEL))
    return jnp.transpose(out[:, :, :N_GROUPS_ROWS, :], (0, 2, 1, 3))


CONV_TM = 1024


def _conv_kernel(xc_ref, xcp_ref, xcn_ref, xl_ref, xlp_ref, xln_ref, mods_ref, win_ref, ck_ref, wout_ref,
                 g_ref, b_ref, o_ref):
    i = pl.program_id(0)
    tm = xc_ref.shape[0]
    is_ctx = i < T_CTX // tm
    m = mods_ref[0]
    sm, cm, gm = m[0:1], m[1:2], m[2:3]
    x = jnp.where(is_ctx, xc_ref[...], xl_ref[...])
    xp = jnp.where(is_ctx, xcp_ref[...], xlp_ref[...])
    xn = jnp.where(is_ctx, xcn_ref[...], xln_ref[...])
    def halo_u(xh):
        hh = (xh * (1.0 + cm) + sm).astype(BF16)
        zh = _dot(hh, win_ref[:, D_MODEL:])
        return zh[:, :D_MODEL] * zh[:, D_MODEL:]

    th = tm // 2
    halves = [x[:th], x[th:]]
    zs = [_dot((xh * (1.0 + cm) + sm).astype(BF16), win_ref[...]) for xh in halves]
    us = [z[:, D_MODEL:2 * D_MODEL] * z[:, 2 * D_MODEL:] for z in zs]
    befores = [halo_u(xp)[SUBLANES - 1:SUBLANES], us[0][th - 1:th]]
    afters = [us[1][0:1], halo_u(xn)[0:1]]

    ck = ck_ref[...]
    row = lax.broadcasted_iota(jnp.int32, (th, 1), 0)
    for hf in range(2):
        grow = i * tm + hf * th + row
        seq_len = jnp.where(grow < T_CTX, SEQ, DEC_SEQ)
        pos = jnp.bitwise_and(grow, seq_len - 1)
        u = us[hf]
        left = jnp.where(row == 0, befores[hf], pltpu.roll(u, 1, 0))
        left = jnp.where(pos == 0, 0.0, left)
        right = jnp.where(row == th - 1, afters[hf], pltpu.roll(u, th - 1, 0))
        right = jnp.where(pos == seq_len - 1, 0.0, right)
        conv = left * ck[0:1] + u * ck[1:2] + right * ck[2:3]
        v = (zs[hf][:, :D_MODEL] * conv).astype(BF16)
        y = _dot(v, wout_ref[...])
        o_ref[hf * th:(hf + 1) * th, :] = _layer_norm(ALPHA * halves[hf] + gm * y, g_ref[...], b_ref[...])


def _conv_mixer(x_ctx, x_lat, mods, w_in, conv_k, w_out, ln_g, ln_b):
    tm = CONV_TM
    per8 = tm // SUBLANES
    n_ctx = T_CTX // tm
    n_lat = T_LAT // tm

    def stream_specs(first, n):
        def blk(i):
            return jnp.clip(i - first, 0, n - 1)
        return [
            pl.BlockSpec((tm, D_MODEL), lambda i: (blk(i), 0)),
            pl.BlockSpec((SUBLANES, D_MODEL), lambda i: (jnp.maximum(blk(i) * per8 - 1, 0), 0)),
            pl.BlockSpec((SUBLANES, D_MODEL), lambda i: (jnp.minimum((blk(i) + 1) * per8, n * per8 - 1), 0)),
        ]

    return pl.pallas_call(
        _conv_kernel,
        grid=(n_ctx + n_lat,),
        in_specs=stream_specs(0, n_ctx) + stream_specs(n_ctx, n_lat) + [
            pl.BlockSpec((1, 6, D_MODEL), lambda i: (i * tm // GROUP_ROWS, 0, 0)),
            pl.BlockSpec((D_MODEL, 3 * D_MODEL), lambda i: (0, 0)),
            pl.BlockSpec((3, D_MODEL), lambda i: (0, 0)),
            pl.BlockSpec((D_MODEL, D_MODEL), lambda i: (0, 0)),
            pl.BlockSpec((1, D_MODEL), lambda i: (0, 0)),
            pl.BlockSpec((1, D_MODEL), lambda i: (0, 0)),
        ],
        out_specs=pl.BlockSpec((tm, D_MODEL), lambda i: (i, 0)),
        out_shape=jax.ShapeDtypeStruct((T_ALL, D_MODEL), F32),
        compiler_params=pltpu.CompilerParams(dimension_semantics=("parallel",)),
        name="conv_mixer",
    )(x_ctx, x_ctx, x_ctx, x_lat, x_lat, x_lat, mods, w_in, conv_k, w_out, ln_g, ln_b)


ROUTER_TM = 512
MOE_CHUNKS = ((0, T_ALL),)
COMBINE_SPLITS = 4


def _first_argmax_mask(cur, ridx, n):
    mx = jnp.max(cur, axis=0, keepdims=True)
    first = jnp.min(jnp.where(cur == mx, ridx, n), axis=0, keepdims=True)
    return ridx == first, mx


def _router_kernel(x_ref, mods_ref, rwt_ref, bias_ref, idx_ref, pos_ref, wcol_ref, count_ref, hpa_ref, hpb_ref,
                   carry_ref):
    tm = x_ref.shape[0]
    m = mods_ref[0]
    sf, cf = m[3:4], m[4:5]
    hc = x_ref[...] * (1.0 + cf) + sf
    words = _pack_rows(hc)
    hpa_ref[...] = words[:, :D_QUARTER]
    hpb_ref[...] = words[:, D_QUARTER:]
    logits = _dot_nt(rwt_ref[...], hc, precision=lax.Precision.HIGHEST)
    scores = _sigmoid(logits)
    biased = scores + bias_ref[...]

    ridx8 = lax.broadcasted_iota(jnp.int32, (GROUP_SIZE, tm), 0)
    gscore = jnp.full((N_GROUPS, tm), NEG_INF, F32)
    for g in range(N_GROUPS):
        blk = biased[g * GROUP_SIZE:(g + 1) * GROUP_SIZE]
        sel, m1 = _first_argmax_mask(blk, ridx8, GROUP_SIZE)
        m2 = jnp.max(jnp.where(sel, NEG_INF, blk), axis=0, keepdims=True)
        gscore = jnp.where(ridx8 == g, m1 + m2, gscore)

    gmask = jnp.zeros((N_GROUPS, tm), jnp.bool_)
    cur = gscore
    for _ in range(TOPK_GROUPS):
        sel, _unused = _first_argmax_mask(cur, ridx8, N_GROUPS)
        gmask = jnp.logical_or(gmask, sel)
        cur = jnp.where(sel, NEG_INF, cur)

    gmask_f = gmask.astype(F32)
    blocks = []
    for g in range(N_GROUPS):
        keep = jnp.broadcast_to(gmask_f[g:g + 1], (GROUP_SIZE, tm)) > 0.5
        blocks.append(jnp.where(keep, biased[g * GROUP_SIZE:(g + 1) * GROUP_SIZE], NEG_INF))
    cur = jnp.concatenate(blocks, axis=0)

    first_tiles = [row0 // tm for row0, _rows in MOE_CHUNKS]
    starts_chunk = functools.reduce(jnp.logical_or, [pl.program_id(0) == ft for ft in first_tiles])

    @pl.when(starts_chunk)
    def _():
        carry_ref[...] = jnp.zeros(carry_ref.shape, F32)

    ridx = lax.broadcasted_iota(jnp.int32, (N_EXPERTS, tm), 0)
    kidx = lax.broadcasted_iota(jnp.int32, (TOP_K, tm), 0)
    sels = []
    chosen = jnp.zeros((N_EXPERTS, tm), jnp.bool_)
    idx_rows = jnp.zeros((TOP_K, tm), jnp.int32)
    for k in range(TOP_K):
        mx = jnp.max(cur, axis=0, keepdims=True)
        first = jnp.min(jnp.where(cur == mx, ridx, N_EXPERTS), axis=0, keepdims=True)
        sel = ridx == first
        sels.append(sel)
        chosen = jnp.logical_or(chosen, sel)
        idx_rows = jnp.where(kidx == k, first, idx_rows)
        cur = jnp.where(sel, NEG_INF, cur)

    onehot = chosen.astype(F32)
    t_row = lax.broadcasted_iota(jnp.int32, (tm, tm), 0)
    t_col = lax.broadcasted_iota(jnp.int32, (tm, tm), 1)
    before = (t_row < t_col).astype(BF16)
    rank = carry_ref[...] + _dot(onehot.astype(BF16), before)
    carry_ref[...] = carry_ref[...] + jnp.sum(onehot, axis=1, keepdims=True)
    count_ref[0] = jnp.broadcast_to(carry_ref[...], count_ref.shape[1:])

    w = jnp.where(chosen, scores, 0.0)
    w = w / jnp.sum(w, axis=0, keepdims=True) * ROUTED_SCALE
    pos_rows = jnp.zeros((TOP_K, tm), F32)
    w_rows = jnp.zeros((TOP_K, tm), F32)
    for k in range(TOP_K):
        pos_rows = jnp.where(kidx == k, jnp.sum(jnp.where(sels[k], rank, 0.0), axis=0, keepdims=True), pos_rows)
        w_rows = jnp.where(kidx == k, jnp.sum(jnp.where(sels[k], w, 0.0), axis=0, keepdims=True), w_rows)
    idx_ref[...] = idx_rows
    pos_ref[...] = pos_rows.astype(jnp.int32)
    wpad = jnp.concatenate([w_rows, jnp.zeros((LANES - TOP_K, tm), F32)], axis=0)
    wcol_ref[...] = wpad.T


def _router(x, mods, router_wt, router_bias):
    t = x.shape[0]
    tm = ROUTER_TM

    def chunk_of(i):
        return sum((i >= row0 // tm).astype(jnp.int32) for row0, _rows in MOE_CHUNKS[1:])

    return pl.pallas_call(
        _router_kernel,
        grid=(t // tm,),
        in_specs=[
            pl.BlockSpec((tm, D_MODEL), lambda i: (i, 0)),
            pl.BlockSpec((1, 6, D_MODEL), lambda i: (i * tm // GROUP_ROWS, 0, 0)),
            pl.BlockSpec((N_EXPERTS, D_MODEL), lambda i: (0, 0)),
            pl.BlockSpec((N_EXPERTS, 1), lambda i: (0, 0)),
        ],
        out_specs=[
            pl.BlockSpec((TOP_K, tm), lambda i: (0, i)),
            pl.BlockSpec((TOP_K, tm), lambda i: (0, i)),
            pl.BlockSpec((tm, LANES), lambda i: (i, 0)),
            pl.BlockSpec((1, N_EXPERTS, LANES), lambda i: (chunk_of(i), 0, 0)),
            pl.BlockSpec((tm, D_QUARTER), lambda i: (i, 0)),
            pl.BlockSpec((tm, D_QUARTER), lambda i: (i, 0)),
        ],
        out_shape=[
            jax.ShapeDtypeStruct((TOP_K, t), jnp.int32),
            jax.ShapeDtypeStruct((TOP_K, t), jnp.int32),
            jax.ShapeDtypeStruct((t, LANES), F32),
            jax.ShapeDtypeStruct((len(MOE_CHUNKS), N_EXPERTS, LANES), F32),
            jax.ShapeDtypeStruct((t, D_QUARTER), jnp.uint32),
            jax.ShapeDtypeStruct((t, D_QUARTER), jnp.uint32),
        ],
        scratch_shapes=[pltpu.VMEM((N_EXPERTS, 1), F32)],
        compiler_params=pltpu.CompilerParams(dimension_semantics=("arbitrary",)),
        name="moe_router",
    )(x, mods, router_wt, router_bias)


FFN_TM = 512


def _n_tiles(rows):
    return rows * TOP_K // FFN_TM + N_EXPERTS


def _expert_layout(counts, n_tiles):
    padded = (counts + FFN_TM - 1) // FFN_TM * FFN_TM
    end = jnp.cumsum(padded)
    start = end - padded
    tile_row = jnp.arange(n_tiles, dtype=jnp.int32) * FFN_TM
    tile_expert = jnp.minimum(jnp.sum(end[None, :] <= tile_row[:, None], axis=1), N_EXPERTS - 1)
    of_tile = tile_expert[:, None] == jnp.arange(N_EXPERTS, dtype=jnp.int32)[None, :]
    live_end = jnp.sum(jnp.where(of_tile, (start + counts)[None, :], 0), axis=1)
    tile_valid = jnp.clip(live_end - tile_row, 0, FFN_TM)
    n_used = (end[-1] // FFN_TM).astype(jnp.int32).reshape(1)
    used = jnp.arange(n_tiles, dtype=jnp.int32) < n_used[0]
    prev_expert = jnp.concatenate([jnp.full((1,), -1, tile_expert.dtype), tile_expert[:-1]])
    run_flag = jnp.logical_and(used, tile_expert != prev_expert)
    run_slot = (jnp.cumsum(run_flag.astype(jnp.int32)) - 1) % 2
    later = jnp.logical_and(used[None, :], tile_expert[None, :] > tile_expert[:, None])
    next_expert = jnp.min(jnp.where(later, tile_expert[None, :], N_EXPERTS), axis=1)
    next_expert = jnp.where(next_expert == N_EXPERTS, -1, next_expert)
    plan = (run_flag.astype(jnp.int32), run_slot.astype(jnp.int32), next_expert.astype(jnp.int32))
    return start.astype(jnp.int32), tile_expert.astype(jnp.int32), tile_valid.astype(jnp.int32), n_used, plan


D_HALF = D_MODEL // 2
D_QUARTER = D_MODEL // 4


def _pack_rows(v):
    hi = lax.bitcast_convert_type(v[:, :D_HALF].astype(BF16).astype(F32), jnp.uint32)
    lo = lax.bitcast_convert_type(v[:, D_HALF:].astype(BF16).astype(F32), jnp.uint32)
    return jnp.bitwise_or(hi, jnp.right_shift(lo, jnp.uint32(16)))


def _unpack_rows(w):
    hi = lax.bitcast_convert_type(jnp.bitwise_and(w, jnp.uint32(0xFFFF0000)), F32)
    lo = lax.bitcast_convert_type(jnp.left_shift(w, jnp.uint32(16)), F32)
    return hi, lo


DEST_TM = 2048


def _dest_kernel(start_ref, idx_ref, pos_ref, dest_ref):
    idx = idx_ref[...]
    base = jnp.zeros(idx.shape, jnp.int32)
    for e in range(N_EXPERTS):
        base = jnp.where(idx == e, start_ref[e], base)
    dest_ref[...] = base + pos_ref[...]


def _dest_rows(start, idx, pos):
    t = idx.shape[1]
    return pl.pallas_call(
        _dest_kernel,
        grid_spec=pltpu.PrefetchScalarGridSpec(
            num_scalar_prefetch=1,
            grid=(t // DEST_TM,),
            in_specs=[pl.BlockSpec((TOP_K, DEST_TM), lambda i, s: (0, i)),
                      pl.BlockSpec((TOP_K, DEST_TM), lambda i, s: (0, i))],
            out_specs=pl.BlockSpec((TOP_K, DEST_TM), lambda i, s: (0, i)),
        ),
        out_shape=jax.ShapeDtypeStruct((TOP_K, t), jnp.int32),
        compiler_params=pltpu.CompilerParams(dimension_semantics=("parallel",)),
        name="moe_dest",
    )(start, idx, pos)


SC_WINDOW = 128


def _sc_mesh():
    return plsc.VectorSubcoreMesh(core_axis_name="c", subcore_axis_name="s")


def _sc_scatter_rows(x, dest, row0, n_sorted):
    t = dest.shape[1]
    blk0 = row0 // SC_WINDOW

    @functools.partial(
        pl.kernel,
        out_type=jax.ShapeDtypeStruct((n_sorted, D_QUARTER), x.dtype),
        mesh=_sc_mesh(),
        scratch_types=[],
    )
    def scatter(x_hbm, i_hbm, o_hbm):
        def body(x_vmem, i_vmem):
            pltpu.sync_copy(x_vmem, o_hbm.at[i_vmem.at[0]])

        pltpu.emit_pipeline(
            body,
            grid=(t // SC_WINDOW, TOP_K),
            in_specs=[pl.BlockSpec((SC_WINDOW, D_QUARTER), lambda i, k: (i + blk0, 0)),
                      pl.BlockSpec((1, SC_WINDOW), lambda i, k: (k, i))],
            out_specs=[],
            core_axis_name=("c", "s"),
            dimension_semantics=(pltpu.PARALLEL, pltpu.ARBITRARY),
        )(x_hbm, i_hbm)

    return scatter(x, dest)


def _sc_gather_rows(table, idx):
    m = idx.shape[1]

    @functools.partial(
        pl.kernel,
        out_type=jax.ShapeDtypeStruct((m, D_QUARTER), table.dtype),
        mesh=_sc_mesh(),
        scratch_types=[],
    )
    def gather(t_hbm, i_hbm, o_hbm):
        def body(i_vmem, o_vmem):
            pltpu.sync_copy(t_hbm.at[i_vmem.at[0]], o_vmem)

        pltpu.emit_pipeline(
            body,
            grid=(m // SC_WINDOW,),
            in_specs=[pl.BlockSpec((1, SC_WINDOW), lambda i: (0, i))],
            out_specs=[pl.BlockSpec((SC_WINDOW, D_QUARTER), lambda i: (i, 0))],
            core_axis_name=("c", "s"),
            dimension_semantics=(pltpu.PARALLEL,),
        )(i_hbm, o_hbm)

    return gather(table, idx)


FFN_STEP_TILES = 2


def _ffn_kernel(layer, te_ref, tv_ref, nu_ref, flag_ref, slot_ref, next_ref, xa_ref, xb_ref,
                wg_hbm, wu_hbm, wd_hbm, ya_ref, yb_ref, wg_st, wu_st, wd_st, wgb_ref, wub_ref, wdb_ref, sem):
    step = pl.program_id(0)

    def fetch(expert, slot):
        return [pltpu.make_async_copy(src.at[layer, expert], dst.at[slot], sem.at[slot])
                for src, dst in ((wg_hbm, wg_st), (wu_hbm, wu_st), (wd_hbm, wd_st))]

    for s in range(FFN_STEP_TILES):
        tile = step * FFN_STEP_TILES + s
        active = tile < nu_ref[0]
        tile_rows = s * FFN_TM

        @pl.when(jnp.logical_and(active, flag_ref[tile] == 1))
        def _():
            slot = slot_ref[tile]

            @pl.when(tile == 0)
            def _():
                for cp in fetch(te_ref[tile], slot):
                    cp.start()

            for cp in fetch(te_ref[tile], slot):
                cp.wait()
            wgb_ref[...] = wg_st[slot].astype(BF16)
            wub_ref[...] = wu_st[slot].astype(BF16)
            wdb_ref[...] = wd_st[slot].astype(BF16)

            @pl.when(next_ref[tile] >= 0)
            def _():
                for cp in fetch(next_ref[tile], 1 - slot):
                    cp.start()

        @pl.when(active)
        def _():
            wg = wgb_ref[...]
            wu = wub_ref[...]
            wd = wdb_ref[...]
            th = FFN_TM // 2
            for r0 in (0, th):
                rows = pl.ds(tile_rows + r0, th)
                live = (lax.broadcasted_iota(jnp.int32, (th, D_QUARTER), 0) + r0) < tv_ref[tile]
                hi_a, lo_a = _unpack_rows(jnp.where(live, xa_ref[rows, :], jnp.uint32(0)))
                hi_b, lo_b = _unpack_rows(jnp.where(live, xb_ref[rows, :], jnp.uint32(0)))
                xb = jnp.concatenate([hi_a.astype(BF16), hi_b.astype(BF16), lo_a.astype(BF16), lo_b.astype(BF16)],
                                     axis=1)
                a = (_silu(_dot(xb, wg)) * _dot(xb, wu)).astype(BF16)
                words = _pack_rows(_dot(a, wd))
                ya_ref[rows, :] = words[:, :D_QUARTER]
                yb_ref[rows, :] = words[:, D_QUARTER:]

        @pl.when(jnp.logical_not(active))
        def _():
            rows = pl.ds(tile_rows, FFN_TM)
            ya_ref[rows, :] = jnp.zeros((FFN_TM, D_QUARTER), jnp.uint32)
            yb_ref[rows, :] = jnp.zeros((FFN_TM, D_QUARTER), jnp.uint32)


def _expert_ffn(xs_a, xs_b, tile_expert, tile_valid, n_used, plan, wg, wu, wd, layer):
    n_tiles = xs_a.shape[0] // FFN_TM
    step_rows = FFN_STEP_TILES * FFN_TM

    def row_map(i, te, tv, nu, fl, sl, nx):
        return (jnp.minimum(i, (nu[0] - 1) // FFN_STEP_TILES), 0)

    def out_map(i, te, tv, nu, fl, sl, nx):
        return (i, 0)

    hbm = pl.BlockSpec(memory_space=pl.ANY)
    return pl.pallas_call(
        functools.partial(_ffn_kernel, layer),
        grid_spec=pltpu.PrefetchScalarGridSpec(
            num_scalar_prefetch=6,
            grid=(n_tiles // FFN_STEP_TILES,),
            in_specs=[pl.BlockSpec((step_rows, D_QUARTER), row_map),
                      pl.BlockSpec((step_rows, D_QUARTER), row_map), hbm, hbm, hbm],
            out_specs=[pl.BlockSpec((step_rows, D_QUARTER), out_map), pl.BlockSpec((step_rows, D_QUARTER), out_map)],
            scratch_shapes=[pltpu.VMEM((2, D_MODEL, D_EXPERT), F32), pltpu.VMEM((2, D_MODEL, D_EXPERT), F32),
                            pltpu.VMEM((2, D_EXPERT, D_MODEL), F32),
                            pltpu.VMEM((D_MODEL, D_EXPERT), BF16), pltpu.VMEM((D_MODEL, D_EXPERT), BF16),
                            pltpu.VMEM((D_EXPERT, D_MODEL), BF16), pltpu.SemaphoreType.DMA((2,))],
        ),
        out_shape=[jax.ShapeDtypeStruct((n_tiles * FFN_TM, D_QUARTER), jnp.uint32),
                   jax.ShapeDtypeStruct((n_tiles * FFN_TM, D_QUARTER), jnp.uint32)],
        compiler_params=pltpu.CompilerParams(dimension_semantics=("arbitrary",)),
        name="moe_expert_ffn",
    )(tile_expert, tile_valid, n_used, *plan, xs_a, xs_b, wg, wu, wd)


COMBINE_TM = 256


def _combine_kernel(n_prev, tile0, x_ref, mods_ref, wcol_ref, ga_ref, gb_ref, sg_ref, su_ref, sd_ref, g_ref, b_ref,
                    *rest):
    o_refs = rest[n_prev:]
    m = mods_ref[0]
    sf, cf, gf = m[3:4], m[4:5], m[5:6]
    x = x_ref[...]
    hc = (x * (1.0 + cf) + sf).astype(BF16)
    a = _silu(_dot(hc, sg_ref[...])) * _dot(hc, su_ref[...])
    y = _dot(a.astype(BF16), sd_ref[...])
    wcol = wcol_ref[...]
    parts = [y[:, q * D_QUARTER:(q + 1) * D_QUARTER] for q in range(4)]
    for k in range(TOP_K):
        hi_a, lo_a = _unpack_rows(ga_ref[k])
        hi_b, lo_b = _unpack_rows(gb_ref[k])
        wk = wcol[:, k:k + 1]
        parts = [parts[0] + wk * hi_a, parts[1] + wk * hi_b, parts[2] + wk * lo_a, parts[3] + wk * lo_b]
    y = jnp.concatenate(parts, axis=1)
    out = _layer_norm(ALPHA * x + gf * y, g_ref[...], b_ref[...])
    if len(o_refs) == 1:
        o_refs[0][...] = out
    else:
        is_ctx = pl.program_id(0) + tile0 < T_CTX // x_ref.shape[0]

        @pl.when(is_ctx)
        def _():
            o_refs[0][...] = out

        @pl.when(jnp.logical_not(is_ctx))
        def _():
            o_refs[1][...] = out


def _combine(x, mods, wcol, g_a, g_b, sg, su, sd, ln_g, ln_b, row0, split_streams, prev):
    tm = COMBINE_TM
    n = g_a.shape[1] // tm
    tile0 = row0 // tm
    n_ctx = T_CTX // tm
    full = lambda shape: pl.BlockSpec(shape, lambda i: (0,) * len(shape))
    layouts = {
        "all": (lambda i: (tile0 + i, 0), T_ALL),
        "ctx": (lambda i: (jnp.minimum(tile0 + i, n_ctx - 1), 0), T_CTX),
        "lat": (lambda i: (jnp.maximum(tile0 + i - n_ctx, 0), 0), T_LAT),
    }
    if split_streams:
        kinds = (["ctx"] if tile0 < n_ctx else []) + (["lat"] if tile0 + n > n_ctx else [])
    else:
        kinds = ["all"]
    carried = [kd for kd in kinds if kd in prev]
    outs = pl.pallas_call(
        functools.partial(_combine_kernel, len(carried), tile0),
        grid=(n,),
        in_specs=[
            pl.BlockSpec((tm, D_MODEL), lambda i: (tile0 + i, 0)),
            pl.BlockSpec((1, 6, D_MODEL), lambda i: ((tile0 + i) * tm // GROUP_ROWS, 0, 0)),
            pl.BlockSpec((tm, LANES), lambda i: (tile0 + i, 0)),
            pl.BlockSpec((TOP_K, tm, D_QUARTER), lambda i: (0, i, 0)),
            pl.BlockSpec((TOP_K, tm, D_QUARTER), lambda i: (0, i, 0)),
            full((D_MODEL, D_SHARED)), full((D_MODEL, D_SHARED)), full((D_SHARED, D_MODEL)),
            full((1, D_MODEL)), full((1, D_MODEL)),
        ] + [pl.BlockSpec(memory_space=pl.ANY) for _ in carried],
        out_specs=[pl.BlockSpec((tm, D_MODEL), layouts[kd][0]) for kd in kinds],
        out_shape=[jax.ShapeDtypeStruct((layouts[kd][1], D_MODEL), F32) for kd in kinds],
        input_output_aliases={10 + j: kinds.index(kd) for j, kd in enumerate(carried)},
        compiler_params=pltpu.CompilerParams(dimension_semantics=("arbitrary",)),
        name="moe_combine",
    )(x, mods, wcol, g_a, g_b, sg, su, sd, ln_g, ln_b, *[prev[kd] for kd in carried])
    return {**prev, **dict(zip(kinds, outs))}


MLA_TM = 512


def _mla_proj_kernel(x_ref, mods_ref, wdq_ref, wdkv_ref, wkr_ref, qn_ref, kvn_ref, wuq_ref,
                     wukn_ref, wuv_ref, ta_ref, tb_ref, q_ref, k_ref, v_ref, ckv_ref, kr_ref):
    m = mods_ref[0]
    sm, cm = m[0:1], m[1:2]
    h = (x_ref[...] * (1.0 + cm) + sm).astype(BF16)
    cq = _rms_norm(_dot(h, wdq_ref[...]), qn_ref[...])
    ckv = _rms_norm(_dot(h, wdkv_ref[...]), kvn_ref[...])
    kr2 = _dot(h, wkr_ref[...])
    ckv_ref[...] = ckv
    kr_ref[...] = kr2

    ka = ta_ref[0]
    kb = tb_ref[0]
    tm = ka.shape[0]
    ta = jnp.concatenate([jnp.full((tm, QK_NOPE), Q_PRESCALE, F32), ka * Q_PRESCALE], axis=1)
    tb = jnp.concatenate([jnp.zeros((tm, QK_NOPE), F32), kb * Q_PRESCALE], axis=1)
    krr = kr2 * ka + pltpu.roll(kr2, QK_ROPE, 1) * kb

    qpre = _dot(cq.astype(BF16), wuq_ref[...])
    ckv_b = ckv.astype(BF16)
    kn = _dot(ckv_b, wukn_ref[...])
    v_ref[...] = _dot(ckv_b, wuv_ref[...]).astype(BF16)
    for hd in range(N_HEADS):
        qh = qpre[:, hd * HEAD_PAD:(hd + 1) * HEAD_PAD]
        qrot = qh * ta + pltpu.roll(qh, HEAD_PAD - QK_ROPE, 1) * tb
        q_ref[:, hd * HEAD_PAD:(hd + 1) * HEAD_PAD] = qrot.astype(BF16)
        k_ref[:, hd * HEAD_PAD:hd * HEAD_PAD + QK_NOPE] = kn[:, hd * QK_NOPE:(hd + 1) * QK_NOPE].astype(BF16)
        k_ref[:, hd * HEAD_PAD + QK_NOPE:(hd + 1) * HEAD_PAD] = krr.astype(BF16)


def _mla_proj(x, mods, p, rope_a, rope_b):
    t = x.shape[0]
    tm = MLA_TM
    full = lambda shape: pl.BlockSpec(shape, lambda i: (0,) * len(shape))
    rope_spec = pl.BlockSpec(
        (1, tm, 2 * QK_ROPE),
        lambda i: (jnp.minimum(i * tm // GROUP_ROWS, 1), (i * tm % GROUP_ROWS) // tm, 0))
    return pl.pallas_call(
        _mla_proj_kernel,
        grid=(t // tm,),
        in_specs=[
            pl.BlockSpec((tm, D_MODEL), lambda i: (i, 0)),
            pl.BlockSpec((1, 6, D_MODEL), lambda i: (i * tm // GROUP_ROWS, 0, 0)),
            full((D_MODEL, Q_LORA)), full((D_MODEL, KV_LORA)), full((D_MODEL, 2 * QK_ROPE)),
            full((1, Q_LORA)), full((1, KV_LORA)),
            full((Q_LORA, N_HEADS * HEAD_PAD)),
            full((KV_LORA, N_HEADS * QK_NOPE)), full((KV_LORA, N_HEADS * V_DIM)),
            rope_spec, rope_spec,
        ],
        out_specs=[
            pl.BlockSpec((tm, N_HEADS * HEAD_PAD), lambda i: (i, 0)),
            pl.BlockSpec((tm, N_HEADS * HEAD_PAD), lambda i: (i, 0)),
            pl.BlockSpec((tm, N_HEADS * V_DIM), lambda i: (i, 0)),
            pl.BlockSpec((tm, KV_LORA), lambda i: (i, 0)),
            pl.BlockSpec((tm, 2 * QK_ROPE), lambda i: (i, 0)),
        ],
        out_shape=[
            jax.ShapeDtypeStruct((t, N_HEADS * HEAD_PAD), BF16),
            jax.ShapeDtypeStruct((t, N_HEADS * HEAD_PAD), BF16),
            jax.ShapeDtypeStruct((t, N_HEADS * V_DIM), BF16),
            jax.ShapeDtypeStruct((t, KV_LORA), F32),
            jax.ShapeDtypeStruct((t, 2 * QK_ROPE), F32),
        ],
        compiler_params=pltpu.CompilerParams(dimension_semantics=("parallel",)),
        name="mla_proj",
    )(x, mods, p["w_dq"], p["w_dkv"], p["w_kr"], p["q_norm"], p["kv_norm"], p["w_uq"],
      p["w_ukn"], p["w_uv"], rope_a, rope_b)


def _cache_kv_kernel(ckv_ref, kr_ref, wukn_ref, wuv_ref, k_ref, v_ref):
    ckv_b = ckv_ref[...].astype(BF16)
    kn = _dot(ckv_b, wukn_ref[...])
    v_ref[...] = _dot(ckv_b, wuv_ref[...]).astype(BF16)
    kr = kr_ref[...].astype(BF16)
    for hd in range(N_HEADS):
        k_ref[:, hd * HEAD_PAD:hd * HEAD_PAD + QK_NOPE] = kn[:, hd * QK_NOPE:(hd + 1) * QK_NOPE].astype(BF16)
        k_ref[:, hd * HEAD_PAD + QK_NOPE:(hd + 1) * HEAD_PAD] = kr


def _cache_kv(ckv, kr_pad, p):
    t = ckv.shape[0]
    tm = PAST_LEN
    full = lambda shape: pl.BlockSpec(shape, lambda i: (0,) * len(shape))
    return pl.pallas_call(
        _cache_kv_kernel,
        grid=(t // tm,),
        in_specs=[
            pl.BlockSpec((tm, KV_LORA), lambda i: (i, 0)),
            pl.BlockSpec((tm, 2 * QK_ROPE), lambda i: (i, 0)),
            full((KV_LORA, N_HEADS * QK_NOPE)), full((KV_LORA, N_HEADS * V_DIM)),
        ],
        out_specs=[
            pl.BlockSpec((tm, N_HEADS * HEAD_PAD), lambda i: (i, 0)),
            pl.BlockSpec((tm, N_HEADS * V_DIM), lambda i: (i, 0)),
        ],
        out_shape=[
            jax.ShapeDtypeStruct((t, N_HEADS * HEAD_PAD), BF16),
            jax.ShapeDtypeStruct((t, N_HEADS * V_DIM), BF16),
        ],
        compiler_params=pltpu.CompilerParams(dimension_semantics=("parallel",)),
        name="mla_cache_kv",
    )(ckv, kr_pad, p["w_ukn"], p["w_uv"])


def _ctx_attn_kernel(q_ref, k_ref, v_ref, o_ref):
    for hd in range(N_HEADS):
        q = q_ref[:, hd * HEAD_PAD:(hd + 1) * HEAD_PAD]
        k = k_ref[:, hd * HEAD_PAD:(hd + 1) * HEAD_PAD]
        s = _dot_nt(q, k)
        s = s - jnp.max(s, axis=-1, keepdims=True)
        p = jnp.exp2(s)
        p = p / jnp.sum(p, axis=-1, keepdims=True)
        o = _dot(p.astype(BF16), v_ref[:, hd * V_DIM:(hd + 1) * V_DIM])
        o_ref[:, hd * V_DIM:(hd + 1) * V_DIM] = o.astype(BF16)


def _ctx_attention(q, k, v):
    return pl.pallas_call(
        _ctx_attn_kernel,
        grid=(BATCH,),
        in_specs=[
            pl.BlockSpec((SEQ, N_HEADS * HEAD_PAD), lambda b: (b, 0)),
            pl.BlockSpec((SEQ, N_HEADS * HEAD_PAD), lambda b: (b, 0)),
            pl.BlockSpec((SEQ, N_HEADS * V_DIM), lambda b: (b, 0)),
        ],
        out_specs=pl.BlockSpec((SEQ, N_HEADS * V_DIM), lambda b: (b, 0)),
        out_shape=jax.ShapeDtypeStruct((T_CTX, N_HEADS * V_DIM), BF16),
        compiler_params=pltpu.CompilerParams(dimension_semantics=("parallel",)),
        name="ctx_attention",
    )(q, k, v)


LAT_TQ = 1024
LAT_TK = 512
LAT_PIECES = 4


def _lat_attn_kernel(q_ref, k_ref, v_ref, kc_ref, vc_ref, o_ref, s_ref, p_ref, m_ref):
    n_chunks = DEC_SEQ // LAT_TK + 1
    tp = q_ref.shape[0] // LAT_PIECES
    pieces = [pl.ds(j * tp, tp) for j in range(LAT_PIECES)]
    groups = [pieces[:2], pieces[2:]]
    state = {}

    def keys(c):
        return kc_ref[...] if c == n_chunks - 1 else k_ref[c * LAT_TK:(c + 1) * LAT_TK, :]

    def values(c):
        return vc_ref[...] if c == n_chunks - 1 else v_ref[c * LAT_TK:(c + 1) * LAT_TK, :]

    def qk(r, c):
        s = _dot_nt(q_ref[r, :], keys(c))
        s_ref[r, c * LAT_TK:(c + 1) * LAT_TK] = s
        mp = state.get(("m", r.start), jnp.full((tp, LANES), NEG_INF, F32))
        for j in range(LAT_TK // LANES):
            mp = jnp.maximum(mp, s[:, j * LANES:(j + 1) * LANES])
        state[("m", r.start)] = mp

    def row_max(r):
        m_ref[r, :] = jnp.broadcast_to(jnp.max(state[("m", r.start)], axis=-1, keepdims=True), (tp, LANES))

    def exp_chunk(r, c, after=None):
        lp = state.get(("l", r.start), jnp.zeros((tp, LANES), F32))
        m = m_ref[r, :]
        if after is not None:
            bits = lax.bitcast_convert_type(after, jnp.uint32)
            zero = lax.shift_right_logical(lax.shift_right_logical(bits, jnp.uint32(16)), jnp.uint32(16))
            m = m + lax.bitcast_convert_type(zero, F32)
        for j in range(c * LAT_TK // LANES, (c + 1) * LAT_TK // LANES):
            p = jnp.exp2(s_ref[r, j * LANES:(j + 1) * LANES] - m)
            lp = lp + p
            p_ref[r, j * LANES:(j + 1) * LANES] = p.astype(BF16)
        state[("l", r.start)] = lp

    def pv(r, c):
        acc = state.get(("a", r.start), jnp.zeros((tp, V_DIM), F32))
        state[("a", r.start)] = acc + _dot(p_ref[r, c * LAT_TK:(c + 1) * LAT_TK], values(c))

    def finish(r):
        o_ref[r, :] = (state[("a", r.start)] / jnp.sum(state[("l", r.start)], axis=-1, keepdims=True)).astype(BF16)

    for c in range(n_chunks):
        for r in groups[0]:
            qk(r, c)
    for r in groups[0]:
        row_max(r)
    for c in range(n_chunks):
        for r in groups[1]:
            qk(r, c)
        for r in groups[0]:
            exp_chunk(r, c)
    for r in groups[1]:
        row_max(r)
    for c in range(n_chunks):
        for r in groups[0]:
            pv(r, c)
        for r0, r in zip(groups[0], groups[1]):
            exp_chunk(r, c, after=state[("a", r0.start)])
    for r in groups[0]:
        finish(r)
    for c in range(n_chunks):
        for r in groups[1]:
            pv(r, c)
    for r in groups[1]:
        finish(r)


def _lat_attention(q, k, v, kc, vc):
    nq = DEC_SEQ // LAT_TQ
    return pl.pallas_call(
        _lat_attn_kernel,
        grid=(DEC_BATCH, N_HEADS, nq),
        in_specs=[
            pl.BlockSpec((LAT_TQ, HEAD_PAD), lambda b, h, i: ((b + 1) * nq + i, h)),
            pl.BlockSpec((DEC_SEQ, HEAD_PAD), lambda b, h, i: (b + 1, h)),
            pl.BlockSpec((DEC_SEQ, V_DIM), lambda b, h, i: (b + 1, h)),
            pl.BlockSpec((PAST_LEN, HEAD_PAD), lambda b, h, i: (b, h)),
            pl.BlockSpec((PAST_LEN, V_DIM), lambda b, h, i: (b, h)),
        ],
        out_specs=pl.BlockSpec((LAT_TQ, V_DIM), lambda b, h, i: (b * nq + i, h)),
        out_shape=jax.ShapeDtypeStruct((T_LAT, N_HEADS * V_DIM), BF16),
        scratch_shapes=[pltpu.VMEM((LAT_TQ, DEC_SEQ + PAST_LEN), F32),
                        pltpu.VMEM((LAT_TQ, DEC_SEQ + PAST_LEN), BF16),
                        pltpu.VMEM((LAT_TQ, LANES), F32)],
        compiler_params=pltpu.CompilerParams(
            dimension_semantics=("parallel", "parallel", "parallel")),
        name="lat_attention",
    )(q, k, v, kc, vc)


OPROJ_TM = 512


def _oproj_kernel(oc_ref, ol_ref, x_ref, mods_ref, wo_ref, g_ref, b_ref, out_ref):
    is_ctx = pl.program_id(0) < T_CTX // OPROJ_TM
    o = jnp.where(is_ctx, oc_ref[...], ol_ref[...])
    gm = mods_ref[0][2:3]
    y = _dot(o, wo_ref[...])
    out_ref[...] = _layer_norm(ALPHA * x_ref[...] + gm * y, g_ref[...], b_ref[...])


def _oproj(o_ctx, o_lat, x, mods, w_o, ln_g, ln_b):
    t = x.shape[0]
    tm = OPROJ_TM
    n_ctx = T_CTX // tm
    return pl.pallas_call(
        _oproj_kernel,
        grid=(t // tm,),
        in_specs=[
            pl.BlockSpec((tm, N_HEADS * V_DIM), lambda i: (jnp.minimum(i, n_ctx - 1), 0)),
            pl.BlockSpec((tm, N_HEADS * V_DIM), lambda i: (jnp.maximum(i - n_ctx, 0), 0)),
            pl.BlockSpec((tm, D_MODEL), lambda i: (i, 0)),
            pl.BlockSpec((1, 6, D_MODEL), lambda i: (i * tm // GROUP_ROWS, 0, 0)),
            pl.BlockSpec((N_HEADS * V_DIM, D_MODEL), lambda i: (0, 0)),
            pl.BlockSpec((1, D_MODEL), lambda i: (0, 0)),
            pl.BlockSpec((1, D_MODEL), lambda i: (0, 0)),
        ],
        out_specs=pl.BlockSpec((tm, D_MODEL), lambda i: (i, 0)),
        out_shape=jax.ShapeDtypeStruct((t, D_MODEL), F32),
        compiler_params=pltpu.CompilerParams(dimension_semantics=("parallel",)),
        name="attn_oproj",
    )(o_ctx, o_lat, x, mods, w_o, ln_g, ln_b)


def _swap16(w):
    q = QK_ROPE // 4
    return jnp.concatenate([w[..., q:2 * q], w[..., :q], w[..., 3 * q:], w[..., 2 * q:3 * q]], axis=-1)


def _mla_params(w_dqkv, q_norm, w_uq, kv_norm, w_ukv):
    w_kr = w_dqkv[:, Q_LORA + KV_LORA:]
    wq = w_uq.reshape(Q_LORA, N_HEADS, QK_NOPE + QK_ROPE)
    wq_r = wq[..., QK_NOPE:]
    wq = jnp.concatenate([wq[..., :QK_NOPE], wq_r, _swap16(wq_r)], axis=-1)
    wkv = w_ukv.reshape(KV_LORA, N_HEADS, QK_NOPE + V_DIM)
    return {
        "w_dq": w_dqkv[:, :Q_LORA].astype(BF16),
        "w_dkv": w_dqkv[:, Q_LORA:Q_LORA + KV_LORA].astype(BF16),
        "w_kr": jnp.concatenate([w_kr, _swap16(w_kr)], axis=-1).astype(BF16),
        "q_norm": q_norm.reshape(1, Q_LORA),
        "kv_norm": kv_norm.reshape(1, KV_LORA),
        "w_uq": wq.reshape(Q_LORA, N_HEADS * HEAD_PAD).astype(BF16),
        "w_ukn": wkv[..., :QK_NOPE].reshape(KV_LORA, N_HEADS * QK_NOPE).astype(BF16),
        "w_uv": wkv[..., QK_NOPE:].reshape(KV_LORA, N_HEADS * V_DIM).astype(BF16),
    }


def _rope_tables():
    nf = QK_ROPE // 4
    t = np.arange(DEC_SEQ)
    row = (t // GRID_W).astype(np.float32)
    col = (t % GRID_W).astype(np.float32)
    inv = (ROPE_THETA ** (-np.arange(nf, dtype=np.float32) / nf)).astype(np.float32)
    ar, ac = row[:, None] * inv, col[:, None] * inv
    pad = np.zeros((DEC_SEQ, QK_ROPE), np.float32)
    cos = np.concatenate([np.cos(ar), np.cos(ar), np.cos(ac), np.cos(ac), pad], axis=-1)
    sin = np.concatenate([-np.sin(ar), np.sin(ar), -np.sin(ac), np.sin(ac), pad], axis=-1)
    cos_id = np.concatenate([np.ones((DEC_SEQ, QK_ROPE), np.float32), pad], axis=-1)
    sin_id = np.zeros((DEC_SEQ, 2 * QK_ROPE), np.float32)
    return (jnp.asarray(np.stack([cos_id, cos]).astype(np.float32)),
            jnp.asarray(np.stack([sin_id, sin]).astype(np.float32)))


def kernel(x_prompt, x_sample, cache_ckv, cache_krope, c, c_ctx, ada_w, ada_b, ln_g, ln_b, conv_w_in, conv_k, conv_w_out, mla_w_dqkv, mla_q_norm, mla_w_uq, mla_kv_norm, mla_w_ukv, mla_w_o, router_w, router_bias, exp_w_gate, exp_w_up, exp_w_down, sh_w_gate, sh_w_up, sh_w_down):
    cvecs =jnp.concatenate([c_ctx[None, :], c, jnp.zeros((SUBLANES - N_GROUPS_ROWS, D_MODEL), F32)], axis=0)
    mods = _adaln(cvecs, ada_w, ada_b)

    def ln(l, k):
        return ln_g[l, k].reshape(1, D_MODEL), ln_b[l, k].reshape(1, D_MODEL)

    def moe_layer(xin, l, split_streams):
        idx, pos, wcol, counts, hp_a, hp_b = _router(xin, mods[l], router_w[l].T,
                                                     router_bias[l].reshape(N_EXPERTS, 1))
        shared = (sh_w_gate[l].astype(BF16), sh_w_up[l].astype(BF16), sh_w_down[l].astype(BF16))
        outs = {}
        for ch in sorted(range(len(MOE_CHUNKS)), key=lambda j: -MOE_CHUNKS[j][1]):
            row0, rows = MOE_CHUNKS[ch]
            n_tiles = _n_tiles(rows)
            cnt = counts[ch, :, 0].astype(jnp.int32)
            start, tile_expert, tile_valid, n_used, plan = _expert_layout(cnt, n_tiles)
            dest = _dest_rows(start, idx[:, row0:row0 + rows], pos[:, row0:row0 + rows])
            xs_a = _sc_scatter_rows(hp_a, dest, row0, n_tiles * FFN_TM)
            xs_b = _sc_scatter_rows(hp_b, dest, row0, n_tiles * FFN_TM)
            ys_a, ys_b = _expert_ffn(xs_a, xs_b, tile_expert, tile_valid, n_used, plan,
                                     exp_w_gate, exp_w_up, exp_w_down, l)
            piece = rows // COMBINE_SPLITS
            for s in range(COMBINE_SPLITS):
                dest_row = dest[:, s * piece:(s + 1) * piece].reshape(1, piece * TOP_K)
                g_a = _sc_gather_rows(ys_a, dest_row).reshape(TOP_K, piece, D_QUARTER)
                g_b = _sc_gather_rows(ys_b, dest_row).reshape(TOP_K, piece, D_QUARTER)
                outs = _combine(xin, mods[l], wcol, g_a, g_b, *shared, *ln(l, 1), row0=row0 + s * piece,
                                split_streams=split_streams, prev=outs)
        return (outs["ctx"], outs["lat"]) if split_streams else outs["all"]

    x = _conv_mixer(x_prompt.reshape(T_CTX, D_MODEL), x_sample.reshape(T_LAT, D_MODEL), mods[0],
                    conv_w_in[0].astype(BF16), conv_k[0], conv_w_out[0].astype(BF16), *ln(0, 0))
    x = moe_layer(x, 0, split_streams=False)

    p = _mla_params(mla_w_dqkv[0], mla_q_norm[0], mla_w_uq[0], mla_kv_norm[0], mla_w_ukv[0])
    rope_a, rope_b = _rope_tables()
    q, k, v, ckv, kr = _mla_proj(x, mods[1], p, rope_a, rope_b)
    kr_cache = jnp.concatenate([cache_krope[:, 0].reshape(DEC_BATCH * PAST_LEN, QK_ROPE),
                                jnp.zeros((DEC_BATCH * PAST_LEN, QK_ROPE), F32)], axis=-1)
    kc, vc = _cache_kv(cache_ckv[:, 0].reshape(DEC_BATCH * PAST_LEN, KV_LORA), kr_cache, p)
    o_ctx = _ctx_attention(q, k, v)
    o_lat = _lat_attention(q, k, v, kc, vc)
    x = _oproj(o_ctx, o_lat, x, mods[1], mla_w_o[0].astype(BF16), *ln(1, 0))
    y_ctx, y_lat = moe_layer(x, 1, split_streams=True)

    y_prompt = y_ctx.reshape(BATCH, SEQ, D_MODEL)
    y_sample = y_lat.reshape(DEC_BATCH, DEC_SEQ, D_MODEL)
    state_ckv = ckv[:T_CTX].reshape(BATCH, 1, SEQ, KV_LORA)
    state_krope = kr[:T_CTX, :QK_ROPE].reshape(BATCH, 1, SEQ, QK_ROPE)
    return (y_prompt, y_sample, state_ckv, state_krope)
```

```python
import functools
import math

import jax
import jax.numpy as jnp
import numpy as np
from jax import lax
from jax.experimental import pallas as pl
from jax.experimental.pallas import tpu as pltpu
from jax.experimental.pallas import tpu_sc as plsc

D_MODEL = 1024
BATCH = 16
SEQ = 256
DEPTH = 2
DEC_BATCH = 4
DEC_SEQ = 4096
PAST_LEN = 512
GRID_W = 64

N_HEADS = 8
QK_NOPE = 128
QK_ROPE = 64
V_DIM = 128
Q_LORA = 384
KV_LORA = 256
ROPE_THETA = 10000.0
ATTN_SCALE = (QK_NOPE + QK_ROPE) ** -0.5
HEAD_PAD = 256
Q_PRESCALE = ATTN_SCALE * math.log2(math.e)

N_EXPERTS = 64
TOP_K = 8
N_GROUPS = 8
TOPK_GROUPS = 4
GROUP_SIZE = N_EXPERTS // N_GROUPS
D_EXPERT = 256
D_SHARED = 256
ROUTED_SCALE = 2.5

ALPHA = (2 * DEPTH) ** 0.25
LN_EPS = 1e-5
RMS_EPS = 1e-6

GROUP_ROWS = 4096
N_GROUPS_ROWS = 1 + DEC_BATCH
T_CTX = BATCH * SEQ
T_LAT = DEC_BATCH * DEC_SEQ
T_ALL = T_CTX + T_LAT
LANES = 128
SUBLANES = 8

F32 = jnp.float32
BF16 = jnp.bfloat16
NEG_INF = float("-inf")


def _dot(a, b):
    return jnp.dot(a, b, preferred_element_type=F32)


def _dot_nt(a, b, precision=None):
    return lax.dot_general(a, b, (((1,), (1,)), ((), ())), precision=precision,
                           preferred_element_type=F32)


def _layer_norm(v, g, b):
    mu = jnp.mean(v, axis=-1, keepdims=True)
    d = v - mu
    var = jnp.mean(d * d, axis=-1, keepdims=True)
    return d * lax.rsqrt(var + LN_EPS) * g + b


def _rms_norm(v, g):
    return v * lax.rsqrt(jnp.mean(v * v, axis=-1, keepdims=True) + RMS_EPS) * g


def _silu(v):
    return v / (1.0 + jnp.exp(-v))


def _sigmoid(v):
    return 1.0 / (1.0 + jnp.exp(-v))


def _adaln_kernel(c_ref, w_ref, b_ref, o_ref):
    c = c_ref[...]
    s = _silu(c)
    o_ref[0, 0] = jnp.dot(s, w_ref[0], precision=lax.Precision.HIGHEST,
                          preferred_element_type=F32) + b_ref[0]


def _adaln(cvecs, ada_w, ada_b):
    out = pl.pallas_call(
        _adaln_kernel,
        grid=(DEPTH, 6),
        in_specs=[
            pl.BlockSpec((SUBLANES, D_MODEL), lambda l, j: (0, 0)),
            pl.BlockSpec((1, D_MODEL, D_MODEL), lambda l, j: (l, 0, j)),
            pl.BlockSpec((1, 1, D_MODEL), lambda l, j: (l, 0, j)),
        ],
        out_specs=pl.BlockSpec((1, 1, SUBLANES, D_MODEL), lambda l, j: (l, j, 0, 0)),
        out_shape=jax.ShapeDtypeStruct((DEPTH, 6, SUBLANES, D_MODEL), F32),
        compiler_params=pltpu.CompilerParams(dimension_semantics=("parallel", "parallel")),
        name="adaln",
    )(cvecs, ada_w, ada_b.reshape(DEPTH, 1, 6 * D_MODEL))
    return jnp.transpose(out[:, :, :N_GROUPS_ROWS, :], (0, 2, 1, 3))


CONV_TM = 1024


def _conv_kernel(xc_ref, xcp_ref, xcn_ref, xl_ref, xlp_ref, xln_ref, mods_ref, win_ref, ck_ref, wout_ref,
                 g_ref, b_ref, o_ref):
    i = pl.program_id(0)
    tm = xc_ref.shape[0]
    is_ctx = i < T_CTX // tm
    m = mods_ref[0]
    sm, cm, gm = m[0:1], m[1:2], m[2:3]
    x = jnp.where(is_ctx, xc_ref[...], xl_ref[...])
    xp = jnp.where(is_ctx, xcp_ref[...], xlp_ref[...])
    xn = jnp.where(is_ctx, xcn_ref[...], xln_ref[...])
    def halo_u(xh):
        hh = (xh * (1.0 + cm) + sm).astype(BF16)
        zh = _dot(hh, win_ref[:, D_MODEL:])
        return zh[:, :D_MODEL] * zh[:, D_MODEL:]

    th = tm // 2
    halves = [x[:th], x[th:]]
    zs = [_dot((xh * (1.0 + cm) + sm).astype(BF16), win_ref[...]) for xh in halves]
    us = [z[:, D_MODEL:2 * D_MODEL] * z[:, 2 * D_MODEL:] for z in zs]
    befores = [halo_u(xp)[SUBLANES - 1:SUBLANES], us[0][th - 1:th]]
    afters = [us[1][0:1], halo_u(xn)[0:1]]

    ck = ck_ref[...]
    row = lax.broadcasted_iota(jnp.int32, (th, 1), 0)
    for hf in range(2):
        grow = i * tm + hf * th + row
        seq_len = jnp.where(grow < T_CTX, SEQ, DEC_SEQ)
        pos = jnp.bitwise_and(grow, seq_len - 1)
        u = us[hf]
        left = jnp.where(row == 0, befores[hf], pltpu.roll(u, 1, 0))
        left = jnp.where(pos == 0, 0.0, left)
        right = jnp.where(row == th - 1, afters[hf], pltpu.roll(u, th - 1, 0))
        right = jnp.where(pos == seq_len - 1, 0.0, right)
        conv = left * ck[0:1] + u * ck[1:2] + right * ck[2:3]
        v = (zs[hf][:, :D_MODEL] * conv).astype(BF16)
        y = _dot(v, wout_ref[...])
        o_ref[hf * th:(hf + 1) * th, :] = _layer_norm(ALPHA * halves[hf] + gm * y, g_ref[...], b_ref[...])


def _conv_mixer(x_ctx, x_lat, mods, w_in, conv_k, w_out, ln_g, ln_b):
    tm = CONV_TM
    per8 = tm // SUBLANES
    n_ctx = T_CTX // tm
    n_lat = T_LAT // tm

    def stream_specs(first, n):
        def blk(i):
            return jnp.clip(i - first, 0, n - 1)
        return [
            pl.BlockSpec((tm, D_MODEL), lambda i: (blk(i), 0)),
            pl.BlockSpec((SUBLANES, D_MODEL), lambda i: (jnp.maximum(blk(i) * per8 - 1, 0), 0)),
            pl.BlockSpec((SUBLANES, D_MODEL), lambda i: (jnp.minimum((blk(i) + 1) * per8, n * per8 - 1), 0)),
        ]

    return pl.pallas_call(
        _conv_kernel,
        grid=(n_ctx + n_lat,),
        in_specs=stream_specs(0, n_ctx) + stream_specs(n_ctx, n_lat) + [
            pl.BlockSpec((1, 6, D_MODEL), lambda i: (i * tm // GROUP_ROWS, 0, 0)),
            pl.BlockSpec((D_MODEL, 3 * D_MODEL), lambda i: (0, 0)),
            pl.BlockSpec((3, D_MODEL), lambda i: (0, 0)),
            pl.BlockSpec((D_MODEL, D_MODEL), lambda i: (0, 0)),
            pl.BlockSpec((1, D_MODEL), lambda i: (0, 0)),
            pl.BlockSpec((1, D_MODEL), lambda i: (0, 0)),
        ],
        out_specs=pl.BlockSpec((tm, D_MODEL), lambda i: (i, 0)),
        out_shape=jax.ShapeDtypeStruct((T_ALL, D_MODEL), F32),
        compiler_params=pltpu.CompilerParams(dimension_semantics=("parallel",)),
        name="conv_mixer",
    )(x_ctx, x_ctx, x_ctx, x_lat, x_lat, x_lat, mods, w_in, conv_k, w_out, ln_g, ln_b)


ROUTER_TM = 512
MOE_CHUNKS = ((0, T_ALL),)
COMBINE_SPLITS = 4


def _first_argmax_mask(cur, ridx, n):
    mx = jnp.max(cur, axis=0, keepdims=True)
    first = jnp.min(jnp.where(cur == mx, ridx, n), axis=0, keepdims=True)
    return ridx == first, mx


def _router_kernel(x_ref, mods_ref, rwt_ref, bias_ref, idx_ref, pos_ref, wcol_ref, count_ref, hpa_ref, hpb_ref,
                   carry_ref):
    tm = x_ref.shape[0]
    m = mods_ref[0]
    sf, cf = m[3:4], m[4:5]
    hc = x_ref[...] * (1.0 + cf) + sf
    words = _pack_rows(hc)
    hpa_ref[...] = words[:, :D_QUARTER]
    hpb_ref[...] = words[:, D_QUARTER:]
    logits = _dot_nt(rwt_ref[...], hc, precision=lax.Precision.HIGHEST)
    scores = _sigmoid(logits)
    biased = scores + bias_ref[...]

    ridx8 = lax.broadcasted_iota(jnp.int32, (GROUP_SIZE, tm), 0)
    gscore = jnp.full((N_GROUPS, tm), NEG_INF, F32)
    for g in range(N_GROUPS):
        blk = biased[g * GROUP_SIZE:(g + 1) * GROUP_SIZE]
        sel, m1 = _first_argmax_mask(blk, ridx8, GROUP_SIZE)
        m2 = jnp.max(jnp.where(sel, NEG_INF, blk), axis=0, keepdims=True)
        gscore = jnp.where(ridx8 == g, m1 + m2, gscore)

    gmask = jnp.zeros((N_GROUPS, tm), jnp.bool_)
    cur = gscore
    for _ in range(TOPK_GROUPS):
        sel, _unused = _first_argmax_mask(cur, ridx8, N_GROUPS)
        gmask = jnp.logical_or(gmask, sel)
        cur = jnp.where(sel, NEG_INF, cur)

    gmask_f = gmask.astype(F32)
    blocks = []
    for g in range(N_GROUPS):
        keep = jnp.broadcast_to(gmask_f[g:g + 1], (GROUP_SIZE, tm)) > 0.5
        blocks.append(jnp.where(keep, biased[g * GROUP_SIZE:(g + 1) * GROUP_SIZE], NEG_INF))
    cur = jnp.concatenate(blocks, axis=0)

    first_tiles = [row0 // tm for row0, _rows in MOE_CHUNKS]
    starts_chunk = functools.reduce(jnp.logical_or, [pl.program_id(0) == ft for ft in first_tiles])

    @pl.when(starts_chunk)
    def _():
        carry_ref[...] = jnp.zeros(carry_ref.shape, F32)

    ridx = lax.broadcasted_iota(jnp.int32, (N_EXPERTS, tm), 0)
    kidx = lax.broadcasted_iota(jnp.int32, (TOP_K, tm), 0)
    sels = []
    chosen = jnp.zeros((N_EXPERTS, tm), jnp.bool_)
    idx_rows = jnp.zeros((TOP_K, tm), jnp.int32)
    for k in range(TOP_K):
        mx = jnp.max(cur, axis=0, keepdims=True)
        first = jnp.min(jnp.where(cur == mx, ridx, N_EXPERTS), axis=0, keepdims=True)
        sel = ridx == first
        sels.append(sel)
        chosen = jnp.logical_or(chosen, sel)
        idx_rows = jnp.where(kidx == k, first, idx_rows)
        cur = jnp.where(sel, NEG_INF, cur)

    onehot = chosen.astype(F32)
    t_row = lax.broadcasted_iota(jnp.int32, (tm, tm), 0)
    t_col = lax.broadcasted_iota(jnp.int32, (tm, tm), 1)
    before = (t_row < t_col).astype(BF16)
    rank = carry_ref[...] + _dot(onehot.astype(BF16), before)
    carry_ref[...] = carry_ref[...] + jnp.sum(onehot, axis=1, keepdims=True)
    count_ref[0] = jnp.broadcast_to(carry_ref[...], count_ref.shape[1:])

    w = jnp.where(chosen, scores, 0.0)
    w = w / jnp.sum(w, axis=0, keepdims=True) * ROUTED_SCALE
    pos_rows = jnp.zeros((TOP_K, tm), F32)
    w_rows = jnp.zeros((TOP_K, tm), F32)
    for k in range(TOP_K):
        pos_rows = jnp.where(kidx == k, jnp.sum(jnp.where(sels[k], rank, 0.0), axis=0, keepdims=True), pos_rows)
        w_rows = jnp.where(kidx == k, jnp.sum(jnp.where(sels[k], w, 0.0), axis=0, keepdims=True), w_rows)
    idx_ref[...] = idx_rows
    pos_ref[...] = pos_rows.astype(jnp.int32)
    wpad = jnp.concatenate([w_rows, jnp.zeros((LANES - TOP_K, tm), F32)], axis=0)
    wcol_ref[...] = wpad.T


def _router(x, mods, router_wt, router_bias):
    t = x.shape[0]
    tm = ROUTER_TM

    def chunk_of(i):
        return sum((i >= row0 // tm).astype(jnp.int32) for row0, _rows in MOE_CHUNKS[1:])

    return pl.pallas_call(
        _router_kernel,
        grid=(t // tm,),
        in_specs=[
            pl.BlockSpec((tm, D_MODEL), lambda i: (i, 0)),
            pl.BlockSpec((1, 6, D_MODEL), lambda i: (i * tm // GROUP_ROWS, 0, 0)),
            pl.BlockSpec((N_EXPERTS, D_MODEL), lambda i: (0, 0)),
            pl.BlockSpec((N_EXPERTS, 1), lambda i: (0, 0)),
        ],
        out_specs=[
            pl.BlockSpec((TOP_K, tm), lambda i: (0, i)),
            pl.BlockSpec((TOP_K, tm), lambda i: (0, i)),
            pl.BlockSpec((tm, LANES), lambda i: (i, 0)),
            pl.BlockSpec((1, N_EXPERTS, LANES), lambda i: (chunk_of(i), 0, 0)),
            pl.BlockSpec((tm, D_QUARTER), lambda i: (i, 0)),
            pl.BlockSpec((tm, D_QUARTER), lambda i: (i, 0)),
        ],
        out_shape=[
            jax.ShapeDtypeStruct((TOP_K, t), jnp.int32),
            jax.ShapeDtypeStruct((TOP_K, t), jnp.int32),
            jax.ShapeDtypeStruct((t, LANES), F32),
            jax.ShapeDtypeStruct((len(MOE_CHUNKS), N_EXPERTS, LANES), F32),
            jax.ShapeDtypeStruct((t, D_QUARTER), jnp.uint32),
            jax.ShapeDtypeStruct((t, D_QUARTER), jnp.uint32),
        ],
        scratch_shapes=[pltpu.VMEM((N_EXPERTS, 1), F32)],
        compiler_params=pltpu.CompilerParams(dimension_semantics=("arbitrary",)),
        name="moe_router",
    )(x, mods, router_wt, router_bias)


FFN_TM = 512


def _n_tiles(rows):
    return rows * TOP_K // FFN_TM + N_EXPERTS


def _expert_layout(counts, n_tiles):
    padded = (counts + FFN_TM - 1) // FFN_TM * FFN_TM
    end = jnp.cumsum(padded)
    start = end - padded
    tile_row = jnp.arange(n_tiles, dtype=jnp.int32) * FFN_TM
    tile_expert = jnp.minimum(jnp.sum(end[None, :] <= tile_row[:, None], axis=1), N_EXPERTS - 1)
    of_tile = tile_expert[:, None] == jnp.arange(N_EXPERTS, dtype=jnp.int32)[None, :]
    live_end = jnp.sum(jnp.where(of_tile, (start + counts)[None, :], 0), axis=1)
    tile_valid = jnp.clip(live_end - tile_row, 0, FFN_TM)
    n_used = (end[-1] // FFN_TM).astype(jnp.int32).reshape(1)
    used = jnp.arange(n_tiles, dtype=jnp.int32) < n_used[0]
    prev_expert = jnp.concatenate([jnp.full((1,), -1, tile_expert.dtype), tile_expert[:-1]])
    run_flag = jnp.logical_and(used, tile_expert != prev_expert)
    run_slot = (jnp.cumsum(run_flag.astype(jnp.int32)) - 1) % 2
    later = jnp.logical_and(used[None, :], tile_expert[None, :] > tile_expert[:, None])
    next_expert = jnp.min(jnp.where(later, tile_expert[None, :], N_EXPERTS), axis=1)
    next_expert = jnp.where(next_expert == N_EXPERTS, -1, next_expert)
    plan = (run_flag.astype(jnp.int32), run_slot.astype(jnp.int32), next_expert.astype(jnp.int32))
    return start.astype(jnp.int32), tile_expert.astype(jnp.int32), tile_valid.astype(jnp.int32), n_used, plan


D_HALF = D_MODEL // 2
D_QUARTER = D_MODEL // 4


def _pack_rows(v):
    hi = lax.bitcast_convert_type(v[:, :D_HALF].astype(BF16).astype(F32), jnp.uint32)
    lo = lax.bitcast_convert_type(v[:, D_HALF:].astype(BF16).astype(F32), jnp.uint32)
    return jnp.bitwise_or(hi, jnp.right_shift(lo, jnp.uint32(16)))


def _unpack_rows(w):
    hi = lax.bitcast_convert_type(jnp.bitwise_and(w, jnp.uint32(0xFFFF0000)), F32)
    lo = lax.bitcast_convert_type(jnp.left_shift(w, jnp.uint32(16)), F32)
    return hi, lo


DEST_TM = 2048


def _dest_kernel(start_ref, idx_ref, pos_ref, dest_ref):
    idx = idx_ref[...]
    base = jnp.zeros(idx.shape, jnp.int32)
    for e in range(N_EXPERTS):
        base = jnp.where(idx == e, start_ref[e], base)
    dest_ref[...] = base + pos_ref[...]


def _dest_rows(start, idx, pos):
    t = idx.shape[1]
    return pl.pallas_call(
        _dest_kernel,
        grid_spec=pltpu.PrefetchScalarGridSpec(
            num_scalar_prefetch=1,
            grid=(t // DEST_TM,),
            in_specs=[pl.BlockSpec((TOP_K, DEST_TM), lambda i, s: (0, i)),
                      pl.BlockSpec((TOP_K, DEST_TM), lambda i, s: (0, i))],
            out_specs=pl.BlockSpec((TOP_K, DEST_TM), lambda i, s: (0, i)),
        ),
        out_shape=jax.ShapeDtypeStruct((TOP_K, t), jnp.int32),
        compiler_params=pltpu.CompilerParams(dimension_semantics=("parallel",)),
        name="moe_dest",
    )(start, idx, pos)


SC_WINDOW = 128


def _sc_mesh():
    return plsc.VectorSubcoreMesh(core_axis_name="c", subcore_axis_name="s")


def _sc_scatter_rows(x, dest, row0, n_sorted):
    t = dest.shape[1]
    blk0 = row0 // SC_WINDOW

    @functools.partial(
        pl.kernel,
        out_type=jax.ShapeDtypeStruct((n_sorted, D_QUARTER), x.dtype),
        mesh=_sc_mesh(),
        scratch_types=[],
    )
    def scatter(x_hbm, i_hbm, o_hbm):
        def body(x_vmem, i_vmem):
            pltpu.sync_copy(x_vmem, o_hbm.at[i_vmem.at[0]])

        pltpu.emit_pipeline(
            body,
            grid=(t // SC_WINDOW, TOP_K),
            in_specs=[pl.BlockSpec((SC_WINDOW, D_QUARTER), lambda i, k: (i + blk0, 0)),
                      pl.BlockSpec((1, SC_WINDOW), lambda i, k: (k, i))],
            out_specs=[],
            core_axis_name=("c", "s"),
            dimension_semantics=(pltpu.PARALLEL, pltpu.ARBITRARY),
        )(x_hbm, i_hbm)

    return scatter(x, dest)


def _sc_gather_rows(table, idx):
    m = idx.shape[1]

    @functools.partial(
        pl.kernel,
        out_type=jax.ShapeDtypeStruct((m, D_QUARTER), table.dtype),
        mesh=_sc_mesh(),
        scratch_types=[],
    )
    def gather(t_hbm, i_hbm, o_hbm):
        def body(i_vmem, o_vmem):
            pltpu.sync_copy(t_hbm.at[i_vmem.at[0]], o_vmem)

        pltpu.emit_pipeline(
            body,
            grid=(m // SC_WINDOW,),
            in_specs=[pl.BlockSpec((1, SC_WINDOW), lambda i: (0, i))],
            out_specs=[pl.BlockSpec((SC_WINDOW, D_QUARTER), lambda i: (i, 0))],
            core_axis_name=("c", "s"),
            dimension_semantics=(pltpu.PARALLEL,),
        )(i_hbm, o_hbm)

    return gather(table, idx)


FFN_STEP_TILES = 2
FFN_PIECES = 1


def _ffn_kernel(layer, te_ref, tv_ref, nu_ref, flag_ref, slot_ref, next_ref, xa_ref, xb_ref,
                wg_hbm, wu_hbm, wd_hbm, ya_ref, yb_ref, wg_st, wu_st, wd_st, wgb_ref, wub_ref, wdb_ref, sem):
    step = pl.program_id(0)

    def fetch(expert, slot):
        return [pltpu.make_async_copy(src.at[layer, expert], dst.at[slot], sem.at[slot])
                for src, dst in ((wg_hbm, wg_st), (wu_hbm, wu_st), (wd_hbm, wd_st))]

    for s in range(FFN_STEP_TILES):
        tile = step * FFN_STEP_TILES + s
        active = tile < nu_ref[0]
        tile_rows = s * FFN_TM

        @pl.when(jnp.logical_and(active, flag_ref[tile] == 1))
        def _():
            slot = slot_ref[tile]

            @pl.when(tile == 0)
            def _():
                for cp in fetch(te_ref[tile], slot):
                    cp.start()

            for cp in fetch(te_ref[tile], slot):
                cp.wait()
            wgb_ref[...] = wg_st[slot].astype(BF16)
            wub_ref[...] = wu_st[slot].astype(BF16)
            wdb_ref[...] = wd_st[slot].astype(BF16)

            @pl.when(next_ref[tile] >= 0)
            def _():
                for cp in fetch(next_ref[tile], 1 - slot):
                    cp.start()

        @pl.when(active)
        def _():
            wg = wgb_ref[...]
            wu = wub_ref[...]
            wd = wdb_ref[...]
            th = FFN_TM // FFN_PIECES
            for r0 in range(0, FFN_TM, th):
                rows = pl.ds(tile_rows + r0, th)
                live = (lax.broadcasted_iota(jnp.int32, (th, D_QUARTER), 0) + r0) < tv_ref[tile]
                hi_a, lo_a = _unpack_rows(jnp.where(live, xa_ref[rows, :], jnp.uint32(0)))
                hi_b, lo_b = _unpack_rows(jnp.where(live, xb_ref[rows, :], jnp.uint32(0)))
                xb = jnp.concatenate([hi_a.astype(BF16), hi_b.astype(BF16), lo_a.astype(BF16), lo_b.astype(BF16)],
                                     axis=1)
                a = (_silu(_dot(xb, wg)) * _dot(xb, wu)).astype(BF16)
                words = _pack_rows(_dot(a, wd))
                ya_ref[rows, :] = words[:, :D_QUARTER]
                yb_ref[rows, :] = words[:, D_QUARTER:]

        @pl.when(jnp.logical_not(active))
        def _():
            rows = pl.ds(tile_rows, FFN_TM)
            ya_ref[rows, :] = jnp.zeros((FFN_TM, D_QUARTER), jnp.uint32)
            yb_ref[rows, :] = jnp.zeros((FFN_TM, D_QUARTER), jnp.uint32)


def _expert_ffn(xs_a, xs_b, tile_expert, tile_valid, n_used, plan, wg, wu, wd, layer):
    n_tiles = xs_a.shape[0] // FFN_TM
    step_rows = FFN_STEP_TILES * FFN_TM

    def row_map(i, te, tv, nu, fl, sl, nx):
        return (jnp.minimum(i, (nu[0] - 1) // FFN_STEP_TILES), 0)

    def out_map(i, te, tv, nu, fl, sl, nx):
        return (i, 0)

    hbm = pl.BlockSpec(memory_space=pl.ANY)
    return pl.pallas_call(
        functools.partial(_ffn_kernel, layer),
        grid_spec=pltpu.PrefetchScalarGridSpec(
            num_scalar_prefetch=6,
            grid=(n_tiles // FFN_STEP_TILES,),
            in_specs=[pl.BlockSpec((step_rows, D_QUARTER), row_map),
                      pl.BlockSpec((step_rows, D_QUARTER), row_map), hbm, hbm, hbm],
            out_specs=[pl.BlockSpec((step_rows, D_QUARTER), out_map), pl.BlockSpec((step_rows, D_QUARTER), out_map)],
            scratch_shapes=[pltpu.VMEM((2, D_MODEL, D_EXPERT), F32), pltpu.VMEM((2, D_MODEL, D_EXPERT), F32),
                            pltpu.VMEM((2, D_EXPERT, D_MODEL), F32),
                            pltpu.VMEM((D_MODEL, D_EXPERT), BF16), pltpu.VMEM((D_MODEL, D_EXPERT), BF16),
                            pltpu.VMEM((D_EXPERT, D_MODEL), BF16), pltpu.SemaphoreType.DMA((2,))],
        ),
        out_shape=[jax.ShapeDtypeStruct((n_tiles * FFN_TM, D_QUARTER), jnp.uint32),
                   jax.ShapeDtypeStruct((n_tiles * FFN_TM, D_QUARTER), jnp.uint32)],
        compiler_params=pltpu.CompilerParams(dimension_semantics=("arbitrary",)),
        name="moe_expert_ffn",
    )(tile_expert, tile_valid, n_used, *plan, xs_a, xs_b, wg, wu, wd)


COMBINE_TM = 256


def _combine_kernel(n_prev, tile0, x_ref, mods_ref, wcol_ref, ga_ref, gb_ref, sg_ref, su_ref, sd_ref, g_ref, b_ref,
                    *rest):
    o_refs = rest[n_prev:]
    m = mods_ref[0]
    sf, cf, gf = m[3:4], m[4:5], m[5:6]
    x = x_ref[...]
    hc = (x * (1.0 + cf) + sf).astype(BF16)
    a = _silu(_dot(hc, sg_ref[...])) * _dot(hc, su_ref[...])
    y = _dot(a.astype(BF16), sd_ref[...])
    wcol = wcol_ref[...]
    parts = [y[:, q * D_QUARTER:(q + 1) * D_QUARTER] for q in range(4)]
    for k in range(TOP_K):
        hi_a, lo_a = _unpack_rows(ga_ref[k])
        hi_b, lo_b = _unpack_rows(gb_ref[k])
        wk = wcol[:, k:k + 1]
        parts = [parts[0] + wk * hi_a, parts[1] + wk * hi_b, parts[2] + wk * lo_a, parts[3] + wk * lo_b]
    y = jnp.concatenate(parts, axis=1)
    out = _layer_norm(ALPHA * x + gf * y, g_ref[...], b_ref[...])
    if len(o_refs) == 1:
        o_refs[0][...] = out
    else:
        is_ctx = pl.program_id(0) + tile0 < T_CTX // x_ref.shape[0]

        @pl.when(is_ctx)
        def _():
            o_refs[0][...] = out

        @pl.when(jnp.logical_not(is_ctx))
        def _():
            o_refs[1][...] = out


def _combine(x, mods, wcol, g_a, g_b, sg, su, sd, ln_g, ln_b, row0, split_streams, prev):
    tm = COMBINE_TM
    n = g_a.shape[1] // tm
    tile0 = row0 // tm
    n_ctx = T_CTX // tm
    full = lambda shape: pl.BlockSpec(shape, lambda i: (0,) * len(shape))
    layouts = {
        "all": (lambda i: (tile0 + i, 0), T_ALL),
        "ctx": (lambda i: (jnp.minimum(tile0 + i, n_ctx - 1), 0), T_CTX),
        "lat": (lambda i: (jnp.maximum(tile0 + i - n_ctx, 0), 0), T_LAT),
    }
    if split_streams:
        kinds = (["ctx"] if tile0 < n_ctx else []) + (["lat"] if tile0 + n > n_ctx else [])
    else:
        kinds = ["all"]
    carried = [kd for kd in kinds if kd in prev]
    outs = pl.pallas_call(
        functools.partial(_combine_kernel, len(carried), tile0),
        grid=(n,),
        in_specs=[
            pl.BlockSpec((tm, D_MODEL), lambda i: (tile0 + i, 0)),
            pl.BlockSpec((1, 6, D_MODEL), lambda i: ((tile0 + i) * tm // GROUP_ROWS, 0, 0)),
            pl.BlockSpec((tm, LANES), lambda i: (tile0 + i, 0)),
            pl.BlockSpec((TOP_K, tm, D_QUARTER), lambda i: (0, i, 0)),
            pl.BlockSpec((TOP_K, tm, D_QUARTER), lambda i: (0, i, 0)),
            full((D_MODEL, D_SHARED)), full((D_MODEL, D_SHARED)), full((D_SHARED, D_MODEL)),
            full((1, D_MODEL)), full((1, D_MODEL)),
        ] + [pl.BlockSpec(memory_space=pl.ANY) for _ in carried],
        out_specs=[pl.BlockSpec((tm, D_MODEL), layouts[kd][0]) for kd in kinds],
        out_shape=[jax.ShapeDtypeStruct((layouts[kd][1], D_MODEL), F32) for kd in kinds],
        input_output_aliases={10 + j: kinds.index(kd) for j, kd in enumerate(carried)},
        compiler_params=pltpu.CompilerParams(dimension_semantics=("arbitrary",)),
        name="moe_combine",
    )(x, mods, wcol, g_a, g_b, sg, su, sd, ln_g, ln_b, *[prev[kd] for kd in carried])
    return {**prev, **dict(zip(kinds, outs))}


MLA_TM = 1024


def _mla_proj_kernel(x_ref, mods_ref, wdq_ref, wdkv_ref, wkr_ref, qn_ref, kvn_ref, wuq_ref,
                     wukn_ref, wuv_ref, ta_ref, tb_ref, q_ref, k_ref, v_ref, ckv_ref, kr_ref):
    m = mods_ref[0]
    sm, cm = m[0:1], m[1:2]
    h = (x_ref[...] * (1.0 + cm) + sm).astype(BF16)
    cq = _rms_norm(_dot(h, wdq_ref[...]), qn_ref[...])
    ckv = _rms_norm(_dot(h, wdkv_ref[...]), kvn_ref[...])
    kr2 = _dot(h, wkr_ref[...])
    ckv_ref[...] = ckv
    kr_ref[...] = kr2

    ka = ta_ref[0]
    kb = tb_ref[0]
    tm = ka.shape[0]
    ta = jnp.concatenate([jnp.full((tm, QK_NOPE), Q_PRESCALE, F32), ka * Q_PRESCALE], axis=1)
    tb = jnp.concatenate([jnp.zeros((tm, QK_NOPE), F32), kb * Q_PRESCALE], axis=1)
    krr = kr2 * ka + pltpu.roll(kr2, QK_ROPE, 1) * kb

    qpre = _dot(cq.astype(BF16), wuq_ref[...])
    ckv_b = ckv.astype(BF16)
    kn = _dot(ckv_b, wukn_ref[...])
    v_ref[...] = _dot(ckv_b, wuv_ref[...]).astype(BF16)
    for hd in range(N_HEADS):
        qh = qpre[:, hd * HEAD_PAD:(hd + 1) * HEAD_PAD]
        qrot = qh * ta + pltpu.roll(qh, HEAD_PAD - QK_ROPE, 1) * tb
        q_ref[:, hd * HEAD_PAD:(hd + 1) * HEAD_PAD] = qrot.astype(BF16)
        k_ref[:, hd * HEAD_PAD:hd * HEAD_PAD + QK_NOPE] = kn[:, hd * QK_NOPE:(hd + 1) * QK_NOPE].astype(BF16)
        k_ref[:, hd * HEAD_PAD + QK_NOPE:(hd + 1) * HEAD_PAD] = krr.astype(BF16)


def _mla_proj(x, mods, p, rope_a, rope_b):
    t = x.shape[0]
    tm = MLA_TM
    full = lambda shape: pl.BlockSpec(shape, lambda i: (0,) * len(shape))
    rope_spec = pl.BlockSpec(
        (1, tm, 2 * QK_ROPE),
        lambda i: (jnp.minimum(i * tm // GROUP_ROWS, 1), (i * tm % GROUP_ROWS) // tm, 0))
    return pl.pallas_call(
        _mla_proj_kernel,
        grid=(t // tm,),
        in_specs=[
            pl.BlockSpec((tm, D_MODEL), lambda i: (i, 0)),
            pl.BlockSpec((1, 6, D_MODEL), lambda i: (i * tm // GROUP_ROWS, 0, 0)),
            full((D_MODEL, Q_LORA)), full((D_MODEL, KV_LORA)), full((D_MODEL, 2 * QK_ROPE)),
            full((1, Q_LORA)), full((1, KV_LORA)),
            full((Q_LORA, N_HEADS * HEAD_PAD)),
            full((KV_LORA, N_HEADS * QK_NOPE)), full((KV_LORA, N_HEADS * V_DIM)),
            rope_spec, rope_spec,
        ],
        out_specs=[
            pl.BlockSpec((tm, N_HEADS * HEAD_PAD), lambda i: (i, 0)),
            pl.BlockSpec((tm, N_HEADS * HEAD_PAD), lambda i: (i, 0)),
            pl.BlockSpec((tm, N_HEADS * V_DIM), lambda i: (i, 0)),
            pl.BlockSpec((tm, KV_LORA), lambda i: (i, 0)),
            pl.BlockSpec((tm, 2 * QK_ROPE), lambda i: (i, 0)),
        ],
        out_shape=[
            jax.ShapeDtypeStruct((t, N_HEADS * HEAD_PAD), BF16),
            jax.ShapeDtypeStruct((t, N_HEADS * HEAD_PAD), BF16),
            jax.ShapeDtypeStruct((t, N_HEADS * V_DIM), BF16),
            jax.ShapeDtypeStruct((t, KV_LORA), F32),
            jax.ShapeDtypeStruct((t, 2 * QK_ROPE), F32),
        ],
        compiler_params=pltpu.CompilerParams(dimension_semantics=("parallel",)),
        name="mla_proj",
    )(x, mods, p["w_dq"], p["w_dkv"], p["w_kr"], p["q_norm"], p["kv_norm"], p["w_uq"],
      p["w_ukn"], p["w_uv"], rope_a, rope_b)


def _cache_kv_kernel(ckv_ref, kr_ref, wukn_ref, wuv_ref, k_ref, v_ref):
    ckv_b = ckv_ref[...].astype(BF16)
    kn = _dot(ckv_b, wukn_ref[...])
    v_ref[...] = _dot(ckv_b, wuv_ref[...]).astype(BF16)
    kr = kr_ref[...].astype(BF16)
    for hd in range(N_HEADS):
        k_ref[:, hd * HEAD_PAD:hd * HEAD_PAD + QK_NOPE] = kn[:, hd * QK_NOPE:(hd + 1) * QK_NOPE].astype(BF16)
        k_ref[:, hd * HEAD_PAD + QK_NOPE:(hd + 1) * HEAD_PAD] = kr


def _cache_kv(ckv, kr_pad, p):
    t = ckv.shape[0]
    tm = PAST_LEN
    full = lambda shape: pl.BlockSpec(shape, lambda i: (0,) * len(shape))
    return pl.pallas_call(
        _cache_kv_kernel,
        grid=(t // tm,),
        in_specs=[
            pl.BlockSpec((tm, KV_LORA), lambda i: (i, 0)),
            pl.BlockSpec((tm, 2 * QK_ROPE), lambda i: (i, 0)),
            full((KV_LORA, N_HEADS * QK_NOPE)), full((KV_LORA, N_HEADS * V_DIM)),
        ],
        out_specs=[
            pl.BlockSpec((tm, N_HEADS * HEAD_PAD), lambda i: (i, 0)),
            pl.BlockSpec((tm, N_HEADS * V_DIM), lambda i: (i, 0)),
        ],
        out_shape=[
            jax.ShapeDtypeStruct((t, N_HEADS * HEAD_PAD), BF16),
            jax.ShapeDtypeStruct((t, N_HEADS * V_DIM), BF16),
        ],
        compiler_params=pltpu.CompilerParams(dimension_semantics=("parallel",)),
        name="mla_cache_kv",
    )(ckv, kr_pad, p["w_ukn"], p["w_uv"])


def _ctx_attn_kernel(q_ref, k_ref, v_ref, o_ref):
    for hd in range(N_HEADS):
        q = q_ref[:, hd * HEAD_PAD:(hd + 1) * HEAD_PAD]
        k = k_ref[:, hd * HEAD_PAD:(hd + 1) * HEAD_PAD]
        s = _dot_nt(q, k)
        s = s - jnp.max(s, axis=-1, keepdims=True)
        p = jnp.exp2(s)
        p = p / jnp.sum(p, axis=-1, keepdims=True)
        o = _dot(p.astype(BF16), v_ref[:, hd * V_DIM:(hd + 1) * V_DIM])
        o_ref[:, hd * V_DIM:(hd + 1) * V_DIM] = o.astype(BF16)


def _ctx_attention(q, k, v):
    return pl.pallas_call(
        _ctx_attn_kernel,
        grid=(BATCH,),
        in_specs=[
            pl.BlockSpec((SEQ, N_HEADS * HEAD_PAD), lambda b: (b, 0)),
            pl.BlockSpec((SEQ, N_HEADS * HEAD_PAD), lambda b: (b, 0)),
            pl.BlockSpec((SEQ, N_HEADS * V_DIM), lambda b: (b, 0)),
        ],
        out_specs=pl.BlockSpec((SEQ, N_HEADS * V_DIM), lambda b: (b, 0)),
        out_shape=jax.ShapeDtypeStruct((T_CTX, N_HEADS * V_DIM), BF16),
        compiler_params=pltpu.CompilerParams(dimension_semantics=("parallel",)),
        name="ctx_attention",
    )(q, k, v)


LAT_TQ = 1024
LAT_TK = 512
LAT_PIECES = 4


def _lat_attn_kernel(q_ref, k_ref, v_ref, kc_ref, vc_ref, o_ref, s_ref, p_ref, m_ref):
    n_chunks = DEC_SEQ // LAT_TK + 1
    tp = q_ref.shape[0] // LAT_PIECES
    pieces = [pl.ds(j * tp, tp) for j in range(LAT_PIECES)]
    groups = [pieces[:2], pieces[2:]]
    state = {}

    def keys(c):
        return kc_ref[...] if c == n_chunks - 1 else k_ref[c * LAT_TK:(c + 1) * LAT_TK, :]

    def values(c):
        return vc_ref[...] if c == n_chunks - 1 else v_ref[c * LAT_TK:(c + 1) * LAT_TK, :]

    def qk(r, c):
        s = _dot_nt(q_ref[r, :], keys(c))
        s_ref[r, c * LAT_TK:(c + 1) * LAT_TK] = s
        mp = state.get(("m", r.start), jnp.full((tp, LANES), NEG_INF, F32))
        for j in range(LAT_TK // LANES):
            mp = jnp.maximum(mp, s[:, j * LANES:(j + 1) * LANES])
        state[("m", r.start)] = mp

    def row_max(r):
        m_ref[r, :] = jnp.broadcast_to(jnp.max(state[("m", r.start)], axis=-1, keepdims=True), (tp, LANES))

    def exp_chunk(r, c, after=None):
        lp = state.get(("l", r.start), jnp.zeros((tp, LANES), F32))
        m = m_ref[r, :]
        if after is not None:
            bits = lax.bitcast_convert_type(after, jnp.uint32)
            zero = lax.shift_right_logical(lax.shift_right_logical(bits, jnp.uint32(16)), jnp.uint32(16))
            m = m + lax.bitcast_convert_type(zero, F32)
        for j in range(c * LAT_TK // LANES, (c + 1) * LAT_TK // LANES):
            p = jnp.exp2(s_ref[r, j * LANES:(j + 1) * LANES] - m)
            lp = lp + p
            p_ref[r, j * LANES:(j + 1) * LANES] = p.astype(BF16)
        state[("l", r.start)] = lp

    def pv(r, c):
        acc = state.get(("a", r.start), jnp.zeros((tp, V_DIM), F32))
        state[("a", r.start)] = acc + _dot(p_ref[r, c * LAT_TK:(c + 1) * LAT_TK], values(c))

    def finish(r):
        o_ref[r, :] = (state[("a", r.start)] / jnp.sum(state[("l", r.start)], axis=-1, keepdims=True)).astype(BF16)

    for c in range(n_chunks):
        for r in groups[0]:
            qk(r, c)
    for r in groups[0]:
        row_max(r)
    for c in range(n_chunks):
        for r in groups[1]:
            qk(r, c)
        for r in groups[0]:
            exp_chunk(r, c)
    for r in groups[1]:
        row_max(r)
    for c in range(n_chunks):
        for r in groups[0]:
            pv(r, c)
        for r0, r in zip(groups[0], groups[1]):
            exp_chunk(r, c, after=state[("a", r0.start)])
    for r in groups[0]:
        finish(r)
    for c in range(n_chunks):
        for r in groups[1]:
            pv(r, c)
    for r in groups[1]:
        finish(r)


def _lat_attention(q, k, v, kc, vc):
    nq = DEC_SEQ // LAT_TQ
    return pl.pallas_call(
        _lat_attn_kernel,
        grid=(DEC_BATCH, N_HEADS, nq),
        in_specs=[
            pl.BlockSpec((LAT_TQ, HEAD_PAD), lambda b, h, i: ((b + 1) * nq + i, h)),
            pl.BlockSpec((DEC_SEQ, HEAD_PAD), lambda b, h, i: (b + 1, h)),
            pl.BlockSpec((DEC_SEQ, V_DIM), lambda b, h, i: (b + 1, h)),
            pl.BlockSpec((PAST_LEN, HEAD_PAD), lambda b, h, i: (b, h)),
            pl.BlockSpec((PAST_LEN, V_DIM), lambda b, h, i: (b, h)),
        ],
        out_specs=pl.BlockSpec((LAT_TQ, V_DIM), lambda b, h, i: (b * nq + i, h)),
        out_shape=jax.ShapeDtypeStruct((T_LAT, N_HEADS * V_DIM), BF16),
        scratch_shapes=[pltpu.VMEM((LAT_TQ, DEC_SEQ + PAST_LEN), F32),
                        pltpu.VMEM((LAT_TQ, DEC_SEQ + PAST_LEN), BF16),
                        pltpu.VMEM((LAT_TQ, LANES), F32)],
        compiler_params=pltpu.CompilerParams(
            dimension_semantics=("parallel", "parallel", "parallel")),
        name="lat_attention",
    )(q, k, v, kc, vc)


OPROJ_TM = 1024


def _oproj_kernel(oc_ref, ol_ref, x_ref, mods_ref, wo_ref, g_ref, b_ref, out_ref):
    is_ctx = pl.program_id(0) < T_CTX // OPROJ_TM
    o = jnp.where(is_ctx, oc_ref[...], ol_ref[...])
    gm = mods_ref[0][2:3]
    y = _dot(o, wo_ref[...])
    out_ref[...] = _layer_norm(ALPHA * x_ref[...] + gm * y, g_ref[...], b_ref[...])


def _oproj(o_ctx, o_lat, x, mods, w_o, ln_g, ln_b):
    t = x.shape[0]
    tm = OPROJ_TM
    n_ctx = T_CTX // tm
    return pl.pallas_call(
        _oproj_kernel,
        grid=(t // tm,),
        in_specs=[
            pl.BlockSpec((tm, N_HEADS * V_DIM), lambda i: (jnp.minimum(i, n_ctx - 1), 0)),
            pl.BlockSpec((tm, N_HEADS * V_DIM), lambda i: (jnp.maximum(i - n_ctx, 0), 0)),
            pl.BlockSpec((tm, D_MODEL), lambda i: (i, 0)),
            pl.BlockSpec((1, 6, D_MODEL), lambda i: (i * tm // GROUP_ROWS, 0, 0)),
            pl.BlockSpec((N_HEADS * V_DIM, D_MODEL), lambda i: (0, 0)),
            pl.BlockSpec((1, D_MODEL), lambda i: (0, 0)),
            pl.BlockSpec((1, D_MODEL), lambda i: (0, 0)),
        ],
        out_specs=pl.BlockSpec((tm, D_MODEL), lambda i: (i, 0)),
        out_shape=jax.ShapeDtypeStruct((t, D_MODEL), F32),
        compiler_params=pltpu.CompilerParams(dimension_semantics=("parallel",)),
        name="attn_oproj",
    )(o_ctx, o_lat, x, mods, w_o, ln_g, ln_b)


def _swap16(w):
    q = QK_ROPE // 4
    return jnp.concatenate([w[..., q:2 * q], w[..., :q], w[..., 3 * q:], w[..., 2 * q:3 * q]], axis=-1)


def _mla_params(w_dqkv, q_norm, w_uq, kv_norm, w_ukv):
    w_kr = w_dqkv[:, Q_LORA + KV_LORA:]
    wq = w_uq.reshape(Q_LORA, N_HEADS, QK_NOPE + QK_ROPE)
    wq_r = wq[..., QK_NOPE:]
    wq = jnp.concatenate([wq[..., :QK_NOPE], wq_r, _swap16(wq_r)], axis=-1)
    wkv = w_ukv.reshape(KV_LORA, N_HEADS, QK_NOPE + V_DIM)
    return {
        "w_dq": w_dqkv[:, :Q_LORA].astype(BF16),
        "w_dkv": w_dqkv[:, Q_LORA:Q_LORA + KV_LORA].astype(BF16),
        "w_kr": jnp.concatenate([w_kr, _swap16(w_kr)], axis=-1).astype(BF16),
        "q_norm": q_norm.reshape(1, Q_LORA),
        "kv_norm": kv_norm.reshape(1, KV_LORA),
        "w_uq": wq.reshape(Q_LORA, N_HEADS * HEAD_PAD).astype(BF16),
        "w_ukn": wkv[..., :QK_NOPE].reshape(KV_LORA, N_HEADS * QK_NOPE).astype(BF16),
        "w_uv": wkv[..., QK_NOPE:].reshape(KV_LORA, N_HEADS * V_DIM).astype(BF16),
    }


def _rope_tables():
    nf = QK_ROPE // 4
    t = np.arange(DEC_SEQ)
    row = (t // GRID_W).astype(np.float32)
    col = (t % GRID_W).astype(np.float32)
    inv = (ROPE_THETA ** (-np.arange(nf, dtype=np.float32) / nf)).astype(np.float32)
    ar, ac = row[:, None] * inv, col[:, None] * inv
    pad = np.zeros((DEC_SEQ, QK_ROPE), np.float32)
    cos = np.concatenate([np.cos(ar), np.cos(ar), np.cos(ac), np.cos(ac), pad], axis=-1)
    sin = np.concatenate([-np.sin(ar), np.sin(ar), -np.sin(ac), np.sin(ac), pad], axis=-1)
    cos_id = np.concatenate([np.ones((DEC_SEQ, QK_ROPE), np.float32), pad], axis=-1)
    sin_id = np.zeros((DEC_SEQ, 2 * QK_ROPE), np.float32)
    return (jnp.asarray(np.stack([cos_id, cos]).astype(np.float32)),
            jnp.asarray(np.stack([sin_id, sin]).astype(np.float32)))


def kernel(x_prompt, x_sample, cache_ckv, cache_krope, c, c_ctx, ada_w, ada_b, ln_g, ln_b, conv_w_in, conv_k, conv_w_out, mla_w_dqkv, mla_q_norm, mla_w_uq, mla_kv_norm, mla_w_ukv, mla_w_o, router_w, router_bias, exp_w_gate, exp_w_up, exp_w_down, sh_w_gate, sh_w_up, sh_w_down):
    cvecs =jnp.concatenate([c_ctx[None, :], c, jnp.zeros((SUBLANES - N_GROUPS_ROWS, D_MODEL), F32)], axis=0)
    mods = _adaln(cvecs, ada_w, ada_b)

    def ln(l, k):
        return ln_g[l, k].reshape(1, D_MODEL), ln_b[l, k].reshape(1, D_MODEL)

    def moe_layer(xin, l, split_streams):
        idx, pos, wcol, counts, hp_a, hp_b = _router(xin, mods[l], router_w[l].T,
                                                     router_bias[l].reshape(N_EXPERTS, 1))
        shared = (sh_w_gate[l].astype(BF16), sh_w_up[l].astype(BF16), sh_w_down[l].astype(BF16))
        outs = {}
        for ch in sorted(range(len(MOE_CHUNKS)), key=lambda j: -MOE_CHUNKS[j][1]):
            row0, rows = MOE_CHUNKS[ch]
            n_tiles = _n_tiles(rows)
            cnt = counts[ch, :, 0].astype(jnp.int32)
            start, tile_expert, tile_valid, n_used, plan = _expert_layout(cnt, n_tiles)
            dest = _dest_rows(start, idx[:, row0:row0 + rows], pos[:, row0:row0 + rows])
            xs_a = _sc_scatter_rows(hp_a, dest, row0, n_tiles * FFN_TM)
            xs_b = _sc_scatter_rows(hp_b, dest, row0, n_tiles * FFN_TM)
            ys_a, ys_b = _expert_ffn(xs_a, xs_b, tile_expert, tile_valid, n_used, plan,
                                     exp_w_gate, exp_w_up, exp_w_down, l)
            piece = rows // COMBINE_SPLITS
            for s in range(COMBINE_SPLITS):
                dest_row = dest[:, s * piece:(s + 1) * piece].reshape(1, piece * TOP_K)
                g_a = _sc_gather_rows(ys_a, dest_row).reshape(TOP_K, piece, D_QUARTER)
                g_b = _sc_gather_rows(ys_b, dest_row).reshape(TOP_K, piece, D_QUARTER)
                outs = _combine(xin, mods[l], wcol, g_a, g_b, *shared, *ln(l, 1), row0=row0 + s * piece,
                                split_streams=split_streams, prev=outs)
        return (outs["ctx"], outs["lat"]) if split_streams else outs["all"]

    x = _conv_mixer(x_prompt.reshape(T_CTX, D_MODEL), x_sample.reshape(T_LAT, D_MODEL), mods[0],
                    conv_w_in[0].astype(BF16), conv_k[0], conv_w_out[0].astype(BF16), *ln(0, 0))
    x = moe_layer(x, 0, split_streams=False)

    p = _mla_params(mla_w_dqkv[0], mla_q_norm[0], mla_w_uq[0], mla_kv_norm[0], mla_w_ukv[0])
    rope_a, rope_b = _rope_tables()
    q, k, v, ckv, kr = _mla_proj(x, mods[1], p, rope_a, rope_b)
    kr_cache = jnp.concatenate([cache_krope[:, 0].reshape(DEC_BATCH * PAST_LEN, QK_ROPE),
                                jnp.zeros((DEC_BATCH * PAST_LEN, QK_ROPE), F32)], axis=-1)
    kc, vc = _cache_kv(cache_ckv[:, 0].reshape(DEC_BATCH * PAST_LEN, KV_LORA), kr_cache, p)
    o_ctx = _ctx_attention(q, k, v)
    o_lat = _lat_attention(q, k, v, kc, vc)
    x = _oproj(o_ctx, o_lat, x, mods[1], mla_w_o[0].astype(BF16), *ln(1, 0))
    y_ctx, y_lat = moe_layer(x, 1, split_streams=True)

    y_prompt = y_ctx.reshape(BATCH, SEQ, D_MODEL)
    y_sample = y_lat.reshape(DEC_BATCH, DEC_SEQ, D_MODEL)
    state_ckv = ckv[:T_CTX].reshape(BATCH, 1, SEQ, KV_LORA)
    state_krope = kr[:T_CTX, :QK_ROPE].reshape(BATCH, 1, SEQ, QK_ROPE)
    return (y_prompt, y_sample, state_ckv, state_krope)
```

```python
import functools
import math

import jax
import jax.numpy as jnp
import numpy as np
from jax import lax
from jax.experimental import pallas as pl
from jax.experimental.pallas import tpu as pltpu
from jax.experimental.pallas import tpu_sc as plsc

D_MODEL = 1024
BATCH = 16
SEQ = 256
DEPTH = 2
DEC_BATCH = 4
DEC_SEQ = 4096
PAST_LEN = 512
GRID_W = 64

N_HEADS = 8
QK_NOPE = 128
QK_ROPE = 64
V_DIM = 128
Q_LORA = 384
KV_LORA = 256
ROPE_THETA = 10000.0
ATTN_SCALE = (QK_NOPE + QK_ROPE) ** -0.5
HEAD_PAD = 256
Q_PRESCALE = ATTN_SCALE * math.log2(math.e)

N_EXPERTS = 64
TOP_K = 8
N_GROUPS = 8
TOPK_GROUPS = 4
GROUP_SIZE = N_EXPERTS // N_GROUPS
D_EXPERT = 256
D_SHARED = 256
ROUTED_SCALE = 2.5

ALPHA = (2 * DEPTH) ** 0.25
LN_EPS = 1e-5
RMS_EPS = 1e-6

GROUP_ROWS = 4096
N_GROUPS_ROWS = 1 + DEC_BATCH
T_CTX = BATCH * SEQ
T_LAT = DEC_BATCH * DEC_SEQ
T_ALL = T_CTX + T_LAT
LANES = 128
SUBLANES = 8

F32 = jnp.float32
BF16 = jnp.bfloat16
NEG_INF = float("-inf")


def _dot(a, b):
    return jnp.dot(a, b, preferred_element_type=F32)


def _dot_nt(a, b, precision=None):
    return lax.dot_general(a, b, (((1,), (1,)), ((), ())), precision=precision,
                           preferred_element_type=F32)


def _layer_norm(v, g, b):
    mu = jnp.mean(v, axis=-1, keepdims=True)
    d = v - mu
    var = jnp.mean(d * d, axis=-1, keepdims=True)
    return d * lax.rsqrt(var + LN_EPS) * g + b


def _rms_norm(v, g):
    return v * lax.rsqrt(jnp.mean(v * v, axis=-1, keepdims=True) + RMS_EPS) * g


def _silu(v):
    return v / (1.0 + jnp.exp(-v))


def _sigmoid(v):
    return 1.0 / (1.0 + jnp.exp(-v))


def _adaln_kernel(c_ref, w_ref, b_ref, o_ref):
    c = c_ref[...]
    s = _silu(c)
    o_ref[0, 0] = jnp.dot(s, w_ref[0], precision=lax.Precision.HIGHEST,
                          preferred_element_type=F32) + b_ref[0]


def _adaln(cvecs, ada_w, ada_b):
    out = pl.pallas_call(
        _adaln_kernel,
        grid=(DEPTH, 6),
        in_specs=[
            pl.BlockSpec((SUBLANES, D_MODEL), lambda l, j: (0, 0)),
            pl.BlockSpec((1, D_MODEL, D_MODEL), lambda l, j: (l, 0, j)),
            pl.BlockSpec((1, 1, D_MODEL), lambda l, j: (l, 0, j)),
        ],
        out_specs=pl.BlockSpec((1, 1, SUBLANES, D_MODEL), lambda l, j: (l, j, 0, 0)),
        out_shape=jax.ShapeDtypeStruct((DEPTH, 6, SUBLANES, D_MODEL), F32),
        compiler_params=pltpu.CompilerParams(dimension_semantics=("parallel", "parallel")),
        name="adaln",
    )(cvecs, ada_w, ada_b.reshape(DEPTH, 1, 6 * D_MODEL))
    return jnp.transpose(out[:, :, :N_GROUPS_ROWS, :], (0, 2, 1, 3))


CONV_TM = 1024
CONV_PIECES = 2


def _conv_kernel(xc_ref, xcp_ref, xcn_ref, xl_ref, xlp_ref, xln_ref, mods_ref, win_ref, ck_ref, wout_ref,
                 g_ref, b_ref, o_ref):
    i = pl.program_id(0)
    tm = xc_ref.shape[0]
    is_ctx = i < T_CTX // tm
    m = mods_ref[0]
    sm, cm, gm = m[0:1], m[1:2], m[2:3]
    x = jnp.where(is_ctx, xc_ref[...], xl_ref[...])
    xp = jnp.where(is_ctx, xcp_ref[...], xlp_ref[...])
    xn = jnp.where(is_ctx, xcn_ref[...], xln_ref[...])
    def halo_u(xh):
        hh = (xh * (1.0 + cm) + sm).astype(BF16)
        zh = _dot(hh, win_ref[:, D_MODEL:])
        return zh[:, :D_MODEL] * zh[:, D_MODEL:]

    th = tm // CONV_PIECES
    halves = [x[j * th:(j + 1) * th] for j in range(CONV_PIECES)]
    zs = [_dot((xh * (1.0 + cm) + sm).astype(BF16), win_ref[...]) for xh in halves]
    us = [z[:, D_MODEL:2 * D_MODEL] * z[:, 2 * D_MODEL:] for z in zs]
    befores = [halo_u(xp)[SUBLANES - 1:SUBLANES]] + [u[th - 1:th] for u in us[:-1]]
    afters = [u[0:1] for u in us[1:]] + [halo_u(xn)[0:1]]

    ck = ck_ref[...]
    row = lax.broadcasted_iota(jnp.int32, (th, 1), 0)
    for hf in range(CONV_PIECES):
        grow = i * tm + hf * th + row
        seq_len = jnp.where(grow < T_CTX, SEQ, DEC_SEQ)
        pos = jnp.bitwise_and(grow, seq_len - 1)
        u = us[hf]
        left = jnp.where(row == 0, befores[hf], pltpu.roll(u, 1, 0))
        left = jnp.where(pos == 0, 0.0, left)
        right = jnp.where(row == th - 1, afters[hf], pltpu.roll(u, th - 1, 0))
        right = jnp.where(pos == seq_len - 1, 0.0, right)
        conv = left * ck[0:1] + u * ck[1:2] + right * ck[2:3]
        v = (zs[hf][:, :D_MODEL] * conv).astype(BF16)
        y = _dot(v, wout_ref[...])
        o_ref[hf * th:(hf + 1) * th, :] = _layer_norm(ALPHA * halves[hf] + gm * y, g_ref[...], b_ref[...])


def _conv_mixer(x_ctx, x_lat, mods, w_in, conv_k, w_out, ln_g, ln_b):
    tm = CONV_TM
    per8 = tm // SUBLANES
    n_ctx = T_CTX // tm
    n_lat = T_LAT // tm

    def stream_specs(first, n):
        def blk(i):
            return jnp.clip(i - first, 0, n - 1)
        return [
            pl.BlockSpec((tm, D_MODEL), lambda i: (blk(i), 0)),
            pl.BlockSpec((SUBLANES, D_MODEL), lambda i: (jnp.maximum(blk(i) * per8 - 1, 0), 0)),
            pl.BlockSpec((SUBLANES, D_MODEL), lambda i: (jnp.minimum((blk(i) + 1) * per8, n * per8 - 1), 0)),
        ]

    return pl.pallas_call(
        _conv_kernel,
        grid=(n_ctx + n_lat,),
        in_specs=stream_specs(0, n_ctx) + stream_specs(n_ctx, n_lat) + [
            pl.BlockSpec((1, 6, D_MODEL), lambda i: (i * tm // GROUP_ROWS, 0, 0)),
            pl.BlockSpec((D_MODEL, 3 * D_MODEL), lambda i: (0, 0)),
            pl.BlockSpec((3, D_MODEL), lambda i: (0, 0)),
            pl.BlockSpec((D_MODEL, D_MODEL), lambda i: (0, 0)),
            pl.BlockSpec((1, D_MODEL), lambda i: (0, 0)),
            pl.BlockSpec((1, D_MODEL), lambda i: (0, 0)),
        ],
        out_specs=pl.BlockSpec((tm, D_MODEL), lambda i: (i, 0)),
        out_shape=jax.ShapeDtypeStruct((T_ALL, D_MODEL), F32),
        compiler_params=pltpu.CompilerParams(dimension_semantics=("parallel",)),
        name="conv_mixer",
    )(x_ctx, x_ctx, x_ctx, x_lat, x_lat, x_lat, mods, w_in, conv_k, w_out, ln_g, ln_b)


ROUTER_TM = 512
MOE_CHUNKS = ((0, T_ALL),)
COMBINE_EIGHTHS = (1, 2, 2, 2, 1)


def _first_argmax_mask(cur, ridx, n):
    mx = jnp.max(cur, axis=0, keepdims=True)
    first = jnp.min(jnp.where(cur == mx, ridx, n), axis=0, keepdims=True)
    return ridx == first, mx


def _router_kernel(x_ref, mods_ref, rwt_ref, bias_ref, idx_ref, pos_ref, wcol_ref, count_ref, hpa_ref, hpb_ref,
                   carry_ref):
    tm = x_ref.shape[0]
    m = mods_ref[0]
    sf, cf = m[3:4], m[4:5]
    hc = x_ref[...] * (1.0 + cf) + sf
    words = _pack_rows(hc)
    hpa_ref[...] = words[:, :D_QUARTER]
    hpb_ref[...] = words[:, D_QUARTER:]
    logits = _dot_nt(rwt_ref[...], hc, precision=lax.Precision.HIGHEST)
    scores = _sigmoid(logits)
    biased = scores + bias_ref[...]

    ridx8 = lax.broadcasted_iota(jnp.int32, (GROUP_SIZE, tm), 0)
    gscore = jnp.full((N_GROUPS, tm), NEG_INF, F32)
    for g in range(N_GROUPS):
        blk = biased[g * GROUP_SIZE:(g + 1) * GROUP_SIZE]
        sel, m1 = _first_argmax_mask(blk, ridx8, GROUP_SIZE)
        m2 = jnp.max(jnp.where(sel, NEG_INF, blk), axis=0, keepdims=True)
        gscore = jnp.where(ridx8 == g, m1 + m2, gscore)

    gmask = jnp.zeros((N_GROUPS, tm), jnp.bool_)
    cur = gscore
    for _ in range(TOPK_GROUPS):
        sel, _unused = _first_argmax_mask(cur, ridx8, N_GROUPS)
        gmask = jnp.logical_or(gmask, sel)
        cur = jnp.where(sel, NEG_INF, cur)

    gmask_f = gmask.astype(F32)
    blocks = []
    for g in range(N_GROUPS):
        keep = jnp.broadcast_to(gmask_f[g:g + 1], (GROUP_SIZE, tm)) > 0.5
        blocks.append(jnp.where(keep, biased[g * GROUP_SIZE:(g + 1) * GROUP_SIZE], NEG_INF))
    cur = jnp.concatenate(blocks, axis=0)

    first_tiles = [row0 // tm for row0, _rows in MOE_CHUNKS]
    starts_chunk = functools.reduce(jnp.logical_or, [pl.program_id(0) == ft for ft in first_tiles])

    @pl.when(starts_chunk)
    def _():
        carry_ref[...] = jnp.zeros(carry_ref.shape, F32)

    ridx = lax.broadcasted_iota(jnp.int32, (N_EXPERTS, tm), 0)
    kidx = lax.broadcasted_iota(jnp.int32, (TOP_K, tm), 0)
    sels = []
    chosen = jnp.zeros((N_EXPERTS, tm), jnp.bool_)
    idx_rows = jnp.zeros((TOP_K, tm), jnp.int32)
    for k in range(TOP_K):
        mx = jnp.max(cur, axis=0, keepdims=True)
        first = jnp.min(jnp.where(cur == mx, ridx, N_EXPERTS), axis=0, keepdims=True)
        sel = ridx == first
        sels.append(sel)
        chosen = jnp.logical_or(chosen, sel)
        idx_rows = jnp.where(kidx == k, first, idx_rows)
        cur = jnp.where(sel, NEG_INF, cur)

    onehot = chosen.astype(F32)
    t_row = lax.broadcasted_iota(jnp.int32, (tm, tm), 0)
    t_col = lax.broadcasted_iota(jnp.int32, (tm, tm), 1)
    before = (t_row < t_col).astype(BF16)
    rank = carry_ref[...] + _dot(onehot.astype(BF16), before)
    carry_ref[...] = carry_ref[...] + jnp.sum(onehot, axis=1, keepdims=True)
    count_ref[0] = jnp.broadcast_to(carry_ref[...], count_ref.shape[1:])

    w = jnp.where(chosen, scores, 0.0)
    w = w / jnp.sum(w, axis=0, keepdims=True) * ROUTED_SCALE
    pos_rows = jnp.zeros((TOP_K, tm), F32)
    w_rows = jnp.zeros((TOP_K, tm), F32)
    for k in range(TOP_K):
        pos_rows = jnp.where(kidx == k, jnp.sum(jnp.where(sels[k], rank, 0.0), axis=0, keepdims=True), pos_rows)
        w_rows = jnp.where(kidx == k, jnp.sum(jnp.where(sels[k], w, 0.0), axis=0, keepdims=True), w_rows)
    idx_ref[...] = idx_rows
    pos_ref[...] = pos_rows.astype(jnp.int32)
    wpad = jnp.concatenate([w_rows, jnp.zeros((LANES - TOP_K, tm), F32)], axis=0)
    wcol_ref[...] = wpad.T


def _router(x, mods, router_wt, router_bias):
    t = x.shape[0]
    tm = ROUTER_TM

    def chunk_of(i):
        return sum((i >= row0 // tm).astype(jnp.int32) for row0, _rows in MOE_CHUNKS[1:])

    return pl.pallas_call(
        _router_kernel,
        grid=(t // tm,),
        in_specs=[
            pl.BlockSpec((tm, D_MODEL), lambda i: (i, 0)),
            pl.BlockSpec((1, 6, D_MODEL), lambda i: (i * tm // GROUP_ROWS, 0, 0)),
            pl.BlockSpec((N_EXPERTS, D_MODEL), lambda i: (0, 0)),
            pl.BlockSpec((N_EXPERTS, 1), lambda i: (0, 0)),
        ],
        out_specs=[
            pl.BlockSpec((TOP_K, tm), lambda i: (0, i)),
            pl.BlockSpec((TOP_K, tm), lambda i: (0, i)),
            pl.BlockSpec((tm, LANES), lambda i: (i, 0)),
            pl.BlockSpec((1, N_EXPERTS, LANES), lambda i: (chunk_of(i), 0, 0)),
            pl.BlockSpec((tm, D_QUARTER), lambda i: (i, 0)),
            pl.BlockSpec((tm, D_QUARTER), lambda i: (i, 0)),
        ],
        out_shape=[
            jax.ShapeDtypeStruct((TOP_K, t), jnp.int32),
            jax.ShapeDtypeStruct((TOP_K, t), jnp.int32),
            jax.ShapeDtypeStruct((t, LANES), F32),
            jax.ShapeDtypeStruct((len(MOE_CHUNKS), N_EXPERTS, LANES), F32),
            jax.ShapeDtypeStruct((t, D_QUARTER), jnp.uint32),
            jax.ShapeDtypeStruct((t, D_QUARTER), jnp.uint32),
        ],
        scratch_shapes=[pltpu.VMEM((N_EXPERTS, 1), F32)],
        compiler_params=pltpu.CompilerParams(dimension_semantics=("arbitrary",)),
        name="moe_router",
    )(x, mods, router_wt, router_bias)


FFN_TM = 512


def _n_tiles(rows):
    return rows * TOP_K // FFN_TM + N_EXPERTS


def _expert_layout(counts, n_tiles):
    padded = (counts + FFN_TM - 1) // FFN_TM * FFN_TM
    end = jnp.cumsum(padded)
    start = end - padded
    tile_row = jnp.arange(n_tiles, dtype=jnp.int32) * FFN_TM
    tile_expert = jnp.minimum(jnp.sum(end[None, :] <= tile_row[:, None], axis=1), N_EXPERTS - 1)
    of_tile = tile_expert[:, None] == jnp.arange(N_EXPERTS, dtype=jnp.int32)[None, :]
    live_end = jnp.sum(jnp.where(of_tile, (start + counts)[None, :], 0), axis=1)
    tile_valid = jnp.clip(live_end - tile_row, 0, FFN_TM)
    n_used = (end[-1] // FFN_TM).astype(jnp.int32).reshape(1)
    used = jnp.arange(n_tiles, dtype=jnp.int32) < n_used[0]
    prev_expert = jnp.concatenate([jnp.full((1,), -1, tile_expert.dtype), tile_expert[:-1]])
    run_flag = jnp.logical_and(used, tile_expert != prev_expert)
    run_slot = (jnp.cumsum(run_flag.astype(jnp.int32)) - 1) % 2
    later = jnp.logical_and(used[None, :], tile_expert[None, :] > tile_expert[:, None])
    next_expert = jnp.min(jnp.where(later, tile_expert[None, :], N_EXPERTS), axis=1)
    next_expert = jnp.where(next_expert == N_EXPERTS, -1, next_expert)
    plan = (run_flag.astype(jnp.int32), run_slot.astype(jnp.int32), next_expert.astype(jnp.int32))
    return start.astype(jnp.int32), tile_expert.astype(jnp.int32), tile_valid.astype(jnp.int32), n_used, plan


D_HALF = D_MODEL // 2
D_QUARTER = D_MODEL // 4


def _pack_rows(v):
    hi = lax.bitcast_convert_type(v[:, :D_HALF].astype(BF16).astype(F32), jnp.uint32)
    lo = lax.bitcast_convert_type(v[:, D_HALF:].astype(BF16).astype(F32), jnp.uint32)
    return jnp.bitwise_or(hi, jnp.right_shift(lo, jnp.uint32(16)))


def _unpack_rows(w):
    hi = lax.bitcast_convert_type(jnp.bitwise_and(w, jnp.uint32(0xFFFF0000)), F32)
    lo = lax.bitcast_convert_type(jnp.left_shift(w, jnp.uint32(16)), F32)
    return hi, lo


DEST_TM = 2048


def _dest_kernel(start_ref, idx_ref, pos_ref, dest_ref):
    idx = idx_ref[...]
    base = jnp.zeros(idx.shape, jnp.int32)
    for e in range(N_EXPERTS):
        base = jnp.where(idx == e, start_ref[e], base)
    dest_ref[...] = base + pos_ref[...]


def _dest_rows(start, idx, pos):
    t = idx.shape[1]
    return pl.pallas_call(
        _dest_kernel,
        grid_spec=pltpu.PrefetchScalarGridSpec(
            num_scalar_prefetch=1,
            grid=(t // DEST_TM,),
            in_specs=[pl.BlockSpec((TOP_K, DEST_TM), lambda i, s: (0, i)),
                      pl.BlockSpec((TOP_K, DEST_TM), lambda i, s: (0, i))],
            out_specs=pl.BlockSpec((TOP_K, DEST_TM), lambda i, s: (0, i)),
        ),
        out_shape=jax.ShapeDtypeStruct((TOP_K, t), jnp.int32),
        compiler_params=pltpu.CompilerParams(dimension_semantics=("parallel",)),
        name="moe_dest",
    )(start, idx, pos)


SC_WINDOW = 128


def _sc_mesh():
    return plsc.VectorSubcoreMesh(core_axis_name="c", subcore_axis_name="s")


def _sc_scatter_rows(x, dest, row0, n_sorted):
    t = dest.shape[1]
    blk0 = row0 // SC_WINDOW

    @functools.partial(
        pl.kernel,
        out_type=jax.ShapeDtypeStruct((n_sorted, D_QUARTER), x.dtype),
        mesh=_sc_mesh(),
        scratch_types=[],
    )
    def scatter(x_hbm, i_hbm, o_hbm):
        def body(x_vmem, i_vmem):
            pltpu.sync_copy(x_vmem, o_hbm.at[i_vmem.at[0]])

        pltpu.emit_pipeline(
            body,
            grid=(t // SC_WINDOW, TOP_K),
            in_specs=[pl.BlockSpec((SC_WINDOW, D_QUARTER), lambda i, k: (i + blk0, 0)),
                      pl.BlockSpec((1, SC_WINDOW), lambda i, k: (k, i))],
            out_specs=[],
            core_axis_name=("c", "s"),
            dimension_semantics=(pltpu.PARALLEL, pltpu.ARBITRARY),
        )(x_hbm, i_hbm)

    return scatter(x, dest)


def _sc_gather_rows(table, idx):
    m = idx.shape[1]

    @functools.partial(
        pl.kernel,
        out_type=jax.ShapeDtypeStruct((m, D_QUARTER), table.dtype),
        mesh=_sc_mesh(),
        scratch_types=[],
    )
    def gather(t_hbm, i_hbm, o_hbm):
        def body(i_vmem, o_vmem):
            pltpu.sync_copy(t_hbm.at[i_vmem.at[0]], o_vmem)

        pltpu.emit_pipeline(
            body,
            grid=(m // SC_WINDOW,),
            in_specs=[pl.BlockSpec((1, SC_WINDOW), lambda i: (0, i))],
            out_specs=[pl.BlockSpec((SC_WINDOW, D_QUARTER), lambda i: (i, 0))],
            core_axis_name=("c", "s"),
            dimension_semantics=(pltpu.PARALLEL,),
        )(i_hbm, o_hbm)

    return gather(table, idx)


FFN_STEP_TILES = 2
FFN_PIECES = 1


def _ffn_kernel(layer, te_ref, tv_ref, nu_ref, flag_ref, slot_ref, next_ref, xa_ref, xb_ref,
                wg_hbm, wu_hbm, wd_hbm, ya_ref, yb_ref, wg_st, wu_st, wd_st, wgb_ref, wub_ref, wdb_ref, sem):
    step = pl.program_id(0)

    def fetch(expert, slot):
        return [pltpu.make_async_copy(src.at[layer, expert], dst.at[slot], sem.at[slot])
                for src, dst in ((wg_hbm, wg_st), (wu_hbm, wu_st), (wd_hbm, wd_st))]

    for s in range(FFN_STEP_TILES):
        tile = step * FFN_STEP_TILES + s
        active = tile < nu_ref[0]
        tile_rows = s * FFN_TM

        @pl.when(jnp.logical_and(active, flag_ref[tile] == 1))
        def _():
            slot = slot_ref[tile]

            @pl.when(tile == 0)
            def _():
                for cp in fetch(te_ref[tile], slot):
                    cp.start()

            for cp in fetch(te_ref[tile], slot):
                cp.wait()
            wgb_ref[...] = wg_st[slot].astype(BF16)
            wub_ref[...] = wu_st[slot].astype(BF16)
            wdb_ref[...] = wd_st[slot].astype(BF16)

            @pl.when(next_ref[tile] >= 0)
            def _():
                for cp in fetch(next_ref[tile], 1 - slot):
                    cp.start()

        @pl.when(active)
        def _():
            wg = wgb_ref[...]
            wu = wub_ref[...]
            wd = wdb_ref[...]
            th = FFN_TM // FFN_PIECES
            for r0 in range(0, FFN_TM, th):
                rows = pl.ds(tile_rows + r0, th)
                live = (lax.broadcasted_iota(jnp.int32, (th, D_QUARTER), 0) + r0) < tv_ref[tile]
                hi_a, lo_a = _unpack_rows(jnp.where(live, xa_ref[rows, :], jnp.uint32(0)))
                hi_b, lo_b = _unpack_rows(jnp.where(live, xb_ref[rows, :], jnp.uint32(0)))
                xb = jnp.concatenate([hi_a.astype(BF16), hi_b.astype(BF16), lo_a.astype(BF16), lo_b.astype(BF16)],
                                     axis=1)
                a = (_silu(_dot(xb, wg)) * _dot(xb, wu)).astype(BF16)
                words = _pack_rows(_dot(a, wd))
                ya_ref[rows, :] = words[:, :D_QUARTER]
                yb_ref[rows, :] = words[:, D_QUARTER:]

        @pl.when(jnp.logical_not(active))
        def _():
            rows = pl.ds(tile_rows, FFN_TM)
            ya_ref[rows, :] = jnp.zeros((FFN_TM, D_QUARTER), jnp.uint32)
            yb_ref[rows, :] = jnp.zeros((FFN_TM, D_QUARTER), jnp.uint32)


def _expert_ffn(xs_a, xs_b, tile_expert, tile_valid, n_used, plan, wg, wu, wd, layer):
    n_tiles = xs_a.shape[0] // FFN_TM
    step_rows = FFN_STEP_TILES * FFN_TM

    def row_map(i, te, tv, nu, fl, sl, nx):
        return (jnp.minimum(i, (nu[0] - 1) // FFN_STEP_TILES), 0)

    def out_map(i, te, tv, nu, fl, sl, nx):
        return (i, 0)

    hbm = pl.BlockSpec(memory_space=pl.ANY)
    return pl.pallas_call(
        functools.partial(_ffn_kernel, layer),
        grid_spec=pltpu.PrefetchScalarGridSpec(
            num_scalar_prefetch=6,
            grid=(n_tiles // FFN_STEP_TILES,),
            in_specs=[pl.BlockSpec((step_rows, D_QUARTER), row_map),
                      pl.BlockSpec((step_rows, D_QUARTER), row_map), hbm, hbm, hbm],
            out_specs=[pl.BlockSpec((step_rows, D_QUARTER), out_map), pl.BlockSpec((step_rows, D_QUARTER), out_map)],
            scratch_shapes=[pltpu.VMEM((2, D_MODEL, D_EXPERT), F32), pltpu.VMEM((2, D_MODEL, D_EXPERT), F32),
                            pltpu.VMEM((2, D_EXPERT, D_MODEL), F32),
                            pltpu.VMEM((D_MODEL, D_EXPERT), BF16), pltpu.VMEM((D_MODEL, D_EXPERT), BF16),
                            pltpu.VMEM((D_EXPERT, D_MODEL), BF16), pltpu.SemaphoreType.DMA((2,))],
        ),
        out_shape=[jax.ShapeDtypeStruct((n_tiles * FFN_TM, D_QUARTER), jnp.uint32),
                   jax.ShapeDtypeStruct((n_tiles * FFN_TM, D_QUARTER), jnp.uint32)],
        compiler_params=pltpu.CompilerParams(dimension_semantics=("arbitrary",)),
        name="moe_expert_ffn",
    )(tile_expert, tile_valid, n_used, *plan, xs_a, xs_b, wg, wu, wd)


COMBINE_TM = 256


def _combine_kernel(n_prev, tile0, x_ref, mods_ref, wcol_ref, ga_ref, gb_ref, sg_ref, su_ref, sd_ref, g_ref, b_ref,
                    *rest):
    o_refs = rest[n_prev:]
    m = mods_ref[0]
    sf, cf, gf = m[3:4], m[4:5], m[5:6]
    x = x_ref[...]
    hc = (x * (1.0 + cf) + sf).astype(BF16)
    a = _silu(_dot(hc, sg_ref[...])) * _dot(hc, su_ref[...])
    y = _dot(a.astype(BF16), sd_ref[...])
    wcol = wcol_ref[...]
    parts = [y[:, q * D_QUARTER:(q + 1) * D_QUARTER] for q in range(4)]
    for k in range(TOP_K):
        hi_a, lo_a = _unpack_rows(ga_ref[k])
        hi_b, lo_b = _unpack_rows(gb_ref[k])
        wk = wcol[:, k:k + 1]
        parts = [parts[0] + wk * hi_a, parts[1] + wk * hi_b, parts[2] + wk * lo_a, parts[3] + wk * lo_b]
    y = jnp.concatenate(parts, axis=1)
    out = _layer_norm(ALPHA * x + gf * y, g_ref[...], b_ref[...])
    if len(o_refs) == 1:
        o_refs[0][...] = out
    else:
        is_ctx = pl.program_id(0) + tile0 < T_CTX // x_ref.shape[0]

        @pl.when(is_ctx)
        def _():
            o_refs[0][...] = out

        @pl.when(jnp.logical_not(is_ctx))
        def _():
            o_refs[1][...] = out


def _combine(x, mods, wcol, g_a, g_b, sg, su, sd, ln_g, ln_b, row0, split_streams, prev):
    tm = COMBINE_TM
    n = g_a.shape[1] // tm
    tile0 = row0 // tm
    n_ctx = T_CTX // tm
    full = lambda shape: pl.BlockSpec(shape, lambda i: (0,) * len(shape))
    layouts = {
        "all": (lambda i: (tile0 + i, 0), T_ALL),
        "ctx": (lambda i: (jnp.minimum(tile0 + i, n_ctx - 1), 0), T_CTX),
        "lat": (lambda i: (jnp.maximum(tile0 + i - n_ctx, 0), 0), T_LAT),
    }
    if split_streams:
        kinds = (["ctx"] if tile0 < n_ctx else []) + (["lat"] if tile0 + n > n_ctx else [])
    else:
        kinds = ["all"]
    carried = [kd for kd in kinds if kd in prev]
    outs = pl.pallas_call(
        functools.partial(_combine_kernel, len(carried), tile0),
        grid=(n,),
        in_specs=[
            pl.BlockSpec((tm, D_MODEL), lambda i: (tile0 + i, 0)),
            pl.BlockSpec((1, 6, D_MODEL), lambda i: ((tile0 + i) * tm // GROUP_ROWS, 0, 0)),
            pl.BlockSpec((tm, LANES), lambda i: (tile0 + i, 0)),
            pl.BlockSpec((TOP_K, tm, D_QUARTER), lambda i: (0, i, 0)),
            pl.BlockSpec((TOP_K, tm, D_QUARTER), lambda i: (0, i, 0)),
            full((D_MODEL, D_SHARED)), full((D_MODEL, D_SHARED)), full((D_SHARED, D_MODEL)),
            full((1, D_MODEL)), full((1, D_MODEL)),
        ] + [pl.BlockSpec(memory_space=pl.ANY) for _ in carried],
        out_specs=[pl.BlockSpec((tm, D_MODEL), layouts[kd][0]) for kd in kinds],
        out_shape=[jax.ShapeDtypeStruct((layouts[kd][1], D_MODEL), F32) for kd in kinds],
        input_output_aliases={10 + j: kinds.index(kd) for j, kd in enumerate(carried)},
        compiler_params=pltpu.CompilerParams(dimension_semantics=("arbitrary",)),
        name="moe_combine",
    )(x, mods, wcol, g_a, g_b, sg, su, sd, ln_g, ln_b, *[prev[kd] for kd in carried])
    return {**prev, **dict(zip(kinds, outs))}


MLA_TM = 1024


def _mla_proj_kernel(x_ref, mods_ref, wdq_ref, wdkv_ref, wkr_ref, qn_ref, kvn_ref, wuq_ref,
                     wukn_ref, wuv_ref, ta_ref, tb_ref, q_ref, k_ref, v_ref, ckv_ref, kr_ref):
    m = mods_ref[0]
    sm, cm = m[0:1], m[1:2]
    h = (x_ref[...] * (1.0 + cm) + sm).astype(BF16)
    cq = _rms_norm(_dot(h, wdq_ref[...]), qn_ref[...])
    ckv = _rms_norm(_dot(h, wdkv_ref[...]), kvn_ref[...])
    kr2 = _dot(h, wkr_ref[...])
    ckv_ref[...] = ckv
    kr_ref[...] = kr2

    ka = ta_ref[0]
    kb = tb_ref[0]
    tm = ka.shape[0]
    ta = jnp.concatenate([jnp.full((tm, QK_NOPE), Q_PRESCALE, F32), ka * Q_PRESCALE], axis=1)
    tb = jnp.concatenate([jnp.zeros((tm, QK_NOPE), F32), kb * Q_PRESCALE], axis=1)
    krr = kr2 * ka + pltpu.roll(kr2, QK_ROPE, 1) * kb

    qpre = _dot(cq.astype(BF16), wuq_ref[...])
    ckv_b = ckv.astype(BF16)
    kn = _dot(ckv_b, wukn_ref[...])
    v_ref[...] = _dot(ckv_b, wuv_ref[...]).astype(BF16)
    for hd in range(N_HEADS):
        qh = qpre[:, hd * HEAD_PAD:(hd + 1) * HEAD_PAD]
        qrot = qh * ta + pltpu.roll(qh, HEAD_PAD - QK_ROPE, 1) * tb
        q_ref[:, hd * HEAD_PAD:(hd + 1) * HEAD_PAD] = qrot.astype(BF16)
        k_ref[:, hd * HEAD_PAD:hd * HEAD_PAD + QK_NOPE] = kn[:, hd * QK_NOPE:(hd + 1) * QK_NOPE].astype(BF16)
        k_ref[:, hd * HEAD_PAD + QK_NOPE:(hd + 1) * HEAD_PAD] = krr.astype(BF16)


def _mla_proj(x, mods, p, rope_a, rope_b):
    t = x.shape[0]
    tm = MLA_TM
    full = lambda shape: pl.BlockSpec(shape, lambda i: (0,) * len(shape))
    rope_spec = pl.BlockSpec(
        (1, tm, 2 * QK_ROPE),
        lambda i: (jnp.minimum(i * tm // GROUP_ROWS, 1), (i * tm % GROUP_ROWS) // tm, 0))
    return pl.pallas_call(
        _mla_proj_kernel,
        grid=(t // tm,),
        in_specs=[
            pl.BlockSpec((tm, D_MODEL), lambda i: (i, 0)),
            pl.BlockSpec((1, 6, D_MODEL), lambda i: (i * tm // GROUP_ROWS, 0, 0)),
            full((D_MODEL, Q_LORA)), full((D_MODEL, KV_LORA)), full((D_MODEL, 2 * QK_ROPE)),
            full((1, Q_LORA)), full((1, KV_LORA)),
            full((Q_LORA, N_HEADS * HEAD_PAD)),
            full((KV_LORA, N_HEADS * QK_NOPE)), full((KV_LORA, N_HEADS * V_DIM)),
            rope_spec, rope_spec,
        ],
        out_specs=[
            pl.BlockSpec((tm, N_HEADS * HEAD_PAD), lambda i: (i, 0)),
            pl.BlockSpec((tm, N_HEADS * HEAD_PAD), lambda i: (i, 0)),
            pl.BlockSpec((tm, N_HEADS * V_DIM), lambda i: (i, 0)),
            pl.BlockSpec((tm, KV_LORA), lambda i: (i, 0)),
            pl.BlockSpec((tm, 2 * QK_ROPE), lambda i: (i, 0)),
        ],
        out_shape=[
            jax.ShapeDtypeStruct((t, N_HEADS * HEAD_PAD), BF16),
            jax.ShapeDtypeStruct((t, N_HEADS * HEAD_PAD), BF16),
            jax.ShapeDtypeStruct((t, N_HEADS * V_DIM), BF16),
            jax.ShapeDtypeStruct((t, KV_LORA), F32),
            jax.ShapeDtypeStruct((t, 2 * QK_ROPE), F32),
        ],
        compiler_params=pltpu.CompilerParams(dimension_semantics=("parallel",)),
        name="mla_proj",
    )(x, mods, p["w_dq"], p["w_dkv"], p["w_kr"], p["q_norm"], p["kv_norm"], p["w_uq"],
      p["w_ukn"], p["w_uv"], rope_a, rope_b)


def _cache_kv_kernel(ckv_ref, kr_ref, wukn_ref, wuv_ref, k_ref, v_ref):
    ckv_b = ckv_ref[...].astype(BF16)
    kn = _dot(ckv_b, wukn_ref[...])
    v_ref[...] = _dot(ckv_b, wuv_ref[...]).astype(BF16)
    kr = kr_ref[...].astype(BF16)
    for hd in range(N_HEADS):
        k_ref[:, hd * HEAD_PAD:hd * HEAD_PAD + QK_NOPE] = kn[:, hd * QK_NOPE:(hd + 1) * QK_NOPE].astype(BF16)
        k_ref[:, hd * HEAD_PAD + QK_NOPE:(hd + 1) * HEAD_PAD] = kr


def _cache_kv(ckv, kr_pad, p):
    t = ckv.shape[0]
    tm = PAST_LEN
    full = lambda shape: pl.BlockSpec(shape, lambda i: (0,) * len(shape))
    return pl.pallas_call(
        _cache_kv_kernel,
        grid=(t // tm,),
        in_specs=[
            pl.BlockSpec((tm, KV_LORA), lambda i: (i, 0)),
            pl.BlockSpec((tm, 2 * QK_ROPE), lambda i: (i, 0)),
            full((KV_LORA, N_HEADS * QK_NOPE)), full((KV_LORA, N_HEADS * V_DIM)),
        ],
        out_specs=[
            pl.BlockSpec((tm, N_HEADS * HEAD_PAD), lambda i: (i, 0)),
            pl.BlockSpec((tm, N_HEADS * V_DIM), lambda i: (i, 0)),
        ],
        out_shape=[
            jax.ShapeDtypeStruct((t, N_HEADS * HEAD_PAD), BF16),
            jax.ShapeDtypeStruct((t, N_HEADS * V_DIM), BF16),
        ],
        compiler_params=pltpu.CompilerParams(dimension_semantics=("parallel",)),
        name="mla_cache_kv",
    )(ckv, kr_pad, p["w_ukn"], p["w_uv"])


def _ctx_attn_kernel(q_ref, k_ref, v_ref, o_ref):
    for hd in range(N_HEADS):
        q = q_ref[:, hd * HEAD_PAD:(hd + 1) * HEAD_PAD]
        k = k_ref[:, hd * HEAD_PAD:(hd + 1) * HEAD_PAD]
        s = _dot_nt(q, k)
        s = s - jnp.max(s, axis=-1, keepdims=True)
        p = jnp.exp2(s)
        p = p / jnp.sum(p, axis=-1, keepdims=True)
        o = _dot(p.astype(BF16), v_ref[:, hd * V_DIM:(hd + 1) * V_DIM])
        o_ref[:, hd * V_DIM:(hd + 1) * V_DIM] = o.astype(BF16)


def _ctx_attention(q, k, v):
    return pl.pallas_call(
        _ctx_attn_kernel,
        grid=(BATCH,),
        in_specs=[
            pl.BlockSpec((SEQ, N_HEADS * HEAD_PAD), lambda b: (b, 0)),
            pl.BlockSpec((SEQ, N_HEADS * HEAD_PAD), lambda b: (b, 0)),
            pl.BlockSpec((SEQ, N_HEADS * V_DIM), lambda b: (b, 0)),
        ],
        out_specs=pl.BlockSpec((SEQ, N_HEADS * V_DIM), lambda b: (b, 0)),
        out_shape=jax.ShapeDtypeStruct((T_CTX, N_HEADS * V_DIM), BF16),
        compiler_params=pltpu.CompilerParams(dimension_semantics=("parallel",)),
        name="ctx_attention",
    )(q, k, v)


LAT_TQ = 1024
LAT_TK = 512
LAT_PIECES = 4


def _lat_attn_kernel(q_ref, k_ref, v_ref, kc_ref, vc_ref, o_ref, s_ref, p_ref, m_ref):
    n_chunks = DEC_SEQ // LAT_TK + 1
    tp = q_ref.shape[0] // LAT_PIECES
    pieces = [pl.ds(j * tp, tp) for j in range(LAT_PIECES)]
    groups = [pieces[:2], pieces[2:]]
    state = {}

    def keys(c):
        return kc_ref[...] if c == n_chunks - 1 else k_ref[c * LAT_TK:(c + 1) * LAT_TK, :]

    def values(c):
        return vc_ref[...] if c == n_chunks - 1 else v_ref[c * LAT_TK:(c + 1) * LAT_TK, :]

    def qk(r, c):
        s = _dot_nt(q_ref[r, :], keys(c))
        s_ref[r, c * LAT_TK:(c + 1) * LAT_TK] = s
        mp = state.get(("m", r.start), jnp.full((tp, LANES), NEG_INF, F32))
        for j in range(LAT_TK // LANES):
            mp = jnp.maximum(mp, s[:, j * LANES:(j + 1) * LANES])
        state[("m", r.start)] = mp

    def row_max(r):
        m_ref[r, :] = jnp.broadcast_to(jnp.max(state[("m", r.start)], axis=-1, keepdims=True), (tp, LANES))

    def exp_chunk(r, c, after=None):
        lp = state.get(("l", r.start), jnp.zeros((tp, LANES), F32))
        m = m_ref[r, :]
        if after is not None:
            bits = lax.bitcast_convert_type(after, jnp.uint32)
            zero = lax.shift_right_logical(lax.shift_right_logical(bits, jnp.uint32(16)), jnp.uint32(16))
            m = m + lax.bitcast_convert_type(zero, F32)
        for j in range(c * LAT_TK // LANES, (c + 1) * LAT_TK // LANES):
            p = jnp.exp2(s_ref[r, j * LANES:(j + 1) * LANES] - m)
            lp = lp + p
            p_ref[r, j * LANES:(j + 1) * LANES] = p.astype(BF16)
        state[("l", r.start)] = lp

    def pv(r, c):
        acc = state.get(("a", r.start), jnp.zeros((tp, V_DIM), F32))
        state[("a", r.start)] = acc + _dot(p_ref[r, c * LAT_TK:(c + 1) * LAT_TK], values(c))

    def finish(r):
        o_ref[r, :] = (state[("a", r.start)] / jnp.sum(state[("l", r.start)], axis=-1, keepdims=True)).astype(BF16)

    for c in range(n_chunks):
        for r in groups[0]:
            qk(r, c)
    for r in groups[0]:
        row_max(r)
    for c in range(n_chunks):
        for r in groups[1]:
            qk(r, c)
        for r in groups[0]:
            exp_chunk(r, c)
    for r in groups[1]:
        row_max(r)
    for c in range(n_chunks):
        for r in groups[0]:
            pv(r, c)
        for r0, r in zip(groups[0], groups[1]):
            exp_chunk(r, c, after=state[("a", r0.start)])
    for r in groups[0]:
        finish(r)
    for c in range(n_chunks):
        for r in groups[1]:
            pv(r, c)
    for r in groups[1]:
        finish(r)


def _lat_attention(q, k, v, kc, vc):
    nq = DEC_SEQ // LAT_TQ
    return pl.pallas_call(
        _lat_attn_kernel,
        grid=(DEC_BATCH, N_HEADS, nq),
        in_specs=[
            pl.BlockSpec((LAT_TQ, HEAD_PAD), lambda b, h, i: ((b + 1) * nq + i, h)),
            pl.BlockSpec((DEC_SEQ, HEAD_PAD), lambda b, h, i: (b + 1, h)),
            pl.BlockSpec((DEC_SEQ, V_DIM), lambda b, h, i: (b + 1, h)),
            pl.BlockSpec((PAST_LEN, HEAD_PAD), lambda b, h, i: (b, h)),
            pl.BlockSpec((PAST_LEN, V_DIM), lambda b, h, i: (b, h)),
        ],
        out_specs=pl.BlockSpec((LAT_TQ, V_DIM), lambda b, h, i: (b * nq + i, h)),
        out_shape=jax.ShapeDtypeStruct((T_LAT, N_HEADS * V_DIM), BF16),
        scratch_shapes=[pltpu.VMEM((LAT_TQ, DEC_SEQ + PAST_LEN), F32),
                        pltpu.VMEM((LAT_TQ, DEC_SEQ + PAST_LEN), BF16),
                        pltpu.VMEM((LAT_TQ, LANES), F32)],
        compiler_params=pltpu.CompilerParams(
            dimension_semantics=("parallel", "parallel", "parallel")),
        name="lat_attention",
    )(q, k, v, kc, vc)


OPROJ_TM = 1024


def _oproj_kernel(oc_ref, ol_ref, x_ref, mods_ref, wo_ref, g_ref, b_ref, out_ref):
    is_ctx = pl.program_id(0) < T_CTX // OPROJ_TM
    o = jnp.where(is_ctx, oc_ref[...], ol_ref[...])
    gm = mods_ref[0][2:3]
    y = _dot(o, wo_ref[...])
    out_ref[...] = _layer_norm(ALPHA * x_ref[...] + gm * y, g_ref[...], b_ref[...])


def _oproj(o_ctx, o_lat, x, mods, w_o, ln_g, ln_b):
    t = x.shape[0]
    tm = OPROJ_TM
    n_ctx = T_CTX // tm
    return pl.pallas_call(
        _oproj_kernel,
        grid=(t // tm,),
        in_specs=[
            pl.BlockSpec((tm, N_HEADS * V_DIM), lambda i: (jnp.minimum(i, n_ctx - 1), 0)),
            pl.BlockSpec((tm, N_HEADS * V_DIM), lambda i: (jnp.maximum(i - n_ctx, 0), 0)),
            pl.BlockSpec((tm, D_MODEL), lambda i: (i, 0)),
            pl.BlockSpec((1, 6, D_MODEL), lambda i: (i * tm // GROUP_ROWS, 0, 0)),
            pl.BlockSpec((N_HEADS * V_DIM, D_MODEL), lambda i: (0, 0)),
            pl.BlockSpec((1, D_MODEL), lambda i: (0, 0)),
            pl.BlockSpec((1, D_MODEL), lambda i: (0, 0)),
        ],
        out_specs=pl.BlockSpec((tm, D_MODEL), lambda i: (i, 0)),
        out_shape=jax.ShapeDtypeStruct((t, D_MODEL), F32),
        compiler_params=pltpu.CompilerParams(dimension_semantics=("parallel",)),
        name="attn_oproj",
    )(o_ctx, o_lat, x, mods, w_o, ln_g, ln_b)


def _swap16(w):
    q = QK_ROPE // 4
    return jnp.concatenate([w[..., q:2 * q], w[..., :q], w[..., 3 * q:], w[..., 2 * q:3 * q]], axis=-1)


def _mla_params(w_dqkv, q_norm, w_uq, kv_norm, w_ukv):
    w_kr = w_dqkv[:, Q_LORA + KV_LORA:]
    wq = w_uq.reshape(Q_LORA, N_HEADS, QK_NOPE + QK_ROPE)
    wq_r = wq[..., QK_NOPE:]
    wq = jnp.concatenate([wq[..., :QK_NOPE], wq_r, _swap16(wq_r)], axis=-1)
    wkv = w_ukv.reshape(KV_LORA, N_HEADS, QK_NOPE + V_DIM)
    return {
        "w_dq": w_dqkv[:, :Q_LORA].astype(BF16),
        "w_dkv": w_dqkv[:, Q_LORA:Q_LORA + KV_LORA].astype(BF16),
        "w_kr": jnp.concatenate([w_kr, _swap16(w_kr)], axis=-1).astype(BF16),
        "q_norm": q_norm.reshape(1, Q_LORA),
        "kv_norm": kv_norm.reshape(1, KV_LORA),
        "w_uq": wq.reshape(Q_LORA, N_HEADS * HEAD_PAD).astype(BF16),
        "w_ukn": wkv[..., :QK_NOPE].reshape(KV_LORA, N_HEADS * QK_NOPE).astype(BF16),
        "w_uv": wkv[..., QK_NOPE:].reshape(KV_LORA, N_HEADS * V_DIM).astype(BF16),
    }


def _rope_tables():
    nf = QK_ROPE // 4
    t = np.arange(DEC_SEQ)
    row = (t // GRID_W).astype(np.float32)
    col = (t % GRID_W).astype(np.float32)
    inv = (ROPE_THETA ** (-np.arange(nf, dtype=np.float32) / nf)).astype(np.float32)
    ar, ac = row[:, None] * inv, col[:, None] * inv
    pad = np.zeros((DEC_SEQ, QK_ROPE), np.float32)
    cos = np.concatenate([np.cos(ar), np.cos(ar), np.cos(ac), np.cos(ac), pad], axis=-1)
    sin = np.concatenate([-np.sin(ar), np.sin(ar), -np.sin(ac), np.sin(ac), pad], axis=-1)
    cos_id = np.concatenate([np.ones((DEC_SEQ, QK_ROPE), np.float32), pad], axis=-1)
    sin_id = np.zeros((DEC_SEQ, 2 * QK_ROPE), np.float32)
    return (jnp.asarray(np.stack([cos_id, cos]).astype(np.float32)),
            jnp.asarray(np.stack([sin_id, sin]).astype(np.float32)))


def kernel(x_prompt, x_sample, cache_ckv, cache_krope, c, c_ctx, ada_w, ada_b, ln_g, ln_b, conv_w_in, conv_k, conv_w_out, mla_w_dqkv, mla_q_norm, mla_w_uq, mla_kv_norm, mla_w_ukv, mla_w_o, router_w, router_bias, exp_w_gate, exp_w_up, exp_w_down, sh_w_gate, sh_w_up, sh_w_down):
    cvecs =jnp.concatenate([c_ctx[None, :], c, jnp.zeros((SUBLANES - N_GROUPS_ROWS, D_MODEL), F32)], axis=0)
    mods = _adaln(cvecs, ada_w, ada_b)

    def ln(l, k):
        return ln_g[l, k].reshape(1, D_MODEL), ln_b[l, k].reshape(1, D_MODEL)

    def moe_layer(xin, l, split_streams):
        idx, pos, wcol, counts, hp_a, hp_b = _router(xin, mods[l], router_w[l].T,
                                                     router_bias[l].reshape(N_EXPERTS, 1))
        shared = (sh_w_gate[l].astype(BF16), sh_w_up[l].astype(BF16), sh_w_down[l].astype(BF16))
        outs = {}
        for ch in sorted(range(len(MOE_CHUNKS)), key=lambda j: -MOE_CHUNKS[j][1]):
            row0, rows = MOE_CHUNKS[ch]
            n_tiles = _n_tiles(rows)
            cnt = counts[ch, :, 0].astype(jnp.int32)
            start, tile_expert, tile_valid, n_used, plan = _expert_layout(cnt, n_tiles)
            dest = _dest_rows(start, idx[:, row0:row0 + rows], pos[:, row0:row0 + rows])
            xs_a = _sc_scatter_rows(hp_a, dest, row0, n_tiles * FFN_TM)
            xs_b = _sc_scatter_rows(hp_b, dest, row0, n_tiles * FFN_TM)
            ys_a, ys_b = _expert_ffn(xs_a, xs_b, tile_expert, tile_valid, n_used, plan,
                                     exp_w_gate, exp_w_up, exp_w_down, l)
            first = 0
            for eighths in COMBINE_EIGHTHS:
                piece = rows * eighths // 8
                dest_row = dest[:, first:first + piece].reshape(1, piece * TOP_K)
                g_a = _sc_gather_rows(ys_a, dest_row).reshape(TOP_K, piece, D_QUARTER)
                g_b = _sc_gather_rows(ys_b, dest_row).reshape(TOP_K, piece, D_QUARTER)
                outs = _combine(xin, mods[l], wcol, g_a, g_b, *shared, *ln(l, 1), row0=row0 + first,
                                split_streams=split_streams, prev=outs)
                first += piece
        return (outs["ctx"], outs["lat"]) if split_streams else outs["all"]

    x = _conv_mixer(x_prompt.reshape(T_CTX, D_MODEL), x_sample.reshape(T_LAT, D_MODEL), mods[0],
                    conv_w_in[0].astype(BF16), conv_k[0], conv_w_out[0].astype(BF16), *ln(0, 0))
    x = moe_layer(x, 0, split_streams=False)

    p = _mla_params(mla_w_dqkv[0], mla_q_norm[0], mla_w_uq[0], mla_kv_norm[0], mla_w_ukv[0])
    rope_a, rope_b = _rope_tables()
    q, k, v, ckv, kr = _mla_proj(x, mods[1], p, rope_a, rope_b)
    kr_cache = jnp.concatenate([cache_krope[:, 0].reshape(DEC_BATCH * PAST_LEN, QK_ROPE),
                                jnp.zeros((DEC_BATCH * PAST_LEN, QK_ROPE), F32)], axis=-1)
    kc, vc = _cache_kv(cache_ckv[:, 0].reshape(DEC_BATCH * PAST_LEN, KV_LORA), kr_cache, p)
    o_ctx = _ctx_attention(q, k, v)
    o_lat = _lat_attention(q, k, v, kc, vc)
    x = _oproj(o_ctx, o_lat, x, mods[1], mla_w_o[0].astype(BF16), *ln(1, 0))
    y_ctx, y_lat = moe_layer(x, 1, split_streams=True)

    y_prompt = y_ctx.reshape(BATCH, SEQ, D_MODEL)
    y_sample = y_lat.reshape(DEC_BATCH, DEC_SEQ, D_MODEL)
    state_ckv = ckv[:T_CTX].reshape(BATCH, 1, SEQ, KV_LORA)
    state_krope = kr[:T_CTX, :QK_ROPE].reshape(BATCH, 1, SEQ, QK_ROPE)
    return (y_prompt, y_sample, state_ckv, state_krope)
```

```python
import functools
import math

import jax
import jax.numpy as jnp
import numpy as np
from jax import lax
from jax.experimental import pallas as pl
from jax.experimental.pallas import tpu as pltpu
from jax.experimental.pallas import tpu_sc as plsc

D_MODEL = 1024
BATCH = 16
SEQ = 256
DEPTH = 2
DEC_BATCH = 4
DEC_SEQ = 4096
PAST_LEN = 512
GRID_W = 64

N_HEADS = 8
QK_NOPE = 128
QK_ROPE = 64
V_DIM = 128
Q_LORA = 384
KV_LORA = 256
ROPE_THETA = 10000.0
ATTN_SCALE = (QK_NOPE + QK_ROPE) ** -0.5
HEAD_PAD = 256
Q_PRESCALE = ATTN_SCALE * math.log2(math.e)

N_EXPERTS = 64
TOP_K = 8
N_GROUPS = 8
TOPK_GROUPS = 4
GROUP_SIZE = N_EXPERTS // N_GROUPS
D_EXPERT = 256
D_SHARED = 256
ROUTED_SCALE = 2.5

ALPHA = (2 * DEPTH) ** 0.25
LN_EPS = 1e-5
RMS_EPS = 1e-6

GROUP_ROWS = 4096
N_GROUPS_ROWS = 1 + DEC_BATCH
T_CTX = BATCH * SEQ
T_LAT = DEC_BATCH * DEC_SEQ
T_ALL = T_CTX + T_LAT
LANES = 128
SUBLANES = 8

F32 = jnp.float32
BF16 = jnp.bfloat16
NEG_INF = float("-inf")


def _dot(a, b):
    return jnp.dot(a, b, preferred_element_type=F32)


def _dot_nt(a, b, precision=None):
    return lax.dot_general(a, b, (((1,), (1,)), ((), ())), precision=precision,
                           preferred_element_type=F32)


def _layer_norm(v, g, b):
    mu = jnp.mean(v, axis=-1, keepdims=True)
    d = v - mu
    var = jnp.mean(d * d, axis=-1, keepdims=True)
    return d * lax.rsqrt(var + LN_EPS) * g + b


def _rms_norm(v, g):
    return v * lax.rsqrt(jnp.mean(v * v, axis=-1, keepdims=True) + RMS_EPS) * g


def _silu(v):
    return v / (1.0 + jnp.exp(-v))


def _sigmoid(v):
    return 1.0 / (1.0 + jnp.exp(-v))


def _adaln_kernel(c_ref, w_ref, b_ref, o_ref):
    c = c_ref[...]
    s = _silu(c)
    o_ref[0, 0] = jnp.dot(s, w_ref[0], precision=lax.Precision.HIGHEST,
                          preferred_element_type=F32) + b_ref[0]


def _adaln(cvecs, ada_w, ada_b):
    out = pl.pallas_call(
        _adaln_kernel,
        grid=(DEPTH, 6),
        in_specs=[
            pl.BlockSpec((SUBLANES, D_MODEL), lambda l, j: (0, 0)),
            pl.BlockSpec((1, D_MODEL, D_MODEL), lambda l, j: (l, 0, j)),
            pl.BlockSpec((1, 1, D_MODEL), lambda l, j: (l, 0, j)),
        ],
        out_specs=pl.BlockSpec((1, 1, SUBLANES, D_MODEL), lambda l, j: (l, j, 0, 0)),
        out_shape=jax.ShapeDtypeStruct((DEPTH, 6, SUBLANES, D_MODEL), F32),
        compiler_params=pltpu.CompilerParams(dimension_semantics=("parallel", "parallel")),
        name="adaln",
    )(cvecs, ada_w, ada_b.reshape(DEPTH, 1, 6 * D_MODEL))
    return jnp.transpose(out[:, :, :N_GROUPS_ROWS, :], (0, 2, 1, 3))


CONV_TM = 1024
CONV_PIECES = 2


def _conv_kernel(xc_ref, xcp_ref, xcn_ref, xl_ref, xlp_ref, xln_ref, mods_ref, win_ref, ck_ref, wout_ref,
                 g_ref, b_ref, o_ref):
    i = pl.program_id(0)
    tm = xc_ref.shape[0]
    is_ctx = i < T_CTX // tm
    m = mods_ref[0]
    sm, cm, gm = m[0:1], m[1:2], m[2:3]
    x = jnp.where(is_ctx, xc_ref[...], xl_ref[...])
    xp = jnp.where(is_ctx, xcp_ref[...], xlp_ref[...])
    xn = jnp.where(is_ctx, xcn_ref[...], xln_ref[...])
    def halo_u(xh):
        hh = (xh * (1.0 + cm) + sm).astype(BF16)
        zh = _dot(hh, win_ref[:, D_MODEL:])
        return zh[:, :D_MODEL] * zh[:, D_MODEL:]

    th = tm // CONV_PIECES
    halves = [x[j * th:(j + 1) * th] for j in range(CONV_PIECES)]
    zs = [_dot((xh * (1.0 + cm) + sm).astype(BF16), win_ref[...]) for xh in halves]
    us = [z[:, D_MODEL:2 * D_MODEL] * z[:, 2 * D_MODEL:] for z in zs]
    befores = [halo_u(xp)[SUBLANES - 1:SUBLANES]] + [u[th - 1:th] for u in us[:-1]]
    afters = [u[0:1] for u in us[1:]] + [halo_u(xn)[0:1]]

    ck = ck_ref[...]
    row = lax.broadcasted_iota(jnp.int32, (th, 1), 0)
    for hf in range(CONV_PIECES):
        grow = i * tm + hf * th + row
        seq_len = jnp.where(grow < T_CTX, SEQ, DEC_SEQ)
        pos = jnp.bitwise_and(grow, seq_len - 1)
        u = us[hf]
        left = jnp.where(row == 0, befores[hf], pltpu.roll(u, 1, 0))
        left = jnp.where(pos == 0, 0.0, left)
        right = jnp.where(row == th - 1, afters[hf], pltpu.roll(u, th - 1, 0))
        right = jnp.where(pos == seq_len - 1, 0.0, right)
        conv = left * ck[0:1] + u * ck[1:2] + right * ck[2:3]
        v = (zs[hf][:, :D_MODEL] * conv).astype(BF16)
        y = _dot(v, wout_ref[...])
        o_ref[hf * th:(hf + 1) * th, :] = _layer_norm(ALPHA * halves[hf] + gm * y, g_ref[...], b_ref[...])


def _conv_mixer(x_ctx, x_lat, mods, w_in, conv_k, w_out, ln_g, ln_b):
    tm = CONV_TM
    per8 = tm // SUBLANES
    n_ctx = T_CTX // tm
    n_lat = T_LAT // tm

    def stream_specs(first, n):
        def blk(i):
            return jnp.clip(i - first, 0, n - 1)
        return [
            pl.BlockSpec((tm, D_MODEL), lambda i: (blk(i), 0)),
            pl.BlockSpec((SUBLANES, D_MODEL), lambda i: (jnp.maximum(blk(i) * per8 - 1, 0), 0)),
            pl.BlockSpec((SUBLANES, D_MODEL), lambda i: (jnp.minimum((blk(i) + 1) * per8, n * per8 - 1), 0)),
        ]

    return pl.pallas_call(
        _conv_kernel,
        grid=(n_ctx + n_lat,),
        in_specs=stream_specs(0, n_ctx) + stream_specs(n_ctx, n_lat) + [
            pl.BlockSpec((1, 6, D_MODEL), lambda i: (i * tm // GROUP_ROWS, 0, 0)),
            pl.BlockSpec((D_MODEL, 3 * D_MODEL), lambda i: (0, 0)),
            pl.BlockSpec((3, D_MODEL), lambda i: (0, 0)),
            pl.BlockSpec((D_MODEL, D_MODEL), lambda i: (0, 0)),
            pl.BlockSpec((1, D_MODEL), lambda i: (0, 0)),
            pl.BlockSpec((1, D_MODEL), lambda i: (0, 0)),
        ],
        out_specs=pl.BlockSpec((tm, D_MODEL), lambda i: (i, 0)),
        out_shape=jax.ShapeDtypeStruct((T_ALL, D_MODEL), F32),
        compiler_params=pltpu.CompilerParams(dimension_semantics=("parallel",)),
        name="conv_mixer",
    )(x_ctx, x_ctx, x_ctx, x_lat, x_lat, x_lat, mods, w_in, conv_k, w_out, ln_g, ln_b)


ROUTER_TM = 512
MOE_CHUNKS = ((0, T_ALL),)
COMBINE_EIGHTHS = (2, 2, 2, 2)


def _first_argmax_mask(cur, ridx, n):
    mx = jnp.max(cur, axis=0, keepdims=True)
    first = jnp.min(jnp.where(cur == mx, ridx, n), axis=0, keepdims=True)
    return ridx == first, mx


def _router_kernel(x_ref, mods_ref, rwt_ref, bias_ref, idx_ref, pos_ref, wcol_ref, count_ref, hpa_ref, hpb_ref,
                   carry_ref):
    tm = x_ref.shape[0]
    m = mods_ref[0]
    sf, cf = m[3:4], m[4:5]
    hc = x_ref[...] * (1.0 + cf) + sf
    words = _pack_rows(hc)
    hpa_ref[...] = words[:, :D_QUARTER]
    hpb_ref[...] = words[:, D_QUARTER:]
    logits = _dot_nt(rwt_ref[...], hc, precision=lax.Precision.HIGHEST)
    scores = _sigmoid(logits)
    biased = scores + bias_ref[...]

    ridx8 = lax.broadcasted_iota(jnp.int32, (GROUP_SIZE, tm), 0)
    gscore = jnp.full((N_GROUPS, tm), NEG_INF, F32)
    for g in range(N_GROUPS):
        blk = biased[g * GROUP_SIZE:(g + 1) * GROUP_SIZE]
        sel, m1 = _first_argmax_mask(blk, ridx8, GROUP_SIZE)
        m2 = jnp.max(jnp.where(sel, NEG_INF, blk), axis=0, keepdims=True)
        gscore = jnp.where(ridx8 == g, m1 + m2, gscore)

    gmask = jnp.zeros((N_GROUPS, tm), jnp.bool_)
    cur = gscore
    for _ in range(TOPK_GROUPS):
        sel, _unused = _first_argmax_mask(cur, ridx8, N_GROUPS)
        gmask = jnp.logical_or(gmask, sel)
        cur = jnp.where(sel, NEG_INF, cur)

    gmask_f = gmask.astype(F32)
    blocks = []
    for g in range(N_GROUPS):
        keep = jnp.broadcast_to(gmask_f[g:g + 1], (GROUP_SIZE, tm)) > 0.5
        blocks.append(jnp.where(keep, biased[g * GROUP_SIZE:(g + 1) * GROUP_SIZE], NEG_INF))
    cur = jnp.concatenate(blocks, axis=0)

    first_tiles = [row0 // tm for row0, _rows in MOE_CHUNKS]
    starts_chunk = functools.reduce(jnp.logical_or, [pl.program_id(0) == ft for ft in first_tiles])

    @pl.when(starts_chunk)
    def _():
        carry_ref[...] = jnp.zeros(carry_ref.shape, F32)

    ridx = lax.broadcasted_iota(jnp.int32, (N_EXPERTS, tm), 0)
    kidx = lax.broadcasted_iota(jnp.int32, (TOP_K, tm), 0)
    sels = []
    chosen = jnp.zeros((N_EXPERTS, tm), jnp.bool_)
    idx_rows = jnp.zeros((TOP_K, tm), jnp.int32)
    for k in range(TOP_K):
        mx = jnp.max(cur, axis=0, keepdims=True)
        first = jnp.min(jnp.where(cur == mx, ridx, N_EXPERTS), axis=0, keepdims=True)
        sel = ridx == first
        sels.append(sel)
        chosen = jnp.logical_or(chosen, sel)
        idx_rows = jnp.where(kidx == k, first, idx_rows)
        cur = jnp.where(sel, NEG_INF, cur)

    onehot = chosen.astype(F32)
    t_row = lax.broadcasted_iota(jnp.int32, (tm, tm), 0)
    t_col = lax.broadcasted_iota(jnp.int32, (tm, tm), 1)
    before = (t_row < t_col).astype(BF16)
    rank = carry_ref[...] + _dot(onehot.astype(BF16), before)
    carry_ref[...] = carry_ref[...] + jnp.sum(onehot, axis=1, keepdims=True)
    count_ref[0] = jnp.broadcast_to(carry_ref[...], count_ref.shape[1:])

    w = jnp.where(chosen, scores, 0.0)
    w = w / jnp.sum(w, axis=0, keepdims=True) * ROUTED_SCALE
    pos_rows = jnp.zeros((TOP_K, tm), F32)
    w_rows = jnp.zeros((TOP_K, tm), F32)
    for k in range(TOP_K):
        pos_rows = jnp.where(kidx == k, jnp.sum(jnp.where(sels[k], rank, 0.0), axis=0, keepdims=True), pos_rows)
        w_rows = jnp.where(kidx == k, jnp.sum(jnp.where(sels[k], w, 0.0), axis=0, keepdims=True), w_rows)
    idx_ref[...] = idx_rows
    pos_ref[...] = pos_rows.astype(jnp.int32)
    wpad = jnp.concatenate([w_rows, jnp.zeros((LANES - TOP_K, tm), F32)], axis=0)
    wcol_ref[...] = wpad.T


def _router(x, mods, router_wt, router_bias):
    t = x.shape[0]
    tm = ROUTER_TM

    def chunk_of(i):
        return sum((i >= row0 // tm).astype(jnp.int32) for row0, _rows in MOE_CHUNKS[1:])

    return pl.pallas_call(
        _router_kernel,
        grid=(t // tm,),
        in_specs=[
            pl.BlockSpec((tm, D_MODEL), lambda i: (i, 0)),
            pl.BlockSpec((1, 6, D_MODEL), lambda i: (i * tm // GROUP_ROWS, 0, 0)),
            pl.BlockSpec((N_EXPERTS, D_MODEL), lambda i: (0, 0)),
            pl.BlockSpec((N_EXPERTS, 1), lambda i: (0, 0)),
        ],
        out_specs=[
            pl.BlockSpec((TOP_K, tm), lambda i: (0, i)),
            pl.BlockSpec((TOP_K, tm), lambda i: (0, i)),
            pl.BlockSpec((tm, LANES), lambda i: (i, 0)),
            pl.BlockSpec((1, N_EXPERTS, LANES), lambda i: (chunk_of(i), 0, 0)),
            pl.BlockSpec((tm, D_QUARTER), lambda i: (i, 0)),
            pl.BlockSpec((tm, D_QUARTER), lambda i: (i, 0)),
        ],
        out_shape=[
            jax.ShapeDtypeStruct((TOP_K, t), jnp.int32),
            jax.ShapeDtypeStruct((TOP_K, t), jnp.int32),
            jax.ShapeDtypeStruct((t, LANES), F32),
            jax.ShapeDtypeStruct((len(MOE_CHUNKS), N_EXPERTS, LANES), F32),
            jax.ShapeDtypeStruct((t, D_QUARTER), jnp.uint32),
            jax.ShapeDtypeStruct((t, D_QUARTER), jnp.uint32),
        ],
        scratch_shapes=[pltpu.VMEM((N_EXPERTS, 1), F32)],
        compiler_params=pltpu.CompilerParams(dimension_semantics=("arbitrary",)),
        name="moe_router",
    )(x, mods, router_wt, router_bias)


FFN_TM = 512


def _n_tiles(rows):
    return rows * TOP_K // FFN_TM + N_EXPERTS


def _expert_layout(counts, n_tiles):
    padded = (counts + FFN_TM - 1) // FFN_TM * FFN_TM
    end = jnp.cumsum(padded)
    start = end - padded
    tile_row = jnp.arange(n_tiles, dtype=jnp.int32) * FFN_TM
    tile_expert = jnp.minimum(jnp.sum(end[None, :] <= tile_row[:, None], axis=1), N_EXPERTS - 1)
    of_tile = tile_expert[:, None] == jnp.arange(N_EXPERTS, dtype=jnp.int32)[None, :]
    live_end = jnp.sum(jnp.where(of_tile, (start + counts)[None, :], 0), axis=1)
    tile_valid = jnp.clip(live_end - tile_row, 0, FFN_TM)
    n_used = (end[-1] // FFN_TM).astype(jnp.int32).reshape(1)
    used = jnp.arange(n_tiles, dtype=jnp.int32) < n_used[0]
    prev_expert = jnp.concatenate([jnp.full((1,), -1, tile_expert.dtype), tile_expert[:-1]])
    run_flag = jnp.logical_and(used, tile_expert != prev_expert)
    run_slot = (jnp.cumsum(run_flag.astype(jnp.int32)) - 1) % 2
    later = jnp.logical_and(used[None, :], tile_expert[None, :] > tile_expert[:, None])
    next_expert = jnp.min(jnp.where(later, tile_expert[None, :], N_EXPERTS), axis=1)
    next_expert = jnp.where(next_expert == N_EXPERTS, -1, next_expert)
    plan = (run_flag.astype(jnp.int32), run_slot.astype(jnp.int32), next_expert.astype(jnp.int32))
    return start.astype(jnp.int32), tile_expert.astype(jnp.int32), tile_valid.astype(jnp.int32), n_used, plan


D_HALF = D_MODEL // 2
D_QUARTER = D_MODEL // 4


def _pack_rows(v):
    hi = lax.bitcast_convert_type(v[:, :D_HALF].astype(BF16).astype(F32), jnp.uint32)
    lo = lax.bitcast_convert_type(v[:, D_HALF:].astype(BF16).astype(F32), jnp.uint32)
    return jnp.bitwise_or(hi, jnp.right_shift(lo, jnp.uint32(16)))


def _unpack_rows(w):
    hi = lax.bitcast_convert_type(jnp.bitwise_and(w, jnp.uint32(0xFFFF0000)), F32)
    lo = lax.bitcast_convert_type(jnp.left_shift(w, jnp.uint32(16)), F32)
    return hi, lo


DEST_TM = 2048


def _dest_kernel(start_ref, idx_ref, pos_ref, dest_ref):
    idx = idx_ref[...]
    base = jnp.zeros(idx.shape, jnp.int32)
    for e in range(N_EXPERTS):
        base = jnp.where(idx == e, start_ref[e], base)
    dest_ref[...] = base + pos_ref[...]


def _dest_rows(start, idx, pos):
    t = idx.shape[1]
    return pl.pallas_call(
        _dest_kernel,
        grid_spec=pltpu.PrefetchScalarGridSpec(
            num_scalar_prefetch=1,
            grid=(t // DEST_TM,),
            in_specs=[pl.BlockSpec((TOP_K, DEST_TM), lambda i, s: (0, i)),
                      pl.BlockSpec((TOP_K, DEST_TM), lambda i, s: (0, i))],
            out_specs=pl.BlockSpec((TOP_K, DEST_TM), lambda i, s: (0, i)),
        ),
        out_shape=jax.ShapeDtypeStruct((TOP_K, t), jnp.int32),
        compiler_params=pltpu.CompilerParams(dimension_semantics=("parallel",)),
        name="moe_dest",
    )(start, idx, pos)


SC_WINDOW = 128


def _sc_mesh():
    return plsc.VectorSubcoreMesh(core_axis_name="c", subcore_axis_name="s")


def _sc_scatter_rows(x, dest, row0, n_sorted):
    t = dest.shape[1]
    blk0 = row0 // SC_WINDOW

    @functools.partial(
        pl.kernel,
        out_type=jax.ShapeDtypeStruct((n_sorted, D_QUARTER), x.dtype),
        mesh=_sc_mesh(),
        scratch_types=[],
    )
    def scatter(x_hbm, i_hbm, o_hbm):
        def body(x_vmem, i_vmem):
            pltpu.sync_copy(x_vmem, o_hbm.at[i_vmem.at[0]])

        pltpu.emit_pipeline(
            body,
            grid=(t // SC_WINDOW, TOP_K),
            in_specs=[pl.BlockSpec((SC_WINDOW, D_QUARTER), lambda i, k: (i + blk0, 0)),
                      pl.BlockSpec((1, SC_WINDOW), lambda i, k: (k, i))],
            out_specs=[],
            core_axis_name=("c", "s"),
            dimension_semantics=(pltpu.PARALLEL, pltpu.ARBITRARY),
        )(x_hbm, i_hbm)

    return scatter(x, dest)


def _sc_gather_rows(table, idx):
    m = idx.shape[1]

    @functools.partial(
        pl.kernel,
        out_type=jax.ShapeDtypeStruct((m, D_QUARTER), table.dtype),
        mesh=_sc_mesh(),
        scratch_types=[],
    )
    def gather(t_hbm, i_hbm, o_hbm):
        def body(i_vmem, o_vmem):
            pltpu.sync_copy(t_hbm.at[i_vmem.at[0]], o_vmem)

        pltpu.emit_pipeline(
            body,
            grid=(m // SC_WINDOW,),
            in_specs=[pl.BlockSpec((1, SC_WINDOW), lambda i: (0, i))],
            out_specs=[pl.BlockSpec((SC_WINDOW, D_QUARTER), lambda i: (i, 0))],
            core_axis_name=("c", "s"),
            dimension_semantics=(pltpu.PARALLEL,),
        )(i_hbm, o_hbm)

    return gather(table, idx)


FFN_STEP_TILES = 2
FFN_PIECES = 1


def _ffn_kernel(layer, te_ref, tv_ref, nu_ref, flag_ref, slot_ref, next_ref, xa_ref, xb_ref,
                wg_hbm, wu_hbm, wd_hbm, ya_ref, yb_ref, wg_st, wu_st, wd_st, wgb_ref, wub_ref, wdb_ref, sem):
    step = pl.program_id(0)

    def fetch(expert, slot):
        return [pltpu.make_async_copy(src.at[layer, expert], dst.at[slot], sem.at[slot])
                for src, dst in ((wg_hbm, wg_st), (wu_hbm, wu_st), (wd_hbm, wd_st))]

    for s in range(FFN_STEP_TILES):
        tile = step * FFN_STEP_TILES + s
        active = tile < nu_ref[0]
        tile_rows = s * FFN_TM

        @pl.when(jnp.logical_and(active, flag_ref[tile] == 1))
        def _():
            slot = slot_ref[tile]

            @pl.when(tile == 0)
            def _():
                for cp in fetch(te_ref[tile], slot):
                    cp.start()

            for cp in fetch(te_ref[tile], slot):
                cp.wait()
            wgb_ref[...] = wg_st[slot].astype(BF16)
            wub_ref[...] = wu_st[slot].astype(BF16)
            wdb_ref[...] = wd_st[slot].astype(BF16)

            @pl.when(next_ref[tile] >= 0)
            def _():
                for cp in fetch(next_ref[tile], 1 - slot):
                    cp.start()

        @pl.when(active)
        def _():
            wg = wgb_ref[...]
            wu = wub_ref[...]
            wd = wdb_ref[...]
            th = FFN_TM // FFN_PIECES
            for r0 in range(0, FFN_TM, th):
                rows = pl.ds(tile_rows + r0, th)
                live = (lax.broadcasted_iota(jnp.int32, (th, D_QUARTER), 0) + r0) < tv_ref[tile]
                hi_a, lo_a = _unpack_rows(jnp.where(live, xa_ref[rows, :], jnp.uint32(0)))
                hi_b, lo_b = _unpack_rows(jnp.where(live, xb_ref[rows, :], jnp.uint32(0)))
                xb = jnp.concatenate([hi_a.astype(BF16), hi_b.astype(BF16), lo_a.astype(BF16), lo_b.astype(BF16)],
                                     axis=1)
                a = (_silu(_dot(xb, wg)) * _dot(xb, wu)).astype(BF16)
                words = _pack_rows(_dot(a, wd))
                ya_ref[rows, :] = words[:, :D_QUARTER]
                yb_ref[rows, :] = words[:, D_QUARTER:]

        @pl.when(jnp.logical_not(active))
        def _():
            rows = pl.ds(tile_rows, FFN_TM)
            ya_ref[rows, :] = jnp.zeros((FFN_TM, D_QUARTER), jnp.uint32)
            yb_ref[rows, :] = jnp.zeros((FFN_TM, D_QUARTER), jnp.uint32)


def _expert_ffn(xs_a, xs_b, tile_expert, tile_valid, n_used, plan, wg, wu, wd, layer):
    n_tiles = xs_a.shape[0] // FFN_TM
    step_rows = FFN_STEP_TILES * FFN_TM

    def row_map(i, te, tv, nu, fl, sl, nx):
        return (jnp.minimum(i, (nu[0] - 1) // FFN_STEP_TILES), 0)

    def out_map(i, te, tv, nu, fl, sl, nx):
        return (i, 0)

    hbm = pl.BlockSpec(memory_space=pl.ANY)
    return pl.pallas_call(
        functools.partial(_ffn_kernel, layer),
        grid_spec=pltpu.PrefetchScalarGridSpec(
            num_scalar_prefetch=6,
            grid=(n_tiles // FFN_STEP_TILES,),
            in_specs=[pl.BlockSpec((step_rows, D_QUARTER), row_map),
                      pl.BlockSpec((step_rows, D_QUARTER), row_map), hbm, hbm, hbm],
            out_specs=[pl.BlockSpec((step_rows, D_QUARTER), out_map), pl.BlockSpec((step_rows, D_QUARTER), out_map)],
            scratch_shapes=[pltpu.VMEM((2, D_MODEL, D_EXPERT), F32), pltpu.VMEM((2, D_MODEL, D_EXPERT), F32),
                            pltpu.VMEM((2, D_EXPERT, D_MODEL), F32),
                            pltpu.VMEM((D_MODEL, D_EXPERT), BF16), pltpu.VMEM((D_MODEL, D_EXPERT), BF16),
                            pltpu.VMEM((D_EXPERT, D_MODEL), BF16), pltpu.SemaphoreType.DMA((2,))],
        ),
        out_shape=[jax.ShapeDtypeStruct((n_tiles * FFN_TM, D_QUARTER), jnp.uint32),
                   jax.ShapeDtypeStruct((n_tiles * FFN_TM, D_QUARTER), jnp.uint32)],
        compiler_params=pltpu.CompilerParams(dimension_semantics=("arbitrary",)),
        name="moe_expert_ffn",
    )(tile_expert, tile_valid, n_used, *plan, xs_a, xs_b, wg, wu, wd)


COMBINE_TM = 512


def _combine_kernel(n_prev, tile0, x_ref, mods_ref, wcol_ref, ga_ref, gb_ref, sg_ref, su_ref, sd_ref, g_ref, b_ref,
                    *rest):
    o_refs = rest[n_prev:]
    m = mods_ref[0]
    sf, cf, gf = m[3:4], m[4:5], m[5:6]
    x = x_ref[...]
    hc = (x * (1.0 + cf) + sf).astype(BF16)
    a = _silu(_dot(hc, sg_ref[...])) * _dot(hc, su_ref[...])
    y = _dot(a.astype(BF16), sd_ref[...])
    wcol = wcol_ref[...]
    parts = [y[:, q * D_QUARTER:(q + 1) * D_QUARTER] for q in range(4)]
    for k in range(TOP_K):
        hi_a, lo_a = _unpack_rows(ga_ref[k])
        hi_b, lo_b = _unpack_rows(gb_ref[k])
        wk = wcol[:, k:k + 1]
        parts = [parts[0] + wk * hi_a, parts[1] + wk * hi_b, parts[2] + wk * lo_a, parts[3] + wk * lo_b]
    y = jnp.concatenate(parts, axis=1)
    out = _layer_norm(ALPHA * x + gf * y, g_ref[...], b_ref[...])
    if len(o_refs) == 1:
        o_refs[0][...] = out
    else:
        is_ctx = pl.program_id(0) + tile0 < T_CTX // x_ref.shape[0]

        @pl.when(is_ctx)
        def _():
            o_refs[0][...] = out

        @pl.when(jnp.logical_not(is_ctx))
        def _():
            o_refs[1][...] = out


def _combine(x, mods, wcol, g_a, g_b, sg, su, sd, ln_g, ln_b, row0, split_streams, prev):
    tm = COMBINE_TM
    n = g_a.shape[1] // tm
    tile0 = row0 // tm
    n_ctx = T_CTX // tm
    full = lambda shape: pl.BlockSpec(shape, lambda i: (0,) * len(shape))
    layouts = {
        "all": (lambda i: (tile0 + i, 0), T_ALL),
        "ctx": (lambda i: (jnp.minimum(tile0 + i, n_ctx - 1), 0), T_CTX),
        "lat": (lambda i: (jnp.maximum(tile0 + i - n_ctx, 0), 0), T_LAT),
    }
    if split_streams:
        kinds = (["ctx"] if tile0 < n_ctx else []) + (["lat"] if tile0 + n > n_ctx else [])
    else:
        kinds = ["all"]
    carried = [kd for kd in kinds if kd in prev]
    outs = pl.pallas_call(
        functools.partial(_combine_kernel, len(carried), tile0),
        grid=(n,),
        in_specs=[
            pl.BlockSpec((tm, D_MODEL), lambda i: (tile0 + i, 0)),
            pl.BlockSpec((1, 6, D_MODEL), lambda i: ((tile0 + i) * tm // GROUP_ROWS, 0, 0)),
            pl.BlockSpec((tm, LANES), lambda i: (tile0 + i, 0)),
            pl.BlockSpec((TOP_K, tm, D_QUARTER), lambda i: (0, i, 0)),
            pl.BlockSpec((TOP_K, tm, D_QUARTER), lambda i: (0, i, 0)),
            full((D_MODEL, D_SHARED)), full((D_MODEL, D_SHARED)), full((D_SHARED, D_MODEL)),
            full((1, D_MODEL)), full((1, D_MODEL)),
        ] + [pl.BlockSpec(memory_space=pl.ANY) for _ in carried],
        out_specs=[pl.BlockSpec((tm, D_MODEL), layouts[kd][0]) for kd in kinds],
        out_shape=[jax.ShapeDtypeStruct((layouts[kd][1], D_MODEL), F32) for kd in kinds],
        input_output_aliases={10 + j: kinds.index(kd) for j, kd in enumerate(carried)},
        compiler_params=pltpu.CompilerParams(dimension_semantics=("arbitrary",)),
        name="moe_combine",
    )(x, mods, wcol, g_a, g_b, sg, su, sd, ln_g, ln_b, *[prev[kd] for kd in carried])
    return {**prev, **dict(zip(kinds, outs))}


MLA_TM = 1024


def _mla_proj_kernel(x_ref, mods_ref, wdq_ref, wdkv_ref, wkr_ref, qn_ref, kvn_ref, wuq_ref,
                     wukn_ref, wuv_ref, ta_ref, tb_ref, q_ref, k_ref, v_ref, ckv_ref, kr_ref):
    m = mods_ref[0]
    sm, cm = m[0:1], m[1:2]
    h = (x_ref[...] * (1.0 + cm) + sm).astype(BF16)
    cq = _rms_norm(_dot(h, wdq_ref[...]), qn_ref[...])
    ckv = _rms_norm(_dot(h, wdkv_ref[...]), kvn_ref[...])
    kr2 = _dot(h, wkr_ref[...])
    ckv_ref[...] = ckv
    kr_ref[...] = kr2

    ka = ta_ref[0]
    kb = tb_ref[0]
    tm = ka.shape[0]
    ta = jnp.concatenate([jnp.full((tm, QK_NOPE), Q_PRESCALE, F32), ka * Q_PRESCALE], axis=1)
    tb = jnp.concatenate([jnp.zeros((tm, QK_NOPE), F32), kb * Q_PRESCALE], axis=1)
    krr = kr2 * ka + pltpu.roll(kr2, QK_ROPE, 1) * kb

    qpre = _dot(cq.astype(BF16), wuq_ref[...])
    ckv_b = ckv.astype(BF16)
    kn = _dot(ckv_b, wukn_ref[...])
    v_ref[...] = _dot(ckv_b, wuv_ref[...]).astype(BF16)
    for hd in range(N_HEADS):
        qh = qpre[:, hd * HEAD_PAD:(hd + 1) * HEAD_PAD]
        qrot = qh * ta + pltpu.roll(qh, HEAD_PAD - QK_ROPE, 1) * tb
        q_ref[:, hd * HEAD_PAD:(hd + 1) * HEAD_PAD] = qrot.astype(BF16)
        k_ref[:, hd * HEAD_PAD:hd * HEAD_PAD + QK_NOPE] = kn[:, hd * QK_NOPE:(hd + 1) * QK_NOPE].astype(BF16)
        k_ref[:, hd * HEAD_PAD + QK_NOPE:(hd + 1) * HEAD_PAD] = krr.astype(BF16)


def _mla_proj(x, mods, p, rope_a, rope_b):
    t = x.shape[0]
    tm = MLA_TM
    full = lambda shape: pl.BlockSpec(shape, lambda i: (0,) * len(shape))
    rope_spec = pl.BlockSpec(
        (1, tm, 2 * QK_ROPE),
        lambda i: (jnp.minimum(i * tm // GROUP_ROWS, 1), (i * tm % GROUP_ROWS) // tm, 0))
    return pl.pallas_call(
        _mla_proj_kernel,
        grid=(t // tm,),
        in_specs=[
            pl.BlockSpec((tm, D_MODEL), lambda i: (i, 0)),
            pl.BlockSpec((1, 6, D_MODEL), lambda i: (i * tm // GROUP_ROWS, 0, 0)),
            full((D_MODEL, Q_LORA)), full((D_MODEL, KV_LORA)), full((D_MODEL, 2 * QK_ROPE)),
            full((1, Q_LORA)), full((1, KV_LORA)),
            full((Q_LORA, N_HEADS * HEAD_PAD)),
            full((KV_LORA, N_HEADS * QK_NOPE)), full((KV_LORA, N_HEADS * V_DIM)),
            rope_spec, rope_spec,
        ],
        out_specs=[
            pl.BlockSpec((tm, N_HEADS * HEAD_PAD), lambda i: (i, 0)),
            pl.BlockSpec((tm, N_HEADS * HEAD_PAD), lambda i: (i, 0)),
            pl.BlockSpec((tm, N_HEADS * V_DIM), lambda i: (i, 0)),
            pl.BlockSpec((tm, KV_LORA), lambda i: (i, 0)),
            pl.BlockSpec((tm, 2 * QK_ROPE), lambda i: (i, 0)),
        ],
        out_shape=[
            jax.ShapeDtypeStruct((t, N_HEADS * HEAD_PAD), BF16),
            jax.ShapeDtypeStruct((t, N_HEADS * HEAD_PAD), BF16),
            jax.ShapeDtypeStruct((t, N_HEADS * V_DIM), BF16),
            jax.ShapeDtypeStruct((t, KV_LORA), F32),
            jax.ShapeDtypeStruct((t, 2 * QK_ROPE), F32),
        ],
        compiler_params=pltpu.CompilerParams(dimension_semantics=("parallel",)),
        name="mla_proj",
    )(x, mods, p["w_dq"], p["w_dkv"], p["w_kr"], p["q_norm"], p["kv_norm"], p["w_uq"],
      p["w_ukn"], p["w_uv"], rope_a, rope_b)


def _cache_kv_kernel(ckv_ref, kr_ref, wukn_ref, wuv_ref, k_ref, v_ref):
    ckv_b = ckv_ref[...].astype(BF16)
    kn = _dot(ckv_b, wukn_ref[...])
    v_ref[...] = _dot(ckv_b, wuv_ref[...]).astype(BF16)
    kr = kr_ref[...].astype(BF16)
    for hd in range(N_HEADS):
        k_ref[:, hd * HEAD_PAD:hd * HEAD_PAD + QK_NOPE] = kn[:, hd * QK_NOPE:(hd + 1) * QK_NOPE].astype(BF16)
        k_ref[:, hd * HEAD_PAD + QK_NOPE:(hd + 1) * HEAD_PAD] = kr


def _cache_kv(ckv, kr_pad, p):
    t = ckv.shape[0]
    tm = PAST_LEN
    full = lambda shape: pl.BlockSpec(shape, lambda i: (0,) * len(shape))
    return pl.pallas_call(
        _cache_kv_kernel,
        grid=(t // tm,),
        in_specs=[
            pl.BlockSpec((tm, KV_LORA), lambda i: (i, 0)),
            pl.BlockSpec((tm, 2 * QK_ROPE), lambda i: (i, 0)),
            full((KV_LORA, N_HEADS * QK_NOPE)), full((KV_LORA, N_HEADS * V_DIM)),
        ],
        out_specs=[
            pl.BlockSpec((tm, N_HEADS * HEAD_PAD), lambda i: (i, 0)),
            pl.BlockSpec((tm, N_HEADS * V_DIM), lambda i: (i, 0)),
        ],
        out_shape=[
            jax.ShapeDtypeStruct((t, N_HEADS * HEAD_PAD), BF16),
            jax.ShapeDtypeStruct((t, N_HEADS * V_DIM), BF16),
        ],
        compiler_params=pltpu.CompilerParams(dimension_semantics=("parallel",)),
        name="mla_cache_kv",
    )(ckv, kr_pad, p["w_ukn"], p["w_uv"])


def _ctx_attn_kernel(q_ref, k_ref, v_ref, o_ref):
    for hd in range(N_HEADS):
        q = q_ref[:, hd * HEAD_PAD:(hd + 1) * HEAD_PAD]
        k = k_ref[:, hd * HEAD_PAD:(hd + 1) * HEAD_PAD]
        s = _dot_nt(q, k)
        s = s - jnp.max(s, axis=-1, keepdims=True)
        p = jnp.exp2(s)
        p = p / jnp.sum(p, axis=-1, keepdims=True)
        o = _dot(p.astype(BF16), v_ref[:, hd * V_DIM:(hd + 1) * V_DIM])
        o_ref[:, hd * V_DIM:(hd + 1) * V_DIM] = o.astype(BF16)


def _ctx_attention(q, k, v):
    return pl.pallas_call(
        _ctx_attn_kernel,
        grid=(BATCH,),
        in_specs=[
            pl.BlockSpec((SEQ, N_HEADS * HEAD_PAD), lambda b: (b, 0)),
            pl.BlockSpec((SEQ, N_HEADS * HEAD_PAD), lambda b: (b, 0)),
            pl.BlockSpec((SEQ, N_HEADS * V_DIM), lambda b: (b, 0)),
        ],
        out_specs=pl.BlockSpec((SEQ, N_HEADS * V_DIM), lambda b: (b, 0)),
        out_shape=jax.ShapeDtypeStruct((T_CTX, N_HEADS * V_DIM), BF16),
        compiler_params=pltpu.CompilerParams(dimension_semantics=("parallel",)),
        name="ctx_attention",
    )(q, k, v)


LAT_TQ = 1024
LAT_TK = 512
LAT_PIECES = 4


def _lat_attn_kernel(q_ref, k_ref, v_ref, kc_ref, vc_ref, o_ref, s_ref, p_ref, m_ref):
    n_chunks = DEC_SEQ // LAT_TK + 1
    tp = q_ref.shape[0] // LAT_PIECES
    pieces = [pl.ds(j * tp, tp) for j in range(LAT_PIECES)]
    groups = [pieces[:2], pieces[2:]]
    state = {}

    def keys(c):
        return kc_ref[...] if c == n_chunks - 1 else k_ref[c * LAT_TK:(c + 1) * LAT_TK, :]

    def values(c):
        return vc_ref[...] if c == n_chunks - 1 else v_ref[c * LAT_TK:(c + 1) * LAT_TK, :]

    def qk(r, c):
        s = _dot_nt(q_ref[r, :], keys(c))
        s_ref[r, c * LAT_TK:(c + 1) * LAT_TK] = s
        mp = state.get(("m", r.start), jnp.full((tp, LANES), NEG_INF, F32))
        for j in range(LAT_TK // LANES):
            mp = jnp.maximum(mp, s[:, j * LANES:(j + 1) * LANES])
        state[("m", r.start)] = mp

    def row_max(r):
        m_ref[r, :] = jnp.broadcast_to(jnp.max(state[("m", r.start)], axis=-1, keepdims=True), (tp, LANES))

    def exp_chunk(r, c, after=None):
        m = m_ref[r, :]
        if after is not None:
            bits = lax.bitcast_convert_type(after[:, :LANES], jnp.uint32)
            zero = lax.shift_right_logical(lax.shift_right_logical(bits, jnp.uint32(16)), jnp.uint32(16))
            m = m + lax.bitcast_convert_type(zero, F32)
        for j in range(c * LAT_TK // LANES, (c + 1) * LAT_TK // LANES):
            p = jnp.exp2(s_ref[r, j * LANES:(j + 1) * LANES] - m)
            p_ref[r, j * LANES:(j + 1) * LANES] = p.astype(BF16)

    def pv(r, c):
        v = values(c)
        v_and_ones = jnp.concatenate([v, jnp.ones_like(v)], axis=1)
        acc = state.get(("a", r.start), jnp.zeros((tp, 2 * V_DIM), F32))
        state[("a", r.start)] = acc + _dot(p_ref[r, c * LAT_TK:(c + 1) * LAT_TK], v_and_ones)

    def finish(r):
        acc = state[("a", r.start)]
        o_ref[r, :] = (acc[:, :V_DIM] / acc[:, V_DIM:]).astype(BF16)

    for c in range(n_chunks):
        for r in groups[0]:
            qk(r, c)
    for r in groups[0]:
        row_max(r)
    for c in range(n_chunks):
        for r in groups[1]:
            qk(r, c)
        for r in groups[0]:
            exp_chunk(r, c)
    for r in groups[1]:
        row_max(r)
    for c in range(n_chunks):
        for r in groups[0]:
            pv(r, c)
        for r0, r in zip(groups[0], groups[1]):
            exp_chunk(r, c, after=state[("a", r0.start)])
    for r in groups[0]:
        finish(r)
    for c in range(n_chunks):
        for r in groups[1]:
            pv(r, c)
    for r in groups[1]:
        finish(r)


def _lat_attention(q, k, v, kc, vc):
    nq = DEC_SEQ // LAT_TQ
    return pl.pallas_call(
        _lat_attn_kernel,
        grid=(DEC_BATCH, N_HEADS, nq),
        in_specs=[
            pl.BlockSpec((LAT_TQ, HEAD_PAD), lambda b, h, i: ((b + 1) * nq + i, h)),
            pl.BlockSpec((DEC_SEQ, HEAD_PAD), lambda b, h, i: (b + 1, h)),
            pl.BlockSpec((DEC_SEQ, V_DIM), lambda b, h, i: (b + 1, h)),
            pl.BlockSpec((PAST_LEN, HEAD_PAD), lambda b, h, i: (b, h)),
            pl.BlockSpec((PAST_LEN, V_DIM), lambda b, h, i: (b, h)),
        ],
        out_specs=pl.BlockSpec((LAT_TQ, V_DIM), lambda b, h, i: (b * nq + i, h)),
        out_shape=jax.ShapeDtypeStruct((T_LAT, N_HEADS * V_DIM), BF16),
        scratch_shapes=[pltpu.VMEM((LAT_TQ, DEC_SEQ + PAST_LEN), F32),
                        pltpu.VMEM((LAT_TQ, DEC_SEQ + PAST_LEN), BF16),
                        pltpu.VMEM((LAT_TQ, LANES), F32)],
        compiler_params=pltpu.CompilerParams(
            dimension_semantics=("parallel", "parallel", "parallel")),
        name="lat_attention",
    )(q, k, v, kc, vc)


OPROJ_TM = 1024


def _oproj_kernel(oc_ref, ol_ref, x_ref, mods_ref, wo_ref, g_ref, b_ref, out_ref):
    is_ctx = pl.program_id(0) < T_CTX // OPROJ_TM
    o = jnp.where(is_ctx, oc_ref[...], ol_ref[...])
    gm = mods_ref[0][2:3]
    y = _dot(o, wo_ref[...])
    out_ref[...] = _layer_norm(ALPHA * x_ref[...] + gm * y, g_ref[...], b_ref[...])


def _oproj(o_ctx, o_lat, x, mods, w_o, ln_g, ln_b):
    t = x.shape[0]
    tm = OPROJ_TM
    n_ctx = T_CTX // tm
    return pl.pallas_call(
        _oproj_kernel,
        grid=(t // tm,),
        in_specs=[
            pl.BlockSpec((tm, N_HEADS * V_DIM), lambda i: (jnp.minimum(i, n_ctx - 1), 0)),
            pl.BlockSpec((tm, N_HEADS * V_DIM), lambda i: (jnp.maximum(i - n_ctx, 0), 0)),
            pl.BlockSpec((tm, D_MODEL), lambda i: (i, 0)),
            pl.BlockSpec((1, 6, D_MODEL), lambda i: (i * tm // GROUP_ROWS, 0, 0)),
            pl.BlockSpec((N_HEADS * V_DIM, D_MODEL), lambda i: (0, 0)),
            pl.BlockSpec((1, D_MODEL), lambda i: (0, 0)),
            pl.BlockSpec((1, D_MODEL), lambda i: (0, 0)),
        ],
        out_specs=pl.BlockSpec((tm, D_MODEL), lambda i: (i, 0)),
        out_shape=jax.ShapeDtypeStruct((t, D_MODEL), F32),
        compiler_params=pltpu.CompilerParams(dimension_semantics=("parallel",)),
        name="attn_oproj",
    )(o_ctx, o_lat, x, mods, w_o, ln_g, ln_b)


def _swap16(w):
    q = QK_ROPE // 4
    return jnp.concatenate([w[..., q:2 * q], w[..., :q], w[..., 3 * q:], w[..., 2 * q:3 * q]], axis=-1)


def _mla_params(w_dqkv, q_norm, w_uq, kv_norm, w_ukv):
    w_kr = w_dqkv[:, Q_LORA + KV_LORA:]
    wq = w_uq.reshape(Q_LORA, N_HEADS, QK_NOPE + QK_ROPE)
    wq_r = wq[..., QK_NOPE:]
    wq = jnp.concatenate([wq[..., :QK_NOPE], wq_r, _swap16(wq_r)], axis=-1)
    wkv = w_ukv.reshape(KV_LORA, N_HEADS, QK_NOPE + V_DIM)
    return {
        "w_dq": w_dqkv[:, :Q_LORA].astype(BF16),
        "w_dkv": w_dqkv[:, Q_LORA:Q_LORA + KV_LORA].astype(BF16),
        "w_kr": jnp.concatenate([w_kr, _swap16(w_kr)], axis=-1).astype(BF16),
        "q_norm": q_norm.reshape(1, Q_LORA),
        "kv_norm": kv_norm.reshape(1, KV_LORA),
        "w_uq": wq.reshape(Q_LORA, N_HEADS * HEAD_PAD).astype(BF16),
        "w_ukn": wkv[..., :QK_NOPE].reshape(KV_LORA, N_HEADS * QK_NOPE).astype(BF16),
        "w_uv": wkv[..., QK_NOPE:].reshape(KV_LORA, N_HEADS * V_DIM).astype(BF16),
    }


def _rope_tables():
    nf = QK_ROPE // 4
    t = np.arange(DEC_SEQ)
    row = (t // GRID_W).astype(np.float32)
    col = (t % GRID_W).astype(np.float32)
    inv = (ROPE_THETA ** (-np.arange(nf, dtype=np.float32) / nf)).astype(np.float32)
    ar, ac = row[:, None] * inv, col[:, None] * inv
    pad = np.zeros((DEC_SEQ, QK_ROPE), np.float32)
    cos = np.concatenate([np.cos(ar), np.cos(ar), np.cos(ac), np.cos(ac), pad], axis=-1)
    sin = np.concatenate([-np.sin(ar), np.sin(ar), -np.sin(ac), np.sin(ac), pad], axis=-1)
    cos_id = np.concatenate([np.ones((DEC_SEQ, QK_ROPE), np.float32), pad], axis=-1)
    sin_id = np.zeros((DEC_SEQ, 2 * QK_ROPE), np.float32)
    return (jnp.asarray(np.stack([cos_id, cos]).astype(np.float32)),
            jnp.asarray(np.stack([sin_id, sin]).astype(np.float32)))


def kernel(x_prompt, x_sample, cache_ckv, cache_krope, c, c_ctx, ada_w, ada_b, ln_g, ln_b, conv_w_in, conv_k, conv_w_out, mla_w_dqkv, mla_q_norm, mla_w_uq, mla_kv_norm, mla_w_ukv, mla_w_o, router_w, router_bias, exp_w_gate, exp_w_up, exp_w_down, sh_w_gate, sh_w_up, sh_w_down):
    cvecs =jnp.concatenate([c_ctx[None, :], c, jnp.zeros((SUBLANES - N_GROUPS_ROWS, D_MODEL), F32)], axis=0)
    mods = _adaln(cvecs, ada_w, ada_b)

    def ln(l, k):
        return ln_g[l, k].reshape(1, D_MODEL), ln_b[l, k].reshape(1, D_MODEL)

    def moe_layer(xin, l, split_streams):
        idx, pos, wcol, counts, hp_a, hp_b = _router(xin, mods[l], router_w[l].T,
                                                     router_bias[l].reshape(N_EXPERTS, 1))
        shared = (sh_w_gate[l].astype(BF16), sh_w_up[l].astype(BF16), sh_w_down[l].astype(BF16))
        outs = {}
        for ch in sorted(range(len(MOE_CHUNKS)), key=lambda j: -MOE_CHUNKS[j][1]):
            row0, rows = MOE_CHUNKS[ch]
            n_tiles = _n_tiles(rows)
            cnt = counts[ch, :, 0].astype(jnp.int32)
            start, tile_expert, tile_valid, n_used, plan = _expert_layout(cnt, n_tiles)
            dest = _dest_rows(start, idx[:, row0:row0 + rows], pos[:, row0:row0 + rows])
            xs_a = _sc_scatter_rows(hp_a, dest, row0, n_tiles * FFN_TM)
            xs_b = _sc_scatter_rows(hp_b, dest, row0, n_tiles * FFN_TM)
            ys_a, ys_b = _expert_ffn(xs_a, xs_b, tile_expert, tile_valid, n_used, plan,
                                     exp_w_gate, exp_w_up, exp_w_down, l)
            first = 0
            for eighths in COMBINE_EIGHTHS:
                piece = rows * eighths // 8
                dest_row = dest[:, first:first + piece].reshape(1, piece * TOP_K)
                g_a = _sc_gather_rows(ys_a, dest_row).reshape(TOP_K, piece, D_QUARTER)
                g_b = _sc_gather_rows(ys_b, dest_row).reshape(TOP_K, piece, D_QUARTER)
                outs = _combine(xin, mods[l], wcol, g_a, g_b, *shared, *ln(l, 1), row0=row0 + first,
                                split_streams=split_streams, prev=outs)
                first += piece
        return (outs["ctx"], outs["lat"]) if split_streams else outs["all"]

    x = _conv_mixer(x_prompt.reshape(T_CTX, D_MODEL), x_sample.reshape(T_LAT, D_MODEL), mods[0],
                    conv_w_in[0].astype(BF16), conv_k[0], conv_w_out[0].astype(BF16), *ln(0, 0))
    x = moe_layer(x, 0, split_streams=False)

    p = _mla_params(mla_w_dqkv[0], mla_q_norm[0], mla_w_uq[0], mla_kv_norm[0], mla_w_ukv[0])
    rope_a, rope_b = _rope_tables()
    q, k, v, ckv, kr = _mla_proj(x, mods[1], p, rope_a, rope_b)
    kr_cache = jnp.concatenate([cache_krope[:, 0].reshape(DEC_BATCH * PAST_LEN, QK_ROPE),
                                jnp.zeros((DEC_BATCH * PAST_LEN, QK_ROPE), F32)], axis=-1)
    kc, vc = _cache_kv(cache_ckv[:, 0].reshape(DEC_BATCH * PAST_LEN, KV_LORA), kr_cache, p)
    o_ctx = _ctx_attention(q, k, v)
    o_lat = _lat_attention(q, k, v, kc, vc)
    x = _oproj(o_ctx, o_lat, x, mods[1], mla_w_o[0].astype(BF16), *ln(1, 0))
    y_ctx, y_lat = moe_layer(x, 1, split_streams=True)

    y_prompt = y_ctx.reshape(BATCH, SEQ, D_MODEL)
    y_sample = y_lat.reshape(DEC_BATCH, DEC_SEQ, D_MODEL)
    state_ckv = ckv[:T_CTX].reshape(BATCH, 1, SEQ, KV_LORA)
    state_krope = kr[:T_CTX, :QK_ROPE].reshape(BATCH, 1, SEQ, QK_ROPE)
    return (y_prompt, y_sample, state_ckv, state_krope)
```

```python
import functools
import math

import jax
import jax.numpy as jnp
import numpy as np
from jax import lax
from jax.experimental import pallas as pl
from jax.experimental.pallas import tpu as pltpu
from jax.experimental.pallas import tpu_sc as plsc

D_MODEL = 1024
BATCH = 16
SEQ = 256
DEPTH = 2
DEC_BATCH = 4
DEC_SEQ = 4096
PAST_LEN = 512
GRID_W = 64

N_HEADS = 8
QK_NOPE = 128
QK_ROPE = 64
V_DIM = 128
Q_LORA = 384
KV_LORA = 256
ROPE_THETA = 10000.0
ATTN_SCALE = (QK_NOPE + QK_ROPE) ** -0.5
HEAD_PAD = 256
Q_PRESCALE = ATTN_SCALE * math.log2(math.e)

N_EXPERTS = 64
TOP_K = 8
N_GROUPS = 8
TOPK_GROUPS = 4
GROUP_SIZE = N_EXPERTS // N_GROUPS
D_EXPERT = 256
D_SHARED = 256
ROUTED_SCALE = 2.5

ALPHA = (2 * DEPTH) ** 0.25
LN_EPS = 1e-5
RMS_EPS = 1e-6

GROUP_ROWS = 4096
N_GROUPS_ROWS = 1 + DEC_BATCH
T_CTX = BATCH * SEQ
T_LAT = DEC_BATCH * DEC_SEQ
T_ALL = T_CTX + T_LAT
LANES = 128
SUBLANES = 8

F32 = jnp.float32
BF16 = jnp.bfloat16
NEG_INF = float("-inf")


def _dot(a, b):
    return jnp.dot(a, b, preferred_element_type=F32)


def _dot_nt(a, b, precision=None):
    return lax.dot_general(a, b, (((1,), (1,)), ((), ())), precision=precision,
                           preferred_element_type=F32)


def _layer_norm(v, g, b):
    mu = jnp.mean(v, axis=-1, keepdims=True)
    d = v - mu
    var = jnp.mean(d * d, axis=-1, keepdims=True)
    return d * lax.rsqrt(var + LN_EPS) * g + b


def _rms_norm(v, g):
    return v * lax.rsqrt(jnp.mean(v * v, axis=-1, keepdims=True) + RMS_EPS) * g


def _silu(v):
    return v / (1.0 + jnp.exp(-v))


def _sigmoid(v):
    return 1.0 / (1.0 + jnp.exp(-v))


def _adaln_kernel(c_ref, w_ref, b_ref, o_ref):
    c = c_ref[...]
    s = _silu(c)
    o_ref[0, 0] = jnp.dot(s, w_ref[0], precision=lax.Precision.HIGHEST,
                          preferred_element_type=F32) + b_ref[0]


def _adaln(cvecs, ada_w, ada_b):
    out = pl.pallas_call(
        _adaln_kernel,
        grid=(DEPTH, 6),
        in_specs=[
            pl.BlockSpec((SUBLANES, D_MODEL), lambda l, j: (0, 0)),
            pl.BlockSpec((1, D_MODEL, D_MODEL), lambda l, j: (l, 0, j)),
            pl.BlockSpec((1, 1, D_MODEL), lambda l, j: (l, 0, j)),
        ],
        out_specs=pl.BlockSpec((1, 1, SUBLANES, D_MODEL), lambda l, j: (l, j, 0, 0)),
        out_shape=jax.ShapeDtypeStruct((DEPTH, 6, SUBLANES, D_MODEL), F32),
        compiler_params=pltpu.CompilerParams(dimension_semantics=("parallel", "parallel")),
        name="adaln",
    )(cvecs, ada_w, ada_b.reshape(DEPTH, 1, 6 * D_MODEL))
    return jnp.transpose(out[:, :, :N_GROUPS_ROWS, :], (0, 2, 1, 3))


CONV_TM = 1024
CONV_PIECES = 2


def _conv_kernel(xc_ref, xcp_ref, xcn_ref, xl_ref, xlp_ref, xln_ref, mods_ref, win_ref, ck_ref, wout_ref,
                 g_ref, b_ref, o_ref):
    i = pl.program_id(0)
    tm = xc_ref.shape[0]
    is_ctx = i < T_CTX // tm
    m = mods_ref[0]
    sm, cm, gm = m[0:1], m[1:2], m[2:3]
    x = jnp.where(is_ctx, xc_ref[...], xl_ref[...])
    xp = jnp.where(is_ctx, xcp_ref[...], xlp_ref[...])
    xn = jnp.where(is_ctx, xcn_ref[...], xln_ref[...])
    def halo_u(xh):
        hh = (xh * (1.0 + cm) + sm).astype(BF16)
        zh = _dot(hh, win_ref[:, D_MODEL:])
        return zh[:, :D_MODEL] * zh[:, D_MODEL:]

    th = tm // CONV_PIECES
    halves = [x[j * th:(j + 1) * th] for j in range(CONV_PIECES)]
    zs = [_dot((xh * (1.0 + cm) + sm).astype(BF16), win_ref[...]) for xh in halves]
    us = [z[:, D_MODEL:2 * D_MODEL] * z[:, 2 * D_MODEL:] for z in zs]
    befores = [halo_u(xp)[SUBLANES - 1:SUBLANES]] + [u[th - 1:th] for u in us[:-1]]
    afters = [u[0:1] for u in us[1:]] + [halo_u(xn)[0:1]]

    ck = ck_ref[...]
    row = lax.broadcasted_iota(jnp.int32, (th, 1), 0)
    for hf in range(CONV_PIECES):
        grow = i * tm + hf * th + row
        seq_len = jnp.where(grow < T_CTX, SEQ, DEC_SEQ)
        pos = jnp.bitwise_and(grow, seq_len - 1)
        u = us[hf]
        left = jnp.where(row == 0, befores[hf], pltpu.roll(u, 1, 0))
        left = jnp.where(pos == 0, 0.0, left)
        right = jnp.where(row == th - 1, afters[hf], pltpu.roll(u, th - 1, 0))
        right = jnp.where(pos == seq_len - 1, 0.0, right)
        conv = left * ck[0:1] + u * ck[1:2] + right * ck[2:3]
        v = (zs[hf][:, :D_MODEL] * conv).astype(BF16)
        y = _dot(v, wout_ref[...])
        o_ref[hf * th:(hf + 1) * th, :] = _layer_norm(ALPHA * halves[hf] + gm * y, g_ref[...], b_ref[...])


def _conv_mixer(x_ctx, x_lat, mods, w_in, conv_k, w_out, ln_g, ln_b):
    tm = CONV_TM
    per8 = tm // SUBLANES
    n_ctx = T_CTX // tm
    n_lat = T_LAT // tm

    def stream_specs(first, n):
        def blk(i):
            return jnp.clip(i - first, 0, n - 1)
        return [
            pl.BlockSpec((tm, D_MODEL), lambda i: (blk(i), 0)),
            pl.BlockSpec((SUBLANES, D_MODEL), lambda i: (jnp.maximum(blk(i) * per8 - 1, 0), 0)),
            pl.BlockSpec((SUBLANES, D_MODEL), lambda i: (jnp.minimum((blk(i) + 1) * per8, n * per8 - 1), 0)),
        ]

    return pl.pallas_call(
        _conv_kernel,
        grid=(n_ctx + n_lat,),
        in_specs=stream_specs(0, n_ctx) + stream_specs(n_ctx, n_lat) + [
            pl.BlockSpec((1, 6, D_MODEL), lambda i: (i * tm // GROUP_ROWS, 0, 0)),
            pl.BlockSpec((D_MODEL, 3 * D_MODEL), lambda i: (0, 0)),
            pl.BlockSpec((3, D_MODEL), lambda i: (0, 0)),
            pl.BlockSpec((D_MODEL, D_MODEL), lambda i: (0, 0)),
            pl.BlockSpec((1, D_MODEL), lambda i: (0, 0)),
            pl.BlockSpec((1, D_MODEL), lambda i: (0, 0)),
        ],
        out_specs=pl.BlockSpec((tm, D_MODEL), lambda i: (i, 0)),
        out_shape=jax.ShapeDtypeStruct((T_ALL, D_MODEL), F32),
        compiler_params=pltpu.CompilerParams(dimension_semantics=("parallel",)),
        name="conv_mixer",
    )(x_ctx, x_ctx, x_ctx, x_lat, x_lat, x_lat, mods, w_in, conv_k, w_out, ln_g, ln_b)


ROUTER_TM = 512
MOE_CHUNKS = ((0, T_ALL),)
COMBINE_EIGHTHS = (2, 2, 2, 2)


def _first_argmax_mask(cur, ridx, n):
    mx = jnp.max(cur, axis=0, keepdims=True)
    first = jnp.min(jnp.where(cur == mx, ridx, n), axis=0, keepdims=True)
    return ridx == first, mx


def _router_kernel(x_ref, mods_ref, rwt_ref, bias_ref, idx_ref, pos_ref, wcol_ref, count_ref, hpa_ref, hpb_ref,
                   carry_ref):
    tm = x_ref.shape[0]
    m = mods_ref[0]
    sf, cf = m[3:4], m[4:5]
    hc = x_ref[...] * (1.0 + cf) + sf
    words = _pack_rows(hc)
    hpa_ref[...] = words[:, :D_QUARTER]
    hpb_ref[...] = words[:, D_QUARTER:]
    logits = _dot_nt(rwt_ref[...], hc, precision=lax.Precision.HIGHEST)
    scores = _sigmoid(logits)
    biased = scores + bias_ref[...]

    ridx8 = lax.broadcasted_iota(jnp.int32, (GROUP_SIZE, tm), 0)
    gscore = jnp.full((N_GROUPS, tm), NEG_INF, F32)
    for g in range(N_GROUPS):
        blk = biased[g * GROUP_SIZE:(g + 1) * GROUP_SIZE]
        sel, m1 = _first_argmax_mask(blk, ridx8, GROUP_SIZE)
        m2 = jnp.max(jnp.where(sel, NEG_INF, blk), axis=0, keepdims=True)
        gscore = jnp.where(ridx8 == g, m1 + m2, gscore)

    gmask = jnp.zeros((N_GROUPS, tm), jnp.bool_)
    cur = gscore
    for _ in range(TOPK_GROUPS):
        sel, _unused = _first_argmax_mask(cur, ridx8, N_GROUPS)
        gmask = jnp.logical_or(gmask, sel)
        cur = jnp.where(sel, NEG_INF, cur)

    gmask_f = gmask.astype(F32)
    blocks = []
    for g in range(N_GROUPS):
        keep = jnp.broadcast_to(gmask_f[g:g + 1], (GROUP_SIZE, tm)) > 0.5
        blocks.append(jnp.where(keep, biased[g * GROUP_SIZE:(g + 1) * GROUP_SIZE], NEG_INF))
    cur = jnp.concatenate(blocks, axis=0)

    first_tiles = [row0 // tm for row0, _rows in MOE_CHUNKS]
    starts_chunk = functools.reduce(jnp.logical_or, [pl.program_id(0) == ft for ft in first_tiles])

    @pl.when(starts_chunk)
    def _():
        carry_ref[...] = jnp.zeros(carry_ref.shape, F32)

    ridx = lax.broadcasted_iota(jnp.int32, (N_EXPERTS, tm), 0)
    kidx = lax.broadcasted_iota(jnp.int32, (TOP_K, tm), 0)
    sels = []
    chosen = jnp.zeros((N_EXPERTS, tm), jnp.bool_)
    idx_rows = jnp.zeros((TOP_K, tm), jnp.int32)
    for k in range(TOP_K):
        mx = jnp.max(cur, axis=0, keepdims=True)
        first = jnp.min(jnp.where(cur == mx, ridx, N_EXPERTS), axis=0, keepdims=True)
        sel = ridx == first
        sels.append(sel)
        chosen = jnp.logical_or(chosen, sel)
        idx_rows = jnp.where(kidx == k, first, idx_rows)
        cur = jnp.where(sel, NEG_INF, cur)

    onehot = chosen.astype(F32)
    t_row = lax.broadcasted_iota(jnp.int32, (tm, tm), 0)
    t_col = lax.broadcasted_iota(jnp.int32, (tm, tm), 1)
    before = (t_row < t_col).astype(BF16)
    rank = carry_ref[...] + _dot(onehot.astype(BF16), before)
    carry_ref[...] = carry_ref[...] + jnp.sum(onehot, axis=1, keepdims=True)
    count_ref[0] = jnp.broadcast_to(carry_ref[...], count_ref.shape[1:])

    w = jnp.where(chosen, scores, 0.0)
    w = w / jnp.sum(w, axis=0, keepdims=True) * ROUTED_SCALE
    pos_rows = jnp.zeros((TOP_K, tm), F32)
    w_rows = jnp.zeros((TOP_K, tm), F32)
    for k in range(TOP_K):
        pos_rows = jnp.where(kidx == k, jnp.sum(jnp.where(sels[k], rank, 0.0), axis=0, keepdims=True), pos_rows)
        w_rows = jnp.where(kidx == k, jnp.sum(jnp.where(sels[k], w, 0.0), axis=0, keepdims=True), w_rows)
    idx_ref[...] = idx_rows
    pos_ref[...] = pos_rows.astype(jnp.int32)
    wpad = jnp.concatenate([w_rows, jnp.zeros((LANES - TOP_K, tm), F32)], axis=0)
    wcol_ref[...] = wpad.T


def _router(x, mods, router_wt, router_bias):
    t = x.shape[0]
    tm = ROUTER_TM

    def chunk_of(i):
        return sum((i >= row0 // tm).astype(jnp.int32) for row0, _rows in MOE_CHUNKS[1:])

    return pl.pallas_call(
        _router_kernel,
        grid=(t // tm,),
        in_specs=[
            pl.BlockSpec((tm, D_MODEL), lambda i: (i, 0)),
            pl.BlockSpec((1, 6, D_MODEL), lambda i: (i * tm // GROUP_ROWS, 0, 0)),
            pl.BlockSpec((N_EXPERTS, D_MODEL), lambda i: (0, 0)),
            pl.BlockSpec((N_EXPERTS, 1), lambda i: (0, 0)),
        ],
        out_specs=[
            pl.BlockSpec((TOP_K, tm), lambda i: (0, i)),
            pl.BlockSpec((TOP_K, tm), lambda i: (0, i)),
            pl.BlockSpec((tm, LANES), lambda i: (i, 0)),
            pl.BlockSpec((1, N_EXPERTS, LANES), lambda i: (chunk_of(i), 0, 0)),
            pl.BlockSpec((tm, D_QUARTER), lambda i: (i, 0)),
            pl.BlockSpec((tm, D_QUARTER), lambda i: (i, 0)),
        ],
        out_shape=[
            jax.ShapeDtypeStruct((TOP_K, t), jnp.int32),
            jax.ShapeDtypeStruct((TOP_K, t), jnp.int32),
            jax.ShapeDtypeStruct((t, LANES), F32),
            jax.ShapeDtypeStruct((len(MOE_CHUNKS), N_EXPERTS, LANES), F32),
            jax.ShapeDtypeStruct((t, D_QUARTER), jnp.uint32),
            jax.ShapeDtypeStruct((t, D_QUARTER), jnp.uint32),
        ],
        scratch_shapes=[pltpu.VMEM((N_EXPERTS, 1), F32)],
        compiler_params=pltpu.CompilerParams(dimension_semantics=("arbitrary",)),
        name="moe_router",
    )(x, mods, router_wt, router_bias)


FFN_TM = 512


def _n_tiles(rows):
    return rows * TOP_K // FFN_TM + N_EXPERTS


def _expert_layout(counts, n_tiles):
    padded = (counts + FFN_TM - 1) // FFN_TM * FFN_TM
    end = jnp.cumsum(padded)
    start = end - padded
    tile_row = jnp.arange(n_tiles, dtype=jnp.int32) * FFN_TM
    tile_expert = jnp.minimum(jnp.sum(end[None, :] <= tile_row[:, None], axis=1), N_EXPERTS - 1)
    of_tile = tile_expert[:, None] == jnp.arange(N_EXPERTS, dtype=jnp.int32)[None, :]
    live_end = jnp.sum(jnp.where(of_tile, (start + counts)[None, :], 0), axis=1)
    tile_valid = jnp.clip(live_end - tile_row, 0, FFN_TM)
    n_used = (end[-1] // FFN_TM).astype(jnp.int32).reshape(1)
    used = jnp.arange(n_tiles, dtype=jnp.int32) < n_used[0]
    prev_expert = jnp.concatenate([jnp.full((1,), -1, tile_expert.dtype), tile_expert[:-1]])
    run_flag = jnp.logical_and(used, tile_expert != prev_expert)
    run_slot = (jnp.cumsum(run_flag.astype(jnp.int32)) - 1) % 2
    later = jnp.logical_and(used[None, :], tile_expert[None, :] > tile_expert[:, None])
    next_expert = jnp.min(jnp.where(later, tile_expert[None, :], N_EXPERTS), axis=1)
    next_expert = jnp.where(next_expert == N_EXPERTS, -1, next_expert)
    plan = (run_flag.astype(jnp.int32), run_slot.astype(jnp.int32), next_expert.astype(jnp.int32))
    return start.astype(jnp.int32), tile_expert.astype(jnp.int32), tile_valid.astype(jnp.int32), n_used, plan


D_HALF = D_MODEL // 2
D_QUARTER = D_MODEL // 4


def _pack_rows(v):
    hi = lax.bitcast_convert_type(v[:, :D_HALF].astype(BF16).astype(F32), jnp.uint32)
    lo = lax.bitcast_convert_type(v[:, D_HALF:].astype(BF16).astype(F32), jnp.uint32)
    return jnp.bitwise_or(hi, jnp.right_shift(lo, jnp.uint32(16)))


def _unpack_rows(w):
    hi = lax.bitcast_convert_type(jnp.bitwise_and(w, jnp.uint32(0xFFFF0000)), F32)
    lo = lax.bitcast_convert_type(jnp.left_shift(w, jnp.uint32(16)), F32)
    return hi, lo


DEST_TM = 2048


def _dest_kernel(start_ref, idx_ref, pos_ref, dest_ref):
    idx = idx_ref[...]
    base = jnp.zeros(idx.shape, jnp.int32)
    for e in range(N_EXPERTS):
        base = jnp.where(idx == e, start_ref[e], base)
    dest_ref[...] = base + pos_ref[...]


def _dest_rows(start, idx, pos):
    t = idx.shape[1]
    return pl.pallas_call(
        _dest_kernel,
        grid_spec=pltpu.PrefetchScalarGridSpec(
            num_scalar_prefetch=1,
            grid=(t // DEST_TM,),
            in_specs=[pl.BlockSpec((TOP_K, DEST_TM), lambda i, s: (0, i)),
                      pl.BlockSpec((TOP_K, DEST_TM), lambda i, s: (0, i))],
            out_specs=pl.BlockSpec((TOP_K, DEST_TM), lambda i, s: (0, i)),
        ),
        out_shape=jax.ShapeDtypeStruct((TOP_K, t), jnp.int32),
        compiler_params=pltpu.CompilerParams(dimension_semantics=("parallel",)),
        name="moe_dest",
    )(start, idx, pos)


SC_WINDOW = 128


def _sc_mesh():
    return plsc.VectorSubcoreMesh(core_axis_name="c", subcore_axis_name="s")


def _sc_scatter_rows(x, dest, row0, n_sorted):
    t = dest.shape[1]
    blk0 = row0 // SC_WINDOW

    @functools.partial(
        pl.kernel,
        out_type=jax.ShapeDtypeStruct((n_sorted, D_QUARTER), x.dtype),
        mesh=_sc_mesh(),
        scratch_types=[],
    )
    def scatter(x_hbm, i_hbm, o_hbm):
        def body(x_vmem, i_vmem):
            pltpu.sync_copy(x_vmem, o_hbm.at[i_vmem.at[0]])

        pltpu.emit_pipeline(
            body,
            grid=(t // SC_WINDOW, TOP_K),
            in_specs=[pl.BlockSpec((SC_WINDOW, D_QUARTER), lambda i, k: (i + blk0, 0)),
                      pl.BlockSpec((1, SC_WINDOW), lambda i, k: (k, i))],
            out_specs=[],
            core_axis_name=("c", "s"),
            dimension_semantics=(pltpu.PARALLEL, pltpu.ARBITRARY),
        )(x_hbm, i_hbm)

    return scatter(x, dest)


def _sc_gather_rows(table, idx):
    m = idx.shape[1]

    @functools.partial(
        pl.kernel,
        out_type=jax.ShapeDtypeStruct((m, D_QUARTER), table.dtype),
        mesh=_sc_mesh(),
        scratch_types=[],
    )
    def gather(t_hbm, i_hbm, o_hbm):
        def body(i_vmem, o_vmem):
            pltpu.sync_copy(t_hbm.at[i_vmem.at[0]], o_vmem)

        pltpu.emit_pipeline(
            body,
            grid=(m // SC_WINDOW,),
            in_specs=[pl.BlockSpec((1, SC_WINDOW), lambda i: (0, i))],
            out_specs=[pl.BlockSpec((SC_WINDOW, D_QUARTER), lambda i: (i, 0))],
            core_axis_name=("c", "s"),
            dimension_semantics=(pltpu.PARALLEL,),
        )(i_hbm, o_hbm)

    return gather(table, idx)


FFN_STEP_TILES = 2


def _ffn_kernel(layer, te_ref, tv_ref, nu_ref, flag_ref, slot_ref, next_ref, xa_ref, xb_ref,
                wg_hbm, wu_hbm, wd_hbm, ya_ref, yb_ref, wg_st, wu_st, wd_st, wgb_ref, wub_ref, wdb_ref, sem):
    step = pl.program_id(0)

    def fetch(expert, slot):
        return [pltpu.make_async_copy(src.at[layer, expert], dst.at[slot], sem.at[slot])
                for src, dst in ((wg_hbm, wg_st), (wu_hbm, wu_st), (wd_hbm, wd_st))]

    for s in range(FFN_STEP_TILES):
        tile = step * FFN_STEP_TILES + s

        @pl.when(jnp.logical_and(tile < nu_ref[0], flag_ref[tile] == 1))
        def _():
            slot = slot_ref[tile]

            @pl.when(tile == 0)
            def _():
                for cp in fetch(te_ref[tile], slot):
                    cp.start()

            for cp in fetch(te_ref[tile], slot):
                cp.wait()
            wgb_ref[slot] = wg_st[slot].astype(BF16)
            wub_ref[slot] = wu_st[slot].astype(BF16)
            wdb_ref[slot] = wd_st[slot].astype(BF16)

            @pl.when(next_ref[tile] >= 0)
            def _():
                for cp in fetch(next_ref[tile], 1 - slot):
                    cp.start()

    @pl.when(step * FFN_STEP_TILES < nu_ref[0])
    def _():
        for s in range(FFN_STEP_TILES):
            tile = step * FFN_STEP_TILES + s
            slot = slot_ref[tile]
            wg = wgb_ref[slot]
            wu = wub_ref[slot]
            wd = wdb_ref[slot]
            rows = pl.ds(s * FFN_TM, FFN_TM)
            live = lax.broadcasted_iota(jnp.int32, (FFN_TM, D_QUARTER), 0) < tv_ref[tile]
            hi_a, lo_a = _unpack_rows(jnp.where(live, xa_ref[rows, :], jnp.uint32(0)))
            hi_b, lo_b = _unpack_rows(jnp.where(live, xb_ref[rows, :], jnp.uint32(0)))
            xb = jnp.concatenate([hi_a.astype(BF16), hi_b.astype(BF16), lo_a.astype(BF16), lo_b.astype(BF16)],
                                 axis=1)
            a = (_silu(_dot(xb, wg)) * _dot(xb, wu)).astype(BF16)
            words = _pack_rows(_dot(a, wd))
            ya_ref[rows, :] = words[:, :D_QUARTER]
            yb_ref[rows, :] = words[:, D_QUARTER:]

    @pl.when(step * FFN_STEP_TILES >= nu_ref[0])
    def _():
        ya_ref[...] = jnp.zeros(ya_ref.shape, jnp.uint32)
        yb_ref[...] = jnp.zeros(yb_ref.shape, jnp.uint32)


def _expert_ffn(xs_a, xs_b, tile_expert, tile_valid, n_used, plan, wg, wu, wd, layer):
    n_tiles = xs_a.shape[0] // FFN_TM
    step_rows = FFN_STEP_TILES * FFN_TM

    def row_map(i, te, tv, nu, fl, sl, nx):
        return (jnp.minimum(i, (nu[0] - 1) // FFN_STEP_TILES), 0)

    def out_map(i, te, tv, nu, fl, sl, nx):
        return (i, 0)

    hbm = pl.BlockSpec(memory_space=pl.ANY)
    return pl.pallas_call(
        functools.partial(_ffn_kernel, layer),
        grid_spec=pltpu.PrefetchScalarGridSpec(
            num_scalar_prefetch=6,
            grid=(n_tiles // FFN_STEP_TILES,),
            in_specs=[pl.BlockSpec((step_rows, D_QUARTER), row_map),
                      pl.BlockSpec((step_rows, D_QUARTER), row_map), hbm, hbm, hbm],
            out_specs=[pl.BlockSpec((step_rows, D_QUARTER), out_map), pl.BlockSpec((step_rows, D_QUARTER), out_map)],
            scratch_shapes=[pltpu.VMEM((2, D_MODEL, D_EXPERT), F32), pltpu.VMEM((2, D_MODEL, D_EXPERT), F32),
                            pltpu.VMEM((2, D_EXPERT, D_MODEL), F32),
                            pltpu.VMEM((2, D_MODEL, D_EXPERT), BF16), pltpu.VMEM((2, D_MODEL, D_EXPERT), BF16),
                            pltpu.VMEM((2, D_EXPERT, D_MODEL), BF16), pltpu.SemaphoreType.DMA((2,))],
        ),
        out_shape=[jax.ShapeDtypeStruct((n_tiles * FFN_TM, D_QUARTER), jnp.uint32),
                   jax.ShapeDtypeStruct((n_tiles * FFN_TM, D_QUARTER), jnp.uint32)],
        compiler_params=pltpu.CompilerParams(dimension_semantics=("arbitrary",)),
        name="moe_expert_ffn",
    )(tile_expert, tile_valid, n_used, *plan, xs_a, xs_b, wg, wu, wd)


COMBINE_TM = 512


def _combine_kernel(n_prev, tile0, x_ref, mods_ref, wcol_ref, ga_ref, gb_ref, sg_ref, su_ref, sd_ref, g_ref, b_ref,
                    *rest):
    o_refs = rest[n_prev:]
    m = mods_ref[0]
    sf, cf, gf = m[3:4], m[4:5], m[5:6]
    x = x_ref[...]
    hc = (x * (1.0 + cf) + sf).astype(BF16)
    a = _silu(_dot(hc, sg_ref[...])) * _dot(hc, su_ref[...])
    y = _dot(a.astype(BF16), sd_ref[...])
    wcol = wcol_ref[...]
    parts = [y[:, q * D_QUARTER:(q + 1) * D_QUARTER] for q in range(4)]
    for k in range(TOP_K):
        hi_a, lo_a = _unpack_rows(ga_ref[k])
        hi_b, lo_b = _unpack_rows(gb_ref[k])
        wk = wcol[:, k:k + 1]
        parts = [parts[0] + wk * hi_a, parts[1] + wk * hi_b, parts[2] + wk * lo_a, parts[3] + wk * lo_b]
    y = jnp.concatenate(parts, axis=1)
    out = _layer_norm(ALPHA * x + gf * y, g_ref[...], b_ref[...])
    if len(o_refs) == 1:
        o_refs[0][...] = out
    else:
        is_ctx = pl.program_id(0) + tile0 < T_CTX // x_ref.shape[0]

        @pl.when(is_ctx)
        def _():
            o_refs[0][...] = out

        @pl.when(jnp.logical_not(is_ctx))
        def _():
            o_refs[1][...] = out


def _combine(x, mods, wcol, g_a, g_b, sg, su, sd, ln_g, ln_b, row0, split_streams, prev):
    tm = COMBINE_TM
    n = g_a.shape[1] // tm
    tile0 = row0 // tm
    n_ctx = T_CTX // tm
    full = lambda shape: pl.BlockSpec(shape, lambda i: (0,) * len(shape))
    layouts = {
        "all": (lambda i: (tile0 + i, 0), T_ALL),
        "ctx": (lambda i: (jnp.minimum(tile0 + i, n_ctx - 1), 0), T_CTX),
        "lat": (lambda i: (jnp.maximum(tile0 + i - n_ctx, 0), 0), T_LAT),
    }
    if split_streams:
        kinds = (["ctx"] if tile0 < n_ctx else []) + (["lat"] if tile0 + n > n_ctx else [])
    else:
        kinds = ["all"]
    carried = [kd for kd in kinds if kd in prev]
    outs = pl.pallas_call(
        functools.partial(_combine_kernel, len(carried), tile0),
        grid=(n,),
        in_specs=[
            pl.BlockSpec((tm, D_MODEL), lambda i: (tile0 + i, 0)),
            pl.BlockSpec((1, 6, D_MODEL), lambda i: ((tile0 + i) * tm // GROUP_ROWS, 0, 0)),
            pl.BlockSpec((tm, LANES), lambda i: (tile0 + i, 0)),
            pl.BlockSpec((TOP_K, tm, D_QUARTER), lambda i: (0, i, 0)),
            pl.BlockSpec((TOP_K, tm, D_QUARTER), lambda i: (0, i, 0)),
            full((D_MODEL, D_SHARED)), full((D_MODEL, D_SHARED)), full((D_SHARED, D_MODEL)),
            full((1, D_MODEL)), full((1, D_MODEL)),
        ] + [pl.BlockSpec(memory_space=pl.ANY) for _ in carried],
        out_specs=[pl.BlockSpec((tm, D_MODEL), layouts[kd][0]) for kd in kinds],
        out_shape=[jax.ShapeDtypeStruct((layouts[kd][1], D_MODEL), F32) for kd in kinds],
        input_output_aliases={10 + j: kinds.index(kd) for j, kd in enumerate(carried)},
        compiler_params=pltpu.CompilerParams(dimension_semantics=("arbitrary",)),
        name="moe_combine",
    )(x, mods, wcol, g_a, g_b, sg, su, sd, ln_g, ln_b, *[prev[kd] for kd in carried])
    return {**prev, **dict(zip(kinds, outs))}


MLA_TM = 1024


def _mla_proj_kernel(x_ref, mods_ref, wdq_ref, wdkv_ref, wkr_ref, qn_ref, kvn_ref, wuq_ref,
                     wukn_ref, wuv_ref, ta_ref, tb_ref, q_ref, k_ref, v_ref, ckv_ref, kr_ref):
    m = mods_ref[0]
    sm, cm = m[0:1], m[1:2]
    h = (x_ref[...] * (1.0 + cm) + sm).astype(BF16)
    cq = _rms_norm(_dot(h, wdq_ref[...]), qn_ref[...])
    ckv = _rms_norm(_dot(h, wdkv_ref[...]), kvn_ref[...])
    kr2 = _dot(h, wkr_ref[...])
    ckv_ref[...] = ckv
    kr_ref[...] = kr2

    ka = ta_ref[0]
    kb = tb_ref[0]
    tm = ka.shape[0]
    ta = jnp.concatenate([jnp.full((tm, QK_NOPE), Q_PRESCALE, F32), ka * Q_PRESCALE], axis=1)
    tb = jnp.concatenate([jnp.zeros((tm, QK_NOPE), F32), kb * Q_PRESCALE], axis=1)
    krr = kr2 * ka + pltpu.roll(kr2, QK_ROPE, 1) * kb

    qpre = _dot(cq.astype(BF16), wuq_ref[...])
    ckv_b = ckv.astype(BF16)
    kn = _dot(ckv_b, wukn_ref[...])
    v_ref[...] = _dot(ckv_b, wuv_ref[...]).astype(BF16)
    for hd in range(N_HEADS):
        qh = qpre[:, hd * HEAD_PAD:(hd + 1) * HEAD_PAD]
        qrot = qh * ta + pltpu.roll(qh, HEAD_PAD - QK_ROPE, 1) * tb
        q_ref[:, hd * HEAD_PAD:(hd + 1) * HEAD_PAD] = qrot.astype(BF16)
        k_ref[:, hd * HEAD_PAD:hd * HEAD_PAD + QK_NOPE] = kn[:, hd * QK_NOPE:(hd + 1) * QK_NOPE].astype(BF16)
        k_ref[:, hd * HEAD_PAD + QK_NOPE:(hd + 1) * HEAD_PAD] = krr.astype(BF16)


def _mla_proj(x, mods, p, rope_a, rope_b):
    t = x.shape[0]
    tm = MLA_TM
    full = lambda shape: pl.BlockSpec(shape, lambda i: (0,) * len(shape))
    rope_spec = pl.BlockSpec(
        (1, tm, 2 * QK_ROPE),
        lambda i: (jnp.minimum(i * tm // GROUP_ROWS, 1), (i * tm % GROUP_ROWS) // tm, 0))
    return pl.pallas_call(
        _mla_proj_kernel,
        grid=(t // tm,),
        in_specs=[
            pl.BlockSpec((tm, D_MODEL), lambda i: (i, 0)),
            pl.BlockSpec((1, 6, D_MODEL), lambda i: (i * tm // GROUP_ROWS, 0, 0)),
            full((D_MODEL, Q_LORA)), full((D_MODEL, KV_LORA)), full((D_MODEL, 2 * QK_ROPE)),
            full((1, Q_LORA)), full((1, KV_LORA)),
            full((Q_LORA, N_HEADS * HEAD_PAD)),
            full((KV_LORA, N_HEADS * QK_NOPE)), full((KV_LORA, N_HEADS * V_DIM)),
            rope_spec, rope_spec,
        ],
        out_specs=[
            pl.BlockSpec((tm, N_HEADS * HEAD_PAD), lambda i: (i, 0)),
            pl.BlockSpec((tm, N_HEADS * HEAD_PAD), lambda i: (i, 0)),
            pl.BlockSpec((tm, N_HEADS * V_DIM), lambda i: (i, 0)),
            pl.BlockSpec((tm, KV_LORA), lambda i: (i, 0)),
            pl.BlockSpec((tm, 2 * QK_ROPE), lambda i: (i, 0)),
        ],
        out_shape=[
            jax.ShapeDtypeStruct((t, N_HEADS * HEAD_PAD), BF16),
            jax.ShapeDtypeStruct((t, N_HEADS * HEAD_PAD), BF16),
            jax.ShapeDtypeStruct((t, N_HEADS * V_DIM), BF16),
            jax.ShapeDtypeStruct((t, KV_LORA), F32),
            jax.ShapeDtypeStruct((t, 2 * QK_ROPE), F32),
        ],
        compiler_params=pltpu.CompilerParams(dimension_semantics=("parallel",)),
        name="mla_proj",
    )(x, mods, p["w_dq"], p["w_dkv"], p["w_kr"], p["q_norm"], p["kv_norm"], p["w_uq"],
      p["w_ukn"], p["w_uv"], rope_a, rope_b)


def _cache_kv_kernel(ckv_ref, kr_ref, wukn_ref, wuv_ref, k_ref, v_ref):
    ckv_b = ckv_ref[...].astype(BF16)
    kn = _dot(ckv_b, wukn_ref[...])
    v_ref[...] = _dot(ckv_b, wuv_ref[...]).astype(BF16)
    kr = kr_ref[...].astype(BF16)
    for hd in range(N_HEADS):
        k_ref[:, hd * HEAD_PAD:hd * HEAD_PAD + QK_NOPE] = kn[:, hd * QK_NOPE:(hd + 1) * QK_NOPE].astype(BF16)
        k_ref[:, hd * HEAD_PAD + QK_NOPE:(hd + 1) * HEAD_PAD] = kr


def _cache_kv(ckv, kr_pad, p):
    t = ckv.shape[0]
    tm = PAST_LEN
    full = lambda shape: pl.BlockSpec(shape, lambda i: (0,) * len(shape))
    return pl.pallas_call(
        _cache_kv_kernel,
        grid=(t // tm,),
        in_specs=[
            pl.BlockSpec((tm, KV_LORA), lambda i: (i, 0)),
            pl.BlockSpec((tm, 2 * QK_ROPE), lambda i: (i, 0)),
            full((KV_LORA, N_HEADS * QK_NOPE)), full((KV_LORA, N_HEADS * V_DIM)),
        ],
        out_specs=[
            pl.BlockSpec((tm, N_HEADS * HEAD_PAD), lambda i: (i, 0)),
            pl.BlockSpec((tm, N_HEADS * V_DIM), lambda i: (i, 0)),
        ],
        out_shape=[
            jax.ShapeDtypeStruct((t, N_HEADS * HEAD_PAD), BF16),
            jax.ShapeDtypeStruct((t, N_HEADS * V_DIM), BF16),
        ],
        compiler_params=pltpu.CompilerParams(dimension_semantics=("parallel",)),
        name="mla_cache_kv",
    )(ckv, kr_pad, p["w_ukn"], p["w_uv"])


def _ctx_attn_kernel(q_ref, k_ref, v_ref, o_ref):
    for hd in range(N_HEADS):
        q = q_ref[:, hd * HEAD_PAD:(hd + 1) * HEAD_PAD]
        k = k_ref[:, hd * HEAD_PAD:(hd + 1) * HEAD_PAD]
        s = _dot_nt(q, k)
        s = s - jnp.max(s, axis=-1, keepdims=True)
        p = jnp.exp2(s)
        p = p / jnp.sum(p, axis=-1, keepdims=True)
        o = _dot(p.astype(BF16), v_ref[:, hd * V_DIM:(hd + 1) * V_DIM])
        o_ref[:, hd * V_DIM:(hd + 1) * V_DIM] = o.astype(BF16)


def _ctx_attention(q, k, v):
    return pl.pallas_call(
        _ctx_attn_kernel,
        grid=(BATCH,),
        in_specs=[
            pl.BlockSpec((SEQ, N_HEADS * HEAD_PAD), lambda b: (b, 0)),
            pl.BlockSpec((SEQ, N_HEADS * HEAD_PAD), lambda b: (b, 0)),
            pl.BlockSpec((SEQ, N_HEADS * V_DIM), lambda b: (b, 0)),
        ],
        out_specs=pl.BlockSpec((SEQ, N_HEADS * V_DIM), lambda b: (b, 0)),
        out_shape=jax.ShapeDtypeStruct((T_CTX, N_HEADS * V_DIM), BF16),
        compiler_params=pltpu.CompilerParams(dimension_semantics=("parallel",)),
        name="ctx_attention",
    )(q, k, v)


LAT_TQ = 1024
LAT_TK = 512
LAT_PIECES = 4


def _lat_attn_kernel(q_ref, k_ref, v_ref, kc_ref, vc_ref, o_ref, s_ref, p_ref, m_ref):
    n_chunks = DEC_SEQ // LAT_TK + 1
    tp = q_ref.shape[0] // LAT_PIECES
    pieces = [pl.ds(j * tp, tp) for j in range(LAT_PIECES)]
    groups = [pieces[:2], pieces[2:]]
    state = {}

    def keys(c):
        return kc_ref[...] if c == n_chunks - 1 else k_ref[c * LAT_TK:(c + 1) * LAT_TK, :]

    def values(c):
        return vc_ref[...] if c == n_chunks - 1 else v_ref[c * LAT_TK:(c + 1) * LAT_TK, :]

    def qk(r, c):
        s = _dot_nt(q_ref[r, :], keys(c))
        s_ref[r, c * LAT_TK:(c + 1) * LAT_TK] = s
        mp = state.get(("m", r.start), jnp.full((tp, LANES), NEG_INF, F32))
        for j in range(LAT_TK // LANES):
            mp = jnp.maximum(mp, s[:, j * LANES:(j + 1) * LANES])
        state[("m", r.start)] = mp

    def row_max(r):
        m_ref[r, :] = jnp.broadcast_to(jnp.max(state[("m", r.start)], axis=-1, keepdims=True), (tp, LANES))

    def exp_chunk(r, c, after=None):
        m = m_ref[r, :]
        if after is not None:
            bits = lax.bitcast_convert_type(after[:, :LANES], jnp.uint32)
            zero = lax.shift_right_logical(lax.shift_right_logical(bits, jnp.uint32(16)), jnp.uint32(16))
            m = m + lax.bitcast_convert_type(zero, F32)
        for j in range(c * LAT_TK // LANES, (c + 1) * LAT_TK // LANES):
            p = jnp.exp2(s_ref[r, j * LANES:(j + 1) * LANES] - m)
            p_ref[r, j * LANES:(j + 1) * LANES] = p.astype(BF16)

    def pv(r, c):
        v = values(c)
        v_and_ones = jnp.concatenate([v, jnp.ones_like(v)], axis=1)
        acc = state.get(("a", r.start), jnp.zeros((tp, 2 * V_DIM), F32))
        state[("a", r.start)] = acc + _dot(p_ref[r, c * LAT_TK:(c + 1) * LAT_TK], v_and_ones)

    def finish(r):
        acc = state[("a", r.start)]
        o_ref[r, :] = (acc[:, :V_DIM] / acc[:, V_DIM:]).astype(BF16)

    for c in range(n_chunks):
        for r in groups[0]:
            qk(r, c)
    for r in groups[0]:
        row_max(r)
    for c in range(n_chunks):
        for r in groups[1]:
            qk(r, c)
        for r in groups[0]:
            exp_chunk(r, c)
    for r in groups[1]:
        row_max(r)
    for c in range(n_chunks):
        for r in groups[0]:
            pv(r, c)
        for r0, r in zip(groups[0], groups[1]):
            exp_chunk(r, c, after=state[("a", r0.start)])
    for r in groups[0]:
        finish(r)
    for c in range(n_chunks):
        for r in groups[1]:
            pv(r, c)
    for r in groups[1]:
        finish(r)


def _lat_attention(q, k, v, kc, vc):
    nq = DEC_SEQ // LAT_TQ
    return pl.pallas_call(
        _lat_attn_kernel,
        grid=(DEC_BATCH, N_HEADS, nq),
        in_specs=[
            pl.BlockSpec((LAT_TQ, HEAD_PAD), lambda b, h, i: ((b + 1) * nq + i, h)),
            pl.BlockSpec((DEC_SEQ, HEAD_PAD), lambda b, h, i: (b + 1, h)),
            pl.BlockSpec((DEC_SEQ, V_DIM), lambda b, h, i: (b + 1, h)),
            pl.BlockSpec((PAST_LEN, HEAD_PAD), lambda b, h, i: (b, h)),
            pl.BlockSpec((PAST_LEN, V_DIM), lambda b, h, i: (b, h)),
        ],
        out_specs=pl.BlockSpec((LAT_TQ, V_DIM), lambda b, h, i: (b * nq + i, h)),
        out_shape=jax.ShapeDtypeStruct((T_LAT, N_HEADS * V_DIM), BF16),
        scratch_shapes=[pltpu.VMEM((LAT_TQ, DEC_SEQ + PAST_LEN), F32),
                        pltpu.VMEM((LAT_TQ, DEC_SEQ + PAST_LEN), BF16),
                        pltpu.VMEM((LAT_TQ, LANES), F32)],
        compiler_params=pltpu.CompilerParams(
            dimension_semantics=("parallel", "parallel", "parallel")),
        name="lat_attention",
    )(q, k, v, kc, vc)


OPROJ_TM = 1024


def _oproj_kernel(oc_ref, ol_ref, x_ref, mods_ref, wo_ref, g_ref, b_ref, out_ref):
    is_ctx = pl.program_id(0) < T_CTX // OPROJ_TM
    o = jnp.where(is_ctx, oc_ref[...], ol_ref[...])
    gm = mods_ref[0][2:3]
    y = _dot(o, wo_ref[...])
    out_ref[...] = _layer_norm(ALPHA * x_ref[...] + gm * y, g_ref[...], b_ref[...])


def _oproj(o_ctx, o_lat, x, mods, w_o, ln_g, ln_b):
    t = x.shape[0]
    tm = OPROJ_TM
    n_ctx = T_CTX // tm
    return pl.pallas_call(
        _oproj_kernel,
        grid=(t // tm,),
        in_specs=[
            pl.BlockSpec((tm, N_HEADS * V_DIM), lambda i: (jnp.minimum(i, n_ctx - 1), 0)),
            pl.BlockSpec((tm, N_HEADS * V_DIM), lambda i: (jnp.maximum(i - n_ctx, 0), 0)),
            pl.BlockSpec((tm, D_MODEL), lambda i: (i, 0)),
            pl.BlockSpec((1, 6, D_MODEL), lambda i: (i * tm // GROUP_ROWS, 0, 0)),
            pl.BlockSpec((N_HEADS * V_DIM, D_MODEL), lambda i: (0, 0)),
            pl.BlockSpec((1, D_MODEL), lambda i: (0, 0)),
            pl.BlockSpec((1, D_MODEL), lambda i: (0, 0)),
        ],
        out_specs=pl.BlockSpec((tm, D_MODEL), lambda i: (i, 0)),
        out_shape=jax.ShapeDtypeStruct((t, D_MODEL), F32),
        compiler_params=pltpu.CompilerParams(dimension_semantics=("parallel",)),
        name="attn_oproj",
    )(o_ctx, o_lat, x, mods, w_o, ln_g, ln_b)


def _swap16(w):
    q = QK_ROPE // 4
    return jnp.concatenate([w[..., q:2 * q], w[..., :q], w[..., 3 * q:], w[..., 2 * q:3 * q]], axis=-1)


def _mla_params(w_dqkv, q_norm, w_uq, kv_norm, w_ukv):
    w_kr = w_dqkv[:, Q_LORA + KV_LORA:]
    wq = w_uq.reshape(Q_LORA, N_HEADS, QK_NOPE + QK_ROPE)
    wq_r = wq[..., QK_NOPE:]
    wq = jnp.concatenate([wq[..., :QK_NOPE], wq_r, _swap16(wq_r)], axis=-1)
    wkv = w_ukv.reshape(KV_LORA, N_HEADS, QK_NOPE + V_DIM)
    return {
        "w_dq": w_dqkv[:, :Q_LORA].astype(BF16),
        "w_dkv": w_dqkv[:, Q_LORA:Q_LORA + KV_LORA].astype(BF16),
        "w_kr": jnp.concatenate([w_kr, _swap16(w_kr)], axis=-1).astype(BF16),
        "q_norm": q_norm.reshape(1, Q_LORA),
        "kv_norm": kv_norm.reshape(1, KV_LORA),
        "w_uq": wq.reshape(Q_LORA, N_HEADS * HEAD_PAD).astype(BF16),
        "w_ukn": wkv[..., :QK_NOPE].reshape(KV_LORA, N_HEADS * QK_NOPE).astype(BF16),
        "w_uv": wkv[..., QK_NOPE:].reshape(KV_LORA, N_HEADS * V_DIM).astype(BF16),
    }


def _rope_tables():
    nf = QK_ROPE // 4
    t = np.arange(DEC_SEQ)
    row = (t // GRID_W).astype(np.float32)
    col = (t % GRID_W).astype(np.float32)
    inv = (ROPE_THETA ** (-np.arange(nf, dtype=np.float32) / nf)).astype(np.float32)
    ar, ac = row[:, None] * inv, col[:, None] * inv
    pad = np.zeros((DEC_SEQ, QK_ROPE), np.float32)
    cos = np.concatenate([np.cos(ar), np.cos(ar), np.cos(ac), np.cos(ac), pad], axis=-1)
    sin = np.concatenate([-np.sin(ar), np.sin(ar), -np.sin(ac), np.sin(ac), pad], axis=-1)
    cos_id = np.concatenate([np.ones((DEC_SEQ, QK_ROPE), np.float32), pad], axis=-1)
    sin_id = np.zeros((DEC_SEQ, 2 * QK_ROPE), np.float32)
    return (jnp.asarray(np.stack([cos_id, cos]).astype(np.float32)),
            jnp.asarray(np.stack([sin_id, sin]).astype(np.float32)))


def kernel(x_prompt, x_sample, cache_ckv, cache_krope, c, c_ctx, ada_w, ada_b, ln_g, ln_b, conv_w_in, conv_k, conv_w_out, mla_w_dqkv, mla_q_norm, mla_w_uq, mla_kv_norm, mla_w_ukv, mla_w_o, router_w, router_bias, exp_w_gate, exp_w_up, exp_w_down, sh_w_gate, sh_w_up, sh_w_down):
    cvecs =jnp.concatenate([c_ctx[None, :], c, jnp.zeros((SUBLANES - N_GROUPS_ROWS, D_MODEL), F32)], axis=0)
    mods = _adaln(cvecs, ada_w, ada_b)

    def ln(l, k):
        return ln_g[l, k].reshape(1, D_MODEL), ln_b[l, k].reshape(1, D_MODEL)

    def moe_layer(xin, l, split_streams):
        idx, pos, wcol, counts, hp_a, hp_b = _router(xin, mods[l], router_w[l].T,
                                                     router_bias[l].reshape(N_EXPERTS, 1))
        shared = (sh_w_gate[l].astype(BF16), sh_w_up[l].astype(BF16), sh_w_down[l].astype(BF16))
        outs = {}
        for ch in sorted(range(len(MOE_CHUNKS)), key=lambda j: -MOE_CHUNKS[j][1]):
            row0, rows = MOE_CHUNKS[ch]
            n_tiles = _n_tiles(rows)
            cnt = counts[ch, :, 0].astype(jnp.int32)
            start, tile_expert, tile_valid, n_used, plan = _expert_layout(cnt, n_tiles)
            dest = _dest_rows(start, idx[:, row0:row0 + rows], pos[:, row0:row0 + rows])
            xs_a = _sc_scatter_rows(hp_a, dest, row0, n_tiles * FFN_TM)
            xs_b = _sc_scatter_rows(hp_b, dest, row0, n_tiles * FFN_TM)
            ys_a, ys_b = _expert_ffn(xs_a, xs_b, tile_expert, tile_valid, n_used, plan,
                                     exp_w_gate, exp_w_up, exp_w_down, l)
            first = 0
            for eighths in COMBINE_EIGHTHS:
                piece = rows * eighths // 8
                dest_row = dest[:, first:first + piece].reshape(1, piece * TOP_K)
                g_a = _sc_gather_rows(ys_a, dest_row).reshape(TOP_K, piece, D_QUARTER)
                g_b = _sc_gather_rows(ys_b, dest_row).reshape(TOP_K, piece, D_QUARTER)
                outs = _combine(xin, mods[l], wcol, g_a, g_b, *shared, *ln(l, 1), row0=row0 + first,
                                split_streams=split_streams, prev=outs)
                first += piece
        return (outs["ctx"], outs["lat"]) if split_streams else outs["all"]

    x = _conv_mixer(x_prompt.reshape(T_CTX, D_MODEL), x_sample.reshape(T_LAT, D_MODEL), mods[0],
                    conv_w_in[0].astype(BF16), conv_k[0], conv_w_out[0].astype(BF16), *ln(0, 0))
    x = moe_layer(x, 0, split_streams=False)

    p = _mla_params(mla_w_dqkv[0], mla_q_norm[0], mla_w_uq[0], mla_kv_norm[0], mla_w_ukv[0])
    rope_a, rope_b = _rope_tables()
    q, k, v, ckv, kr = _mla_proj(x, mods[1], p, rope_a, rope_b)
    kr_cache = jnp.concatenate([cache_krope[:, 0].reshape(DEC_BATCH * PAST_LEN, QK_ROPE),
                                jnp.zeros((DEC_BATCH * PAST_LEN, QK_ROPE), F32)], axis=-1)
    kc, vc = _cache_kv(cache_ckv[:, 0].reshape(DEC_BATCH * PAST_LEN, KV_LORA), kr_cache, p)
    o_ctx = _ctx_attention(q, k, v)
    o_lat = _lat_attention(q, k, v, kc, vc)
    x = _oproj(o_ctx, o_lat, x, mods[1], mla_w_o[0].astype(BF16), *ln(1, 0))
    y_ctx, y_lat = moe_layer(x, 1, split_streams=True)

    y_prompt = y_ctx.reshape(BATCH, SEQ, D_MODEL)
    y_sample = y_lat.reshape(DEC_BATCH, DEC_SEQ, D_MODEL)
    state_ckv = ckv[:T_CTX].reshape(BATCH, 1, SEQ, KV_LORA)
    state_krope = kr[:T_CTX, :QK_ROPE].reshape(BATCH, 1, SEQ, QK_ROPE)
    return (y_prompt, y_sample, state_ckv, state_krope)
```

```python
import functools
import math

import jax
import jax.numpy as jnp
import numpy as np
from jax import lax
from jax.experimental import pallas as pl
from jax.experimental.pallas import tpu as pltpu
from jax.experimental.pallas import tpu_sc as plsc

D_MODEL = 1024
BATCH = 16
SEQ = 256
DEPTH = 2
DEC_BATCH = 4
DEC_SEQ = 4096
PAST_LEN = 512
GRID_W = 64

N_HEADS = 8
QK_NOPE = 128
QK_ROPE = 64
V_DIM = 128
Q_LORA = 384
KV_LORA = 256
ROPE_THETA = 10000.0
ATTN_SCALE = (QK_NOPE + QK_ROPE) ** -0.5
HEAD_PAD = 256
Q_PRESCALE = ATTN_SCALE * math.log2(math.e)

N_EXPERTS = 64
TOP_K = 8
N_GROUPS = 8
TOPK_GROUPS = 4
GROUP_SIZE = N_EXPERTS // N_GROUPS
D_EXPERT = 256
D_SHARED = 256
ROUTED_SCALE = 2.5

ALPHA = (2 * DEPTH) ** 0.25
LN_EPS = 1e-5
RMS_EPS = 1e-6

GROUP_ROWS = 4096
N_GROUPS_ROWS = 1 + DEC_BATCH
T_CTX = BATCH * SEQ
T_LAT = DEC_BATCH * DEC_SEQ
T_ALL = T_CTX + T_LAT
LANES = 128
SUBLANES = 8

F32 = jnp.float32
BF16 = jnp.bfloat16
NEG_INF = float("-inf")


def _dot(a, b):
    return jnp.dot(a, b, preferred_element_type=F32)


def _dot_nt(a, b, precision=None):
    return lax.dot_general(a, b, (((1,), (1,)), ((), ())), precision=precision,
                           preferred_element_type=F32)


def _layer_norm(v, g, b):
    mu = jnp.mean(v, axis=-1, keepdims=True)
    d = v - mu
    var = jnp.mean(d * d, axis=-1, keepdims=True)
    return d * lax.rsqrt(var + LN_EPS) * g + b


def _rms_norm(v, g):
    return v * lax.rsqrt(jnp.mean(v * v, axis=-1, keepdims=True) + RMS_EPS) * g


def _silu(v):
    return v / (1.0 + jnp.exp(-v))


def _sigmoid(v):
    return 1.0 / (1.0 + jnp.exp(-v))


def _adaln_kernel(c_ref, w_ref, b_ref, o_ref):
    c = c_ref[...]
    s = _silu(c)
    o_ref[0, 0] = jnp.dot(s, w_ref[0], precision=lax.Precision.HIGHEST,
                          preferred_element_type=F32) + b_ref[0]


def _adaln(cvecs, ada_w, ada_b):
    out = pl.pallas_call(
        _adaln_kernel,
        grid=(DEPTH, 6),
        in_specs=[
            pl.BlockSpec((SUBLANES, D_MODEL), lambda l, j: (0, 0)),
            pl.BlockSpec((1, D_MODEL, D_MODEL), lambda l, j: (l, 0, j)),
            pl.BlockSpec((1, 1, D_MODEL), lambda l, j: (l, 0, j)),
        ],
        out_specs=pl.BlockSpec((1, 1, SUBLANES, D_MODEL), lambda l, j: (l, j, 0, 0)),
        out_shape=jax.ShapeDtypeStruct((DEPTH, 6, SUBLANES, D_MODEL), F32),
        compiler_params=pltpu.CompilerParams(dimension_semantics=("parallel", "parallel")),
        name="adaln",
    )(cvecs, ada_w, ada_b.reshape(DEPTH, 1, 6 * D_MODEL))
    return jnp.transpose(out[:, :, :N_GROUPS_ROWS, :], (0, 2, 1, 3))


CONV_TM = 1024
CONV_PIECES = 2


def _conv_kernel(xc_ref, xcp_ref, xcn_ref, xl_ref, xlp_ref, xln_ref, mods_ref, win_ref, ck_ref, wout_ref,
                 g_ref, b_ref, o_ref):
    i = pl.program_id(0)
    tm = xc_ref.shape[0]
    is_ctx = i < T_CTX // tm
    m = mods_ref[0]
    sm, cm, gm = m[0:1], m[1:2], m[2:3]
    x = jnp.where(is_ctx, xc_ref[...], xl_ref[...])
    xp = jnp.where(is_ctx, xcp_ref[...], xlp_ref[...])
    xn = jnp.where(is_ctx, xcn_ref[...], xln_ref[...])
    def halo_u(xh):
        hh = (xh * (1.0 + cm) + sm).astype(BF16)
        zh = _dot(hh, win_ref[:, D_MODEL:])
        return zh[:, :D_MODEL] * zh[:, D_MODEL:]

    th = tm // CONV_PIECES
    halves = [x[j * th:(j + 1) * th] for j in range(CONV_PIECES)]
    zs = [_dot((xh * (1.0 + cm) + sm).astype(BF16), win_ref[...]) for xh in halves]
    us = [z[:, D_MODEL:2 * D_MODEL] * z[:, 2 * D_MODEL:] for z in zs]
    befores = [halo_u(xp)[SUBLANES - 1:SUBLANES]] + [u[th - 1:th] for u in us[:-1]]
    afters = [u[0:1] for u in us[1:]] + [halo_u(xn)[0:1]]

    ck = ck_ref[...]
    row = lax.broadcasted_iota(jnp.int32, (th, 1), 0)
    for hf in range(CONV_PIECES):
        grow = i * tm + hf * th + row
        seq_len = jnp.where(grow < T_CTX, SEQ, DEC_SEQ)
        pos = jnp.bitwise_and(grow, seq_len - 1)
        u = us[hf]
        left = jnp.where(row == 0, befores[hf], pltpu.roll(u, 1, 0))
        left = jnp.where(pos == 0, 0.0, left)
        right = jnp.where(row == th - 1, afters[hf], pltpu.roll(u, th - 1, 0))
        right = jnp.where(pos == seq_len - 1, 0.0, right)
        conv = left * ck[0:1] + u * ck[1:2] + right * ck[2:3]
        v = (zs[hf][:, :D_MODEL] * conv).astype(BF16)
        y = _dot(v, wout_ref[...])
        o_ref[hf * th:(hf + 1) * th, :] = _layer_norm(ALPHA * halves[hf] + gm * y, g_ref[...], b_ref[...])


def _conv_mixer(x_ctx, x_lat, mods, w_in, conv_k, w_out, ln_g, ln_b):
    tm = CONV_TM
    per8 = tm // SUBLANES
    n_ctx = T_CTX // tm
    n_lat = T_LAT // tm

    def stream_specs(first, n):
        def blk(i):
            return jnp.clip(i - first, 0, n - 1)
        return [
            pl.BlockSpec((tm, D_MODEL), lambda i: (blk(i), 0)),
            pl.BlockSpec((SUBLANES, D_MODEL), lambda i: (jnp.maximum(blk(i) * per8 - 1, 0), 0)),
            pl.BlockSpec((SUBLANES, D_MODEL), lambda i: (jnp.minimum((blk(i) + 1) * per8, n * per8 - 1), 0)),
        ]

    return pl.pallas_call(
        _conv_kernel,
        grid=(n_ctx + n_lat,),
        in_specs=stream_specs(0, n_ctx) + stream_specs(n_ctx, n_lat) + [
            pl.BlockSpec((1, 6, D_MODEL), lambda i: (i * tm // GROUP_ROWS, 0, 0)),
            pl.BlockSpec((D_MODEL, 3 * D_MODEL), lambda i: (0, 0)),
            pl.BlockSpec((3, D_MODEL), lambda i: (0, 0)),
            pl.BlockSpec((D_MODEL, D_MODEL), lambda i: (0, 0)),
            pl.BlockSpec((1, D_MODEL), lambda i: (0, 0)),
            pl.BlockSpec((1, D_MODEL), lambda i: (0, 0)),
        ],
        out_specs=pl.BlockSpec((tm, D_MODEL), lambda i: (i, 0)),
        out_shape=jax.ShapeDtypeStruct((T_ALL, D_MODEL), F32),
        compiler_params=pltpu.CompilerParams(dimension_semantics=("parallel",)),
        name="conv_mixer",
    )(x_ctx, x_ctx, x_ctx, x_lat, x_lat, x_lat, mods, w_in, conv_k, w_out, ln_g, ln_b)


ROUTER_TM = 512
MOE_CHUNKS = ((0, T_ALL),)
COMBINE_EIGHTHS = (2, 2, 2, 2)


def _first_argmax_mask(cur, ridx, n):
    mx = jnp.max(cur, axis=0, keepdims=True)
    first = jnp.min(jnp.where(cur == mx, ridx, n), axis=0, keepdims=True)
    return ridx == first, mx


def _router_kernel(x_ref, mods_ref, rwt_ref, bias_ref, idx_ref, pos_ref, wcol_ref, count_ref, hpa_ref, hpb_ref,
                   carry_ref):
    tm = x_ref.shape[0]
    m = mods_ref[0]
    sf, cf = m[3:4], m[4:5]
    hc = x_ref[...] * (1.0 + cf) + sf
    words = _pack_rows(hc)
    hpa_ref[...] = words[:, :D_QUARTER]
    hpb_ref[...] = words[:, D_QUARTER:]
    logits = _dot_nt(rwt_ref[...], hc, precision=lax.Precision.HIGHEST)
    scores = _sigmoid(logits)
    biased = scores + bias_ref[...]

    ridx8 = lax.broadcasted_iota(jnp.int32, (GROUP_SIZE, tm), 0)
    gscore = jnp.full((N_GROUPS, tm), NEG_INF, F32)
    for g in range(N_GROUPS):
        blk = biased[g * GROUP_SIZE:(g + 1) * GROUP_SIZE]
        sel, m1 = _first_argmax_mask(blk, ridx8, GROUP_SIZE)
        m2 = jnp.max(jnp.where(sel, NEG_INF, blk), axis=0, keepdims=True)
        gscore = jnp.where(ridx8 == g, m1 + m2, gscore)

    gmask = jnp.zeros((N_GROUPS, tm), jnp.bool_)
    cur = gscore
    for _ in range(TOPK_GROUPS):
        sel, _unused = _first_argmax_mask(cur, ridx8, N_GROUPS)
        gmask = jnp.logical_or(gmask, sel)
        cur = jnp.where(sel, NEG_INF, cur)

    gmask_f = gmask.astype(F32)
    blocks = []
    for g in range(N_GROUPS):
        keep = jnp.broadcast_to(gmask_f[g:g + 1], (GROUP_SIZE, tm)) > 0.5
        blocks.append(jnp.where(keep, biased[g * GROUP_SIZE:(g + 1) * GROUP_SIZE], NEG_INF))
    cur = jnp.concatenate(blocks, axis=0)

    first_tiles = [row0 // tm for row0, _rows in MOE_CHUNKS]
    starts_chunk = functools.reduce(jnp.logical_or, [pl.program_id(0) == ft for ft in first_tiles])

    @pl.when(starts_chunk)
    def _():
        carry_ref[...] = jnp.zeros(carry_ref.shape, F32)

    ridx = lax.broadcasted_iota(jnp.int32, (N_EXPERTS, tm), 0)
    kidx = lax.broadcasted_iota(jnp.int32, (TOP_K, tm), 0)
    sels = []
    chosen = jnp.zeros((N_EXPERTS, tm), jnp.bool_)
    idx_rows = jnp.zeros((TOP_K, tm), jnp.int32)
    for k in range(TOP_K):
        mx = jnp.max(cur, axis=0, keepdims=True)
        first = jnp.min(jnp.where(cur == mx, ridx, N_EXPERTS), axis=0, keepdims=True)
        sel = ridx == first
        sels.append(sel)
        chosen = jnp.logical_or(chosen, sel)
        idx_rows = jnp.where(kidx == k, first, idx_rows)
        cur = jnp.where(sel, NEG_INF, cur)

    onehot = chosen.astype(F32)
    t_row = lax.broadcasted_iota(jnp.int32, (tm, tm), 0)
    t_col = lax.broadcasted_iota(jnp.int32, (tm, tm), 1)
    before = (t_row < t_col).astype(BF16)
    rank = carry_ref[...] + _dot(onehot.astype(BF16), before)
    carry_ref[...] = carry_ref[...] + jnp.sum(onehot, axis=1, keepdims=True)
    count_ref[0] = jnp.broadcast_to(carry_ref[...], count_ref.shape[1:])

    w = jnp.where(chosen, scores, 0.0)
    w = w / jnp.sum(w, axis=0, keepdims=True) * ROUTED_SCALE
    pos_rows = jnp.zeros((TOP_K, tm), F32)
    w_rows = jnp.zeros((TOP_K, tm), F32)
    for k in range(TOP_K):
        pos_rows = jnp.where(kidx == k, jnp.sum(jnp.where(sels[k], rank, 0.0), axis=0, keepdims=True), pos_rows)
        w_rows = jnp.where(kidx == k, jnp.sum(jnp.where(sels[k], w, 0.0), axis=0, keepdims=True), w_rows)
    idx_ref[...] = idx_rows
    pos_ref[...] = pos_rows.astype(jnp.int32)
    wpad = jnp.concatenate([w_rows, jnp.zeros((LANES - TOP_K, tm), F32)], axis=0)
    wcol_ref[...] = wpad.T


def _router(x, mods, router_wt, router_bias):
    t = x.shape[0]
    tm = ROUTER_TM

    def chunk_of(i):
        return sum((i >= row0 // tm).astype(jnp.int32) for row0, _rows in MOE_CHUNKS[1:])

    return pl.pallas_call(
        _router_kernel,
        grid=(t // tm,),
        in_specs=[
            pl.BlockSpec((tm, D_MODEL), lambda i: (i, 0)),
            pl.BlockSpec((1, 6, D_MODEL), lambda i: (i * tm // GROUP_ROWS, 0, 0)),
            pl.BlockSpec((N_EXPERTS, D_MODEL), lambda i: (0, 0)),
            pl.BlockSpec((N_EXPERTS, 1), lambda i: (0, 0)),
        ],
        out_specs=[
            pl.BlockSpec((TOP_K, tm), lambda i: (0, i)),
            pl.BlockSpec((TOP_K, tm), lambda i: (0, i)),
            pl.BlockSpec((tm, LANES), lambda i: (i, 0)),
            pl.BlockSpec((1, N_EXPERTS, LANES), lambda i: (chunk_of(i), 0, 0)),
            pl.BlockSpec((tm, D_QUARTER), lambda i: (i, 0)),
            pl.BlockSpec((tm, D_QUARTER), lambda i: (i, 0)),
        ],
        out_shape=[
            jax.ShapeDtypeStruct((TOP_K, t), jnp.int32),
            jax.ShapeDtypeStruct((TOP_K, t), jnp.int32),
            jax.ShapeDtypeStruct((t, LANES), F32),
            jax.ShapeDtypeStruct((len(MOE_CHUNKS), N_EXPERTS, LANES), F32),
            jax.ShapeDtypeStruct((t, D_QUARTER), jnp.uint32),
            jax.ShapeDtypeStruct((t, D_QUARTER), jnp.uint32),
        ],
        scratch_shapes=[pltpu.VMEM((N_EXPERTS, 1), F32)],
        compiler_params=pltpu.CompilerParams(dimension_semantics=("arbitrary",)),
        name="moe_router",
    )(x, mods, router_wt, router_bias)


FFN_TM = 512


def _n_tiles(rows):
    return rows * TOP_K // FFN_TM + N_EXPERTS


def _expert_layout(counts, n_tiles):
    padded = (counts + FFN_TM - 1) // FFN_TM * FFN_TM
    end = jnp.cumsum(padded)
    start = end - padded
    tile_row = jnp.arange(n_tiles, dtype=jnp.int32) * FFN_TM
    tile_expert = jnp.minimum(jnp.sum(end[None, :] <= tile_row[:, None], axis=1), N_EXPERTS - 1)
    of_tile = tile_expert[:, None] == jnp.arange(N_EXPERTS, dtype=jnp.int32)[None, :]
    live_end = jnp.sum(jnp.where(of_tile, (start + counts)[None, :], 0), axis=1)
    tile_valid = jnp.clip(live_end - tile_row, 0, FFN_TM)
    n_used = (end[-1] // FFN_TM).astype(jnp.int32).reshape(1)
    used = jnp.arange(n_tiles, dtype=jnp.int32) < n_used[0]
    prev_expert = jnp.concatenate([jnp.full((1,), -1, tile_expert.dtype), tile_expert[:-1]])
    run_flag = jnp.logical_and(used, tile_expert != prev_expert)
    run_slot = (jnp.cumsum(run_flag.astype(jnp.int32)) - 1) % 2
    later = jnp.logical_and(used[None, :], tile_expert[None, :] > tile_expert[:, None])
    next_expert = jnp.min(jnp.where(later, tile_expert[None, :], N_EXPERTS), axis=1)
    next_expert = jnp.where(next_expert == N_EXPERTS, -1, next_expert)
    plan = (run_flag.astype(jnp.int32), run_slot.astype(jnp.int32), next_expert.astype(jnp.int32))
    return start.astype(jnp.int32), tile_expert.astype(jnp.int32), tile_valid.astype(jnp.int32), n_used, plan


D_HALF = D_MODEL // 2
D_QUARTER = D_MODEL // 4


def _pack_rows(v):
    hi = lax.bitcast_convert_type(v[:, :D_HALF].astype(BF16).astype(F32), jnp.uint32)
    lo = lax.bitcast_convert_type(v[:, D_HALF:].astype(BF16).astype(F32), jnp.uint32)
    return jnp.bitwise_or(hi, jnp.right_shift(lo, jnp.uint32(16)))


def _unpack_rows(w):
    hi = lax.bitcast_convert_type(jnp.bitwise_and(w, jnp.uint32(0xFFFF0000)), F32)
    lo = lax.bitcast_convert_type(jnp.left_shift(w, jnp.uint32(16)), F32)
    return hi, lo


DEST_TM = 2048


def _dest_kernel(start_ref, idx_ref, pos_ref, dest_ref):
    idx = idx_ref[...]
    base = jnp.zeros(idx.shape, jnp.int32)
    for e in range(N_EXPERTS):
        base = jnp.where(idx == e, start_ref[e], base)
    dest_ref[...] = base + pos_ref[...]


def _dest_rows(start, idx, pos):
    t = idx.shape[1]
    return pl.pallas_call(
        _dest_kernel,
        grid_spec=pltpu.PrefetchScalarGridSpec(
            num_scalar_prefetch=1,
            grid=(t // DEST_TM,),
            in_specs=[pl.BlockSpec((TOP_K, DEST_TM), lambda i, s: (0, i)),
                      pl.BlockSpec((TOP_K, DEST_TM), lambda i, s: (0, i))],
            out_specs=pl.BlockSpec((TOP_K, DEST_TM), lambda i, s: (0, i)),
        ),
        out_shape=jax.ShapeDtypeStruct((TOP_K, t), jnp.int32),
        compiler_params=pltpu.CompilerParams(dimension_semantics=("parallel",)),
        name="moe_dest",
    )(start, idx, pos)


SC_WINDOW = 128


def _sc_mesh():
    return plsc.VectorSubcoreMesh(core_axis_name="c", subcore_axis_name="s")


def _sc_scatter_rows(x_a, x_b, dest, row0, n_sorted):
    t = dest.shape[1]
    blk0 = row0 // SC_WINDOW
    out = jax.ShapeDtypeStruct((n_sorted, D_QUARTER), x_a.dtype)

    @functools.partial(pl.kernel, out_type=(out, out), mesh=_sc_mesh(), scratch_types=[])
    def scatter(xa_hbm, xb_hbm, i_hbm, oa_hbm, ob_hbm):
        for x_hbm, o_hbm in ((xa_hbm, oa_hbm), (xb_hbm, ob_hbm)):
            def body(x_vmem, i_vmem, o_hbm=o_hbm):
                pltpu.sync_copy(x_vmem, o_hbm.at[i_vmem.at[0]])

            pltpu.emit_pipeline(
                body,
                grid=(t // SC_WINDOW, TOP_K),
                in_specs=[pl.BlockSpec((SC_WINDOW, D_QUARTER), lambda i, k: (i + blk0, 0)),
                          pl.BlockSpec((1, SC_WINDOW), lambda i, k: (k, i))],
                out_specs=[],
                core_axis_name=("c", "s"),
                dimension_semantics=(pltpu.PARALLEL, pltpu.ARBITRARY),
            )(x_hbm, i_hbm)

    return scatter(x_a, x_b, dest)


def _sc_gather_rows(table_a, table_b, idx):
    m = idx.shape[1]
    out = jax.ShapeDtypeStruct((m, D_QUARTER), table_a.dtype)

    @functools.partial(pl.kernel, out_type=(out, out), mesh=_sc_mesh(), scratch_types=[])
    def gather(ta_hbm, tb_hbm, i_hbm, oa_hbm, ob_hbm):
        for t_hbm, o_hbm in ((ta_hbm, oa_hbm), (tb_hbm, ob_hbm)):
            def body(i_vmem, o_vmem, t_hbm=t_hbm):
                pltpu.sync_copy(t_hbm.at[i_vmem.at[0]], o_vmem)

            pltpu.emit_pipeline(
                body,
                grid=(m // SC_WINDOW,),
                in_specs=[pl.BlockSpec((1, SC_WINDOW), lambda i: (0, i))],
                out_specs=[pl.BlockSpec((SC_WINDOW, D_QUARTER), lambda i: (i, 0))],
                core_axis_name=("c", "s"),
                dimension_semantics=(pltpu.PARALLEL,),
            )(i_hbm, o_hbm)

    return gather(table_a, table_b, idx)


FFN_STEP_TILES = 2


def _ffn_kernel(layer, te_ref, tv_ref, nu_ref, flag_ref, slot_ref, next_ref, xa_ref, xb_ref,
                wg_hbm, wu_hbm, wd_hbm, ya_ref, yb_ref, wg_st, wu_st, wd_st, wgb_ref, wub_ref, wdb_ref, sem):
    step = pl.program_id(0)

    def fetch(expert, slot):
        return [pltpu.make_async_copy(src.at[layer, expert], dst.at[slot], sem.at[slot])
                for src, dst in ((wg_hbm, wg_st), (wu_hbm, wu_st), (wd_hbm, wd_st))]

    for s in range(FFN_STEP_TILES):
        tile = step * FFN_STEP_TILES + s

        @pl.when(jnp.logical_and(tile < nu_ref[0], flag_ref[tile] == 1))
        def _():
            slot = slot_ref[tile]

            @pl.when(tile == 0)
            def _():
                for cp in fetch(te_ref[tile], slot):
                    cp.start()

            for cp in fetch(te_ref[tile], slot):
                cp.wait()
            wgb_ref[slot] = wg_st[slot].astype(BF16)
            wub_ref[slot] = wu_st[slot].astype(BF16)
            wdb_ref[slot] = wd_st[slot].astype(BF16)

            @pl.when(next_ref[tile] >= 0)
            def _():
                for cp in fetch(next_ref[tile], 1 - slot):
                    cp.start()

    @pl.when(step * FFN_STEP_TILES < nu_ref[0])
    def _():
        for s in range(FFN_STEP_TILES):
            tile = step * FFN_STEP_TILES + s
            slot = slot_ref[tile]
            wg = wgb_ref[slot]
            wu = wub_ref[slot]
            wd = wdb_ref[slot]
            rows = pl.ds(s * FFN_TM, FFN_TM)
            live = lax.broadcasted_iota(jnp.int32, (FFN_TM, D_QUARTER), 0) < tv_ref[tile]
            hi_a, lo_a = _unpack_rows(jnp.where(live, xa_ref[rows, :], jnp.uint32(0)))
            hi_b, lo_b = _unpack_rows(jnp.where(live, xb_ref[rows, :], jnp.uint32(0)))
            xb = jnp.concatenate([hi_a.astype(BF16), hi_b.astype(BF16), lo_a.astype(BF16), lo_b.astype(BF16)],
                                 axis=1)
            a = (_silu(_dot(xb, wg)) * _dot(xb, wu)).astype(BF16)
            words = _pack_rows(_dot(a, wd))
            ya_ref[rows, :] = words[:, :D_QUARTER]
            yb_ref[rows, :] = words[:, D_QUARTER:]

    @pl.when(step * FFN_STEP_TILES >= nu_ref[0])
    def _():
        ya_ref[...] = jnp.zeros(ya_ref.shape, jnp.uint32)
        yb_ref[...] = jnp.zeros(yb_ref.shape, jnp.uint32)


def _expert_ffn(xs_a, xs_b, tile_expert, tile_valid, n_used, plan, wg, wu, wd, layer):
    n_tiles = xs_a.shape[0] // FFN_TM
    step_rows = FFN_STEP_TILES * FFN_TM

    def row_map(i, te, tv, nu, fl, sl, nx):
        return (jnp.minimum(i, (nu[0] - 1) // FFN_STEP_TILES), 0)

    def out_map(i, te, tv, nu, fl, sl, nx):
        return (i, 0)

    hbm = pl.BlockSpec(memory_space=pl.ANY)
    return pl.pallas_call(
        functools.partial(_ffn_kernel, layer),
        grid_spec=pltpu.PrefetchScalarGridSpec(
            num_scalar_prefetch=6,
            grid=(n_tiles // FFN_STEP_TILES,),
            in_specs=[pl.BlockSpec((step_rows, D_QUARTER), row_map),
                      pl.BlockSpec((step_rows, D_QUARTER), row_map), hbm, hbm, hbm],
            out_specs=[pl.BlockSpec((step_rows, D_QUARTER), out_map), pl.BlockSpec((step_rows, D_QUARTER), out_map)],
            scratch_shapes=[pltpu.VMEM((2, D_MODEL, D_EXPERT), F32), pltpu.VMEM((2, D_MODEL, D_EXPERT), F32),
                            pltpu.VMEM((2, D_EXPERT, D_MODEL), F32),
                            pltpu.VMEM((2, D_MODEL, D_EXPERT), BF16), pltpu.VMEM((2, D_MODEL, D_EXPERT), BF16),
                            pltpu.VMEM((2, D_EXPERT, D_MODEL), BF16), pltpu.SemaphoreType.DMA((2,))],
        ),
        out_shape=[jax.ShapeDtypeStruct((n_tiles * FFN_TM, D_QUARTER), jnp.uint32),
                   jax.ShapeDtypeStruct((n_tiles * FFN_TM, D_QUARTER), jnp.uint32)],
        compiler_params=pltpu.CompilerParams(dimension_semantics=("arbitrary",)),
        name="moe_expert_ffn",
    )(tile_expert, tile_valid, n_used, *plan, xs_a, xs_b, wg, wu, wd)


COMBINE_TM = 512


def _combine_kernel(n_prev, tile0, x_ref, mods_ref, wcol_ref, ga_ref, gb_ref, sg_ref, su_ref, sd_ref, g_ref, b_ref,
                    *rest):
    o_refs = rest[n_prev:]
    m = mods_ref[0]
    sf, cf, gf = m[3:4], m[4:5], m[5:6]
    x = x_ref[...]
    hc = (x * (1.0 + cf) + sf).astype(BF16)
    a = _silu(_dot(hc, sg_ref[...])) * _dot(hc, su_ref[...])
    y = _dot(a.astype(BF16), sd_ref[...])
    wcol = wcol_ref[...]
    parts = [y[:, q * D_QUARTER:(q + 1) * D_QUARTER] for q in range(4)]
    for k in range(TOP_K):
        hi_a, lo_a = _unpack_rows(ga_ref[k])
        hi_b, lo_b = _unpack_rows(gb_ref[k])
        wk = wcol[:, k:k + 1]
        parts = [parts[0] + wk * hi_a, parts[1] + wk * hi_b, parts[2] + wk * lo_a, parts[3] + wk * lo_b]
    y = jnp.concatenate(parts, axis=1)
    out = _layer_norm(ALPHA * x + gf * y, g_ref[...], b_ref[...])
    if len(o_refs) == 1:
        o_refs[0][...] = out
    else:
        is_ctx = pl.program_id(0) + tile0 < T_CTX // x_ref.shape[0]

        @pl.when(is_ctx)
        def _():
            o_refs[0][...] = out

        @pl.when(jnp.logical_not(is_ctx))
        def _():
            o_refs[1][...] = out


def _combine(x, mods, wcol, g_a, g_b, sg, su, sd, ln_g, ln_b, row0, split_streams, prev):
    tm = COMBINE_TM
    n = g_a.shape[1] // tm
    tile0 = row0 // tm
    n_ctx = T_CTX // tm
    full = lambda shape: pl.BlockSpec(shape, lambda i: (0,) * len(shape))
    layouts = {
        "all": (lambda i: (tile0 + i, 0), T_ALL),
        "ctx": (lambda i: (jnp.minimum(tile0 + i, n_ctx - 1), 0), T_CTX),
        "lat": (lambda i: (jnp.maximum(tile0 + i - n_ctx, 0), 0), T_LAT),
    }
    if split_streams:
        kinds = (["ctx"] if tile0 < n_ctx else []) + (["lat"] if tile0 + n > n_ctx else [])
    else:
        kinds = ["all"]
    carried = [kd for kd in kinds if kd in prev]
    outs = pl.pallas_call(
        functools.partial(_combine_kernel, len(carried), tile0),
        grid=(n,),
        in_specs=[
            pl.BlockSpec((tm, D_MODEL), lambda i: (tile0 + i, 0)),
            pl.BlockSpec((1, 6, D_MODEL), lambda i: ((tile0 + i) * tm // GROUP_ROWS, 0, 0)),
            pl.BlockSpec((tm, LANES), lambda i: (tile0 + i, 0)),
            pl.BlockSpec((TOP_K, tm, D_QUARTER), lambda i: (0, i, 0)),
            pl.BlockSpec((TOP_K, tm, D_QUARTER), lambda i: (0, i, 0)),
            full((D_MODEL, D_SHARED)), full((D_MODEL, D_SHARED)), full((D_SHARED, D_MODEL)),
            full((1, D_MODEL)), full((1, D_MODEL)),
        ] + [pl.BlockSpec(memory_space=pl.ANY) for _ in carried],
        out_specs=[pl.BlockSpec((tm, D_MODEL), layouts[kd][0]) for kd in kinds],
        out_shape=[jax.ShapeDtypeStruct((layouts[kd][1], D_MODEL), F32) for kd in kinds],
        input_output_aliases={10 + j: kinds.index(kd) for j, kd in enumerate(carried)},
        compiler_params=pltpu.CompilerParams(dimension_semantics=("arbitrary",)),
        name="moe_combine",
    )(x, mods, wcol, g_a, g_b, sg, su, sd, ln_g, ln_b, *[prev[kd] for kd in carried])
    return {**prev, **dict(zip(kinds, outs))}


MLA_TM = 1024


def _mla_proj_kernel(x_ref, mods_ref, wdq_ref, wdkv_ref, wkr_ref, qn_ref, kvn_ref, wuq_ref,
                     wukn_ref, wuv_ref, ta_ref, tb_ref, q_ref, k_ref, v_ref, ckv_ref, kr_ref):
    m = mods_ref[0]
    sm, cm = m[0:1], m[1:2]
    h = (x_ref[...] * (1.0 + cm) + sm).astype(BF16)
    cq = _rms_norm(_dot(h, wdq_ref[...]), qn_ref[...])
    ckv = _rms_norm(_dot(h, wdkv_ref[...]), kvn_ref[...])
    kr2 = _dot(h, wkr_ref[...])
    ckv_ref[...] = ckv
    kr_ref[...] = kr2

    ka = ta_ref[0]
    kb = tb_ref[0]
    tm = ka.shape[0]
    ta = jnp.concatenate([jnp.full((tm, QK_NOPE), Q_PRESCALE, F32), ka * Q_PRESCALE], axis=1)
    tb = jnp.concatenate([jnp.zeros((tm, QK_NOPE), F32), kb * Q_PRESCALE], axis=1)
    krr = kr2 * ka + pltpu.roll(kr2, QK_ROPE, 1) * kb

    qpre = _dot(cq.astype(BF16), wuq_ref[...])
    ckv_b = ckv.astype(BF16)
    kn = _dot(ckv_b, wukn_ref[...])
    v_ref[...] = _dot(ckv_b, wuv_ref[...]).astype(BF16)
    for hd in range(N_HEADS):
        qh = qpre[:, hd * HEAD_PAD:(hd + 1) * HEAD_PAD]
        qrot = qh * ta + pltpu.roll(qh, HEAD_PAD - QK_ROPE, 1) * tb
        q_ref[:, hd * HEAD_PAD:(hd + 1) * HEAD_PAD] = qrot.astype(BF16)
        k_ref[:, hd * HEAD_PAD:hd * HEAD_PAD + QK_NOPE] = kn[:, hd * QK_NOPE:(hd + 1) * QK_NOPE].astype(BF16)
        k_ref[:, hd * HEAD_PAD + QK_NOPE:(hd + 1) * HEAD_PAD] = krr.astype(BF16)


def _mla_proj(x, mods, p, rope_a, rope_b):
    t = x.shape[0]
    tm = MLA_TM
    full = lambda shape: pl.BlockSpec(shape, lambda i: (0,) * len(shape))
    rope_spec = pl.BlockSpec(
        (1, tm, 2 * QK_ROPE),
        lambda i: (jnp.minimum(i * tm // GROUP_ROWS, 1), (i * tm % GROUP_ROWS) // tm, 0))
    return pl.pallas_call(
        _mla_proj_kernel,
        grid=(t // tm,),
        in_specs=[
            pl.BlockSpec((tm, D_MODEL), lambda i: (i, 0)),
            pl.BlockSpec((1, 6, D_MODEL), lambda i: (i * tm // GROUP_ROWS, 0, 0)),
            full((D_MODEL, Q_LORA)), full((D_MODEL, KV_LORA)), full((D_MODEL, 2 * QK_ROPE)),
            full((1, Q_LORA)), full((1, KV_LORA)),
            full((Q_LORA, N_HEADS * HEAD_PAD)),
            full((KV_LORA, N_HEADS * QK_NOPE)), full((KV_LORA, N_HEADS * V_DIM)),
            rope_spec, rope_spec,
        ],
        out_specs=[
            pl.BlockSpec((tm, N_HEADS * HEAD_PAD), lambda i: (i, 0)),
            pl.BlockSpec((tm, N_HEADS * HEAD_PAD), lambda i: (i, 0)),
            pl.BlockSpec((tm, N_HEADS * V_DIM), lambda i: (i, 0)),
            pl.BlockSpec((tm, KV_LORA), lambda i: (i, 0)),
            pl.BlockSpec((tm, 2 * QK_ROPE), lambda i: (i, 0)),
        ],
        out_shape=[
            jax.ShapeDtypeStruct((t, N_HEADS * HEAD_PAD), BF16),
            jax.ShapeDtypeStruct((t, N_HEADS * HEAD_PAD), BF16),
            jax.ShapeDtypeStruct((t, N_HEADS * V_DIM), BF16),
            jax.ShapeDtypeStruct((t, KV_LORA), F32),
            jax.ShapeDtypeStruct((t, 2 * QK_ROPE), F32),
        ],
        compiler_params=pltpu.CompilerParams(dimension_semantics=("parallel",)),
        name="mla_proj",
    )(x, mods, p["w_dq"], p["w_dkv"], p["w_kr"], p["q_norm"], p["kv_norm"], p["w_uq"],
      p["w_ukn"], p["w_uv"], rope_a, rope_b)


def _cache_kv_kernel(ckv_ref, kr_ref, wukn_ref, wuv_ref, k_ref, v_ref):
    ckv_b = ckv_ref[...].astype(BF16)
    kn = _dot(ckv_b, wukn_ref[...])
    v_ref[...] = _dot(ckv_b, wuv_ref[...]).astype(BF16)
    kr = kr_ref[...].astype(BF16)
    for hd in range(N_HEADS):
        k_ref[:, hd * HEAD_PAD:hd * HEAD_PAD + QK_NOPE] = kn[:, hd * QK_NOPE:(hd + 1) * QK_NOPE].astype(BF16)
        k_ref[:, hd * HEAD_PAD + QK_NOPE:(hd + 1) * HEAD_PAD] = kr


def _cache_kv(ckv, kr_pad, p):
    t = ckv.shape[0]
    tm = PAST_LEN
    full = lambda shape: pl.BlockSpec(shape, lambda i: (0,) * len(shape))
    return pl.pallas_call(
        _cache_kv_kernel,
        grid=(t // tm,),
        in_specs=[
            pl.BlockSpec((tm, KV_LORA), lambda i: (i, 0)),
            pl.BlockSpec((tm, 2 * QK_ROPE), lambda i: (i, 0)),
            full((KV_LORA, N_HEADS * QK_NOPE)), full((KV_LORA, N_HEADS * V_DIM)),
        ],
        out_specs=[
            pl.BlockSpec((tm, N_HEADS * HEAD_PAD), lambda i: (i, 0)),
            pl.BlockSpec((tm, N_HEADS * V_DIM), lambda i: (i, 0)),
        ],
        out_shape=[
            jax.ShapeDtypeStruct((t, N_HEADS * HEAD_PAD), BF16),
            jax.ShapeDtypeStruct((t, N_HEADS * V_DIM), BF16),
        ],
        compiler_params=pltpu.CompilerParams(dimension_semantics=("parallel",)),
        name="mla_cache_kv",
    )(ckv, kr_pad, p["w_ukn"], p["w_uv"])


def _ctx_attn_kernel(q_ref, k_ref, v_ref, o_ref):
    for hd in range(N_HEADS):
        q = q_ref[:, hd * HEAD_PAD:(hd + 1) * HEAD_PAD]
        k = k_ref[:, hd * HEAD_PAD:(hd + 1) * HEAD_PAD]
        s = _dot_nt(q, k)
        s = s - jnp.max(s, axis=-1, keepdims=True)
        p = jnp.exp2(s)
        p = p / jnp.sum(p, axis=-1, keepdims=True)
        o = _dot(p.astype(BF16), v_ref[:, hd * V_DIM:(hd + 1) * V_DIM])
        o_ref[:, hd * V_DIM:(hd + 1) * V_DIM] = o.astype(BF16)


def _ctx_attention(q, k, v):
    return pl.pallas_call(
        _ctx_attn_kernel,
        grid=(BATCH,),
        in_specs=[
            pl.BlockSpec((SEQ, N_HEADS * HEAD_PAD), lambda b: (b, 0)),
            pl.BlockSpec((SEQ, N_HEADS * HEAD_PAD), lambda b: (b, 0)),
            pl.BlockSpec((SEQ, N_HEADS * V_DIM), lambda b: (b, 0)),
        ],
        out_specs=pl.BlockSpec((SEQ, N_HEADS * V_DIM), lambda b: (b, 0)),
        out_shape=jax.ShapeDtypeStruct((T_CTX, N_HEADS * V_DIM), BF16),
        compiler_params=pltpu.CompilerParams(dimension_semantics=("parallel",)),
        name="ctx_attention",
    )(q, k, v)


LAT_TQ = 1024
LAT_TK = 512
LAT_PIECES = 4


def _lat_attn_kernel(q_ref, k_ref, v_ref, kc_ref, vc_ref, o_ref, s_ref, p_ref, m_ref):
    n_chunks = DEC_SEQ // LAT_TK + 1
    tp = q_ref.shape[0] // LAT_PIECES
    pieces = [pl.ds(j * tp, tp) for j in range(LAT_PIECES)]
    groups = [pieces[:2], pieces[2:]]
    state = {}

    def keys(c):
        return kc_ref[...] if c == n_chunks - 1 else k_ref[c * LAT_TK:(c + 1) * LAT_TK, :]

    def values(c):
        return vc_ref[...] if c == n_chunks - 1 else v_ref[c * LAT_TK:(c + 1) * LAT_TK, :]

    def qk(r, c):
        s = _dot_nt(q_ref[r, :], keys(c))
        s_ref[r, c * LAT_TK:(c + 1) * LAT_TK] = s
        mp = state.get(("m", r.start), jnp.full((tp, LANES), NEG_INF, F32))
        for j in range(LAT_TK // LANES):
            mp = jnp.maximum(mp, s[:, j * LANES:(j + 1) * LANES])
        state[("m", r.start)] = mp

    def row_max(r):
        m_ref[r, :] = jnp.broadcast_to(jnp.max(state[("m", r.start)], axis=-1, keepdims=True), (tp, LANES))

    def exp_chunk(r, c, after=None):
        m = m_ref[r, :]
        if after is not None:
            bits = lax.bitcast_convert_type(after[:, :LANES], jnp.uint32)
            zero = lax.shift_right_logical(lax.shift_right_logical(bits, jnp.uint32(16)), jnp.uint32(16))
            m = m + lax.bitcast_convert_type(zero, F32)
        for j in range(c * LAT_TK // LANES, (c + 1) * LAT_TK // LANES):
            p = jnp.exp2(s_ref[r, j * LANES:(j + 1) * LANES] - m)
            p_ref[r, j * LANES:(j + 1) * LANES] = p.astype(BF16)

    def pv(r, c):
        v = values(c)
        v_and_ones = jnp.concatenate([v, jnp.ones_like(v)], axis=1)
        acc = state.get(("a", r.start), jnp.zeros((tp, 2 * V_DIM), F32))
        state[("a", r.start)] = acc + _dot(p_ref[r, c * LAT_TK:(c + 1) * LAT_TK], v_and_ones)

    def finish(r):
        acc = state[("a", r.start)]
        o_ref[r, :] = (acc[:, :V_DIM] / acc[:, V_DIM:]).astype(BF16)

    for c in range(n_chunks):
        for r in groups[0]:
            qk(r, c)
    for r in groups[0]:
        row_max(r)
    for c in range(n_chunks):
        for r in groups[1]:
            qk(r, c)
        for r in groups[0]:
            exp_chunk(r, c)
    for r in groups[1]:
        row_max(r)
    for c in range(n_chunks):
        for r in groups[0]:
            pv(r, c)
        for r0, r in zip(groups[0], groups[1]):
            exp_chunk(r, c, after=state[("a", r0.start)])
    for r in groups[0]:
        finish(r)
    for c in range(n_chunks):
        for r in groups[1]:
            pv(r, c)
    for r in groups[1]:
        finish(r)


def _lat_attention(q, k, v, kc, vc):
    nq = DEC_SEQ // LAT_TQ
    return pl.pallas_call(
        _lat_attn_kernel,
        grid=(DEC_BATCH, N_HEADS, nq),
        in_specs=[
            pl.BlockSpec((LAT_TQ, HEAD_PAD), lambda b, h, i: ((b + 1) * nq + i, h)),
            pl.BlockSpec((DEC_SEQ, HEAD_PAD), lambda b, h, i: (b + 1, h)),
            pl.BlockSpec((DEC_SEQ, V_DIM), lambda b, h, i: (b + 1, h)),
            pl.BlockSpec((PAST_LEN, HEAD_PAD), lambda b, h, i: (b, h)),
            pl.BlockSpec((PAST_LEN, V_DIM), lambda b, h, i: (b, h)),
        ],
        out_specs=pl.BlockSpec((LAT_TQ, V_DIM), lambda b, h, i: (b * nq + i, h)),
        out_shape=jax.ShapeDtypeStruct((T_LAT, N_HEADS * V_DIM), BF16),
        scratch_shapes=[pltpu.VMEM((LAT_TQ, DEC_SEQ + PAST_LEN), F32),
                        pltpu.VMEM((LAT_TQ, DEC_SEQ + PAST_LEN), BF16),
                        pltpu.VMEM((LAT_TQ, LANES), F32)],
        compiler_params=pltpu.CompilerParams(
            dimension_semantics=("parallel", "parallel", "parallel")),
        name="lat_attention",
    )(q, k, v, kc, vc)


OPROJ_TM = 1024


def _oproj_kernel(oc_ref, ol_ref, x_ref, mods_ref, wo_ref, g_ref, b_ref, out_ref):
    is_ctx = pl.program_id(0) < T_CTX // OPROJ_TM
    o = jnp.where(is_ctx, oc_ref[...], ol_ref[...])
    gm = mods_ref[0][2:3]
    y = _dot(o, wo_ref[...])
    out_ref[...] = _layer_norm(ALPHA * x_ref[...] + gm * y, g_ref[...], b_ref[...])


def _oproj(o_ctx, o_lat, x, mods, w_o, ln_g, ln_b):
    t = x.shape[0]
    tm = OPROJ_TM
    n_ctx = T_CTX // tm
    return pl.pallas_call(
        _oproj_kernel,
        grid=(t // tm,),
        in_specs=[
            pl.BlockSpec((tm, N_HEADS * V_DIM), lambda i: (jnp.minimum(i, n_ctx - 1), 0)),
            pl.BlockSpec((tm, N_HEADS * V_DIM), lambda i: (jnp.maximum(i - n_ctx, 0), 0)),
            pl.BlockSpec((tm, D_MODEL), lambda i: (i, 0)),
            pl.BlockSpec((1, 6, D_MODEL), lambda i: (i * tm // GROUP_ROWS, 0, 0)),
            pl.BlockSpec((N_HEADS * V_DIM, D_MODEL), lambda i: (0, 0)),
            pl.BlockSpec((1, D_MODEL), lambda i: (0, 0)),
            pl.BlockSpec((1, D_MODEL), lambda i: (0, 0)),
        ],
        out_specs=pl.BlockSpec((tm, D_MODEL), lambda i: (i, 0)),
        out_shape=jax.ShapeDtypeStruct((t, D_MODEL), F32),
        compiler_params=pltpu.CompilerParams(dimension_semantics=("parallel",)),
        name="attn_oproj",
    )(o_ctx, o_lat, x, mods, w_o, ln_g, ln_b)


def _swap16(w):
    q = QK_ROPE // 4
    return jnp.concatenate([w[..., q:2 * q], w[..., :q], w[..., 3 * q:], w[..., 2 * q:3 * q]], axis=-1)


def _mla_params(w_dqkv, q_norm, w_uq, kv_norm, w_ukv):
    w_kr = w_dqkv[:, Q_LORA + KV_LORA:]
    wq = w_uq.reshape(Q_LORA, N_HEADS, QK_NOPE + QK_ROPE)
    wq_r = wq[..., QK_NOPE:]
    wq = jnp.concatenate([wq[..., :QK_NOPE], wq_r, _swap16(wq_r)], axis=-1)
    wkv = w_ukv.reshape(KV_LORA, N_HEADS, QK_NOPE + V_DIM)
    return {
        "w_dq": w_dqkv[:, :Q_LORA].astype(BF16),
        "w_dkv": w_dqkv[:, Q_LORA:Q_LORA + KV_LORA].astype(BF16),
        "w_kr": jnp.concatenate([w_kr, _swap16(w_kr)], axis=-1).astype(BF16),
        "q_norm": q_norm.reshape(1, Q_LORA),
        "kv_norm": kv_norm.reshape(1, KV_LORA),
        "w_uq": wq.reshape(Q_LORA, N_HEADS * HEAD_PAD).astype(BF16),
        "w_ukn": wkv[..., :QK_NOPE].reshape(KV_LORA, N_HEADS * QK_NOPE).astype(BF16),
        "w_uv": wkv[..., QK_NOPE:].reshape(KV_LORA, N_HEADS * V_DIM).astype(BF16),
    }


def _rope_tables():
    nf = QK_ROPE // 4
    t = np.arange(DEC_SEQ)
    row = (t // GRID_W).astype(np.float32)
    col = (t % GRID_W).astype(np.float32)
    inv = (ROPE_THETA ** (-np.arange(nf, dtype=np.float32) / nf)).astype(np.float32)
    ar, ac = row[:, None] * inv, col[:, None] * inv
    pad = np.zeros((DEC_SEQ, QK_ROPE), np.float32)
    cos = np.concatenate([np.cos(ar), np.cos(ar), np.cos(ac), np.cos(ac), pad], axis=-1)
    sin = np.concatenate([-np.sin(ar), np.sin(ar), -np.sin(ac), np.sin(ac), pad], axis=-1)
    cos_id = np.concatenate([np.ones((DEC_SEQ, QK_ROPE), np.float32), pad], axis=-1)
    sin_id = np.zeros((DEC_SEQ, 2 * QK_ROPE), np.float32)
    return (jnp.asarray(np.stack([cos_id, cos]).astype(np.float32)),
            jnp.asarray(np.stack([sin_id, sin]).astype(np.float32)))


def kernel(x_prompt, x_sample, cache_ckv, cache_krope, c, c_ctx, ada_w, ada_b, ln_g, ln_b, conv_w_in, conv_k, conv_w_out, mla_w_dqkv, mla_q_norm, mla_w_uq, mla_kv_norm, mla_w_ukv, mla_w_o, router_w, router_bias, exp_w_gate, exp_w_up, exp_w_down, sh_w_gate, sh_w_up, sh_w_down):
    cvecs =jnp.concatenate([c_ctx[None, :], c, jnp.zeros((SUBLANES - N_GROUPS_ROWS, D_MODEL), F32)], axis=0)
    mods = _adaln(cvecs, ada_w, ada_b)

    def ln(l, k):
        return ln_g[l, k].reshape(1, D_MODEL), ln_b[l, k].reshape(1, D_MODEL)

    def moe_layer(xin, l, split_streams):
        idx, pos, wcol, counts, hp_a, hp_b = _router(xin, mods[l], router_w[l].T,
                                                     router_bias[l].reshape(N_EXPERTS, 1))
        shared = (sh_w_gate[l].astype(BF16), sh_w_up[l].astype(BF16), sh_w_down[l].astype(BF16))
        outs = {}
        for ch in sorted(range(len(MOE_CHUNKS)), key=lambda j: -MOE_CHUNKS[j][1]):
            row0, rows = MOE_CHUNKS[ch]
            n_tiles = _n_tiles(rows)
            cnt = counts[ch, :, 0].astype(jnp.int32)
            start, tile_expert, tile_valid, n_used, plan = _expert_layout(cnt, n_tiles)
            dest = _dest_rows(start, idx[:, row0:row0 + rows], pos[:, row0:row0 + rows])
            xs_a, xs_b = _sc_scatter_rows(hp_a, hp_b, dest, row0, n_tiles * FFN_TM)
            ys_a, ys_b = _expert_ffn(xs_a, xs_b, tile_expert, tile_valid, n_used, plan,
                                     exp_w_gate, exp_w_up, exp_w_down, l)
            first = 0
            for eighths in COMBINE_EIGHTHS:
                piece = rows * eighths // 8
                dest_row = dest[:, first:first + piece].reshape(1, piece * TOP_K)
                g_a, g_b = _sc_gather_rows(ys_a, ys_b, dest_row)
                g_a = g_a.reshape(TOP_K, piece, D_QUARTER)
                g_b = g_b.reshape(TOP_K, piece, D_QUARTER)
                outs = _combine(xin, mods[l], wcol, g_a, g_b, *shared, *ln(l, 1), row0=row0 + first,
                                split_streams=split_streams, prev=outs)
                first += piece
        return (outs["ctx"], outs["lat"]) if split_streams else outs["all"]

    x = _conv_mixer(x_prompt.reshape(T_CTX, D_MODEL), x_sample.reshape(T_LAT, D_MODEL), mods[0],
                    conv_w_in[0].astype(BF16), conv_k[0], conv_w_out[0].astype(BF16), *ln(0, 0))
    x = moe_layer(x, 0, split_streams=False)

    p = _mla_params(mla_w_dqkv[0], mla_q_norm[0], mla_w_uq[0], mla_kv_norm[0], mla_w_ukv[0])
    rope_a, rope_b = _rope_tables()
    q, k, v, ckv, kr = _mla_proj(x, mods[1], p, rope_a, rope_b)
    kr_cache = jnp.concatenate([cache_krope[:, 0].reshape(DEC_BATCH * PAST_LEN, QK_ROPE),
                                jnp.zeros((DEC_BATCH * PAST_LEN, QK_ROPE), F32)], axis=-1)
    kc, vc = _cache_kv(cache_ckv[:, 0].reshape(DEC_BATCH * PAST_LEN, KV_LORA), kr_cache, p)
    o_ctx = _ctx_attention(q, k, v)
    o_lat = _lat_attention(q, k, v, kc, vc)
    x = _oproj(o_ctx, o_lat, x, mods[1], mla_w_o[0].astype(BF16), *ln(1, 0))
    y_ctx, y_lat = moe_layer(x, 1, split_streams=True)

    y_prompt = y_ctx.reshape(BATCH, SEQ, D_MODEL)
    y_sample = y_lat.reshape(DEC_BATCH, DEC_SEQ, D_MODEL)
    state_ckv = ckv[:T_CTX].reshape(BATCH, 1, SEQ, KV_LORA)
    state_krope = kr[:T_CTX, :QK_ROPE].reshape(BATCH, 1, SEQ, QK_ROPE)
    return (y_prompt, y_sample, state_ckv, state_krope)
```

```python
import functools
import math

import jax
import jax.numpy as jnp
import numpy as np
from jax import lax
from jax.experimental import pallas as pl
from jax.experimental.pallas import tpu as pltpu
from jax.experimental.pallas import tpu_sc as plsc

D_MODEL = 1024
BATCH = 16
SEQ = 256
DEPTH = 2
DEC_BATCH = 4
DEC_SEQ = 4096
PAST_LEN = 512
GRID_W = 64

N_HEADS = 8
QK_NOPE = 128
QK_ROPE = 64
V_DIM = 128
Q_LORA = 384
KV_LORA = 256
ROPE_THETA = 10000.0
ATTN_SCALE = (QK_NOPE + QK_ROPE) ** -0.5
HEAD_PAD = 256
Q_PRESCALE = ATTN_SCALE * math.log2(math.e)

N_EXPERTS = 64
TOP_K = 8
N_GROUPS = 8
TOPK_GROUPS = 4
GROUP_SIZE = N_EXPERTS // N_GROUPS
D_EXPERT = 256
D_SHARED = 256
ROUTED_SCALE = 2.5

ALPHA = (2 * DEPTH) ** 0.25
LN_EPS = 1e-5
RMS_EPS = 1e-6

GROUP_ROWS = 4096
N_GROUPS_ROWS = 1 + DEC_BATCH
T_CTX = BATCH * SEQ
T_LAT = DEC_BATCH * DEC_SEQ
T_ALL = T_CTX + T_LAT
LANES = 128
SUBLANES = 8

F32 = jnp.float32
BF16 = jnp.bfloat16
NEG_INF = float("-inf")


def _dot(a, b):
    return jnp.dot(a, b, preferred_element_type=F32)


def _dot_nt(a, b, precision=None):
    return lax.dot_general(a, b, (((1,), (1,)), ((), ())), precision=precision,
                           preferred_element_type=F32)


def _layer_norm(v, g, b):
    mu = jnp.mean(v, axis=-1, keepdims=True)
    d = v - mu
    var = jnp.mean(d * d, axis=-1, keepdims=True)
    return d * lax.rsqrt(var + LN_EPS) * g + b


def _rms_norm(v, g):
    return v * lax.rsqrt(jnp.mean(v * v, axis=-1, keepdims=True) + RMS_EPS) * g


def _silu(v):
    return v / (1.0 + jnp.exp(-v))


def _sigmoid(v):
    return 1.0 / (1.0 + jnp.exp(-v))


ADALN_COLS = 2


def _adaln_kernel(c_ref, w_ref, b_ref, o_ref):
    c = c_ref[...]
    s = _silu(c)
    res = jnp.dot(s, w_ref[0], precision=lax.Precision.HIGHEST, preferred_element_type=F32) + b_ref[0]
    for j in range(ADALN_COLS):
        o_ref[0, j] = res[:, j * D_MODEL:(j + 1) * D_MODEL]


def _adaln(cvecs, ada_w, ada_b):
    out = pl.pallas_call(
        _adaln_kernel,
        grid=(DEPTH, 6 // ADALN_COLS),
        in_specs=[
            pl.BlockSpec((SUBLANES, D_MODEL), lambda l, j: (0, 0)),
            pl.BlockSpec((1, D_MODEL, ADALN_COLS * D_MODEL), lambda l, j: (l, 0, j)),
            pl.BlockSpec((1, 1, ADALN_COLS * D_MODEL), lambda l, j: (l, 0, j)),
        ],
        out_specs=pl.BlockSpec((1, ADALN_COLS, SUBLANES, D_MODEL), lambda l, j: (l, j, 0, 0)),
        out_shape=jax.ShapeDtypeStruct((DEPTH, 6, SUBLANES, D_MODEL), F32),
        compiler_params=pltpu.CompilerParams(dimension_semantics=("parallel", "parallel")),
        name="adaln",
    )(cvecs, ada_w, ada_b.reshape(DEPTH, 1, 6 * D_MODEL))
    return jnp.transpose(out[:, :, :N_GROUPS_ROWS, :], (0, 2, 1, 3))


CONV_TM = 1024
CONV_PIECES = 2


def _conv_kernel(xc_ref, xcp_ref, xcn_ref, xl_ref, xlp_ref, xln_ref, mods_ref, win_ref, ck_ref, wout_ref,
                 g_ref, b_ref, o_ref):
    i = pl.program_id(0)
    tm = xc_ref.shape[0]
    is_ctx = i < T_CTX // tm
    m = mods_ref[0]
    sm, cm, gm = m[0:1], m[1:2], m[2:3]
    x = jnp.where(is_ctx, xc_ref[...], xl_ref[...])
    xp = jnp.where(is_ctx, xcp_ref[...], xlp_ref[...])
    xn = jnp.where(is_ctx, xcn_ref[...], xln_ref[...])
    th = tm // CONV_PIECES
    halves = [x[j * th:(j + 1) * th] for j in range(CONV_PIECES)]
    lhs = list(halves)
    lhs[0] = jnp.concatenate([xp, lhs[0]], axis=0)
    lhs[-1] = jnp.concatenate([lhs[-1], xn], axis=0)
    zs = [_dot((xh * (1.0 + cm) + sm).astype(BF16), win_ref[...]) for xh in lhs]
    us = [z[:, D_MODEL:2 * D_MODEL] * z[:, 2 * D_MODEL:] for z in zs]
    u_before = us[0][SUBLANES - 1:SUBLANES]
    u_after = us[-1][us[-1].shape[0] - SUBLANES:us[-1].shape[0] - SUBLANES + 1]
    zs[0], us[0] = zs[0][SUBLANES:], us[0][SUBLANES:]
    zs[-1], us[-1] = zs[-1][:th], us[-1][:th]
    befores = [u_before] + [u[th - 1:th] for u in us[:-1]]
    afters = [u[0:1] for u in us[1:]] + [u_after]

    ck = ck_ref[...]
    row = lax.broadcasted_iota(jnp.int32, (th, 1), 0)
    for hf in range(CONV_PIECES):
        grow = i * tm + hf * th + row
        seq_len = jnp.where(grow < T_CTX, SEQ, DEC_SEQ)
        pos = jnp.bitwise_and(grow, seq_len - 1)
        u = us[hf]
        left = jnp.where(row == 0, befores[hf], pltpu.roll(u, 1, 0))
        left = jnp.where(pos == 0, 0.0, left)
        right = jnp.where(row == th - 1, afters[hf], pltpu.roll(u, th - 1, 0))
        right = jnp.where(pos == seq_len - 1, 0.0, right)
        conv = left * ck[0:1] + u * ck[1:2] + right * ck[2:3]
        v = (zs[hf][:, :D_MODEL] * conv).astype(BF16)
        y = _dot(v, wout_ref[...])
        o_ref[hf * th:(hf + 1) * th, :] = _layer_norm(ALPHA * halves[hf] + gm * y, g_ref[...], b_ref[...])


def _conv_mixer(x_ctx, x_lat, mods, w_in, conv_k, w_out, ln_g, ln_b):
    tm = CONV_TM
    per8 = tm // SUBLANES
    n_ctx = T_CTX // tm
    n_lat = T_LAT // tm

    def stream_specs(first, n):
        def blk(i):
            return jnp.clip(i - first, 0, n - 1)
        return [
            pl.BlockSpec((tm, D_MODEL), lambda i: (blk(i), 0)),
            pl.BlockSpec((SUBLANES, D_MODEL), lambda i: (jnp.maximum(blk(i) * per8 - 1, 0), 0)),
            pl.BlockSpec((SUBLANES, D_MODEL), lambda i: (jnp.minimum((blk(i) + 1) * per8, n * per8 - 1), 0)),
        ]

    return pl.pallas_call(
        _conv_kernel,
        grid=(n_ctx + n_lat,),
        in_specs=stream_specs(0, n_ctx) + stream_specs(n_ctx, n_lat) + [
            pl.BlockSpec((1, 6, D_MODEL), lambda i: (i * tm // GROUP_ROWS, 0, 0)),
            pl.BlockSpec((D_MODEL, 3 * D_MODEL), lambda i: (0, 0)),
            pl.BlockSpec((3, D_MODEL), lambda i: (0, 0)),
            pl.BlockSpec((D_MODEL, D_MODEL), lambda i: (0, 0)),
            pl.BlockSpec((1, D_MODEL), lambda i: (0, 0)),
            pl.BlockSpec((1, D_MODEL), lambda i: (0, 0)),
        ],
        out_specs=pl.BlockSpec((tm, D_MODEL), lambda i: (i, 0)),
        out_shape=jax.ShapeDtypeStruct((T_ALL, D_MODEL), F32),
        compiler_params=pltpu.CompilerParams(dimension_semantics=("parallel",)),
        name="conv_mixer",
    )(x_ctx, x_ctx, x_ctx, x_lat, x_lat, x_lat, mods, w_in, conv_k, w_out, ln_g, ln_b)


ROUTER_TM = 512
MOE_CHUNKS = ((0, T_ALL),)
COMBINE_EIGHTHS = (2, 2, 2, 2)


def _first_argmax_mask(cur, ridx, n):
    mx = jnp.max(cur, axis=0, keepdims=True)
    first = jnp.min(jnp.where(cur == mx, ridx, n), axis=0, keepdims=True)
    return ridx == first, mx


def _router_kernel(x_ref, mods_ref, rwt_ref, bias_ref, before_ref, idx_ref, pos_ref, wcol_ref, count_ref,
                   hpa_ref, hpb_ref, carry_ref):
    tm = x_ref.shape[0]
    m = mods_ref[0]
    sf, cf = m[3:4], m[4:5]
    hc = x_ref[...] * (1.0 + cf) + sf
    words = _pack_rows(hc)
    hpa_ref[...] = words[:, :D_QUARTER]
    hpb_ref[...] = words[:, D_QUARTER:]
    logits = _dot_nt(rwt_ref[...], hc, precision=lax.Precision.HIGHEST)
    scores = _sigmoid(logits)
    biased = scores + bias_ref[...]

    ridx8 = lax.broadcasted_iota(jnp.int32, (GROUP_SIZE, tm), 0)
    gscore = jnp.full((N_GROUPS, tm), NEG_INF, F32)
    for g in range(N_GROUPS):
        blk = biased[g * GROUP_SIZE:(g + 1) * GROUP_SIZE]
        sel, m1 = _first_argmax_mask(blk, ridx8, GROUP_SIZE)
        m2 = jnp.max(jnp.where(sel, NEG_INF, blk), axis=0, keepdims=True)
        gscore = jnp.where(ridx8 == g, m1 + m2, gscore)

    gmask = jnp.zeros((N_GROUPS, tm), jnp.bool_)
    cur = gscore
    for _ in range(TOPK_GROUPS):
        sel, _unused = _first_argmax_mask(cur, ridx8, N_GROUPS)
        gmask = jnp.logical_or(gmask, sel)
        cur = jnp.where(sel, NEG_INF, cur)

    gmask_f = gmask.astype(F32)
    blocks = []
    for g in range(N_GROUPS):
        keep = jnp.broadcast_to(gmask_f[g:g + 1], (GROUP_SIZE, tm)) > 0.5
        blocks.append(jnp.where(keep, biased[g * GROUP_SIZE:(g + 1) * GROUP_SIZE], NEG_INF))
    cur = jnp.concatenate(blocks, axis=0)

    first_tiles = [row0 // tm for row0, _rows in MOE_CHUNKS]
    starts_chunk = functools.reduce(jnp.logical_or, [pl.program_id(0) == ft for ft in first_tiles])

    @pl.when(starts_chunk)
    def _():
        carry_ref[...] = jnp.zeros(carry_ref.shape, F32)

    ridx = lax.broadcasted_iota(jnp.int32, (N_EXPERTS, tm), 0)
    kidx = lax.broadcasted_iota(jnp.int32, (TOP_K, tm), 0)
    sels = []
    chosen = jnp.zeros((N_EXPERTS, tm), jnp.bool_)
    idx_rows = jnp.zeros((TOP_K, tm), jnp.int32)
    for k in range(TOP_K):
        mx = jnp.max(cur, axis=0, keepdims=True)
        first = jnp.min(jnp.where(cur == mx, ridx, N_EXPERTS), axis=0, keepdims=True)
        sel = ridx == first
        sels.append(sel)
        chosen = jnp.logical_or(chosen, sel)
        idx_rows = jnp.where(kidx == k, first, idx_rows)
        cur = jnp.where(sel, NEG_INF, cur)

    onehot = chosen.astype(F32)
    rank = carry_ref[...] + _dot(onehot.astype(BF16), before_ref[...])
    carry_ref[...] = carry_ref[...] + jnp.sum(onehot, axis=1, keepdims=True)
    count_ref[0] = jnp.broadcast_to(carry_ref[...], count_ref.shape[1:])

    w = jnp.where(chosen, scores, 0.0)
    w = w / jnp.sum(w, axis=0, keepdims=True) * ROUTED_SCALE
    pos_rows = jnp.zeros((TOP_K, tm), F32)
    w_rows = jnp.zeros((TOP_K, tm), F32)
    for k in range(TOP_K):
        pos_rows = jnp.where(kidx == k, jnp.sum(jnp.where(sels[k], rank, 0.0), axis=0, keepdims=True), pos_rows)
        w_rows = jnp.where(kidx == k, jnp.sum(jnp.where(sels[k], w, 0.0), axis=0, keepdims=True), w_rows)
    idx_ref[...] = idx_rows
    pos_ref[...] = pos_rows.astype(jnp.int32)
    wpad = jnp.concatenate([w_rows, jnp.zeros((LANES - TOP_K, tm), F32)], axis=0)
    wcol_ref[...] = wpad.T


def _router(x, mods, router_wt, router_bias):
    t = x.shape[0]
    tm = ROUTER_TM
    before = jnp.asarray(np.triu(np.ones((tm, tm), np.float32), 1), dtype=BF16)

    def chunk_of(i):
        return sum((i >= row0 // tm).astype(jnp.int32) for row0, _rows in MOE_CHUNKS[1:])

    return pl.pallas_call(
        _router_kernel,
        grid=(t // tm,),
        in_specs=[
            pl.BlockSpec((tm, D_MODEL), lambda i: (i, 0)),
            pl.BlockSpec((1, 6, D_MODEL), lambda i: (i * tm // GROUP_ROWS, 0, 0)),
            pl.BlockSpec((N_EXPERTS, D_MODEL), lambda i: (0, 0)),
            pl.BlockSpec((N_EXPERTS, 1), lambda i: (0, 0)),
            pl.BlockSpec((tm, tm), lambda i: (0, 0)),
        ],
        out_specs=[
            pl.BlockSpec((TOP_K, tm), lambda i: (0, i)),
            pl.BlockSpec((TOP_K, tm), lambda i: (0, i)),
            pl.BlockSpec((tm, LANES), lambda i: (i, 0)),
            pl.BlockSpec((1, N_EXPERTS, LANES), lambda i: (chunk_of(i), 0, 0)),
            pl.BlockSpec((tm, D_QUARTER), lambda i: (i, 0)),
            pl.BlockSpec((tm, D_QUARTER), lambda i: (i, 0)),
        ],
        out_shape=[
            jax.ShapeDtypeStruct((TOP_K, t), jnp.int32),
            jax.ShapeDtypeStruct((TOP_K, t), jnp.int32),
            jax.ShapeDtypeStruct((t, LANES), F32),
            jax.ShapeDtypeStruct((len(MOE_CHUNKS), N_EXPERTS, LANES), F32),
            jax.ShapeDtypeStruct((t, D_QUARTER), jnp.uint32),
            jax.ShapeDtypeStruct((t, D_QUARTER), jnp.uint32),
        ],
        scratch_shapes=[pltpu.VMEM((N_EXPERTS, 1), F32)],
        compiler_params=pltpu.CompilerParams(dimension_semantics=("arbitrary",)),
        name="moe_router",
    )(x, mods, router_wt, router_bias, before)


FFN_TM = 512


def _n_tiles(rows):
    return rows * TOP_K // FFN_TM + N_EXPERTS


def _expert_layout(counts, n_tiles):
    padded = (counts + FFN_TM - 1) // FFN_TM * FFN_TM
    end = jnp.cumsum(padded)
    start = end - padded
    tile_row = jnp.arange(n_tiles, dtype=jnp.int32) * FFN_TM
    tile_expert = jnp.minimum(jnp.sum(end[None, :] <= tile_row[:, None], axis=1), N_EXPERTS - 1)
    of_tile = tile_expert[:, None] == jnp.arange(N_EXPERTS, dtype=jnp.int32)[None, :]
    live_end = jnp.sum(jnp.where(of_tile, (start + counts)[None, :], 0), axis=1)
    tile_valid = jnp.clip(live_end - tile_row, 0, FFN_TM)
    n_used = (end[-1] // FFN_TM).astype(jnp.int32).reshape(1)
    used = jnp.arange(n_tiles, dtype=jnp.int32) < n_used[0]
    prev_expert = jnp.concatenate([jnp.full((1,), -1, tile_expert.dtype), tile_expert[:-1]])
    run_flag = jnp.logical_and(used, tile_expert != prev_expert)
    run_slot = (jnp.cumsum(run_flag.astype(jnp.int32)) - 1) % 2
    later = jnp.logical_and(used[None, :], tile_expert[None, :] > tile_expert[:, None])
    next_expert = jnp.min(jnp.where(later, tile_expert[None, :], N_EXPERTS), axis=1)
    next_expert = jnp.where(next_expert == N_EXPERTS, -1, next_expert)
    plan = (run_flag.astype(jnp.int32), run_slot.astype(jnp.int32), next_expert.astype(jnp.int32))
    return start.astype(jnp.int32), tile_expert.astype(jnp.int32), tile_valid.astype(jnp.int32), n_used, plan


D_HALF = D_MODEL // 2
D_QUARTER = D_MODEL // 4


def _pack_rows(v):
    hi = lax.bitcast_convert_type(v[:, :D_HALF].astype(BF16).astype(F32), jnp.uint32)
    lo = lax.bitcast_convert_type(v[:, D_HALF:].astype(BF16).astype(F32), jnp.uint32)
    return jnp.bitwise_or(hi, jnp.right_shift(lo, jnp.uint32(16)))


def _unpack_rows(w):
    hi = lax.bitcast_convert_type(jnp.bitwise_and(w, jnp.uint32(0xFFFF0000)), F32)
    lo = lax.bitcast_convert_type(jnp.left_shift(w, jnp.uint32(16)), F32)
    return hi, lo


DEST_TM = 2048


def _dest_kernel(start_ref, idx_ref, pos_ref, dest_ref):
    idx = idx_ref[...]
    base = jnp.zeros(idx.shape, jnp.int32)
    for e in range(N_EXPERTS):
        base = jnp.where(idx == e, start_ref[e], base)
    dest_ref[...] = base + pos_ref[...]


def _dest_rows(start, idx, pos):
    t = idx.shape[1]
    return pl.pallas_call(
        _dest_kernel,
        grid_spec=pltpu.PrefetchScalarGridSpec(
            num_scalar_prefetch=1,
            grid=(t // DEST_TM,),
            in_specs=[pl.BlockSpec((TOP_K, DEST_TM), lambda i, s: (0, i)),
                      pl.BlockSpec((TOP_K, DEST_TM), lambda i, s: (0, i))],
            out_specs=pl.BlockSpec((TOP_K, DEST_TM), lambda i, s: (0, i)),
        ),
        out_shape=jax.ShapeDtypeStruct((TOP_K, t), jnp.int32),
        compiler_params=pltpu.CompilerParams(dimension_semantics=("parallel",)),
        name="moe_dest",
    )(start, idx, pos)


SC_WINDOW = 128


def _sc_mesh():
    return plsc.VectorSubcoreMesh(core_axis_name="c", subcore_axis_name="s")


def _sc_scatter_rows(x_a, x_b, dest, row0, n_sorted):
    t = dest.shape[1]
    blk0 = row0 // SC_WINDOW
    out = jax.ShapeDtypeStruct((n_sorted, D_QUARTER), x_a.dtype)

    @functools.partial(pl.kernel, out_type=(out, out), mesh=_sc_mesh(), scratch_types=[])
    def scatter(xa_hbm, xb_hbm, i_hbm, oa_hbm, ob_hbm):
        for x_hbm, o_hbm in ((xa_hbm, oa_hbm), (xb_hbm, ob_hbm)):
            def body(x_vmem, i_vmem, o_hbm=o_hbm):
                pltpu.sync_copy(x_vmem, o_hbm.at[i_vmem.at[0]])

            pltpu.emit_pipeline(
                body,
                grid=(t // SC_WINDOW, TOP_K),
                in_specs=[pl.BlockSpec((SC_WINDOW, D_QUARTER), lambda i, k: (i + blk0, 0)),
                          pl.BlockSpec((1, SC_WINDOW), lambda i, k: (k, i))],
                out_specs=[],
                core_axis_name=("c", "s"),
                dimension_semantics=(pltpu.PARALLEL, pltpu.ARBITRARY),
            )(x_hbm, i_hbm)

    return scatter(x_a, x_b, dest)


def _sc_gather_rows(table_a, table_b, idx):
    m = idx.shape[1]
    out = jax.ShapeDtypeStruct((m, D_QUARTER), table_a.dtype)

    @functools.partial(pl.kernel, out_type=(out, out), mesh=_sc_mesh(), scratch_types=[])
    def gather(ta_hbm, tb_hbm, i_hbm, oa_hbm, ob_hbm):
        for t_hbm, o_hbm in ((ta_hbm, oa_hbm), (tb_hbm, ob_hbm)):
            def body(i_vmem, o_vmem, t_hbm=t_hbm):
                pltpu.sync_copy(t_hbm.at[i_vmem.at[0]], o_vmem)

            pltpu.emit_pipeline(
                body,
                grid=(m // SC_WINDOW,),
                in_specs=[pl.BlockSpec((1, SC_WINDOW), lambda i: (0, i))],
                out_specs=[pl.BlockSpec((SC_WINDOW, D_QUARTER), lambda i: (i, 0))],
                core_axis_name=("c", "s"),
                dimension_semantics=(pltpu.PARALLEL,),
            )(i_hbm, o_hbm)

    return gather(table_a, table_b, idx)


FFN_STEP_TILES = 2


def _ffn_kernel(layer, te_ref, tv_ref, nu_ref, flag_ref, slot_ref, next_ref, xa_ref, xb_ref,
                wg_hbm, wu_hbm, wd_hbm, ya_ref, yb_ref, wg_st, wu_st, wd_st, wgb_ref, wub_ref, wdb_ref, sem):
    step = pl.program_id(0)

    def fetch(expert, slot):
        return [pltpu.make_async_copy(src.at[layer, expert], dst.at[slot], sem.at[slot])
                for src, dst in ((wg_hbm, wg_st), (wu_hbm, wu_st), (wd_hbm, wd_st))]

    for s in range(FFN_STEP_TILES):
        tile = step * FFN_STEP_TILES + s

        @pl.when(jnp.logical_and(tile < nu_ref[0], flag_ref[tile] == 1))
        def _():
            slot = slot_ref[tile]

            @pl.when(tile == 0)
            def _():
                for cp in fetch(te_ref[tile], slot):
                    cp.start()

            for cp in fetch(te_ref[tile], slot):
                cp.wait()
            wgb_ref[slot] = wg_st[slot].astype(BF16)
            wub_ref[slot] = wu_st[slot].astype(BF16)
            wdb_ref[slot] = wd_st[slot].astype(BF16)

            @pl.when(next_ref[tile] >= 0)
            def _():
                for cp in fetch(next_ref[tile], 1 - slot):
                    cp.start()

    @pl.when(step * FFN_STEP_TILES < nu_ref[0])
    def _():
        for s in range(FFN_STEP_TILES):
            tile = step * FFN_STEP_TILES + s
            slot = slot_ref[tile]
            wg = wgb_ref[slot]
            wu = wub_ref[slot]
            wd = wdb_ref[slot]
            rows = pl.ds(s * FFN_TM, FFN_TM)
            live = lax.broadcasted_iota(jnp.int32, (FFN_TM, D_QUARTER), 0) < tv_ref[tile]
            hi_a, lo_a = _unpack_rows(jnp.where(live, xa_ref[rows, :], jnp.uint32(0)))
            hi_b, lo_b = _unpack_rows(jnp.where(live, xb_ref[rows, :], jnp.uint32(0)))
            xb = jnp.concatenate([hi_a.astype(BF16), hi_b.astype(BF16), lo_a.astype(BF16), lo_b.astype(BF16)],
                                 axis=1)
            a = (_silu(_dot(xb, wg)) * _dot(xb, wu)).astype(BF16)
            words = _pack_rows(_dot(a, wd))
            ya_ref[rows, :] = words[:, :D_QUARTER]
            yb_ref[rows, :] = words[:, D_QUARTER:]

    @pl.when(step * FFN_STEP_TILES >= nu_ref[0])
    def _():
        ya_ref[...] = jnp.zeros(ya_ref.shape, jnp.uint32)
        yb_ref[...] = jnp.zeros(yb_ref.shape, jnp.uint32)


def _expert_ffn(xs_a, xs_b, tile_expert, tile_valid, n_used, plan, wg, wu, wd, layer):
    n_tiles = xs_a.shape[0] // FFN_TM
    step_rows = FFN_STEP_TILES * FFN_TM

    def row_map(i, te, tv, nu, fl, sl, nx):
        return (jnp.minimum(i, (nu[0] - 1) // FFN_STEP_TILES), 0)

    def out_map(i, te, tv, nu, fl, sl, nx):
        return (i, 0)

    hbm = pl.BlockSpec(memory_space=pl.ANY)
    return pl.pallas_call(
        functools.partial(_ffn_kernel, layer),
        grid_spec=pltpu.PrefetchScalarGridSpec(
            num_scalar_prefetch=6,
            grid=(n_tiles // FFN_STEP_TILES,),
            in_specs=[pl.BlockSpec((step_rows, D_QUARTER), row_map),
                      pl.BlockSpec((step_rows, D_QUARTER), row_map), hbm, hbm, hbm],
            out_specs=[pl.BlockSpec((step_rows, D_QUARTER), out_map), pl.BlockSpec((step_rows, D_QUARTER), out_map)],
            scratch_shapes=[pltpu.VMEM((2, D_MODEL, D_EXPERT), F32), pltpu.VMEM((2, D_MODEL, D_EXPERT), F32),
                            pltpu.VMEM((2, D_EXPERT, D_MODEL), F32),
                            pltpu.VMEM((2, D_MODEL, D_EXPERT), BF16), pltpu.VMEM((2, D_MODEL, D_EXPERT), BF16),
                            pltpu.VMEM((2, D_EXPERT, D_MODEL), BF16), pltpu.SemaphoreType.DMA((2,))],
        ),
        out_shape=[jax.ShapeDtypeStruct((n_tiles * FFN_TM, D_QUARTER), jnp.uint32),
                   jax.ShapeDtypeStruct((n_tiles * FFN_TM, D_QUARTER), jnp.uint32)],
        compiler_params=pltpu.CompilerParams(dimension_semantics=("arbitrary",)),
        name="moe_expert_ffn",
    )(tile_expert, tile_valid, n_used, *plan, xs_a, xs_b, wg, wu, wd)


COMBINE_TM = 512


def _combine_kernel(n_prev, tile0, x_ref, mods_ref, wcol_ref, ga_ref, gb_ref, sg_ref, su_ref, sd_ref, g_ref, b_ref,
                    *rest):
    o_refs = rest[n_prev:]
    m = mods_ref[0]
    sf, cf, gf = m[3:4], m[4:5], m[5:6]
    x = x_ref[...]
    hc = (x * (1.0 + cf) + sf).astype(BF16)
    a = _silu(_dot(hc, sg_ref[...])) * _dot(hc, su_ref[...])
    y = _dot(a.astype(BF16), sd_ref[...])
    wcol = wcol_ref[...]
    parts = [y[:, q * D_QUARTER:(q + 1) * D_QUARTER] for q in range(4)]
    for k in range(TOP_K):
        hi_a, lo_a = _unpack_rows(ga_ref[k])
        hi_b, lo_b = _unpack_rows(gb_ref[k])
        wk = wcol[:, k:k + 1]
        parts = [parts[0] + wk * hi_a, parts[1] + wk * hi_b, parts[2] + wk * lo_a, parts[3] + wk * lo_b]
    y = jnp.concatenate(parts, axis=1)
    out = _layer_norm(ALPHA * x + gf * y, g_ref[...], b_ref[...])
    if len(o_refs) == 1:
        o_refs[0][...] = out
    else:
        is_ctx = pl.program_id(0) + tile0 < T_CTX // x_ref.shape[0]

        @pl.when(is_ctx)
        def _():
            o_refs[0][...] = out

        @pl.when(jnp.logical_not(is_ctx))
        def _():
            o_refs[1][...] = out


def _combine(x, mods, wcol, g_a, g_b, sg, su, sd, ln_g, ln_b, row0, split_streams, prev):
    tm = COMBINE_TM
    n = g_a.shape[1] // tm
    tile0 = row0 // tm
    n_ctx = T_CTX // tm
    full = lambda shape: pl.BlockSpec(shape, lambda i: (0,) * len(shape))
    layouts = {
        "all": (lambda i: (tile0 + i, 0), T_ALL),
        "ctx": (lambda i: (jnp.minimum(tile0 + i, n_ctx - 1), 0), T_CTX),
        "lat": (lambda i: (jnp.maximum(tile0 + i - n_ctx, 0), 0), T_LAT),
    }
    if split_streams:
        kinds = (["ctx"] if tile0 < n_ctx else []) + (["lat"] if tile0 + n > n_ctx else [])
    else:
        kinds = ["all"]
    carried = [kd for kd in kinds if kd in prev]
    outs = pl.pallas_call(
        functools.partial(_combine_kernel, len(carried), tile0),
        grid=(n,),
        in_specs=[
            pl.BlockSpec((tm, D_MODEL), lambda i: (tile0 + i, 0)),
            pl.BlockSpec((1, 6, D_MODEL), lambda i: ((tile0 + i) * tm // GROUP_ROWS, 0, 0)),
            pl.BlockSpec((tm, LANES), lambda i: (tile0 + i, 0)),
            pl.BlockSpec((TOP_K, tm, D_QUARTER), lambda i: (0, i, 0)),
            pl.BlockSpec((TOP_K, tm, D_QUARTER), lambda i: (0, i, 0)),
            full((D_MODEL, D_SHARED)), full((D_MODEL, D_SHARED)), full((D_SHARED, D_MODEL)),
            full((1, D_MODEL)), full((1, D_MODEL)),
        ] + [pl.BlockSpec(memory_space=pl.ANY) for _ in carried],
        out_specs=[pl.BlockSpec((tm, D_MODEL), layouts[kd][0]) for kd in kinds],
        out_shape=[jax.ShapeDtypeStruct((layouts[kd][1], D_MODEL), F32) for kd in kinds],
        input_output_aliases={10 + j: kinds.index(kd) for j, kd in enumerate(carried)},
        compiler_params=pltpu.CompilerParams(dimension_semantics=("arbitrary",)),
        name="moe_combine",
    )(x, mods, wcol, g_a, g_b, sg, su, sd, ln_g, ln_b, *[prev[kd] for kd in carried])
    return {**prev, **dict(zip(kinds, outs))}


MLA_TM = 1024


def _mla_proj_kernel(x_ref, mods_ref, wdq_ref, wdkv_ref, wkr_ref, qn_ref, kvn_ref, wuq_ref,
                     wukn_ref, wuv_ref, ta_ref, tb_ref, q_ref, k_ref, v_ref, ckv_ref, kr_ref):
    m = mods_ref[0]
    sm, cm = m[0:1], m[1:2]
    h = (x_ref[...] * (1.0 + cm) + sm).astype(BF16)
    cq = _rms_norm(_dot(h, wdq_ref[...]), qn_ref[...])
    ckv = _rms_norm(_dot(h, wdkv_ref[...]), kvn_ref[...])
    kr2 = _dot(h, wkr_ref[...])
    ckv_ref[...] = ckv
    kr_ref[...] = kr2

    ka = ta_ref[0]
    kb = tb_ref[0]
    tm = ka.shape[0]
    ta = jnp.concatenate([jnp.full((tm, QK_NOPE), Q_PRESCALE, F32), ka * Q_PRESCALE], axis=1)
    tb = jnp.concatenate([jnp.zeros((tm, QK_NOPE), F32), kb * Q_PRESCALE], axis=1)
    krr = kr2 * ka + pltpu.roll(kr2, QK_ROPE, 1) * kb

    qpre = _dot(cq.astype(BF16), wuq_ref[...])
    ckv_b = ckv.astype(BF16)
    kn = _dot(ckv_b, wukn_ref[...])
    v_ref[...] = _dot(ckv_b, wuv_ref[...]).astype(BF16)
    for hd in range(N_HEADS):
        qh = qpre[:, hd * HEAD_PAD:(hd + 1) * HEAD_PAD]
        qrot = qh * ta + pltpu.roll(qh, HEAD_PAD - QK_ROPE, 1) * tb
        q_ref[:, hd * HEAD_PAD:(hd + 1) * HEAD_PAD] = qrot.astype(BF16)
        k_ref[:, hd * HEAD_PAD:hd * HEAD_PAD + QK_NOPE] = kn[:, hd * QK_NOPE:(hd + 1) * QK_NOPE].astype(BF16)
        k_ref[:, hd * HEAD_PAD + QK_NOPE:(hd + 1) * HEAD_PAD] = krr.astype(BF16)


def _mla_proj(x, mods, p, rope_a, rope_b):
    t = x.shape[0]
    tm = MLA_TM
    full = lambda shape: pl.BlockSpec(shape, lambda i: (0,) * len(shape))
    rope_spec = pl.BlockSpec(
        (1, tm, 2 * QK_ROPE),
        lambda i: (jnp.minimum(i * tm // GROUP_ROWS, 1), (i * tm % GROUP_ROWS) // tm, 0))
    return pl.pallas_call(
        _mla_proj_kernel,
        grid=(t // tm,),
        in_specs=[
            pl.BlockSpec((tm, D_MODEL), lambda i: (i, 0)),
            pl.BlockSpec((1, 6, D_MODEL), lambda i: (i * tm // GROUP_ROWS, 0, 0)),
            full((D_MODEL, Q_LORA)), full((D_MODEL, KV_LORA)), full((D_MODEL, 2 * QK_ROPE)),
            full((1, Q_LORA)), full((1, KV_LORA)),
            full((Q_LORA, N_HEADS * HEAD_PAD)),
            full((KV_LORA, N_HEADS * QK_NOPE)), full((KV_LORA, N_HEADS * V_DIM)),
            rope_spec, rope_spec,
        ],
        out_specs=[
            pl.BlockSpec((tm, N_HEADS * HEAD_PAD), lambda i: (i, 0)),
            pl.BlockSpec((tm, N_HEADS * HEAD_PAD), lambda i: (i, 0)),
            pl.BlockSpec((tm, N_HEADS * V_DIM), lambda i: (i, 0)),
            pl.BlockSpec((tm, KV_LORA), lambda i: (i, 0)),
            pl.BlockSpec((tm, 2 * QK_ROPE), lambda i: (i, 0)),
        ],
        out_shape=[
            jax.ShapeDtypeStruct((t, N_HEADS * HEAD_PAD), BF16),
            jax.ShapeDtypeStruct((t, N_HEADS * HEAD_PAD), BF16),
            jax.ShapeDtypeStruct((t, N_HEADS * V_DIM), BF16),
            jax.ShapeDtypeStruct((t, KV_LORA), F32),
            jax.ShapeDtypeStruct((t, 2 * QK_ROPE), F32),
        ],
        compiler_params=pltpu.CompilerParams(dimension_semantics=("parallel",)),
        name="mla_proj",
    )(x, mods, p["w_dq"], p["w_dkv"], p["w_kr"], p["q_norm"], p["kv_norm"], p["w_uq"],
      p["w_ukn"], p["w_uv"], rope_a, rope_b)


def _cache_kv_kernel(ckv_ref, kr_ref, wukn_ref, wuv_ref, k_ref, v_ref):
    ckv_b = ckv_ref[...].astype(BF16)
    kn = _dot(ckv_b, wukn_ref[...])
    v_ref[...] = _dot(ckv_b, wuv_ref[...]).astype(BF16)
    kr = kr_ref[...].astype(BF16)
    for hd in range(N_HEADS):
        k_ref[:, hd * HEAD_PAD:hd * HEAD_PAD + QK_NOPE] = kn[:, hd * QK_NOPE:(hd + 1) * QK_NOPE].astype(BF16)
        k_ref[:, hd * HEAD_PAD + QK_NOPE:(hd + 1) * HEAD_PAD] = kr


def _cache_kv(ckv, kr_pad, p):
    t = ckv.shape[0]
    tm = PAST_LEN
    full = lambda shape: pl.BlockSpec(shape, lambda i: (0,) * len(shape))
    return pl.pallas_call(
        _cache_kv_kernel,
        grid=(t // tm,),
        in_specs=[
            pl.BlockSpec((tm, KV_LORA), lambda i: (i, 0)),
            pl.BlockSpec((tm, 2 * QK_ROPE), lambda i: (i, 0)),
            full((KV_LORA, N_HEADS * QK_NOPE)), full((KV_LORA, N_HEADS * V_DIM)),
        ],
        out_specs=[
            pl.BlockSpec((tm, N_HEADS * HEAD_PAD), lambda i: (i, 0)),
            pl.BlockSpec((tm, N_HEADS * V_DIM), lambda i: (i, 0)),
        ],
        out_shape=[
            jax.ShapeDtypeStruct((t, N_HEADS * HEAD_PAD), BF16),
            jax.ShapeDtypeStruct((t, N_HEADS * V_DIM), BF16),
        ],
        compiler_params=pltpu.CompilerParams(dimension_semantics=("parallel",)),
        name="mla_cache_kv",
    )(ckv, kr_pad, p["w_ukn"], p["w_uv"])


def _ctx_attn_kernel(q_ref, k_ref, v_ref, o_ref):
    for hd in range(N_HEADS):
        q = q_ref[:, hd * HEAD_PAD:(hd + 1) * HEAD_PAD]
        k = k_ref[:, hd * HEAD_PAD:(hd + 1) * HEAD_PAD]
        s = _dot_nt(q, k)
        s = s - jnp.max(s, axis=-1, keepdims=True)
        p = jnp.exp2(s)
        p = p / jnp.sum(p, axis=-1, keepdims=True)
        o = _dot(p.astype(BF16), v_ref[:, hd * V_DIM:(hd + 1) * V_DIM])
        o_ref[:, hd * V_DIM:(hd + 1) * V_DIM] = o.astype(BF16)


def _ctx_attention(q, k, v):
    return pl.pallas_call(
        _ctx_attn_kernel,
        grid=(BATCH,),
        in_specs=[
            pl.BlockSpec((SEQ, N_HEADS * HEAD_PAD), lambda b: (b, 0)),
            pl.BlockSpec((SEQ, N_HEADS * HEAD_PAD), lambda b: (b, 0)),
            pl.BlockSpec((SEQ, N_HEADS * V_DIM), lambda b: (b, 0)),
        ],
        out_specs=pl.BlockSpec((SEQ, N_HEADS * V_DIM), lambda b: (b, 0)),
        out_shape=jax.ShapeDtypeStruct((T_CTX, N_HEADS * V_DIM), BF16),
        compiler_params=pltpu.CompilerParams(dimension_semantics=("parallel",)),
        name="ctx_attention",
    )(q, k, v)


LAT_TQ = 1024
LAT_TK = 512
LAT_PIECES = 4


def _lat_attn_kernel(q_ref, k_ref, v_ref, kc_ref, vc_ref, o_ref, s_ref, p_ref, m_ref):
    n_chunks = DEC_SEQ // LAT_TK + 1
    tp = q_ref.shape[0] // LAT_PIECES
    pieces = [pl.ds(j * tp, tp) for j in range(LAT_PIECES)]
    groups = [pieces[:2], pieces[2:]]
    state = {}

    def keys(c):
        return kc_ref[...] if c == n_chunks - 1 else k_ref[c * LAT_TK:(c + 1) * LAT_TK, :]

    def values(c):
        return vc_ref[...] if c == n_chunks - 1 else v_ref[c * LAT_TK:(c + 1) * LAT_TK, :]

    def qk(r, c):
        s = _dot_nt(q_ref[r, :], keys(c))
        s_ref[r, c * LAT_TK:(c + 1) * LAT_TK] = s
        mp = state.get(("m", r.start), jnp.full((tp, LANES), NEG_INF, F32))
        for j in range(LAT_TK // LANES):
            mp = jnp.maximum(mp, s[:, j * LANES:(j + 1) * LANES])
        state[("m", r.start)] = mp

    def row_max(r):
        m_ref[r, :] = jnp.broadcast_to(jnp.max(state[("m", r.start)], axis=-1, keepdims=True), (tp, LANES))

    def exp_chunk(r, c, after=None):
        m = m_ref[r, :]
        if after is not None:
            bits = lax.bitcast_convert_type(after[:, :LANES], jnp.uint32)
            zero = lax.shift_right_logical(lax.shift_right_logical(bits, jnp.uint32(16)), jnp.uint32(16))
            m = m + lax.bitcast_convert_type(zero, F32)
        for j in range(c * LAT_TK // LANES, (c + 1) * LAT_TK // LANES):
            p = jnp.exp2(s_ref[r, j * LANES:(j + 1) * LANES] - m)
            p_ref[r, j * LANES:(j + 1) * LANES] = p.astype(BF16)

    def pv(r, c):
        v = values(c)
        v_and_ones = jnp.concatenate([v, jnp.ones_like(v)], axis=1)
        acc = state.get(("a", r.start), jnp.zeros((tp, 2 * V_DIM), F32))
        state[("a", r.start)] = acc + _dot(p_ref[r, c * LAT_TK:(c + 1) * LAT_TK], v_and_ones)

    def finish(r):
        acc = state[("a", r.start)]
        o_ref[r, :] = (acc[:, :V_DIM] / acc[:, V_DIM:]).astype(BF16)

    for c in range(n_chunks):
        for r in groups[0]:
            qk(r, c)
    for r in groups[0]:
        row_max(r)
    for c in range(n_chunks):
        for r in groups[1]:
            qk(r, c)
        for r in groups[0]:
            exp_chunk(r, c)
    for r in groups[1]:
        row_max(r)
    for c in range(n_chunks):
        for r in groups[0]:
            pv(r, c)
        for r0, r in zip(groups[0], groups[1]):
            exp_chunk(r, c, after=state[("a", r0.start)])
    for r in groups[0]:
        finish(r)
    for c in range(n_chunks):
        for r in groups[1]:
            pv(r, c)
    for r in groups[1]:
        finish(r)


def _lat_attention(q, k, v, kc, vc):
    nq = DEC_SEQ // LAT_TQ
    return pl.pallas_call(
        _lat_attn_kernel,
        grid=(DEC_BATCH, N_HEADS, nq),
        in_specs=[
            pl.BlockSpec((LAT_TQ, HEAD_PAD), lambda b, h, i: ((b + 1) * nq + i, h)),
            pl.BlockSpec((DEC_SEQ, HEAD_PAD), lambda b, h, i: (b + 1, h)),
            pl.BlockSpec((DEC_SEQ, V_DIM), lambda b, h, i: (b + 1, h)),
            pl.BlockSpec((PAST_LEN, HEAD_PAD), lambda b, h, i: (b, h)),
            pl.BlockSpec((PAST_LEN, V_DIM), lambda b, h, i: (b, h)),
        ],
        out_specs=pl.BlockSpec((LAT_TQ, V_DIM), lambda b, h, i: (b * nq + i, h)),
        out_shape=jax.ShapeDtypeStruct((T_LAT, N_HEADS * V_DIM), BF16),
        scratch_shapes=[pltpu.VMEM((LAT_TQ, DEC_SEQ + PAST_LEN), F32),
                        pltpu.VMEM((LAT_TQ, DEC_SEQ + PAST_LEN), BF16),
                        pltpu.VMEM((LAT_TQ, LANES), F32)],
        compiler_params=pltpu.CompilerParams(
            dimension_semantics=("parallel", "parallel", "parallel")),
        name="lat_attention",
    )(q, k, v, kc, vc)


OPROJ_TM = 1024


def _oproj_kernel(oc_ref, ol_ref, x_ref, mods_ref, wo_ref, g_ref, b_ref, out_ref):
    is_ctx = pl.program_id(0) < T_CTX // OPROJ_TM
    o = jnp.where(is_ctx, oc_ref[...], ol_ref[...])
    gm = mods_ref[0][2:3]
    y = _dot(o, wo_ref[...])
    out_ref[...] = _layer_norm(ALPHA * x_ref[...] + gm * y, g_ref[...], b_ref[...])


def _oproj(o_ctx, o_lat, x, mods, w_o, ln_g, ln_b):
    t = x.shape[0]
    tm = OPROJ_TM
    n_ctx = T_CTX // tm
    return pl.pallas_call(
        _oproj_kernel,
        grid=(t // tm,),
        in_specs=[
            pl.BlockSpec((tm, N_HEADS * V_DIM), lambda i: (jnp.minimum(i, n_ctx - 1), 0)),
            pl.BlockSpec((tm, N_HEADS * V_DIM), lambda i: (jnp.maximum(i - n_ctx, 0), 0)),
            pl.BlockSpec((tm, D_MODEL), lambda i: (i, 0)),
            pl.BlockSpec((1, 6, D_MODEL), lambda i: (i * tm // GROUP_ROWS, 0, 0)),
            pl.BlockSpec((N_HEADS * V_DIM, D_MODEL), lambda i: (0, 0)),
            pl.BlockSpec((1, D_MODEL), lambda i: (0, 0)),
            pl.BlockSpec((1, D_MODEL), lambda i: (0, 0)),
        ],
        out_specs=pl.BlockSpec((tm, D_MODEL), lambda i: (i, 0)),
        out_shape=jax.ShapeDtypeStruct((t, D_MODEL), F32),
        compiler_params=pltpu.CompilerParams(dimension_semantics=("parallel",)),
        name="attn_oproj",
    )(o_ctx, o_lat, x, mods, w_o, ln_g, ln_b)


def _swap16(w):
    q = QK_ROPE // 4
    return jnp.concatenate([w[..., q:2 * q], w[..., :q], w[..., 3 * q:], w[..., 2 * q:3 * q]], axis=-1)


def _mla_params(w_dqkv, q_norm, w_uq, kv_norm, w_ukv):
    w_kr = w_dqkv[:, Q_LORA + KV_LORA:]
    wq = w_uq.reshape(Q_LORA, N_HEADS, QK_NOPE + QK_ROPE)
    wq_r = wq[..., QK_NOPE:]
    wq = jnp.concatenate([wq[..., :QK_NOPE], wq_r, _swap16(wq_r)], axis=-1)
    wkv = w_ukv.reshape(KV_LORA, N_HEADS, QK_NOPE + V_DIM)
    return {
        "w_dq": w_dqkv[:, :Q_LORA].astype(BF16),
        "w_dkv": w_dqkv[:, Q_LORA:Q_LORA + KV_LORA].astype(BF16),
        "w_kr": jnp.concatenate([w_kr, _swap16(w_kr)], axis=-1).astype(BF16),
        "q_norm": q_norm.reshape(1, Q_LORA),
        "kv_norm": kv_norm.reshape(1, KV_LORA),
        "w_uq": wq.reshape(Q_LORA, N_HEADS * HEAD_PAD).astype(BF16),
        "w_ukn": wkv[..., :QK_NOPE].reshape(KV_LORA, N_HEADS * QK_NOPE).astype(BF16),
        "w_uv": wkv[..., QK_NOPE:].reshape(KV_LORA, N_HEADS * V_DIM).astype(BF16),
    }


def _rope_tables():
    nf = QK_ROPE // 4
    t = np.arange(DEC_SEQ)
    row = (t // GRID_W).astype(np.float32)
    col = (t % GRID_W).astype(np.float32)
    inv = (ROPE_THETA ** (-np.arange(nf, dtype=np.float32) / nf)).astype(np.float32)
    ar, ac = row[:, None] * inv, col[:, None] * inv
    pad = np.zeros((DEC_SEQ, QK_ROPE), np.float32)
    cos = np.concatenate([np.cos(ar), np.cos(ar), np.cos(ac), np.cos(ac), pad], axis=-1)
    sin = np.concatenate([-np.sin(ar), np.sin(ar), -np.sin(ac), np.sin(ac), pad], axis=-1)
    cos_id = np.concatenate([np.ones((DEC_SEQ, QK_ROPE), np.float32), pad], axis=-1)
    sin_id = np.zeros((DEC_SEQ, 2 * QK_ROPE), np.float32)
    return (jnp.asarray(np.stack([cos_id, cos]).astype(np.float32)),
            jnp.asarray(np.stack([sin_id, sin]).astype(np.float32)))


def kernel(x_prompt, x_sample, cache_ckv, cache_krope, c, c_ctx, ada_w, ada_b, ln_g, ln_b, conv_w_in, conv_k, conv_w_out, mla_w_dqkv, mla_q_norm, mla_w_uq, mla_kv_norm, mla_w_ukv, mla_w_o, router_w, router_bias, exp_w_gate, exp_w_up, exp_w_down, sh_w_gate, sh_w_up, sh_w_down):
    cvecs =jnp.concatenate([c_ctx[None, :], c, jnp.zeros((SUBLANES - N_GROUPS_ROWS, D_MODEL), F32)], axis=0)
    mods = _adaln(cvecs, ada_w, ada_b)

    def ln(l, k):
        return ln_g[l, k].reshape(1, D_MODEL), ln_b[l, k].reshape(1, D_MODEL)

    def moe_layer(xin, l, split_streams):
        idx, pos, wcol, counts, hp_a, hp_b = _router(xin, mods[l], router_w[l].T,
                                                     router_bias[l].reshape(N_EXPERTS, 1))
        shared = (sh_w_gate[l].astype(BF16), sh_w_up[l].astype(BF16), sh_w_down[l].astype(BF16))
        outs = {}
        for ch in sorted(range(len(MOE_CHUNKS)), key=lambda j: -MOE_CHUNKS[j][1]):
            row0, rows = MOE_CHUNKS[ch]
            n_tiles = _n_tiles(rows)
            cnt = counts[ch, :, 0].astype(jnp.int32)
            start, tile_expert, tile_valid, n_used, plan = _expert_layout(cnt, n_tiles)
            dest = _dest_rows(start, idx[:, row0:row0 + rows], pos[:, row0:row0 + rows])
            xs_a, xs_b = _sc_scatter_rows(hp_a, hp_b, dest, row0, n_tiles * FFN_TM)
            ys_a, ys_b = _expert_ffn(xs_a, xs_b, tile_expert, tile_valid, n_used, plan,
                                     exp_w_gate, exp_w_up, exp_w_down, l)
            first = 0
            for eighths in COMBINE_EIGHTHS:
                piece = rows * eighths // 8
                dest_row = dest[:, first:first + piece].reshape(1, piece * TOP_K)
                g_a, g_b = _sc_gather_rows(ys_a, ys_b, dest_row)
                g_a = g_a.reshape(TOP_K, piece, D_QUARTER)
                g_b = g_b.reshape(TOP_K, piece, D_QUARTER)
                outs = _combine(xin, mods[l], wcol, g_a, g_b, *shared, *ln(l, 1), row0=row0 + first,
                                split_streams=split_streams, prev=outs)
                first += piece
        return (outs["ctx"], outs["lat"]) if split_streams else outs["all"]

    x = _conv_mixer(x_prompt.reshape(T_CTX, D_MODEL), x_sample.reshape(T_LAT, D_MODEL), mods[0],
                    conv_w_in[0].astype(BF16), conv_k[0], conv_w_out[0].astype(BF16), *ln(0, 0))
    x = moe_layer(x, 0, split_streams=False)

    p = _mla_params(mla_w_dqkv[0], mla_q_norm[0], mla_w_uq[0], mla_kv_norm[0], mla_w_ukv[0])
    rope_a, rope_b = _rope_tables()
    q, k, v, ckv, kr = _mla_proj(x, mods[1], p, rope_a, rope_b)
    kr_cache = jnp.concatenate([cache_krope[:, 0].reshape(DEC_BATCH * PAST_LEN, QK_ROPE),
                                jnp.zeros((DEC_BATCH * PAST_LEN, QK_ROPE), F32)], axis=-1)
    kc, vc = _cache_kv(cache_ckv[:, 0].reshape(DEC_BATCH * PAST_LEN, KV_LORA), kr_cache, p)
    o_ctx = _ctx_attention(q, k, v)
    o_lat = _lat_attention(q, k, v, kc, vc)
    x = _oproj(o_ctx, o_lat, x, mods[1], mla_w_o[0].astype(BF16), *ln(1, 0))
    y_ctx, y_lat = moe_layer(x, 1, split_streams=True)

    y_prompt = y_ctx.reshape(BATCH, SEQ, D_MODEL)
    y_sample = y_lat.reshape(DEC_BATCH, DEC_SEQ, D_MODEL)
    state_ckv = ckv[:T_CTX].reshape(BATCH, 1, SEQ, KV_LORA)
    state_krope = kr[:T_CTX, :QK_ROPE].reshape(BATCH, 1, SEQ, QK_ROPE)
    return (y_prompt, y_sample, state_ckv, state_krope)
```

```python
import functools
import math

import jax
import jax.numpy as jnp
import numpy as np
from jax import lax
from jax.experimental import pallas as pl
from jax.experimental.pallas import tpu as pltpu
from jax.experimental.pallas import tpu_sc as plsc

D_MODEL = 1024
BATCH = 16
SEQ = 256
DEPTH = 2
DEC_BATCH = 4
DEC_SEQ = 4096
PAST_LEN = 512
GRID_W = 64

N_HEADS = 8
QK_NOPE = 128
QK_ROPE = 64
V_DIM = 128
Q_LORA = 384
KV_LORA = 256
ROPE_THETA = 10000.0
ATTN_SCALE = (QK_NOPE + QK_ROPE) ** -0.5
HEAD_PAD = 256
Q_PRESCALE = ATTN_SCALE * math.log2(math.e)

N_EXPERTS = 64
TOP_K = 8
N_GROUPS = 8
TOPK_GROUPS = 4
GROUP_SIZE = N_EXPERTS // N_GROUPS
D_EXPERT = 256
D_SHARED = 256
ROUTED_SCALE = 2.5

ALPHA = (2 * DEPTH) ** 0.25
LN_EPS = 1e-5
RMS_EPS = 1e-6

GROUP_ROWS = 4096
N_GROUPS_ROWS = 1 + DEC_BATCH
T_CTX = BATCH * SEQ
T_LAT = DEC_BATCH * DEC_SEQ
T_ALL = T_CTX + T_LAT
LANES = 128
SUBLANES = 8

F32 = jnp.float32
BF16 = jnp.bfloat16
NEG_INF = float("-inf")


def _dot(a, b):
    return jnp.dot(a, b, preferred_element_type=F32)


def _dot_nt(a, b, precision=None):
    return lax.dot_general(a, b, (((1,), (1,)), ((), ())), precision=precision,
                           preferred_element_type=F32)


def _layer_norm(v, g, b):
    mu = jnp.mean(v, axis=-1, keepdims=True)
    d = v - mu
    var = jnp.mean(d * d, axis=-1, keepdims=True)
    return d * lax.rsqrt(var + LN_EPS) * g + b


def _rms_norm(v, g):
    return v * lax.rsqrt(jnp.mean(v * v, axis=-1, keepdims=True) + RMS_EPS) * g


def _silu(v):
    return v / (1.0 + jnp.exp(-v))


def _sigmoid(v):
    return 1.0 / (1.0 + jnp.exp(-v))


ADALN_COLS = 2


def _adaln_kernel(c_ref, w_ref, b_ref, o_ref):
    c = c_ref[...]
    s = _silu(c)
    res = jnp.dot(s, w_ref[0], precision=lax.Precision.HIGHEST, preferred_element_type=F32) + b_ref[0]
    for j in range(ADALN_COLS):
        o_ref[0, j] = res[:, j * D_MODEL:(j + 1) * D_MODEL]


def _adaln(cvecs, ada_w, ada_b):
    out = pl.pallas_call(
        _adaln_kernel,
        grid=(DEPTH, 6 // ADALN_COLS),
        in_specs=[
            pl.BlockSpec((SUBLANES, D_MODEL), lambda l, j: (0, 0)),
            pl.BlockSpec((1, D_MODEL, ADALN_COLS * D_MODEL), lambda l, j: (l, 0, j)),
            pl.BlockSpec((1, 1, ADALN_COLS * D_MODEL), lambda l, j: (l, 0, j)),
        ],
        out_specs=pl.BlockSpec((1, ADALN_COLS, SUBLANES, D_MODEL), lambda l, j: (l, j, 0, 0)),
        out_shape=jax.ShapeDtypeStruct((DEPTH, 6, SUBLANES, D_MODEL), F32),
        compiler_params=pltpu.CompilerParams(dimension_semantics=("parallel", "parallel")),
        name="adaln",
    )(cvecs, ada_w, ada_b.reshape(DEPTH, 1, 6 * D_MODEL))
    return jnp.transpose(out[:, :, :N_GROUPS_ROWS, :], (0, 2, 1, 3))


CONV_TM = 1024
CONV_PIECES = 2


def _conv_kernel(xc_ref, xcp_ref, xcn_ref, xl_ref, xlp_ref, xln_ref, mods_ref, win_ref, ck_ref, wout_ref,
                 g_ref, b_ref, o_ref):
    i = pl.program_id(0)
    tm = xc_ref.shape[0]
    is_ctx = i < T_CTX // tm
    m = mods_ref[0]
    sm, cm, gm = m[0:1], m[1:2], m[2:3]
    x = jnp.where(is_ctx, xc_ref[...], xl_ref[...])
    xp = jnp.where(is_ctx, xcp_ref[...], xlp_ref[...])
    xn = jnp.where(is_ctx, xcn_ref[...], xln_ref[...])
    th = tm // CONV_PIECES
    halves = [x[j * th:(j + 1) * th] for j in range(CONV_PIECES)]
    lhs = list(halves)
    lhs[0] = jnp.concatenate([xp, lhs[0]], axis=0)
    lhs[-1] = jnp.concatenate([lhs[-1], xn], axis=0)
    zs = [_dot((xh * (1.0 + cm) + sm).astype(BF16), win_ref[...]) for xh in lhs]
    us = [z[:, D_MODEL:2 * D_MODEL] * z[:, 2 * D_MODEL:] for z in zs]
    u_before = us[0][SUBLANES - 1:SUBLANES]
    u_after = us[-1][us[-1].shape[0] - SUBLANES:us[-1].shape[0] - SUBLANES + 1]
    zs[0], us[0] = zs[0][SUBLANES:], us[0][SUBLANES:]
    zs[-1], us[-1] = zs[-1][:th], us[-1][:th]
    befores = [u_before] + [u[th - 1:th] for u in us[:-1]]
    afters = [u[0:1] for u in us[1:]] + [u_after]

    ck = ck_ref[...]
    row = lax.broadcasted_iota(jnp.int32, (th, 1), 0)
    for hf in range(CONV_PIECES):
        grow = i * tm + hf * th + row
        seq_len = jnp.where(grow < T_CTX, SEQ, DEC_SEQ)
        pos = jnp.bitwise_and(grow, seq_len - 1)
        u = us[hf]
        left = jnp.where(row == 0, befores[hf], pltpu.roll(u, 1, 0))
        left = jnp.where(pos == 0, 0.0, left)
        right = jnp.where(row == th - 1, afters[hf], pltpu.roll(u, th - 1, 0))
        right = jnp.where(pos == seq_len - 1, 0.0, right)
        conv = left * ck[0:1] + u * ck[1:2] + right * ck[2:3]
        v = (zs[hf][:, :D_MODEL] * conv).astype(BF16)
        y = _dot(v, wout_ref[...])
        o_ref[hf * th:(hf + 1) * th, :] = _layer_norm(ALPHA * halves[hf] + gm * y, g_ref[...], b_ref[...])


def _conv_mixer(x_ctx, x_lat, mods, w_in, conv_k, w_out, ln_g, ln_b):
    tm = CONV_TM
    per8 = tm // SUBLANES
    n_ctx = T_CTX // tm
    n_lat = T_LAT // tm

    def stream_specs(first, n):
        def blk(i):
            return jnp.clip(i - first, 0, n - 1)
        return [
            pl.BlockSpec((tm, D_MODEL), lambda i: (blk(i), 0)),
            pl.BlockSpec((SUBLANES, D_MODEL), lambda i: (jnp.maximum(blk(i) * per8 - 1, 0), 0)),
            pl.BlockSpec((SUBLANES, D_MODEL), lambda i: (jnp.minimum((blk(i) + 1) * per8, n * per8 - 1), 0)),
        ]

    return pl.pallas_call(
        _conv_kernel,
        grid=(n_ctx + n_lat,),
        in_specs=stream_specs(0, n_ctx) + stream_specs(n_ctx, n_lat) + [
            pl.BlockSpec((1, 6, D_MODEL), lambda i: (i * tm // GROUP_ROWS, 0, 0)),
            pl.BlockSpec((D_MODEL, 3 * D_MODEL), lambda i: (0, 0)),
            pl.BlockSpec((3, D_MODEL), lambda i: (0, 0)),
            pl.BlockSpec((D_MODEL, D_MODEL), lambda i: (0, 0)),
            pl.BlockSpec((1, D_MODEL), lambda i: (0, 0)),
            pl.BlockSpec((1, D_MODEL), lambda i: (0, 0)),
        ],
        out_specs=pl.BlockSpec((tm, D_MODEL), lambda i: (i, 0)),
        out_shape=jax.ShapeDtypeStruct((T_ALL, D_MODEL), F32),
        compiler_params=pltpu.CompilerParams(dimension_semantics=("parallel",)),
        name="conv_mixer",
    )(x_ctx, x_ctx, x_ctx, x_lat, x_lat, x_lat, mods, w_in, conv_k, w_out, ln_g, ln_b)


ROUTER_TM = 512
MOE_CHUNKS = ((0, T_ALL),)
COMBINE_EIGHTHS = (2, 2, 2, 2)


def _first_argmax_mask(cur, ridx, n):
    mx = jnp.max(cur, axis=0, keepdims=True)
    first = jnp.min(jnp.where(cur == mx, ridx, n), axis=0, keepdims=True)
    return ridx == first, mx


def _router_kernel(x_ref, mods_ref, rwt_ref, bias_ref, before_ref, idx_ref, pos_ref, wcol_ref, count_ref,
                   hpa_ref, hpb_ref, carry_ref):
    tm = x_ref.shape[0]
    m = mods_ref[0]
    sf, cf = m[3:4], m[4:5]
    hc = x_ref[...] * (1.0 + cf) + sf
    words = _pack_rows(hc)
    hpa_ref[...] = words[:, :D_QUARTER]
    hpb_ref[...] = words[:, D_QUARTER:]
    logits = _dot_nt(rwt_ref[...], hc, precision=lax.Precision.HIGHEST)
    scores = _sigmoid(logits)
    biased = scores + bias_ref[...]

    ridx8 = lax.broadcasted_iota(jnp.int32, (GROUP_SIZE, tm), 0)
    gscore = jnp.full((N_GROUPS, tm), NEG_INF, F32)
    for g in range(N_GROUPS):
        blk = biased[g * GROUP_SIZE:(g + 1) * GROUP_SIZE]
        sel, m1 = _first_argmax_mask(blk, ridx8, GROUP_SIZE)
        m2 = jnp.max(jnp.where(sel, NEG_INF, blk), axis=0, keepdims=True)
        gscore = jnp.where(ridx8 == g, m1 + m2, gscore)

    gmask = jnp.zeros((N_GROUPS, tm), jnp.bool_)
    cur = gscore
    for _ in range(TOPK_GROUPS):
        sel, _unused = _first_argmax_mask(cur, ridx8, N_GROUPS)
        gmask = jnp.logical_or(gmask, sel)
        cur = jnp.where(sel, NEG_INF, cur)

    gmask_f = gmask.astype(F32)
    blocks = []
    for g in range(N_GROUPS):
        keep = jnp.broadcast_to(gmask_f[g:g + 1], (GROUP_SIZE, tm)) > 0.5
        blocks.append(jnp.where(keep, biased[g * GROUP_SIZE:(g + 1) * GROUP_SIZE], NEG_INF))
    cur = jnp.concatenate(blocks, axis=0)

    first_tiles = [row0 // tm for row0, _rows in MOE_CHUNKS]
    starts_chunk = functools.reduce(jnp.logical_or, [pl.program_id(0) == ft for ft in first_tiles])

    @pl.when(starts_chunk)
    def _():
        carry_ref[...] = jnp.zeros(carry_ref.shape, F32)

    ridx = lax.broadcasted_iota(jnp.int32, (N_EXPERTS, tm), 0)
    kidx = lax.broadcasted_iota(jnp.int32, (TOP_K, tm), 0)
    sels = []
    chosen = jnp.zeros((N_EXPERTS, tm), jnp.bool_)
    idx_rows = jnp.zeros((TOP_K, tm), jnp.int32)
    for k in range(TOP_K):
        mx = jnp.max(cur, axis=0, keepdims=True)
        first = jnp.min(jnp.where(cur == mx, ridx, N_EXPERTS), axis=0, keepdims=True)
        sel = ridx == first
        sels.append(sel)
        chosen = jnp.logical_or(chosen, sel)
        idx_rows = jnp.where(kidx == k, first, idx_rows)
        cur = jnp.where(sel, NEG_INF, cur)

    onehot = chosen.astype(F32)
    rank = carry_ref[...] + _dot(onehot.astype(BF16), before_ref[...])
    carry_ref[...] = carry_ref[...] + jnp.sum(onehot, axis=1, keepdims=True)
    count_ref[0] = jnp.broadcast_to(carry_ref[...], count_ref.shape[1:])

    w = jnp.where(chosen, scores, 0.0)
    w = w / jnp.sum(w, axis=0, keepdims=True) * ROUTED_SCALE
    pos_rows = jnp.zeros((TOP_K, tm), F32)
    w_rows = jnp.zeros((TOP_K, tm), F32)
    for k in range(TOP_K):
        pos_rows = jnp.where(kidx == k, jnp.sum(jnp.where(sels[k], rank, 0.0), axis=0, keepdims=True), pos_rows)
        w_rows = jnp.where(kidx == k, jnp.sum(jnp.where(sels[k], w, 0.0), axis=0, keepdims=True), w_rows)
    idx_ref[...] = idx_rows
    pos_ref[...] = pos_rows.astype(jnp.int32)
    wpad = jnp.concatenate([w_rows, jnp.zeros((LANES - TOP_K, tm), F32)], axis=0)
    wcol_ref[...] = wpad.T


def _router(x, mods, router_wt, router_bias):
    t = x.shape[0]
    tm = ROUTER_TM
    before = jnp.asarray(np.triu(np.ones((tm, tm), np.float32), 1), dtype=BF16)

    def chunk_of(i):
        return sum((i >= row0 // tm).astype(jnp.int32) for row0, _rows in MOE_CHUNKS[1:])

    return pl.pallas_call(
        _router_kernel,
        grid=(t // tm,),
        in_specs=[
            pl.BlockSpec((tm, D_MODEL), lambda i: (i, 0)),
            pl.BlockSpec((1, 6, D_MODEL), lambda i: (i * tm // GROUP_ROWS, 0, 0)),
            pl.BlockSpec((N_EXPERTS, D_MODEL), lambda i: (0, 0)),
            pl.BlockSpec((N_EXPERTS, 1), lambda i: (0, 0)),
            pl.BlockSpec((tm, tm), lambda i: (0, 0)),
        ],
        out_specs=[
            pl.BlockSpec((TOP_K, tm), lambda i: (0, i)),
            pl.BlockSpec((TOP_K, tm), lambda i: (0, i)),
            pl.BlockSpec((tm, LANES), lambda i: (i, 0)),
            pl.BlockSpec((1, N_EXPERTS, LANES), lambda i: (chunk_of(i), 0, 0)),
            pl.BlockSpec((tm, D_QUARTER), lambda i: (i, 0)),
            pl.BlockSpec((tm, D_QUARTER), lambda i: (i, 0)),
        ],
        out_shape=[
            jax.ShapeDtypeStruct((TOP_K, t), jnp.int32),
            jax.ShapeDtypeStruct((TOP_K, t), jnp.int32),
            jax.ShapeDtypeStruct((t, LANES), F32),
            jax.ShapeDtypeStruct((len(MOE_CHUNKS), N_EXPERTS, LANES), F32),
            jax.ShapeDtypeStruct((t, D_QUARTER), jnp.uint32),
            jax.ShapeDtypeStruct((t, D_QUARTER), jnp.uint32),
        ],
        scratch_shapes=[pltpu.VMEM((N_EXPERTS, 1), F32)],
        compiler_params=pltpu.CompilerParams(dimension_semantics=("arbitrary",)),
        name="moe_router",
    )(x, mods, router_wt, router_bias, before)


FFN_TM = 512


def _n_tiles(rows):
    return rows * TOP_K // FFN_TM + N_EXPERTS


def _expert_layout(counts, n_tiles):
    padded = (counts + FFN_TM - 1) // FFN_TM * FFN_TM
    end = jnp.cumsum(padded)
    start = end - padded
    tile_row = jnp.arange(n_tiles, dtype=jnp.int32) * FFN_TM
    tile_expert = jnp.minimum(jnp.sum(end[None, :] <= tile_row[:, None], axis=1), N_EXPERTS - 1)
    of_tile = tile_expert[:, None] == jnp.arange(N_EXPERTS, dtype=jnp.int32)[None, :]
    live_end = jnp.sum(jnp.where(of_tile, (start + counts)[None, :], 0), axis=1)
    tile_valid = jnp.clip(live_end - tile_row, 0, FFN_TM)
    n_used = (end[-1] // FFN_TM).astype(jnp.int32).reshape(1)
    used = jnp.arange(n_tiles, dtype=jnp.int32) < n_used[0]
    prev_expert = jnp.concatenate([jnp.full((1,), -1, tile_expert.dtype), tile_expert[:-1]])
    run_flag = jnp.logical_and(used, tile_expert != prev_expert)
    run_slot = (jnp.cumsum(run_flag.astype(jnp.int32)) - 1) % 2
    later = jnp.logical_and(used[None, :], tile_expert[None, :] > tile_expert[:, None])
    next_expert = jnp.min(jnp.where(later, tile_expert[None, :], N_EXPERTS), axis=1)
    next_expert = jnp.where(next_expert == N_EXPERTS, -1, next_expert)
    plan = (run_flag.astype(jnp.int32), run_slot.astype(jnp.int32), next_expert.astype(jnp.int32))
    return start.astype(jnp.int32), tile_expert.astype(jnp.int32), tile_valid.astype(jnp.int32), n_used, plan


D_HALF = D_MODEL // 2
D_QUARTER = D_MODEL // 4


def _pack_rows(v):
    hi = lax.bitcast_convert_type(v[:, :D_HALF].astype(BF16).astype(F32), jnp.uint32)
    lo = lax.bitcast_convert_type(v[:, D_HALF:].astype(BF16).astype(F32), jnp.uint32)
    return jnp.bitwise_or(hi, jnp.right_shift(lo, jnp.uint32(16)))


def _unpack_rows(w):
    hi = lax.bitcast_convert_type(jnp.bitwise_and(w, jnp.uint32(0xFFFF0000)), F32)
    lo = lax.bitcast_convert_type(jnp.left_shift(w, jnp.uint32(16)), F32)
    return hi, lo


DEST_TM = 2048


def _dest_kernel(start_ref, idx_ref, pos_ref, dest_ref):
    idx = idx_ref[...]
    base = jnp.zeros(idx.shape, jnp.int32)
    for e in range(N_EXPERTS):
        base = jnp.where(idx == e, start_ref[e], base)
    dest_ref[...] = base + pos_ref[...]


def _dest_rows(start, idx, pos):
    t = idx.shape[1]
    return pl.pallas_call(
        _dest_kernel,
        grid_spec=pltpu.PrefetchScalarGridSpec(
            num_scalar_prefetch=1,
            grid=(t // DEST_TM,),
            in_specs=[pl.BlockSpec((TOP_K, DEST_TM), lambda i, s: (0, i)),
                      pl.BlockSpec((TOP_K, DEST_TM), lambda i, s: (0, i))],
            out_specs=pl.BlockSpec((TOP_K, DEST_TM), lambda i, s: (0, i)),
        ),
        out_shape=jax.ShapeDtypeStruct((TOP_K, t), jnp.int32),
        compiler_params=pltpu.CompilerParams(dimension_semantics=("parallel",)),
        name="moe_dest",
    )(start, idx, pos)


SC_WINDOW = 128


def _sc_mesh():
    return plsc.VectorSubcoreMesh(core_axis_name="c", subcore_axis_name="s")


def _sc_scatter_rows(x_a, x_b, dest, row0, n_sorted):
    t = dest.shape[1]
    blk0 = row0 // SC_WINDOW
    out = jax.ShapeDtypeStruct((n_sorted, D_QUARTER), x_a.dtype)

    @functools.partial(pl.kernel, out_type=(out, out), mesh=_sc_mesh(), scratch_types=[])
    def scatter(xa_hbm, xb_hbm, i_hbm, oa_hbm, ob_hbm):
        for x_hbm, o_hbm in ((xa_hbm, oa_hbm), (xb_hbm, ob_hbm)):
            def body(x_vmem, i_vmem, o_hbm=o_hbm):
                pltpu.sync_copy(x_vmem, o_hbm.at[i_vmem.at[0]])

            pltpu.emit_pipeline(
                body,
                grid=(t // SC_WINDOW, TOP_K),
                in_specs=[pl.BlockSpec((SC_WINDOW, D_QUARTER), lambda i, k: (i + blk0, 0)),
                          pl.BlockSpec((1, SC_WINDOW), lambda i, k: (k, i))],
                out_specs=[],
                core_axis_name=("c", "s"),
                dimension_semantics=(pltpu.PARALLEL, pltpu.ARBITRARY),
            )(x_hbm, i_hbm)

    return scatter(x_a, x_b, dest)


def _sc_gather_rows(table_a, table_b, idx):
    m = idx.shape[1]
    out = jax.ShapeDtypeStruct((m, D_QUARTER), table_a.dtype)

    @functools.partial(pl.kernel, out_type=(out, out), mesh=_sc_mesh(), scratch_types=[])
    def gather(ta_hbm, tb_hbm, i_hbm, oa_hbm, ob_hbm):
        for t_hbm, o_hbm in ((ta_hbm, oa_hbm), (tb_hbm, ob_hbm)):
            def body(i_vmem, o_vmem, t_hbm=t_hbm):
                pltpu.sync_copy(t_hbm.at[i_vmem.at[0]], o_vmem)

            pltpu.emit_pipeline(
                body,
                grid=(m // SC_WINDOW,),
                in_specs=[pl.BlockSpec((1, SC_WINDOW), lambda i: (0, i))],
                out_specs=[pl.BlockSpec((SC_WINDOW, D_QUARTER), lambda i: (i, 0))],
                core_axis_name=("c", "s"),
                dimension_semantics=(pltpu.PARALLEL,),
            )(i_hbm, o_hbm)

    return gather(table_a, table_b, idx)


FFN_STEP_TILES = 2


def _ffn_kernel(layer, te_ref, tv_ref, nu_ref, flag_ref, slot_ref, next_ref, xa_ref, xb_ref,
                wg_hbm, wu_hbm, wd_hbm, ya_ref, yb_ref, wg_st, wu_st, wd_st, wgb_ref, wub_ref, wdb_ref, sem):
    step = pl.program_id(0)

    def fetch(expert, slot):
        return [pltpu.make_async_copy(src.at[layer, expert], dst.at[slot], sem.at[slot])
                for src, dst in ((wg_hbm, wg_st), (wu_hbm, wu_st), (wd_hbm, wd_st))]

    for s in range(FFN_STEP_TILES):
        tile = step * FFN_STEP_TILES + s

        @pl.when(jnp.logical_and(tile < nu_ref[0], flag_ref[tile] == 1))
        def _():
            slot = slot_ref[tile]

            @pl.when(tile == 0)
            def _():
                for cp in fetch(te_ref[tile], slot):
                    cp.start()

            for cp in fetch(te_ref[tile], slot):
                cp.wait()
            wgb_ref[slot] = wg_st[slot].astype(BF16)
            wub_ref[slot] = wu_st[slot].astype(BF16)
            wdb_ref[slot] = wd_st[slot].astype(BF16)

            @pl.when(next_ref[tile] >= 0)
            def _():
                for cp in fetch(next_ref[tile], 1 - slot):
                    cp.start()

    @pl.when(step * FFN_STEP_TILES < nu_ref[0])
    def _():
        for s in range(FFN_STEP_TILES):
            tile = step * FFN_STEP_TILES + s
            slot = slot_ref[tile]
            wg = wgb_ref[slot]
            wu = wub_ref[slot]
            wd = wdb_ref[slot]
            rows = pl.ds(s * FFN_TM, FFN_TM)
            live = lax.broadcasted_iota(jnp.int32, (FFN_TM, D_QUARTER), 0) < tv_ref[tile]
            hi_a, lo_a = _unpack_rows(jnp.where(live, xa_ref[rows, :], jnp.uint32(0)))
            hi_b, lo_b = _unpack_rows(jnp.where(live, xb_ref[rows, :], jnp.uint32(0)))
            xb = jnp.concatenate([hi_a.astype(BF16), hi_b.astype(BF16), lo_a.astype(BF16), lo_b.astype(BF16)],
                                 axis=1)
            a = (_silu(_dot(xb, wg)) * _dot(xb, wu)).astype(BF16)
            words = _pack_rows(_dot(a, wd))
            ya_ref[rows, :] = words[:, :D_QUARTER]
            yb_ref[rows, :] = words[:, D_QUARTER:]

    @pl.when(step * FFN_STEP_TILES >= nu_ref[0])
    def _():
        ya_ref[...] = jnp.zeros(ya_ref.shape, jnp.uint32)
        yb_ref[...] = jnp.zeros(yb_ref.shape, jnp.uint32)


def _expert_ffn(xs_a, xs_b, tile_expert, tile_valid, n_used, plan, wg, wu, wd, layer):
    n_tiles = xs_a.shape[0] // FFN_TM
    step_rows = FFN_STEP_TILES * FFN_TM

    def row_map(i, te, tv, nu, fl, sl, nx):
        return (jnp.minimum(i, (nu[0] - 1) // FFN_STEP_TILES), 0)

    def out_map(i, te, tv, nu, fl, sl, nx):
        return (i, 0)

    hbm = pl.BlockSpec(memory_space=pl.ANY)
    return pl.pallas_call(
        functools.partial(_ffn_kernel, layer),
        grid_spec=pltpu.PrefetchScalarGridSpec(
            num_scalar_prefetch=6,
            grid=(n_tiles // FFN_STEP_TILES,),
            in_specs=[pl.BlockSpec((step_rows, D_QUARTER), row_map),
                      pl.BlockSpec((step_rows, D_QUARTER), row_map), hbm, hbm, hbm],
            out_specs=[pl.BlockSpec((step_rows, D_QUARTER), out_map), pl.BlockSpec((step_rows, D_QUARTER), out_map)],
            scratch_shapes=[pltpu.VMEM((2, D_MODEL, D_EXPERT), F32), pltpu.VMEM((2, D_MODEL, D_EXPERT), F32),
                            pltpu.VMEM((2, D_EXPERT, D_MODEL), F32),
                            pltpu.VMEM((2, D_MODEL, D_EXPERT), BF16), pltpu.VMEM((2, D_MODEL, D_EXPERT), BF16),
                            pltpu.VMEM((2, D_EXPERT, D_MODEL), BF16), pltpu.SemaphoreType.DMA((2,))],
        ),
        out_shape=[jax.ShapeDtypeStruct((n_tiles * FFN_TM, D_QUARTER), jnp.uint32),
                   jax.ShapeDtypeStruct((n_tiles * FFN_TM, D_QUARTER), jnp.uint32)],
        compiler_params=pltpu.CompilerParams(dimension_semantics=("arbitrary",)),
        name="moe_expert_ffn",
    )(tile_expert, tile_valid, n_used, *plan, xs_a, xs_b, wg, wu, wd)


COMBINE_TM = 512


def _combine_kernel(n_prev, tile0, x_ref, mods_ref, wcol_ref, ga_ref, gb_ref, sg_ref, su_ref, sd_ref, g_ref, b_ref,
                    *rest):
    o_refs = rest[n_prev:]
    m = mods_ref[0]
    sf, cf, gf = m[3:4], m[4:5], m[5:6]
    x = x_ref[...]
    hc = (x * (1.0 + cf) + sf).astype(BF16)
    a = _silu(_dot(hc, sg_ref[...])) * _dot(hc, su_ref[...])
    y = _dot(a.astype(BF16), sd_ref[...])
    wcol = wcol_ref[...]
    parts = [y[:, q * D_QUARTER:(q + 1) * D_QUARTER] for q in range(4)]
    for k in range(TOP_K):
        hi_a, lo_a = _unpack_rows(ga_ref[k])
        hi_b, lo_b = _unpack_rows(gb_ref[k])
        wk = wcol[:, k:k + 1]
        parts = [parts[0] + wk * hi_a, parts[1] + wk * hi_b, parts[2] + wk * lo_a, parts[3] + wk * lo_b]
    y = jnp.concatenate(parts, axis=1)
    out = _layer_norm(ALPHA * x + gf * y, g_ref[...], b_ref[...])
    if len(o_refs) == 1:
        o_refs[0][...] = out
    else:
        is_ctx = pl.program_id(0) + tile0 < T_CTX // x_ref.shape[0]

        @pl.when(is_ctx)
        def _():
            o_refs[0][...] = out

        @pl.when(jnp.logical_not(is_ctx))
        def _():
            o_refs[1][...] = out


def _combine(x, mods, wcol, g_a, g_b, sg, su, sd, ln_g, ln_b, row0, split_streams, prev):
    tm = COMBINE_TM
    n = g_a.shape[1] // tm
    tile0 = row0 // tm
    n_ctx = T_CTX // tm
    full = lambda shape: pl.BlockSpec(shape, lambda i: (0,) * len(shape))
    layouts = {
        "all": (lambda i: (tile0 + i, 0), T_ALL),
        "ctx": (lambda i: (jnp.minimum(tile0 + i, n_ctx - 1), 0), T_CTX),
        "lat": (lambda i: (jnp.maximum(tile0 + i - n_ctx, 0), 0), T_LAT),
    }
    if split_streams:
        kinds = (["ctx"] if tile0 < n_ctx else []) + (["lat"] if tile0 + n > n_ctx else [])
    else:
        kinds = ["all"]
    carried = [kd for kd in kinds if kd in prev]
    outs = pl.pallas_call(
        functools.partial(_combine_kernel, len(carried), tile0),
        grid=(n,),
        in_specs=[
            pl.BlockSpec((tm, D_MODEL), lambda i: (tile0 + i, 0)),
            pl.BlockSpec((1, 6, D_MODEL), lambda i: ((tile0 + i) * tm // GROUP_ROWS, 0, 0)),
            pl.BlockSpec((tm, LANES), lambda i: (tile0 + i, 0)),
            pl.BlockSpec((TOP_K, tm, D_QUARTER), lambda i: (0, i, 0)),
            pl.BlockSpec((TOP_K, tm, D_QUARTER), lambda i: (0, i, 0)),
            full((D_MODEL, D_SHARED)), full((D_MODEL, D_SHARED)), full((D_SHARED, D_MODEL)),
            full((1, D_MODEL)), full((1, D_MODEL)),
        ] + [pl.BlockSpec(memory_space=pl.ANY) for _ in carried],
        out_specs=[pl.BlockSpec((tm, D_MODEL), layouts[kd][0]) for kd in kinds],
        out_shape=[jax.ShapeDtypeStruct((layouts[kd][1], D_MODEL), F32) for kd in kinds],
        input_output_aliases={10 + j: kinds.index(kd) for j, kd in enumerate(carried)},
        compiler_params=pltpu.CompilerParams(dimension_semantics=("arbitrary",)),
        name="moe_combine",
    )(x, mods, wcol, g_a, g_b, sg, su, sd, ln_g, ln_b, *[prev[kd] for kd in carried])
    return {**prev, **dict(zip(kinds, outs))}


MLA_TM = 1024


def _mla_proj_kernel(x_ref, mods_ref, wdq_ref, wdkv_ref, wkr_ref, qn_ref, kvn_ref, wuq_ref,
                     wukn_ref, wuv_ref, ta_ref, tb_ref, q_ref, k_ref, v_ref, ckv_ref, kr_ref):
    m = mods_ref[0]
    sm, cm = m[0:1], m[1:2]
    h = (x_ref[...] * (1.0 + cm) + sm).astype(BF16)
    cq = _rms_norm(_dot(h, wdq_ref[...]), qn_ref[...])
    ckv = _rms_norm(_dot(h, wdkv_ref[...]), kvn_ref[...])
    kr2 = _dot(h, wkr_ref[...])
    ckv_ref[...] = ckv
    kr_ref[...] = kr2

    ka = ta_ref[0]
    kb = tb_ref[0]
    tm = ka.shape[0]
    ta = jnp.concatenate([jnp.full((tm, QK_NOPE), Q_PRESCALE, F32), ka * Q_PRESCALE], axis=1)
    tb = jnp.concatenate([jnp.zeros((tm, QK_NOPE), F32), kb * Q_PRESCALE], axis=1)
    krr = kr2 * ka + pltpu.roll(kr2, QK_ROPE, 1) * kb

    qpre = _dot(cq.astype(BF16), wuq_ref[...])
    ckv_b = ckv.astype(BF16)
    kn = _dot(ckv_b, wukn_ref[...])
    v_ref[...] = _dot(ckv_b, wuv_ref[...]).astype(BF16)
    for hd in range(N_HEADS):
        qh = qpre[:, hd * HEAD_PAD:(hd + 1) * HEAD_PAD]
        qrot = qh * ta + pltpu.roll(qh, HEAD_PAD - QK_ROPE, 1) * tb
        q_ref[:, hd * HEAD_PAD:(hd + 1) * HEAD_PAD] = qrot.astype(BF16)
        k_ref[:, hd * HEAD_PAD:hd * HEAD_PAD + QK_NOPE] = kn[:, hd * QK_NOPE:(hd + 1) * QK_NOPE].astype(BF16)
        k_ref[:, hd * HEAD_PAD + QK_NOPE:(hd + 1) * HEAD_PAD] = krr.astype(BF16)


def _mla_proj(x, mods, p, rope_a, rope_b):
    t = x.shape[0]
    tm = MLA_TM
    full = lambda shape: pl.BlockSpec(shape, lambda i: (0,) * len(shape))
    rope_spec = pl.BlockSpec(
        (1, tm, 2 * QK_ROPE),
        lambda i: (jnp.minimum(i * tm // GROUP_ROWS, 1), (i * tm % GROUP_ROWS) // tm, 0))
    return pl.pallas_call(
        _mla_proj_kernel,
        grid=(t // tm,),
        in_specs=[
            pl.BlockSpec((tm, D_MODEL), lambda i: (i, 0)),
            pl.BlockSpec((1, 6, D_MODEL), lambda i: (i * tm // GROUP_ROWS, 0, 0)),
            full((D_MODEL, Q_LORA)), full((D_MODEL, KV_LORA)), full((D_MODEL, 2 * QK_ROPE)),
            full((1, Q_LORA)), full((1, KV_LORA)),
            full((Q_LORA, N_HEADS * HEAD_PAD)),
            full((KV_LORA, N_HEADS * QK_NOPE)), full((KV_LORA, N_HEADS * V_DIM)),
            rope_spec, rope_spec,
        ],
        out_specs=[
            pl.BlockSpec((tm, N_HEADS * HEAD_PAD), lambda i: (i, 0)),
            pl.BlockSpec((tm, N_HEADS * HEAD_PAD), lambda i: (i, 0)),
            pl.BlockSpec((tm, N_HEADS * V_DIM), lambda i: (i, 0)),
            pl.BlockSpec((tm, KV_LORA), lambda i: (i, 0)),
            pl.BlockSpec((tm, 2 * QK_ROPE), lambda i: (i, 0)),
        ],
        out_shape=[
            jax.ShapeDtypeStruct((t, N_HEADS * HEAD_PAD), BF16),
            jax.ShapeDtypeStruct((t, N_HEADS * HEAD_PAD), BF16),
            jax.ShapeDtypeStruct((t, N_HEADS * V_DIM), BF16),
            jax.ShapeDtypeStruct((t, KV_LORA), F32),
            jax.ShapeDtypeStruct((t, 2 * QK_ROPE), F32),
        ],
        compiler_params=pltpu.CompilerParams(dimension_semantics=("parallel",)),
        name="mla_proj",
    )(x, mods, p["w_dq"], p["w_dkv"], p["w_kr"], p["q_norm"], p["kv_norm"], p["w_uq"],
      p["w_ukn"], p["w_uv"], rope_a, rope_b)


def _cache_kv_kernel(ckv_ref, kr_ref, wukn_ref, wuv_ref, k_ref, v_ref):
    ckv_b = ckv_ref[...].astype(BF16)
    kn = _dot(ckv_b, wukn_ref[...])
    v_ref[...] = _dot(ckv_b, wuv_ref[...]).astype(BF16)
    kr = kr_ref[...].astype(BF16)
    for hd in range(N_HEADS):
        k_ref[:, hd * HEAD_PAD:hd * HEAD_PAD + QK_NOPE] = kn[:, hd * QK_NOPE:(hd + 1) * QK_NOPE].astype(BF16)
        k_ref[:, hd * HEAD_PAD + QK_NOPE:(hd + 1) * HEAD_PAD] = kr


def _cache_kv(ckv, kr_pad, p):
    t = ckv.shape[0]
    tm = PAST_LEN
    full = lambda shape: pl.BlockSpec(shape, lambda i: (0,) * len(shape))
    return pl.pallas_call(
        _cache_kv_kernel,
        grid=(t // tm,),
        in_specs=[
            pl.BlockSpec((tm, KV_LORA), lambda i: (i, 0)),
            pl.BlockSpec((tm, 2 * QK_ROPE), lambda i: (i, 0)),
            full((KV_LORA, N_HEADS * QK_NOPE)), full((KV_LORA, N_HEADS * V_DIM)),
        ],
        out_specs=[
            pl.BlockSpec((tm, N_HEADS * HEAD_PAD), lambda i: (i, 0)),
            pl.BlockSpec((tm, N_HEADS * V_DIM), lambda i: (i, 0)),
        ],
        out_shape=[
            jax.ShapeDtypeStruct((t, N_HEADS * HEAD_PAD), BF16),
            jax.ShapeDtypeStruct((t, N_HEADS * V_DIM), BF16),
        ],
        compiler_params=pltpu.CompilerParams(dimension_semantics=("parallel",)),
        name="mla_cache_kv",
    )(ckv, kr_pad, p["w_ukn"], p["w_uv"])


def _ctx_attn_kernel(q_ref, k_ref, v_ref, o_ref):
    for hd in range(N_HEADS):
        q = q_ref[:, hd * HEAD_PAD:(hd + 1) * HEAD_PAD]
        k = k_ref[:, hd * HEAD_PAD:(hd + 1) * HEAD_PAD]
        s = _dot_nt(q, k)
        s = s - jnp.max(s, axis=-1, keepdims=True)
        p = jnp.exp2(s)
        p = p / jnp.sum(p, axis=-1, keepdims=True)
        o = _dot(p.astype(BF16), v_ref[:, hd * V_DIM:(hd + 1) * V_DIM])
        o_ref[:, hd * V_DIM:(hd + 1) * V_DIM] = o.astype(BF16)


def _ctx_attention(q, k, v):
    return pl.pallas_call(
        _ctx_attn_kernel,
        grid=(BATCH,),
        in_specs=[
            pl.BlockSpec((SEQ, N_HEADS * HEAD_PAD), lambda b: (b, 0)),
            pl.BlockSpec((SEQ, N_HEADS * HEAD_PAD), lambda b: (b, 0)),
            pl.BlockSpec((SEQ, N_HEADS * V_DIM), lambda b: (b, 0)),
        ],
        out_specs=pl.BlockSpec((SEQ, N_HEADS * V_DIM), lambda b: (b, 0)),
        out_shape=jax.ShapeDtypeStruct((T_CTX, N_HEADS * V_DIM), BF16),
        compiler_params=pltpu.CompilerParams(dimension_semantics=("parallel",)),
        name="ctx_attention",
    )(q, k, v)


LAT_TQ = 1024
LAT_TK = 1024
LAT_PIECES = 4


def _lat_attn_kernel(q_ref, k_ref, v_ref, kc_ref, vc_ref, o_ref, s_ref, p_ref, m_ref):
    spans = [(c0, LAT_TK) for c0 in range(0, DEC_SEQ, LAT_TK)] + [(DEC_SEQ, PAST_LEN)]
    n_chunks = len(spans)
    tp = q_ref.shape[0] // LAT_PIECES
    pieces = [pl.ds(j * tp, tp) for j in range(LAT_PIECES)]
    groups = [pieces[:2], pieces[2:]]
    state = {}

    def keys(c):
        c0, width = spans[c]
        return kc_ref[...] if c == n_chunks - 1 else k_ref[c0:c0 + width, :]

    def values(c):
        c0, width = spans[c]
        return vc_ref[...] if c == n_chunks - 1 else v_ref[c0:c0 + width, :]

    def qk(r, c):
        c0, width = spans[c]
        s = _dot_nt(q_ref[r, :], keys(c))
        s_ref[r, c0:c0 + width] = s
        mp = state.get(("m", r.start), jnp.full((tp, LANES), NEG_INF, F32))
        for j in range(width // LANES):
            mp = jnp.maximum(mp, s[:, j * LANES:(j + 1) * LANES])
        state[("m", r.start)] = mp

    def row_max(r):
        m_ref[r, :] = jnp.broadcast_to(jnp.max(state[("m", r.start)], axis=-1, keepdims=True), (tp, LANES))

    def exp_chunk(r, c, after=None):
        m = m_ref[r, :]
        if after is not None:
            bits = lax.bitcast_convert_type(after[:, :LANES], jnp.uint32)
            zero = lax.shift_right_logical(lax.shift_right_logical(bits, jnp.uint32(16)), jnp.uint32(16))
            m = m + lax.bitcast_convert_type(zero, F32)
        c0, width = spans[c]
        for j in range(c0 // LANES, (c0 + width) // LANES):
            p = jnp.exp2(s_ref[r, j * LANES:(j + 1) * LANES] - m)
            p_ref[r, j * LANES:(j + 1) * LANES] = p.astype(BF16)

    def pv(r, c):
        c0, width = spans[c]
        v = values(c)
        v_and_ones = jnp.concatenate([v, jnp.ones_like(v)], axis=1)
        acc = state.get(("a", r.start), jnp.zeros((tp, 2 * V_DIM), F32))
        state[("a", r.start)] = acc + _dot(p_ref[r, c0:c0 + width], v_and_ones)

    def finish(r):
        acc = state[("a", r.start)]
        o_ref[r, :] = (acc[:, :V_DIM] / acc[:, V_DIM:]).astype(BF16)

    for c in range(n_chunks):
        for r in groups[0]:
            qk(r, c)
    for r in groups[0]:
        row_max(r)
    for c in range(n_chunks):
        for r in groups[1]:
            qk(r, c)
        for r in groups[0]:
            exp_chunk(r, c)
    for r in groups[1]:
        row_max(r)
    for c in range(n_chunks):
        for r in groups[0]:
            pv(r, c)
        for r0, r in zip(groups[0], groups[1]):
            exp_chunk(r, c, after=state[("a", r0.start)])
    for r in groups[0]:
        finish(r)
    for c in range(n_chunks):
        for r in groups[1]:
            pv(r, c)
    for r in groups[1]:
        finish(r)


def _lat_attention(q, k, v, kc, vc):
    nq = DEC_SEQ // LAT_TQ
    return pl.pallas_call(
        _lat_attn_kernel,
        grid=(DEC_BATCH, N_HEADS, nq),
        in_specs=[
            pl.BlockSpec((LAT_TQ, HEAD_PAD), lambda b, h, i: ((b + 1) * nq + i, h)),
            pl.BlockSpec((DEC_SEQ, HEAD_PAD), lambda b, h, i: (b + 1, h)),
            pl.BlockSpec((DEC_SEQ, V_DIM), lambda b, h, i: (b + 1, h)),
            pl.BlockSpec((PAST_LEN, HEAD_PAD), lambda b, h, i: (b, h)),
            pl.BlockSpec((PAST_LEN, V_DIM), lambda b, h, i: (b, h)),
        ],
        out_specs=pl.BlockSpec((LAT_TQ, V_DIM), lambda b, h, i: (b * nq + i, h)),
        out_shape=jax.ShapeDtypeStruct((T_LAT, N_HEADS * V_DIM), BF16),
        scratch_shapes=[pltpu.VMEM((LAT_TQ, DEC_SEQ + PAST_LEN), F32),
                        pltpu.VMEM((LAT_TQ, DEC_SEQ + PAST_LEN), BF16),
                        pltpu.VMEM((LAT_TQ, LANES), F32)],
        compiler_params=pltpu.CompilerParams(
            dimension_semantics=("parallel", "parallel", "parallel")),
        name="lat_attention",
    )(q, k, v, kc, vc)


OPROJ_TM = 1024


def _oproj_kernel(oc_ref, ol_ref, x_ref, mods_ref, wo_ref, g_ref, b_ref, out_ref):
    is_ctx = pl.program_id(0) < T_CTX // OPROJ_TM
    o = jnp.where(is_ctx, oc_ref[...], ol_ref[...])
    gm = mods_ref[0][2:3]
    y = _dot(o, wo_ref[...])
    out_ref[...] = _layer_norm(ALPHA * x_ref[...] + gm * y, g_ref[...], b_ref[...])


def _oproj(o_ctx, o_lat, x, mods, w_o, ln_g, ln_b):
    t = x.shape[0]
    tm = OPROJ_TM
    n_ctx = T_CTX // tm
    return pl.pallas_call(
        _oproj_kernel,
        grid=(t // tm,),
        in_specs=[
            pl.BlockSpec((tm, N_HEADS * V_DIM), lambda i: (jnp.minimum(i, n_ctx - 1), 0)),
            pl.BlockSpec((tm, N_HEADS * V_DIM), lambda i: (jnp.maximum(i - n_ctx, 0), 0)),
            pl.BlockSpec((tm, D_MODEL), lambda i: (i, 0)),
            pl.BlockSpec((1, 6, D_MODEL), lambda i: (i * tm // GROUP_ROWS, 0, 0)),
            pl.BlockSpec((N_HEADS * V_DIM, D_MODEL), lambda i: (0, 0)),
            pl.BlockSpec((1, D_MODEL), lambda i: (0, 0)),
            pl.BlockSpec((1, D_MODEL), lambda i: (0, 0)),
        ],
        out_specs=pl.BlockSpec((tm, D_MODEL), lambda i: (i, 0)),
        out_shape=jax.ShapeDtypeStruct((t, D_MODEL), F32),
        compiler_params=pltpu.CompilerParams(dimension_semantics=("parallel",)),
        name="attn_oproj",
    )(o_ctx, o_lat, x, mods, w_o, ln_g, ln_b)


def _swap16(w):
    q = QK_ROPE // 4
    return jnp.concatenate([w[..., q:2 * q], w[..., :q], w[..., 3 * q:], w[..., 2 * q:3 * q]], axis=-1)


def _mla_params(w_dqkv, q_norm, w_uq, kv_norm, w_ukv):
    w_kr = w_dqkv[:, Q_LORA + KV_LORA:]
    wq = w_uq.reshape(Q_LORA, N_HEADS, QK_NOPE + QK_ROPE)
    wq_r = wq[..., QK_NOPE:]
    wq = jnp.concatenate([wq[..., :QK_NOPE], wq_r, _swap16(wq_r)], axis=-1)
    wkv = w_ukv.reshape(KV_LORA, N_HEADS, QK_NOPE + V_DIM)
    return {
        "w_dq": w_dqkv[:, :Q_LORA].astype(BF16),
        "w_dkv": w_dqkv[:, Q_LORA:Q_LORA + KV_LORA].astype(BF16),
        "w_kr": jnp.concatenate([w_kr, _swap16(w_kr)], axis=-1).astype(BF16),
        "q_norm": q_norm.reshape(1, Q_LORA),
        "kv_norm": kv_norm.reshape(1, KV_LORA),
        "w_uq": wq.reshape(Q_LORA, N_HEADS * HEAD_PAD).astype(BF16),
        "w_ukn": wkv[..., :QK_NOPE].reshape(KV_LORA, N_HEADS * QK_NOPE).astype(BF16),
        "w_uv": wkv[..., QK_NOPE:].reshape(KV_LORA, N_HEADS * V_DIM).astype(BF16),
    }


def _rope_tables():
    nf = QK_ROPE // 4
    t = np.arange(DEC_SEQ)
    row = (t // GRID_W).astype(np.float32)
    col = (t % GRID_W).astype(np.float32)
    inv = (ROPE_THETA ** (-np.arange(nf, dtype=np.float32) / nf)).astype(np.float32)
    ar, ac = row[:, None] * inv, col[:, None] * inv
    pad = np.zeros((DEC_SEQ, QK_ROPE), np.float32)
    cos = np.concatenate([np.cos(ar), np.cos(ar), np.cos(ac), np.cos(ac), pad], axis=-1)
    sin = np.concatenate([-np.sin(ar), np.sin(ar), -np.sin(ac), np.sin(ac), pad], axis=-1)
    cos_id = np.concatenate([np.ones((DEC_SEQ, QK_ROPE), np.float32), pad], axis=-1)
    sin_id = np.zeros((DEC_SEQ, 2 * QK_ROPE), np.float32)
    return (jnp.asarray(np.stack([cos_id, cos]).astype(np.float32)),
            jnp.asarray(np.stack([sin_id, sin]).astype(np.float32)))


def kernel(x_prompt, x_sample, cache_ckv, cache_krope, c, c_ctx, ada_w, ada_b, ln_g, ln_b, conv_w_in, conv_k, conv_w_out, mla_w_dqkv, mla_q_norm, mla_w_uq, mla_kv_norm, mla_w_ukv, mla_w_o, router_w, router_bias, exp_w_gate, exp_w_up, exp_w_down, sh_w_gate, sh_w_up, sh_w_down):
    cvecs =jnp.concatenate([c_ctx[None, :], c, jnp.zeros((SUBLANES - N_GROUPS_ROWS, D_MODEL), F32)], axis=0)
    mods = _adaln(cvecs, ada_w, ada_b)

    def ln(l, k):
        return ln_g[l, k].reshape(1, D_MODEL), ln_b[l, k].reshape(1, D_MODEL)

    def moe_layer(xin, l, split_streams):
        idx, pos, wcol, counts, hp_a, hp_b = _router(xin, mods[l], router_w[l].T,
                                                     router_bias[l].reshape(N_EXPERTS, 1))
        shared = (sh_w_gate[l].astype(BF16), sh_w_up[l].astype(BF16), sh_w_down[l].astype(BF16))
        outs = {}
        for ch in sorted(range(len(MOE_CHUNKS)), key=lambda j: -MOE_CHUNKS[j][1]):
            row0, rows = MOE_CHUNKS[ch]
            n_tiles = _n_tiles(rows)
            cnt = counts[ch, :, 0].astype(jnp.int32)
            start, tile_expert, tile_valid, n_used, plan = _expert_layout(cnt, n_tiles)
            dest = _dest_rows(start, idx[:, row0:row0 + rows], pos[:, row0:row0 + rows])
            xs_a, xs_b = _sc_scatter_rows(hp_a, hp_b, dest, row0, n_tiles * FFN_TM)
            ys_a, ys_b = _expert_ffn(xs_a, xs_b, tile_expert, tile_valid, n_used, plan,
                                     exp_w_gate, exp_w_up, exp_w_down, l)
            first = 0
            for eighths in COMBINE_EIGHTHS:
                piece = rows * eighths // 8
                dest_row = dest[:, first:first + piece].reshape(1, piece * TOP_K)
                g_a, g_b = _sc_gather_rows(ys_a, ys_b, dest_row)
                g_a = g_a.reshape(TOP_K, piece, D_QUARTER)
                g_b = g_b.reshape(TOP_K, piece, D_QUARTER)
                outs = _combine(xin, mods[l], wcol, g_a, g_b, *shared, *ln(l, 1), row0=row0 + first,
                                split_streams=split_streams, prev=outs)
                first += piece
        return (outs["ctx"], outs["lat"]) if split_streams else outs["all"]

    x = _conv_mixer(x_prompt.reshape(T_CTX, D_MODEL), x_sample.reshape(T_LAT, D_MODEL), mods[0],
                    conv_w_in[0].astype(BF16), conv_k[0], conv_w_out[0].astype(BF16), *ln(0, 0))
    x = moe_layer(x, 0, split_streams=False)

    p = _mla_params(mla_w_dqkv[0], mla_q_norm[0], mla_w_uq[0], mla_kv_norm[0], mla_w_ukv[0])
    rope_a, rope_b = _rope_tables()
    q, k, v, ckv, kr = _mla_proj(x, mods[1], p, rope_a, rope_b)
    kr_cache = jnp.concatenate([cache_krope[:, 0].reshape(DEC_BATCH * PAST_LEN, QK_ROPE),
                                jnp.zeros((DEC_BATCH * PAST_LEN, QK_ROPE), F32)], axis=-1)
    kc, vc = _cache_kv(cache_ckv[:, 0].reshape(DEC_BATCH * PAST_LEN, KV_LORA), kr_cache, p)
    o_ctx = _ctx_attention(q, k, v)
    o_lat = _lat_attention(q, k, v, kc, vc)
    x = _oproj(o_ctx, o_lat, x, mods[1], mla_w_o[0].astype(BF16), *ln(1, 0))
    y_ctx, y_lat = moe_layer(x, 1, split_streams=True)

    y_prompt = y_ctx.reshape(BATCH, SEQ, D_MODEL)
    y_sample = y_lat.reshape(DEC_BATCH, DEC_SEQ, D_MODEL)
    state_ckv = ckv[:T_CTX].reshape(BATCH, 1, SEQ, KV_LORA)
    state_krope = kr[:T_CTX, :QK_ROPE].reshape(BATCH, 1, SEQ, QK_ROPE)
    return (y_prompt, y_sample, state_ckv, state_krope)
```

```python
import functools
import math

import jax
import jax.numpy as jnp
import numpy as np
from jax import lax
from jax.experimental import pallas as pl
from jax.experimental.pallas import tpu as pltpu
from jax.experimental.pallas import tpu_sc as plsc

D_MODEL = 1024
BATCH = 16
SEQ = 256
DEPTH = 2
DEC_BATCH = 4
DEC_SEQ = 4096
PAST_LEN = 512
GRID_W = 64

N_HEADS = 8
QK_NOPE = 128
QK_ROPE = 64
V_DIM = 128
Q_LORA = 384
KV_LORA = 256
ROPE_THETA = 10000.0
ATTN_SCALE = (QK_NOPE + QK_ROPE) ** -0.5
HEAD_PAD = 256
Q_PRESCALE = ATTN_SCALE * math.log2(math.e)

N_EXPERTS = 64
TOP_K = 8
N_GROUPS = 8
TOPK_GROUPS = 4
GROUP_SIZE = N_EXPERTS // N_GROUPS
D_EXPERT = 256
D_SHARED = 256
ROUTED_SCALE = 2.5

ALPHA = (2 * DEPTH) ** 0.25
LN_EPS = 1e-5
RMS_EPS = 1e-6

GROUP_ROWS = 4096
N_GROUPS_ROWS = 1 + DEC_BATCH
T_CTX = BATCH * SEQ
T_LAT = DEC_BATCH * DEC_SEQ
T_ALL = T_CTX + T_LAT
LANES = 128
SUBLANES = 8

F32 = jnp.float32
BF16 = jnp.bfloat16
NEG_INF = float("-inf")


def _dot(a, b):
    return jnp.dot(a, b, preferred_element_type=F32)


def _dot_nt(a, b, precision=None):
    return lax.dot_general(a, b, (((1,), (1,)), ((), ())), precision=precision,
                           preferred_element_type=F32)


def _layer_norm(v, g, b):
    mu = jnp.mean(v, axis=-1, keepdims=True)
    d = v - mu
    var = jnp.mean(d * d, axis=-1, keepdims=True)
    return d * lax.rsqrt(var + LN_EPS) * g + b


def _rms_norm(v, g):
    return v * lax.rsqrt(jnp.mean(v * v, axis=-1, keepdims=True) + RMS_EPS) * g


def _silu(v):
    return v / (1.0 + jnp.exp(-v))


def _sigmoid(v):
    return 1.0 / (1.0 + jnp.exp(-v))


ADALN_COLS = 2


def _adaln_kernel(c_ref, w_ref, b_ref, o_ref):
    c = c_ref[...]
    s = _silu(c)
    res = jnp.dot(s, w_ref[0], precision=lax.Precision.HIGHEST, preferred_element_type=F32) + b_ref[0]
    for j in range(ADALN_COLS):
        o_ref[0, j] = res[:, j * D_MODEL:(j + 1) * D_MODEL]


def _adaln(cvecs, ada_w, ada_b):
    out = pl.pallas_call(
        _adaln_kernel,
        grid=(DEPTH, 6 // ADALN_COLS),
        in_specs=[
            pl.BlockSpec((SUBLANES, D_MODEL), lambda l, j: (0, 0)),
            pl.BlockSpec((1, D_MODEL, ADALN_COLS * D_MODEL), lambda l, j: (l, 0, j)),
            pl.BlockSpec((1, 1, ADALN_COLS * D_MODEL), lambda l, j: (l, 0, j)),
        ],
        out_specs=pl.BlockSpec((1, ADALN_COLS, SUBLANES, D_MODEL), lambda l, j: (l, j, 0, 0)),
        out_shape=jax.ShapeDtypeStruct((DEPTH, 6, SUBLANES, D_MODEL), F32),
        compiler_params=pltpu.CompilerParams(dimension_semantics=("parallel", "parallel")),
        name="adaln",
    )(cvecs, ada_w, ada_b.reshape(DEPTH, 1, 6 * D_MODEL))
    return jnp.transpose(out[:, :, :N_GROUPS_ROWS, :], (0, 2, 1, 3))


CONV_TM = 1024
CONV_PIECES = 2


def _conv_kernel(xc_ref, xcp_ref, xcn_ref, xl_ref, xlp_ref, xln_ref, mods_ref, win_ref, ck_ref, wout_ref,
                 g_ref, b_ref, o_ref):
    i = pl.program_id(0)
    tm = xc_ref.shape[0]
    is_ctx = i < T_CTX // tm
    m = mods_ref[0]
    sm, cm, gm = m[0:1], m[1:2], m[2:3]
    x = jnp.where(is_ctx, xc_ref[...], xl_ref[...])
    xp = jnp.where(is_ctx, xcp_ref[...], xlp_ref[...])
    xn = jnp.where(is_ctx, xcn_ref[...], xln_ref[...])
    th = tm // CONV_PIECES
    halves = [x[j * th:(j + 1) * th] for j in range(CONV_PIECES)]
    lhs = list(halves)
    lhs[0] = jnp.concatenate([xp, lhs[0]], axis=0)
    lhs[-1] = jnp.concatenate([lhs[-1], xn], axis=0)
    zs = [_dot((xh * (1.0 + cm) + sm).astype(BF16), win_ref[...]) for xh in lhs]
    us = [z[:, D_MODEL:2 * D_MODEL] * z[:, 2 * D_MODEL:] for z in zs]
    u_before = us[0][SUBLANES - 1:SUBLANES]
    u_after = us[-1][us[-1].shape[0] - SUBLANES:us[-1].shape[0] - SUBLANES + 1]
    zs[0], us[0] = zs[0][SUBLANES:], us[0][SUBLANES:]
    zs[-1], us[-1] = zs[-1][:th], us[-1][:th]
    befores = [u_before] + [u[th - 1:th] for u in us[:-1]]
    afters = [u[0:1] for u in us[1:]] + [u_after]

    ck = ck_ref[...]
    row = lax.broadcasted_iota(jnp.int32, (th, 1), 0)
    for hf in range(CONV_PIECES):
        grow = i * tm + hf * th + row
        seq_len = jnp.where(grow < T_CTX, SEQ, DEC_SEQ)
        pos = jnp.bitwise_and(grow, seq_len - 1)
        u = us[hf]
        left = jnp.where(row == 0, befores[hf], pltpu.roll(u, 1, 0))
        left = jnp.where(pos == 0, 0.0, left)
        right = jnp.where(row == th - 1, afters[hf], pltpu.roll(u, th - 1, 0))
        right = jnp.where(pos == seq_len - 1, 0.0, right)
        conv = left * ck[0:1] + u * ck[1:2] + right * ck[2:3]
        v = (zs[hf][:, :D_MODEL] * conv).astype(BF16)
        y = _dot(v, wout_ref[...])
        o_ref[hf * th:(hf + 1) * th, :] = _layer_norm(ALPHA * halves[hf] + gm * y, g_ref[...], b_ref[...])


def _conv_mixer(x_ctx, x_lat, mods, w_in, conv_k, w_out, ln_g, ln_b):
    tm = CONV_TM
    per8 = tm // SUBLANES
    n_ctx = T_CTX // tm
    n_lat = T_LAT // tm

    def stream_specs(first, n):
        def blk(i):
            return jnp.clip(i - first, 0, n - 1)
        return [
            pl.BlockSpec((tm, D_MODEL), lambda i: (blk(i), 0)),
            pl.BlockSpec((SUBLANES, D_MODEL), lambda i: (jnp.maximum(blk(i) * per8 - 1, 0), 0)),
            pl.BlockSpec((SUBLANES, D_MODEL), lambda i: (jnp.minimum((blk(i) + 1) * per8, n * per8 - 1), 0)),
        ]

    return pl.pallas_call(
        _conv_kernel,
        grid=(n_ctx + n_lat,),
        in_specs=stream_specs(0, n_ctx) + stream_specs(n_ctx, n_lat) + [
            pl.BlockSpec((1, 6, D_MODEL), lambda i: (i * tm // GROUP_ROWS, 0, 0)),
            pl.BlockSpec((D_MODEL, 3 * D_MODEL), lambda i: (0, 0)),
            pl.BlockSpec((3, D_MODEL), lambda i: (0, 0)),
            pl.BlockSpec((D_MODEL, D_MODEL), lambda i: (0, 0)),
            pl.BlockSpec((1, D_MODEL), lambda i: (0, 0)),
            pl.BlockSpec((1, D_MODEL), lambda i: (0, 0)),
        ],
        out_specs=pl.BlockSpec((tm, D_MODEL), lambda i: (i, 0)),
        out_shape=jax.ShapeDtypeStruct((T_ALL, D_MODEL), F32),
        compiler_params=pltpu.CompilerParams(dimension_semantics=("parallel",)),
        name="conv_mixer",
    )(x_ctx, x_ctx, x_ctx, x_lat, x_lat, x_lat, mods, w_in, conv_k, w_out, ln_g, ln_b)


ROUTER_TM = 1024
MOE_CHUNKS = ((0, T_ALL),)
COMBINE_EIGHTHS = (2, 2, 2, 2)


def _first_argmax_mask(cur, ridx, n):
    mx = jnp.max(cur, axis=0, keepdims=True)
    first = jnp.min(jnp.where(cur == mx, ridx, n), axis=0, keepdims=True)
    return ridx == first, mx


def _router_kernel(x_ref, mods_ref, rwt_ref, bias_ref, before_ref, idx_ref, pos_ref, wcol_ref, count_ref,
                   hpa_ref, hpb_ref, carry_ref):
    tm = x_ref.shape[0]
    m = mods_ref[0]
    sf, cf = m[3:4], m[4:5]
    hc = x_ref[...] * (1.0 + cf) + sf
    words = _pack_rows(hc)
    hpa_ref[...] = words[:, :D_QUARTER]
    hpb_ref[...] = words[:, D_QUARTER:]
    logits = _dot_nt(rwt_ref[...], hc, precision=lax.Precision.HIGHEST)
    scores = _sigmoid(logits)
    biased = scores + bias_ref[...]

    ridx8 = lax.broadcasted_iota(jnp.int32, (GROUP_SIZE, tm), 0)
    gscore = jnp.full((N_GROUPS, tm), NEG_INF, F32)
    for g in range(N_GROUPS):
        blk = biased[g * GROUP_SIZE:(g + 1) * GROUP_SIZE]
        sel, m1 = _first_argmax_mask(blk, ridx8, GROUP_SIZE)
        m2 = jnp.max(jnp.where(sel, NEG_INF, blk), axis=0, keepdims=True)
        gscore = jnp.where(ridx8 == g, m1 + m2, gscore)

    gmask = jnp.zeros((N_GROUPS, tm), jnp.bool_)
    cur = gscore
    for _ in range(TOPK_GROUPS):
        sel, _unused = _first_argmax_mask(cur, ridx8, N_GROUPS)
        gmask = jnp.logical_or(gmask, sel)
        cur = jnp.where(sel, NEG_INF, cur)

    gmask_f = gmask.astype(F32)
    blocks = []
    for g in range(N_GROUPS):
        keep = jnp.broadcast_to(gmask_f[g:g + 1], (GROUP_SIZE, tm)) > 0.5
        blocks.append(jnp.where(keep, biased[g * GROUP_SIZE:(g + 1) * GROUP_SIZE], NEG_INF))
    cur = jnp.concatenate(blocks, axis=0)

    first_tiles = [row0 // tm for row0, _rows in MOE_CHUNKS]
    starts_chunk = functools.reduce(jnp.logical_or, [pl.program_id(0) == ft for ft in first_tiles])

    @pl.when(starts_chunk)
    def _():
        carry_ref[...] = jnp.zeros(carry_ref.shape, F32)

    ridx = lax.broadcasted_iota(jnp.int32, (N_EXPERTS, tm), 0)
    kidx = lax.broadcasted_iota(jnp.int32, (TOP_K, tm), 0)
    sels = []
    chosen = jnp.zeros((N_EXPERTS, tm), jnp.bool_)
    idx_rows = jnp.zeros((TOP_K, tm), jnp.int32)
    for k in range(TOP_K):
        mx = jnp.max(cur, axis=0, keepdims=True)
        first = jnp.min(jnp.where(cur == mx, ridx, N_EXPERTS), axis=0, keepdims=True)
        sel = ridx == first
        sels.append(sel)
        chosen = jnp.logical_or(chosen, sel)
        idx_rows = jnp.where(kidx == k, first, idx_rows)
        cur = jnp.where(sel, NEG_INF, cur)

    onehot = chosen.astype(F32)
    rank = carry_ref[...] + _dot(onehot.astype(BF16), before_ref[...])
    carry_ref[...] = carry_ref[...] + jnp.sum(onehot, axis=1, keepdims=True)
    count_ref[0] = jnp.broadcast_to(carry_ref[...], count_ref.shape[1:])

    w = jnp.where(chosen, scores, 0.0)
    w = w / jnp.sum(w, axis=0, keepdims=True) * ROUTED_SCALE
    pos_rows = jnp.zeros((TOP_K, tm), F32)
    w_rows = jnp.zeros((TOP_K, tm), F32)
    for k in range(TOP_K):
        pos_rows = jnp.where(kidx == k, jnp.sum(jnp.where(sels[k], rank, 0.0), axis=0, keepdims=True), pos_rows)
        w_rows = jnp.where(kidx == k, jnp.sum(jnp.where(sels[k], w, 0.0), axis=0, keepdims=True), w_rows)
    idx_ref[...] = idx_rows
    pos_ref[...] = pos_rows.astype(jnp.int32)
    wpad = jnp.concatenate([w_rows, jnp.zeros((LANES - TOP_K, tm), F32)], axis=0)
    wcol_ref[...] = wpad.T


def _router(x, mods, router_wt, router_bias):
    t = x.shape[0]
    tm = ROUTER_TM
    before = jnp.asarray(np.triu(np.ones((tm, tm), np.float32), 1), dtype=BF16)

    def chunk_of(i):
        return sum((i >= row0 // tm).astype(jnp.int32) for row0, _rows in MOE_CHUNKS[1:])

    return pl.pallas_call(
        _router_kernel,
        grid=(t // tm,),
        in_specs=[
            pl.BlockSpec((tm, D_MODEL), lambda i: (i, 0)),
            pl.BlockSpec((1, 6, D_MODEL), lambda i: (i * tm // GROUP_ROWS, 0, 0)),
            pl.BlockSpec((N_EXPERTS, D_MODEL), lambda i: (0, 0)),
            pl.BlockSpec((N_EXPERTS, 1), lambda i: (0, 0)),
            pl.BlockSpec((tm, tm), lambda i: (0, 0)),
        ],
        out_specs=[
            pl.BlockSpec((TOP_K, tm), lambda i: (0, i)),
            pl.BlockSpec((TOP_K, tm), lambda i: (0, i)),
            pl.BlockSpec((tm, LANES), lambda i: (i, 0)),
            pl.BlockSpec((1, N_EXPERTS, LANES), lambda i: (chunk_of(i), 0, 0)),
            pl.BlockSpec((tm, D_QUARTER), lambda i: (i, 0)),
            pl.BlockSpec((tm, D_QUARTER), lambda i: (i, 0)),
        ],
        out_shape=[
            jax.ShapeDtypeStruct((TOP_K, t), jnp.int32),
            jax.ShapeDtypeStruct((TOP_K, t), jnp.int32),
            jax.ShapeDtypeStruct((t, LANES), F32),
            jax.ShapeDtypeStruct((len(MOE_CHUNKS), N_EXPERTS, LANES), F32),
            jax.ShapeDtypeStruct((t, D_QUARTER), jnp.uint32),
            jax.ShapeDtypeStruct((t, D_QUARTER), jnp.uint32),
        ],
        scratch_shapes=[pltpu.VMEM((N_EXPERTS, 1), F32)],
        compiler_params=pltpu.CompilerParams(dimension_semantics=("arbitrary",)),
        name="moe_router",
    )(x, mods, router_wt, router_bias, before)


FFN_TM = 512


def _n_tiles(rows):
    return rows * TOP_K // FFN_TM + N_EXPERTS


def _expert_layout(counts, n_tiles):
    padded = (counts + FFN_TM - 1) // FFN_TM * FFN_TM
    end = jnp.cumsum(padded)
    start = end - padded
    tile_row = jnp.arange(n_tiles, dtype=jnp.int32) * FFN_TM
    tile_expert = jnp.minimum(jnp.sum(end[None, :] <= tile_row[:, None], axis=1), N_EXPERTS - 1)
    of_tile = tile_expert[:, None] == jnp.arange(N_EXPERTS, dtype=jnp.int32)[None, :]
    live_end = jnp.sum(jnp.where(of_tile, (start + counts)[None, :], 0), axis=1)
    tile_valid = jnp.clip(live_end - tile_row, 0, FFN_TM)
    n_used = (end[-1] // FFN_TM).astype(jnp.int32).reshape(1)
    used = jnp.arange(n_tiles, dtype=jnp.int32) < n_used[0]
    prev_expert = jnp.concatenate([jnp.full((1,), -1, tile_expert.dtype), tile_expert[:-1]])
    run_flag = jnp.logical_and(used, tile_expert != prev_expert)
    run_slot = (jnp.cumsum(run_flag.astype(jnp.int32)) - 1) % 2
    later = jnp.logical_and(used[None, :], tile_expert[None, :] > tile_expert[:, None])
    next_expert = jnp.min(jnp.where(later, tile_expert[None, :], N_EXPERTS), axis=1)
    next_expert = jnp.where(next_expert == N_EXPERTS, -1, next_expert)
    plan = (run_flag.astype(jnp.int32), run_slot.astype(jnp.int32), next_expert.astype(jnp.int32))
    return start.astype(jnp.int32), tile_expert.astype(jnp.int32), tile_valid.astype(jnp.int32), n_used, plan


D_HALF = D_MODEL // 2
D_QUARTER = D_MODEL // 4


def _pack_rows(v):
    hi = lax.bitcast_convert_type(v[:, :D_HALF].astype(BF16).astype(F32), jnp.uint32)
    lo = lax.bitcast_convert_type(v[:, D_HALF:].astype(BF16).astype(F32), jnp.uint32)
    return jnp.bitwise_or(hi, jnp.right_shift(lo, jnp.uint32(16)))


def _unpack_rows(w):
    hi = lax.bitcast_convert_type(jnp.bitwise_and(w, jnp.uint32(0xFFFF0000)), F32)
    lo = lax.bitcast_convert_type(jnp.left_shift(w, jnp.uint32(16)), F32)
    return hi, lo


DEST_TM = 2048


def _dest_kernel(start_ref, idx_ref, pos_ref, dest_ref):
    idx = idx_ref[...]
    base = jnp.zeros(idx.shape, jnp.int32)
    for e in range(N_EXPERTS):
        base = jnp.where(idx == e, start_ref[e], base)
    dest_ref[...] = base + pos_ref[...]


def _dest_rows(start, idx, pos):
    t = idx.shape[1]
    return pl.pallas_call(
        _dest_kernel,
        grid_spec=pltpu.PrefetchScalarGridSpec(
            num_scalar_prefetch=1,
            grid=(t // DEST_TM,),
            in_specs=[pl.BlockSpec((TOP_K, DEST_TM), lambda i, s: (0, i)),
                      pl.BlockSpec((TOP_K, DEST_TM), lambda i, s: (0, i))],
            out_specs=pl.BlockSpec((TOP_K, DEST_TM), lambda i, s: (0, i)),
        ),
        out_shape=jax.ShapeDtypeStruct((TOP_K, t), jnp.int32),
        compiler_params=pltpu.CompilerParams(dimension_semantics=("parallel",)),
        name="moe_dest",
    )(start, idx, pos)


SC_WINDOW = 128


def _sc_mesh():
    return plsc.VectorSubcoreMesh(core_axis_name="c", subcore_axis_name="s")


def _sc_scatter_rows(x_a, x_b, dest, row0, n_sorted):
    t = dest.shape[1]
    blk0 = row0 // SC_WINDOW
    out = jax.ShapeDtypeStruct((n_sorted, D_QUARTER), x_a.dtype)

    @functools.partial(pl.kernel, out_type=(out, out), mesh=_sc_mesh(), scratch_types=[])
    def scatter(xa_hbm, xb_hbm, i_hbm, oa_hbm, ob_hbm):
        for x_hbm, o_hbm in ((xa_hbm, oa_hbm), (xb_hbm, ob_hbm)):
            def body(x_vmem, i_vmem, o_hbm=o_hbm):
                pltpu.sync_copy(x_vmem, o_hbm.at[i_vmem.at[0]])

            pltpu.emit_pipeline(
                body,
                grid=(t // SC_WINDOW, TOP_K),
                in_specs=[pl.BlockSpec((SC_WINDOW, D_QUARTER), lambda i, k: (i + blk0, 0)),
                          pl.BlockSpec((1, SC_WINDOW), lambda i, k: (k, i))],
                out_specs=[],
                core_axis_name=("c", "s"),
                dimension_semantics=(pltpu.PARALLEL, pltpu.ARBITRARY),
            )(x_hbm, i_hbm)

    return scatter(x_a, x_b, dest)


def _sc_gather_rows(table_a, table_b, idx):
    m = idx.shape[1]
    out = jax.ShapeDtypeStruct((m, D_QUARTER), table_a.dtype)

    @functools.partial(pl.kernel, out_type=(out, out), mesh=_sc_mesh(), scratch_types=[])
    def gather(ta_hbm, tb_hbm, i_hbm, oa_hbm, ob_hbm):
        for t_hbm, o_hbm in ((ta_hbm, oa_hbm), (tb_hbm, ob_hbm)):
            def body(i_vmem, o_vmem, t_hbm=t_hbm):
                pltpu.sync_copy(t_hbm.at[i_vmem.at[0]], o_vmem)

            pltpu.emit_pipeline(
                body,
                grid=(m // SC_WINDOW,),
                in_specs=[pl.BlockSpec((1, SC_WINDOW), lambda i: (0, i))],
                out_specs=[pl.BlockSpec((SC_WINDOW, D_QUARTER), lambda i: (i, 0))],
                core_axis_name=("c", "s"),
                dimension_semantics=(pltpu.PARALLEL,),
            )(i_hbm, o_hbm)

    return gather(table_a, table_b, idx)


FFN_STEP_TILES = 2


def _ffn_kernel(layer, te_ref, tv_ref, nu_ref, flag_ref, slot_ref, next_ref, xa_ref, xb_ref,
                wg_hbm, wu_hbm, wd_hbm, ya_ref, yb_ref, wg_st, wu_st, wd_st, wgb_ref, wub_ref, wdb_ref, sem):
    step = pl.program_id(0)

    def fetch(expert, slot):
        return [pltpu.make_async_copy(src.at[layer, expert], dst.at[slot], sem.at[slot])
                for src, dst in ((wg_hbm, wg_st), (wu_hbm, wu_st), (wd_hbm, wd_st))]

    for s in range(FFN_STEP_TILES):
        tile = step * FFN_STEP_TILES + s

        @pl.when(jnp.logical_and(tile < nu_ref[0], flag_ref[tile] == 1))
        def _():
            slot = slot_ref[tile]

            @pl.when(tile == 0)
            def _():
                for cp in fetch(te_ref[tile], slot):
                    cp.start()

            for cp in fetch(te_ref[tile], slot):
                cp.wait()
            wgb_ref[slot] = wg_st[slot].astype(BF16)
            wub_ref[slot] = wu_st[slot].astype(BF16)
            wdb_ref[slot] = wd_st[slot].astype(BF16)

            @pl.when(next_ref[tile] >= 0)
            def _():
                for cp in fetch(next_ref[tile], 1 - slot):
                    cp.start()

    @pl.when(step * FFN_STEP_TILES < nu_ref[0])
    def _():
        for s in range(FFN_STEP_TILES):
            tile = step * FFN_STEP_TILES + s
            slot = slot_ref[tile]
            wg = wgb_ref[slot]
            wu = wub_ref[slot]
            wd = wdb_ref[slot]
            rows = pl.ds(s * FFN_TM, FFN_TM)
            live = lax.broadcasted_iota(jnp.int32, (FFN_TM, D_QUARTER), 0) < tv_ref[tile]
            hi_a, lo_a = _unpack_rows(jnp.where(live, xa_ref[rows, :], jnp.uint32(0)))
            hi_b, lo_b = _unpack_rows(jnp.where(live, xb_ref[rows, :], jnp.uint32(0)))
            xb = jnp.concatenate([hi_a.astype(BF16), hi_b.astype(BF16), lo_a.astype(BF16), lo_b.astype(BF16)],
                                 axis=1)
            a = (_silu(_dot(xb, wg)) * _dot(xb, wu)).astype(BF16)
            words = _pack_rows(_dot(a, wd))
            ya_ref[rows, :] = words[:, :D_QUARTER]
            yb_ref[rows, :] = words[:, D_QUARTER:]

    @pl.when(step * FFN_STEP_TILES >= nu_ref[0])
    def _():
        ya_ref[...] = jnp.zeros(ya_ref.shape, jnp.uint32)
        yb_ref[...] = jnp.zeros(yb_ref.shape, jnp.uint32)


def _expert_ffn(xs_a, xs_b, tile_expert, tile_valid, n_used, plan, wg, wu, wd, layer):
    n_tiles = xs_a.shape[0] // FFN_TM
    step_rows = FFN_STEP_TILES * FFN_TM

    def row_map(i, te, tv, nu, fl, sl, nx):
        return (jnp.minimum(i, (nu[0] - 1) // FFN_STEP_TILES), 0)

    def out_map(i, te, tv, nu, fl, sl, nx):
        return (i, 0)

    hbm = pl.BlockSpec(memory_space=pl.ANY)
    return pl.pallas_call(
        functools.partial(_ffn_kernel, layer),
        grid_spec=pltpu.PrefetchScalarGridSpec(
            num_scalar_prefetch=6,
            grid=(n_tiles // FFN_STEP_TILES,),
            in_specs=[pl.BlockSpec((step_rows, D_QUARTER), row_map),
                      pl.BlockSpec((step_rows, D_QUARTER), row_map), hbm, hbm, hbm],
            out_specs=[pl.BlockSpec((step_rows, D_QUARTER), out_map), pl.BlockSpec((step_rows, D_QUARTER), out_map)],
            scratch_shapes=[pltpu.VMEM((2, D_MODEL, D_EXPERT), F32), pltpu.VMEM((2, D_MODEL, D_EXPERT), F32),
                            pltpu.VMEM((2, D_EXPERT, D_MODEL), F32),
                            pltpu.VMEM((2, D_MODEL, D_EXPERT), BF16), pltpu.VMEM((2, D_MODEL, D_EXPERT), BF16),
                            pltpu.VMEM((2, D_EXPERT, D_MODEL), BF16), pltpu.SemaphoreType.DMA((2,))],
        ),
        out_shape=[jax.ShapeDtypeStruct((n_tiles * FFN_TM, D_QUARTER), jnp.uint32),
                   jax.ShapeDtypeStruct((n_tiles * FFN_TM, D_QUARTER), jnp.uint32)],
        compiler_params=pltpu.CompilerParams(dimension_semantics=("arbitrary",)),
        name="moe_expert_ffn",
    )(tile_expert, tile_valid, n_used, *plan, xs_a, xs_b, wg, wu, wd)


COMBINE_TM = 512


def _combine_kernel(n_prev, tile0, x_ref, mods_ref, wcol_ref, ga_ref, gb_ref, sg_ref, su_ref, sd_ref, g_ref, b_ref,
                    *rest):
    o_refs = rest[n_prev:]
    m = mods_ref[0]
    sf, cf, gf = m[3:4], m[4:5], m[5:6]
    x = x_ref[...]
    hc = (x * (1.0 + cf) + sf).astype(BF16)
    a = _silu(_dot(hc, sg_ref[...])) * _dot(hc, su_ref[...])
    y = _dot(a.astype(BF16), sd_ref[...])
    wcol = wcol_ref[...]
    parts = [y[:, q * D_QUARTER:(q + 1) * D_QUARTER] for q in range(4)]
    for k in range(TOP_K):
        hi_a, lo_a = _unpack_rows(ga_ref[k])
        hi_b, lo_b = _unpack_rows(gb_ref[k])
        wk = wcol[:, k:k + 1]
        parts = [parts[0] + wk * hi_a, parts[1] + wk * hi_b, parts[2] + wk * lo_a, parts[3] + wk * lo_b]
    y = jnp.concatenate(parts, axis=1)
    out = _layer_norm(ALPHA * x + gf * y, g_ref[...], b_ref[...])
    if len(o_refs) == 1:
        o_refs[0][...] = out
    else:
        is_ctx = pl.program_id(0) + tile0 < T_CTX // x_ref.shape[0]

        @pl.when(is_ctx)
        def _():
            o_refs[0][...] = out

        @pl.when(jnp.logical_not(is_ctx))
        def _():
            o_refs[1][...] = out


def _combine(x, mods, wcol, g_a, g_b, sg, su, sd, ln_g, ln_b, row0, split_streams, prev):
    tm = COMBINE_TM
    n = g_a.shape[1] // tm
    tile0 = row0 // tm
    n_ctx = T_CTX // tm
    full = lambda shape: pl.BlockSpec(shape, lambda i: (0,) * len(shape))
    layouts = {
        "all": (lambda i: (tile0 + i, 0), T_ALL),
        "ctx": (lambda i: (jnp.minimum(tile0 + i, n_ctx - 1), 0), T_CTX),
        "lat": (lambda i: (jnp.maximum(tile0 + i - n_ctx, 0), 0), T_LAT),
    }
    if split_streams:
        kinds = (["ctx"] if tile0 < n_ctx else []) + (["lat"] if tile0 + n > n_ctx else [])
    else:
        kinds = ["all"]
    carried = [kd for kd in kinds if kd in prev]
    outs = pl.pallas_call(
        functools.partial(_combine_kernel, len(carried), tile0),
        grid=(n,),
        in_specs=[
            pl.BlockSpec((tm, D_MODEL), lambda i: (tile0 + i, 0)),
            pl.BlockSpec((1, 6, D_MODEL), lambda i: ((tile0 + i) * tm // GROUP_ROWS, 0, 0)),
            pl.BlockSpec((tm, LANES), lambda i: (tile0 + i, 0)),
            pl.BlockSpec((TOP_K, tm, D_QUARTER), lambda i: (0, i, 0)),
            pl.BlockSpec((TOP_K, tm, D_QUARTER), lambda i: (0, i, 0)),
            full((D_MODEL, D_SHARED)), full((D_MODEL, D_SHARED)), full((D_SHARED, D_MODEL)),
            full((1, D_MODEL)), full((1, D_MODEL)),
        ] + [pl.BlockSpec(memory_space=pl.ANY) for _ in carried],
        out_specs=[pl.BlockSpec((tm, D_MODEL), layouts[kd][0]) for kd in kinds],
        out_shape=[jax.ShapeDtypeStruct((layouts[kd][1], D_MODEL), F32) for kd in kinds],
        input_output_aliases={10 + j: kinds.index(kd) for j, kd in enumerate(carried)},
        compiler_params=pltpu.CompilerParams(dimension_semantics=("arbitrary",)),
        name="moe_combine",
    )(x, mods, wcol, g_a, g_b, sg, su, sd, ln_g, ln_b, *[prev[kd] for kd in carried])
    return {**prev, **dict(zip(kinds, outs))}


MLA_TM = 1024


def _mla_proj_kernel(x_ref, mods_ref, wdq_ref, wdkv_ref, wkr_ref, qn_ref, kvn_ref, wuq_ref,
                     wukn_ref, wuv_ref, ta_ref, tb_ref, q_ref, k_ref, v_ref, ckv_ref, kr_ref):
    m = mods_ref[0]
    sm, cm = m[0:1], m[1:2]
    h = (x_ref[...] * (1.0 + cm) + sm).astype(BF16)
    cq = _rms_norm(_dot(h, wdq_ref[...]), qn_ref[...])
    ckv = _rms_norm(_dot(h, wdkv_ref[...]), kvn_ref[...])
    kr2 = _dot(h, wkr_ref[...])

    @pl.when(pl.program_id(0) < T_CTX // x_ref.shape[0])
    def _():
        ckv_ref[...] = ckv
        kr_ref[...] = kr2

    ka = ta_ref[0]
    kb = tb_ref[0]
    tm = ka.shape[0]
    ta = jnp.concatenate([jnp.full((tm, QK_NOPE), Q_PRESCALE, F32), ka * Q_PRESCALE], axis=1)
    tb = jnp.concatenate([jnp.zeros((tm, QK_NOPE), F32), kb * Q_PRESCALE], axis=1)
    krr = kr2 * ka + pltpu.roll(kr2, QK_ROPE, 1) * kb

    qpre = _dot(cq.astype(BF16), wuq_ref[...])
    ckv_b = ckv.astype(BF16)
    kn = _dot(ckv_b, wukn_ref[...])
    v_ref[...] = _dot(ckv_b, wuv_ref[...]).astype(BF16)
    for hd in range(N_HEADS):
        qh = qpre[:, hd * HEAD_PAD:(hd + 1) * HEAD_PAD]
        qrot = qh * ta + pltpu.roll(qh, HEAD_PAD - QK_ROPE, 1) * tb
        q_ref[:, hd * HEAD_PAD:(hd + 1) * HEAD_PAD] = qrot.astype(BF16)
        k_ref[:, hd * HEAD_PAD:hd * HEAD_PAD + QK_NOPE] = kn[:, hd * QK_NOPE:(hd + 1) * QK_NOPE].astype(BF16)
        k_ref[:, hd * HEAD_PAD + QK_NOPE:(hd + 1) * HEAD_PAD] = krr.astype(BF16)


def _mla_proj(x, mods, p, rope_a, rope_b):
    t = x.shape[0]
    tm = MLA_TM
    n_ctx = T_CTX // tm
    full = lambda shape: pl.BlockSpec(shape, lambda i: (0,) * len(shape))
    rope_spec = pl.BlockSpec(
        (1, tm, 2 * QK_ROPE),
        lambda i: (jnp.minimum(i * tm // GROUP_ROWS, 1), (i * tm % GROUP_ROWS) // tm, 0))
    return pl.pallas_call(
        _mla_proj_kernel,
        grid=(t // tm,),
        in_specs=[
            pl.BlockSpec((tm, D_MODEL), lambda i: (i, 0)),
            pl.BlockSpec((1, 6, D_MODEL), lambda i: (i * tm // GROUP_ROWS, 0, 0)),
            full((D_MODEL, Q_LORA)), full((D_MODEL, KV_LORA)), full((D_MODEL, 2 * QK_ROPE)),
            full((1, Q_LORA)), full((1, KV_LORA)),
            full((Q_LORA, N_HEADS * HEAD_PAD)),
            full((KV_LORA, N_HEADS * QK_NOPE)), full((KV_LORA, N_HEADS * V_DIM)),
            rope_spec, rope_spec,
        ],
        out_specs=[
            pl.BlockSpec((tm, N_HEADS * HEAD_PAD), lambda i: (i, 0)),
            pl.BlockSpec((tm, N_HEADS * HEAD_PAD), lambda i: (i, 0)),
            pl.BlockSpec((tm, N_HEADS * V_DIM), lambda i: (i, 0)),
            pl.BlockSpec((tm, KV_LORA), lambda i: (jnp.minimum(i, n_ctx - 1), 0)),
            pl.BlockSpec((tm, 2 * QK_ROPE), lambda i: (jnp.minimum(i, n_ctx - 1), 0)),
        ],
        out_shape=[
            jax.ShapeDtypeStruct((t, N_HEADS * HEAD_PAD), BF16),
            jax.ShapeDtypeStruct((t, N_HEADS * HEAD_PAD), BF16),
            jax.ShapeDtypeStruct((t, N_HEADS * V_DIM), BF16),
            jax.ShapeDtypeStruct((T_CTX, KV_LORA), F32),
            jax.ShapeDtypeStruct((T_CTX, 2 * QK_ROPE), F32),
        ],
        compiler_params=pltpu.CompilerParams(dimension_semantics=("arbitrary",)),
        name="mla_proj",
    )(x, mods, p["w_dq"], p["w_dkv"], p["w_kr"], p["q_norm"], p["kv_norm"], p["w_uq"],
      p["w_ukn"], p["w_uv"], rope_a, rope_b)


def _cache_kv_kernel(ckv_ref, kr_ref, wukn_ref, wuv_ref, k_ref, v_ref):
    ckv_b = ckv_ref[...].astype(BF16)
    kn = _dot(ckv_b, wukn_ref[...])
    v_ref[...] = _dot(ckv_b, wuv_ref[...]).astype(BF16)
    kr = kr_ref[...].astype(BF16)
    for hd in range(N_HEADS):
        k_ref[:, hd * HEAD_PAD:hd * HEAD_PAD + QK_NOPE] = kn[:, hd * QK_NOPE:(hd + 1) * QK_NOPE].astype(BF16)
        k_ref[:, hd * HEAD_PAD + QK_NOPE:(hd + 1) * HEAD_PAD] = kr


def _cache_kv(ckv, kr_pad, p):
    t = ckv.shape[0]
    tm = PAST_LEN
    full = lambda shape: pl.BlockSpec(shape, lambda i: (0,) * len(shape))
    return pl.pallas_call(
        _cache_kv_kernel,
        grid=(t // tm,),
        in_specs=[
            pl.BlockSpec((tm, KV_LORA), lambda i: (i, 0)),
            pl.BlockSpec((tm, 2 * QK_ROPE), lambda i: (i, 0)),
            full((KV_LORA, N_HEADS * QK_NOPE)), full((KV_LORA, N_HEADS * V_DIM)),
        ],
        out_specs=[
            pl.BlockSpec((tm, N_HEADS * HEAD_PAD), lambda i: (i, 0)),
            pl.BlockSpec((tm, N_HEADS * V_DIM), lambda i: (i, 0)),
        ],
        out_shape=[
            jax.ShapeDtypeStruct((t, N_HEADS * HEAD_PAD), BF16),
            jax.ShapeDtypeStruct((t, N_HEADS * V_DIM), BF16),
        ],
        compiler_params=pltpu.CompilerParams(dimension_semantics=("parallel",)),
        name="mla_cache_kv",
    )(ckv, kr_pad, p["w_ukn"], p["w_uv"])


def _ctx_attn_kernel(q_ref, k_ref, v_ref, o_ref):
    for hd in range(N_HEADS):
        q = q_ref[:, hd * HEAD_PAD:(hd + 1) * HEAD_PAD]
        k = k_ref[:, hd * HEAD_PAD:(hd + 1) * HEAD_PAD]
        s = _dot_nt(q, k)
        s = s - jnp.max(s, axis=-1, keepdims=True)
        p = jnp.exp2(s)
        p = p / jnp.sum(p, axis=-1, keepdims=True)
        o = _dot(p.astype(BF16), v_ref[:, hd * V_DIM:(hd + 1) * V_DIM])
        o_ref[:, hd * V_DIM:(hd + 1) * V_DIM] = o.astype(BF16)


def _ctx_attention(q, k, v):
    return pl.pallas_call(
        _ctx_attn_kernel,
        grid=(BATCH,),
        in_specs=[
            pl.BlockSpec((SEQ, N_HEADS * HEAD_PAD), lambda b: (b, 0)),
            pl.BlockSpec((SEQ, N_HEADS * HEAD_PAD), lambda b: (b, 0)),
            pl.BlockSpec((SEQ, N_HEADS * V_DIM), lambda b: (b, 0)),
        ],
        out_specs=pl.BlockSpec((SEQ, N_HEADS * V_DIM), lambda b: (b, 0)),
        out_shape=jax.ShapeDtypeStruct((T_CTX, N_HEADS * V_DIM), BF16),
        compiler_params=pltpu.CompilerParams(dimension_semantics=("parallel",)),
        name="ctx_attention",
    )(q, k, v)


LAT_TQ = 1024
LAT_TK = 2048
LAT_PIECES = 4


def _lat_attn_kernel(q_ref, k_ref, v_ref, kc_ref, vc_ref, o_ref, s_ref, p_ref, m_ref):
    spans = [(c0, LAT_TK) for c0 in range(0, DEC_SEQ, LAT_TK)] + [(DEC_SEQ, PAST_LEN)]
    n_chunks = len(spans)
    tp = q_ref.shape[0] // LAT_PIECES
    pieces = [pl.ds(j * tp, tp) for j in range(LAT_PIECES)]
    groups = [pieces[:2], pieces[2:]]
    state = {}

    def keys(c):
        c0, width = spans[c]
        return kc_ref[...] if c == n_chunks - 1 else k_ref[c0:c0 + width, :]

    def values(c):
        c0, width = spans[c]
        return vc_ref[...] if c == n_chunks - 1 else v_ref[c0:c0 + width, :]

    def qk(r, c):
        c0, width = spans[c]
        s = _dot_nt(q_ref[r, :], keys(c))
        s_ref[r, c0:c0 + width] = s
        mp = state.get(("m", r.start), jnp.full((tp, LANES), NEG_INF, F32))
        for j in range(width // LANES):
            mp = jnp.maximum(mp, s[:, j * LANES:(j + 1) * LANES])
        state[("m", r.start)] = mp

    def row_max(r):
        m_ref[r, :] = jnp.broadcast_to(jnp.max(state[("m", r.start)], axis=-1, keepdims=True), (tp, LANES))

    def exp_chunk(r, c, after=None):
        m = m_ref[r, :]
        if after is not None:
            bits = lax.bitcast_convert_type(after[:, :LANES], jnp.uint32)
            zero = lax.shift_right_logical(lax.shift_right_logical(bits, jnp.uint32(16)), jnp.uint32(16))
            m = m + lax.bitcast_convert_type(zero, F32)
        c0, width = spans[c]
        for j in range(c0 // LANES, (c0 + width) // LANES):
            p = jnp.exp2(s_ref[r, j * LANES:(j + 1) * LANES] - m)
            p_ref[r, j * LANES:(j + 1) * LANES] = p.astype(BF16)

    def pv(r, c):
        c0, width = spans[c]
        v = values(c)
        v_and_ones = jnp.concatenate([v, jnp.ones_like(v)], axis=1)
        acc = state.get(("a", r.start), jnp.zeros((tp, 2 * V_DIM), F32))
        state[("a", r.start)] = acc + _dot(p_ref[r, c0:c0 + width], v_and_ones)

    def finish(r):
        acc = state[("a", r.start)]
        o_ref[r, :] = (acc[:, :V_DIM] / acc[:, V_DIM:]).astype(BF16)

    for c in range(n_chunks):
        for r in groups[0]:
            qk(r, c)
    for r in groups[0]:
        row_max(r)
    for c in range(n_chunks):
        for r in groups[1]:
            qk(r, c)
        for r in groups[0]:
            exp_chunk(r, c)
    for r in groups[1]:
        row_max(r)
    for c in range(n_chunks):
        for r in groups[0]:
            pv(r, c)
        for r0, r in zip(groups[0], groups[1]):
            exp_chunk(r, c, after=state[("a", r0.start)])
    for r in groups[0]:
        finish(r)
    for c in range(n_chunks):
        for r in groups[1]:
            pv(r, c)
    for r in groups[1]:
        finish(r)


def _lat_attention(q, k, v, kc, vc):
    nq = DEC_SEQ // LAT_TQ
    return pl.pallas_call(
        _lat_attn_kernel,
        grid=(DEC_BATCH, N_HEADS, nq),
        in_specs=[
            pl.BlockSpec((LAT_TQ, HEAD_PAD), lambda b, h, i: ((b + 1) * nq + i, h)),
            pl.BlockSpec((DEC_SEQ, HEAD_PAD), lambda b, h, i: (b + 1, h)),
            pl.BlockSpec((DEC_SEQ, V_DIM), lambda b, h, i: (b + 1, h)),
            pl.BlockSpec((PAST_LEN, HEAD_PAD), lambda b, h, i: (b, h)),
            pl.BlockSpec((PAST_LEN, V_DIM), lambda b, h, i: (b, h)),
        ],
        out_specs=pl.BlockSpec((LAT_TQ, V_DIM), lambda b, h, i: (b * nq + i, h)),
        out_shape=jax.ShapeDtypeStruct((T_LAT, N_HEADS * V_DIM), BF16),
        scratch_shapes=[pltpu.VMEM((LAT_TQ, DEC_SEQ + PAST_LEN), F32),
                        pltpu.VMEM((LAT_TQ, DEC_SEQ + PAST_LEN), BF16),
                        pltpu.VMEM((LAT_TQ, LANES), F32)],
        compiler_params=pltpu.CompilerParams(
            dimension_semantics=("parallel", "parallel", "parallel")),
        name="lat_attention",
    )(q, k, v, kc, vc)


OPROJ_TM = 1024


def _oproj_kernel(oc_ref, ol_ref, x_ref, mods_ref, wo_ref, g_ref, b_ref, out_ref):
    is_ctx = pl.program_id(0) < T_CTX // OPROJ_TM
    o = jnp.where(is_ctx, oc_ref[...], ol_ref[...])
    gm = mods_ref[0][2:3]
    y = _dot(o, wo_ref[...])
    out_ref[...] = _layer_norm(ALPHA * x_ref[...] + gm * y, g_ref[...], b_ref[...])


def _oproj(o_ctx, o_lat, x, mods, w_o, ln_g, ln_b):
    t = x.shape[0]
    tm = OPROJ_TM
    n_ctx = T_CTX // tm
    return pl.pallas_call(
        _oproj_kernel,
        grid=(t // tm,),
        in_specs=[
            pl.BlockSpec((tm, N_HEADS * V_DIM), lambda i: (jnp.minimum(i, n_ctx - 1), 0)),
            pl.BlockSpec((tm, N_HEADS * V_DIM), lambda i: (jnp.maximum(i - n_ctx, 0), 0)),
            pl.BlockSpec((tm, D_MODEL), lambda i: (i, 0)),
            pl.BlockSpec((1, 6, D_MODEL), lambda i: (i * tm // GROUP_ROWS, 0, 0)),
            pl.BlockSpec((N_HEADS * V_DIM, D_MODEL), lambda i: (0, 0)),
            pl.BlockSpec((1, D_MODEL), lambda i: (0, 0)),
            pl.BlockSpec((1, D_MODEL), lambda i: (0, 0)),
        ],
        out_specs=pl.BlockSpec((tm, D_MODEL), lambda i: (i, 0)),
        out_shape=jax.ShapeDtypeStruct((t, D_MODEL), F32),
        compiler_params=pltpu.CompilerParams(dimension_semantics=("parallel",)),
        name="attn_oproj",
    )(o_ctx, o_lat, x, mods, w_o, ln_g, ln_b)


def _swap16(w):
    q = QK_ROPE // 4
    return jnp.concatenate([w[..., q:2 * q], w[..., :q], w[..., 3 * q:], w[..., 2 * q:3 * q]], axis=-1)


def _mla_params(w_dqkv, q_norm, w_uq, kv_norm, w_ukv):
    w_kr = w_dqkv[:, Q_LORA + KV_LORA:]
    wq = w_uq.reshape(Q_LORA, N_HEADS, QK_NOPE + QK_ROPE)
    wq_r = wq[..., QK_NOPE:]
    wq = jnp.concatenate([wq[..., :QK_NOPE], wq_r, _swap16(wq_r)], axis=-1)
    wkv = w_ukv.reshape(KV_LORA, N_HEADS, QK_NOPE + V_DIM)
    return {
        "w_dq": w_dqkv[:, :Q_LORA].astype(BF16),
        "w_dkv": w_dqkv[:, Q_LORA:Q_LORA + KV_LORA].astype(BF16),
        "w_kr": jnp.concatenate([w_kr, _swap16(w_kr)], axis=-1).astype(BF16),
        "q_norm": q_norm.reshape(1, Q_LORA),
        "kv_norm": kv_norm.reshape(1, KV_LORA),
        "w_uq": wq.reshape(Q_LORA, N_HEADS * HEAD_PAD).astype(BF16),
        "w_ukn": wkv[..., :QK_NOPE].reshape(KV_LORA, N_HEADS * QK_NOPE).astype(BF16),
        "w_uv": wkv[..., QK_NOPE:].reshape(KV_LORA, N_HEADS * V_DIM).astype(BF16),
    }


def _rope_tables():
    nf = QK_ROPE // 4
    t = np.arange(DEC_SEQ)
    row = (t // GRID_W).astype(np.float32)
    col = (t % GRID_W).astype(np.float32)
    inv = (ROPE_THETA ** (-np.arange(nf, dtype=np.float32) / nf)).astype(np.float32)
    ar, ac = row[:, None] * inv, col[:, None] * inv
    pad = np.zeros((DEC_SEQ, QK_ROPE), np.float32)
    cos = np.concatenate([np.cos(ar), np.cos(ar), np.cos(ac), np.cos(ac), pad], axis=-1)
    sin = np.concatenate([-np.sin(ar), np.sin(ar), -np.sin(ac), np.sin(ac), pad], axis=-1)
    cos_id = np.concatenate([np.ones((DEC_SEQ, QK_ROPE), np.float32), pad], axis=-1)
    sin_id = np.zeros((DEC_SEQ, 2 * QK_ROPE), np.float32)
    return (jnp.asarray(np.stack([cos_id, cos]).astype(np.float32)),
            jnp.asarray(np.stack([sin_id, sin]).astype(np.float32)))


def kernel(x_prompt, x_sample, cache_ckv, cache_krope, c, c_ctx, ada_w, ada_b, ln_g, ln_b, conv_w_in, conv_k, conv_w_out, mla_w_dqkv, mla_q_norm, mla_w_uq, mla_kv_norm, mla_w_ukv, mla_w_o, router_w, router_bias, exp_w_gate, exp_w_up, exp_w_down, sh_w_gate, sh_w_up, sh_w_down):
    cvecs =jnp.concatenate([c_ctx[None, :], c, jnp.zeros((SUBLANES - N_GROUPS_ROWS, D_MODEL), F32)], axis=0)
    mods = _adaln(cvecs, ada_w, ada_b)

    def ln(l, k):
        return ln_g[l, k].reshape(1, D_MODEL), ln_b[l, k].reshape(1, D_MODEL)

    def moe_layer(xin, l, split_streams):
        idx, pos, wcol, counts, hp_a, hp_b = _router(xin, mods[l], router_w[l].T,
                                                     router_bias[l].reshape(N_EXPERTS, 1))
        shared = (sh_w_gate[l].astype(BF16), sh_w_up[l].astype(BF16), sh_w_down[l].astype(BF16))
        outs = {}
        for ch in sorted(range(len(MOE_CHUNKS)), key=lambda j: -MOE_CHUNKS[j][1]):
            row0, rows = MOE_CHUNKS[ch]
            n_tiles = _n_tiles(rows)
            cnt = counts[ch, :, 0].astype(jnp.int32)
            start, tile_expert, tile_valid, n_used, plan = _expert_layout(cnt, n_tiles)
            dest = _dest_rows(start, idx[:, row0:row0 + rows], pos[:, row0:row0 + rows])
            xs_a, xs_b = _sc_scatter_rows(hp_a, hp_b, dest, row0, n_tiles * FFN_TM)
            ys_a, ys_b = _expert_ffn(xs_a, xs_b, tile_expert, tile_valid, n_used, plan,
                                     exp_w_gate, exp_w_up, exp_w_down, l)
            first = 0
            for eighths in COMBINE_EIGHTHS:
                piece = rows * eighths // 8
                dest_row = dest[:, first:first + piece].reshape(1, piece * TOP_K)
                g_a, g_b = _sc_gather_rows(ys_a, ys_b, dest_row)
                g_a = g_a.reshape(TOP_K, piece, D_QUARTER)
                g_b = g_b.reshape(TOP_K, piece, D_QUARTER)
                outs = _combine(xin, mods[l], wcol, g_a, g_b, *shared, *ln(l, 1), row0=row0 + first,
                                split_streams=split_streams, prev=outs)
                first += piece
        return (outs["ctx"], outs["lat"]) if split_streams else outs["all"]

    x = _conv_mixer(x_prompt.reshape(T_CTX, D_MODEL), x_sample.reshape(T_LAT, D_MODEL), mods[0],
                    conv_w_in[0].astype(BF16), conv_k[0], conv_w_out[0].astype(BF16), *ln(0, 0))
    x = moe_layer(x, 0, split_streams=False)

    p = _mla_params(mla_w_dqkv[0], mla_q_norm[0], mla_w_uq[0], mla_kv_norm[0], mla_w_ukv[0])
    rope_a, rope_b = _rope_tables()
    q, k, v, ckv, kr = _mla_proj(x, mods[1], p, rope_a, rope_b)
    kr_cache = jnp.concatenate([cache_krope[:, 0].reshape(DEC_BATCH * PAST_LEN, QK_ROPE),
                                jnp.zeros((DEC_BATCH * PAST_LEN, QK_ROPE), F32)], axis=-1)
    kc, vc = _cache_kv(cache_ckv[:, 0].reshape(DEC_BATCH * PAST_LEN, KV_LORA), kr_cache, p)
    o_ctx = _ctx_attention(q, k, v)
    o_lat = _lat_attention(q, k, v, kc, vc)
    x = _oproj(o_ctx, o_lat, x, mods[1], mla_w_o[0].astype(BF16), *ln(1, 0))
    y_ctx, y_lat = moe_layer(x, 1, split_streams=True)

    y_prompt = y_ctx.reshape(BATCH, SEQ, D_MODEL)
    y_sample = y_lat.reshape(DEC_BATCH, DEC_SEQ, D_MODEL)
    state_ckv = ckv.reshape(BATCH, 1, SEQ, KV_LORA)
    state_krope = kr[:, :QK_ROPE].reshape(BATCH, 1, SEQ, QK_ROPE)
    return (y_prompt, y_sample, state_ckv, state_krope)
```
